```python
import math
import jax, jax.numpy as jnp
from jax import lax
import numpy as np

D_MODEL = 1024
BATCH = 8
SEQ = 4096
DEPTH = 2

N_EVEN = (DEPTH + 1) // 2
N_ODD = DEPTH // 2

RET_HEADS = 4
RET_DK = 128
RET_DV = 256
RET_CHUNK = 128
ROPE_BASE = 10000.0
RET_QK = RET_HEADS * RET_DK
RET_V = RET_HEADS * RET_DV
SSM_HEADS = 16
SSM_HEADDIM = 64
SSM_DINNER = SSM_HEADS * SSM_HEADDIM
SSM_STATE = 128
SSM_GROUPS = 2
SSM_CONV = 4
SSM_CHUNK = 128
SSM_XBC = SSM_DINNER + 2 * SSM_GROUPS * SSM_STATE
EVEN_SPLITS = [RET_QK, RET_QK, RET_V, RET_V, SSM_DINNER, SSM_XBC, SSM_HEADS]
EVEN_IN = sum(EVEN_SPLITS)
EVEN_MIX = RET_V + SSM_DINNER
CONF_DIM = D_MODEL // 2
CONF_KERNEL = 31
S5_DIM = D_MODEL // 2
S5_GROUP = 16
S5_GROUPS = S5_DIM // S5_GROUP
S5_STATE = 64
ODD_IN = 2 * CONF_DIM + S5_DIM
ODD_MIX = CONF_DIM + S5_DIM
D_FF = 2816
FFN_CONV = 3
EPS = 1e-6

kernel_name = "hybrid_retention_ssd_conformer_s5_trunk"

F32 = jnp.float32


def rms_norm(x, g, eps=EPS):
    xf = x.astype(F32)
    y = xf * lax.rsqrt(jnp.mean(xf * xf, axis=-1, keepdims=True) + eps)
    return (y * g.astype(F32)).astype(x.dtype)


def layer_norm(x, g, b, eps=EPS):
    xf = x.astype(F32)
    mu = jnp.mean(xf, axis=-1, keepdims=True)
    xc = xf - mu
    var = jnp.mean(xc * xc, axis=-1, keepdims=True)
    return (xc * lax.rsqrt(var + eps) * g.astype(F32) + b.astype(F32)).astype(x.dtype)


def causal_dwconv(x, w, b):
    k = w.shape[0]
    y = lax.conv_general_dilated(
        x, w[:, None, :].astype(x.dtype), window_strides=(1,),
        padding=((k - 1, 0),), dimension_numbers=("NWC", "WIO", "NWC"),
        feature_group_count=x.shape[-1])
    return y + b.astype(x.dtype)


def split_cols(a, sizes):
    return jnp.split(a, np.cumsum(sizes)[:-1].tolist(), axis=-1)


def rotary(x, pos):
    d = x.shape[-1]
    inv = ROPE_BASE ** (-jnp.arange(0, d, 2, dtype=F32) / d)
    ang = pos.astype(F32)[:, None] * inv[None, :]
    cos = jnp.cos(ang)[None, :, None, :]
    sin = jnp.sin(ang)[None, :, None, :]
    x1, x2 = x[..., : d // 2], x[..., d // 2:]
    return jnp.concatenate([x1 * cos - x2 * sin, x1 * sin + x2 * cos], axis=-1)


def retention_chunkwise(q, k, v):
    b, l, h, dk = q.shape
    dv = v.shape[-1]
    c = RET_CHUNK
    nc = l // c
    log_g = jnp.log1p(-(2.0 ** (-5.0 - jnp.arange(h, dtype=F32))))
    idx = jnp.arange(c, dtype=F32)
    diff = idx[:, None] - idx[None, :]
    intra = jnp.where(diff[None] >= 0,
                      jnp.exp(jnp.maximum(diff, 0.0)[None] * log_g[:, None, None]), 0.0)
    q = q.reshape(b, nc, c, h, dk)
    k = k.reshape(b, nc, c, h, dk) * (dk ** -0.5)
    v = v.reshape(b, nc, c, h, dv)
    s = jnp.einsum("bcihd,bcjhd->bchij", q, k) * intra
    inner = jnp.einsum("bchij,bcjhe->bcihe", s, v)
    zeta = jnp.exp((c - 1 - idx)[:, None] * log_g[None, :])
    kv = jnp.einsum("bcjhd,bcjhe->bchde", k * zeta[None, None, :, :, None], v)
    chunk_decay = jnp.exp(c * log_g)[None, :, None, None]

    def step(state, kv_c):
        return state * chunk_decay + kv_c, state

    _, prev = lax.scan(step, jnp.zeros_like(kv[:, 0]), jnp.moveaxis(kv, 1, 0))
    prev = jnp.moveaxis(prev, 0, 1)
    xi = jnp.exp((idx + 1)[:, None] * log_g[None, :])
    cross = jnp.einsum("bcihd,bchde->bcihe", q, prev) * xi[None, None, :, :, None]
    return (inner + cross).reshape(b, l, h, dv)


def head_group_norm(y, eps=EPS):
    mu = jnp.mean(y, axis=-1, keepdims=True)
    yc = y - mu
    return yc * lax.rsqrt(jnp.mean(yc * yc, axis=-1, keepdims=True) + eps)


def ssd_chunked(xh, dt, a_neg, bm, cm):
    b, l, h, p = xh.shape
    g, n = bm.shape[-2], bm.shape[-1]
    hg = h // g
    c = SSM_CHUNK
    nc = l // c
    X = (xh * dt[..., None]).reshape(b, nc, c, g, hg, p)
    acs = jnp.cumsum((dt * a_neg).reshape(b, nc, c, g, hg), axis=2)
    Bc = bm.reshape(b, nc, c, g, n)
    Cc = cm.reshape(b, nc, c, g, n)
    mask = (jnp.arange(c)[:, None] >= jnp.arange(c)[None, :])[:, :, None, None]
    seg = acs[:, :, :, None] - acs[:, :, None, :]
    lmat = jnp.exp(jnp.where(mask, seg, -jnp.inf))
    cb = jnp.einsum("bcign,bcjgn->bcijg", Cc, Bc)
    y_diag = jnp.einsum("bcijgh,bcjghp->bcighp", cb[..., None] * lmat, X)
    decay = jnp.exp(acs[:, :, -1:] - acs)
    states = jnp.einsum("bcjgn,bcjghp->bcghpn", Bc, X * decay[..., None])
    chunk_decay = jnp.exp(acs[:, :, -1])

    def step(state, inp):
        st, cd = inp
        return state * cd[..., None, None] + st, state

    _, prev = lax.scan(step, jnp.zeros_like(states[:, 0]),
                       (jnp.moveaxis(states, 1, 0), jnp.moveaxis(chunk_decay, 1, 0)))
    prev = jnp.moveaxis(prev, 0, 1)
    y_off = jnp.einsum("bcign,bcghpn->bcighp", Cc, prev) * jnp.exp(acs)[..., None]
    return (y_diag + y_off).reshape(b, l, h, p)


def even_mixer(h, w_in, conv_w, conv_b, dt_bias, a_log, d_skip, ssm_norm_w, w_out):
    b, l, _ = h.shape
    proj = h @ w_in
    q, k, v, g, z, xbc, dtr = split_cols(proj, EVEN_SPLITS)
    pos = jnp.arange(l)
    q = rotary(q.reshape(b, l, RET_HEADS, RET_DK).astype(F32), pos)
    k = rotary(k.reshape(b, l, RET_HEADS, RET_DK).astype(F32), pos)
    v = v.reshape(b, l, RET_HEADS, RET_DV).astype(F32)
    r = head_group_norm(retention_chunkwise(q, k, v)).reshape(b, l, RET_V)
    y_ret = jax.nn.silu(g.astype(F32)) * r
    xbc = jax.nn.silu(causal_dwconv(xbc, conv_w, conv_b))
    xs, bm, cm = split_cols(xbc, [SSM_DINNER, SSM_GROUPS * SSM_STATE, SSM_GROUPS * SSM_STATE])
    xs = xs.reshape(b, l, SSM_HEADS, SSM_HEADDIM).astype(F32)
    dt = jax.nn.softplus(dtr.astype(F32) + dt_bias.astype(F32))
    a_neg = -jnp.exp(a_log.astype(F32))
    y = ssd_chunked(xs, dt, a_neg,
                    bm.reshape(b, l, SSM_GROUPS, SSM_STATE).astype(F32),
                    cm.reshape(b, l, SSM_GROUPS, SSM_STATE).astype(F32))
    y = y + d_skip.astype(F32)[:, None] * xs
    y = y.reshape(b, l, SSM_DINNER) * jax.nn.silu(z.astype(F32))
    y = y.reshape(b, l, SSM_GROUPS, SSM_DINNER // SSM_GROUPS)
    y = y * lax.rsqrt(jnp.mean(y * y, axis=-1, keepdims=True) + EPS)
    y_ssm = y.reshape(b, l, SSM_DINNER) * ssm_norm_w.astype(F32)
    mix = jnp.concatenate([y_ret, y_ssm], axis=-1).astype(h.dtype)
    return mix @ w_out


def s5_ssm(u, a_re, a_im, b_re, b_im, c_re, c_im, d_skip, log_step):
    bsz, l, _ = u.shape
    uf = u.astype(F32)
    ug = jnp.moveaxis(uf.reshape(bsz, l, S5_GROUPS, S5_GROUP), 1, 0)
    step = jnp.exp(log_step.astype(F32))[:, None]
    lr, li = a_re.astype(F32), a_im.astype(F32)
    mag = jnp.exp(lr * step)
    ab_re = mag * jnp.cos(li * step)
    ab_im = mag * jnp.sin(li * step)
    den = lr * lr + li * li
    f_re = ((ab_re - 1.0) * lr + ab_im * li) / den
    f_im = (ab_im * lr - (ab_re - 1.0) * li) / den
    br, bi = b_re.astype(F32), b_im.astype(F32)
    bb_re = f_re[..., None] * br - f_im[..., None] * bi
    bb_im = f_re[..., None] * bi + f_im[..., None] * br
    bu_re = jnp.einsum("lbgc,gnc->lbgn", ug, bb_re)
    bu_im = jnp.einsum("lbgc,gnc->lbgn", ug, bb_im)
    a_re_seq = jnp.broadcast_to(ab_re[None, None], (l, 1) + ab_re.shape)
    a_im_seq = jnp.broadcast_to(ab_im[None, None], (l, 1) + ab_im.shape)

    def combine(e1, e2):
        a1r, a1i, b1r, b1i = e1
        a2r, a2i, b2r, b2i = e2
        return (a2r * a1r - a2i * a1i,
                a2r * a1i + a2i * a1r,
                a2r * b1r - a2i * b1i + b2r,
                a2r * b1i + a2i * b1r + b2i)

    _, _, xr, xi = lax.associative_scan(combine, (a_re_seq, a_im_seq, bu_re, bu_im), axis=0)
    y = (jnp.einsum("lbgn,gcn->lbgc", xr, c_re.astype(F32))
         - jnp.einsum("lbgn,gcn->lbgc", xi, c_im.astype(F32)))
    y = jnp.moveaxis(y, 0, 1).reshape(bsz, l, S5_DIM)
    return y + d_skip.astype(F32) * uf


def odd_mixer(h, w_in, conf_dw_w, conf_dw_b, conf_ln_g, conf_ln_b, s5_a_re, s5_a_im,
              s5_b_re, s5_b_im, s5_c_re, s5_c_im, s5_d, s5_log_step, s5_glu_w, w_out):
    proj = h @ w_in
    ca, cg, u = split_cols(proj, [CONF_DIM, CONF_DIM, S5_DIM])
    c = ca * jax.nn.sigmoid(cg)
    c = causal_dwconv(c, conf_dw_w, conf_dw_b)
    c = jax.nn.silu(layer_norm(c, conf_ln_g, conf_ln_b)).astype(F32)
    s = jax.nn.gelu(s5_ssm(u, s5_a_re, s5_a_im, s5_b_re, s5_b_im, s5_c_re, s5_c_im, s5_d, s5_log_step))
    s = s * jax.nn.sigmoid(s @ s5_glu_w.astype(F32))
    mix = jnp.concatenate([c, s], axis=-1).astype(h.dtype)
    return mix @ w_out


def conv_ffn(h, w_up, dw_w, dw_b, w_down):
    a = causal_dwconv(h @ w_up, dw_w, dw_b)
    gate, up = jnp.split(a, 2, axis=-1)
    return (jax.nn.silu(gate) * up) @ w_down


def _fwd_setup_inputs(seed: int = 0) -> dict:
    key = jax.random.key(seed)
    ks = iter(jax.random.split(key, 48))

    def nrm(shape, scale):
        return jax.random.normal(next(ks), shape, F32) * scale

    def gain(shape):
        return 1.0 + 0.02 * jax.random.normal(next(ks), shape, F32)

    x = jax.random.normal(next(ks), (BATCH, SEQ, D_MODEL), F32)
    mix_norm = gain((DEPTH, D_MODEL))
    e_w_in = nrm((N_EVEN, D_MODEL, EVEN_IN), D_MODEL ** -0.5)
    e_conv_w = nrm((N_EVEN, SSM_CONV, SSM_XBC), SSM_CONV ** -0.5)
    e_conv_b = nrm((N_EVEN, SSM_XBC), 0.02)
    dt0 = jnp.exp(jax.random.uniform(next(ks), (N_EVEN, SSM_HEADS), F32,
                                     math.log(1e-3), math.log(1e-1)))
    e_dt_bias = dt0 + jnp.log(-jnp.expm1(-dt0))
    e_a_log = jnp.log(jax.random.uniform(next(ks), (N_EVEN, SSM_HEADS), F32, 1.0, 16.0))
    e_d = gain((N_EVEN, SSM_HEADS))
    e_ssm_norm = gain((N_EVEN, SSM_DINNER))
    e_w_out = nrm((N_EVEN, EVEN_MIX, D_MODEL), EVEN_MIX ** -0.5)
    o_w_in = nrm((N_ODD, D_MODEL, ODD_IN), D_MODEL ** -0.5)
    o_dw_w = nrm((N_ODD, CONF_KERNEL, CONF_DIM), CONF_KERNEL ** -0.5)
    o_dw_b = nrm((N_ODD, CONF_DIM), 0.02)
    o_ln_g = gain((N_ODD, CONF_DIM))
    o_ln_b = nrm((N_ODD, CONF_DIM), 0.02)
    o_a_re = -0.5 + nrm((N_ODD, S5_GROUPS, S5_STATE), 0.01)
    o_a_im = (math.pi * jnp.arange(S5_STATE, dtype=F32))[None, None, :] + nrm((N_ODD, S5_GROUPS, S5_STATE), 0.01)
    o_b_re = nrm((N_ODD, S5_GROUPS, S5_STATE, S5_GROUP), S5_GROUP ** -0.5)
    o_b_im = nrm((N_ODD, S5_GROUPS, S5_STATE, S5_GROUP), S5_GROUP ** -0.5)
    o_c_re = nrm((N_ODD, S5_GROUPS, S5_GROUP, S5_STATE), S5_STATE ** -0.5)
    o_c_im = nrm((N_ODD, S5_GROUPS, S5_GROUP, S5_STATE), S5_STATE ** -0.5)
    o_d = nrm((N_ODD, S5_DIM), 1.0)
    o_log_step = jax.random.uniform(next(ks), (N_ODD, S5_GROUPS), F32, math.log(1e-3), math.log(1e-1))
    o_glu_w = nrm((N_ODD, S5_DIM, S5_DIM), S5_DIM ** -0.5)
    o_w_out = nrm((N_ODD, ODD_MIX, D_MODEL), ODD_MIX ** -0.5)
    ffn_norm = gain((DEPTH, D_MODEL))
    ffn_w_up = nrm((DEPTH, D_MODEL, 2 * D_FF), D_MODEL ** -0.5)
    ffn_dw_w = nrm((DEPTH, FFN_CONV, 2 * D_FF), FFN_CONV ** -0.5)
    ffn_dw_b = nrm((DEPTH, 2 * D_FF), 0.02)
    ffn_w_down = nrm((DEPTH, D_FF, D_MODEL), D_FF ** -0.5)
    final_norm = gain((D_MODEL,))
    return {
        "x": x, "mix_norm": mix_norm,
        "e_w_in": e_w_in, "e_conv_w": e_conv_w, "e_conv_b": e_conv_b,
        "e_dt_bias": e_dt_bias, "e_a_log": e_a_log, "e_d": e_d,
        "e_ssm_norm": e_ssm_norm, "e_w_out": e_w_out,
        "o_w_in": o_w_in, "o_dw_w": o_dw_w, "o_dw_b": o_dw_b,
        "o_ln_g": o_ln_g, "o_ln_b": o_ln_b, "o_a_re": o_a_re, "o_a_im": o_a_im,
        "o_b_re": o_b_re, "o_b_im": o_b_im, "o_c_re": o_c_re, "o_c_im": o_c_im,
        "o_d": o_d, "o_log_step": o_log_step, "o_glu_w": o_glu_w, "o_w_out": o_w_out,
        "ffn_norm": ffn_norm, "ffn_w_up": ffn_w_up, "ffn_dw_w": ffn_dw_w,
        "ffn_dw_b": ffn_dw_b, "ffn_w_down": ffn_w_down, "final_norm": final_norm,
    }


def _fwd_reference(x, mix_norm, e_w_in, e_conv_w, e_conv_b, e_dt_bias, e_a_log, e_d,
              e_ssm_norm, e_w_out, o_w_in, o_dw_w, o_dw_b, o_ln_g, o_ln_b, o_a_re,
              o_a_im, o_b_re, o_b_im, o_c_re, o_c_im, o_d, o_log_step, o_glu_w,
              o_w_out, ffn_norm, ffn_w_up, ffn_dw_w, ffn_dw_b, ffn_w_down, final_norm):
    for i in range(DEPTH):
        j = i // 2
        hn = rms_norm(x, mix_norm[i])
        if i % 2 == 0:
            m = even_mixer(hn, e_w_in[j], e_conv_w[j], e_conv_b[j], e_dt_bias[j],
                           e_a_log[j], e_d[j], e_ssm_norm[j], e_w_out[j])
        else:
            m = odd_mixer(hn, o_w_in[j], o_dw_w[j], o_dw_b[j], o_ln_g[j], o_ln_b[j],
                          o_a_re[j], o_a_im[j], o_b_re[j], o_b_im[j], o_c_re[j],
                          o_c_im[j], o_d[j], o_log_step[j], o_glu_w[j], o_w_out[j])
        x = x + m.astype(x.dtype)
        f = conv_ffn(rms_norm(x, ffn_norm[i]), ffn_w_up[i], ffn_dw_w[i], ffn_dw_b[i], ffn_w_down[i])
        x = x + f.astype(x.dtype)
    return rms_norm(x, final_norm)


import jax as _jax
import jax.numpy as _jnp

TWIN_FORMAT = 'train_step'
FWD_PARAMS = ['x', 'mix_norm', 'e_w_in', 'e_conv_w', 'e_conv_b', 'e_dt_bias', 'e_a_log', 'e_d', 'e_ssm_norm', 'e_w_out', 'o_w_in', 'o_dw_w', 'o_dw_b', 'o_ln_g', 'o_ln_b', 'o_a_re', 'o_a_im', 'o_b_re', 'o_b_im', 'o_c_re', 'o_c_im', 'o_d', 'o_log_step', 'o_glu_w', 'o_w_out', 'ffn_norm', 'ffn_w_up', 'ffn_dw_w', 'ffn_dw_b', 'ffn_w_down', 'final_norm']
TWIN_WEIGHTS = ['mix_norm', 'e_w_in', 'e_conv_w', 'e_conv_b', 'e_dt_bias', 'e_a_log', 'e_d', 'e_ssm_norm', 'e_w_out', 'o_w_in', 'o_dw_w', 'o_dw_b', 'o_ln_g', 'o_ln_b', 'o_a_re', 'o_a_im', 'o_b_re', 'o_b_im', 'o_c_re', 'o_c_im', 'o_d', 'o_log_step', 'o_glu_w', 'o_w_out', 'ffn_norm', 'ffn_w_up', 'ffn_dw_w', 'ffn_dw_b', 'ffn_w_down', 'final_norm']
TWIN_DIFF_INPUT = 'x'
TWIN_INPUTS = ['x', 'mix_norm', 'e_w_in', 'e_conv_w', 'e_conv_b', 'e_dt_bias', 'e_a_log', 'e_d', 'e_ssm_norm', 'e_w_out', 'o_w_in', 'o_dw_w', 'o_dw_b', 'o_ln_g', 'o_ln_b', 'o_a_re', 'o_a_im', 'o_b_re', 'o_b_im', 'o_c_re', 'o_c_im', 'o_d', 'o_log_step', 'o_glu_w', 'o_w_out', 'ffn_norm', 'ffn_w_up', 'ffn_dw_w', 'ffn_dw_b', 'ffn_w_down', 'final_norm', 'loss_target', 'm_mix_norm', 'm_e_w_in', 'm_e_conv_w', 'm_e_conv_b', 'm_e_dt_bias', 'm_e_a_log', 'm_e_d', 'm_e_ssm_norm', 'm_e_w_out', 'm_o_w_in', 'm_o_dw_w', 'm_o_dw_b', 'm_o_ln_g', 'm_o_ln_b', 'm_o_a_re', 'm_o_a_im', 'm_o_b_re', 'm_o_b_im', 'm_o_c_re', 'm_o_c_im', 'm_o_d', 'm_o_log_step', 'm_o_glu_w', 'm_o_w_out', 'm_ffn_norm', 'm_ffn_w_up', 'm_ffn_dw_w', 'm_ffn_dw_b', 'm_ffn_w_down', 'm_final_norm', 'v_mix_norm', 'v_e_w_in', 'v_e_conv_w', 'v_e_conv_b', 'v_e_dt_bias', 'v_e_a_log', 'v_e_d', 'v_e_ssm_norm', 'v_e_w_out', 'v_o_w_in', 'v_o_dw_w', 'v_o_dw_b', 'v_o_ln_g', 'v_o_ln_b', 'v_o_a_re', 'v_o_a_im', 'v_o_b_re', 'v_o_b_im', 'v_o_c_re', 'v_o_c_im', 'v_o_d', 'v_o_log_step', 'v_o_glu_w', 'v_o_w_out', 'v_ffn_norm', 'v_ffn_w_up', 'v_ffn_dw_w', 'v_ffn_dw_b', 'v_ffn_w_down', 'v_final_norm']
TWIN_OUTPUTS = ['loss', 'grad_x', 'grad_mix_norm', 'grad_e_w_in', 'grad_e_conv_w', 'grad_e_conv_b', 'grad_e_dt_bias', 'grad_e_a_log', 'grad_e_d', 'grad_e_ssm_norm', 'grad_e_w_out', 'grad_o_w_in', 'grad_o_dw_w', 'grad_o_dw_b', 'grad_o_ln_g', 'grad_o_ln_b', 'grad_o_a_re', 'grad_o_a_im', 'grad_o_b_re', 'grad_o_b_im', 'grad_o_c_re', 'grad_o_c_im', 'grad_o_d', 'grad_o_log_step', 'grad_o_glu_w', 'grad_o_w_out', 'grad_ffn_norm', 'grad_ffn_w_up', 'grad_ffn_dw_w', 'grad_ffn_dw_b', 'grad_ffn_w_down', 'grad_final_norm', 'delta_mix_norm', 'delta_e_w_in', 'delta_e_conv_w', 'delta_e_conv_b', 'delta_e_dt_bias', 'delta_e_a_log', 'delta_e_d', 'delta_e_ssm_norm', 'delta_e_w_out', 'delta_o_w_in', 'delta_o_dw_w', 'delta_o_dw_b', 'delta_o_ln_g', 'delta_o_ln_b', 'delta_o_a_re', 'delta_o_a_im', 'delta_o_b_re', 'delta_o_b_im', 'delta_o_c_re', 'delta_o_c_im', 'delta_o_d', 'delta_o_log_step', 'delta_o_glu_w', 'delta_o_w_out', 'delta_ffn_norm', 'delta_ffn_w_up', 'delta_ffn_dw_w', 'delta_ffn_dw_b', 'delta_ffn_w_down', 'delta_final_norm', 'new_m_mix_norm', 'new_m_e_w_in', 'new_m_e_conv_w', 'new_m_e_conv_b', 'new_m_e_dt_bias', 'new_m_e_a_log', 'new_m_e_d', 'new_m_e_ssm_norm', 'new_m_e_w_out', 'new_m_o_w_in', 'new_m_o_dw_w', 'new_m_o_dw_b', 'new_m_o_ln_g', 'new_m_o_ln_b', 'new_m_o_a_re', 'new_m_o_a_im', 'new_m_o_b_re', 'new_m_o_b_im', 'new_m_o_c_re', 'new_m_o_c_im', 'new_m_o_d', 'new_m_o_log_step', 'new_m_o_glu_w', 'new_m_o_w_out', 'new_m_ffn_norm', 'new_m_ffn_w_up', 'new_m_ffn_dw_w', 'new_m_ffn_dw_b', 'new_m_ffn_w_down', 'new_m_final_norm', 'new_v_mix_norm', 'new_v_e_w_in', 'new_v_e_conv_w', 'new_v_e_conv_b', 'new_v_e_dt_bias', 'new_v_e_a_log', 'new_v_e_d', 'new_v_e_ssm_norm', 'new_v_e_w_out', 'new_v_o_w_in', 'new_v_o_dw_w', 'new_v_o_dw_b', 'new_v_o_ln_g', 'new_v_o_ln_b', 'new_v_o_a_re', 'new_v_o_a_im', 'new_v_o_b_re', 'new_v_o_b_im', 'new_v_o_c_re', 'new_v_o_c_im', 'new_v_o_d', 'new_v_o_log_step', 'new_v_o_glu_w', 'new_v_o_w_out', 'new_v_ffn_norm', 'new_v_ffn_w_up', 'new_v_ffn_dw_w', 'new_v_ffn_dw_b', 'new_v_ffn_w_down', 'new_v_final_norm']
TWIN_LEAF_KINDS = {'loss': 'loss', 'grad_x': 'grad_x', 'grad_mix_norm': 'grad_w', 'grad_e_w_in': 'grad_w', 'grad_e_conv_w': 'grad_w', 'grad_e_conv_b': 'grad_w', 'grad_e_dt_bias': 'grad_w', 'grad_e_a_log': 'grad_w', 'grad_e_d': 'grad_w', 'grad_e_ssm_norm': 'grad_w', 'grad_e_w_out': 'grad_w', 'grad_o_w_in': 'grad_w', 'grad_o_dw_w': 'grad_w', 'grad_o_dw_b': 'grad_w', 'grad_o_ln_g': 'grad_w', 'grad_o_ln_b': 'grad_w', 'grad_o_a_re': 'grad_w', 'grad_o_a_im': 'grad_w', 'grad_o_b_re': 'grad_w', 'grad_o_b_im': 'grad_w', 'grad_o_c_re': 'grad_w', 'grad_o_c_im': 'grad_w', 'grad_o_d': 'grad_w', 'grad_o_log_step': 'grad_w', 'grad_o_glu_w': 'grad_w', 'grad_o_w_out': 'grad_w', 'grad_ffn_norm': 'grad_w', 'grad_ffn_w_up': 'grad_w', 'grad_ffn_dw_w': 'grad_w', 'grad_ffn_dw_b': 'grad_w', 'grad_ffn_w_down': 'grad_w', 'grad_final_norm': 'grad_w', 'delta_mix_norm': 'delta_w', 'delta_e_w_in': 'delta_w', 'delta_e_conv_w': 'delta_w', 'delta_e_conv_b': 'delta_w', 'delta_e_dt_bias': 'delta_w', 'delta_e_a_log': 'delta_w', 'delta_e_d': 'delta_w', 'delta_e_ssm_norm': 'delta_w', 'delta_e_w_out': 'delta_w', 'delta_o_w_in': 'delta_w', 'delta_o_dw_w': 'delta_w', 'delta_o_dw_b': 'delta_w', 'delta_o_ln_g': 'delta_w', 'delta_o_ln_b': 'delta_w', 'delta_o_a_re': 'delta_w', 'delta_o_a_im': 'delta_w', 'delta_o_b_re': 'delta_w', 'delta_o_b_im': 'delta_w', 'delta_o_c_re': 'delta_w', 'delta_o_c_im': 'delta_w', 'delta_o_d': 'delta_w', 'delta_o_log_step': 'delta_w', 'delta_o_glu_w': 'delta_w', 'delta_o_w_out': 'delta_w', 'delta_ffn_norm': 'delta_w', 'delta_ffn_w_up': 'delta_w', 'delta_ffn_dw_w': 'delta_w', 'delta_ffn_dw_b': 'delta_w', 'delta_ffn_w_down': 'delta_w', 'delta_final_norm': 'delta_w', 'new_m_mix_norm': 'new_m', 'new_m_e_w_in': 'new_m', 'new_m_e_conv_w': 'new_m', 'new_m_e_conv_b': 'new_m', 'new_m_e_dt_bias': 'new_m', 'new_m_e_a_log': 'new_m', 'new_m_e_d': 'new_m', 'new_m_e_ssm_norm': 'new_m', 'new_m_e_w_out': 'new_m', 'new_m_o_w_in': 'new_m', 'new_m_o_dw_w': 'new_m', 'new_m_o_dw_b': 'new_m', 'new_m_o_ln_g': 'new_m', 'new_m_o_ln_b': 'new_m', 'new_m_o_a_re': 'new_m', 'new_m_o_a_im': 'new_m', 'new_m_o_b_re': 'new_m', 'new_m_o_b_im': 'new_m', 'new_m_o_c_re': 'new_m', 'new_m_o_c_im': 'new_m', 'new_m_o_d': 'new_m', 'new_m_o_log_step': 'new_m', 'new_m_o_glu_w': 'new_m', 'new_m_o_w_out': 'new_m', 'new_m_ffn_norm': 'new_m', 'new_m_ffn_w_up': 'new_m', 'new_m_ffn_dw_w': 'new_m', 'new_m_ffn_dw_b': 'new_m', 'new_m_ffn_w_down': 'new_m', 'new_m_final_norm': 'new_m', 'new_v_mix_norm': 'new_v', 'new_v_e_w_in': 'new_v', 'new_v_e_conv_w': 'new_v', 'new_v_e_conv_b': 'new_v', 'new_v_e_dt_bias': 'new_v', 'new_v_e_a_log': 'new_v', 'new_v_e_d': 'new_v', 'new_v_e_ssm_norm': 'new_v', 'new_v_e_w_out': 'new_v', 'new_v_o_w_in': 'new_v', 'new_v_o_dw_w': 'new_v', 'new_v_o_dw_b': 'new_v', 'new_v_o_ln_g': 'new_v', 'new_v_o_ln_b': 'new_v', 'new_v_o_a_re': 'new_v', 'new_v_o_a_im': 'new_v', 'new_v_o_b_re': 'new_v', 'new_v_o_b_im': 'new_v', 'new_v_o_c_re': 'new_v', 'new_v_o_c_im': 'new_v', 'new_v_o_d': 'new_v', 'new_v_o_log_step': 'new_v', 'new_v_o_glu_w': 'new_v', 'new_v_o_w_out': 'new_v', 'new_v_ffn_norm': 'new_v', 'new_v_ffn_w_up': 'new_v', 'new_v_ffn_dw_w': 'new_v', 'new_v_ffn_dw_b': 'new_v', 'new_v_ffn_w_down': 'new_v', 'new_v_final_norm': 'new_v'}


def _forward(args):
    return _fwd_reference(*[args[k] for k in FWD_PARAMS])


def _output_shape():
    def fwd():
        inp = _fwd_setup_inputs(0)
        return _fwd_reference(*[inp[k] for k in FWD_PARAMS])
    out = _jax.eval_shape(fwd)
    return out.shape, out.dtype

N_MICROBATCH = 1
ADAM_LR = 0.001
ADAM_B1 = 0.9
ADAM_B2 = 0.999
ADAM_EPS = 1e-08
ADAM_WD = 0.01
ADAM_STEP = 10
PER_EXAMPLE_BATCH_AXIS = {'x': 0, 'loss_target': 0}
SHARED_INPUTS = []
_WEIGHT_DTYPES = {'mix_norm': _jnp.float32, 'e_w_in': _jnp.float32, 'e_conv_w': _jnp.float32, 'e_conv_b': _jnp.float32, 'e_dt_bias': _jnp.float32, 'e_a_log': _jnp.float32, 'e_d': _jnp.float32, 'e_ssm_norm': _jnp.float32, 'e_w_out': _jnp.float32, 'o_w_in': _jnp.float32, 'o_dw_w': _jnp.float32, 'o_dw_b': _jnp.float32, 'o_ln_g': _jnp.float32, 'o_ln_b': _jnp.float32, 'o_a_re': _jnp.float32, 'o_a_im': _jnp.float32, 'o_b_re': _jnp.float32, 'o_b_im': _jnp.float32, 'o_c_re': _jnp.float32, 'o_c_im': _jnp.float32, 'o_d': _jnp.float32, 'o_log_step': _jnp.float32, 'o_glu_w': _jnp.float32, 'o_w_out': _jnp.float32, 'ffn_norm': _jnp.float32, 'ffn_w_up': _jnp.float32, 'ffn_dw_w': _jnp.float32, 'ffn_dw_b': _jnp.float32, 'ffn_w_down': _jnp.float32, 'final_norm': _jnp.float32}
MOMENT_SCALE = {'mix_norm': 1.670146e-01, 'e_w_in': 9.559758e-02, 'e_conv_w': 1.037254e-01, 'e_conv_b': 1.311840e-01, 'e_dt_bias': 3.832969e-01, 'e_a_log': 2.401668e-01, 'e_d': 1.162865e+00, 'e_ssm_norm': 1.180095e-01, 'e_w_out': 1.338025e-01, 'o_w_in': 5.512548e-02, 'o_dw_w': 7.763298e-02, 'o_dw_b': 1.543962e-01, 'o_ln_g': 9.306416e-02, 'o_ln_b': 8.471061e-02, 'o_a_re': 4.132953e-03, 'o_a_im': 4.336925e-03, 'o_b_re': 1.992381e-03, 'o_b_im': 2.024929e-03, 'o_c_re': 4.012643e-03, 'o_c_im': 3.929808e-03, 'o_d': 4.757453e-02, 'o_log_step': 3.425611e+00, 'o_glu_w': 1.283417e-02, 'o_w_out': 6.321272e-02, 'ffn_norm': 1.131306e-01, 'ffn_w_up': 4.602053e-02, 'ffn_dw_w': 4.622220e-02, 'ffn_dw_b': 4.490349e-02, 'ffn_w_down': 7.551054e-02, 'final_norm': 3.200744e+01}


def _to_microbatches(a, axis):
    t = _jnp.moveaxis(a, axis, 0)
    t = t.reshape((N_MICROBATCH, t.shape[0] // N_MICROBATCH) + t.shape[1:])
    return _jnp.moveaxis(t, 1, axis + 1)


def setup_inputs(seed: int = 0) -> dict:
    inp = _fwd_setup_inputs(seed)
    key = _jax.random.fold_in(_jax.random.key(seed), 7919)
    shape, _ = _output_shape()
    out = dict(inp)
    out["loss_target"] = _jax.random.normal(_jax.random.fold_in(key, 0), shape, _jnp.float32)
    for i, name in enumerate(TWIN_WEIGHTS):
        w = inp[name].astype(_jnp.float32)
        if MOMENT_SCALE is None:
            s = _jnp.sqrt(_jnp.mean(_jnp.square(w)) + 1e-30)
        else:
            s = MOMENT_SCALE[name]
        km, kv = _jax.random.split(_jax.random.fold_in(key, i + 1))
        out[name] = w
        out["m_" + name] = s * _jax.random.normal(km, w.shape, _jnp.float32)
        out["v_" + name] = (s * s) * _jax.random.uniform(kv, w.shape, _jnp.float32, 0.5, 1.5)
    if N_MICROBATCH > 1:
        for name, axis in PER_EXAMPLE_BATCH_AXIS.items():
            out[name] = _to_microbatches(out[name], axis)
    return {'x': out['x'], 'mix_norm': out['mix_norm'], 'e_w_in': out['e_w_in'], 'e_conv_w': out['e_conv_w'], 'e_conv_b': out['e_conv_b'], 'e_dt_bias': out['e_dt_bias'], 'e_a_log': out['e_a_log'], 'e_d': out['e_d'], 'e_ssm_norm': out['e_ssm_norm'], 'e_w_out': out['e_w_out'], 'o_w_in': out['o_w_in'], 'o_dw_w': out['o_dw_w'], 'o_dw_b': out['o_dw_b'], 'o_ln_g': out['o_ln_g'], 'o_ln_b': out['o_ln_b'], 'o_a_re': out['o_a_re'], 'o_a_im': out['o_a_im'], 'o_b_re': out['o_b_re'], 'o_b_im': out['o_b_im'], 'o_c_re': out['o_c_re'], 'o_c_im': out['o_c_im'], 'o_d': out['o_d'], 'o_log_step': out['o_log_step'], 'o_glu_w': out['o_glu_w'], 'o_w_out': out['o_w_out'], 'ffn_norm': out['ffn_norm'], 'ffn_w_up': out['ffn_w_up'], 'ffn_dw_w': out['ffn_dw_w'], 'ffn_dw_b': out['ffn_dw_b'], 'ffn_w_down': out['ffn_w_down'], 'final_norm': out['final_norm'], 'loss_target': out['loss_target'], 'm_mix_norm': out['m_mix_norm'], 'm_e_w_in': out['m_e_w_in'], 'm_e_conv_w': out['m_e_conv_w'], 'm_e_conv_b': out['m_e_conv_b'], 'm_e_dt_bias': out['m_e_dt_bias'], 'm_e_a_log': out['m_e_a_log'], 'm_e_d': out['m_e_d'], 'm_e_ssm_norm': out['m_e_ssm_norm'], 'm_e_w_out': out['m_e_w_out'], 'm_o_w_in': out['m_o_w_in'], 'm_o_dw_w': out['m_o_dw_w'], 'm_o_dw_b': out['m_o_dw_b'], 'm_o_ln_g': out['m_o_ln_g'], 'm_o_ln_b': out['m_o_ln_b'], 'm_o_a_re': out['m_o_a_re'], 'm_o_a_im': out['m_o_a_im'], 'm_o_b_re': out['m_o_b_re'], 'm_o_b_im': out['m_o_b_im'], 'm_o_c_re': out['m_o_c_re'], 'm_o_c_im': out['m_o_c_im'], 'm_o_d': out['m_o_d'], 'm_o_log_step': out['m_o_log_step'], 'm_o_glu_w': out['m_o_glu_w'], 'm_o_w_out': out['m_o_w_out'], 'm_ffn_norm': out['m_ffn_norm'], 'm_ffn_w_up': out['m_ffn_w_up'], 'm_ffn_dw_w': out['m_ffn_dw_w'], 'm_ffn_dw_b': out['m_ffn_dw_b'], 'm_ffn_w_down': out['m_ffn_w_down'], 'm_final_norm': out['m_final_norm'], 'v_mix_norm': out['v_mix_norm'], 'v_e_w_in': out['v_e_w_in'], 'v_e_conv_w': out['v_e_conv_w'], 'v_e_conv_b': out['v_e_conv_b'], 'v_e_dt_bias': out['v_e_dt_bias'], 'v_e_a_log': out['v_e_a_log'], 'v_e_d': out['v_e_d'], 'v_e_ssm_norm': out['v_e_ssm_norm'], 'v_e_w_out': out['v_e_w_out'], 'v_o_w_in': out['v_o_w_in'], 'v_o_dw_w': out['v_o_dw_w'], 'v_o_dw_b': out['v_o_dw_b'], 'v_o_ln_g': out['v_o_ln_g'], 'v_o_ln_b': out['v_o_ln_b'], 'v_o_a_re': out['v_o_a_re'], 'v_o_a_im': out['v_o_a_im'], 'v_o_b_re': out['v_o_b_re'], 'v_o_b_im': out['v_o_b_im'], 'v_o_c_re': out['v_o_c_re'], 'v_o_c_im': out['v_o_c_im'], 'v_o_d': out['v_o_d'], 'v_o_log_step': out['v_o_log_step'], 'v_o_glu_w': out['v_o_glu_w'], 'v_o_w_out': out['v_o_w_out'], 'v_ffn_norm': out['v_ffn_norm'], 'v_ffn_w_up': out['v_ffn_w_up'], 'v_ffn_dw_w': out['v_ffn_dw_w'], 'v_ffn_dw_b': out['v_ffn_dw_b'], 'v_ffn_w_down': out['v_ffn_w_down'], 'v_final_norm': out['v_final_norm']}


def _loss(weights, diff, rest, loss_target):
    with _jax.named_scope("forward"):
        args = {**rest, TWIN_DIFF_INPUT: diff, **{k: w.astype(_WEIGHT_DTYPES[k]) for k, w in weights.items()}}
        y = _forward(args)
    with _jax.named_scope("loss_head"):
        err = _jnp.square(y.astype(_jnp.float32) - loss_target)
        return 0.5 * _jnp.sum(_jnp.mean(err, axis=-1)) if err.ndim else 0.5 * err


def _adamw(w, g, m, v):
    m = ADAM_B1 * m + (1.0 - ADAM_B1) * g
    v = ADAM_B2 * v + (1.0 - ADAM_B2) * _jnp.square(g)
    m_hat = m / (1.0 - ADAM_B1 ** ADAM_STEP)
    v_hat = v / (1.0 - ADAM_B2 ** ADAM_STEP)
    delta = -ADAM_LR * (m_hat / (_jnp.sqrt(v_hat) + ADAM_EPS) + ADAM_WD * w)
    return delta, m, v


def reference(x, mix_norm, e_w_in, e_conv_w, e_conv_b, e_dt_bias, e_a_log, e_d, e_ssm_norm, e_w_out, o_w_in, o_dw_w, o_dw_b, o_ln_g, o_ln_b, o_a_re, o_a_im, o_b_re, o_b_im, o_c_re, o_c_im, o_d, o_log_step, o_glu_w, o_w_out, ffn_norm, ffn_w_up, ffn_dw_w, ffn_dw_b, ffn_w_down, final_norm, loss_target, m_mix_norm, m_e_w_in, m_e_conv_w, m_e_conv_b, m_e_dt_bias, m_e_a_log, m_e_d, m_e_ssm_norm, m_e_w_out, m_o_w_in, m_o_dw_w, m_o_dw_b, m_o_ln_g, m_o_ln_b, m_o_a_re, m_o_a_im, m_o_b_re, m_o_b_im, m_o_c_re, m_o_c_im, m_o_d, m_o_log_step, m_o_glu_w, m_o_w_out, m_ffn_norm, m_ffn_w_up, m_ffn_dw_w, m_ffn_dw_b, m_ffn_w_down, m_final_norm, v_mix_norm, v_e_w_in, v_e_conv_w, v_e_conv_b, v_e_dt_bias, v_e_a_log, v_e_d, v_e_ssm_norm, v_e_w_out, v_o_w_in, v_o_dw_w, v_o_dw_b, v_o_ln_g, v_o_ln_b, v_o_a_re, v_o_a_im, v_o_b_re, v_o_b_im, v_o_c_re, v_o_c_im, v_o_d, v_o_log_step, v_o_glu_w, v_o_w_out, v_ffn_norm, v_ffn_w_up, v_ffn_dw_w, v_ffn_dw_b, v_ffn_w_down, v_final_norm):
    given = dict(x=x, mix_norm=mix_norm, e_w_in=e_w_in, e_conv_w=e_conv_w, e_conv_b=e_conv_b, e_dt_bias=e_dt_bias, e_a_log=e_a_log, e_d=e_d, e_ssm_norm=e_ssm_norm, e_w_out=e_w_out, o_w_in=o_w_in, o_dw_w=o_dw_w, o_dw_b=o_dw_b, o_ln_g=o_ln_g, o_ln_b=o_ln_b, o_a_re=o_a_re, o_a_im=o_a_im, o_b_re=o_b_re, o_b_im=o_b_im, o_c_re=o_c_re, o_c_im=o_c_im, o_d=o_d, o_log_step=o_log_step, o_glu_w=o_glu_w, o_w_out=o_w_out, ffn_norm=ffn_norm, ffn_w_up=ffn_w_up, ffn_dw_w=ffn_dw_w, ffn_dw_b=ffn_dw_b, ffn_w_down=ffn_w_down, final_norm=final_norm, loss_target=loss_target, m_mix_norm=m_mix_norm, m_e_w_in=m_e_w_in, m_e_conv_w=m_e_conv_w, m_e_conv_b=m_e_conv_b, m_e_dt_bias=m_e_dt_bias, m_e_a_log=m_e_a_log, m_e_d=m_e_d, m_e_ssm_norm=m_e_ssm_norm, m_e_w_out=m_e_w_out, m_o_w_in=m_o_w_in, m_o_dw_w=m_o_dw_w, m_o_dw_b=m_o_dw_b, m_o_ln_g=m_o_ln_g, m_o_ln_b=m_o_ln_b, m_o_a_re=m_o_a_re, m_o_a_im=m_o_a_im, m_o_b_re=m_o_b_re, m_o_b_im=m_o_b_im, m_o_c_re=m_o_c_re, m_o_c_im=m_o_c_im, m_o_d=m_o_d, m_o_log_step=m_o_log_step, m_o_glu_w=m_o_glu_w, m_o_w_out=m_o_w_out, m_ffn_norm=m_ffn_norm, m_ffn_w_up=m_ffn_w_up, m_ffn_dw_w=m_ffn_dw_w, m_ffn_dw_b=m_ffn_dw_b, m_ffn_w_down=m_ffn_w_down, m_final_norm=m_final_norm, v_mix_norm=v_mix_norm, v_e_w_in=v_e_w_in, v_e_conv_w=v_e_conv_w, v_e_conv_b=v_e_conv_b, v_e_dt_bias=v_e_dt_bias, v_e_a_log=v_e_a_log, v_e_d=v_e_d, v_e_ssm_norm=v_e_ssm_norm, v_e_w_out=v_e_w_out, v_o_w_in=v_o_w_in, v_o_dw_w=v_o_dw_w, v_o_dw_b=v_o_dw_b, v_o_ln_g=v_o_ln_g, v_o_ln_b=v_o_ln_b, v_o_a_re=v_o_a_re, v_o_a_im=v_o_a_im, v_o_b_re=v_o_b_re, v_o_b_im=v_o_b_im, v_o_c_re=v_o_c_re, v_o_c_im=v_o_c_im, v_o_d=v_o_d, v_o_log_step=v_o_log_step, v_o_glu_w=v_o_glu_w, v_o_w_out=v_o_w_out, v_ffn_norm=v_ffn_norm, v_ffn_w_up=v_ffn_w_up, v_ffn_dw_w=v_ffn_dw_w, v_ffn_dw_b=v_ffn_dw_b, v_ffn_w_down=v_ffn_w_down, v_final_norm=v_final_norm)
    weights = {n: given[n] for n in TWIN_WEIGHTS}
    shared = {n: given[n] for n in SHARED_INPUTS}
    per_example = {n: given[n] for n in ['x']}
    grad_fn = _jax.value_and_grad(_loss, argnums=(0, 1))

    def one_microbatch(ex, loss_target):
        ex = dict(ex)
        diff = ex.pop(TWIN_DIFF_INPUT)
        return grad_fn(weights, diff, {**shared, **ex}, loss_target)

    if N_MICROBATCH == 1:
        loss, (grad_w, grad_x) = one_microbatch(per_example, given["loss_target"])
    else:
        def body(carry, xs):
            loss_sum, grad_sum = carry
            l_k, (gw_k, gx_k) = one_microbatch(xs[0], xs[1])
            with _jax.named_scope("update"):
                return (loss_sum + l_k, _jax.tree.map(_jnp.add, grad_sum, gw_k)), gx_k

        init = (_jnp.zeros((), _jnp.float32), _jax.tree.map(_jnp.zeros_like, weights))
        (loss, grad_w), grad_x = _jax.lax.scan(body, init, (per_example, given["loss_target"]))
    with _jax.named_scope("update"):
        delta_w, new_m, new_v = {}, {}, {}
        for n in TWIN_WEIGHTS:
            delta_w[n], new_m[n], new_v[n] = _adamw(weights[n], grad_w[n], given["m_" + n], given["v_" + n])
    return (loss, grad_x, *[grad_w[n] for n in TWIN_WEIGHTS], *[delta_w[n] for n in TWIN_WEIGHTS],
            *[new_m[n] for n in TWIN_WEIGHTS], *[new_v[n] for n in TWIN_WEIGHTS])
```

```python
import functools
import math

import jax
import jax.numpy as jnp
from jax import lax
from jax.experimental import pallas as pl
from jax.experimental.pallas import tpu as pltpu

F32, BF16 = jnp.float32, jnp.bfloat16
HIGHEST = lax.Precision.HIGHEST
N_DEV = 8
MESH_AXES = ("x", "y", "c")
VMEM_LIMIT = 48 * 1024 * 1024
LANES = 128
PACK_COLS = 1024

D_MODEL = 1024
EPS = 1e-6
RET_HEADS, RET_DK, RET_DV = 4, 128, 256
SSM_HEADS, SSM_P, SSM_N, SSM_GROUPS = 16, 64, 128, 2
SSM_HG = SSM_HEADS // SSM_GROUPS
S5_GROUPS, S5_GROUP, S5_STATE = 32, 16, 64
S5_COLS = S5_GROUPS * S5_STATE
D_FF = 2816
E_IN, E_IN_PAD = 5648, 5760
ADAM_LR, ADAM_B1, ADAM_B2, ADAM_EPS, ADAM_WD, ADAM_STEP = 0.001, 0.9, 0.999, 1e-08, 0.01, 10

WEIGHTS = ['mix_norm', 'e_w_in', 'e_conv_w', 'e_conv_b', 'e_dt_bias', 'e_a_log', 'e_d', 'e_ssm_norm', 'e_w_out', 'o_w_in', 'o_dw_w', 'o_dw_b', 'o_ln_g', 'o_ln_b', 'o_a_re', 'o_a_im', 'o_b_re', 'o_b_im', 'o_c_re', 'o_c_im', 'o_d', 'o_log_step', 'o_glu_w', 'o_w_out', 'ffn_norm', 'ffn_w_up', 'ffn_dw_w', 'ffn_dw_b', 'ffn_w_down', 'final_norm']
SHARDED = {'e_w_in': 2, 'e_conv_w': 2, 'e_w_out': 1, 'o_w_in': 2, 'o_dw_w': 2, 'o_dw_b': 1, 'o_ln_g': 1, 'o_ln_b': 1,
           'o_d': 1, 'o_glu_w': 1, 'o_w_out': 1, 'ffn_w_up': 2, 'ffn_dw_w': 2, 'ffn_w_down': 1}
MATMUL_WEIGHTS = ['e_w_in', 'e_w_out', 'o_w_in', 'o_glu_w', 'o_w_out', 'ffn_w_up', 'ffn_w_down']
SMALL_SHARDED = [n for n in WEIGHTS if n in SHARDED and n not in MATMUL_WEIGHTS]
SHARDED_ORDER = [n for n in WEIGHTS if n in SHARDED]
REPLICATED = [n for n in WEIGHTS if n not in SHARDED]


def _call(fn, grid, ins, outs, name):
    n_in = len(ins)

    def body(*refs):
        vals = fn(*[r[...] for r in refs[:n_in]])
        first = pl.program_id(0) == 0
        for r, v, o in zip(refs[n_in:], vals, outs):
            if o[4]:
                @pl.when(first)
                def _():
                    r[...] = jnp.zeros_like(r)
                r[...] += v.astype(r.dtype)
            else:
                r[...] = v.astype(r.dtype)

    return pl.pallas_call(
        body, grid=grid,
        in_specs=[pl.BlockSpec(b, m) for _, b, m in ins],
        out_specs=[pl.BlockSpec(o[2], o[3]) for o in outs],
        out_shape=[jax.ShapeDtypeStruct(o[0], o[1]) for o in outs],
        compiler_params=pltpu.CompilerParams(dimension_semantics=("arbitrary",) * len(grid), vmem_limit_bytes=VMEM_LIMIT),
        name=name)(*[a for a, _, _ in ins])


def _rows(fn, n_rows, row_ins, full_ins, row_outs, acc_outs, name, tm=256):
    tm = min(tm, n_rows)
    ins = [(a, (tm, w), (lambda i, c=c: (i, c))) for a, w, c in row_ins]
    ins += [(a, a.shape, (lambda i, n=a.ndim: (0,) * n)) for a in full_ins]
    outs = [((n_rows, w), F32, (tm, w), (lambda i: (i, 0)), False) for w in row_outs]
    outs += [(tuple(s), F32, tuple(s), (lambda i, n=len(s): (0,) * n), True) for s in acc_outs]
    return _call(fn, (n_rows // tm,), ins, outs, name)


def _cols(fn, n_blocks, col_ins, out_leads, name, cb=LANES):
    ins = [(a, a.shape[:-1] + (cb,), (lambda j, n=a.ndim, o=o: (0,) * (n - 1) + (j + o,))) for a, o in col_ins]
    outs = [(tuple(s) + (n_blocks * cb,), F32, tuple(s) + (cb,), (lambda j, n=len(s): (0,) * n + (j,)), False)
            for s in out_leads]
    return _call(fn, (n_blocks,), ins, outs, name)


def _grad_fn(f, n_diff, n_in):
    def g(*a):
        diff, consts, cts = a[:n_diff], a[n_diff:n_in], a[n_in:]
        _, vjp = jax.vjp(lambda *d: f(*d, *consts), *diff)
        return vjp(tuple(cts))
    return g


def _silu(x):
    return x * jax.nn.sigmoid(x)


def _rms(x, g):
    return x * lax.rsqrt(jnp.mean(x * x, axis=-1, keepdims=True) + EPS) * g


@jax.custom_vjp
def _softplus(x):
    return jnp.maximum(x, 0.0) + jnp.log(1.0 + jnp.exp(-jnp.abs(x)))


_softplus.defvjp(lambda x: (_softplus(x), x), lambda x, g: (g * jax.nn.sigmoid(x),))


@jax.custom_vjp
def _swap_halves(x):
    return pltpu.roll(x, 64, 1)


_swap_halves.defvjp(lambda x: (_swap_halves(x), None), lambda _, g: (_swap_halves(g),))


def _shift_rows(x, k, up):
    if k == 0:
        return x
    n = x.shape[0]
    t = lax.broadcasted_iota(jnp.int32, x.shape, 0)
    if up:
        return jnp.where(t < n - k, pltpu.roll(x, n - k, 0), 0.0)
    return jnp.where(t >= k, pltpu.roll(x, k, 0), 0.0)


def _shift_rows_dyn(x, k, up):
    n = x.shape[0]
    t = lax.broadcasted_iota(jnp.int32, x.shape, 0)
    if up:
        return jnp.where(t < n - k, pltpu.roll(x, n - k, 0), 0.0)
    return jnp.where(t >= k, pltpu.roll(x, k, 0), 0.0)


def _tap(w, k):
    hot = (lax.broadcasted_iota(jnp.int32, w.shape, 0) == k).astype(F32)
    return jnp.sum(w * hot, axis=0), hot


UNROLLED_TAPS = 8


@jax.custom_vjp
def _dwconv(x, w, b):
    k_taps = w.shape[0]
    y = b + w[k_taps - 1] * x
    if k_taps <= UNROLLED_TAPS:
        for k in range(k_taps - 1):
            y = y + w[k] * _shift_rows(x, k_taps - 1 - k, False)
        return y
    return lax.fori_loop(0, k_taps - 1, lambda k, y: y + _tap(w, k)[0] * _shift_rows_dyn(x, k_taps - 1 - k, False), y)


def _dwconv_fwd(x, w, b):
    return _dwconv(x, w, b), (x, w)


def _dwconv_bwd(saved, dy):
    x, w = saved
    k_taps = w.shape[0]
    dx = w[k_taps - 1] * dy
    db = jnp.sum(dy, axis=0, keepdims=True)
    if k_taps <= UNROLLED_TAPS:
        dws = []
        for k in range(k_taps - 1):
            s = k_taps - 1 - k
            dx = dx + w[k] * _shift_rows(dy, s, True)
            dws.append(jnp.sum(dy * _shift_rows(x, s, False), axis=0, keepdims=True)[None])
        dws.append(jnp.sum(dy * x, axis=0, keepdims=True)[None])
        return dx, jnp.concatenate(dws, axis=0), db

    def tap(k, carry):
        dx, dw = carry
        wk, hot = _tap(w, k)
        s = k_taps - 1 - k
        dwk = jnp.sum(dy * _shift_rows_dyn(x, s, False), axis=0, keepdims=True)
        return dx + wk * _shift_rows_dyn(dy, s, True), dw + hot * dwk[None]

    _, last = _tap(w, k_taps - 1)
    dx, dw = lax.fori_loop(0, k_taps - 1, tap, (dx, last * jnp.sum(dy * x, axis=0, keepdims=True)[None]))
    return dx, dw, db


_dwconv.defvjp(_dwconv_fwd, _dwconv_bwd)


def _f_rms(x, g):
    return (_rms(x, g),)


def _rot(x, cosf, sins):
    outs = []
    for h in range(RET_HEADS):
        xh = x[:, h * RET_DK:(h + 1) * RET_DK]
        outs.append(xh * cosf + _swap_halves(xh) * sins)
    return jnp.concatenate(outs, axis=1)


def _f_retpre(q, k, cosf, sins):
    return _rot(q, cosf, sins), _rot(k, cosf, sins) * (RET_DK ** -0.5)


def _f_retpost(r, g):
    outs = []
    for h in range(RET_HEADS):
        rh = r[:, h * RET_DV:(h + 1) * RET_DV]
        rc = rh - jnp.mean(rh, axis=-1, keepdims=True)
        outs.append(_silu(g[:, h * RET_DV:(h + 1) * RET_DV]) * (rc * lax.rsqrt(jnp.mean(rc * rc, axis=-1, keepdims=True) + EPS)))
    return (jnp.concatenate(outs, axis=1),)


def _f_ssdconv(xbc, w, b):
    return (_silu(_dwconv(xbc, w, b)),)


def _f_ssdpre(xs, dtr, bias, alog, expand):
    dt = _softplus(dtr + bias)
    return xs * jnp.dot(dt, expand, precision=HIGHEST, preferred_element_type=F32), dt * (-jnp.exp(alog))


def _f_ssdpost(yc, xs, z, dskip, norm_w, expand):
    d_wide = jnp.dot(jnp.broadcast_to(dskip, (yc.shape[0], LANES)), expand, precision=HIGHEST, preferred_element_type=F32)
    y = (yc + d_wide * xs) * _silu(z)
    half = y.shape[1] // SSM_GROUPS
    outs = []
    for g in range(SSM_GROUPS):
        yg = y[:, g * half:(g + 1) * half]
        outs.append(yg * lax.rsqrt(jnp.mean(yg * yg, axis=-1, keepdims=True) + EPS))
    return (jnp.concatenate(outs, axis=1) * norm_w,)


def _f_ffnmid(gin, uin, wg, wu, bg, bu):
    return (_silu(_dwconv(gin, wg, bg)) * _dwconv(uin, wu, bu),)


def _f_confa(ca, cg, w, b):
    return (_dwconv(ca * jax.nn.sigmoid(cg), w, b),)


def _f_confb(c1, g, b):
    mu = jnp.mean(c1, axis=-1, keepdims=True)
    xc = c1 - mu
    return (_silu(xc * lax.rsqrt(jnp.mean(xc * xc, axis=-1, keepdims=True) + EPS) * g + b),)


def _f_s5post(yc, u, dskip, glu_w):
    s = jax.nn.gelu(yc + dskip * u)
    z = jnp.dot(s.astype(BF16), glu_w.astype(BF16), preferred_element_type=F32)
    return (s * jax.nn.sigmoid(z),)


def _f_s5par(lr, li, ls, br, bi):
    step = jnp.exp(ls)
    mag = jnp.exp(lr * step)
    ab_re = mag * jnp.cos(li * step)
    ab_im = mag * jnp.sin(li * step)
    den = lr * lr + li * li
    f_re = ((ab_re - 1.0) * lr + ab_im * li) / den
    f_im = (ab_im * lr - (ab_re - 1.0) * li) / den
    return ab_re, ab_im, f_re[None] * br - f_im[None] * bi, f_re[None] * bi + f_im[None] * br


def _loss_step(x, tgt, g):
    def f(x_, g_):
        e = _rms(x_, g_) - tgt
        return 0.5 * jnp.sum(jnp.mean(e * e, axis=-1, keepdims=True), axis=0, keepdims=True)
    loss, vjp = jax.vjp(f, x, g)
    dx, dg = vjp(jnp.ones((1, 1), F32))
    return dx, jnp.broadcast_to(loss, (8, LANES)), dg


def _tile(n, pref):
    if n <= pref:
        return n
    t = (pref // LANES) * LANES
    while n % t:
        t -= LANES
    return t


_DIMS = {"nn": (((1,), (0,)), ((), ())), "nt": (((1,), (1,)), ((), ())), "tn": (((0,), (0,)), ((), ()))}


def _mm(a, b, mode, name, res=None, tm=512, tn=1024, tk=1024):
    if mode == "nn":
        (m, k), n = a.shape, b.shape[1]
    elif mode == "nt":
        (m, k), n = a.shape, b.shape[0]
    else:
        (k, m), n = a.shape, b.shape[1]
    tm, tn, tk = _tile(m, tm), _tile(n, tn), _tile(k, tk)
    nk = k // tk
    a_spec = pl.BlockSpec((tk, tm), lambda i, j, kk: (kk, i)) if mode == "tn" else pl.BlockSpec((tm, tk), lambda i, j, kk: (i, kk))
    b_spec = pl.BlockSpec((tn, tk), lambda i, j, kk: (j, kk)) if mode == "nt" else pl.BlockSpec((tk, tn), lambda i, j, kk: (kk, j))
    o_spec = pl.BlockSpec((tm, tn), lambda i, j, kk: (i, j))
    has_res = res is not None

    def body(*refs):
        a_ref, b_ref = refs[0], refs[1]
        o_ref, acc = refs[-2], refs[-1]
        kk = pl.program_id(2)

        @pl.when(kk == 0)
        def _():
            acc[...] = jnp.zeros_like(acc)

        acc[...] += lax.dot_general(a_ref[...].astype(BF16), b_ref[...].astype(BF16), _DIMS[mode], preferred_element_type=F32)

        @pl.when(kk == nk - 1)
        def _():
            o_ref[...] = acc[...] + refs[2][...] if has_res else acc[...]

    return pl.pallas_call(
        body, grid=(m // tm, n // tn, nk),
        in_specs=[a_spec, b_spec] + ([o_spec] if has_res else []),
        out_specs=o_spec, out_shape=jax.ShapeDtypeStruct((m, n), F32),
        scratch_shapes=[pltpu.VMEM((tm, tn), F32)],
        compiler_params=pltpu.CompilerParams(dimension_semantics=("parallel", "parallel", "arbitrary"), vmem_limit_bytes=VMEM_LIMIT),
        name=name)(*([a, b] + ([res] if has_res else [])))


def _seq_block(t):
    return min(512, t)


def _causal_diff(i, j, blk):
    r = lax.broadcasted_iota(jnp.int32, (blk, blk), 0)
    c = lax.broadcasted_iota(jnp.int32, (blk, blk), 1)
    return (i - j) * blk + r - c


def _ret_decay(lg, i, j, blk):
    diff = _causal_diff(i, j, blk)
    return jnp.where(diff >= 0, jnp.exp(lg * jnp.maximum(diff, 0).astype(F32)), 0.0)


def _ret_log_gamma():
    return jnp.log1p(-(2.0 ** (-5.0 - jnp.arange(RET_HEADS, dtype=F32))))


def _ret_fwd(qr, kr, proj, t):
    blk = _seq_block(t)
    nb = t // blk
    v_off = (2 * RET_HEADS * RET_DK) // RET_DV

    def body(lg_ref, q_ref, k_ref, v_ref, o_ref, acc):
        h, i, j = pl.program_id(0), pl.program_id(1), pl.program_id(2)

        @pl.when(j == 0)
        def _():
            acc[...] = jnp.zeros_like(acc)

        @pl.when(j <= i)
        def _():
            s = lax.dot_general(q_ref[...].astype(BF16), k_ref[...].astype(BF16), _DIMS["nt"], preferred_element_type=F32)
            p = (s * _ret_decay(lg_ref[h], i, j, blk)).astype(BF16)
            acc[...] += jnp.dot(p, v_ref[...].astype(BF16), preferred_element_type=F32)

        @pl.when(j == nb - 1)
        def _():
            o_ref[...] = acc[...]

    return pl.pallas_call(
        body, grid=(RET_HEADS, nb, nb),
        in_specs=[pl.BlockSpec(memory_space=pltpu.SMEM),
                  pl.BlockSpec((blk, RET_DK), lambda h, i, j: (i, h)),
                  pl.BlockSpec((blk, RET_DK), lambda h, i, j: (jnp.minimum(j, i), h)),
                  pl.BlockSpec((blk, RET_DV), lambda h, i, j: (jnp.minimum(j, i), v_off + h))],
        out_specs=pl.BlockSpec((blk, RET_DV), lambda h, i, j: (i, h)),
        out_shape=jax.ShapeDtypeStruct((t, RET_HEADS * RET_DV), F32),
        scratch_shapes=[pltpu.VMEM((blk, RET_DV), F32)],
        compiler_params=pltpu.CompilerParams(dimension_semantics=("arbitrary",) * 3, vmem_limit_bytes=VMEM_LIMIT),
        name="ret_fwd")(_ret_log_gamma(), qr, kr, proj)


def _ret_bwd(qr, kr, proj, dr, t):
    blk = _seq_block(t)
    nb = t // blk
    v_off = (2 * RET_HEADS * RET_DK) // RET_DV

    def dq_body(lg_ref, q_ref, k_ref, v_ref, do_ref, dq_ref, acc):
        h, i, j = pl.program_id(0), pl.program_id(1), pl.program_id(2)

        @pl.when(j == 0)
        def _():
            acc[...] = jnp.zeros_like(acc)

        @pl.when(j <= i)
        def _():
            ds = lax.dot_general(do_ref[...].astype(BF16), v_ref[...].astype(BF16), _DIMS["nt"], preferred_element_type=F32)
            dsm = (ds * _ret_decay(lg_ref[h], i, j, blk)).astype(BF16)
            acc[...] += jnp.dot(dsm, k_ref[...].astype(BF16), preferred_element_type=F32)

        @pl.when(j == nb - 1)
        def _():
            dq_ref[...] = acc[...]

    dq = pl.pallas_call(
        dq_body, grid=(RET_HEADS, nb, nb),
        in_specs=[pl.BlockSpec(memory_space=pltpu.SMEM),
                  pl.BlockSpec((blk, RET_DK), lambda h, i, j: (i, h)),
                  pl.BlockSpec((blk, RET_DK), lambda h, i, j: (jnp.minimum(j, i), h)),
                  pl.BlockSpec((blk, RET_DV), lambda h, i, j: (jnp.minimum(j, i), v_off + h)),
                  pl.BlockSpec((blk, RET_DV), lambda h, i, j: (i, h))],
        out_specs=pl.BlockSpec((blk, RET_DK), lambda h, i, j: (i, h)),
        out_shape=jax.ShapeDtypeStruct((t, RET_HEADS * RET_DK), F32),
        scratch_shapes=[pltpu.VMEM((blk, RET_DK), F32)],
        compiler_params=pltpu.CompilerParams(dimension_semantics=("arbitrary",) * 3, vmem_limit_bytes=VMEM_LIMIT),
        name="ret_bwd_dq")(_ret_log_gamma(), qr, kr, proj, dr)

    def dkv_body(lg_ref, q_ref, k_ref, v_ref, do_ref, dk_ref, dv_ref, acc_k, acc_v):
        h, j, i = pl.program_id(0), pl.program_id(1), pl.program_id(2)

        @pl.when(i == 0)
        def _():
            acc_k[...] = jnp.zeros_like(acc_k)
            acc_v[...] = jnp.zeros_like(acc_v)

        @pl.when(i >= j)
        def _():
            q = q_ref[...].astype(BF16)
            do = do_ref[...].astype(BF16)
            decay = _ret_decay(lg_ref[h], i, j, blk)
            s = lax.dot_general(q, k_ref[...].astype(BF16), _DIMS["nt"], preferred_element_type=F32)
            acc_v[...] += lax.dot_general((s * decay).astype(BF16), do, _DIMS["tn"], preferred_element_type=F32)
            ds = lax.dot_general(do, v_ref[...].astype(BF16), _DIMS["nt"], preferred_element_type=F32)
            acc_k[...] += lax.dot_general((ds * decay).astype(BF16), q, _DIMS["tn"], preferred_element_type=F32)

        @pl.when(i == nb - 1)
        def _():
            dk_ref[...] = acc_k[...]
            dv_ref[...] = acc_v[...]

    dk, dv = pl.pallas_call(
        dkv_body, grid=(RET_HEADS, nb, nb),
        in_specs=[pl.BlockSpec(memory_space=pltpu.SMEM),
                  pl.BlockSpec((blk, RET_DK), lambda h, j, i: (jnp.maximum(i, j), h)),
                  pl.BlockSpec((blk, RET_DK), lambda h, j, i: (j, h)),
                  pl.BlockSpec((blk, RET_DV), lambda h, j, i: (j, v_off + h)),
                  pl.BlockSpec((blk, RET_DV), lambda h, j, i: (jnp.maximum(i, j), h))],
        out_specs=[pl.BlockSpec((blk, RET_DK), lambda h, j, i: (j, h)), pl.BlockSpec((blk, RET_DV), lambda h, j, i: (j, h))],
        out_shape=[jax.ShapeDtypeStruct((t, RET_HEADS * RET_DK), F32), jax.ShapeDtypeStruct((t, RET_HEADS * RET_DV), F32)],
        scratch_shapes=[pltpu.VMEM((blk, RET_DK), F32), pltpu.VMEM((blk, RET_DV), F32)],
        compiler_params=pltpu.CompilerParams(dimension_semantics=("arbitrary",) * 3, vmem_limit_bytes=VMEM_LIMIT),
        name="ret_bwd_dkv")(_ret_log_gamma(), qr, kr, proj, dr)
    return dq, dk, dv


def _cumsum(x, reverse, name):
    t = x.shape[0]
    blk = _seq_block(t)
    nb = t // blk

    def body(x_ref, o_ref, carry):
        @pl.when(pl.program_id(0) == 0)
        def _():
            carry[...] = jnp.zeros_like(carry)

        r = lax.broadcasted_iota(jnp.int32, (blk, blk), 0)
        c = lax.broadcasted_iota(jnp.int32, (blk, blk), 1)
        tri = ((r <= c) if reverse else (r >= c)).astype(F32)
        o_ref[...] = jnp.dot(tri, x_ref[...], precision=HIGHEST, preferred_element_type=F32) + carry[...]
        carry[...] = o_ref[0:1, :] if reverse else o_ref[blk - 1:blk, :]

    idx = (lambda i: (nb - 1 - i, 0)) if reverse else (lambda i: (i, 0))
    return pl.pallas_call(
        body, grid=(nb,), in_specs=[pl.BlockSpec((blk, LANES), idx)], out_specs=pl.BlockSpec((blk, LANES), idx),
        out_shape=jax.ShapeDtypeStruct((t, LANES), F32), scratch_shapes=[pltpu.VMEM((1, LANES), F32)],
        compiler_params=pltpu.CompilerParams(dimension_semantics=("arbitrary",), vmem_limit_bytes=VMEM_LIMIT),
        name=name)(x)


def _ssd_decay(a_ref, at_ref, hh, mask):
    return jnp.exp(jnp.where(mask, a_ref[:, hh:hh + 1] - at_ref[hh:hh + 1, :], -jnp.inf))


_B_OFF, _C_OFF = 1024 // SSM_N, 1024 // SSM_N + SSM_GROUPS


def _ssd_fwd(xbc_act, xdt, a_cum, a_cum_t, t):
    blk = _seq_block(t)
    nb = t // blk

    def body(c0, c1, b0, b1, x_ref, a_ref, at_ref, o_ref, acc):
        i, j = pl.program_id(0), pl.program_id(1)

        @pl.when(j == 0)
        def _():
            acc[...] = jnp.zeros_like(acc)

        @pl.when(j <= i)
        def _():
            mask = _causal_diff(i, j, blk) >= 0
            for g, (c_ref, b_ref) in enumerate(((c0, b0), (c1, b1))):
                cb = lax.dot_general(c_ref[...].astype(BF16), b_ref[...].astype(BF16), _DIMS["nt"], preferred_element_type=F32)
                for h in range(SSM_HG):
                    hh = g * SSM_HG + h
                    cols = slice(hh * SSM_P, (hh + 1) * SSM_P)
                    m = (cb * _ssd_decay(a_ref, at_ref, hh, mask)).astype(BF16)
                    acc[:, cols] += jnp.dot(m, x_ref[:, cols].astype(BF16), preferred_element_type=F32)

        @pl.when(j == nb - 1)
        def _():
            o_ref[...] = acc[...]

    row_i = lambda off: pl.BlockSpec((blk, SSM_N), lambda i, j, off=off: (i, off))
    row_j = lambda off: pl.BlockSpec((blk, SSM_N), lambda i, j, off=off: (jnp.minimum(j, i), off))
    return pl.pallas_call(
        body, grid=(nb, nb),
        in_specs=[row_i(_C_OFF), row_i(_C_OFF + 1), row_j(_B_OFF), row_j(_B_OFF + 1),
                  pl.BlockSpec((blk, SSM_HEADS * SSM_P), lambda i, j: (jnp.minimum(j, i), 0)),
                  pl.BlockSpec((blk, LANES), lambda i, j: (i, 0)),
                  pl.BlockSpec((SSM_HEADS, blk), lambda i, j: (0, jnp.minimum(j, i)))],
        out_specs=pl.BlockSpec((blk, SSM_HEADS * SSM_P), lambda i, j: (i, 0)),
        out_shape=jax.ShapeDtypeStruct((t, SSM_HEADS * SSM_P), F32),
        scratch_shapes=[pltpu.VMEM((blk, SSM_HEADS * SSM_P), F32)],
        compiler_params=pltpu.CompilerParams(dimension_semantics=("arbitrary",) * 2, vmem_limit_bytes=VMEM_LIMIT),
        name="ssd_fwd")(xbc_act, xbc_act, xbc_act, xbc_act, xdt, a_cum, a_cum_t)


def _ssd_bwd(xbc_act, xdt, a_cum, a_cum_t, dy, t):
    blk = _seq_block(t)
    nb = t // blk
    width = SSM_HEADS * SSM_P

    def q_body(c0, c1, b0, b1, x_ref, a_ref, at_ref, dy_ref, dc_ref, da_ref, acc_c, acc_a):
        i, j = pl.program_id(0), pl.program_id(1)

        @pl.when(j == 0)
        def _():
            acc_c[...] = jnp.zeros_like(acc_c)
            acc_a[...] = jnp.zeros_like(acc_a)

        @pl.when(j <= i)
        def _():
            mask = _causal_diff(i, j, blk) >= 0
            for g, (c_ref, b_ref) in enumerate(((c0, b0), (c1, b1))):
                bj = b_ref[...].astype(BF16)
                cb = lax.dot_general(c_ref[...].astype(BF16), bj, _DIMS["nt"], preferred_element_type=F32)
                dcb = jnp.zeros((blk, blk), F32)
                for h in range(SSM_HG):
                    hh = g * SSM_HG + h
                    cols = slice(hh * SSM_P, (hh + 1) * SSM_P)
                    decay = _ssd_decay(a_ref, at_ref, hh, mask)
                    dm = lax.dot_general(dy_ref[:, cols].astype(BF16), x_ref[:, cols].astype(BF16), _DIMS["nt"],
                                         preferred_element_type=F32) * decay
                    dcb = dcb + dm
                    acc_a[:, hh:hh + 1] += jnp.sum(dm * cb, axis=1, keepdims=True)
                acc_c[:, g * SSM_N:(g + 1) * SSM_N] += jnp.dot(dcb.astype(BF16), bj, preferred_element_type=F32)

        @pl.when(j == nb - 1)
        def _():
            dc_ref[...] = acc_c[...]
            da_ref[...] = acc_a[...]

    row_i = lambda off: pl.BlockSpec((blk, SSM_N), lambda i, j, off=off: (i, off))
    row_j = lambda off: pl.BlockSpec((blk, SSM_N), lambda i, j, off=off: (jnp.minimum(j, i), off))
    dc, da_q = pl.pallas_call(
        q_body, grid=(nb, nb),
        in_specs=[row_i(_C_OFF), row_i(_C_OFF + 1), row_j(_B_OFF), row_j(_B_OFF + 1),
                  pl.BlockSpec((blk, width), lambda i, j: (jnp.minimum(j, i), 0)),
                  pl.BlockSpec((blk, LANES), lambda i, j: (i, 0)),
                  pl.BlockSpec((SSM_HEADS, blk), lambda i, j: (0, jnp.minimum(j, i))),
                  pl.BlockSpec((blk, width), lambda i, j: (i, 0))],
        out_specs=[pl.BlockSpec((blk, SSM_GROUPS * SSM_N), lambda i, j: (i, 0)), pl.BlockSpec((blk, LANES), lambda i, j: (i, 0))],
        out_shape=[jax.ShapeDtypeStruct((t, SSM_GROUPS * SSM_N), F32), jax.ShapeDtypeStruct((t, LANES), F32)],
        scratch_shapes=[pltpu.VMEM((blk, SSM_GROUPS * SSM_N), F32), pltpu.VMEM((blk, LANES), F32)],
        compiler_params=pltpu.CompilerParams(dimension_semantics=("arbitrary",) * 2, vmem_limit_bytes=VMEM_LIMIT),
        name="ssd_bwd_q")(xbc_act, xbc_act, xbc_act, xbc_act, xdt, a_cum, a_cum_t, dy)

    def k_body(c0, c1, b0, b1, x_ref, a_ref, at_ref, dy_ref, db_ref, dx_ref, dat_ref, acc_b, acc_x, acc_a):
        j, i = pl.program_id(0), pl.program_id(1)

        @pl.when(i == 0)
        def _():
            acc_b[...] = jnp.zeros_like(acc_b)
            acc_x[...] = jnp.zeros_like(acc_x)
            acc_a[...] = jnp.zeros_like(acc_a)

        @pl.when(i >= j)
        def _():
            mask = _causal_diff(i, j, blk) >= 0
            for g, (c_ref, b_ref) in enumerate(((c0, b0), (c1, b1))):
                ci = c_ref[...].astype(BF16)
                cb = lax.dot_general(ci, b_ref[...].astype(BF16), _DIMS["nt"], preferred_element_type=F32)
                dcb = jnp.zeros((blk, blk), F32)
                for h in range(SSM_HG):
                    hh = g * SSM_HG + h
                    cols = slice(hh * SSM_P, (hh + 1) * SSM_P)
                    decay = _ssd_decay(a_ref, at_ref, hh, mask)
                    dyh = dy_ref[:, cols].astype(BF16)
                    acc_x[:, cols] += lax.dot_general((cb * decay).astype(BF16), dyh, _DIMS["tn"], preferred_element_type=F32)
                    dm = lax.dot_general(dyh, x_ref[:, cols].astype(BF16), _DIMS["nt"], preferred_element_type=F32) * decay
                    dcb = dcb + dm
                    acc_a[hh:hh + 1, :] += jnp.sum(dm * cb, axis=0, keepdims=True)
                acc_b[:, g * SSM_N:(g + 1) * SSM_N] += lax.dot_general(dcb.astype(BF16), ci, _DIMS["tn"], preferred_element_type=F32)

        @pl.when(i == nb - 1)
        def _():
            db_ref[...] = acc_b[...]
            dx_ref[...] = acc_x[...]
            dat_ref[...] = acc_a[...]

    rowk_i = lambda off: pl.BlockSpec((blk, SSM_N), lambda j, i, off=off: (jnp.maximum(i, j), off))
    rowk_j = lambda off: pl.BlockSpec((blk, SSM_N), lambda j, i, off=off: (j, off))
    db, dx, da_k_t = pl.pallas_call(
        k_body, grid=(nb, nb),
        in_specs=[rowk_i(_C_OFF), rowk_i(_C_OFF + 1), rowk_j(_B_OFF), rowk_j(_B_OFF + 1),
                  pl.BlockSpec((blk, width), lambda j, i: (j, 0)),
                  pl.BlockSpec((blk, LANES), lambda j, i: (jnp.maximum(i, j), 0)),
                  pl.BlockSpec((SSM_HEADS, blk), lambda j, i: (0, j)),
                  pl.BlockSpec((blk, width), lambda j, i: (jnp.maximum(i, j), 0))],
        out_specs=[pl.BlockSpec((blk, SSM_GROUPS * SSM_N), lambda j, i: (j, 0)), pl.BlockSpec((blk, width), lambda j, i: (j, 0)),
                   pl.BlockSpec((SSM_HEADS, blk), lambda j, i: (0, j))],
        out_shape=[jax.ShapeDtypeStruct((t, SSM_GROUPS * SSM_N), F32), jax.ShapeDtypeStruct((t, width), F32),
                   jax.ShapeDtypeStruct((SSM_HEADS, t), F32)],
        scratch_shapes=[pltpu.VMEM((blk, SSM_GROUPS * SSM_N), F32), pltpu.VMEM((blk, width), F32), pltpu.VMEM((SSM_HEADS, blk), F32)],
        compiler_params=pltpu.CompilerParams(dimension_semantics=("arbitrary",) * 2, vmem_limit_bytes=VMEM_LIMIT),
        name="ssd_bwd_k")(xbc_act, xbc_act, xbc_act, xbc_act, xdt, a_cum, a_cum_t, dy)
    return dc, da_q, db, dx, da_k_t


def _s5_scan(bu, a_re, a_im, reverse, x_prev=None):
    t = bu.shape[0]
    cb = LANES
    ncb = S5_COLS // cb
    pad = t // 2
    chunk = min(512, t)
    base = 0 if reverse else pad
    zero0 = t if reverse else 0

    def body(*refs):
        if reverse:
            br_ref, bi_ref, ar_ref, ai_ref, xr_ref, xi_ref, or_ref, oi_ref, dar_ref, dai_ref, sr, si = refs
        else:
            br_ref, bi_ref, ar_ref, ai_ref, or_ref, oi_ref, sr, si = refs
        sr[pl.ds(zero0, pad), :] = jnp.zeros((pad, cb), F32)
        si[pl.ds(zero0, pad), :] = jnp.zeros((pad, cb), F32)
        sr[pl.ds(base, t), :] = br_ref[...]
        si[pl.ds(base, t), :] = bi_ref[...]
        pr = ar_ref[...]
        pi = -ai_ref[...] if reverse else ai_ref[...]
        s = 1
        while s < t:
            shift = s if reverse else -s

            def step(t0, pr=pr, pi=pi, shift=shift):
                cr, ci = sr[pl.ds(base + t0, chunk), :], si[pl.ds(base + t0, chunk), :]
                qr, qi = sr[pl.ds(base + t0 + shift, chunk), :], si[pl.ds(base + t0 + shift, chunk), :]
                sr[pl.ds(base + t0, chunk), :] = cr + pr * qr - pi * qi
                si[pl.ds(base + t0, chunk), :] = ci + pr * qi + pi * qr

            order = range(0, t, chunk) if reverse else range(t - chunk, -1, -chunk)
            for t0 in order:
                step(t0)
            pr, pi = pr * pr - pi * pi, 2.0 * pr * pi
            s *= 2
        or_ref[...] = sr[pl.ds(base, t), :]
        oi_ref[...] = si[pl.ds(base, t), :]
        if reverse:
            dar = jnp.zeros((1, cb), F32)
            dai = jnp.zeros((1, cb), F32)
            for t0 in range(0, t, chunk):
                gr, gi = sr[pl.ds(t0 + 1, chunk), :], si[pl.ds(t0 + 1, chunk), :]
                xr, xi = xr_ref[pl.ds(t0, chunk), :], xi_ref[pl.ds(t0, chunk), :]
                dar = dar + jnp.sum(gr * xr + gi * xi, axis=0, keepdims=True)
                dai = dai + jnp.sum(gi * xr - gr * xi, axis=0, keepdims=True)
            dar_ref[...] = dar
            dai_ref[...] = dai

    re_spec = pl.BlockSpec((t, cb), lambda j: (0, j))
    im_spec = pl.BlockSpec((t, cb), lambda j: (0, j + ncb))
    a_spec = pl.BlockSpec((1, cb), lambda j: (0, j))
    ins, in_specs = [bu, bu, a_re, a_im], [re_spec, im_spec, a_spec, a_spec]
    out_shape = [jax.ShapeDtypeStruct((t, S5_COLS), F32)] * 2
    out_specs = [re_spec, re_spec]
    if reverse:
        ins += [x_prev, x_prev]
        in_specs += [re_spec, im_spec]
        out_shape += [jax.ShapeDtypeStruct((1, S5_COLS), F32)] * 2
        out_specs += [a_spec, a_spec]
    return pl.pallas_call(
        body, grid=(ncb,), in_specs=in_specs, out_specs=out_specs, out_shape=out_shape,
        scratch_shapes=[pltpu.VMEM((t + pad, cb), F32), pltpu.VMEM((t + pad, cb), F32)],
        compiler_params=pltpu.CompilerParams(dimension_semantics=("arbitrary",), vmem_limit_bytes=VMEM_LIMIT),
        name="s5_scan_bwd" if reverse else "s5_scan_fwd")(*ins)


def _exchange(src, broadcast, name):
    blk = src.shape if broadcast else src.shape[1:]

    def body(src_ref, out_ref, send_sems, recv_sems, local_sem):
        x, y, c = lax.axis_index("x"), lax.axis_index("y"), lax.axis_index("c")
        me = 4 * x + 2 * y + c

        def piece(idx):
            return src_ref if broadcast else src_ref.at[idx]

        local = pltpu.make_async_copy(piece(me), out_ref.at[me], local_sem)
        local.start()
        copies = []
        for k in range(1, N_DEV):
            px, py, pc = x ^ ((k >> 2) & 1), y ^ ((k >> 1) & 1), c ^ (k & 1)
            peer = 4 * px + 2 * py + pc
            send = pltpu.make_async_remote_copy(
                src_ref=piece(peer), dst_ref=out_ref.at[me], send_sem=send_sems.at[k - 1], recv_sem=recv_sems.at[k - 1],
                device_id=(px, py, pc), device_id_type=pl.DeviceIdType.MESH)
            send.start()
            recv = pltpu.make_async_remote_copy(
                src_ref=piece(peer), dst_ref=out_ref.at[peer], send_sem=send_sems.at[k - 1], recv_sem=recv_sems.at[k - 1],
                device_id=(px, py, pc), device_id_type=pl.DeviceIdType.MESH)
            copies.append((send, recv))
        for _, recv in copies:
            recv.wait_recv()
        for send, _ in copies:
            send.wait_send()
        local.wait()

    return pl.pallas_call(
        body, out_shape=jax.ShapeDtypeStruct((N_DEV,) + tuple(blk), src.dtype),
        in_specs=[pl.BlockSpec(memory_space=pl.ANY)], out_specs=pl.BlockSpec(memory_space=pl.ANY),
        scratch_shapes=[pltpu.SemaphoreType.DMA((N_DEV - 1,)), pltpu.SemaphoreType.DMA((N_DEV - 1,)), pltpu.SemaphoreType.DMA],
        compiler_params=pltpu.CompilerParams(has_side_effects=True),
        name=name)(src)


def _adamw(parts, w, m, v, name):
    r, c = w.shape
    tr = min(128, r)
    c1 = 1.0 / (1.0 - ADAM_B1 ** ADAM_STEP)
    c2 = 1.0 / (1.0 - ADAM_B2 ** ADAM_STEP)

    def body(p_ref, w_ref, m_ref, v_ref, g_ref, d_ref, nm_ref, nv_ref):
        g = p_ref[0]
        for s in range(1, N_DEV):
            g = g + p_ref[s]
        nm = ADAM_B1 * m_ref[...] + (1.0 - ADAM_B1) * g
        nv = ADAM_B2 * v_ref[...] + (1.0 - ADAM_B2) * (g * g)
        m_hat = nm / (1.0 - ADAM_B1 ** ADAM_STEP)
        v_hat = nv / (1.0 - ADAM_B2 ** ADAM_STEP)
        g_ref[...] = g
        nm_ref[...] = nm
        nv_ref[...] = nv
        d_ref[...] = -ADAM_LR * (m_hat / (jnp.sqrt(v_hat) + ADAM_EPS) + ADAM_WD * w_ref[...])

    spec = pl.BlockSpec((tr, c), lambda i: (i, 0))
    return pl.pallas_call(
        body, grid=(r // tr,), in_specs=[pl.BlockSpec((N_DEV, tr, c), lambda i: (0, i, 0)), spec, spec, spec],
        out_specs=[spec] * 4, out_shape=[jax.ShapeDtypeStruct((r, c), F32)] * 4,
        compiler_params=pltpu.CompilerParams(dimension_semantics=("parallel",), vmem_limit_bytes=VMEM_LIMIT),
        name=name)(parts, w, m, v)


def _pack_rows(n_elems, mult):
    rows = -(-n_elems // PACK_COLS)
    return -(-rows // mult) * mult


def _pack(arrays, dtype, mult, lead=()):
    flat = jnp.concatenate([a.astype(dtype).reshape(lead + (-1,)) for a in arrays], axis=-1)
    rows = _pack_rows(flat.shape[-1], mult)
    flat = jnp.pad(flat, [(0, 0)] * len(lead) + [(0, rows * PACK_COLS - flat.shape[-1])])
    return flat.reshape(lead + (rows, PACK_COLS))


def _unpack(buf, shapes, lead=()):
    flat = buf.reshape(lead + (-1,))
    out, off = [], 0
    for s in shapes:
        n = math.prod(s)
        out.append(flat[..., off:off + n].reshape(lead + tuple(s)))
        off += n
    return out


def _join_shards(piece, axis):
    moved = jnp.moveaxis(piece, 0, axis)
    shape = moved.shape
    return moved.reshape(shape[:axis] + (shape[axis] * shape[axis + 1],) + shape[axis + 2:])


def _split_shards(full, axis):
    shape = full.shape
    return jnp.moveaxis(full.reshape(shape[:axis] + (N_DEV, shape[axis] // N_DEV) + shape[axis + 1:]), axis, 0)


def _block_diag(blocks):
    g, r, c = blocks.shape
    eye = jnp.eye(g, dtype=blocks.dtype)
    return (blocks[:, :, None, :] * eye[:, None, :, None]).reshape(g * r, g * c)


def _diag_blocks(mat, g):
    r, c = mat.shape[0] // g, mat.shape[1] // g
    eye = jnp.eye(g, dtype=mat.dtype)
    return jnp.sum(mat.reshape(g, r, g, c) * eye[:, None, :, None], axis=2)


def _pad_lanes(a):
    a = a.reshape(1, -1)
    return jnp.pad(a, ((0, 0), (0, LANES - a.shape[1])))


def _head_expand():
    h = jnp.arange(LANES)[:, None]
    ch = jnp.arange(SSM_HEADS * SSM_P)[None, :] // SSM_P
    return (h == ch).astype(F32)


def _rotary_tables(t):
    inv = 10000.0 ** (-jnp.arange(0, RET_DK, 2, dtype=F32) / RET_DK)
    ang = jnp.arange(t).astype(F32)[:, None] * inv[None, :]
    cos, sin = jnp.cos(ang), jnp.sin(ang)
    return jnp.concatenate([cos, cos], axis=1), jnp.concatenate([-sin, sin], axis=1)


def _rms_fwd(x, g, name):
    return _rows(_f_rms, x.shape[0], [(x, D_MODEL, 0)], [g], [D_MODEL], [], name)[0]


def _rms_bwd(x, g, dh, dres, name):
    def fn(x_, dh_, dres_, g_):
        _, vjp = jax.vjp(lambda a, b: _rms(a, b), x_, g_)
        dx, dg = vjp(dh_)
        return dx + dres_, dg
    return _rows(fn, x.shape[0], [(x, D_MODEL, 0), (dh, D_MODEL, 0), (dres, D_MODEL, 0)], [g], [D_MODEL], [(1, D_MODEL)], name)


def _ffn_fwd(x, norm_g, w_up, dw_w, dw_b, w_down, tag):
    t = x.shape[0]
    nbk = D_FF // LANES
    h = _rms_fwd(x, norm_g, "ffn_norm_" + tag)
    up = _mm(h, w_up, "nn", "ffn_up_" + tag)
    mid = _cols(_f_ffnmid, nbk, [(up, 0), (up, nbk), (dw_w, 0), (dw_w, nbk), (dw_b, 0), (dw_b, nbk)], [(t,)], "ffn_mid_" + tag)[0]
    out = _mm(mid, w_down, "nn", "ffn_down_" + tag, res=x)
    return out, (h, up, mid)


def _ffn_bwd(x, norm_g, w_up, dw_w, dw_b, w_down, saved, dout, tag):
    t = x.shape[0]
    nbk = D_FF // LANES
    h, up, mid = saved
    d_w_down = _mm(mid, dout, "tn", "ffn_down_dw_" + tag)
    dmid = _mm(dout, w_down, "nt", "ffn_down_dx_" + tag)
    dgin, duin, dwg, dwu, dbg, dbu = _cols(
        _grad_fn(_f_ffnmid, 6, 6), nbk,
        [(up, 0), (up, nbk), (dw_w, 0), (dw_w, nbk), (dw_b, 0), (dw_b, nbk), (dmid, 0)],
        [(t,), (t,), (3, 1), (3, 1), (1,), (1,)], "ffn_mid_bwd_" + tag)
    dup = jnp.concatenate([dgin, duin], axis=1)
    d_w_up = _mm(h, dup, "tn", "ffn_up_dw_" + tag)
    dh = _mm(dup, w_up, "nt", "ffn_up_dx_" + tag)
    dx, dnorm = _rms_bwd(x, norm_g, dh, dout, "ffn_norm_bwd_" + tag)
    return dx, dict(norm=dnorm, w_up=d_w_up, dw_w=jnp.concatenate([dwg, dwu], axis=2)[:, 0], dw_b=jnp.concatenate([dbg, dbu], axis=1),
                    w_down=d_w_down)


def _local_step(x, tgt, w):
    t = x.shape[0]
    grads = {}
    expand = _head_expand()
    cosf, sins = _rotary_tables(t)
    mix_g = [w['mix_norm'][i:i + 1] for i in range(2)]
    ffn_g = [w['ffn_norm'][i:i + 1] for i in range(2)]
    ffn_dw_w = [w['ffn_dw_w'][i][:, None, :] for i in range(2)]
    ffn_dw_b = [w['ffn_dw_b'][i:i + 1] for i in range(2)]

    w_e = jnp.concatenate([w['e_w_in'][0], jnp.zeros((D_MODEL, E_IN_PAD - E_IN), BF16)], axis=1)
    conv_w = w['e_conv_w'][0][:, None, :]
    conv_b = w['e_conv_b']
    dt_bias, a_log, d_skip = _pad_lanes(w['e_dt_bias']), _pad_lanes(w['e_a_log']), _pad_lanes(w['e_d'])
    h0 = _rms_fwd(x, mix_g[0], "mix_norm_0")
    proj = _mm(h0, w_e, "nn", "e_in")
    qr, kr = _rows(_f_retpre, t, [(proj, 512, 0), (proj, 512, 1), (cosf, LANES, 0), (sins, LANES, 0)], [], [512, 512], [], "ret_pre")
    r = _ret_fwd(qr, kr, proj, t)
    y_ret = _rows(_f_retpost, t, [(r, 1024, 0), (proj, 1024, 2)], [], [1024], [], "ret_post")[0]
    xbc_act = _cols(_f_ssdconv, 12, [(proj, 32), (conv_w, 0), (conv_b, 0)], [(t,)], "ssd_conv")[0]
    xdt, da = _rows(_f_ssdpre, t, [(xbc_act, 1024, 0), (proj, LANES, 44)], [dt_bias, a_log, expand], [1024, LANES], [], "ssd_pre")
    a_cum = _cumsum(da, False, "ssd_cumsum")
    a_cum_t = a_cum[:, :SSM_HEADS].T
    yc = _ssd_fwd(xbc_act, xdt, a_cum, a_cum_t, t)
    y_ssm = _rows(_f_ssdpost, t, [(yc, 1024, 0), (xbc_act, 1024, 0), (proj, 1024, 3)], [d_skip, w['e_ssm_norm'], expand],
                  [1024], [], "ssd_post")[0]
    mix_e = jnp.concatenate([y_ret, y_ssm], axis=1)
    x1 = _mm(mix_e, w['e_w_out'][0], "nn", "e_out", res=x)
    x2, ffn0 = _ffn_fwd(x1, ffn_g[0], w['ffn_w_up'][0], ffn_dw_w[0], ffn_dw_b[0], w['ffn_w_down'][0], "0")

    lr, li = w['o_a_re'][0], w['o_a_im'][0]
    ls = w['o_log_step'].reshape(S5_GROUPS, 1)
    b_re3, b_im3 = jnp.transpose(w['o_b_re'][0], (2, 0, 1)), jnp.transpose(w['o_b_im'][0], (2, 0, 1))
    par_ins = [lr, li, ls, b_re3, b_im3]
    whole = lambda a: (a, a.shape, (lambda i, n=a.ndim: (0,) * n))
    par_shapes = [(S5_GROUPS, S5_STATE)] * 2 + [(S5_GROUP, S5_GROUPS, S5_STATE)] * 2
    ab_re, ab_im, bb_re, bb_im = _call(_f_s5par, (1,), [whole(a) for a in par_ins],
                                       [(s, F32, s, (lambda i, n=len(s): (0,) * n), False) for s in par_shapes], "s5_params")
    w_b = jnp.concatenate([_block_diag(jnp.transpose(bb_re, (1, 0, 2))), _block_diag(jnp.transpose(bb_im, (1, 0, 2)))], axis=1)
    w_c = jnp.concatenate([_block_diag(jnp.transpose(w['o_c_re'][0], (0, 2, 1))),
                           -_block_diag(jnp.transpose(w['o_c_im'][0], (0, 2, 1)))], axis=0)
    a_re, a_im = ab_re.reshape(1, S5_COLS), ab_im.reshape(1, S5_COLS)
    dw_w = w['o_dw_w'][0][:, None, :]
    glu_w = w['o_glu_w'][0].astype(F32)

    h1 = _rms_fwd(x2, mix_g[1], "mix_norm_1")
    proj_o = _mm(h1, w['o_w_in'][0], "nn", "o_in")
    c1 = _cols(_f_confa, 4, [(proj_o, 0), (proj_o, 4), (dw_w, 0), (w['o_dw_b'], 0)], [(t,)], "conf_conv")[0]
    c2 = _rows(_f_confb, t, [(c1, 512, 0)], [w['o_ln_g'], w['o_ln_b']], [512], [], "conf_norm")[0]
    u = proj_o[:, 1024:]
    bu = _mm(u, w_b, "nn", "s5_bu")
    xs_re, xs_im = _s5_scan(bu, a_re, a_im, False)
    xs_cat = jnp.concatenate([xs_re, xs_im], axis=1)
    y_s5 = _mm(xs_cat, w_c, "nn", "s5_cx")
    s_out = _rows(_f_s5post, t, [(y_s5, 512, 0), (proj_o, 512, 2)], [w['o_d'], glu_w], [512], [], "s5_post")[0]
    mix_o = jnp.concatenate([c2, s_out], axis=1)
    x3 = _mm(mix_o, w['o_w_out'][0], "nn", "o_out", res=x2)
    x4, ffn1 = _ffn_fwd(x3, ffn_g[1], w['ffn_w_up'][1], ffn_dw_w[1], ffn_dw_b[1], w['ffn_w_down'][1], "1")

    dx4, loss_blk, d_final = _rows(_loss_step, t, [(x4, D_MODEL, 0), (tgt, D_MODEL, 0)], [w['final_norm'].reshape(1, D_MODEL)],
                                   [D_MODEL], [(8, LANES), (1, D_MODEL)], "loss_head")
    loss = loss_blk[0, 0]
    grads['final_norm'] = d_final.reshape(D_MODEL)

    dx3, g1 = _ffn_bwd(x3, ffn_g[1], w['ffn_w_up'][1], ffn_dw_w[1], ffn_dw_b[1], w['ffn_w_down'][1], ffn1, dx4, "1")
    grads['o_w_out'] = _mm(mix_o, dx3, "tn", "o_out_dw")[None]
    dmix_o = _mm(dx3, w['o_w_out'][0], "nt", "o_out_dx")
    dc1, d_ln_g, d_ln_b = _conf_norm_bwd(c1, w['o_ln_g'], w['o_ln_b'], dmix_o, t)
    dca, dcg, d_dw_w, d_dw_b = _cols(_grad_fn(_f_confa, 4, 4), 4, [(proj_o, 0), (proj_o, 4), (dw_w, 0), (w['o_dw_b'], 0), (dc1, 0)],
                                     [(t,), (t,), (31, 1), (1,)], "conf_conv_bwd")
    dyc, du_skip, d_od, d_glu = _s5_post_bwd(y_s5, proj_o, w['o_d'], glu_w, dmix_o, t)
    grads['o_glu_w'] = d_glu[None]
    d_w_c = _mm(xs_cat, dyc, "tn", "s5_cx_dw")
    dxs = _mm(dyc, w_c, "nt", "s5_cx_dx")
    g_re, g_im, d_are, d_aim = _s5_scan(dxs, a_re, a_im, True, xs_cat)
    g_cat = jnp.concatenate([g_re, g_im], axis=1)
    d_w_b = _mm(u, g_cat, "tn", "s5_bu_dw")
    du = _mm(g_cat, w_b, "nt", "s5_bu_dx", res=du_skip)
    d_bb_re = jnp.transpose(_diag_blocks(d_w_b[:, :S5_COLS], S5_GROUPS), (1, 0, 2))
    d_bb_im = jnp.transpose(_diag_blocks(d_w_b[:, S5_COLS:], S5_GROUPS), (1, 0, 2))
    par_cts = [d_are.reshape(S5_GROUPS, S5_STATE), d_aim.reshape(S5_GROUPS, S5_STATE), d_bb_re, d_bb_im]
    in_shapes = [a.shape for a in par_ins]
    d_lr, d_li, d_ls, d_br3, d_bi3 = _call(_grad_fn(_f_s5par, 5, 5), (1,), [whole(a) for a in par_ins + par_cts],
                                           [(s, F32, s, (lambda i, n=len(s): (0,) * n), False) for s in in_shapes], "s5_params_bwd")
    grads['o_a_re'], grads['o_a_im'], grads['o_log_step'] = d_lr[None], d_li[None], d_ls.reshape(1, S5_GROUPS)
    grads['o_b_re'], grads['o_b_im'] = jnp.transpose(d_br3, (1, 2, 0))[None], jnp.transpose(d_bi3, (1, 2, 0))[None]
    grads['o_c_re'] = jnp.transpose(_diag_blocks(d_w_c[:S5_COLS], S5_GROUPS), (0, 2, 1))[None]
    grads['o_c_im'] = -jnp.transpose(_diag_blocks(d_w_c[S5_COLS:], S5_GROUPS), (0, 2, 1))[None]
    grads['o_d'], grads['o_ln_g'], grads['o_ln_b'] = d_od, d_ln_g, d_ln_b
    grads['o_dw_w'], grads['o_dw_b'] = d_dw_w[:, 0][None], d_dw_b
    dproj_o = jnp.concatenate([dca, dcg, du], axis=1)
    grads['o_w_in'] = _mm(h1, dproj_o, "tn", "o_in_dw")[None]
    dh1 = _mm(dproj_o, w['o_w_in'][0], "nt", "o_in_dx")
    dx2, d_mix1 = _rms_bwd(x2, mix_g[1], dh1, dx3, "mix_norm_bwd_1")

    dx1, g0 = _ffn_bwd(x1, ffn_g[0], w['ffn_w_up'][0], ffn_dw_w[0], ffn_dw_b[0], w['ffn_w_down'][0], ffn0, dx2, "0")
    grads['e_w_out'] = _mm(mix_e, dx1, "tn", "e_out_dw")[None]
    dmix_e = _mm(dx1, w['e_w_out'][0], "nt", "e_out_dx")
    dr, dg = _rows(_grad_fn(_f_retpost, 2, 2), t, [(r, 1024, 0), (proj, 1024, 2), (dmix_e, 1024, 0)], [], [1024, 1024], [], "ret_post_bwd")
    dqr, dkr, dv = _ret_bwd(qr, kr, proj, dr, t)
    dq, dk = _ret_pre_bwd(proj, cosf, sins, dqr, dkr, t)
    dyc0, dxs1, dz, d_dskip, d_ssm_norm = _ssd_post_bwd(yc, xbc_act, proj, d_skip, w['e_ssm_norm'], expand, dmix_e, t)
    dcm, da_q, dbm, dxdt, da_k_t = _ssd_bwd(xbc_act, xdt, a_cum, a_cum_t, dyc0, t)
    d_a_cum = da_q - jnp.pad(da_k_t.T, ((0, 0), (0, LANES - SSM_HEADS)))
    dda = _cumsum(d_a_cum, True, "ssd_cumsum_bwd")
    dxs, ddtr, d_dt_bias, d_a_log = _ssd_pre_bwd(xbc_act, proj, dt_bias, a_log, expand, dxdt, dda, dxs1, t)
    dxbc_act = jnp.concatenate([dxs, dbm, dcm], axis=1)
    dxbc, d_conv_w, d_conv_b = _cols(_grad_fn(_f_ssdconv, 3, 3), 12, [(proj, 32), (conv_w, 0), (conv_b, 0), (dxbc_act, 0)],
                                     [(t,), (4, 1), (1,)], "ssd_conv_bwd")
    dproj = jnp.concatenate([dq, dk, dv, dg, dz, dxbc, ddtr], axis=1)
    grads['e_w_in'] = _mm(h0, dproj, "tn", "e_in_dw")[:, :E_IN][None]
    dh0 = _mm(dproj, w_e, "nt", "e_in_dx")
    dx0, d_mix0 = _rms_bwd(x, mix_g[0], dh0, dx1, "mix_norm_bwd_0")

    grads['mix_norm'] = jnp.concatenate([d_mix0, d_mix1], axis=0)
    grads['e_conv_w'], grads['e_conv_b'] = d_conv_w[:, 0][None], d_conv_b
    grads['e_dt_bias'], grads['e_a_log'], grads['e_d'] = d_dt_bias[:, :SSM_HEADS], d_a_log[:, :SSM_HEADS], d_dskip[:, :SSM_HEADS]
    grads['e_ssm_norm'] = d_ssm_norm
    grads['ffn_norm'] = jnp.concatenate([g0['norm'], g1['norm']], axis=0)
    for key in ('w_up', 'dw_w', 'dw_b', 'w_down'):
        grads['ffn_' + key] = jnp.stack([g0[key], g1[key]], axis=0) if key != 'dw_b' else jnp.concatenate([g0[key], g1[key]], axis=0)
    return loss, dx0, grads


def _conf_norm_bwd(c1, ln_g, ln_b, dmix_o, t):
    def fn(c1_, dy_, g_, b_):
        _, vjp = jax.vjp(lambda a, b, c: _f_confb(a, b, c)[0], c1_, g_, b_)
        return vjp(dy_)
    return _rows(fn, t, [(c1, 512, 0), (dmix_o, 512, 0)], [ln_g, ln_b], [512], [(1, 512), (1, 512)], "conf_norm_bwd")


def _s5_post_bwd(y_s5, proj_o, d_skip, glu_w, dmix_o, t):
    def fn(yc_, u_, dy_, d_, gw_):
        _, vjp = jax.vjp(lambda a, b, c, e: _f_s5post(a, b, c, e)[0], yc_, u_, d_, gw_)
        return vjp(dy_)
    return _rows(fn, t, [(y_s5, 512, 0), (proj_o, 512, 2), (dmix_o, 512, 1)], [d_skip, glu_w], [512, 512], [(1, 512), (512, 512)],
                 "s5_post_bwd")


def _ret_pre_bwd(proj, cosf, sins, dqr, dkr, t):
    def fn(q_, k_, cos_, sin_, dq_, dk_):
        _, vjp = jax.vjp(lambda a, b: _f_retpre(a, b, cos_, sin_), q_, k_)
        return vjp((dq_, dk_))
    return _rows(fn, t, [(proj, 512, 0), (proj, 512, 1), (cosf, LANES, 0), (sins, LANES, 0), (dqr, 512, 0), (dkr, 512, 0)], [],
                 [512, 512], [], "ret_pre_bwd")


def _ssd_post_bwd(yc, xbc_act, proj, d_skip, norm_w, expand, dmix_e, t):
    def fn(yc_, xs_, z_, dy_, d_, nw_, e_):
        _, vjp = jax.vjp(lambda a, b, c, dd, n: _f_ssdpost(a, b, c, dd, n, e_)[0], yc_, xs_, z_, d_, nw_)
        return vjp(dy_)
    return _rows(fn, t, [(yc, 1024, 0), (xbc_act, 1024, 0), (proj, 1024, 3), (dmix_e, 1024, 1)], [d_skip, norm_w, expand],
                 [1024, 1024, 1024], [(1, LANES), (1, 1024)], "ssd_post_bwd")


def _ssd_pre_bwd(xbc_act, proj, dt_bias, a_log, expand, dxdt, dda, dxs1, t):
    def fn(xs_, dtr_, dx_, dda_, dxs1_, bias_, alog_, e_):
        _, vjp = jax.vjp(lambda a, b, c, dd: _f_ssdpre(a, b, c, dd, e_), xs_, dtr_, bias_, alog_)
        dxs, ddtr, dbias, dalog = vjp((dx_, dda_))
        return dxs + dxs1_, ddtr, dbias, dalog
    return _rows(fn, t, [(xbc_act, 1024, 0), (proj, LANES, 44), (dxdt, 1024, 0), (dda, LANES, 0), (dxs1, 1024, 0)],
                 [dt_bias, a_log, expand], [1024, LANES], [(1, LANES), (1, LANES)], "ssd_pre_bwd")


def kernel(x, mix_norm, e_w_in, e_conv_w, e_conv_b, e_dt_bias, e_a_log, e_d, e_ssm_norm, e_w_out, o_w_in, o_dw_w, o_dw_b, o_ln_g, o_ln_b, o_a_re, o_a_im, o_b_re, o_b_im, o_c_re, o_c_im, o_d, o_log_step, o_glu_w, o_w_out, ffn_norm, ffn_w_up, ffn_dw_w, ffn_dw_b, ffn_w_down, final_norm, loss_target, m_mix_norm, m_e_w_in, m_e_conv_w, m_e_conv_b, m_e_dt_bias, m_e_a_log, m_e_d, m_e_ssm_norm, m_e_w_out, m_o_w_in, m_o_dw_w, m_o_dw_b, m_o_ln_g, m_o_ln_b, m_o_a_re, m_o_a_im, m_o_b_re, m_o_b_im, m_o_c_re, m_o_c_im, m_o_d, m_o_log_step, m_o_glu_w, m_o_w_out, m_ffn_norm, m_ffn_w_up, m_ffn_dw_w, m_ffn_dw_b, m_ffn_w_down, m_final_norm, v_mix_norm, v_e_w_in, v_e_conv_w, v_e_conv_b, v_e_dt_bias, v_e_a_log, v_e_d, v_e_ssm_norm, v_e_w_out, v_o_w_in, v_o_dw_w, v_o_dw_b, v_o_ln_g, v_o_ln_b, v_o_a_re, v_o_a_im, v_o_b_re, v_o_b_im, v_o_c_re, v_o_c_im, v_o_d, v_o_log_step, v_o_glu_w, v_o_w_out, v_ffn_norm, v_ffn_w_up, v_ffn_dw_w, v_ffn_dw_b, v_ffn_w_down, v_final_norm):
    p = dict(locals())

    w = {n: p[n] for n in REPLICATED}
    for names, dtype, tag in ((MATMUL_WEIGHTS, BF16, "gather_matmul_weights"), (SMALL_SHARDED, F32, "gather_small_weights")):
        gathered = _exchange(_pack([p[n] for n in names], dtype, 16), True, tag)
        for n, piece in zip(names, _unpack(gathered, [p[n].shape for n in names], lead=(N_DEV,))):
            w[n] = _join_shards(piece, SHARDED[n])

    loss, dx, grads = _local_step(x[0], loss_target[0], w)
    loss = lax.psum(loss, MESH_AXES)

    out = {}
    groups = ((SHARDED_ORDER, False, "sharded"), (REPLICATED, True, "replicated"))
    for names, broadcast, tag in groups:
        shapes = [p[n].shape for n in names]
        if broadcast:
            parts = _exchange(_pack([grads[n].reshape(p[n].shape) for n in names], F32, 128), True, "allgather_grads_" + tag)
        else:
            send = _pack([_split_shards(grads[n].reshape(p[n].shape[:SHARDED[n]] + (-1,) + p[n].shape[SHARDED[n] + 1:]), SHARDED[n])
                          for n in names], F32, 128, lead=(N_DEV,))
            parts = _exchange(send, False, "alltoall_grads_" + tag)
        packed = [_pack([p[pre + n] for n in names], F32, 128) for pre in ("", "m_", "v_")]
        results = _adamw(parts, *packed, "adamw_" + tag)
        for kind, buf in zip(("grad_", "delta_", "new_m_", "new_v_"), results):
            for n, a in zip(names, _unpack(buf, shapes)):
                out[kind + n] = a
    return (loss, dx[None], *[out[kind + n] for kind in ("grad_", "delta_", "new_m_", "new_v_") for n in WEIGHTS])
```

```python
import functools
import math

import jax
import jax.numpy as jnp
from jax import lax
from jax.experimental import pallas as pl
from jax.experimental.pallas import tpu as pltpu

F32, BF16 = jnp.float32, jnp.bfloat16
HIGHEST = lax.Precision.HIGHEST
N_DEV = 8
MESH_AXES = ("x", "y", "c")
VMEM_LIMIT = 48 * 1024 * 1024
LANES = 128
PACK_COLS = 1024

D_MODEL = 1024
EPS = 1e-6
RET_HEADS, RET_DK, RET_DV = 4, 128, 256
SSM_HEADS, SSM_P, SSM_N, SSM_GROUPS = 16, 64, 128, 2
SSM_HG = SSM_HEADS // SSM_GROUPS
S5_GROUPS, S5_GROUP, S5_STATE = 32, 16, 64
S5_COLS = S5_GROUPS * S5_STATE
D_FF = 2816
E_IN, E_IN_PAD = 5648, 5760
ADAM_LR, ADAM_B1, ADAM_B2, ADAM_EPS, ADAM_WD, ADAM_STEP = 0.001, 0.9, 0.999, 1e-08, 0.01, 10

WEIGHTS = ['mix_norm', 'e_w_in', 'e_conv_w', 'e_conv_b', 'e_dt_bias', 'e_a_log', 'e_d', 'e_ssm_norm', 'e_w_out', 'o_w_in', 'o_dw_w', 'o_dw_b', 'o_ln_g', 'o_ln_b', 'o_a_re', 'o_a_im', 'o_b_re', 'o_b_im', 'o_c_re', 'o_c_im', 'o_d', 'o_log_step', 'o_glu_w', 'o_w_out', 'ffn_norm', 'ffn_w_up', 'ffn_dw_w', 'ffn_dw_b', 'ffn_w_down', 'final_norm']
SHARDED = {'e_w_in': 2, 'e_conv_w': 2, 'e_w_out': 1, 'o_w_in': 2, 'o_dw_w': 2, 'o_dw_b': 1, 'o_ln_g': 1, 'o_ln_b': 1,
           'o_d': 1, 'o_glu_w': 1, 'o_w_out': 1, 'ffn_w_up': 2, 'ffn_dw_w': 2, 'ffn_w_down': 1}
MATMUL_WEIGHTS = ['e_w_in', 'e_w_out', 'o_w_in', 'o_glu_w', 'o_w_out', 'ffn_w_up', 'ffn_w_down']
MATMUL_BLOCKS = [('e_w_in', None, True), ('e_w_out', None, False), ('o_w_in', None, True), ('o_glu_w', None, False),
                 ('o_w_out', None, False), ('ffn_w_up', 0, True), ('ffn_w_up', 1, True), ('ffn_w_down', 0, False), ('ffn_w_down', 1, False)]
SMALL_SHARDED = [n for n in WEIGHTS if n in SHARDED and n not in MATMUL_WEIGHTS]
REPLICATED = [n for n in WEIGHTS if n not in SHARDED]


def _call(fn, grid, ins, outs, name):
    n_in = len(ins)

    def body(*refs):
        vals = fn(*[r[...] for r in refs[:n_in]])
        first = pl.program_id(0) == 0
        for r, v, o in zip(refs[n_in:], vals, outs):
            if o[4]:
                @pl.when(first)
                def _():
                    r[...] = jnp.zeros_like(r)
                r[...] += v.astype(r.dtype)
            else:
                r[...] = v.astype(r.dtype)

    return pl.pallas_call(
        body, grid=grid,
        in_specs=[pl.BlockSpec(b, m) for _, b, m in ins],
        out_specs=[pl.BlockSpec(o[2], o[3]) for o in outs],
        out_shape=[jax.ShapeDtypeStruct(o[0], o[1]) for o in outs],
        compiler_params=pltpu.CompilerParams(dimension_semantics=("arbitrary",) * len(grid), vmem_limit_bytes=VMEM_LIMIT),
        name=name)(*[a for a, _, _ in ins])


def _rows(fn, n_rows, row_ins, full_ins, row_outs, acc_outs, name, tm=256):
    tm = min(tm, n_rows)
    ins = [(a, (tm, w), (lambda i, c=c: (i, c))) for a, w, c in row_ins]
    ins += [(a, a.shape, (lambda i, n=a.ndim: (0,) * n)) for a in full_ins]
    outs = [((n_rows, w), F32, (tm, w), (lambda i: (i, 0)), False) for w in row_outs]
    outs += [(tuple(s), F32, tuple(s), (lambda i, n=len(s): (0,) * n), True) for s in acc_outs]
    return _call(fn, (n_rows // tm,), ins, outs, name)


def _cols(fn, n_blocks, col_ins, out_leads, name, cb=LANES):
    ins = [(a, a.shape[:-1] + (cb,), (lambda j, n=a.ndim, o=o: (0,) * (n - 1) + (j + o,))) for a, o in col_ins]
    outs = [(tuple(s) + (n_blocks * cb,), F32, tuple(s) + (cb,), (lambda j, n=len(s): (0,) * n + (j,)), False)
            for s in out_leads]
    return _call(fn, (n_blocks,), ins, outs, name)


def _grad_fn(f, n_diff, n_in):
    def g(*a):
        diff, consts, cts = a[:n_diff], a[n_diff:n_in], a[n_in:]
        _, vjp = jax.vjp(lambda *d: f(*d, *consts), *diff)
        return vjp(tuple(cts))
    return g


def _silu(x):
    return x * jax.nn.sigmoid(x)


def _rms(x, g):
    return x * lax.rsqrt(jnp.mean(x * x, axis=-1, keepdims=True) + EPS) * g


@jax.custom_vjp
def _softplus(x):
    return jnp.maximum(x, 0.0) + jnp.log(1.0 + jnp.exp(-jnp.abs(x)))


_softplus.defvjp(lambda x: (_softplus(x), x), lambda x, g: (g * jax.nn.sigmoid(x),))


@jax.custom_vjp
def _swap_halves(x):
    return pltpu.roll(x, 64, 1)


_swap_halves.defvjp(lambda x: (_swap_halves(x), None), lambda _, g: (_swap_halves(g),))


def _shift_rows(x, k, up):
    if k == 0:
        return x
    n = x.shape[0]
    t = lax.broadcasted_iota(jnp.int32, x.shape, 0)
    if up:
        return jnp.where(t < n - k, pltpu.roll(x, n - k, 0), 0.0)
    return jnp.where(t >= k, pltpu.roll(x, k, 0), 0.0)


@jax.custom_vjp
def _dwconv(x, w, b):
    k_taps = w.shape[0]
    y = b + w[k_taps - 1] * x
    for k in range(k_taps - 1):
        y = y + w[k] * _shift_rows(x, k_taps - 1 - k, False)
    return y


def _dwconv_fwd(x, w, b):
    return _dwconv(x, w, b), (x, w)


def _dwconv_bwd(saved, dy):
    x, w = saved
    k_taps = w.shape[0]
    dx = w[k_taps - 1] * dy
    dws = []
    for k in range(k_taps - 1):
        s = k_taps - 1 - k
        dx = dx + w[k] * _shift_rows(dy, s, True)
        dws.append(jnp.sum(dy * _shift_rows(x, s, False), axis=0, keepdims=True)[None])
    dws.append(jnp.sum(dy * x, axis=0, keepdims=True)[None])
    return dx, jnp.concatenate(dws, axis=0), jnp.sum(dy, axis=0, keepdims=True)


_dwconv.defvjp(_dwconv_fwd, _dwconv_bwd)


def _f_rms(x, g):
    return (_rms(x, g),)


def _rot(x, cosf, sins):
    outs = []
    for h in range(RET_HEADS):
        xh = x[:, h * RET_DK:(h + 1) * RET_DK]
        outs.append(xh * cosf + _swap_halves(xh) * sins)
    return jnp.concatenate(outs, axis=1)


def _f_retpre(q, k, cosf, sins):
    return _rot(q, cosf, sins), _rot(k, cosf, sins) * (RET_DK ** -0.5)


def _f_retpost(r, g):
    outs = []
    for h in range(RET_HEADS):
        rh = r[:, h * RET_DV:(h + 1) * RET_DV]
        rc = rh - jnp.mean(rh, axis=-1, keepdims=True)
        outs.append(_silu(g[:, h * RET_DV:(h + 1) * RET_DV]) * (rc * lax.rsqrt(jnp.mean(rc * rc, axis=-1, keepdims=True) + EPS)))
    return (jnp.concatenate(outs, axis=1),)


def _f_ssdconv(xbc, w, b):
    return (_silu(_dwconv(xbc, w, b)),)


def _f_ssdpre(xs, dtr, bias, alog, expand):
    dt = _softplus(dtr + bias)
    return xs * jnp.dot(dt, expand, precision=HIGHEST, preferred_element_type=F32), dt * (-jnp.exp(alog))


def _f_ssdpost(yc, xs, z, dskip, norm_w, expand):
    d_wide = jnp.dot(jnp.broadcast_to(dskip, (yc.shape[0], LANES)), expand, precision=HIGHEST, preferred_element_type=F32)
    y = (yc + d_wide * xs) * _silu(z)
    half = y.shape[1] // SSM_GROUPS
    outs = []
    for g in range(SSM_GROUPS):
        yg = y[:, g * half:(g + 1) * half]
        outs.append(yg * lax.rsqrt(jnp.mean(yg * yg, axis=-1, keepdims=True) + EPS))
    return (jnp.concatenate(outs, axis=1) * norm_w,)


def _f_ffnmid(gin, uin, wg, wu, bg, bu):
    return (_silu(_dwconv(gin, wg, bg)) * _dwconv(uin, wu, bu),)


def _f_confb(c1, g, b):
    mu = jnp.mean(c1, axis=-1, keepdims=True)
    xc = c1 - mu
    return (_silu(xc * lax.rsqrt(jnp.mean(xc * xc, axis=-1, keepdims=True) + EPS) * g + b),)


def _f_s5post(yc, u, dskip, glu_w):
    s = jax.nn.gelu(yc + dskip * u)
    z = jnp.dot(s.astype(BF16), glu_w.astype(BF16), preferred_element_type=F32)
    return (s * jax.nn.sigmoid(z),)


def _f_s5par(lr, li, ls, br, bi):
    step = jnp.exp(ls)
    mag = jnp.exp(lr * step)
    ab_re = mag * jnp.cos(li * step)
    ab_im = mag * jnp.sin(li * step)
    den = lr * lr + li * li
    f_re = ((ab_re - 1.0) * lr + ab_im * li) / den
    f_im = (ab_im * lr - (ab_re - 1.0) * li) / den
    return ab_re, ab_im, f_re[None] * br - f_im[None] * bi, f_re[None] * bi + f_im[None] * br


def _loss_step(x, tgt, g):
    def f(x_, g_):
        e = _rms(x_, g_) - tgt
        return 0.5 * jnp.sum(jnp.mean(e * e, axis=-1, keepdims=True), axis=0, keepdims=True)
    loss, vjp = jax.vjp(f, x, g)
    dx, dg = vjp(jnp.ones((1, 1), F32))
    return dx, jnp.broadcast_to(loss, (8, LANES)), dg


def _tile(n, pref):
    if n <= pref:
        return n
    t = (pref // LANES) * LANES
    while n % t:
        t -= LANES
    return t


_DIMS = {"nn": (((1,), (0,)), ((), ())), "nt": (((1,), (1,)), ((), ())), "tn": (((0,), (0,)), ((), ()))}


def _mm(a, b, mode, name, res=None, out_dtype=F32, tm=512, tn=1024, tk=1024):
    if mode == "nn":
        (m, k), n = a.shape, b.shape[1]
    elif mode == "nt":
        (m, k), n = a.shape, b.shape[0]
    else:
        (k, m), n = a.shape, b.shape[1]
    tm, tn, tk = _tile(m, tm), _tile(n, tn), _tile(k, tk)
    nk = k // tk
    a_spec = pl.BlockSpec((tk, tm), lambda i, j, kk: (kk, i)) if mode == "tn" else pl.BlockSpec((tm, tk), lambda i, j, kk: (i, kk))
    b_spec = pl.BlockSpec((tn, tk), lambda i, j, kk: (j, kk)) if mode == "nt" else pl.BlockSpec((tk, tn), lambda i, j, kk: (kk, j))
    o_spec = pl.BlockSpec((tm, tn), lambda i, j, kk: (i, j))
    has_res = res is not None

    def body(*refs):
        a_ref, b_ref = refs[0], refs[1]
        o_ref, acc = refs[-2], refs[-1]
        kk = pl.program_id(2)

        @pl.when(kk == 0)
        def _():
            acc[...] = jnp.zeros_like(acc)

        acc[...] += lax.dot_general(a_ref[...].astype(BF16), b_ref[...].astype(BF16), _DIMS[mode], preferred_element_type=F32)

        @pl.when(kk == nk - 1)
        def _():
            o_ref[...] = (acc[...] + refs[2][...] if has_res else acc[...]).astype(out_dtype)

    return pl.pallas_call(
        body, grid=(m // tm, n // tn, nk),
        in_specs=[a_spec, b_spec] + ([o_spec] if has_res else []),
        out_specs=o_spec, out_shape=jax.ShapeDtypeStruct((m, n), out_dtype),
        scratch_shapes=[pltpu.VMEM((tm, tn), F32)],
        compiler_params=pltpu.CompilerParams(dimension_semantics=("parallel", "parallel", "arbitrary"), vmem_limit_bytes=VMEM_LIMIT),
        name=name)(*([a, b] + ([res] if has_res else [])))


def _seq_block(t):
    return min(512, t)


def _causal_diff(i, j, blk):
    r = lax.broadcasted_iota(jnp.int32, (blk, blk), 0)
    c = lax.broadcasted_iota(jnp.int32, (blk, blk), 1)
    return (i - j) * blk + r - c


def _ret_decay(lg, i, j, blk):
    diff = _causal_diff(i, j, blk)
    return jnp.where(diff >= 0, jnp.exp(lg * jnp.maximum(diff, 0).astype(F32)), 0.0)


def _ret_log_gamma():
    return jnp.log1p(-(2.0 ** (-5.0 - jnp.arange(RET_HEADS, dtype=F32))))


def _ret_fwd(qr, kr, proj, t):
    blk = _seq_block(t)
    nb = t // blk
    v_off = (2 * RET_HEADS * RET_DK) // RET_DV

    def body(lg_ref, q_ref, k_ref, v_ref, o_ref, acc):
        h, i, j = pl.program_id(0), pl.program_id(1), pl.program_id(2)

        @pl.when(j == 0)
        def _():
            acc[...] = jnp.zeros_like(acc)

        @pl.when(j <= i)
        def _():
            s = lax.dot_general(q_ref[...].astype(BF16), k_ref[...].astype(BF16), _DIMS["nt"], preferred_element_type=F32)
            p = (s * _ret_decay(lg_ref[h], i, j, blk)).astype(BF16)
            acc[...] += jnp.dot(p, v_ref[...].astype(BF16), preferred_element_type=F32)

        @pl.when(j == nb - 1)
        def _():
            o_ref[...] = acc[...]

    return pl.pallas_call(
        body, grid=(RET_HEADS, nb, nb),
        in_specs=[pl.BlockSpec(memory_space=pltpu.SMEM),
                  pl.BlockSpec((blk, RET_DK), lambda h, i, j: (i, h)),
                  pl.BlockSpec((blk, RET_DK), lambda h, i, j: (jnp.minimum(j, i), h)),
                  pl.BlockSpec((blk, RET_DV), lambda h, i, j: (jnp.minimum(j, i), v_off + h))],
        out_specs=pl.BlockSpec((blk, RET_DV), lambda h, i, j: (i, h)),
        out_shape=jax.ShapeDtypeStruct((t, RET_HEADS * RET_DV), F32),
        scratch_shapes=[pltpu.VMEM((blk, RET_DV), F32)],
        compiler_params=pltpu.CompilerParams(dimension_semantics=("arbitrary",) * 3, vmem_limit_bytes=VMEM_LIMIT),
        name="ret_fwd")(_ret_log_gamma(), qr, kr, proj)


def _ret_bwd(qr, kr, proj, dr, t):
    blk = _seq_block(t)
    nb = t // blk
    v_off = (2 * RET_HEADS * RET_DK) // RET_DV

    def dq_body(lg_ref, q_ref, k_ref, v_ref, do_ref, dq_ref, acc):
        h, i, j = pl.program_id(0), pl.program_id(1), pl.program_id(2)

        @pl.when(j == 0)
        def _():
            acc[...] = jnp.zeros_like(acc)

        @pl.when(j <= i)
        def _():
            ds = lax.dot_general(do_ref[...].astype(BF16), v_ref[...].astype(BF16), _DIMS["nt"], preferred_element_type=F32)
            dsm = (ds * _ret_decay(lg_ref[h], i, j, blk)).astype(BF16)
            acc[...] += jnp.dot(dsm, k_ref[...].astype(BF16), preferred_element_type=F32)

        @pl.when(j == nb - 1)
        def _():
            dq_ref[...] = acc[...]

    dq = pl.pallas_call(
        dq_body, grid=(RET_HEADS, nb, nb),
        in_specs=[pl.BlockSpec(memory_space=pltpu.SMEM),
                  pl.BlockSpec((blk, RET_DK), lambda h, i, j: (i, h)),
                  pl.BlockSpec((blk, RET_DK), lambda h, i, j: (jnp.minimum(j, i), h)),
                  pl.BlockSpec((blk, RET_DV), lambda h, i, j: (jnp.minimum(j, i), v_off + h)),
                  pl.BlockSpec((blk, RET_DV), lambda h, i, j: (i, h))],
        out_specs=pl.BlockSpec((blk, RET_DK), lambda h, i, j: (i, h)),
        out_shape=jax.ShapeDtypeStruct((t, RET_HEADS * RET_DK), F32),
        scratch_shapes=[pltpu.VMEM((blk, RET_DK), F32)],
        compiler_params=pltpu.CompilerParams(dimension_semantics=("arbitrary",) * 3, vmem_limit_bytes=VMEM_LIMIT),
        name="ret_bwd_dq")(_ret_log_gamma(), qr, kr, proj, dr)

    def dkv_body(lg_ref, q_ref, k_ref, v_ref, do_ref, dk_ref, dv_ref, acc_k, acc_v):
        h, j, i = pl.program_id(0), pl.program_id(1), pl.program_id(2)

        @pl.when(i == 0)
        def _():
            acc_k[...] = jnp.zeros_like(acc_k)
            acc_v[...] = jnp.zeros_like(acc_v)

        @pl.when(i >= j)
        def _():
            q = q_ref[...].astype(BF16)
            do = do_ref[...].astype(BF16)
            decay = _ret_decay(lg_ref[h], i, j, blk)
            s = lax.dot_general(q, k_ref[...].astype(BF16), _DIMS["nt"], preferred_element_type=F32)
            acc_v[...] += lax.dot_general((s * decay).astype(BF16), do, _DIMS["tn"], preferred_element_type=F32)
            ds = lax.dot_general(do, v_ref[...].astype(BF16), _DIMS["nt"], preferred_element_type=F32)
            acc_k[...] += lax.dot_general((ds * decay).astype(BF16), q, _DIMS["tn"], preferred_element_type=F32)

        @pl.when(i == nb - 1)
        def _():
            dk_ref[...] = acc_k[...]
            dv_ref[...] = acc_v[...]

    dk, dv = pl.pallas_call(
        dkv_body, grid=(RET_HEADS, nb, nb),
        in_specs=[pl.BlockSpec(memory_space=pltpu.SMEM),
                  pl.BlockSpec((blk, RET_DK), lambda h, j, i: (jnp.maximum(i, j), h)),
                  pl.BlockSpec((blk, RET_DK), lambda h, j, i: (j, h)),
                  pl.BlockSpec((blk, RET_DV), lambda h, j, i: (j, v_off + h)),
                  pl.BlockSpec((blk, RET_DV), lambda h, j, i: (jnp.maximum(i, j), h))],
        out_specs=[pl.BlockSpec((blk, RET_DK), lambda h, j, i: (j, h)), pl.BlockSpec((blk, RET_DV), lambda h, j, i: (j, h))],
        out_shape=[jax.ShapeDtypeStruct((t, RET_HEADS * RET_DK), F32), jax.ShapeDtypeStruct((t, RET_HEADS * RET_DV), F32)],
        scratch_shapes=[pltpu.VMEM((blk, RET_DK), F32), pltpu.VMEM((blk, RET_DV), F32)],
        compiler_params=pltpu.CompilerParams(dimension_semantics=("arbitrary",) * 3, vmem_limit_bytes=VMEM_LIMIT),
        name="ret_bwd_dkv")(_ret_log_gamma(), qr, kr, proj, dr)
    return dq, dk, dv


def _cumsum(x, reverse, name):
    t = x.shape[0]
    blk = _seq_block(t)
    nb = t // blk

    def body(x_ref, o_ref, carry):
        @pl.when(pl.program_id(0) == 0)
        def _():
            carry[...] = jnp.zeros_like(carry)

        r = lax.broadcasted_iota(jnp.int32, (blk, blk), 0)
        c = lax.broadcasted_iota(jnp.int32, (blk, blk), 1)
        tri = ((r <= c) if reverse else (r >= c)).astype(F32)
        o_ref[...] = jnp.dot(tri, x_ref[...], precision=HIGHEST, preferred_element_type=F32) + carry[...]
        carry[...] = o_ref[0:1, :] if reverse else o_ref[blk - 1:blk, :]

    idx = (lambda i: (nb - 1 - i, 0)) if reverse else (lambda i: (i, 0))
    return pl.pallas_call(
        body, grid=(nb,), in_specs=[pl.BlockSpec((blk, LANES), idx)], out_specs=pl.BlockSpec((blk, LANES), idx),
        out_shape=jax.ShapeDtypeStruct((t, LANES), F32), scratch_shapes=[pltpu.VMEM((1, LANES), F32)],
        compiler_params=pltpu.CompilerParams(dimension_semantics=("arbitrary",), vmem_limit_bytes=VMEM_LIMIT),
        name=name)(x)


def _ssd_decay(a_ref, at_ref, hh, mask):
    return jnp.exp(jnp.where(mask, a_ref[:, hh:hh + 1] - at_ref[hh:hh + 1, :], -jnp.inf))


_B_OFF, _C_OFF = 1024 // SSM_N, 1024 // SSM_N + SSM_GROUPS


def _ssd_fwd(xbc_act, xdt, a_cum, a_cum_t, t):
    blk = _seq_block(t)
    nb = t // blk

    def body(c0, c1, b0, b1, x_ref, a_ref, at_ref, o_ref, acc):
        i, j = pl.program_id(0), pl.program_id(1)

        @pl.when(j == 0)
        def _():
            acc[...] = jnp.zeros_like(acc)

        @pl.when(j <= i)
        def _():
            mask = _causal_diff(i, j, blk) >= 0
            for g, (c_ref, b_ref) in enumerate(((c0, b0), (c1, b1))):
                cb = lax.dot_general(c_ref[...].astype(BF16), b_ref[...].astype(BF16), _DIMS["nt"], preferred_element_type=F32)
                for h in range(SSM_HG):
                    hh = g * SSM_HG + h
                    cols = slice(hh * SSM_P, (hh + 1) * SSM_P)
                    m = (cb * _ssd_decay(a_ref, at_ref, hh, mask)).astype(BF16)
                    acc[:, cols] += jnp.dot(m, x_ref[:, cols].astype(BF16), preferred_element_type=F32)

        @pl.when(j == nb - 1)
        def _():
            o_ref[...] = acc[...]

    row_i = lambda off: pl.BlockSpec((blk, SSM_N), lambda i, j, off=off: (i, off))
    row_j = lambda off: pl.BlockSpec((blk, SSM_N), lambda i, j, off=off: (jnp.minimum(j, i), off))
    return pl.pallas_call(
        body, grid=(nb, nb),
        in_specs=[row_i(_C_OFF), row_i(_C_OFF + 1), row_j(_B_OFF), row_j(_B_OFF + 1),
                  pl.BlockSpec((blk, SSM_HEADS * SSM_P), lambda i, j: (jnp.minimum(j, i), 0)),
                  pl.BlockSpec((blk, LANES), lambda i, j: (i, 0)),
                  pl.BlockSpec((SSM_HEADS, blk), lambda i, j: (0, jnp.minimum(j, i)))],
        out_specs=pl.BlockSpec((blk, SSM_HEADS * SSM_P), lambda i, j: (i, 0)),
        out_shape=jax.ShapeDtypeStruct((t, SSM_HEADS * SSM_P), F32),
        scratch_shapes=[pltpu.VMEM((blk, SSM_HEADS * SSM_P), F32)],
        compiler_params=pltpu.CompilerParams(dimension_semantics=("arbitrary",) * 2, vmem_limit_bytes=VMEM_LIMIT),
        name="ssd_fwd")(xbc_act, xbc_act, xbc_act, xbc_act, xdt, a_cum, a_cum_t)


def _ssd_bwd(xbc_act, xdt, a_cum, a_cum_t, dy, t):
    blk = _seq_block(t)
    nb = t // blk
    width = SSM_HEADS * SSM_P

    def q_body(c0, c1, b0, b1, x_ref, a_ref, at_ref, dy_ref, dc_ref, da_ref, acc_c, acc_a):
        i, j = pl.program_id(0), pl.program_id(1)

        @pl.when(j == 0)
        def _():
            acc_c[...] = jnp.zeros_like(acc_c)
            acc_a[...] = jnp.zeros_like(acc_a)

        @pl.when(j <= i)
        def _():
            mask = _causal_diff(i, j, blk) >= 0
            for g, (c_ref, b_ref) in enumerate(((c0, b0), (c1, b1))):
                bj = b_ref[...].astype(BF16)
                cb = lax.dot_general(c_ref[...].astype(BF16), bj, _DIMS["nt"], preferred_element_type=F32)
                dcb = jnp.zeros((blk, blk), F32)
                for h in range(SSM_HG):
                    hh = g * SSM_HG + h
                    cols = slice(hh * SSM_P, (hh + 1) * SSM_P)
                    decay = _ssd_decay(a_ref, at_ref, hh, mask)
                    dm = lax.dot_general(dy_ref[:, cols].astype(BF16), x_ref[:, cols].astype(BF16), _DIMS["nt"],
                                         preferred_element_type=F32) * decay
                    dcb = dcb + dm
                    acc_a[:, hh:hh + 1] += jnp.sum(dm * cb, axis=1, keepdims=True)
                acc_c[:, g * SSM_N:(g + 1) * SSM_N] += jnp.dot(dcb.astype(BF16), bj, preferred_element_type=F32)

        @pl.when(j == nb - 1)
        def _():
            dc_ref[...] = acc_c[...]
            da_ref[...] = acc_a[...]

    row_i = lambda off: pl.BlockSpec((blk, SSM_N), lambda i, j, off=off: (i, off))
    row_j = lambda off: pl.BlockSpec((blk, SSM_N), lambda i, j, off=off: (jnp.minimum(j, i), off))
    dc, da_q = pl.pallas_call(
        q_body, grid=(nb, nb),
        in_specs=[row_i(_C_OFF), row_i(_C_OFF + 1), row_j(_B_OFF), row_j(_B_OFF + 1),
                  pl.BlockSpec((blk, width), lambda i, j: (jnp.minimum(j, i), 0)),
                  pl.BlockSpec((blk, LANES), lambda i, j: (i, 0)),
                  pl.BlockSpec((SSM_HEADS, blk), lambda i, j: (0, jnp.minimum(j, i))),
                  pl.BlockSpec((blk, width), lambda i, j: (i, 0))],
        out_specs=[pl.BlockSpec((blk, SSM_GROUPS * SSM_N), lambda i, j: (i, 0)), pl.BlockSpec((blk, LANES), lambda i, j: (i, 0))],
        out_shape=[jax.ShapeDtypeStruct((t, SSM_GROUPS * SSM_N), F32), jax.ShapeDtypeStruct((t, LANES), F32)],
        scratch_shapes=[pltpu.VMEM((blk, SSM_GROUPS * SSM_N), F32), pltpu.VMEM((blk, LANES), F32)],
        compiler_params=pltpu.CompilerParams(dimension_semantics=("arbitrary",) * 2, vmem_limit_bytes=VMEM_LIMIT),
        name="ssd_bwd_q")(xbc_act, xbc_act, xbc_act, xbc_act, xdt, a_cum, a_cum_t, dy)

    def k_body(c0, c1, b0, b1, x_ref, a_ref, at_ref, dy_ref, db_ref, dx_ref, dat_ref, acc_b, acc_x, acc_a):
        j, i = pl.program_id(0), pl.program_id(1)

        @pl.when(i == 0)
        def _():
            acc_b[...] = jnp.zeros_like(acc_b)
            acc_x[...] = jnp.zeros_like(acc_x)
            acc_a[...] = jnp.zeros_like(acc_a)

        @pl.when(i >= j)
        def _():
            mask = _causal_diff(i, j, blk) >= 0
            for g, (c_ref, b_ref) in enumerate(((c0, b0), (c1, b1))):
                ci = c_ref[...].astype(BF16)
                cb = lax.dot_general(ci, b_ref[...].astype(BF16), _DIMS["nt"], preferred_element_type=F32)
                dcb = jnp.zeros((blk, blk), F32)
                for h in range(SSM_HG):
                    hh = g * SSM_HG + h
                    cols = slice(hh * SSM_P, (hh + 1) * SSM_P)
                    decay = _ssd_decay(a_ref, at_ref, hh, mask)
                    dyh = dy_ref[:, cols].astype(BF16)
                    acc_x[:, cols] += lax.dot_general((cb * decay).astype(BF16), dyh, _DIMS["tn"], preferred_element_type=F32)
                    dm = lax.dot_general(dyh, x_ref[:, cols].astype(BF16), _DIMS["nt"], preferred_element_type=F32) * decay
                    dcb = dcb + dm
                    acc_a[hh:hh + 1, :] += jnp.sum(dm * cb, axis=0, keepdims=True)
                acc_b[:, g * SSM_N:(g + 1) * SSM_N] += lax.dot_general(dcb.astype(BF16), ci, _DIMS["tn"], preferred_element_type=F32)

        @pl.when(i == nb - 1)
        def _():
            db_ref[...] = acc_b[...]
            dx_ref[...] = acc_x[...]
            dat_ref[...] = acc_a[...]

    rowk_i = lambda off: pl.BlockSpec((blk, SSM_N), lambda j, i, off=off: (jnp.maximum(i, j), off))
    rowk_j = lambda off: pl.BlockSpec((blk, SSM_N), lambda j, i, off=off: (j, off))
    db, dx, da_k_t = pl.pallas_call(
        k_body, grid=(nb, nb),
        in_specs=[rowk_i(_C_OFF), rowk_i(_C_OFF + 1), rowk_j(_B_OFF), rowk_j(_B_OFF + 1),
                  pl.BlockSpec((blk, width), lambda j, i: (j, 0)),
                  pl.BlockSpec((blk, LANES), lambda j, i: (jnp.maximum(i, j), 0)),
                  pl.BlockSpec((SSM_HEADS, blk), lambda j, i: (0, j)),
                  pl.BlockSpec((blk, width), lambda j, i: (jnp.maximum(i, j), 0))],
        out_specs=[pl.BlockSpec((blk, SSM_GROUPS * SSM_N), lambda j, i: (j, 0)), pl.BlockSpec((blk, width), lambda j, i: (j, 0)),
                   pl.BlockSpec((SSM_HEADS, blk), lambda j, i: (0, j))],
        out_shape=[jax.ShapeDtypeStruct((t, SSM_GROUPS * SSM_N), F32), jax.ShapeDtypeStruct((t, width), F32),
                   jax.ShapeDtypeStruct((SSM_HEADS, t), F32)],
        scratch_shapes=[pltpu.VMEM((blk, SSM_GROUPS * SSM_N), F32), pltpu.VMEM((blk, width), F32), pltpu.VMEM((SSM_HEADS, blk), F32)],
        compiler_params=pltpu.CompilerParams(dimension_semantics=("arbitrary",) * 2, vmem_limit_bytes=VMEM_LIMIT),
        name="ssd_bwd_k")(xbc_act, xbc_act, xbc_act, xbc_act, xdt, a_cum, a_cum_t, dy)
    return dc, da_q, db, dx, da_k_t


def _s5_scan(bu, a_re, a_im, reverse, x_prev=None):
    t = bu.shape[0]
    cb = LANES
    ncb = S5_COLS // cb
    pad = t // 2
    chunk = min(512, t)
    base = 0 if reverse else pad
    zero0 = t if reverse else 0

    def body(*refs):
        if reverse:
            br_ref, bi_ref, ar_ref, ai_ref, xr_ref, xi_ref, or_ref, oi_ref, dar_ref, dai_ref, sr, si = refs
        else:
            br_ref, bi_ref, ar_ref, ai_ref, or_ref, oi_ref, sr, si = refs
        sr[pl.ds(zero0, pad), :] = jnp.zeros((pad, cb), F32)
        si[pl.ds(zero0, pad), :] = jnp.zeros((pad, cb), F32)
        sr[pl.ds(base, t), :] = br_ref[...]
        si[pl.ds(base, t), :] = bi_ref[...]
        pr = ar_ref[...]
        pi = -ai_ref[...] if reverse else ai_ref[...]
        s = 1
        while s < t:
            shift = s if reverse else -s

            def step(t0, pr=pr, pi=pi, shift=shift):
                cr, ci = sr[pl.ds(base + t0, chunk), :], si[pl.ds(base + t0, chunk), :]
                qr, qi = sr[pl.ds(base + t0 + shift, chunk), :], si[pl.ds(base + t0 + shift, chunk), :]
                sr[pl.ds(base + t0, chunk), :] = cr + pr * qr - pi * qi
                si[pl.ds(base + t0, chunk), :] = ci + pr * qi + pi * qr

            order = range(0, t, chunk) if reverse else range(t - chunk, -1, -chunk)
            for t0 in order:
                step(t0)
            pr, pi = pr * pr - pi * pi, 2.0 * pr * pi
            s *= 2
        or_ref[...] = sr[pl.ds(base, t), :]
        oi_ref[...] = si[pl.ds(base, t), :]
        if reverse:
            dar = jnp.zeros((1, cb), F32)
            dai = jnp.zeros((1, cb), F32)
            for t0 in range(0, t, chunk):
                gr, gi = sr[pl.ds(t0 + 1, chunk), :], si[pl.ds(t0 + 1, chunk), :]
                xr, xi = xr_ref[pl.ds(t0, chunk), :], xi_ref[pl.ds(t0, chunk), :]
                dar = dar + jnp.sum(gr * xr + gi * xi, axis=0, keepdims=True)
                dai = dai + jnp.sum(gi * xr - gr * xi, axis=0, keepdims=True)
            dar_ref[...] = dar
            dai_ref[...] = dai

    re_spec = pl.BlockSpec((t, cb), lambda j: (0, j))
    im_spec = pl.BlockSpec((t, cb), lambda j: (0, j + ncb))
    a_spec = pl.BlockSpec((1, cb), lambda j: (0, j))
    ins, in_specs = [bu, bu, a_re, a_im], [re_spec, im_spec, a_spec, a_spec]
    out_shape = [jax.ShapeDtypeStruct((t, S5_COLS), F32)] * 2
    out_specs = [re_spec, re_spec]
    if reverse:
        ins += [x_prev, x_prev]
        in_specs += [re_spec, im_spec]
        out_shape += [jax.ShapeDtypeStruct((1, S5_COLS), F32)] * 2
        out_specs += [a_spec, a_spec]
    return pl.pallas_call(
        body, grid=(ncb,), in_specs=in_specs, out_specs=out_specs, out_shape=out_shape,
        scratch_shapes=[pltpu.VMEM((t + pad, cb), F32), pltpu.VMEM((t + pad, cb), F32)],
        compiler_params=pltpu.CompilerParams(dimension_semantics=("arbitrary",), vmem_limit_bytes=VMEM_LIMIT),
        name="s5_scan_bwd" if reverse else "s5_scan_fwd")(*ins)


CONF_DIM = 512
CONF_K = 31
CONF_PAD = 32


def _conf_conv(proj_o, w, b, dc1=None):
    t = proj_o.shape[0]
    cb = LANES
    ncb = CONF_DIM // cb
    chunk = min(512, t)
    chunks = range(0, t, chunk)
    bwd = dc1 is not None

    def body(*refs):
        if bwd:
            ca_ref, cg_ref, w_ref, b_ref, dy_ref, dca_ref, dcg_ref, dw_ref, db_ref, xs, dys = refs
        else:
            ca_ref, cg_ref, w_ref, b_ref, o_ref, xs = refs
        xs[pl.ds(0, CONF_PAD), :] = jnp.zeros((CONF_PAD, cb), F32)
        for t0 in chunks:
            rows = pl.ds(t0, chunk)
            xs[pl.ds(CONF_PAD + t0, chunk), :] = ca_ref[rows, :] * jax.nn.sigmoid(cg_ref[rows, :])
        if not bwd:
            for t0 in chunks:
                acc = jnp.broadcast_to(b_ref[...], (chunk, cb))
                for k in range(CONF_K):
                    acc = acc + w_ref[k] * xs[pl.ds(CONF_PAD + t0 - (CONF_K - 1 - k), chunk), :]
                o_ref[pl.ds(t0, chunk), :] = acc
            return
        dys[pl.ds(t, CONF_PAD), :] = jnp.zeros((CONF_PAD, cb), F32)
        db = jnp.zeros((1, cb), F32)
        for t0 in chunks:
            dys[pl.ds(t0, chunk), :] = dy_ref[pl.ds(t0, chunk), :]
            db = db + jnp.sum(dy_ref[pl.ds(t0, chunk), :], axis=0, keepdims=True)
        db_ref[...] = db
        for t0 in chunks:
            rows = pl.ds(t0, chunk)
            acc = jnp.zeros((chunk, cb), F32)
            for k in range(CONF_K):
                acc = acc + w_ref[k] * dys[pl.ds(t0 + (CONF_K - 1 - k), chunk), :]
            sig = jax.nn.sigmoid(cg_ref[rows, :])
            dca_ref[rows, :] = acc * sig
            dcg_ref[rows, :] = acc * ca_ref[rows, :] * sig * (1.0 - sig)
        for k in range(CONF_K):
            dwk = jnp.zeros((1, cb), F32)
            for t0 in chunks:
                window = xs[pl.ds(CONF_PAD + t0 - (CONF_K - 1 - k), chunk), :]
                dwk = dwk + jnp.sum(dy_ref[pl.ds(t0, chunk), :] * window, axis=0, keepdims=True)
            dw_ref[k] = dwk

    col = lambda off: pl.BlockSpec((t, cb), lambda j, off=off: (0, j + off))
    w_spec = pl.BlockSpec((CONF_K, 1, cb), lambda j: (0, 0, j))
    b_spec = pl.BlockSpec((1, cb), lambda j: (0, j))
    seq = jax.ShapeDtypeStruct((t, CONF_DIM), F32)
    ins, in_specs = [proj_o, proj_o, w, b], [col(0), col(ncb), w_spec, b_spec]
    scratch = [pltpu.VMEM((CONF_PAD + t, cb), F32)]
    if bwd:
        ins, in_specs = ins + [dc1], in_specs + [col(0)]
        out_shape, out_specs = [seq, seq, jax.ShapeDtypeStruct(w.shape, F32), jax.ShapeDtypeStruct(b.shape, F32)], [col(0), col(0), w_spec, b_spec]
        scratch = scratch + [pltpu.VMEM((t + CONF_PAD, cb), F32)]
    else:
        out_shape, out_specs = seq, col(0)
    return pl.pallas_call(
        body, grid=(ncb,), in_specs=in_specs, out_specs=out_specs, out_shape=out_shape, scratch_shapes=scratch,
        compiler_params=pltpu.CompilerParams(dimension_semantics=("arbitrary",), vmem_limit_bytes=VMEM_LIMIT),
        name="conf_conv_bwd" if bwd else "conf_conv")(*ins)


def _exchange(srcs, broadcast, name):
    n = len(srcs)

    def body(*refs):
        src_refs, out_refs = refs[:n], refs[n:2 * n]
        send_sems, recv_sems, local_sems = refs[2 * n:]
        x, y, c = lax.axis_index("x"), lax.axis_index("y"), lax.axis_index("c")
        me = 4 * x + 2 * y + c

        def piece(i, idx):
            return src_refs[i] if broadcast[i] else src_refs[i].at[idx]

        local = [pltpu.make_async_copy(piece(i, me), out_refs[i].at[me], local_sems.at[i]) for i in range(n)]
        for cp in local:
            cp.start()
        copies = []
        for k in range(1, N_DEV):
            px, py, pc = x ^ ((k >> 2) & 1), y ^ ((k >> 1) & 1), c ^ (k & 1)
            peer = 4 * px + 2 * py + pc
            for i in range(n):
                sem = (k - 1) * n + i
                send = pltpu.make_async_remote_copy(
                    src_ref=piece(i, peer), dst_ref=out_refs[i].at[me], send_sem=send_sems.at[sem], recv_sem=recv_sems.at[sem],
                    device_id=(px, py, pc), device_id_type=pl.DeviceIdType.MESH)
                send.start()
                recv = pltpu.make_async_remote_copy(
                    src_ref=piece(i, peer), dst_ref=out_refs[i].at[peer], send_sem=send_sems.at[sem], recv_sem=recv_sems.at[sem],
                    device_id=(px, py, pc), device_id_type=pl.DeviceIdType.MESH)
                copies.append((send, recv))
        for _, recv in copies:
            recv.wait_recv()
        for send, _ in copies:
            send.wait_send()
        for cp in local:
            cp.wait()

    blocks = [s.shape if b else s.shape[1:] for s, b in zip(srcs, broadcast)]
    n_sems = (N_DEV - 1) * n
    return pl.pallas_call(
        body, out_shape=[jax.ShapeDtypeStruct((N_DEV,) + tuple(blk), s.dtype) for blk, s in zip(blocks, srcs)],
        in_specs=[pl.BlockSpec(memory_space=pl.ANY)] * n, out_specs=[pl.BlockSpec(memory_space=pl.ANY)] * n,
        scratch_shapes=[pltpu.SemaphoreType.DMA((n_sems,)), pltpu.SemaphoreType.DMA((n_sems,)), pltpu.SemaphoreType.DMA((n,))],
        compiler_params=pltpu.CompilerParams(has_side_effects=True),
        name=name)(*srcs)


def _row_tile(r, pref=256):
    if r <= pref:
        return r
    t = pref // 16 * 16
    while r % t:
        t -= 16
    return t


def _join_cols(g, width, name):
    _, rows, ws = g.shape
    tr = _row_tile(rows)
    tail = width - N_DEV * ws

    def body(g_ref, o_ref):
        for d in range(N_DEV):
            o_ref[:, pl.ds(d * ws, ws)] = g_ref[d]
        if tail:
            o_ref[:, pl.ds(N_DEV * ws, tail)] = jnp.zeros((tr, tail), g.dtype)

    return pl.pallas_call(
        body, grid=(rows // tr,), in_specs=[pl.BlockSpec((N_DEV, tr, ws), lambda i: (0, i, 0))],
        out_specs=pl.BlockSpec((tr, width), lambda i: (i, 0)), out_shape=jax.ShapeDtypeStruct((rows, width), g.dtype),
        compiler_params=pltpu.CompilerParams(dimension_semantics=("parallel",), vmem_limit_bytes=VMEM_LIMIT), name=name)(g)


def _split_cols(full, ws, dtype, name):
    rows, width = full.shape
    tr = _row_tile(rows)

    def body(x_ref, o_ref):
        for d in range(N_DEV):
            o_ref[d] = x_ref[:, pl.ds(d * ws, ws)].astype(dtype)

    return pl.pallas_call(
        body, grid=(rows // tr,), in_specs=[pl.BlockSpec((tr, width), lambda i: (i, 0))],
        out_specs=pl.BlockSpec((N_DEV, tr, ws), lambda i: (0, i, 0)), out_shape=jax.ShapeDtypeStruct((N_DEV, rows, ws), dtype),
        compiler_params=pltpu.CompilerParams(dimension_semantics=("parallel",), vmem_limit_bytes=VMEM_LIMIT), name=name)(full)


def _adamw(parts, w, m, v, name):
    r, c = w.shape
    tr = _row_tile(r)

    def body(p_ref, w_ref, m_ref, v_ref, g_ref, d_ref, nm_ref, nv_ref):
        g = p_ref[0].astype(F32)
        for s in range(1, N_DEV):
            g = g + p_ref[s].astype(F32)
        nm = ADAM_B1 * m_ref[...] + (1.0 - ADAM_B1) * g
        nv = ADAM_B2 * v_ref[...] + (1.0 - ADAM_B2) * (g * g)
        m_hat = nm / (1.0 - ADAM_B1 ** ADAM_STEP)
        v_hat = nv / (1.0 - ADAM_B2 ** ADAM_STEP)
        g_ref[...] = g
        nm_ref[...] = nm
        nv_ref[...] = nv
        d_ref[...] = -ADAM_LR * (m_hat / (jnp.sqrt(v_hat) + ADAM_EPS) + ADAM_WD * w_ref[...])

    spec = pl.BlockSpec((tr, c), lambda i: (i, 0))
    return pl.pallas_call(
        body, grid=(r // tr,), in_specs=[pl.BlockSpec((N_DEV, tr, c), lambda i: (0, i, 0)), spec, spec, spec],
        out_specs=[spec] * 4, out_shape=[jax.ShapeDtypeStruct((r, c), F32)] * 4,
        compiler_params=pltpu.CompilerParams(dimension_semantics=("parallel",), vmem_limit_bytes=VMEM_LIMIT),
        name=name)(parts, w, m, v)


def _pack_rows(n_elems, mult):
    rows = -(-n_elems // PACK_COLS)
    return -(-rows // mult) * mult


def _pack(arrays, dtype, mult, lead=()):
    flat = jnp.concatenate([a.astype(dtype).reshape(lead + (-1,)) for a in arrays], axis=-1)
    rows = _pack_rows(flat.shape[-1], mult)
    flat = jnp.pad(flat, [(0, 0)] * len(lead) + [(0, rows * PACK_COLS - flat.shape[-1])])
    return flat.reshape(lead + (rows, PACK_COLS))


def _unpack(buf, shapes, lead=()):
    flat = buf.reshape(lead + (-1,))
    out, off = [], 0
    for s in shapes:
        n = math.prod(s)
        out.append(flat[..., off:off + n].reshape(lead + tuple(s)))
        off += n
    return out


def _join_shards(piece, axis):
    moved = jnp.moveaxis(piece, 0, axis)
    shape = moved.shape
    return moved.reshape(shape[:axis] + (shape[axis] * shape[axis + 1],) + shape[axis + 2:])


def _split_shards(full, axis):
    shape = full.shape
    return jnp.moveaxis(full.reshape(shape[:axis] + (N_DEV, shape[axis] // N_DEV) + shape[axis + 1:]), axis, 0)


def _block_diag(blocks):
    g, r, c = blocks.shape
    eye = jnp.eye(g, dtype=blocks.dtype)
    return (blocks[:, :, None, :] * eye[:, None, :, None]).reshape(g * r, g * c)


def _diag_blocks(mat, g):
    r, c = mat.shape[0] // g, mat.shape[1] // g
    eye = jnp.eye(g, dtype=mat.dtype)
    return jnp.sum(mat.reshape(g, r, g, c) * eye[:, None, :, None], axis=2)


def _pad_lanes(a):
    a = a.reshape(1, -1)
    return jnp.pad(a, ((0, 0), (0, LANES - a.shape[1])))


def _head_expand():
    h = jnp.arange(LANES)[:, None]
    ch = jnp.arange(SSM_HEADS * SSM_P)[None, :] // SSM_P
    return (h == ch).astype(F32)


def _rotary_tables(t):
    inv = 10000.0 ** (-jnp.arange(0, RET_DK, 2, dtype=F32) / RET_DK)
    ang = jnp.arange(t).astype(F32)[:, None] * inv[None, :]
    cos, sin = jnp.cos(ang), jnp.sin(ang)
    return jnp.concatenate([cos, cos], axis=1), jnp.concatenate([-sin, sin], axis=1)


def _rms_fwd(x, g, name):
    return _rows(_f_rms, x.shape[0], [(x, D_MODEL, 0)], [g], [D_MODEL], [], name)[0]


def _rms_bwd(x, g, dh, dres, name):
    def fn(x_, dh_, dres_, g_):
        _, vjp = jax.vjp(lambda a, b: _rms(a, b), x_, g_)
        dx, dg = vjp(dh_)
        return dx + dres_, dg
    return _rows(fn, x.shape[0], [(x, D_MODEL, 0), (dh, D_MODEL, 0), (dres, D_MODEL, 0)], [g], [D_MODEL], [(1, D_MODEL)], name)


def _ffn_fwd(x, norm_g, w_up, dw_w, dw_b, w_down, tag):
    t = x.shape[0]
    nbk = D_FF // LANES
    h = _rms_fwd(x, norm_g, "ffn_norm_" + tag)
    up = _mm(h, w_up, "nn", "ffn_up_" + tag)
    mid = _cols(_f_ffnmid, nbk, [(up, 0), (up, nbk), (dw_w, 0), (dw_w, nbk), (dw_b, 0), (dw_b, nbk)], [(t,)], "ffn_mid_" + tag)[0]
    out = _mm(mid, w_down, "nn", "ffn_down_" + tag, res=x)
    return out, (h, up, mid)


def _ffn_bwd(x, norm_g, w_up, dw_w, dw_b, w_down, saved, dout, tag):
    t = x.shape[0]
    nbk = D_FF // LANES
    h, up, mid = saved
    d_w_down = _mm(mid, dout, "tn", "ffn_down_dw_" + tag, out_dtype=BF16)
    dmid = _mm(dout, w_down, "nt", "ffn_down_dx_" + tag)
    dgin, duin, dwg, dwu, dbg, dbu = _cols(
        _grad_fn(_f_ffnmid, 6, 6), nbk,
        [(up, 0), (up, nbk), (dw_w, 0), (dw_w, nbk), (dw_b, 0), (dw_b, nbk), (dmid, 0)],
        [(t,), (t,), (3, 1), (3, 1), (1,), (1,)], "ffn_mid_bwd_" + tag)
    dup = jnp.concatenate([dgin, duin], axis=1)
    d_w_up = _mm(h, dup, "tn", "ffn_up_dw_" + tag)
    dh = _mm(dup, w_up, "nt", "ffn_up_dx_" + tag)
    dx, dnorm = _rms_bwd(x, norm_g, dh, dout, "ffn_norm_bwd_" + tag)
    return dx, dict(norm=dnorm, w_up=d_w_up, dw_w=jnp.concatenate([dwg, dwu], axis=2)[:, 0], dw_b=jnp.concatenate([dbg, dbu], axis=1),
                    w_down=d_w_down)


def _local_step(x, tgt, w):
    t = x.shape[0]
    grads = {}
    expand = _head_expand()
    cosf, sins = _rotary_tables(t)
    mix_g = [w['mix_norm'][i:i + 1] for i in range(2)]
    ffn_g = [w['ffn_norm'][i:i + 1] for i in range(2)]
    ffn_dw_w = [w['ffn_dw_w'][i][:, None, :] for i in range(2)]
    ffn_dw_b = [w['ffn_dw_b'][i:i + 1] for i in range(2)]

    w_e = w['e_w_in']
    conv_w = w['e_conv_w'][0][:, None, :]
    conv_b = w['e_conv_b']
    dt_bias, a_log, d_skip = _pad_lanes(w['e_dt_bias']), _pad_lanes(w['e_a_log']), _pad_lanes(w['e_d'])
    h0 = _rms_fwd(x, mix_g[0], "mix_norm_0")
    proj = _mm(h0, w_e, "nn", "e_in")
    qr, kr = _rows(_f_retpre, t, [(proj, 512, 0), (proj, 512, 1), (cosf, LANES, 0), (sins, LANES, 0)], [], [512, 512], [], "ret_pre")
    r = _ret_fwd(qr, kr, proj, t)
    y_ret = _rows(_f_retpost, t, [(r, 1024, 0), (proj, 1024, 2)], [], [1024], [], "ret_post")[0]
    xbc_act = _cols(_f_ssdconv, 12, [(proj, 32), (conv_w, 0), (conv_b, 0)], [(t,)], "ssd_conv")[0]
    xdt, da = _rows(_f_ssdpre, t, [(xbc_act, 1024, 0), (proj, LANES, 44)], [dt_bias, a_log, expand], [1024, LANES], [], "ssd_pre")
    a_cum = _cumsum(da, False, "ssd_cumsum")
    a_cum_t = a_cum[:, :SSM_HEADS].T
    yc = _ssd_fwd(xbc_act, xdt, a_cum, a_cum_t, t)
    y_ssm = _rows(_f_ssdpost, t, [(yc, 1024, 0), (xbc_act, 1024, 0), (proj, 1024, 3)], [d_skip, w['e_ssm_norm'], expand],
                  [1024], [], "ssd_post")[0]
    mix_e = jnp.concatenate([y_ret, y_ssm], axis=1)
    x1 = _mm(mix_e, w['e_w_out'], "nn", "e_out", res=x)
    x2, ffn0 = _ffn_fwd(x1, ffn_g[0], w['ffn_w_up'][0], ffn_dw_w[0], ffn_dw_b[0], w['ffn_w_down'][0], "0")

    lr, li = w['o_a_re'][0], w['o_a_im'][0]
    ls = w['o_log_step'].reshape(S5_GROUPS, 1)
    b_re3, b_im3 = jnp.transpose(w['o_b_re'][0], (2, 0, 1)), jnp.transpose(w['o_b_im'][0], (2, 0, 1))
    par_ins = [lr, li, ls, b_re3, b_im3]
    whole = lambda a: (a, a.shape, (lambda i, n=a.ndim: (0,) * n))
    par_shapes = [(S5_GROUPS, S5_STATE)] * 2 + [(S5_GROUP, S5_GROUPS, S5_STATE)] * 2
    ab_re, ab_im, bb_re, bb_im = _call(_f_s5par, (1,), [whole(a) for a in par_ins],
                                       [(s, F32, s, (lambda i, n=len(s): (0,) * n), False) for s in par_shapes], "s5_params")
    w_b = jnp.concatenate([_block_diag(jnp.transpose(bb_re, (1, 0, 2))), _block_diag(jnp.transpose(bb_im, (1, 0, 2)))], axis=1)
    w_c = jnp.concatenate([_block_diag(jnp.transpose(w['o_c_re'][0], (0, 2, 1))),
                           -_block_diag(jnp.transpose(w['o_c_im'][0], (0, 2, 1)))], axis=0)
    a_re, a_im = ab_re.reshape(1, S5_COLS), ab_im.reshape(1, S5_COLS)
    dw_w = w['o_dw_w'][0][:, None, :]
    glu_w = w['o_glu_w'].astype(F32)

    h1 = _rms_fwd(x2, mix_g[1], "mix_norm_1")
    proj_o = _mm(h1, w['o_w_in'], "nn", "o_in")
    c1 = _conf_conv(proj_o, dw_w, w['o_dw_b'])
    c2 = _rows(_f_confb, t, [(c1, 512, 0)], [w['o_ln_g'], w['o_ln_b']], [512], [], "conf_norm")[0]
    u = proj_o[:, 1024:]
    bu = _mm(u, w_b, "nn", "s5_bu")
    xs_re, xs_im = _s5_scan(bu, a_re, a_im, False)
    xs_cat = jnp.concatenate([xs_re, xs_im], axis=1)
    y_s5 = _mm(xs_cat, w_c, "nn", "s5_cx")
    s_out = _rows(_f_s5post, t, [(y_s5, 512, 0), (proj_o, 512, 2)], [w['o_d'], glu_w], [512], [], "s5_post")[0]
    mix_o = jnp.concatenate([c2, s_out], axis=1)
    x3 = _mm(mix_o, w['o_w_out'], "nn", "o_out", res=x2)
    x4, ffn1 = _ffn_fwd(x3, ffn_g[1], w['ffn_w_up'][1], ffn_dw_w[1], ffn_dw_b[1], w['ffn_w_down'][1], "1")

    dx4, loss_blk, d_final = _rows(_loss_step, t, [(x4, D_MODEL, 0), (tgt, D_MODEL, 0)], [w['final_norm'].reshape(1, D_MODEL)],
                                   [D_MODEL], [(8, LANES), (1, D_MODEL)], "loss_head")
    loss = loss_blk[0, 0]
    grads['final_norm'] = d_final.reshape(D_MODEL)

    dx3, g1 = _ffn_bwd(x3, ffn_g[1], w['ffn_w_up'][1], ffn_dw_w[1], ffn_dw_b[1], w['ffn_w_down'][1], ffn1, dx4, "1")
    grads['o_w_out'] = _mm(mix_o, dx3, "tn", "o_out_dw", out_dtype=BF16)
    dmix_o = _mm(dx3, w['o_w_out'], "nt", "o_out_dx")
    dc1, d_ln_g, d_ln_b = _conf_norm_bwd(c1, w['o_ln_g'], w['o_ln_b'], dmix_o, t)
    dca, dcg, d_dw_w, d_dw_b = _conf_conv(proj_o, dw_w, w['o_dw_b'], dc1)
    dyc, du_skip, d_od, d_glu = _s5_post_bwd(y_s5, proj_o, w['o_d'], glu_w, dmix_o, t)
    grads['o_glu_w'] = d_glu
    d_w_c = _mm(xs_cat, dyc, "tn", "s5_cx_dw")
    dxs = _mm(dyc, w_c, "nt", "s5_cx_dx")
    g_re, g_im, d_are, d_aim = _s5_scan(dxs, a_re, a_im, True, xs_cat)
    g_cat = jnp.concatenate([g_re, g_im], axis=1)
    d_w_b = _mm(u, g_cat, "tn", "s5_bu_dw")
    du = _mm(g_cat, w_b, "nt", "s5_bu_dx", res=du_skip)
    d_bb_re = jnp.transpose(_diag_blocks(d_w_b[:, :S5_COLS], S5_GROUPS), (1, 0, 2))
    d_bb_im = jnp.transpose(_diag_blocks(d_w_b[:, S5_COLS:], S5_GROUPS), (1, 0, 2))
    par_cts = [d_are.reshape(S5_GROUPS, S5_STATE), d_aim.reshape(S5_GROUPS, S5_STATE), d_bb_re, d_bb_im]
    in_shapes = [a.shape for a in par_ins]
    d_lr, d_li, d_ls, d_br3, d_bi3 = _call(_grad_fn(_f_s5par, 5, 5), (1,), [whole(a) for a in par_ins + par_cts],
                                           [(s, F32, s, (lambda i, n=len(s): (0,) * n), False) for s in in_shapes], "s5_params_bwd")
    grads['o_a_re'], grads['o_a_im'], grads['o_log_step'] = d_lr[None], d_li[None], d_ls.reshape(1, S5_GROUPS)
    grads['o_b_re'], grads['o_b_im'] = jnp.transpose(d_br3, (1, 2, 0))[None], jnp.transpose(d_bi3, (1, 2, 0))[None]
    grads['o_c_re'] = jnp.transpose(_diag_blocks(d_w_c[:S5_COLS], S5_GROUPS), (0, 2, 1))[None]
    grads['o_c_im'] = -jnp.transpose(_diag_blocks(d_w_c[S5_COLS:], S5_GROUPS), (0, 2, 1))[None]
    grads['o_d'], grads['o_ln_g'], grads['o_ln_b'] = d_od, d_ln_g, d_ln_b
    grads['o_dw_w'], grads['o_dw_b'] = d_dw_w[:, 0][None], d_dw_b
    dproj_o = jnp.concatenate([dca, dcg, du], axis=1)
    grads['o_w_in'] = _mm(h1, dproj_o, "tn", "o_in_dw")
    dh1 = _mm(dproj_o, w['o_w_in'], "nt", "o_in_dx")
    dx2, d_mix1 = _rms_bwd(x2, mix_g[1], dh1, dx3, "mix_norm_bwd_1")

    dx1, g0 = _ffn_bwd(x1, ffn_g[0], w['ffn_w_up'][0], ffn_dw_w[0], ffn_dw_b[0], w['ffn_w_down'][0], ffn0, dx2, "0")
    grads['e_w_out'] = _mm(mix_e, dx1, "tn", "e_out_dw", out_dtype=BF16)
    dmix_e = _mm(dx1, w['e_w_out'], "nt", "e_out_dx")
    dr, dg = _rows(_grad_fn(_f_retpost, 2, 2), t, [(r, 1024, 0), (proj, 1024, 2), (dmix_e, 1024, 0)], [], [1024, 1024], [], "ret_post_bwd")
    dqr, dkr, dv = _ret_bwd(qr, kr, proj, dr, t)
    dq, dk = _ret_pre_bwd(proj, cosf, sins, dqr, dkr, t)
    dyc0, dxs1, dz, d_dskip, d_ssm_norm = _ssd_post_bwd(yc, xbc_act, proj, d_skip, w['e_ssm_norm'], expand, dmix_e, t)
    dcm, da_q, dbm, dxdt, da_k_t = _ssd_bwd(xbc_act, xdt, a_cum, a_cum_t, dyc0, t)
    d_a_cum = da_q - jnp.pad(da_k_t.T, ((0, 0), (0, LANES - SSM_HEADS)))
    dda = _cumsum(d_a_cum, True, "ssd_cumsum_bwd")
    dxs, ddtr, d_dt_bias, d_a_log = _ssd_pre_bwd(xbc_act, proj, dt_bias, a_log, expand, dxdt, dda, dxs1, t)
    dxbc_act = jnp.concatenate([dxs, dbm, dcm], axis=1)
    dxbc, d_conv_w, d_conv_b = _cols(_grad_fn(_f_ssdconv, 3, 3), 12, [(proj, 32), (conv_w, 0), (conv_b, 0), (dxbc_act, 0)],
                                     [(t,), (4, 1), (1,)], "ssd_conv_bwd")
    dproj = jnp.concatenate([dq, dk, dv, dg, dz, dxbc, ddtr], axis=1)
    grads['e_w_in'] = _mm(h0, dproj, "tn", "e_in_dw")
    dh0 = _mm(dproj, w_e, "nt", "e_in_dx")
    dx0, d_mix0 = _rms_bwd(x, mix_g[0], dh0, dx1, "mix_norm_bwd_0")

    grads['mix_norm'] = jnp.concatenate([d_mix0, d_mix1], axis=0)
    grads['e_conv_w'], grads['e_conv_b'] = d_conv_w[:, 0][None], d_conv_b
    grads['e_dt_bias'], grads['e_a_log'], grads['e_d'] = d_dt_bias[:, :SSM_HEADS], d_a_log[:, :SSM_HEADS], d_dskip[:, :SSM_HEADS]
    grads['e_ssm_norm'] = d_ssm_norm
    grads['ffn_norm'] = jnp.concatenate([g0['norm'], g1['norm']], axis=0)
    grads['ffn_w_up'], grads['ffn_w_down'] = [g0['w_up'], g1['w_up']], [g0['w_down'], g1['w_down']]
    grads['ffn_dw_w'] = jnp.stack([g0['dw_w'], g1['dw_w']], axis=0)
    grads['ffn_dw_b'] = jnp.concatenate([g0['dw_b'], g1['dw_b']], axis=0)
    return loss, dx0, grads


def _conf_norm_bwd(c1, ln_g, ln_b, dmix_o, t):
    def fn(c1_, dy_, g_, b_):
        _, vjp = jax.vjp(lambda a, b, c: _f_confb(a, b, c)[0], c1_, g_, b_)
        return vjp(dy_)
    return _rows(fn, t, [(c1, 512, 0), (dmix_o, 512, 0)], [ln_g, ln_b], [512], [(1, 512), (1, 512)], "conf_norm_bwd")


def _s5_post_bwd(y_s5, proj_o, d_skip, glu_w, dmix_o, t):
    def fn(yc_, u_, dy_, d_, gw_):
        _, vjp = jax.vjp(lambda a, b, c, e: _f_s5post(a, b, c, e)[0], yc_, u_, d_, gw_)
        return vjp(dy_)
    return _rows(fn, t, [(y_s5, 512, 0), (proj_o, 512, 2), (dmix_o, 512, 1)], [d_skip, glu_w], [512, 512], [(1, 512), (512, 512)],
                 "s5_post_bwd")


def _ret_pre_bwd(proj, cosf, sins, dqr, dkr, t):
    def fn(q_, k_, cos_, sin_, dq_, dk_):
        _, vjp = jax.vjp(lambda a, b: _f_retpre(a, b, cos_, sin_), q_, k_)
        return vjp((dq_, dk_))
    return _rows(fn, t, [(proj, 512, 0), (proj, 512, 1), (cosf, LANES, 0), (sins, LANES, 0), (dqr, 512, 0), (dkr, 512, 0)], [],
                 [512, 512], [], "ret_pre_bwd")


def _ssd_post_bwd(yc, xbc_act, proj, d_skip, norm_w, expand, dmix_e, t):
    def fn(yc_, xs_, z_, dy_, d_, nw_, e_):
        _, vjp = jax.vjp(lambda a, b, c, dd, n: _f_ssdpost(a, b, c, dd, n, e_)[0], yc_, xs_, z_, d_, nw_)
        return vjp(dy_)
    return _rows(fn, t, [(yc, 1024, 0), (xbc_act, 1024, 0), (proj, 1024, 3), (dmix_e, 1024, 1)], [d_skip, norm_w, expand],
                 [1024, 1024, 1024], [(1, LANES), (1, 1024)], "ssd_post_bwd")


def _ssd_pre_bwd(xbc_act, proj, dt_bias, a_log, expand, dxdt, dda, dxs1, t):
    def fn(xs_, dtr_, dx_, dda_, dxs1_, bias_, alog_, e_):
        _, vjp = jax.vjp(lambda a, b, c, dd: _f_ssdpre(a, b, c, dd, e_), xs_, dtr_, bias_, alog_)
        dxs, ddtr, dbias, dalog = vjp((dx_, dda_))
        return dxs + dxs1_, ddtr, dbias, dalog
    return _rows(fn, t, [(xbc_act, 1024, 0), (proj, LANES, 44), (dxdt, 1024, 0), (dda, LANES, 0), (dxs1, 1024, 0)],
                 [dt_bias, a_log, expand], [1024, LANES], [(1, LANES), (1, LANES)], "ssd_pre_bwd")


def kernel(x, mix_norm, e_w_in, e_conv_w, e_conv_b, e_dt_bias, e_a_log, e_d, e_ssm_norm, e_w_out, o_w_in, o_dw_w, o_dw_b, o_ln_g, o_ln_b, o_a_re, o_a_im, o_b_re, o_b_im, o_c_re, o_c_im, o_d, o_log_step, o_glu_w, o_w_out, ffn_norm, ffn_w_up, ffn_dw_w, ffn_dw_b, ffn_w_down, final_norm, loss_target, m_mix_norm, m_e_w_in, m_e_conv_w, m_e_conv_b, m_e_dt_bias, m_e_a_log, m_e_d, m_e_ssm_norm, m_e_w_out, m_o_w_in, m_o_dw_w, m_o_dw_b, m_o_ln_g, m_o_ln_b, m_o_a_re, m_o_a_im, m_o_b_re, m_o_b_im, m_o_c_re, m_o_c_im, m_o_d, m_o_log_step, m_o_glu_w, m_o_w_out, m_ffn_norm, m_ffn_w_up, m_ffn_dw_w, m_ffn_dw_b, m_ffn_w_down, m_final_norm, v_mix_norm, v_e_w_in, v_e_conv_w, v_e_conv_b, v_e_dt_bias, v_e_a_log, v_e_d, v_e_ssm_norm, v_e_w_out, v_o_w_in, v_o_dw_w, v_o_dw_b, v_o_ln_g, v_o_ln_b, v_o_a_re, v_o_a_im, v_o_b_re, v_o_b_im, v_o_c_re, v_o_c_im, v_o_d, v_o_log_step, v_o_glu_w, v_o_w_out, v_ffn_norm, v_ffn_w_up, v_ffn_dw_w, v_ffn_dw_b, v_ffn_w_down, v_final_norm):
    p = dict(locals())

    kinds = ("grad_", "delta_", "new_m_", "new_v_")

    def block(name, layer):
        return p[name][0 if layer is None else layer]

    srcs = [block(n, layer).astype(BF16) for n, layer, _ in MATMUL_BLOCKS] + [_pack([p[n] for n in SMALL_SHARDED], F32, 16)]
    gathered = _exchange(srcs, [True] * len(srcs), "gather_weights")
    w = {n: p[n] for n in REPLICATED}
    for (n, layer, by_cols), g in zip(MATMUL_BLOCKS, gathered):
        if by_cols:
            full = _join_cols(g, E_IN_PAD if n == 'e_w_in' else N_DEV * g.shape[2], f"join_{n}_{layer}")
        else:
            full = g.reshape(N_DEV * g.shape[1], g.shape[2])
        if layer is None:
            w[n] = full
        else:
            w.setdefault(n, [None, None])[layer] = full
    for n, piece in zip(SMALL_SHARDED, _unpack(gathered[-1], [p[n].shape for n in SMALL_SHARDED], lead=(N_DEV,))):
        w[n] = _join_shards(piece, SHARDED[n])

    loss, dx, grads = _local_step(x[0], loss_target[0], w)
    loss = lax.psum(loss, MESH_AXES)

    sends = []
    for n, layer, by_cols in MATMUL_BLOCKS:
        g = grads[n] if layer is None else grads[n][layer]
        if by_cols:
            sends.append(_split_cols(g, block(n, layer).shape[1], BF16, f"split_{n}_{layer}"))
        else:
            sends.append(g.astype(BF16).reshape(N_DEV, -1, g.shape[1]))
    sends.append(_pack([_split_shards(grads[n].reshape(p[n].shape[:SHARDED[n]] + (-1,) + p[n].shape[SHARDED[n] + 1:]), SHARDED[n])
                        for n in SMALL_SHARDED], F32, 128, lead=(N_DEV,)))
    sends.append(_pack([grads[n].reshape(p[n].shape) for n in REPLICATED], F32, 128))
    parts = _exchange(sends, [False] * (len(sends) - 1) + [True], "exchange_grads")

    out, by_layer = {}, {}
    for (n, layer, _), part in zip(MATMUL_BLOCKS, parts):
        by_layer.setdefault(n, {})[layer] = _adamw(part, *[block(pre + n, layer) for pre in ("", "m_", "v_")], f"adamw_{n}_{layer}")
    for n, res in by_layer.items():
        for i, kind in enumerate(kinds):
            out[kind + n] = res[None][i][None] if None in res else jnp.stack([res[0][i], res[1][i]], axis=0)
    for names, part, tag in ((SMALL_SHARDED, parts[-2], "small"), (REPLICATED, parts[-1], "replicated")):
        packed = [_pack([p[pre + n] for n in names], F32, 128) for pre in ("", "m_", "v_")]
        for kind, buf in zip(kinds, _adamw(part, *packed, "adamw_" + tag)):
            for n, a in zip(names, _unpack(buf, [p[n].shape for n in names])):
                out[kind + n] = a
    return (loss, dx[None], *[out[kind + n] for kind in kinds for n in WEIGHTS])
```

```python
import functools
import math

import jax
import jax.numpy as jnp
from jax import lax
from jax.experimental import pallas as pl
from jax.experimental.pallas import tpu as pltpu

F32, BF16 = jnp.float32, jnp.bfloat16
HIGHEST = lax.Precision.HIGHEST
N_DEV = 8
MESH_AXES = ("x", "y", "c")
VMEM_LIMIT = 48 * 1024 * 1024
LANES = 128
PACK_COLS = 1024

D_MODEL = 1024
EPS = 1e-6
RET_HEADS, RET_DK, RET_DV = 4, 128, 256
SSM_HEADS, SSM_P, SSM_N, SSM_GROUPS = 16, 64, 128, 2
SSM_HG = SSM_HEADS // SSM_GROUPS
S5_GROUPS, S5_GROUP, S5_STATE = 32, 16, 64
S5_COLS = S5_GROUPS * S5_STATE
D_FF = 2816
E_IN, E_IN_PAD = 5648, 5760
ADAM_LR, ADAM_B1, ADAM_B2, ADAM_EPS, ADAM_WD, ADAM_STEP = 0.001, 0.9, 0.999, 1e-08, 0.01, 10

WEIGHTS = ['mix_norm', 'e_w_in', 'e_conv_w', 'e_conv_b', 'e_dt_bias', 'e_a_log', 'e_d', 'e_ssm_norm', 'e_w_out', 'o_w_in', 'o_dw_w', 'o_dw_b', 'o_ln_g', 'o_ln_b', 'o_a_re', 'o_a_im', 'o_b_re', 'o_b_im', 'o_c_re', 'o_c_im', 'o_d', 'o_log_step', 'o_glu_w', 'o_w_out', 'ffn_norm', 'ffn_w_up', 'ffn_dw_w', 'ffn_dw_b', 'ffn_w_down', 'final_norm']
SHARDED = {'e_w_in': 2, 'e_conv_w': 2, 'e_w_out': 1, 'o_w_in': 2, 'o_dw_w': 2, 'o_dw_b': 1, 'o_ln_g': 1, 'o_ln_b': 1,
           'o_d': 1, 'o_glu_w': 1, 'o_w_out': 1, 'ffn_w_up': 2, 'ffn_dw_w': 2, 'ffn_w_down': 1}
MATMUL_WEIGHTS = ['e_w_in', 'e_w_out', 'o_w_in', 'o_glu_w', 'o_w_out', 'ffn_w_up', 'ffn_w_down']
MATMUL_BLOCKS = [('e_w_in', None, True), ('e_w_out', None, False), ('o_w_in', None, True), ('o_glu_w', None, False),
                 ('o_w_out', None, False), ('ffn_w_up', 0, True), ('ffn_w_up', 1, True), ('ffn_w_down', 0, False), ('ffn_w_down', 1, False)]
SMALL_SHARDED = [n for n in WEIGHTS if n in SHARDED and n not in MATMUL_WEIGHTS]
REPLICATED = [n for n in WEIGHTS if n not in SHARDED]


def _call(fn, grid, ins, outs, name):
    n_in = len(ins)

    def body(*refs):
        vals = fn(*[r[...] for r in refs[:n_in]])
        first = pl.program_id(0) == 0
        for r, v, o in zip(refs[n_in:], vals, outs):
            if o[4]:
                @pl.when(first)
                def _():
                    r[...] = jnp.zeros_like(r)
                r[...] += v.astype(r.dtype)
            else:
                r[...] = v.astype(r.dtype)

    return pl.pallas_call(
        body, grid=grid,
        in_specs=[pl.BlockSpec(b, m) for _, b, m in ins],
        out_specs=[pl.BlockSpec(o[2], o[3]) for o in outs],
        out_shape=[jax.ShapeDtypeStruct(o[0], o[1]) for o in outs],
        compiler_params=pltpu.CompilerParams(dimension_semantics=("arbitrary",) * len(grid), vmem_limit_bytes=VMEM_LIMIT),
        name=name)(*[a for a, _, _ in ins])


def _rows(fn, n_rows, row_ins, full_ins, row_outs, acc_outs, name, tm=256, bf16_outs=()):
    tm = min(tm, n_rows)
    ins = [(a, (tm, w), (lambda i, c=c: (i, c))) for a, w, c in row_ins]
    ins += [(a, a.shape, (lambda i, n=a.ndim: (0,) * n)) for a in full_ins]
    outs = [((n_rows, w), BF16 if k in bf16_outs else F32, (tm, w), (lambda i: (i, 0)), False) for k, w in enumerate(row_outs)]
    outs += [(tuple(s), F32, tuple(s), (lambda i, n=len(s): (0,) * n), True) for s in acc_outs]
    return _call(fn, (n_rows // tm,), ins, outs, name)


def _cols(fn, n_blocks, col_ins, out_leads, name, cb=LANES, bf16_outs=()):
    ins = [(a, a.shape[:-1] + (cb,), (lambda j, n=a.ndim, o=o: (0,) * (n - 1) + (j + o,))) for a, o in col_ins]
    outs = [(tuple(s) + (n_blocks * cb,), BF16 if k in bf16_outs else F32, tuple(s) + (cb,), (lambda j, n=len(s): (0,) * n + (j,)), False)
            for k, s in enumerate(out_leads)]
    return _call(fn, (n_blocks,), ins, outs, name)


def _grad_fn(f, n_diff, n_in):
    def g(*a):
        diff, consts, cts = a[:n_diff], a[n_diff:n_in], a[n_in:]
        _, vjp = jax.vjp(lambda *d: f(*d, *consts), *diff)
        return vjp(tuple(cts))
    return g


def _silu(x):
    return x * jax.nn.sigmoid(x)


def _rms(x, g):
    return x * lax.rsqrt(jnp.mean(x * x, axis=-1, keepdims=True) + EPS) * g


@jax.custom_vjp
def _softplus(x):
    return jnp.maximum(x, 0.0) + jnp.log(1.0 + jnp.exp(-jnp.abs(x)))


_softplus.defvjp(lambda x: (_softplus(x), x), lambda x, g: (g * jax.nn.sigmoid(x),))


@jax.custom_vjp
def _swap_halves(x):
    return pltpu.roll(x, 64, 1)


_swap_halves.defvjp(lambda x: (_swap_halves(x), None), lambda _, g: (_swap_halves(g),))


def _shift_rows(x, k, up):
    if k == 0:
        return x
    n = x.shape[0]
    t = lax.broadcasted_iota(jnp.int32, x.shape, 0)
    if up:
        return jnp.where(t < n - k, pltpu.roll(x, n - k, 0), 0.0)
    return jnp.where(t >= k, pltpu.roll(x, k, 0), 0.0)


@jax.custom_vjp
def _dwconv(x, w, b):
    k_taps = w.shape[0]
    y = b + w[k_taps - 1] * x
    for k in range(k_taps - 1):
        y = y + w[k] * _shift_rows(x, k_taps - 1 - k, False)
    return y


def _dwconv_fwd(x, w, b):
    return _dwconv(x, w, b), (x, w)


def _dwconv_bwd(saved, dy):
    x, w = saved
    k_taps = w.shape[0]
    dx = w[k_taps - 1] * dy
    dws = []
    for k in range(k_taps - 1):
        s = k_taps - 1 - k
        dx = dx + w[k] * _shift_rows(dy, s, True)
        dws.append(jnp.sum(dy * _shift_rows(x, s, False), axis=0, keepdims=True)[None])
    dws.append(jnp.sum(dy * x, axis=0, keepdims=True)[None])
    return dx, jnp.concatenate(dws, axis=0), jnp.sum(dy, axis=0, keepdims=True)


_dwconv.defvjp(_dwconv_fwd, _dwconv_bwd)


def _f_rms(x, g):
    return (_rms(x, g),)


def _rot(x, cosf, sins):
    outs = []
    for h in range(RET_HEADS):
        xh = x[:, h * RET_DK:(h + 1) * RET_DK]
        outs.append(xh * cosf + _swap_halves(xh) * sins)
    return jnp.concatenate(outs, axis=1)


def _f_retpre(q, k, cosf, sins):
    return _rot(q, cosf, sins), _rot(k, cosf, sins) * (RET_DK ** -0.5)


def _f_retpost(r, g):
    outs = []
    for h in range(RET_HEADS):
        rh = r[:, h * RET_DV:(h + 1) * RET_DV]
        rc = rh - jnp.mean(rh, axis=-1, keepdims=True)
        outs.append(_silu(g[:, h * RET_DV:(h + 1) * RET_DV]) * (rc * lax.rsqrt(jnp.mean(rc * rc, axis=-1, keepdims=True) + EPS)))
    return (jnp.concatenate(outs, axis=1),)


def _f_ssdconv(xbc, w, b):
    return (_silu(_dwconv(xbc, w, b)),)


def _f_ssdpre(xs, dtr, bias, alog, expand):
    dt = _softplus(dtr + bias)
    return xs * jnp.dot(dt, expand, precision=HIGHEST, preferred_element_type=F32), dt * (-jnp.exp(alog))


def _f_ssdpost(yc, xs, z, dskip, norm_w, expand):
    d_wide = jnp.dot(jnp.broadcast_to(dskip, (yc.shape[0], LANES)), expand, precision=HIGHEST, preferred_element_type=F32)
    y = (yc + d_wide * xs) * _silu(z)
    half = y.shape[1] // SSM_GROUPS
    outs = []
    for g in range(SSM_GROUPS):
        yg = y[:, g * half:(g + 1) * half]
        outs.append(yg * lax.rsqrt(jnp.mean(yg * yg, axis=-1, keepdims=True) + EPS))
    return (jnp.concatenate(outs, axis=1) * norm_w,)


def _f_ffnmid(gin, uin, wg, wu, bg, bu):
    return (_silu(_dwconv(gin, wg, bg)) * _dwconv(uin, wu, bu),)


def _f_confb(c1, g, b):
    mu = jnp.mean(c1, axis=-1, keepdims=True)
    xc = c1 - mu
    return (_silu(xc * lax.rsqrt(jnp.mean(xc * xc, axis=-1, keepdims=True) + EPS) * g + b),)


def _f_s5post(yc, u, dskip, glu_w):
    s = jax.nn.gelu(yc + dskip * u)
    z = jnp.dot(s.astype(BF16), glu_w.astype(BF16), preferred_element_type=F32)
    return (s * jax.nn.sigmoid(z),)


def _f_s5par(lr, li, ls, br, bi):
    step = jnp.exp(ls)
    mag = jnp.exp(lr * step)
    ab_re = mag * jnp.cos(li * step)
    ab_im = mag * jnp.sin(li * step)
    den = lr * lr + li * li
    f_re = ((ab_re - 1.0) * lr + ab_im * li) / den
    f_im = (ab_im * lr - (ab_re - 1.0) * li) / den
    return ab_re, ab_im, f_re[None] * br - f_im[None] * bi, f_re[None] * bi + f_im[None] * br


def _loss_step(x, tgt, g):
    def f(x_, g_):
        e = _rms(x_, g_) - tgt
        return 0.5 * jnp.sum(jnp.mean(e * e, axis=-1, keepdims=True), axis=0, keepdims=True)
    loss, vjp = jax.vjp(f, x, g)
    dx, dg = vjp(jnp.ones((1, 1), F32))
    return dx, dx, jnp.broadcast_to(loss, (8, LANES)), dg


def _tile(n, pref):
    if n <= pref:
        return n
    t = (pref // LANES) * LANES
    while n % t:
        t -= LANES
    return t


_DIMS = {"nn": (((1,), (0,)), ((), ())), "nt": (((1,), (1,)), ((), ())), "tn": (((0,), (0,)), ((), ()))}


def _mm(a, b, mode, name, res=None, out_dtype=F32, tm=1024, tn=1408, tk=1408):
    if mode == "nn":
        (m, k), n = a.shape, b.shape[1]
    elif mode == "nt":
        (m, k), n = a.shape, b.shape[0]
    else:
        (k, m), n = a.shape, b.shape[1]
    tm, tn, tk = _tile(m, tm), _tile(n, tn), _tile(k, tk)
    nk = k // tk
    a_spec = pl.BlockSpec((tk, tm), lambda i, j, kk: (kk, i)) if mode == "tn" else pl.BlockSpec((tm, tk), lambda i, j, kk: (i, kk))
    b_spec = pl.BlockSpec((tn, tk), lambda i, j, kk: (j, kk)) if mode == "nt" else pl.BlockSpec((tk, tn), lambda i, j, kk: (kk, j))
    o_spec = pl.BlockSpec((tm, tn), lambda i, j, kk: (i, j))
    has_res = res is not None

    def body(*refs):
        a_ref, b_ref = refs[0], refs[1]
        o_ref, acc = refs[-2], refs[-1]
        kk = pl.program_id(2)

        @pl.when(kk == 0)
        def _():
            acc[...] = jnp.zeros_like(acc)

        acc[...] += lax.dot_general(a_ref[...].astype(BF16), b_ref[...].astype(BF16), _DIMS[mode], preferred_element_type=F32)

        @pl.when(kk == nk - 1)
        def _():
            o_ref[...] = (acc[...] + refs[2][...] if has_res else acc[...]).astype(out_dtype)

    return pl.pallas_call(
        body, grid=(m // tm, n // tn, nk),
        in_specs=[a_spec, b_spec] + ([o_spec] if has_res else []),
        out_specs=o_spec, out_shape=jax.ShapeDtypeStruct((m, n), out_dtype),
        scratch_shapes=[pltpu.VMEM((tm, tn), F32)],
        compiler_params=pltpu.CompilerParams(dimension_semantics=("parallel", "parallel", "arbitrary"), vmem_limit_bytes=VMEM_LIMIT),
        name=name)(*([a, b] + ([res] if has_res else [])))


def _seq_block(t):
    return min(512, t)


def _causal_diff(i, j, blk):
    r = lax.broadcasted_iota(jnp.int32, (blk, blk), 0)
    c = lax.broadcasted_iota(jnp.int32, (blk, blk), 1)
    return (i - j) * blk + r - c


def _ret_decay(lg, i, j, blk):
    diff = _causal_diff(i, j, blk)
    return jnp.where(diff >= 0, jnp.exp(lg * jnp.maximum(diff, 0).astype(F32)), 0.0)


def _ret_log_gamma():
    return jnp.log1p(-(2.0 ** (-5.0 - jnp.arange(RET_HEADS, dtype=F32))))


def _ret_fwd(qr, kr, proj, t):
    blk = _seq_block(t)
    nb = t // blk
    v_off = (2 * RET_HEADS * RET_DK) // RET_DV

    def body(lg_ref, q_ref, k_ref, v_ref, o_ref, acc):
        h, i, j = pl.program_id(0), pl.program_id(1), pl.program_id(2)

        @pl.when(j == 0)
        def _():
            acc[...] = jnp.zeros_like(acc)

        @pl.when(j <= i)
        def _():
            s = lax.dot_general(q_ref[...].astype(BF16), k_ref[...].astype(BF16), _DIMS["nt"], preferred_element_type=F32)
            p = (s * _ret_decay(lg_ref[h], i, j, blk)).astype(BF16)
            acc[...] += jnp.dot(p, v_ref[...].astype(BF16), preferred_element_type=F32)

        @pl.when(j == nb - 1)
        def _():
            o_ref[...] = acc[...]

    return pl.pallas_call(
        body, grid=(RET_HEADS, nb, nb),
        in_specs=[pl.BlockSpec(memory_space=pltpu.SMEM),
                  pl.BlockSpec((blk, RET_DK), lambda h, i, j: (i, h)),
                  pl.BlockSpec((blk, RET_DK), lambda h, i, j: (jnp.minimum(j, i), h)),
                  pl.BlockSpec((blk, RET_DV), lambda h, i, j: (jnp.minimum(j, i), v_off + h))],
        out_specs=pl.BlockSpec((blk, RET_DV), lambda h, i, j: (i, h)),
        out_shape=jax.ShapeDtypeStruct((t, RET_HEADS * RET_DV), F32),
        scratch_shapes=[pltpu.VMEM((blk, RET_DV), F32)],
        compiler_params=pltpu.CompilerParams(dimension_semantics=("arbitrary",) * 3, vmem_limit_bytes=VMEM_LIMIT),
        name="ret_fwd")(_ret_log_gamma(), qr, kr, proj)


def _ret_bwd(qr, kr, proj, dr, t):
    blk = _seq_block(t)
    nb = t // blk
    v_off = (2 * RET_HEADS * RET_DK) // RET_DV

    def dq_body(lg_ref, q_ref, k_ref, v_ref, do_ref, dq_ref, acc):
        h, i, j = pl.program_id(0), pl.program_id(1), pl.program_id(2)

        @pl.when(j == 0)
        def _():
            acc[...] = jnp.zeros_like(acc)

        @pl.when(j <= i)
        def _():
            ds = lax.dot_general(do_ref[...].astype(BF16), v_ref[...].astype(BF16), _DIMS["nt"], preferred_element_type=F32)
            dsm = (ds * _ret_decay(lg_ref[h], i, j, blk)).astype(BF16)
            acc[...] += jnp.dot(dsm, k_ref[...].astype(BF16), preferred_element_type=F32)

        @pl.when(j == nb - 1)
        def _():
            dq_ref[...] = acc[...]

    dq = pl.pallas_call(
        dq_body, grid=(RET_HEADS, nb, nb),
        in_specs=[pl.BlockSpec(memory_space=pltpu.SMEM),
                  pl.BlockSpec((blk, RET_DK), lambda h, i, j: (i, h)),
                  pl.BlockSpec((blk, RET_DK), lambda h, i, j: (jnp.minimum(j, i), h)),
                  pl.BlockSpec((blk, RET_DV), lambda h, i, j: (jnp.minimum(j, i), v_off + h)),
                  pl.BlockSpec((blk, RET_DV), lambda h, i, j: (i, h))],
        out_specs=pl.BlockSpec((blk, RET_DK), lambda h, i, j: (i, h)),
        out_shape=jax.ShapeDtypeStruct((t, RET_HEADS * RET_DK), F32),
        scratch_shapes=[pltpu.VMEM((blk, RET_DK), F32)],
        compiler_params=pltpu.CompilerParams(dimension_semantics=("arbitrary",) * 3, vmem_limit_bytes=VMEM_LIMIT),
        name="ret_bwd_dq")(_ret_log_gamma(), qr, kr, proj, dr)

    def dkv_body(lg_ref, q_ref, k_ref, v_ref, do_ref, dk_ref, dv_ref, acc_k, acc_v):
        h, j, i = pl.program_id(0), pl.program_id(1), pl.program_id(2)

        @pl.when(i == 0)
        def _():
            acc_k[...] = jnp.zeros_like(acc_k)
            acc_v[...] = jnp.zeros_like(acc_v)

        @pl.when(i >= j)
        def _():
            q = q_ref[...].astype(BF16)
            do = do_ref[...].astype(BF16)
            decay = _ret_decay(lg_ref[h], i, j, blk)
            s = lax.dot_general(q, k_ref[...].astype(BF16), _DIMS["nt"], preferred_element_type=F32)
            acc_v[...] += lax.dot_general((s * decay).astype(BF16), do, _DIMS["tn"], preferred_element_type=F32)
            ds = lax.dot_general(do, v_ref[...].astype(BF16), _DIMS["nt"], preferred_element_type=F32)
            acc_k[...] += lax.dot_general((ds * decay).astype(BF16), q, _DIMS["tn"], preferred_element_type=F32)

        @pl.when(i == nb - 1)
        def _():
            dk_ref[...] = acc_k[...]
            dv_ref[...] = acc_v[...].astype(BF16)

    dk, dv = pl.pallas_call(
        dkv_body, grid=(RET_HEADS, nb, nb),
        in_specs=[pl.BlockSpec(memory_space=pltpu.SMEM),
                  pl.BlockSpec((blk, RET_DK), lambda h, j, i: (jnp.maximum(i, j), h)),
                  pl.BlockSpec((blk, RET_DK), lambda h, j, i: (j, h)),
                  pl.BlockSpec((blk, RET_DV), lambda h, j, i: (j, v_off + h)),
                  pl.BlockSpec((blk, RET_DV), lambda h, j, i: (jnp.maximum(i, j), h))],
        out_specs=[pl.BlockSpec((blk, RET_DK), lambda h, j, i: (j, h)), pl.BlockSpec((blk, RET_DV), lambda h, j, i: (j, h))],
        out_shape=[jax.ShapeDtypeStruct((t, RET_HEADS * RET_DK), F32), jax.ShapeDtypeStruct((t, RET_HEADS * RET_DV), BF16)],
        scratch_shapes=[pltpu.VMEM((blk, RET_DK), F32), pltpu.VMEM((blk, RET_DV), F32)],
        compiler_params=pltpu.CompilerParams(dimension_semantics=("arbitrary",) * 3, vmem_limit_bytes=VMEM_LIMIT),
        name="ret_bwd_dkv")(_ret_log_gamma(), qr, kr, proj, dr)
    return dq, dk, dv


def _cumsum(x, reverse, name):
    t = x.shape[0]
    blk = _seq_block(t)
    nb = t // blk

    def body(x_ref, o_ref, carry):
        @pl.when(pl.program_id(0) == 0)
        def _():
            carry[...] = jnp.zeros_like(carry)

        r = lax.broadcasted_iota(jnp.int32, (blk, blk), 0)
        c = lax.broadcasted_iota(jnp.int32, (blk, blk), 1)
        tri = ((r <= c) if reverse else (r >= c)).astype(F32)
        o_ref[...] = jnp.dot(tri, x_ref[...], precision=HIGHEST, preferred_element_type=F32) + carry[...]
        carry[...] = o_ref[0:1, :] if reverse else o_ref[blk - 1:blk, :]

    idx = (lambda i: (nb - 1 - i, 0)) if reverse else (lambda i: (i, 0))
    return pl.pallas_call(
        body, grid=(nb,), in_specs=[pl.BlockSpec((blk, LANES), idx)], out_specs=pl.BlockSpec((blk, LANES), idx),
        out_shape=jax.ShapeDtypeStruct((t, LANES), F32), scratch_shapes=[pltpu.VMEM((1, LANES), F32)],
        compiler_params=pltpu.CompilerParams(dimension_semantics=("arbitrary",), vmem_limit_bytes=VMEM_LIMIT),
        name=name)(x)


def _ssd_decay(a_ref, at_ref, hh, mask):
    return jnp.exp(jnp.where(mask, a_ref[:, hh:hh + 1] - at_ref[hh:hh + 1, :], -jnp.inf))


_B_OFF, _C_OFF = 1024 // SSM_N, 1024 // SSM_N + SSM_GROUPS


def _ssd_fwd(xbc_act, xdt, a_cum, a_cum_t, t):
    blk = _seq_block(t)
    nb = t // blk

    def body(c0, c1, b0, b1, x_ref, a_ref, at_ref, o_ref, acc):
        i, j = pl.program_id(0), pl.program_id(1)

        @pl.when(j == 0)
        def _():
            acc[...] = jnp.zeros_like(acc)

        @pl.when(j <= i)
        def _():
            mask = _causal_diff(i, j, blk) >= 0
            for g, (c_ref, b_ref) in enumerate(((c0, b0), (c1, b1))):
                cb = lax.dot_general(c_ref[...].astype(BF16), b_ref[...].astype(BF16), _DIMS["nt"], preferred_element_type=F32)
                for h in range(SSM_HG):
                    hh = g * SSM_HG + h
                    cols = slice(hh * SSM_P, (hh + 1) * SSM_P)
                    m = (cb * _ssd_decay(a_ref, at_ref, hh, mask)).astype(BF16)
                    acc[:, cols] += jnp.dot(m, x_ref[:, cols].astype(BF16), preferred_element_type=F32)

        @pl.when(j == nb - 1)
        def _():
            o_ref[...] = acc[...]

    row_i = lambda off: pl.BlockSpec((blk, SSM_N), lambda i, j, off=off: (i, off))
    row_j = lambda off: pl.BlockSpec((blk, SSM_N), lambda i, j, off=off: (jnp.minimum(j, i), off))
    return pl.pallas_call(
        body, grid=(nb, nb),
        in_specs=[row_i(_C_OFF), row_i(_C_OFF + 1), row_j(_B_OFF), row_j(_B_OFF + 1),
                  pl.BlockSpec((blk, SSM_HEADS * SSM_P), lambda i, j: (jnp.minimum(j, i), 0)),
                  pl.BlockSpec((blk, LANES), lambda i, j: (i, 0)),
                  pl.BlockSpec((SSM_HEADS, blk), lambda i, j: (0, jnp.minimum(j, i)))],
        out_specs=pl.BlockSpec((blk, SSM_HEADS * SSM_P), lambda i, j: (i, 0)),
        out_shape=jax.ShapeDtypeStruct((t, SSM_HEADS * SSM_P), F32),
        scratch_shapes=[pltpu.VMEM((blk, SSM_HEADS * SSM_P), F32)],
        compiler_params=pltpu.CompilerParams(dimension_semantics=("arbitrary",) * 2, vmem_limit_bytes=VMEM_LIMIT),
        name="ssd_fwd")(xbc_act, xbc_act, xbc_act, xbc_act, xdt, a_cum, a_cum_t)


def _ssd_bwd(xbc_act, xdt, a_cum, a_cum_t, dy, t):
    blk = _seq_block(t)
    nb = t // blk
    width = SSM_HEADS * SSM_P

    def q_body(c0, c1, b0, b1, x_ref, a_ref, at_ref, dy_ref, dc_ref, da_ref, acc_c, acc_a):
        i, j = pl.program_id(0), pl.program_id(1)

        @pl.when(j == 0)
        def _():
            acc_c[...] = jnp.zeros_like(acc_c)
            acc_a[...] = jnp.zeros_like(acc_a)

        @pl.when(j <= i)
        def _():
            mask = _causal_diff(i, j, blk) >= 0
            for g, (c_ref, b_ref) in enumerate(((c0, b0), (c1, b1))):
                bj = b_ref[...].astype(BF16)
                cb = lax.dot_general(c_ref[...].astype(BF16), bj, _DIMS["nt"], preferred_element_type=F32)
                dcb = jnp.zeros((blk, blk), F32)
                for h in range(SSM_HG):
                    hh = g * SSM_HG + h
                    cols = slice(hh * SSM_P, (hh + 1) * SSM_P)
                    decay = _ssd_decay(a_ref, at_ref, hh, mask)
                    dm = lax.dot_general(dy_ref[:, cols].astype(BF16), x_ref[:, cols].astype(BF16), _DIMS["nt"],
                                         preferred_element_type=F32) * decay
                    dcb = dcb + dm
                    acc_a[:, hh:hh + 1] += jnp.sum(dm * cb, axis=1, keepdims=True)
                acc_c[:, g * SSM_N:(g + 1) * SSM_N] += jnp.dot(dcb.astype(BF16), bj, preferred_element_type=F32)

        @pl.when(j == nb - 1)
        def _():
            dc_ref[...] = acc_c[...]
            da_ref[...] = acc_a[...]

    row_i = lambda off: pl.BlockSpec((blk, SSM_N), lambda i, j, off=off: (i, off))
    row_j = lambda off: pl.BlockSpec((blk, SSM_N), lambda i, j, off=off: (jnp.minimum(j, i), off))
    dc, da_q = pl.pallas_call(
        q_body, grid=(nb, nb),
        in_specs=[row_i(_C_OFF), row_i(_C_OFF + 1), row_j(_B_OFF), row_j(_B_OFF + 1),
                  pl.BlockSpec((blk, width), lambda i, j: (jnp.minimum(j, i), 0)),
                  pl.BlockSpec((blk, LANES), lambda i, j: (i, 0)),
                  pl.BlockSpec((SSM_HEADS, blk), lambda i, j: (0, jnp.minimum(j, i))),
                  pl.BlockSpec((blk, width), lambda i, j: (i, 0))],
        out_specs=[pl.BlockSpec((blk, SSM_GROUPS * SSM_N), lambda i, j: (i, 0)), pl.BlockSpec((blk, LANES), lambda i, j: (i, 0))],
        out_shape=[jax.ShapeDtypeStruct((t, SSM_GROUPS * SSM_N), F32), jax.ShapeDtypeStruct((t, LANES), F32)],
        scratch_shapes=[pltpu.VMEM((blk, SSM_GROUPS * SSM_N), F32), pltpu.VMEM((blk, LANES), F32)],
        compiler_params=pltpu.CompilerParams(dimension_semantics=("arbitrary",) * 2, vmem_limit_bytes=VMEM_LIMIT),
        name="ssd_bwd_q")(xbc_act, xbc_act, xbc_act, xbc_act, xdt, a_cum, a_cum_t, dy)

    def k_body(c0, c1, b0, b1, x_ref, a_ref, at_ref, dy_ref, db_ref, dx_ref, dat_ref, acc_b, acc_x, acc_a):
        j, i = pl.program_id(0), pl.program_id(1)

        @pl.when(i == 0)
        def _():
            acc_b[...] = jnp.zeros_like(acc_b)
            acc_x[...] = jnp.zeros_like(acc_x)
            acc_a[...] = jnp.zeros_like(acc_a)

        @pl.when(i >= j)
        def _():
            mask = _causal_diff(i, j, blk) >= 0
            for g, (c_ref, b_ref) in enumerate(((c0, b0), (c1, b1))):
                ci = c_ref[...].astype(BF16)
                cb = lax.dot_general(ci, b_ref[...].astype(BF16), _DIMS["nt"], preferred_element_type=F32)
                dcb = jnp.zeros((blk, blk), F32)
                for h in range(SSM_HG):
                    hh = g * SSM_HG + h
                    cols = slice(hh * SSM_P, (hh + 1) * SSM_P)
                    decay = _ssd_decay(a_ref, at_ref, hh, mask)
                    dyh = dy_ref[:, cols].astype(BF16)
                    acc_x[:, cols] += lax.dot_general((cb * decay).astype(BF16), dyh, _DIMS["tn"], preferred_element_type=F32)
                    dm = lax.dot_general(dyh, x_ref[:, cols].astype(BF16), _DIMS["nt"], preferred_element_type=F32) * decay
                    dcb = dcb + dm
                    acc_a[hh:hh + 1, :] += jnp.sum(dm * cb, axis=0, keepdims=True)
                acc_b[:, g * SSM_N:(g + 1) * SSM_N] += lax.dot_general(dcb.astype(BF16), ci, _DIMS["tn"], preferred_element_type=F32)

        @pl.when(i == nb - 1)
        def _():
            db_ref[...] = acc_b[...]
            dx_ref[...] = acc_x[...]
            dat_ref[...] = acc_a[...]

    rowk_i = lambda off: pl.BlockSpec((blk, SSM_N), lambda j, i, off=off: (jnp.maximum(i, j), off))
    rowk_j = lambda off: pl.BlockSpec((blk, SSM_N), lambda j, i, off=off: (j, off))
    db, dx, da_k_t = pl.pallas_call(
        k_body, grid=(nb, nb),
        in_specs=[rowk_i(_C_OFF), rowk_i(_C_OFF + 1), rowk_j(_B_OFF), rowk_j(_B_OFF + 1),
                  pl.BlockSpec((blk, width), lambda j, i: (j, 0)),
                  pl.BlockSpec((blk, LANES), lambda j, i: (jnp.maximum(i, j), 0)),
                  pl.BlockSpec((SSM_HEADS, blk), lambda j, i: (0, j)),
                  pl.BlockSpec((blk, width), lambda j, i: (jnp.maximum(i, j), 0))],
        out_specs=[pl.BlockSpec((blk, SSM_GROUPS * SSM_N), lambda j, i: (j, 0)), pl.BlockSpec((blk, width), lambda j, i: (j, 0)),
                   pl.BlockSpec((SSM_HEADS, blk), lambda j, i: (0, j))],
        out_shape=[jax.ShapeDtypeStruct((t, SSM_GROUPS * SSM_N), F32), jax.ShapeDtypeStruct((t, width), F32),
                   jax.ShapeDtypeStruct((SSM_HEADS, t), F32)],
        scratch_shapes=[pltpu.VMEM((blk, SSM_GROUPS * SSM_N), F32), pltpu.VMEM((blk, width), F32), pltpu.VMEM((SSM_HEADS, blk), F32)],
        compiler_params=pltpu.CompilerParams(dimension_semantics=("arbitrary",) * 2, vmem_limit_bytes=VMEM_LIMIT),
        name="ssd_bwd_k")(xbc_act, xbc_act, xbc_act, xbc_act, xdt, a_cum, a_cum_t, dy)
    return dc, da_q, db, dx, da_k_t


def _s5_scan(bu, a_re, a_im, reverse, x_prev=None):
    t = bu.shape[0]
    cb = LANES
    ncb = S5_COLS // cb
    pad = t // 2
    chunk = min(512, t)
    base = 0 if reverse else pad
    zero0 = t if reverse else 0

    def body(*refs):
        if reverse:
            br_ref, bi_ref, ar_ref, ai_ref, xr_ref, xi_ref, or_ref, oi_ref, dar_ref, dai_ref, sr, si = refs
        else:
            br_ref, bi_ref, ar_ref, ai_ref, or_ref, oi_ref, or_bf_ref, oi_bf_ref, sr, si = refs
        sr[pl.ds(zero0, pad), :] = jnp.zeros((pad, cb), F32)
        si[pl.ds(zero0, pad), :] = jnp.zeros((pad, cb), F32)
        sr[pl.ds(base, t), :] = br_ref[...]
        si[pl.ds(base, t), :] = bi_ref[...]
        pr = ar_ref[...]
        pi = -ai_ref[...] if reverse else ai_ref[...]
        s = 1
        while s < t:
            shift = s if reverse else -s

            def step(t0, pr=pr, pi=pi, shift=shift):
                cr, ci = sr[pl.ds(base + t0, chunk), :], si[pl.ds(base + t0, chunk), :]
                qr, qi = sr[pl.ds(base + t0 + shift, chunk), :], si[pl.ds(base + t0 + shift, chunk), :]
                sr[pl.ds(base + t0, chunk), :] = cr + pr * qr - pi * qi
                si[pl.ds(base + t0, chunk), :] = ci + pr * qi + pi * qr

            order = range(0, t, chunk) if reverse else range(t - chunk, -1, -chunk)
            for t0 in order:
                step(t0)
            pr, pi = pr * pr - pi * pi, 2.0 * pr * pi
            s *= 2
        or_ref[...] = sr[pl.ds(base, t), :].astype(or_ref.dtype)
        oi_ref[...] = si[pl.ds(base, t), :].astype(oi_ref.dtype)
        if not reverse:
            or_bf_ref[...] = sr[pl.ds(base, t), :].astype(BF16)
            oi_bf_ref[...] = si[pl.ds(base, t), :].astype(BF16)
        if reverse:
            dar = jnp.zeros((1, cb), F32)
            dai = jnp.zeros((1, cb), F32)
            for t0 in range(0, t, chunk):
                gr, gi = sr[pl.ds(t0 + 1, chunk), :], si[pl.ds(t0 + 1, chunk), :]
                xr, xi = xr_ref[pl.ds(t0, chunk), :], xi_ref[pl.ds(t0, chunk), :]
                dar = dar + jnp.sum(gr * xr + gi * xi, axis=0, keepdims=True)
                dai = dai + jnp.sum(gi * xr - gr * xi, axis=0, keepdims=True)
            dar_ref[...] = dar
            dai_ref[...] = dai

    re_spec = pl.BlockSpec((t, cb), lambda j: (0, j))
    im_spec = pl.BlockSpec((t, cb), lambda j: (0, j + ncb))
    a_spec = pl.BlockSpec((1, cb), lambda j: (0, j))
    ins, in_specs = [bu, bu, a_re, a_im], [re_spec, im_spec, a_spec, a_spec]
    seq_bf = jax.ShapeDtypeStruct((t, S5_COLS), BF16)
    if reverse:
        ins += [x_prev, x_prev]
        in_specs += [re_spec, im_spec]
        out_shape = [seq_bf, seq_bf] + [jax.ShapeDtypeStruct((1, S5_COLS), F32)] * 2
        out_specs = [re_spec, re_spec, a_spec, a_spec]
    else:
        out_shape = [jax.ShapeDtypeStruct((t, S5_COLS), F32)] * 2 + [seq_bf, seq_bf]
        out_specs = [re_spec] * 4
    return pl.pallas_call(
        body, grid=(ncb,), in_specs=in_specs, out_specs=out_specs, out_shape=out_shape,
        scratch_shapes=[pltpu.VMEM((t + pad, cb), F32), pltpu.VMEM((t + pad, cb), F32)],
        compiler_params=pltpu.CompilerParams(dimension_semantics=("arbitrary",), vmem_limit_bytes=VMEM_LIMIT),
        name="s5_scan_bwd" if reverse else "s5_scan_fwd")(*ins)


CONF_DIM = 512
CONF_K = 31
CONF_PAD = 32


def _conf_conv(proj_o, w, b, dc1=None):
    t = proj_o.shape[0]
    cb = LANES
    ncb = CONF_DIM // cb
    chunk = min(512, t)
    chunks = range(0, t, chunk)
    bwd = dc1 is not None

    def body(*refs):
        if bwd:
            ca_ref, cg_ref, w_ref, b_ref, dy_ref, dca_ref, dcg_ref, dw_ref, db_ref, xs, dys = refs
        else:
            ca_ref, cg_ref, w_ref, b_ref, o_ref, xs = refs
        xs[pl.ds(0, CONF_PAD), :] = jnp.zeros((CONF_PAD, cb), F32)
        for t0 in chunks:
            rows = pl.ds(t0, chunk)
            xs[pl.ds(CONF_PAD + t0, chunk), :] = ca_ref[rows, :] * jax.nn.sigmoid(cg_ref[rows, :])
        if not bwd:
            for t0 in chunks:
                acc = jnp.broadcast_to(b_ref[...], (chunk, cb))
                for k in range(CONF_K):
                    acc = acc + w_ref[k] * xs[pl.ds(CONF_PAD + t0 - (CONF_K - 1 - k), chunk), :]
                o_ref[pl.ds(t0, chunk), :] = acc
            return
        dys[pl.ds(t, CONF_PAD), :] = jnp.zeros((CONF_PAD, cb), F32)
        db = jnp.zeros((1, cb), F32)
        for t0 in chunks:
            dys[pl.ds(t0, chunk), :] = dy_ref[pl.ds(t0, chunk), :]
            db = db + jnp.sum(dy_ref[pl.ds(t0, chunk), :], axis=0, keepdims=True)
        db_ref[...] = db
        for t0 in chunks:
            rows = pl.ds(t0, chunk)
            acc = jnp.zeros((chunk, cb), F32)
            for k in range(CONF_K):
                acc = acc + w_ref[k] * dys[pl.ds(t0 + (CONF_K - 1 - k), chunk), :]
            sig = jax.nn.sigmoid(cg_ref[rows, :])
            dca_ref[rows, :] = (acc * sig).astype(BF16)
            dcg_ref[rows, :] = (acc * ca_ref[rows, :] * sig * (1.0 - sig)).astype(BF16)
        for k in range(CONF_K):
            dwk = jnp.zeros((1, cb), F32)
            for t0 in chunks:
                window = xs[pl.ds(CONF_PAD + t0 - (CONF_K - 1 - k), chunk), :]
                dwk = dwk + jnp.sum(dy_ref[pl.ds(t0, chunk), :] * window, axis=0, keepdims=True)
            dw_ref[k] = dwk

    col = lambda off: pl.BlockSpec((t, cb), lambda j, off=off: (0, j + off))
    w_spec = pl.BlockSpec((CONF_K, 1, cb), lambda j: (0, 0, j))
    b_spec = pl.BlockSpec((1, cb), lambda j: (0, j))
    seq = jax.ShapeDtypeStruct((t, CONF_DIM), F32)
    ins, in_specs = [proj_o, proj_o, w, b], [col(0), col(ncb), w_spec, b_spec]
    scratch = [pltpu.VMEM((CONF_PAD + t, cb), F32)]
    if bwd:
        ins, in_specs = ins + [dc1], in_specs + [col(0)]
        seq_bf = jax.ShapeDtypeStruct((t, CONF_DIM), BF16)
        out_shape, out_specs = [seq_bf, seq_bf, jax.ShapeDtypeStruct(w.shape, F32), jax.ShapeDtypeStruct(b.shape, F32)], [col(0), col(0), w_spec, b_spec]
        scratch = scratch + [pltpu.VMEM((t + CONF_PAD, cb), F32)]
    else:
        out_shape, out_specs = seq, col(0)
    return pl.pallas_call(
        body, grid=(ncb,), in_specs=in_specs, out_specs=out_specs, out_shape=out_shape, scratch_shapes=scratch,
        compiler_params=pltpu.CompilerParams(dimension_semantics=("arbitrary",), vmem_limit_bytes=VMEM_LIMIT),
        name="conf_conv_bwd" if bwd else "conf_conv")(*ins)


def _exchange(srcs, broadcast, name):
    n = len(srcs)

    def body(*refs):
        src_refs, out_refs = refs[:n], refs[n:2 * n]
        send_sems, recv_sems, local_sems = refs[2 * n:]
        x, y, c = lax.axis_index("x"), lax.axis_index("y"), lax.axis_index("c")
        me = 4 * x + 2 * y + c

        def piece(i, idx):
            return src_refs[i] if broadcast[i] else src_refs[i].at[idx]

        local = [pltpu.make_async_copy(piece(i, me), out_refs[i].at[me], local_sems.at[i]) for i in range(n)]
        for cp in local:
            cp.start()
        copies = []
        for k in range(1, N_DEV):
            px, py, pc = x ^ ((k >> 2) & 1), y ^ ((k >> 1) & 1), c ^ (k & 1)
            peer = 4 * px + 2 * py + pc
            for i in range(n):
                sem = (k - 1) * n + i
                send = pltpu.make_async_remote_copy(
                    src_ref=piece(i, peer), dst_ref=out_refs[i].at[me], send_sem=send_sems.at[sem], recv_sem=recv_sems.at[sem],
                    device_id=(px, py, pc), device_id_type=pl.DeviceIdType.MESH)
                send.start()
                recv = pltpu.make_async_remote_copy(
                    src_ref=piece(i, peer), dst_ref=out_refs[i].at[peer], send_sem=send_sems.at[sem], recv_sem=recv_sems.at[sem],
                    device_id=(px, py, pc), device_id_type=pl.DeviceIdType.MESH)
                copies.append((send, recv))
        for _, recv in copies:
            recv.wait_recv()
        for send, _ in copies:
            send.wait_send()
        for cp in local:
            cp.wait()

    blocks = [s.shape if b else s.shape[1:] for s, b in zip(srcs, broadcast)]
    n_sems = (N_DEV - 1) * n
    return pl.pallas_call(
        body, out_shape=[jax.ShapeDtypeStruct((N_DEV,) + tuple(blk), s.dtype) for blk, s in zip(blocks, srcs)],
        in_specs=[pl.BlockSpec(memory_space=pl.ANY)] * n, out_specs=[pl.BlockSpec(memory_space=pl.ANY)] * n,
        scratch_shapes=[pltpu.SemaphoreType.DMA((n_sems,)), pltpu.SemaphoreType.DMA((n_sems,)), pltpu.SemaphoreType.DMA((n,))],
        compiler_params=pltpu.CompilerParams(has_side_effects=True),
        name=name)(*srcs)


def _row_tile(r, pref=256):
    if r <= pref:
        return r
    t = pref // 16 * 16
    while r % t:
        t -= 16
    return t


def _join_cols(g, width, name):
    _, rows, ws = g.shape
    tr = _row_tile(rows)
    tail = width - N_DEV * ws

    def body(g_ref, o_ref):
        for d in range(N_DEV):
            o_ref[:, pl.ds(d * ws, ws)] = g_ref[d]
        if tail:
            o_ref[:, pl.ds(N_DEV * ws, tail)] = jnp.zeros((tr, tail), g.dtype)

    return pl.pallas_call(
        body, grid=(rows // tr,), in_specs=[pl.BlockSpec((N_DEV, tr, ws), lambda i: (0, i, 0))],
        out_specs=pl.BlockSpec((tr, width), lambda i: (i, 0)), out_shape=jax.ShapeDtypeStruct((rows, width), g.dtype),
        compiler_params=pltpu.CompilerParams(dimension_semantics=("parallel",), vmem_limit_bytes=VMEM_LIMIT), name=name)(g)


def _split_cols(full, ws, dtype, name):
    rows, width = full.shape
    tr = _row_tile(rows)

    def body(x_ref, o_ref):
        for d in range(N_DEV):
            o_ref[d] = x_ref[:, pl.ds(d * ws, ws)].astype(dtype)

    return pl.pallas_call(
        body, grid=(rows // tr,), in_specs=[pl.BlockSpec((tr, width), lambda i: (i, 0))],
        out_specs=pl.BlockSpec((N_DEV, tr, ws), lambda i: (0, i, 0)), out_shape=jax.ShapeDtypeStruct((N_DEV, rows, ws), dtype),
        compiler_params=pltpu.CompilerParams(dimension_semantics=("parallel",), vmem_limit_bytes=VMEM_LIMIT), name=name)(full)


def _adamw(parts, w, m, v, name):
    r, c = w.shape
    tr = _row_tile(r)

    def body(p_ref, w_ref, m_ref, v_ref, g_ref, d_ref, nm_ref, nv_ref):
        g = p_ref[0].astype(F32)
        for s in range(1, N_DEV):
            g = g + p_ref[s].astype(F32)
        nm = ADAM_B1 * m_ref[...] + (1.0 - ADAM_B1) * g
        nv = ADAM_B2 * v_ref[...] + (1.0 - ADAM_B2) * (g * g)
        m_hat = nm / (1.0 - ADAM_B1 ** ADAM_STEP)
        v_hat = nv / (1.0 - ADAM_B2 ** ADAM_STEP)
        g_ref[...] = g
        nm_ref[...] = nm
        nv_ref[...] = nv
        d_ref[...] = -ADAM_LR * (m_hat / (jnp.sqrt(v_hat) + ADAM_EPS) + ADAM_WD * w_ref[...])

    spec = pl.BlockSpec((tr, c), lambda i: (i, 0))
    return pl.pallas_call(
        body, grid=(r // tr,), in_specs=[pl.BlockSpec((N_DEV, tr, c), lambda i: (0, i, 0)), spec, spec, spec],
        out_specs=[spec] * 4, out_shape=[jax.ShapeDtypeStruct((r, c), F32)] * 4,
        compiler_params=pltpu.CompilerParams(dimension_semantics=("parallel",), vmem_limit_bytes=VMEM_LIMIT),
        name=name)(parts, w, m, v)


def _pack_rows(n_elems, mult):
    rows = -(-n_elems // PACK_COLS)
    return -(-rows // mult) * mult


def _pack(arrays, dtype, mult, lead=()):
    flat = jnp.concatenate([a.astype(dtype).reshape(lead + (-1,)) for a in arrays], axis=-1)
    rows = _pack_rows(flat.shape[-1], mult)
    flat = jnp.pad(flat, [(0, 0)] * len(lead) + [(0, rows * PACK_COLS - flat.shape[-1])])
    return flat.reshape(lead + (rows, PACK_COLS))


def _unpack(buf, shapes, lead=()):
    flat = buf.reshape(lead + (-1,))
    out, off = [], 0
    for s in shapes:
        n = math.prod(s)
        out.append(flat[..., off:off + n].reshape(lead + tuple(s)))
        off += n
    return out


def _join_shards(piece, axis):
    moved = jnp.moveaxis(piece, 0, axis)
    shape = moved.shape
    return moved.reshape(shape[:axis] + (shape[axis] * shape[axis + 1],) + shape[axis + 2:])


def _split_shards(full, axis):
    shape = full.shape
    return jnp.moveaxis(full.reshape(shape[:axis] + (N_DEV, shape[axis] // N_DEV) + shape[axis + 1:]), axis, 0)


def _block_diag(blocks):
    g, r, c = blocks.shape
    eye = jnp.eye(g, dtype=blocks.dtype)
    return (blocks[:, :, None, :] * eye[:, None, :, None]).reshape(g * r, g * c)


def _diag_blocks(mat, g):
    r, c = mat.shape[0] // g, mat.shape[1] // g
    eye = jnp.eye(g, dtype=mat.dtype)
    return jnp.sum(mat.reshape(g, r, g, c) * eye[:, None, :, None], axis=2)


def _pad_lanes(a):
    a = a.reshape(1, -1)
    return jnp.pad(a, ((0, 0), (0, LANES - a.shape[1])))


def _head_expand():
    h = jnp.arange(LANES)[:, None]
    ch = jnp.arange(SSM_HEADS * SSM_P)[None, :] // SSM_P
    return (h == ch).astype(F32)


def _rotary_tables(t):
    inv = 10000.0 ** (-jnp.arange(0, RET_DK, 2, dtype=F32) / RET_DK)
    ang = jnp.arange(t).astype(F32)[:, None] * inv[None, :]
    cos, sin = jnp.cos(ang), jnp.sin(ang)
    return jnp.concatenate([cos, cos], axis=1), jnp.concatenate([-sin, sin], axis=1)


def _rms_fwd(x, g, name):
    return _rows(_f_rms, x.shape[0], [(x, D_MODEL, 0)], [g], [D_MODEL], [], name, bf16_outs=(0,))[0]


def _rms_bwd(x, g, dh, dres, name):
    def fn(x_, dh_, dres_, g_):
        _, vjp = jax.vjp(lambda a, b: _rms(a, b), x_, g_)
        dx, dg = vjp(dh_)
        return dx + dres_, dx + dres_, dg
    return _rows(fn, x.shape[0], [(x, D_MODEL, 0), (dh, D_MODEL, 0), (dres, D_MODEL, 0)], [g], [D_MODEL, D_MODEL], [(1, D_MODEL)],
                 name, bf16_outs=(1,))


def _ffn_fwd(x, norm_g, w_up, dw_w, dw_b, w_down, tag):
    t = x.shape[0]
    nbk = D_FF // LANES
    h = _rms_fwd(x, norm_g, "ffn_norm_" + tag)
    up = _mm(h, w_up, "nn", "ffn_up_" + tag)
    mid = _cols(_f_ffnmid, nbk, [(up, 0), (up, nbk), (dw_w, 0), (dw_w, nbk), (dw_b, 0), (dw_b, nbk)], [(t,)], "ffn_mid_" + tag,
                bf16_outs=(0,))[0]
    out = _mm(mid, w_down, "nn", "ffn_down_" + tag, res=x)
    return out, (h, up, mid)


def _ffn_bwd(x, norm_g, w_up, dw_w, dw_b, w_down, saved, dout, dout_bf, tag):
    t = x.shape[0]
    nbk = D_FF // LANES
    h, up, mid = saved
    d_w_down = _mm(mid, dout_bf, "tn", "ffn_down_dw_" + tag, out_dtype=BF16)
    dmid = _mm(dout_bf, w_down, "nt", "ffn_down_dx_" + tag)
    dgin, duin, dwg, dwu, dbg, dbu = _cols(
        _grad_fn(_f_ffnmid, 6, 6), nbk,
        [(up, 0), (up, nbk), (dw_w, 0), (dw_w, nbk), (dw_b, 0), (dw_b, nbk), (dmid, 0)],
        [(t,), (t,), (3, 1), (3, 1), (1,), (1,)], "ffn_mid_bwd_" + tag, bf16_outs=(0, 1))
    dup = jnp.concatenate([dgin, duin], axis=1)
    d_w_up = _mm(h, dup, "tn", "ffn_up_dw_" + tag)
    dh = _mm(dup, w_up, "nt", "ffn_up_dx_" + tag)
    dx, dx_bf, dnorm = _rms_bwd(x, norm_g, dh, dout, "ffn_norm_bwd_" + tag)
    return dx, dx_bf, dict(norm=dnorm, w_up=d_w_up, dw_w=jnp.concatenate([dwg, dwu], axis=2)[:, 0],
                           dw_b=jnp.concatenate([dbg, dbu], axis=1), w_down=d_w_down)


def _local_step(x, tgt, w):
    t = x.shape[0]
    grads = {}
    expand = _head_expand()
    cosf, sins = _rotary_tables(t)
    mix_g = [w['mix_norm'][i:i + 1] for i in range(2)]
    ffn_g = [w['ffn_norm'][i:i + 1] for i in range(2)]
    ffn_dw_w = [w['ffn_dw_w'][i][:, None, :] for i in range(2)]
    ffn_dw_b = [w['ffn_dw_b'][i:i + 1] for i in range(2)]

    w_e = w['e_w_in']
    conv_w = w['e_conv_w'][0][:, None, :]
    conv_b = w['e_conv_b']
    dt_bias, a_log, d_skip = _pad_lanes(w['e_dt_bias']), _pad_lanes(w['e_a_log']), _pad_lanes(w['e_d'])
    h0 = _rms_fwd(x, mix_g[0], "mix_norm_0")
    proj = _mm(h0, w_e, "nn", "e_in")
    qr, kr = _rows(_f_retpre, t, [(proj, 512, 0), (proj, 512, 1), (cosf, LANES, 0), (sins, LANES, 0)], [], [512, 512], [], "ret_pre")
    r = _ret_fwd(qr, kr, proj, t)
    y_ret = _rows(_f_retpost, t, [(r, 1024, 0), (proj, 1024, 2)], [], [1024], [], "ret_post", bf16_outs=(0,))[0]
    xbc_act = _cols(_f_ssdconv, 12, [(proj, 32), (conv_w, 0), (conv_b, 0)], [(t,)], "ssd_conv")[0]
    xdt, da = _rows(_f_ssdpre, t, [(xbc_act, 1024, 0), (proj, LANES, 44)], [dt_bias, a_log, expand], [1024, LANES], [], "ssd_pre")
    a_cum = _cumsum(da, False, "ssd_cumsum")
    a_cum_t = a_cum[:, :SSM_HEADS].T
    yc = _ssd_fwd(xbc_act, xdt, a_cum, a_cum_t, t)
    y_ssm = _rows(_f_ssdpost, t, [(yc, 1024, 0), (xbc_act, 1024, 0), (proj, 1024, 3)], [d_skip, w['e_ssm_norm'], expand],
                  [1024], [], "ssd_post", bf16_outs=(0,))[0]
    mix_e = jnp.concatenate([y_ret, y_ssm], axis=1)
    x1 = _mm(mix_e, w['e_w_out'], "nn", "e_out", res=x)
    x2, ffn0 = _ffn_fwd(x1, ffn_g[0], w['ffn_w_up'][0], ffn_dw_w[0], ffn_dw_b[0], w['ffn_w_down'][0], "0")

    lr, li = w['o_a_re'][0], w['o_a_im'][0]
    ls = w['o_log_step'].reshape(S5_GROUPS, 1)
    b_re3, b_im3 = jnp.transpose(w['o_b_re'][0], (2, 0, 1)), jnp.transpose(w['o_b_im'][0], (2, 0, 1))
    par_ins = [lr, li, ls, b_re3, b_im3]
    whole = lambda a: (a, a.shape, (lambda i, n=a.ndim: (0,) * n))
    par_shapes = [(S5_GROUPS, S5_STATE)] * 2 + [(S5_GROUP, S5_GROUPS, S5_STATE)] * 2
    ab_re, ab_im, bb_re, bb_im = _call(_f_s5par, (1,), [whole(a) for a in par_ins],
                                       [(s, F32, s, (lambda i, n=len(s): (0,) * n), False) for s in par_shapes], "s5_params")
    w_b = jnp.concatenate([_block_diag(jnp.transpose(bb_re, (1, 0, 2))), _block_diag(jnp.transpose(bb_im, (1, 0, 2)))], axis=1)
    w_c = jnp.concatenate([_block_diag(jnp.transpose(w['o_c_re'][0], (0, 2, 1))),
                           -_block_diag(jnp.transpose(w['o_c_im'][0], (0, 2, 1)))], axis=0)
    a_re, a_im = ab_re.reshape(1, S5_COLS), ab_im.reshape(1, S5_COLS)
    dw_w = w['o_dw_w'][0][:, None, :]
    glu_w = w['o_glu_w'].astype(F32)

    h1 = _rms_fwd(x2, mix_g[1], "mix_norm_1")
    proj_o = _mm(h1, w['o_w_in'], "nn", "o_in")
    c1 = _conf_conv(proj_o, dw_w, w['o_dw_b'])
    c2 = _rows(_f_confb, t, [(c1, 512, 0)], [w['o_ln_g'], w['o_ln_b']], [512], [], "conf_norm", bf16_outs=(0,))[0]
    u = proj_o[:, 1024:].astype(BF16)
    bu = _mm(u, w_b, "nn", "s5_bu")
    xs_re, xs_im, xs_re_bf, xs_im_bf = _s5_scan(bu, a_re, a_im, False)
    xs_cat = jnp.concatenate([xs_re, xs_im], axis=1)
    xs_cat_bf = jnp.concatenate([xs_re_bf, xs_im_bf], axis=1)
    y_s5 = _mm(xs_cat_bf, w_c, "nn", "s5_cx")
    s_out = _rows(_f_s5post, t, [(y_s5, 512, 0), (proj_o, 512, 2)], [w['o_d'], glu_w], [512], [], "s5_post", bf16_outs=(0,))[0]
    mix_o = jnp.concatenate([c2, s_out], axis=1)
    x3 = _mm(mix_o, w['o_w_out'], "nn", "o_out", res=x2)
    x4, ffn1 = _ffn_fwd(x3, ffn_g[1], w['ffn_w_up'][1], ffn_dw_w[1], ffn_dw_b[1], w['ffn_w_down'][1], "1")

    dx4, dx4_bf, loss_blk, d_final = _rows(_loss_step, t, [(x4, D_MODEL, 0), (tgt, D_MODEL, 0)], [w['final_norm'].reshape(1, D_MODEL)],
                                           [D_MODEL, D_MODEL], [(8, LANES), (1, D_MODEL)], "loss_head", bf16_outs=(1,))
    loss = loss_blk[0, 0]
    grads['final_norm'] = d_final.reshape(D_MODEL)

    dx3, dx3_bf, g1 = _ffn_bwd(x3, ffn_g[1], w['ffn_w_up'][1], ffn_dw_w[1], ffn_dw_b[1], w['ffn_w_down'][1], ffn1, dx4, dx4_bf, "1")
    grads['o_w_out'] = _mm(mix_o, dx3_bf, "tn", "o_out_dw", out_dtype=BF16)
    dmix_o = _mm(dx3_bf, w['o_w_out'], "nt", "o_out_dx")
    dc1, d_ln_g, d_ln_b = _conf_norm_bwd(c1, w['o_ln_g'], w['o_ln_b'], dmix_o, t)
    dca, dcg, d_dw_w, d_dw_b = _conf_conv(proj_o, dw_w, w['o_dw_b'], dc1)
    dyc, du_skip, d_od, d_glu = _s5_post_bwd(y_s5, proj_o, w['o_d'], glu_w, dmix_o, t)
    grads['o_glu_w'] = d_glu
    d_w_c = _mm(xs_cat_bf, dyc, "tn", "s5_cx_dw")
    dxs = _mm(dyc, w_c, "nt", "s5_cx_dx")
    g_re, g_im, d_are, d_aim = _s5_scan(dxs, a_re, a_im, True, xs_cat)
    g_cat = jnp.concatenate([g_re, g_im], axis=1)
    d_w_b = _mm(u, g_cat, "tn", "s5_bu_dw")
    du = _mm(g_cat, w_b, "nt", "s5_bu_dx", res=du_skip, out_dtype=BF16)
    d_bb_re = jnp.transpose(_diag_blocks(d_w_b[:, :S5_COLS], S5_GROUPS), (1, 0, 2))
    d_bb_im = jnp.transpose(_diag_blocks(d_w_b[:, S5_COLS:], S5_GROUPS), (1, 0, 2))
    par_cts = [d_are.reshape(S5_GROUPS, S5_STATE), d_aim.reshape(S5_GROUPS, S5_STATE), d_bb_re, d_bb_im]
    in_shapes = [a.shape for a in par_ins]
    d_lr, d_li, d_ls, d_br3, d_bi3 = _call(_grad_fn(_f_s5par, 5, 5), (1,), [whole(a) for a in par_ins + par_cts],
                                           [(s, F32, s, (lambda i, n=len(s): (0,) * n), False) for s in in_shapes], "s5_params_bwd")
    grads['o_a_re'], grads['o_a_im'], grads['o_log_step'] = d_lr[None], d_li[None], d_ls.reshape(1, S5_GROUPS)
    grads['o_b_re'], grads['o_b_im'] = jnp.transpose(d_br3, (1, 2, 0))[None], jnp.transpose(d_bi3, (1, 2, 0))[None]
    grads['o_c_re'] = jnp.transpose(_diag_blocks(d_w_c[:S5_COLS], S5_GROUPS), (0, 2, 1))[None]
    grads['o_c_im'] = -jnp.transpose(_diag_blocks(d_w_c[S5_COLS:], S5_GROUPS), (0, 2, 1))[None]
    grads['o_d'], grads['o_ln_g'], grads['o_ln_b'] = d_od, d_ln_g, d_ln_b
    grads['o_dw_w'], grads['o_dw_b'] = d_dw_w[:, 0][None], d_dw_b
    dproj_o = jnp.concatenate([dca, dcg, du], axis=1)
    grads['o_w_in'] = _mm(h1, dproj_o, "tn", "o_in_dw")
    dh1 = _mm(dproj_o, w['o_w_in'], "nt", "o_in_dx")
    dx2, dx2_bf, d_mix1 = _rms_bwd(x2, mix_g[1], dh1, dx3, "mix_norm_bwd_1")

    dx1, dx1_bf, g0 = _ffn_bwd(x1, ffn_g[0], w['ffn_w_up'][0], ffn_dw_w[0], ffn_dw_b[0], w['ffn_w_down'][0], ffn0, dx2, dx2_bf, "0")
    grads['e_w_out'] = _mm(mix_e, dx1_bf, "tn", "e_out_dw", out_dtype=BF16)
    dmix_e = _mm(dx1_bf, w['e_w_out'], "nt", "e_out_dx")
    dr, dg = _rows(_grad_fn(_f_retpost, 2, 2), t, [(r, 1024, 0), (proj, 1024, 2), (dmix_e, 1024, 0)], [], [1024, 1024], [], "ret_post_bwd",
                   bf16_outs=(0, 1))
    dqr, dkr, dv = _ret_bwd(qr, kr, proj, dr, t)
    dq, dk = _ret_pre_bwd(proj, cosf, sins, dqr, dkr, t)
    dyc0, dxs1, dz, d_dskip, d_ssm_norm = _ssd_post_bwd(yc, xbc_act, proj, d_skip, w['e_ssm_norm'], expand, dmix_e, t)
    dcm, da_q, dbm, dxdt, da_k_t = _ssd_bwd(xbc_act, xdt, a_cum, a_cum_t, dyc0, t)
    d_a_cum = da_q - jnp.pad(da_k_t.T, ((0, 0), (0, LANES - SSM_HEADS)))
    dda = _cumsum(d_a_cum, True, "ssd_cumsum_bwd")
    dxs, ddtr, d_dt_bias, d_a_log = _ssd_pre_bwd(xbc_act, proj, dt_bias, a_log, expand, dxdt, dda, dxs1, t)
    dxbc_act = jnp.concatenate([dxs, dbm, dcm], axis=1)
    dxbc, d_conv_w, d_conv_b = _cols(_grad_fn(_f_ssdconv, 3, 3), 12, [(proj, 32), (conv_w, 0), (conv_b, 0), (dxbc_act, 0)],
                                     [(t,), (4, 1), (1,)], "ssd_conv_bwd", bf16_outs=(0,))
    dproj = jnp.concatenate([dq, dk, dv, dg, dz, dxbc, ddtr], axis=1)
    grads['e_w_in'] = _mm(h0, dproj, "tn", "e_in_dw")
    dh0 = _mm(dproj, w_e, "nt", "e_in_dx")
    dx0, _, d_mix0 = _rms_bwd(x, mix_g[0], dh0, dx1, "mix_norm_bwd_0")

    grads['mix_norm'] = jnp.concatenate([d_mix0, d_mix1], axis=0)
    grads['e_conv_w'], grads['e_conv_b'] = d_conv_w[:, 0][None], d_conv_b
    grads['e_dt_bias'], grads['e_a_log'], grads['e_d'] = d_dt_bias[:, :SSM_HEADS], d_a_log[:, :SSM_HEADS], d_dskip[:, :SSM_HEADS]
    grads['e_ssm_norm'] = d_ssm_norm
    grads['ffn_norm'] = jnp.concatenate([g0['norm'], g1['norm']], axis=0)
    grads['ffn_w_up'], grads['ffn_w_down'] = [g0['w_up'], g1['w_up']], [g0['w_down'], g1['w_down']]
    grads['ffn_dw_w'] = jnp.stack([g0['dw_w'], g1['dw_w']], axis=0)
    grads['ffn_dw_b'] = jnp.concatenate([g0['dw_b'], g1['dw_b']], axis=0)
    return loss, dx0, grads


def _conf_norm_bwd(c1, ln_g, ln_b, dmix_o, t):
    def fn(c1_, dy_, g_, b_):
        _, vjp = jax.vjp(lambda a, b, c: _f_confb(a, b, c)[0], c1_, g_, b_)
        return vjp(dy_)
    return _rows(fn, t, [(c1, 512, 0), (dmix_o, 512, 0)], [ln_g, ln_b], [512], [(1, 512), (1, 512)], "conf_norm_bwd")


def _s5_post_bwd(y_s5, proj_o, d_skip, glu_w, dmix_o, t):
    def fn(yc_, u_, dy_, d_, gw_):
        _, vjp = jax.vjp(lambda a, b, c, e: _f_s5post(a, b, c, e)[0], yc_, u_, d_, gw_)
        return vjp(dy_)
    return _rows(fn, t, [(y_s5, 512, 0), (proj_o, 512, 2), (dmix_o, 512, 1)], [d_skip, glu_w], [512, 512], [(1, 512), (512, 512)],
                 "s5_post_bwd", bf16_outs=(0,))


def _ret_pre_bwd(proj, cosf, sins, dqr, dkr, t):
    def fn(q_, k_, cos_, sin_, dq_, dk_):
        _, vjp = jax.vjp(lambda a, b: _f_retpre(a, b, cos_, sin_), q_, k_)
        return vjp((dq_, dk_))
    return _rows(fn, t, [(proj, 512, 0), (proj, 512, 1), (cosf, LANES, 0), (sins, LANES, 0), (dqr, 512, 0), (dkr, 512, 0)], [],
                 [512, 512], [], "ret_pre_bwd", bf16_outs=(0, 1))


def _ssd_post_bwd(yc, xbc_act, proj, d_skip, norm_w, expand, dmix_e, t):
    def fn(yc_, xs_, z_, dy_, d_, nw_, e_):
        _, vjp = jax.vjp(lambda a, b, c, dd, n: _f_ssdpost(a, b, c, dd, n, e_)[0], yc_, xs_, z_, d_, nw_)
        return vjp(dy_)
    return _rows(fn, t, [(yc, 1024, 0), (xbc_act, 1024, 0), (proj, 1024, 3), (dmix_e, 1024, 1)], [d_skip, norm_w, expand],
                 [1024, 1024, 1024], [(1, LANES), (1, 1024)], "ssd_post_bwd", bf16_outs=(0, 2))


def _ssd_pre_bwd(xbc_act, proj, dt_bias, a_log, expand, dxdt, dda, dxs1, t):
    def fn(xs_, dtr_, dx_, dda_, dxs1_, bias_, alog_, e_):
        _, vjp = jax.vjp(lambda a, b, c, dd: _f_ssdpre(a, b, c, dd, e_), xs_, dtr_, bias_, alog_)
        dxs, ddtr, dbias, dalog = vjp((dx_, dda_))
        return dxs + dxs1_, ddtr, dbias, dalog
    return _rows(fn, t, [(xbc_act, 1024, 0), (proj, LANES, 44), (dxdt, 1024, 0), (dda, LANES, 0), (dxs1, 1024, 0)],
                 [dt_bias, a_log, expand], [1024, LANES], [(1, LANES), (1, LANES)], "ssd_pre_bwd", bf16_outs=(1,))


def kernel(x, mix_norm, e_w_in, e_conv_w, e_conv_b, e_dt_bias, e_a_log, e_d, e_ssm_norm, e_w_out, o_w_in, o_dw_w, o_dw_b, o_ln_g, o_ln_b, o_a_re, o_a_im, o_b_re, o_b_im, o_c_re, o_c_im, o_d, o_log_step, o_glu_w, o_w_out, ffn_norm, ffn_w_up, ffn_dw_w, ffn_dw_b, ffn_w_down, final_norm, loss_target, m_mix_norm, m_e_w_in, m_e_conv_w, m_e_conv_b, m_e_dt_bias, m_e_a_log, m_e_d, m_e_ssm_norm, m_e_w_out, m_o_w_in, m_o_dw_w, m_o_dw_b, m_o_ln_g, m_o_ln_b, m_o_a_re, m_o_a_im, m_o_b_re, m_o_b_im, m_o_c_re, m_o_c_im, m_o_d, m_o_log_step, m_o_glu_w, m_o_w_out, m_ffn_norm, m_ffn_w_up, m_ffn_dw_w, m_ffn_dw_b, m_ffn_w_down, m_final_norm, v_mix_norm, v_e_w_in, v_e_conv_w, v_e_conv_b, v_e_dt_bias, v_e_a_log, v_e_d, v_e_ssm_norm, v_e_w_out, v_o_w_in, v_o_dw_w, v_o_dw_b, v_o_ln_g, v_o_ln_b, v_o_a_re, v_o_a_im, v_o_b_re, v_o_b_im, v_o_c_re, v_o_c_im, v_o_d, v_o_log_step, v_o_glu_w, v_o_w_out, v_ffn_norm, v_ffn_w_up, v_ffn_dw_w, v_ffn_dw_b, v_ffn_w_down, v_final_norm):
    p = dict(locals())

    kinds = ("grad_", "delta_", "new_m_", "new_v_")

    def block(name, layer):
        return p[name][0 if layer is None else layer]

    srcs = [block(n, layer).astype(BF16) for n, layer, _ in MATMUL_BLOCKS] + [_pack([p[n] for n in SMALL_SHARDED], F32, 16)]
    gathered = _exchange(srcs, [True] * len(srcs), "gather_weights")
    w = {n: p[n] for n in REPLICATED}
    for (n, layer, by_cols), g in zip(MATMUL_BLOCKS, gathered):
        if by_cols:
            full = _join_cols(g, E_IN_PAD if n == 'e_w_in' else N_DEV * g.shape[2], f"join_{n}_{layer}")
        else:
            full = g.reshape(N_DEV * g.shape[1], g.shape[2])
        if layer is None:
            w[n] = full
        else:
            w.setdefault(n, [None, None])[layer] = full
    for n, piece in zip(SMALL_SHARDED, _unpack(gathered[-1], [p[n].shape for n in SMALL_SHARDED], lead=(N_DEV,))):
        w[n] = _join_shards(piece, SHARDED[n])

    loss, dx, grads = _local_step(x[0], loss_target[0], w)
    loss = lax.psum(loss, MESH_AXES)

    sends = []
    for n, layer, by_cols in MATMUL_BLOCKS:
        g = grads[n] if layer is None else grads[n][layer]
        if by_cols:
            sends.append(_split_cols(g, block(n, layer).shape[1], BF16, f"split_{n}_{layer}"))
        else:
            sends.append(g.astype(BF16).reshape(N_DEV, -1, g.shape[1]))
    sends.append(_pack([_split_shards(grads[n].reshape(p[n].shape[:SHARDED[n]] + (-1,) + p[n].shape[SHARDED[n] + 1:]), SHARDED[n])
                        for n in SMALL_SHARDED], F32, 128, lead=(N_DEV,)))
    sends.append(_pack([grads[n].reshape(p[n].shape) for n in REPLICATED], F32, 128))
    parts = _exchange(sends, [False] * (len(sends) - 1) + [True], "exchange_grads")

    out, by_layer = {}, {}
    for (n, layer, _), part in zip(MATMUL_BLOCKS, parts):
        by_layer.setdefault(n, {})[layer] = _adamw(part, *[block(pre + n, layer) for pre in ("", "m_", "v_")], f"adamw_{n}_{layer}")
    for n, res in by_layer.items():
        for i, kind in enumerate(kinds):
            out[kind + n] = res[None][i][None] if None in res else jnp.stack([res[0][i], res[1][i]], axis=0)
    for names, part, tag in ((SMALL_SHARDED, parts[-2], "small"), (REPLICATED, parts[-1], "replicated")):
        packed = [_pack([p[pre + n] for n in names], F32, 128) for pre in ("", "m_", "v_")]
        for kind, buf in zip(kinds, _adamw(part, *packed, "adamw_" + tag)):
            for n, a in zip(names, _unpack(buf, [p[n].shape for n in names])):
                out[kind + n] = a
    return (loss, dx[None], *[out[kind + n] for kind in kinds for n in WEIGHTS])
```

```python
import functools
import math

import jax
import jax.numpy as jnp
from jax import lax
from jax.experimental import pallas as pl
from jax.experimental.pallas import tpu as pltpu

F32, BF16 = jnp.float32, jnp.bfloat16
HIGHEST = lax.Precision.HIGHEST
N_DEV = 8
MESH_AXES = ("x", "y", "c")
VMEM_LIMIT = 48 * 1024 * 1024
LANES = 128
PACK_COLS = 1024

D_MODEL = 1024
EPS = 1e-6
RET_HEADS, RET_DK, RET_DV = 4, 128, 256
SSM_HEADS, SSM_P, SSM_N, SSM_GROUPS = 16, 64, 128, 2
SSM_HG = SSM_HEADS // SSM_GROUPS
S5_GROUPS, S5_GROUP, S5_STATE = 32, 16, 64
S5_COLS = S5_GROUPS * S5_STATE
D_FF = 2816
E_IN, E_IN_PAD = 5648, 5760
ADAM_LR, ADAM_B1, ADAM_B2, ADAM_EPS, ADAM_WD, ADAM_STEP = 0.001, 0.9, 0.999, 1e-08, 0.01, 10

WEIGHTS = ['mix_norm', 'e_w_in', 'e_conv_w', 'e_conv_b', 'e_dt_bias', 'e_a_log', 'e_d', 'e_ssm_norm', 'e_w_out', 'o_w_in', 'o_dw_w', 'o_dw_b', 'o_ln_g', 'o_ln_b', 'o_a_re', 'o_a_im', 'o_b_re', 'o_b_im', 'o_c_re', 'o_c_im', 'o_d', 'o_log_step', 'o_glu_w', 'o_w_out', 'ffn_norm', 'ffn_w_up', 'ffn_dw_w', 'ffn_dw_b', 'ffn_w_down', 'final_norm']
SHARDED = {'e_w_in': 2, 'e_conv_w': 2, 'e_w_out': 1, 'o_w_in': 2, 'o_dw_w': 2, 'o_dw_b': 1, 'o_ln_g': 1, 'o_ln_b': 1,
           'o_d': 1, 'o_glu_w': 1, 'o_w_out': 1, 'ffn_w_up': 2, 'ffn_dw_w': 2, 'ffn_w_down': 1}
MATMUL_WEIGHTS = ['e_w_in', 'e_w_out', 'o_w_in', 'o_glu_w', 'o_w_out', 'ffn_w_up', 'ffn_w_down']
MATMUL_BLOCKS = [('e_w_in', None, True), ('e_w_out', None, False), ('o_w_in', None, True), ('o_glu_w', None, False),
                 ('o_w_out', None, False), ('ffn_w_up', 0, True), ('ffn_w_up', 1, True), ('ffn_w_down', 0, False), ('ffn_w_down', 1, False)]
SMALL_SHARDED = [n for n in WEIGHTS if n in SHARDED and n not in MATMUL_WEIGHTS]
REPLICATED = [n for n in WEIGHTS if n not in SHARDED]


def _call(fn, grid, ins, outs, name):
    n_in = len(ins)

    def body(*refs):
        vals = fn(*[r[...] for r in refs[:n_in]])
        first = pl.program_id(0) == 0
        for r, v, o in zip(refs[n_in:], vals, outs):
            if o[4]:
                @pl.when(first)
                def _():
                    r[...] = jnp.zeros_like(r)
                r[...] += v.astype(r.dtype)
            else:
                r[...] = v.astype(r.dtype)

    return pl.pallas_call(
        body, grid=grid,
        in_specs=[pl.BlockSpec(b, m) for _, b, m in ins],
        out_specs=[pl.BlockSpec(o[2], o[3]) for o in outs],
        out_shape=[jax.ShapeDtypeStruct(o[0], o[1]) for o in outs],
        compiler_params=pltpu.CompilerParams(dimension_semantics=("arbitrary",) * len(grid), vmem_limit_bytes=VMEM_LIMIT),
        name=name)(*[a for a, _, _ in ins])


def _rows(fn, n_rows, row_ins, full_ins, row_outs, acc_outs, name, tm=256, bf16_outs=()):
    tm = min(tm, n_rows)
    ins = [(a, (tm, w), (lambda i, c=c: (i, c))) for a, w, c in row_ins]
    ins += [(a, a.shape, (lambda i, n=a.ndim: (0,) * n)) for a in full_ins]
    outs = [((n_rows, w), BF16 if k in bf16_outs else F32, (tm, w), (lambda i: (i, 0)), False) for k, w in enumerate(row_outs)]
    outs += [(tuple(s), F32, tuple(s), (lambda i, n=len(s): (0,) * n), True) for s in acc_outs]
    return _call(fn, (n_rows // tm,), ins, outs, name)


def _cols(fn, n_blocks, col_ins, out_leads, name, cb=LANES, bf16_outs=()):
    ins = [(a, a.shape[:-1] + (cb,), (lambda j, n=a.ndim, o=o: (0,) * (n - 1) + (j + o,))) for a, o in col_ins]
    outs = [(tuple(s) + (n_blocks * cb,), BF16 if k in bf16_outs else F32, tuple(s) + (cb,), (lambda j, n=len(s): (0,) * n + (j,)), False)
            for k, s in enumerate(out_leads)]
    return _call(fn, (n_blocks,), ins, outs, name)


def _grad_fn(f, n_diff, n_in):
    def g(*a):
        diff, consts, cts = a[:n_diff], a[n_diff:n_in], a[n_in:]
        _, vjp = jax.vjp(lambda *d: f(*d, *consts), *diff)
        return vjp(tuple(cts))
    return g


def _silu(x):
    return x * jax.nn.sigmoid(x)


def _rms(x, g):
    return x * lax.rsqrt(jnp.mean(x * x, axis=-1, keepdims=True) + EPS) * g


@jax.custom_vjp
def _softplus(x):
    return jnp.maximum(x, 0.0) + jnp.log(1.0 + jnp.exp(-jnp.abs(x)))


_softplus.defvjp(lambda x: (_softplus(x), x), lambda x, g: (g * jax.nn.sigmoid(x),))


@jax.custom_vjp
def _swap_halves(x):
    return pltpu.roll(x, 64, 1)


_swap_halves.defvjp(lambda x: (_swap_halves(x), None), lambda _, g: (_swap_halves(g),))


def _shift_rows(x, k, up):
    if k == 0:
        return x
    n = x.shape[0]
    t = lax.broadcasted_iota(jnp.int32, x.shape, 0)
    if up:
        return jnp.where(t < n - k, pltpu.roll(x, n - k, 0), 0.0)
    return jnp.where(t >= k, pltpu.roll(x, k, 0), 0.0)


@jax.custom_vjp
def _dwconv(x, w, b):
    k_taps = w.shape[0]
    y = b + w[k_taps - 1] * x
    for k in range(k_taps - 1):
        y = y + w[k] * _shift_rows(x, k_taps - 1 - k, False)
    return y


def _dwconv_fwd(x, w, b):
    return _dwconv(x, w, b), (x, w)


def _dwconv_bwd(saved, dy):
    x, w = saved
    k_taps = w.shape[0]
    dx = w[k_taps - 1] * dy
    dws = []
    for k in range(k_taps - 1):
        s = k_taps - 1 - k
        dx = dx + w[k] * _shift_rows(dy, s, True)
        dws.append(jnp.sum(dy * _shift_rows(x, s, False), axis=0, keepdims=True)[None])
    dws.append(jnp.sum(dy * x, axis=0, keepdims=True)[None])
    return dx, jnp.concatenate(dws, axis=0), jnp.sum(dy, axis=0, keepdims=True)


_dwconv.defvjp(_dwconv_fwd, _dwconv_bwd)


def _f_rms(x, g):
    return (_rms(x, g),)


def _rot(x, cosf, sins):
    outs = []
    for h in range(RET_HEADS):
        xh = x[:, h * RET_DK:(h + 1) * RET_DK]
        outs.append(xh * cosf + _swap_halves(xh) * sins)
    return jnp.concatenate(outs, axis=1)


def _f_retpre(q, k, cosf, sins):
    return _rot(q, cosf, sins), _rot(k, cosf, sins) * (RET_DK ** -0.5)


def _f_retpost(r, g):
    outs = []
    for h in range(RET_HEADS):
        rh = r[:, h * RET_DV:(h + 1) * RET_DV]
        rc = rh - jnp.mean(rh, axis=-1, keepdims=True)
        outs.append(_silu(g[:, h * RET_DV:(h + 1) * RET_DV]) * (rc * lax.rsqrt(jnp.mean(rc * rc, axis=-1, keepdims=True) + EPS)))
    return (jnp.concatenate(outs, axis=1),)


def _f_ssdconv(xbc, w, b):
    return (_silu(_dwconv(xbc, w, b)),)


def _f_ssdpre(xs, dtr, bias, alog, expand):
    dt = _softplus(dtr + bias)
    return xs * jnp.dot(dt, expand, precision=HIGHEST, preferred_element_type=F32), dt * (-jnp.exp(alog))


def _f_ssdpost(yc, xs, z, dskip, norm_w, expand):
    d_wide = jnp.dot(jnp.broadcast_to(dskip, (yc.shape[0], LANES)), expand, precision=HIGHEST, preferred_element_type=F32)
    y = (yc + d_wide * xs) * _silu(z)
    half = y.shape[1] // SSM_GROUPS
    outs = []
    for g in range(SSM_GROUPS):
        yg = y[:, g * half:(g + 1) * half]
        outs.append(yg * lax.rsqrt(jnp.mean(yg * yg, axis=-1, keepdims=True) + EPS))
    return (jnp.concatenate(outs, axis=1) * norm_w,)


def _f_ffnmid(gin, uin, wg, wu, bg, bu):
    return (_silu(_dwconv(gin, wg, bg)) * _dwconv(uin, wu, bu),)


def _f_confb(c1, g, b):
    mu = jnp.mean(c1, axis=-1, keepdims=True)
    xc = c1 - mu
    return (_silu(xc * lax.rsqrt(jnp.mean(xc * xc, axis=-1, keepdims=True) + EPS) * g + b),)


def _f_s5post(yc, u, dskip, glu_w):
    s = jax.nn.gelu(yc + dskip * u)
    z = jnp.dot(s.astype(BF16), glu_w.astype(BF16), preferred_element_type=F32)
    return (s * jax.nn.sigmoid(z),)


def _f_s5par(lr, li, ls, br, bi):
    step = jnp.exp(ls)
    mag = jnp.exp(lr * step)
    ab_re = mag * jnp.cos(li * step)
    ab_im = mag * jnp.sin(li * step)
    den = lr * lr + li * li
    f_re = ((ab_re - 1.0) * lr + ab_im * li) / den
    f_im = (ab_im * lr - (ab_re - 1.0) * li) / den
    return ab_re, ab_im, f_re[None] * br - f_im[None] * bi, f_re[None] * bi + f_im[None] * br


def _loss_step(x, tgt, g):
    def f(x_, g_):
        e = _rms(x_, g_) - tgt
        return 0.5 * jnp.sum(jnp.mean(e * e, axis=-1, keepdims=True), axis=0, keepdims=True)
    loss, vjp = jax.vjp(f, x, g)
    dx, dg = vjp(jnp.ones((1, 1), F32))
    return dx, dx, jnp.broadcast_to(loss, (8, LANES)), dg


def _tile(n, pref):
    if n <= pref:
        return n
    t = (pref // LANES) * LANES
    while n % t:
        t -= LANES
    return t


_DIMS = {"nn": (((1,), (0,)), ((), ())), "nt": (((1,), (1,)), ((), ())), "tn": (((0,), (0,)), ((), ()))}


def _mm(a, b, mode, name, res=None, out_dtype=F32, tm=1024, tn=1408, tk=1408):
    if mode == "nn":
        (m, k), n = a.shape, b.shape[1]
    elif mode == "nt":
        (m, k), n = a.shape, b.shape[0]
    else:
        (k, m), n = a.shape, b.shape[1]
    tm, tn, tk = _tile(m, tm), _tile(n, tn), _tile(k, tk)
    nk = k // tk
    a_spec = pl.BlockSpec((tk, tm), lambda i, j, kk: (kk, i)) if mode == "tn" else pl.BlockSpec((tm, tk), lambda i, j, kk: (i, kk))
    b_spec = pl.BlockSpec((tn, tk), lambda i, j, kk: (j, kk)) if mode == "nt" else pl.BlockSpec((tk, tn), lambda i, j, kk: (kk, j))
    o_spec = pl.BlockSpec((tm, tn), lambda i, j, kk: (i, j))
    has_res = res is not None

    def body(*refs):
        a_ref, b_ref = refs[0], refs[1]
        o_ref, acc = refs[-2], refs[-1]
        kk = pl.program_id(2)

        @pl.when(kk == 0)
        def _():
            acc[...] = jnp.zeros_like(acc)

        acc[...] += lax.dot_general(a_ref[...].astype(BF16), b_ref[...].astype(BF16), _DIMS[mode], preferred_element_type=F32)

        @pl.when(kk == nk - 1)
        def _():
            o_ref[...] = (acc[...] + refs[2][...] if has_res else acc[...]).astype(out_dtype)

    return pl.pallas_call(
        body, grid=(m // tm, n // tn, nk),
        in_specs=[a_spec, b_spec] + ([o_spec] if has_res else []),
        out_specs=o_spec, out_shape=jax.ShapeDtypeStruct((m, n), out_dtype),
        scratch_shapes=[pltpu.VMEM((tm, tn), F32)],
        compiler_params=pltpu.CompilerParams(dimension_semantics=("parallel", "parallel", "arbitrary"), vmem_limit_bytes=VMEM_LIMIT),
        name=name)(*([a, b] + ([res] if has_res else [])))


def _seq_block(t):
    return min(512, t)


def _causal_diff(i, j, blk):
    r = lax.broadcasted_iota(jnp.int32, (blk, blk), 0)
    c = lax.broadcasted_iota(jnp.int32, (blk, blk), 1)
    return (i - j) * blk + r - c


def _ret_decay(lg, i, j, blk):
    diff = _causal_diff(i, j, blk)
    return jnp.where(diff >= 0, jnp.exp(lg * jnp.maximum(diff, 0).astype(F32)), 0.0)


def _ret_row_decays(lg, i, j, blk):
    row = lax.broadcasted_iota(jnp.int32, (blk, RET_DK), 0)
    return jnp.exp(lg * row.astype(F32)), jnp.exp(lg * ((i - j) * blk - row).astype(F32))


def _ret_scaled(lg, i, j, blk, q_ref, k_ref):
    a, b = _ret_row_decays(lg, i, j, blk)
    return (q_ref[...] * a).astype(BF16), (k_ref[...] * b).astype(BF16)


def _ret_log_gamma():
    return jnp.log1p(-(2.0 ** (-5.0 - jnp.arange(RET_HEADS, dtype=F32))))


def _ret_fwd(qr, kr, proj, t):
    blk = _seq_block(t)
    nb = t // blk
    v_off = (2 * RET_HEADS * RET_DK) // RET_DV

    def body(lg_ref, q_ref, k_ref, v_ref, o_ref, acc):
        h, i, j = pl.program_id(0), pl.program_id(1), pl.program_id(2)

        @pl.when(j == 0)
        def _():
            acc[...] = jnp.zeros_like(acc)

        @pl.when(j < i)
        def _():
            qa, kb = _ret_scaled(lg_ref[h], i, j, blk, q_ref, k_ref)
            p = lax.dot_general(qa, kb, _DIMS["nt"], preferred_element_type=F32).astype(BF16)
            acc[...] += jnp.dot(p, v_ref[...].astype(BF16), preferred_element_type=F32)

        @pl.when(j == i)
        def _():
            s = lax.dot_general(q_ref[...].astype(BF16), k_ref[...].astype(BF16), _DIMS["nt"], preferred_element_type=F32)
            p = (s * _ret_decay(lg_ref[h], i, j, blk)).astype(BF16)
            o_ref[...] = acc[...] + jnp.dot(p, v_ref[...].astype(BF16), preferred_element_type=F32)

    return pl.pallas_call(
        body, grid=(RET_HEADS, nb, nb),
        in_specs=[pl.BlockSpec(memory_space=pltpu.SMEM),
                  pl.BlockSpec((blk, RET_DK), lambda h, i, j: (i, h)),
                  pl.BlockSpec((blk, RET_DK), lambda h, i, j: (jnp.minimum(j, i), h)),
                  pl.BlockSpec((blk, RET_DV), lambda h, i, j: (jnp.minimum(j, i), v_off + h))],
        out_specs=pl.BlockSpec((blk, RET_DV), lambda h, i, j: (i, h)),
        out_shape=jax.ShapeDtypeStruct((t, RET_HEADS * RET_DV), F32),
        scratch_shapes=[pltpu.VMEM((blk, RET_DV), F32)],
        compiler_params=pltpu.CompilerParams(dimension_semantics=("arbitrary",) * 3, vmem_limit_bytes=VMEM_LIMIT),
        name="ret_fwd")(_ret_log_gamma(), qr, kr, proj)


def _ret_bwd(qr, kr, proj, dr, t):
    blk = _seq_block(t)
    nb = t // blk
    v_off = (2 * RET_HEADS * RET_DK) // RET_DV

    def dq_body(lg_ref, q_ref, k_ref, v_ref, do_ref, dq_ref, acc):
        h, i, j = pl.program_id(0), pl.program_id(1), pl.program_id(2)

        @pl.when(j == 0)
        def _():
            acc[...] = jnp.zeros_like(acc)

        @pl.when(j < i)
        def _():
            a, b = _ret_row_decays(lg_ref[h], i, j, blk)
            ds = lax.dot_general(do_ref[...].astype(BF16), v_ref[...].astype(BF16), _DIMS["nt"], preferred_element_type=F32)
            acc[...] += a * jnp.dot(ds.astype(BF16), (k_ref[...] * b).astype(BF16), preferred_element_type=F32)

        @pl.when(j == i)
        def _():
            ds = lax.dot_general(do_ref[...].astype(BF16), v_ref[...].astype(BF16), _DIMS["nt"], preferred_element_type=F32)
            dsm = (ds * _ret_decay(lg_ref[h], i, j, blk)).astype(BF16)
            dq_ref[...] = acc[...] + jnp.dot(dsm, k_ref[...].astype(BF16), preferred_element_type=F32)

    dq = pl.pallas_call(
        dq_body, grid=(RET_HEADS, nb, nb),
        in_specs=[pl.BlockSpec(memory_space=pltpu.SMEM),
                  pl.BlockSpec((blk, RET_DK), lambda h, i, j: (i, h)),
                  pl.BlockSpec((blk, RET_DK), lambda h, i, j: (jnp.minimum(j, i), h)),
                  pl.BlockSpec((blk, RET_DV), lambda h, i, j: (jnp.minimum(j, i), v_off + h)),
                  pl.BlockSpec((blk, RET_DV), lambda h, i, j: (i, h))],
        out_specs=pl.BlockSpec((blk, RET_DK), lambda h, i, j: (i, h)),
        out_shape=jax.ShapeDtypeStruct((t, RET_HEADS * RET_DK), F32),
        scratch_shapes=[pltpu.VMEM((blk, RET_DK), F32)],
        compiler_params=pltpu.CompilerParams(dimension_semantics=("arbitrary",) * 3, vmem_limit_bytes=VMEM_LIMIT),
        name="ret_bwd_dq")(_ret_log_gamma(), qr, kr, proj, dr)

    def dkv_body(lg_ref, q_ref, k_ref, v_ref, do_ref, dk_ref, dv_ref, acc_k, acc_v):
        h, j, i = pl.program_id(0), pl.program_id(1), pl.program_id(2)

        @pl.when(i == 0)
        def _():
            acc_k[...] = jnp.zeros_like(acc_k)
            acc_v[...] = jnp.zeros_like(acc_v)

        @pl.when(i > j)
        def _():
            a, b = _ret_row_decays(lg_ref[h], i, j, blk)
            qa, kb = (q_ref[...] * a).astype(BF16), (k_ref[...] * b).astype(BF16)
            do = do_ref[...].astype(BF16)
            p = lax.dot_general(qa, kb, _DIMS["nt"], preferred_element_type=F32).astype(BF16)
            acc_v[...] += lax.dot_general(p, do, _DIMS["tn"], preferred_element_type=F32)
            ds = lax.dot_general(do, v_ref[...].astype(BF16), _DIMS["nt"], preferred_element_type=F32).astype(BF16)
            acc_k[...] += b * lax.dot_general(ds, qa, _DIMS["tn"], preferred_element_type=F32)

        @pl.when(i == j)
        def _():
            q = q_ref[...].astype(BF16)
            do = do_ref[...].astype(BF16)
            decay = _ret_decay(lg_ref[h], i, j, blk)
            s = lax.dot_general(q, k_ref[...].astype(BF16), _DIMS["nt"], preferred_element_type=F32)
            acc_v[...] += lax.dot_general((s * decay).astype(BF16), do, _DIMS["tn"], preferred_element_type=F32)
            ds = lax.dot_general(do, v_ref[...].astype(BF16), _DIMS["nt"], preferred_element_type=F32)
            acc_k[...] += lax.dot_general((ds * decay).astype(BF16), q, _DIMS["tn"], preferred_element_type=F32)

        @pl.when(i == nb - 1)
        def _():
            dk_ref[...] = acc_k[...]
            dv_ref[...] = acc_v[...].astype(BF16)

    dk, dv = pl.pallas_call(
        dkv_body, grid=(RET_HEADS, nb, nb),
        in_specs=[pl.BlockSpec(memory_space=pltpu.SMEM),
                  pl.BlockSpec((blk, RET_DK), lambda h, j, i: (jnp.maximum(i, j), h)),
                  pl.BlockSpec((blk, RET_DK), lambda h, j, i: (j, h)),
                  pl.BlockSpec((blk, RET_DV), lambda h, j, i: (j, v_off + h)),
                  pl.BlockSpec((blk, RET_DV), lambda h, j, i: (jnp.maximum(i, j), h))],
        out_specs=[pl.BlockSpec((blk, RET_DK), lambda h, j, i: (j, h)), pl.BlockSpec((blk, RET_DV), lambda h, j, i: (j, h))],
        out_shape=[jax.ShapeDtypeStruct((t, RET_HEADS * RET_DK), F32), jax.ShapeDtypeStruct((t, RET_HEADS * RET_DV), BF16)],
        scratch_shapes=[pltpu.VMEM((blk, RET_DK), F32), pltpu.VMEM((blk, RET_DV), F32)],
        compiler_params=pltpu.CompilerParams(dimension_semantics=("arbitrary",) * 3, vmem_limit_bytes=VMEM_LIMIT),
        name="ret_bwd_dkv")(_ret_log_gamma(), qr, kr, proj, dr)
    return dq, dk, dv


def _cumsum(x, reverse, name):
    t = x.shape[0]
    blk = _seq_block(t)
    nb = t // blk

    def body(x_ref, o_ref, carry):
        @pl.when(pl.program_id(0) == 0)
        def _():
            carry[...] = jnp.zeros_like(carry)

        r = lax.broadcasted_iota(jnp.int32, (blk, blk), 0)
        c = lax.broadcasted_iota(jnp.int32, (blk, blk), 1)
        tri = ((r <= c) if reverse else (r >= c)).astype(F32)
        o_ref[...] = jnp.dot(tri, x_ref[...], precision=HIGHEST, preferred_element_type=F32) + carry[...]
        carry[...] = o_ref[0:1, :] if reverse else o_ref[blk - 1:blk, :]

    idx = (lambda i: (nb - 1 - i, 0)) if reverse else (lambda i: (i, 0))
    return pl.pallas_call(
        body, grid=(nb,), in_specs=[pl.BlockSpec((blk, LANES), idx)], out_specs=pl.BlockSpec((blk, LANES), idx),
        out_shape=jax.ShapeDtypeStruct((t, LANES), F32), scratch_shapes=[pltpu.VMEM((1, LANES), F32)],
        compiler_params=pltpu.CompilerParams(dimension_semantics=("arbitrary",), vmem_limit_bytes=VMEM_LIMIT),
        name=name)(x)


def _ssd_decay(a_ref, at_ref, hh, mask):
    return jnp.exp(jnp.where(mask, a_ref[:, hh:hh + 1] - at_ref[hh:hh + 1, :], -jnp.inf))


def _head_lanes(s):
    lane = lax.broadcasted_iota(jnp.int32, (s.shape[0], LANES), 1)
    return jnp.concatenate([jnp.where(lane < SSM_P, s[:, 2 * p:2 * p + 1], s[:, 2 * p + 1:2 * p + 2])
                            for p in range(SSM_HEADS // 2)], axis=1)


_B_OFF, _C_OFF = 1024 // SSM_N, 1024 // SSM_N + SSM_GROUPS


def _ssd_fwd(xbc_act, xdt, a_cum, a_cum_t, t):
    blk = _seq_block(t)
    nb = t // blk

    def body(c0, c1, b0, b1, x_ref, ai_ref, aj_ref, at_ref, o_ref, acc):
        i, j = pl.program_id(0), pl.program_id(1)

        @pl.when(j == 0)
        def _():
            acc[...] = jnp.zeros_like(acc)

        @pl.when(j < i)
        def _():
            xv = (x_ref[...] * _head_lanes(jnp.exp(ai_ref[0:1, :] - aj_ref[...]))).astype(BF16)
            for g, (c_ref, b_ref) in enumerate(((c0, b0), (c1, b1))):
                cb = lax.dot_general(c_ref[...].astype(BF16), b_ref[...].astype(BF16), _DIMS["nt"], preferred_element_type=F32)
                cols = slice(g * half, (g + 1) * half)
                acc[:, cols] += jnp.dot(cb.astype(BF16), xv[:, cols], preferred_element_type=F32)

        @pl.when(j == i)
        def _():
            o_ref[...] = acc[...] * _head_lanes(jnp.exp(ai_ref[...] - ai_ref[0:1, :]))
            mask = _causal_diff(i, j, blk) >= 0
            for g, (c_ref, b_ref) in enumerate(((c0, b0), (c1, b1))):
                cb = lax.dot_general(c_ref[...].astype(BF16), b_ref[...].astype(BF16), _DIMS["nt"], preferred_element_type=F32)
                for h in range(SSM_HG):
                    hh = g * SSM_HG + h
                    cols = slice(hh * SSM_P, (hh + 1) * SSM_P)
                    m = (cb * _ssd_decay(ai_ref, at_ref, hh, mask)).astype(BF16)
                    o_ref[:, cols] += jnp.dot(m, x_ref[:, cols].astype(BF16), preferred_element_type=F32)

    half = SSM_HG * SSM_P
    row_i = lambda off: pl.BlockSpec((blk, SSM_N), lambda i, j, off=off: (i, off))
    row_j = lambda off: pl.BlockSpec((blk, SSM_N), lambda i, j, off=off: (jnp.minimum(j, i), off))
    return pl.pallas_call(
        body, grid=(nb, nb),
        in_specs=[row_i(_C_OFF), row_i(_C_OFF + 1), row_j(_B_OFF), row_j(_B_OFF + 1),
                  pl.BlockSpec((blk, SSM_HEADS * SSM_P), lambda i, j: (jnp.minimum(j, i), 0)),
                  pl.BlockSpec((blk, LANES), lambda i, j: (i, 0)),
                  pl.BlockSpec((blk, LANES), lambda i, j: (jnp.minimum(j, i), 0)),
                  pl.BlockSpec((SSM_HEADS, blk), lambda i, j: (0, jnp.minimum(j, i)))],
        out_specs=pl.BlockSpec((blk, SSM_HEADS * SSM_P), lambda i, j: (i, 0)),
        out_shape=jax.ShapeDtypeStruct((t, SSM_HEADS * SSM_P), F32),
        scratch_shapes=[pltpu.VMEM((blk, SSM_HEADS * SSM_P), F32)],
        compiler_params=pltpu.CompilerParams(dimension_semantics=("arbitrary",) * 2, vmem_limit_bytes=VMEM_LIMIT),
        name="ssd_fwd")(xbc_act, xbc_act, xbc_act, xbc_act, xdt, a_cum, a_cum, a_cum_t)


def _ssd_bwd(xbc_act, xdt, a_cum, a_cum_t, dy, expand, t):
    blk = _seq_block(t)
    nb = t // blk
    width = SSM_HEADS * SSM_P

    half = SSM_HG * SSM_P

    def head_sums(prod, e_ref):
        return lax.dot_general(prod, e_ref[...], _DIMS["nt"], precision=HIGHEST, preferred_element_type=F32)

    def q_body(c0, c1, b0, b1, x_ref, ai_ref, aj_ref, at_ref, dy_ref, e_ref, dc_ref, da_ref, acc_c, acc_a, acc_p, dyu):
        i, j = pl.program_id(0), pl.program_id(1)

        @pl.when(j == 0)
        def _():
            acc_c[...] = jnp.zeros_like(acc_c)
            acc_a[...] = jnp.zeros_like(acc_a)
            acc_p[...] = jnp.zeros_like(acc_p)
            dyu[...] = (dy_ref[...].astype(F32) * _head_lanes(jnp.exp(ai_ref[...] - ai_ref[0:1, :]))).astype(BF16)

        @pl.when(j < i)
        def _():
            xv = (x_ref[...] * _head_lanes(jnp.exp(ai_ref[0:1, :] - aj_ref[...]))).astype(BF16)
            for g, (c_ref, b_ref) in enumerate(((c0, b0), (c1, b1))):
                cols = slice(g * half, (g + 1) * half)
                bj = b_ref[...].astype(BF16)
                cb = lax.dot_general(c_ref[...].astype(BF16), bj, _DIMS["nt"], preferred_element_type=F32).astype(BF16)
                dcb = lax.dot_general(dyu[:, cols], xv[:, cols], _DIMS["nt"], preferred_element_type=F32)
                acc_c[:, g * SSM_N:(g + 1) * SSM_N] += jnp.dot(dcb.astype(BF16), bj, preferred_element_type=F32)
                acc_p[:, cols] += dyu[:, cols].astype(F32) * jnp.dot(cb, xv[:, cols], preferred_element_type=F32)

        @pl.when(j == i)
        def _():
            mask = _causal_diff(i, j, blk) >= 0
            for g, (c_ref, b_ref) in enumerate(((c0, b0), (c1, b1))):
                bj = b_ref[...].astype(BF16)
                cb = lax.dot_general(c_ref[...].astype(BF16), bj, _DIMS["nt"], preferred_element_type=F32)
                dcb = jnp.zeros((blk, blk), F32)
                for h in range(SSM_HG):
                    hh = g * SSM_HG + h
                    cols = slice(hh * SSM_P, (hh + 1) * SSM_P)
                    dm = lax.dot_general(dy_ref[:, cols].astype(BF16), x_ref[:, cols].astype(BF16), _DIMS["nt"],
                                         preferred_element_type=F32) * _ssd_decay(ai_ref, at_ref, hh, mask)
                    dcb = dcb + dm
                    acc_a[:, hh:hh + 1] += jnp.sum(dm * cb, axis=1, keepdims=True)
                acc_c[:, g * SSM_N:(g + 1) * SSM_N] += jnp.dot(dcb.astype(BF16), bj, preferred_element_type=F32)
            dc_ref[...] = acc_c[...]
            da_ref[...] = acc_a[...] + head_sums(acc_p[...], e_ref)

    row_i = lambda off: pl.BlockSpec((blk, SSM_N), lambda i, j, off=off: (i, off))
    row_j = lambda off: pl.BlockSpec((blk, SSM_N), lambda i, j, off=off: (jnp.minimum(j, i), off))
    e_spec = pl.BlockSpec((LANES, width), lambda i, j: (0, 0))
    dc, da_q = pl.pallas_call(
        q_body, grid=(nb, nb),
        in_specs=[row_i(_C_OFF), row_i(_C_OFF + 1), row_j(_B_OFF), row_j(_B_OFF + 1),
                  pl.BlockSpec((blk, width), lambda i, j: (jnp.minimum(j, i), 0)),
                  pl.BlockSpec((blk, LANES), lambda i, j: (i, 0)),
                  pl.BlockSpec((blk, LANES), lambda i, j: (jnp.minimum(j, i), 0)),
                  pl.BlockSpec((SSM_HEADS, blk), lambda i, j: (0, jnp.minimum(j, i))),
                  pl.BlockSpec((blk, width), lambda i, j: (i, 0)), e_spec],
        out_specs=[pl.BlockSpec((blk, SSM_GROUPS * SSM_N), lambda i, j: (i, 0)), pl.BlockSpec((blk, LANES), lambda i, j: (i, 0))],
        out_shape=[jax.ShapeDtypeStruct((t, SSM_GROUPS * SSM_N), F32), jax.ShapeDtypeStruct((t, LANES), F32)],
        scratch_shapes=[pltpu.VMEM((blk, SSM_GROUPS * SSM_N), F32), pltpu.VMEM((blk, LANES), F32), pltpu.VMEM((blk, width), F32),
                        pltpu.VMEM((blk, width), BF16)],
        compiler_params=pltpu.CompilerParams(dimension_semantics=("arbitrary",) * 2, vmem_limit_bytes=VMEM_LIMIT),
        name="ssd_bwd_q")(xbc_act, xbc_act, xbc_act, xbc_act, xdt, a_cum, a_cum, a_cum_t, dy, expand)

    def k_body(c0, c1, b0, b1, x_ref, ai_ref, aj_ref, at_ref, dy_ref, e_ref, db_ref, dx_ref, da_ref, dat_ref, acc_b, acc_x, acc_a, acc_p):
        j, i = pl.program_id(0), pl.program_id(1)

        @pl.when(i == 0)
        def _():
            acc_b[...] = jnp.zeros_like(acc_b)
            acc_x[...] = jnp.zeros_like(acc_x)
            acc_a[...] = jnp.zeros_like(acc_a)
            acc_p[...] = jnp.zeros_like(acc_p)

        @pl.when(i > j)
        def _():
            v = _head_lanes(jnp.exp(ai_ref[0:1, :] - aj_ref[...]))
            dyu_all = (dy_ref[...].astype(F32) * _head_lanes(jnp.exp(ai_ref[...] - ai_ref[0:1, :]))).astype(BF16)
            xv = (x_ref[...] * v).astype(BF16)
            for g, (c_ref, b_ref) in enumerate(((c0, b0), (c1, b1))):
                cols = slice(g * half, (g + 1) * half)
                ci = c_ref[...].astype(BF16)
                cb = lax.dot_general(ci, b_ref[...].astype(BF16), _DIMS["nt"], preferred_element_type=F32).astype(BF16)
                dcb = lax.dot_general(dyu_all[:, cols], xv[:, cols], _DIMS["nt"], preferred_element_type=F32)
                acc_b[:, g * SSM_N:(g + 1) * SSM_N] += lax.dot_general(dcb.astype(BF16), ci, _DIMS["tn"], preferred_element_type=F32)
                dxv = lax.dot_general(cb, dyu_all[:, cols], _DIMS["tn"], preferred_element_type=F32)
                acc_x[:, cols] += v[:, cols] * dxv
                acc_p[:, cols] += xv[:, cols].astype(F32) * dxv

        @pl.when(i == j)
        def _():
            mask = _causal_diff(i, j, blk) >= 0
            for g, (c_ref, b_ref) in enumerate(((c0, b0), (c1, b1))):
                ci = c_ref[...].astype(BF16)
                cb = lax.dot_general(ci, b_ref[...].astype(BF16), _DIMS["nt"], preferred_element_type=F32)
                dcb = jnp.zeros((blk, blk), F32)
                for h in range(SSM_HG):
                    hh = g * SSM_HG + h
                    cols = slice(hh * SSM_P, (hh + 1) * SSM_P)
                    decay = _ssd_decay(ai_ref, at_ref, hh, mask)
                    dyh = dy_ref[:, cols].astype(BF16)
                    acc_x[:, cols] += lax.dot_general((cb * decay).astype(BF16), dyh, _DIMS["tn"], preferred_element_type=F32)
                    dm = lax.dot_general(dyh, x_ref[:, cols].astype(BF16), _DIMS["nt"], preferred_element_type=F32) * decay
                    dcb = dcb + dm
                    acc_a[hh:hh + 1, :] += jnp.sum(dm * cb, axis=0, keepdims=True)
                acc_b[:, g * SSM_N:(g + 1) * SSM_N] += lax.dot_general(dcb.astype(BF16), ci, _DIMS["tn"], preferred_element_type=F32)

        @pl.when(i == nb - 1)
        def _():
            db_ref[...] = acc_b[...]
            dx_ref[...] = acc_x[...]
            dat_ref[...] = acc_a[...]
            da_ref[...] = head_sums(acc_p[...], e_ref)

    rowk_i = lambda off: pl.BlockSpec((blk, SSM_N), lambda j, i, off=off: (jnp.maximum(i, j), off))
    rowk_j = lambda off: pl.BlockSpec((blk, SSM_N), lambda j, i, off=off: (j, off))
    db, dx, da_k, da_k_t = pl.pallas_call(
        k_body, grid=(nb, nb),
        in_specs=[rowk_i(_C_OFF), rowk_i(_C_OFF + 1), rowk_j(_B_OFF), rowk_j(_B_OFF + 1),
                  pl.BlockSpec((blk, width), lambda j, i: (j, 0)),
                  pl.BlockSpec((blk, LANES), lambda j, i: (jnp.maximum(i, j), 0)),
                  pl.BlockSpec((blk, LANES), lambda j, i: (j, 0)),
                  pl.BlockSpec((SSM_HEADS, blk), lambda j, i: (0, j)),
                  pl.BlockSpec((blk, width), lambda j, i: (jnp.maximum(i, j), 0)), e_spec],
        out_specs=[pl.BlockSpec((blk, SSM_GROUPS * SSM_N), lambda j, i: (j, 0)), pl.BlockSpec((blk, width), lambda j, i: (j, 0)),
                   pl.BlockSpec((blk, LANES), lambda j, i: (j, 0)), pl.BlockSpec((SSM_HEADS, blk), lambda j, i: (0, j))],
        out_shape=[jax.ShapeDtypeStruct((t, SSM_GROUPS * SSM_N), F32), jax.ShapeDtypeStruct((t, width), F32),
                   jax.ShapeDtypeStruct((t, LANES), F32), jax.ShapeDtypeStruct((SSM_HEADS, t), F32)],
        scratch_shapes=[pltpu.VMEM((blk, SSM_GROUPS * SSM_N), F32), pltpu.VMEM((blk, width), F32), pltpu.VMEM((SSM_HEADS, blk), F32),
                        pltpu.VMEM((blk, width), F32)],
        compiler_params=pltpu.CompilerParams(dimension_semantics=("arbitrary",) * 2, vmem_limit_bytes=VMEM_LIMIT),
        name="ssd_bwd_k")(xbc_act, xbc_act, xbc_act, xbc_act, xdt, a_cum, a_cum, a_cum_t, dy, expand)
    return dc, da_q, db, dx, da_k, da_k_t


def _s5_scan(bu, a_re, a_im, reverse, x_prev=None):
    t = bu.shape[0]
    cb = LANES
    ncb = S5_COLS // cb
    pad = t // 2
    chunk = min(512, t)
    base = 0 if reverse else pad
    zero0 = t if reverse else 0

    def body(*refs):
        if reverse:
            br_ref, bi_ref, ar_ref, ai_ref, xr_ref, xi_ref, or_ref, oi_ref, dar_ref, dai_ref, sr, si = refs
        else:
            br_ref, bi_ref, ar_ref, ai_ref, or_ref, oi_ref, or_bf_ref, oi_bf_ref, sr, si = refs
        sr[pl.ds(zero0, pad), :] = jnp.zeros((pad, cb), F32)
        si[pl.ds(zero0, pad), :] = jnp.zeros((pad, cb), F32)
        sr[pl.ds(base, t), :] = br_ref[...]
        si[pl.ds(base, t), :] = bi_ref[...]
        pr = ar_ref[...]
        pi = -ai_ref[...] if reverse else ai_ref[...]
        s = 1
        while s < t:
            shift = s if reverse else -s

            def step(t0, pr=pr, pi=pi, shift=shift):
                cr, ci = sr[pl.ds(base + t0, chunk), :], si[pl.ds(base + t0, chunk), :]
                qr, qi = sr[pl.ds(base + t0 + shift, chunk), :], si[pl.ds(base + t0 + shift, chunk), :]
                sr[pl.ds(base + t0, chunk), :] = cr + pr * qr - pi * qi
                si[pl.ds(base + t0, chunk), :] = ci + pr * qi + pi * qr

            order = range(0, t, chunk) if reverse else range(t - chunk, -1, -chunk)
            for t0 in order:
                step(t0)
            pr, pi = pr * pr - pi * pi, 2.0 * pr * pi
            s *= 2
        or_ref[...] = sr[pl.ds(base, t), :].astype(or_ref.dtype)
        oi_ref[...] = si[pl.ds(base, t), :].astype(oi_ref.dtype)
        if not reverse:
            or_bf_ref[...] = sr[pl.ds(base, t), :].astype(BF16)
            oi_bf_ref[...] = si[pl.ds(base, t), :].astype(BF16)
        if reverse:
            dar = jnp.zeros((1, cb), F32)
            dai = jnp.zeros((1, cb), F32)
            for t0 in range(0, t, chunk):
                gr, gi = sr[pl.ds(t0 + 1, chunk), :], si[pl.ds(t0 + 1, chunk), :]
                xr, xi = xr_ref[pl.ds(t0, chunk), :], xi_ref[pl.ds(t0, chunk), :]
                dar = dar + jnp.sum(gr * xr + gi * xi, axis=0, keepdims=True)
                dai = dai + jnp.sum(gi * xr - gr * xi, axis=0, keepdims=True)
            dar_ref[...] = dar
            dai_ref[...] = dai

    re_spec = pl.BlockSpec((t, cb), lambda j: (0, j))
    im_spec = pl.BlockSpec((t, cb), lambda j: (0, j + ncb))
    a_spec = pl.BlockSpec((1, cb), lambda j: (0, j))
    ins, in_specs = [bu, bu, a_re, a_im], [re_spec, im_spec, a_spec, a_spec]
    seq_bf = jax.ShapeDtypeStruct((t, S5_COLS), BF16)
    if reverse:
        ins += [x_prev, x_prev]
        in_specs += [re_spec, im_spec]
        out_shape = [seq_bf, seq_bf] + [jax.ShapeDtypeStruct((1, S5_COLS), F32)] * 2
        out_specs = [re_spec, re_spec, a_spec, a_spec]
    else:
        out_shape = [jax.ShapeDtypeStruct((t, S5_COLS), F32)] * 2 + [seq_bf, seq_bf]
        out_specs = [re_spec] * 4
    return pl.pallas_call(
        body, grid=(ncb,), in_specs=in_specs, out_specs=out_specs, out_shape=out_shape,
        scratch_shapes=[pltpu.VMEM((t + pad, cb), F32), pltpu.VMEM((t + pad, cb), F32)],
        compiler_params=pltpu.CompilerParams(dimension_semantics=("arbitrary",), vmem_limit_bytes=VMEM_LIMIT),
        name="s5_scan_bwd" if reverse else "s5_scan_fwd")(*ins)


CONF_DIM = 512
CONF_K = 31
CONF_PAD = 32


def _conf_conv(proj_o, w, b, dc1=None):
    t = proj_o.shape[0]
    cb = LANES
    ncb = CONF_DIM // cb
    chunk = min(512, t)
    chunks = range(0, t, chunk)
    bwd = dc1 is not None

    def body(*refs):
        if bwd:
            ca_ref, cg_ref, w_ref, b_ref, dy_ref, dca_ref, dcg_ref, dw_ref, db_ref, xs, dys = refs
        else:
            ca_ref, cg_ref, w_ref, b_ref, o_ref, xs = refs
        xs[pl.ds(0, CONF_PAD), :] = jnp.zeros((CONF_PAD, cb), F32)
        for t0 in chunks:
            rows = pl.ds(t0, chunk)
            xs[pl.ds(CONF_PAD + t0, chunk), :] = ca_ref[rows, :] * jax.nn.sigmoid(cg_ref[rows, :])
        if not bwd:
            for t0 in chunks:
                acc = jnp.broadcast_to(b_ref[...], (chunk, cb))
                for k in range(CONF_K):
                    acc = acc + w_ref[k] * xs[pl.ds(CONF_PAD + t0 - (CONF_K - 1 - k), chunk), :]
                o_ref[pl.ds(t0, chunk), :] = acc
            return
        dys[pl.ds(t, CONF_PAD), :] = jnp.zeros((CONF_PAD, cb), F32)
        db = jnp.zeros((1, cb), F32)
        for t0 in chunks:
            dys[pl.ds(t0, chunk), :] = dy_ref[pl.ds(t0, chunk), :]
            db = db + jnp.sum(dy_ref[pl.ds(t0, chunk), :], axis=0, keepdims=True)
        db_ref[...] = db
        for t0 in chunks:
            rows = pl.ds(t0, chunk)
            acc = jnp.zeros((chunk, cb), F32)
            for k in range(CONF_K):
                acc = acc + w_ref[k] * dys[pl.ds(t0 + (CONF_K - 1 - k), chunk), :]
            sig = jax.nn.sigmoid(cg_ref[rows, :])
            dca_ref[rows, :] = (acc * sig).astype(BF16)
            dcg_ref[rows, :] = (acc * ca_ref[rows, :] * sig * (1.0 - sig)).astype(BF16)
        for k in range(CONF_K):
            dwk = jnp.zeros((1, cb), F32)
            for t0 in chunks:
                window = xs[pl.ds(CONF_PAD + t0 - (CONF_K - 1 - k), chunk), :]
                dwk = dwk + jnp.sum(dy_ref[pl.ds(t0, chunk), :] * window, axis=0, keepdims=True)
            dw_ref[k] = dwk

    col = lambda off: pl.BlockSpec((t, cb), lambda j, off=off: (0, j + off))
    w_spec = pl.BlockSpec((CONF_K, 1, cb), lambda j: (0, 0, j))
    b_spec = pl.BlockSpec((1, cb), lambda j: (0, j))
    seq = jax.ShapeDtypeStruct((t, CONF_DIM), F32)
    ins, in_specs = [proj_o, proj_o, w, b], [col(0), col(ncb), w_spec, b_spec]
    scratch = [pltpu.VMEM((CONF_PAD + t, cb), F32)]
    if bwd:
        ins, in_specs = ins + [dc1], in_specs + [col(0)]
        seq_bf = jax.ShapeDtypeStruct((t, CONF_DIM), BF16)
        out_shape, out_specs = [seq_bf, seq_bf, jax.ShapeDtypeStruct(w.shape, F32), jax.ShapeDtypeStruct(b.shape, F32)], [col(0), col(0), w_spec, b_spec]
        scratch = scratch + [pltpu.VMEM((t + CONF_PAD, cb), F32)]
    else:
        out_shape, out_specs = seq, col(0)
    return pl.pallas_call(
        body, grid=(ncb,), in_specs=in_specs, out_specs=out_specs, out_shape=out_shape, scratch_shapes=scratch,
        compiler_params=pltpu.CompilerParams(dimension_semantics=("arbitrary",), vmem_limit_bytes=VMEM_LIMIT),
        name="conf_conv_bwd" if bwd else "conf_conv")(*ins)


def _exchange(srcs, broadcast, name):
    n = len(srcs)

    def body(*refs):
        src_refs, out_refs = refs[:n], refs[n:2 * n]
        send_sems, recv_sems, local_sems = refs[2 * n:]
        x, y, c = lax.axis_index("x"), lax.axis_index("y"), lax.axis_index("c")
        me = 4 * x + 2 * y + c

        def piece(i, idx):
            return src_refs[i] if broadcast[i] else src_refs[i].at[idx]

        local = [pltpu.make_async_copy(piece(i, me), out_refs[i].at[me], local_sems.at[i]) for i in range(n)]
        for cp in local:
            cp.start()
        copies = []
        for k in range(1, N_DEV):
            px, py, pc = x ^ ((k >> 2) & 1), y ^ ((k >> 1) & 1), c ^ (k & 1)
            peer = 4 * px + 2 * py + pc
            for i in range(n):
                sem = (k - 1) * n + i
                send = pltpu.make_async_remote_copy(
                    src_ref=piece(i, peer), dst_ref=out_refs[i].at[me], send_sem=send_sems.at[sem], recv_sem=recv_sems.at[sem],
                    device_id=(px, py, pc), device_id_type=pl.DeviceIdType.MESH)
                send.start()
                recv = pltpu.make_async_remote_copy(
                    src_ref=piece(i, peer), dst_ref=out_refs[i].at[peer], send_sem=send_sems.at[sem], recv_sem=recv_sems.at[sem],
                    device_id=(px, py, pc), device_id_type=pl.DeviceIdType.MESH)
                copies.append((send, recv))
        for _, recv in copies:
            recv.wait_recv()
        for send, _ in copies:
            send.wait_send()
        for cp in local:
            cp.wait()

    blocks = [s.shape if b else s.shape[1:] for s, b in zip(srcs, broadcast)]
    n_sems = (N_DEV - 1) * n
    return pl.pallas_call(
        body, out_shape=[jax.ShapeDtypeStruct((N_DEV,) + tuple(blk), s.dtype) for blk, s in zip(blocks, srcs)],
        in_specs=[pl.BlockSpec(memory_space=pl.ANY)] * n, out_specs=[pl.BlockSpec(memory_space=pl.ANY)] * n,
        scratch_shapes=[pltpu.SemaphoreType.DMA((n_sems,)), pltpu.SemaphoreType.DMA((n_sems,)), pltpu.SemaphoreType.DMA((n,))],
        compiler_params=pltpu.CompilerParams(has_side_effects=True),
        name=name)(*srcs)


def _row_tile(r, pref=256):
    if r <= pref:
        return r
    t = pref // 16 * 16
    while r % t:
        t -= 16
    return t


def _join_cols(g, width, name):
    _, rows, ws = g.shape
    tr = _row_tile(rows)
    tail = width - N_DEV * ws

    def body(g_ref, o_ref):
        for d in range(N_DEV):
            o_ref[:, pl.ds(d * ws, ws)] = g_ref[d]
        if tail:
            o_ref[:, pl.ds(N_DEV * ws, tail)] = jnp.zeros((tr, tail), g.dtype)

    return pl.pallas_call(
        body, grid=(rows // tr,), in_specs=[pl.BlockSpec((N_DEV, tr, ws), lambda i: (0, i, 0))],
        out_specs=pl.BlockSpec((tr, width), lambda i: (i, 0)), out_shape=jax.ShapeDtypeStruct((rows, width), g.dtype),
        compiler_params=pltpu.CompilerParams(dimension_semantics=("parallel",), vmem_limit_bytes=VMEM_LIMIT), name=name)(g)


def _split_cols(full, ws, dtype, name):
    rows, width = full.shape
    tr = _row_tile(rows)

    def body(x_ref, o_ref):
        for d in range(N_DEV):
            o_ref[d] = x_ref[:, pl.ds(d * ws, ws)].astype(dtype)

    return pl.pallas_call(
        body, grid=(rows // tr,), in_specs=[pl.BlockSpec((tr, width), lambda i: (i, 0))],
        out_specs=pl.BlockSpec((N_DEV, tr, ws), lambda i: (0, i, 0)), out_shape=jax.ShapeDtypeStruct((N_DEV, rows, ws), dtype),
        compiler_params=pltpu.CompilerParams(dimension_semantics=("parallel",), vmem_limit_bytes=VMEM_LIMIT), name=name)(full)


def _adamw(parts, w, m, v, name):
    r, c = w.shape
    tr = _row_tile(r)

    def body(p_ref, w_ref, m_ref, v_ref, g_ref, d_ref, nm_ref, nv_ref):
        g = p_ref[0].astype(F32)
        for s in range(1, N_DEV):
            g = g + p_ref[s].astype(F32)
        nm = ADAM_B1 * m_ref[...] + (1.0 - ADAM_B1) * g
        nv = ADAM_B2 * v_ref[...] + (1.0 - ADAM_B2) * (g * g)
        m_hat = nm / (1.0 - ADAM_B1 ** ADAM_STEP)
        v_hat = nv / (1.0 - ADAM_B2 ** ADAM_STEP)
        g_ref[...] = g
        nm_ref[...] = nm
        nv_ref[...] = nv
        d_ref[...] = -ADAM_LR * (m_hat / (jnp.sqrt(v_hat) + ADAM_EPS) + ADAM_WD * w_ref[...])

    spec = pl.BlockSpec((tr, c), lambda i: (i, 0))
    return pl.pallas_call(
        body, grid=(r // tr,), in_specs=[pl.BlockSpec((N_DEV, tr, c), lambda i: (0, i, 0)), spec, spec, spec],
        out_specs=[spec] * 4, out_shape=[jax.ShapeDtypeStruct((r, c), F32)] * 4,
        compiler_params=pltpu.CompilerParams(dimension_semantics=("parallel",), vmem_limit_bytes=VMEM_LIMIT),
        name=name)(parts, w, m, v)


def _pack_rows(n_elems, mult):
    rows = -(-n_elems // PACK_COLS)
    return -(-rows // mult) * mult


def _pack(arrays, dtype, mult, lead=()):
    flat = jnp.concatenate([a.astype(dtype).reshape(lead + (-1,)) for a in arrays], axis=-1)
    rows = _pack_rows(flat.shape[-1], mult)
    flat = jnp.pad(flat, [(0, 0)] * len(lead) + [(0, rows * PACK_COLS - flat.shape[-1])])
    return flat.reshape(lead + (rows, PACK_COLS))


def _unpack(buf, shapes, lead=()):
    flat = buf.reshape(lead + (-1,))
    out, off = [], 0
    for s in shapes:
        n = math.prod(s)
        out.append(flat[..., off:off + n].reshape(lead + tuple(s)))
        off += n
    return out


def _join_shards(piece, axis):
    moved = jnp.moveaxis(piece, 0, axis)
    shape = moved.shape
    return moved.reshape(shape[:axis] + (shape[axis] * shape[axis + 1],) + shape[axis + 2:])


def _split_shards(full, axis):
    shape = full.shape
    return jnp.moveaxis(full.reshape(shape[:axis] + (N_DEV, shape[axis] // N_DEV) + shape[axis + 1:]), axis, 0)


def _block_diag(blocks):
    g, r, c = blocks.shape
    eye = jnp.eye(g, dtype=blocks.dtype)
    return (blocks[:, :, None, :] * eye[:, None, :, None]).reshape(g * r, g * c)


def _diag_blocks(mat, g):
    r, c = mat.shape[0] // g, mat.shape[1] // g
    eye = jnp.eye(g, dtype=mat.dtype)
    return jnp.sum(mat.reshape(g, r, g, c) * eye[:, None, :, None], axis=2)


def _pad_lanes(a):
    a = a.reshape(1, -1)
    return jnp.pad(a, ((0, 0), (0, LANES - a.shape[1])))


def _head_expand():
    h = jnp.arange(LANES)[:, None]
    ch = jnp.arange(SSM_HEADS * SSM_P)[None, :] // SSM_P
    return (h == ch).astype(F32)


def _rotary_tables(t):
    inv = 10000.0 ** (-jnp.arange(0, RET_DK, 2, dtype=F32) / RET_DK)
    ang = jnp.arange(t).astype(F32)[:, None] * inv[None, :]
    cos, sin = jnp.cos(ang), jnp.sin(ang)
    return jnp.concatenate([cos, cos], axis=1), jnp.concatenate([-sin, sin], axis=1)


def _rms_fwd(x, g, name):
    return _rows(_f_rms, x.shape[0], [(x, D_MODEL, 0)], [g], [D_MODEL], [], name, bf16_outs=(0,))[0]


def _rms_bwd(x, g, dh, dres, name):
    def fn(x_, dh_, dres_, g_):
        _, vjp = jax.vjp(lambda a, b: _rms(a, b), x_, g_)
        dx, dg = vjp(dh_)
        return dx + dres_, dx + dres_, dg
    return _rows(fn, x.shape[0], [(x, D_MODEL, 0), (dh, D_MODEL, 0), (dres, D_MODEL, 0)], [g], [D_MODEL, D_MODEL], [(1, D_MODEL)],
                 name, bf16_outs=(1,))


def _ffn_fwd(x, norm_g, w_up, dw_w, dw_b, w_down, tag):
    t = x.shape[0]
    nbk = D_FF // LANES
    h = _rms_fwd(x, norm_g, "ffn_norm_" + tag)
    up = _mm(h, w_up, "nn", "ffn_up_" + tag)
    mid = _cols(_f_ffnmid, nbk, [(up, 0), (up, nbk), (dw_w, 0), (dw_w, nbk), (dw_b, 0), (dw_b, nbk)], [(t,)], "ffn_mid_" + tag,
                bf16_outs=(0,))[0]
    out = _mm(mid, w_down, "nn", "ffn_down_" + tag, res=x)
    return out, (h, up, mid)


def _ffn_bwd(x, norm_g, w_up, dw_w, dw_b, w_down, saved, dout, dout_bf, tag):
    t = x.shape[0]
    nbk = D_FF // LANES
    h, up, mid = saved
    d_w_down = _mm(mid, dout_bf, "tn", "ffn_down_dw_" + tag, out_dtype=BF16)
    dmid = _mm(dout_bf, w_down, "nt", "ffn_down_dx_" + tag)
    dgin, duin, dwg, dwu, dbg, dbu = _cols(
        _grad_fn(_f_ffnmid, 6, 6), nbk,
        [(up, 0), (up, nbk), (dw_w, 0), (dw_w, nbk), (dw_b, 0), (dw_b, nbk), (dmid, 0)],
        [(t,), (t,), (3, 1), (3, 1), (1,), (1,)], "ffn_mid_bwd_" + tag, bf16_outs=(0, 1))
    dup = jnp.concatenate([dgin, duin], axis=1)
    d_w_up = _mm(h, dup, "tn", "ffn_up_dw_" + tag)
    dh = _mm(dup, w_up, "nt", "ffn_up_dx_" + tag)
    dx, dx_bf, dnorm = _rms_bwd(x, norm_g, dh, dout, "ffn_norm_bwd_" + tag)
    return dx, dx_bf, dict(norm=dnorm, w_up=d_w_up, dw_w=jnp.concatenate([dwg, dwu], axis=2)[:, 0],
                           dw_b=jnp.concatenate([dbg, dbu], axis=1), w_down=d_w_down)


def _local_step(x, tgt, w):
    t = x.shape[0]
    grads = {}
    expand = _head_expand()
    cosf, sins = _rotary_tables(t)
    mix_g = [w['mix_norm'][i:i + 1] for i in range(2)]
    ffn_g = [w['ffn_norm'][i:i + 1] for i in range(2)]
    ffn_dw_w = [w['ffn_dw_w'][i][:, None, :] for i in range(2)]
    ffn_dw_b = [w['ffn_dw_b'][i:i + 1] for i in range(2)]

    w_e = w['e_w_in']
    conv_w = w['e_conv_w'][0][:, None, :]
    conv_b = w['e_conv_b']
    dt_bias, a_log, d_skip = _pad_lanes(w['e_dt_bias']), _pad_lanes(w['e_a_log']), _pad_lanes(w['e_d'])
    h0 = _rms_fwd(x, mix_g[0], "mix_norm_0")
    proj = _mm(h0, w_e, "nn", "e_in")
    qr, kr = _rows(_f_retpre, t, [(proj, 512, 0), (proj, 512, 1), (cosf, LANES, 0), (sins, LANES, 0)], [], [512, 512], [], "ret_pre")
    r = _ret_fwd(qr, kr, proj, t)
    y_ret = _rows(_f_retpost, t, [(r, 1024, 0), (proj, 1024, 2)], [], [1024], [], "ret_post", bf16_outs=(0,))[0]
    xbc_act = _cols(_f_ssdconv, 12, [(proj, 32), (conv_w, 0), (conv_b, 0)], [(t,)], "ssd_conv")[0]
    xdt, da = _rows(_f_ssdpre, t, [(xbc_act, 1024, 0), (proj, LANES, 44)], [dt_bias, a_log, expand], [1024, LANES], [], "ssd_pre")
    a_cum = _cumsum(da, False, "ssd_cumsum")
    a_cum_t = a_cum[:, :SSM_HEADS].T
    yc = _ssd_fwd(xbc_act, xdt, a_cum, a_cum_t, t)
    y_ssm = _rows(_f_ssdpost, t, [(yc, 1024, 0), (xbc_act, 1024, 0), (proj, 1024, 3)], [d_skip, w['e_ssm_norm'], expand],
                  [1024], [], "ssd_post", bf16_outs=(0,))[0]
    mix_e = jnp.concatenate([y_ret, y_ssm], axis=1)
    x1 = _mm(mix_e, w['e_w_out'], "nn", "e_out", res=x)
    x2, ffn0 = _ffn_fwd(x1, ffn_g[0], w['ffn_w_up'][0], ffn_dw_w[0], ffn_dw_b[0], w['ffn_w_down'][0], "0")

    lr, li = w['o_a_re'][0], w['o_a_im'][0]
    ls = w['o_log_step'].reshape(S5_GROUPS, 1)
    b_re3, b_im3 = jnp.transpose(w['o_b_re'][0], (2, 0, 1)), jnp.transpose(w['o_b_im'][0], (2, 0, 1))
    par_ins = [lr, li, ls, b_re3, b_im3]
    whole = lambda a: (a, a.shape, (lambda i, n=a.ndim: (0,) * n))
    par_shapes = [(S5_GROUPS, S5_STATE)] * 2 + [(S5_GROUP, S5_GROUPS, S5_STATE)] * 2
    ab_re, ab_im, bb_re, bb_im = _call(_f_s5par, (1,), [whole(a) for a in par_ins],
                                       [(s, F32, s, (lambda i, n=len(s): (0,) * n), False) for s in par_shapes], "s5_params")
    w_b = jnp.concatenate([_block_diag(jnp.transpose(bb_re, (1, 0, 2))), _block_diag(jnp.transpose(bb_im, (1, 0, 2)))], axis=1)
    w_c = jnp.concatenate([_block_diag(jnp.transpose(w['o_c_re'][0], (0, 2, 1))),
                           -_block_diag(jnp.transpose(w['o_c_im'][0], (0, 2, 1)))], axis=0)
    a_re, a_im = ab_re.reshape(1, S5_COLS), ab_im.reshape(1, S5_COLS)
    dw_w = w['o_dw_w'][0][:, None, :]
    glu_w = w['o_glu_w'].astype(F32)

    h1 = _rms_fwd(x2, mix_g[1], "mix_norm_1")
    proj_o = _mm(h1, w['o_w_in'], "nn", "o_in")
    c1 = _conf_conv(proj_o, dw_w, w['o_dw_b'])
    c2 = _rows(_f_confb, t, [(c1, 512, 0)], [w['o_ln_g'], w['o_ln_b']], [512], [], "conf_norm", bf16_outs=(0,))[0]
    u = proj_o[:, 1024:].astype(BF16)
    bu = _mm(u, w_b, "nn", "s5_bu")
    xs_re, xs_im, xs_re_bf, xs_im_bf = _s5_scan(bu, a_re, a_im, False)
    xs_cat = jnp.concatenate([xs_re, xs_im], axis=1)
    xs_cat_bf = jnp.concatenate([xs_re_bf, xs_im_bf], axis=1)
    y_s5 = _mm(xs_cat_bf, w_c, "nn", "s5_cx")
    s_out = _rows(_f_s5post, t, [(y_s5, 512, 0), (proj_o, 512, 2)], [w['o_d'], glu_w], [512], [], "s5_post", bf16_outs=(0,))[0]
    mix_o = jnp.concatenate([c2, s_out], axis=1)
    x3 = _mm(mix_o, w['o_w_out'], "nn", "o_out", res=x2)
    x4, ffn1 = _ffn_fwd(x3, ffn_g[1], w['ffn_w_up'][1], ffn_dw_w[1], ffn_dw_b[1], w['ffn_w_down'][1], "1")

    dx4, dx4_bf, loss_blk, d_final = _rows(_loss_step, t, [(x4, D_MODEL, 0), (tgt, D_MODEL, 0)], [w['final_norm'].reshape(1, D_MODEL)],
                                           [D_MODEL, D_MODEL], [(8, LANES), (1, D_MODEL)], "loss_head", bf16_outs=(1,))
    loss = loss_blk[0, 0]
    grads['final_norm'] = d_final.reshape(D_MODEL)

    dx3, dx3_bf, g1 = _ffn_bwd(x3, ffn_g[1], w['ffn_w_up'][1], ffn_dw_w[1], ffn_dw_b[1], w['ffn_w_down'][1], ffn1, dx4, dx4_bf, "1")
    grads['o_w_out'] = _mm(mix_o, dx3_bf, "tn", "o_out_dw", out_dtype=BF16)
    dmix_o = _mm(dx3_bf, w['o_w_out'], "nt", "o_out_dx")
    dc1, d_ln_g, d_ln_b = _conf_norm_bwd(c1, w['o_ln_g'], w['o_ln_b'], dmix_o, t)
    dca, dcg, d_dw_w, d_dw_b = _conf_conv(proj_o, dw_w, w['o_dw_b'], dc1)
    dyc, du_skip, d_od, d_glu = _s5_post_bwd(y_s5, proj_o, w['o_d'], glu_w, dmix_o, t)
    grads['o_glu_w'] = d_glu
    d_w_c = _mm(xs_cat_bf, dyc, "tn", "s5_cx_dw")
    dxs = _mm(dyc, w_c, "nt", "s5_cx_dx")
    g_re, g_im, d_are, d_aim = _s5_scan(dxs, a_re, a_im, True, xs_cat)
    g_cat = jnp.concatenate([g_re, g_im], axis=1)
    d_w_b = _mm(u, g_cat, "tn", "s5_bu_dw")
    du = _mm(g_cat, w_b, "nt", "s5_bu_dx", res=du_skip, out_dtype=BF16)
    d_bb_re = jnp.transpose(_diag_blocks(d_w_b[:, :S5_COLS], S5_GROUPS), (1, 0, 2))
    d_bb_im = jnp.transpose(_diag_blocks(d_w_b[:, S5_COLS:], S5_GROUPS), (1, 0, 2))
    par_cts = [d_are.reshape(S5_GROUPS, S5_STATE), d_aim.reshape(S5_GROUPS, S5_STATE), d_bb_re, d_bb_im]
    in_shapes = [a.shape for a in par_ins]
    d_lr, d_li, d_ls, d_br3, d_bi3 = _call(_grad_fn(_f_s5par, 5, 5), (1,), [whole(a) for a in par_ins + par_cts],
                                           [(s, F32, s, (lambda i, n=len(s): (0,) * n), False) for s in in_shapes], "s5_params_bwd")
    grads['o_a_re'], grads['o_a_im'], grads['o_log_step'] = d_lr[None], d_li[None], d_ls.reshape(1, S5_GROUPS)
    grads['o_b_re'], grads['o_b_im'] = jnp.transpose(d_br3, (1, 2, 0))[None], jnp.transpose(d_bi3, (1, 2, 0))[None]
    grads['o_c_re'] = jnp.transpose(_diag_blocks(d_w_c[:S5_COLS], S5_GROUPS), (0, 2, 1))[None]
    grads['o_c_im'] = -jnp.transpose(_diag_blocks(d_w_c[S5_COLS:], S5_GROUPS), (0, 2, 1))[None]
    grads['o_d'], grads['o_ln_g'], grads['o_ln_b'] = d_od, d_ln_g, d_ln_b
    grads['o_dw_w'], grads['o_dw_b'] = d_dw_w[:, 0][None], d_dw_b
    dproj_o = jnp.concatenate([dca, dcg, du], axis=1)
    grads['o_w_in'] = _mm(h1, dproj_o, "tn", "o_in_dw")
    dh1 = _mm(dproj_o, w['o_w_in'], "nt", "o_in_dx")
    dx2, dx2_bf, d_mix1 = _rms_bwd(x2, mix_g[1], dh1, dx3, "mix_norm_bwd_1")

    dx1, dx1_bf, g0 = _ffn_bwd(x1, ffn_g[0], w['ffn_w_up'][0], ffn_dw_w[0], ffn_dw_b[0], w['ffn_w_down'][0], ffn0, dx2, dx2_bf, "0")
    grads['e_w_out'] = _mm(mix_e, dx1_bf, "tn", "e_out_dw", out_dtype=BF16)
    dmix_e = _mm(dx1_bf, w['e_w_out'], "nt", "e_out_dx")
    dr, dg = _rows(_grad_fn(_f_retpost, 2, 2), t, [(r, 1024, 0), (proj, 1024, 2), (dmix_e, 1024, 0)], [], [1024, 1024], [], "ret_post_bwd",
                   bf16_outs=(0, 1))
    dqr, dkr, dv = _ret_bwd(qr, kr, proj, dr, t)
    dq, dk = _ret_pre_bwd(proj, cosf, sins, dqr, dkr, t)
    dyc0, dxs1, dz, d_dskip, d_ssm_norm = _ssd_post_bwd(yc, xbc_act, proj, d_skip, w['e_ssm_norm'], expand, dmix_e, t)
    dcm, da_q, dbm, dxdt, da_k, da_k_t = _ssd_bwd(xbc_act, xdt, a_cum, a_cum_t, dyc0, expand, t)
    d_a_cum = da_q - da_k - jnp.pad(da_k_t.T, ((0, 0), (0, LANES - SSM_HEADS)))
    dda = _cumsum(d_a_cum, True, "ssd_cumsum_bwd")
    dxs, ddtr, d_dt_bias, d_a_log = _ssd_pre_bwd(xbc_act, proj, dt_bias, a_log, expand, dxdt, dda, dxs1, t)
    dxbc_act = jnp.concatenate([dxs, dbm, dcm], axis=1)
    dxbc, d_conv_w, d_conv_b = _cols(_grad_fn(_f_ssdconv, 3, 3), 12, [(proj, 32), (conv_w, 0), (conv_b, 0), (dxbc_act, 0)],
                                     [(t,), (4, 1), (1,)], "ssd_conv_bwd", bf16_outs=(0,))
    dproj = jnp.concatenate([dq, dk, dv, dg, dz, dxbc, ddtr], axis=1)
    grads['e_w_in'] = _mm(h0, dproj, "tn", "e_in_dw")
    dh0 = _mm(dproj, w_e, "nt", "e_in_dx")
    dx0, _, d_mix0 = _rms_bwd(x, mix_g[0], dh0, dx1, "mix_norm_bwd_0")

    grads['mix_norm'] = jnp.concatenate([d_mix0, d_mix1], axis=0)
    grads['e_conv_w'], grads['e_conv_b'] = d_conv_w[:, 0][None], d_conv_b
    grads['e_dt_bias'], grads['e_a_log'], grads['e_d'] = d_dt_bias[:, :SSM_HEADS], d_a_log[:, :SSM_HEADS], d_dskip[:, :SSM_HEADS]
    grads['e_ssm_norm'] = d_ssm_norm
    grads['ffn_norm'] = jnp.concatenate([g0['norm'], g1['norm']], axis=0)
    grads['ffn_w_up'], grads['ffn_w_down'] = [g0['w_up'], g1['w_up']], [g0['w_down'], g1['w_down']]
    grads['ffn_dw_w'] = jnp.stack([g0['dw_w'], g1['dw_w']], axis=0)
    grads['ffn_dw_b'] = jnp.concatenate([g0['dw_b'], g1['dw_b']], axis=0)
    return loss, dx0, grads


def _conf_norm_bwd(c1, ln_g, ln_b, dmix_o, t):
    def fn(c1_, dy_, g_, b_):
        _, vjp = jax.vjp(lambda a, b, c: _f_confb(a, b, c)[0], c1_, g_, b_)
        return vjp(dy_)
    return _rows(fn, t, [(c1, 512, 0), (dmix_o, 512, 0)], [ln_g, ln_b], [512], [(1, 512), (1, 512)], "conf_norm_bwd")


def _s5_post_bwd(y_s5, proj_o, d_skip, glu_w, dmix_o, t):
    def fn(yc_, u_, dy_, d_, gw_):
        _, vjp = jax.vjp(lambda a, b, c, e: _f_s5post(a, b, c, e)[0], yc_, u_, d_, gw_)
        return vjp(dy_)
    return _rows(fn, t, [(y_s5, 512, 0), (proj_o, 512, 2), (dmix_o, 512, 1)], [d_skip, glu_w], [512, 512], [(1, 512), (512, 512)],
                 "s5_post_bwd", bf16_outs=(0,))


def _ret_pre_bwd(proj, cosf, sins, dqr, dkr, t):
    def fn(q_, k_, cos_, sin_, dq_, dk_):
        _, vjp = jax.vjp(lambda a, b: _f_retpre(a, b, cos_, sin_), q_, k_)
        return vjp((dq_, dk_))
    return _rows(fn, t, [(proj, 512, 0), (proj, 512, 1), (cosf, LANES, 0), (sins, LANES, 0), (dqr, 512, 0), (dkr, 512, 0)], [],
                 [512, 512], [], "ret_pre_bwd", bf16_outs=(0, 1))


def _ssd_post_bwd(yc, xbc_act, proj, d_skip, norm_w, expand, dmix_e, t):
    def fn(yc_, xs_, z_, dy_, d_, nw_, e_):
        _, vjp = jax.vjp(lambda a, b, c, dd, n: _f_ssdpost(a, b, c, dd, n, e_)[0], yc_, xs_, z_, d_, nw_)
        return vjp(dy_)
    return _rows(fn, t, [(yc, 1024, 0), (xbc_act, 1024, 0), (proj, 1024, 3), (dmix_e, 1024, 1)], [d_skip, norm_w, expand],
                 [1024, 1024, 1024], [(1, LANES), (1, 1024)], "ssd_post_bwd", bf16_outs=(0, 2))


def _ssd_pre_bwd(xbc_act, proj, dt_bias, a_log, expand, dxdt, dda, dxs1, t):
    def fn(xs_, dtr_, dx_, dda_, dxs1_, bias_, alog_, e_):
        _, vjp = jax.vjp(lambda a, b, c, dd: _f_ssdpre(a, b, c, dd, e_), xs_, dtr_, bias_, alog_)
        dxs, ddtr, dbias, dalog = vjp((dx_, dda_))
        return dxs + dxs1_, ddtr, dbias, dalog
    return _rows(fn, t, [(xbc_act, 1024, 0), (proj, LANES, 44), (dxdt, 1024, 0), (dda, LANES, 0), (dxs1, 1024, 0)],
                 [dt_bias, a_log, expand], [1024, LANES], [(1, LANES), (1, LANES)], "ssd_pre_bwd", bf16_outs=(1,))


def kernel(x, mix_norm, e_w_in, e_conv_w, e_conv_b, e_dt_bias, e_a_log, e_d, e_ssm_norm, e_w_out, o_w_in, o_dw_w, o_dw_b, o_ln_g, o_ln_b, o_a_re, o_a_im, o_b_re, o_b_im, o_c_re, o_c_im, o_d, o_log_step, o_glu_w, o_w_out, ffn_norm, ffn_w_up, ffn_dw_w, ffn_dw_b, ffn_w_down, final_norm, loss_target, m_mix_norm, m_e_w_in, m_e_conv_w, m_e_conv_b, m_e_dt_bias, m_e_a_log, m_e_d, m_e_ssm_norm, m_e_w_out, m_o_w_in, m_o_dw_w, m_o_dw_b, m_o_ln_g, m_o_ln_b, m_o_a_re, m_o_a_im, m_o_b_re, m_o_b_im, m_o_c_re, m_o_c_im, m_o_d, m_o_log_step, m_o_glu_w, m_o_w_out, m_ffn_norm, m_ffn_w_up, m_ffn_dw_w, m_ffn_dw_b, m_ffn_w_down, m_final_norm, v_mix_norm, v_e_w_in, v_e_conv_w, v_e_conv_b, v_e_dt_bias, v_e_a_log, v_e_d, v_e_ssm_norm, v_e_w_out, v_o_w_in, v_o_dw_w, v_o_dw_b, v_o_ln_g, v_o_ln_b, v_o_a_re, v_o_a_im, v_o_b_re, v_o_b_im, v_o_c_re, v_o_c_im, v_o_d, v_o_log_step, v_o_glu_w, v_o_w_out, v_ffn_norm, v_ffn_w_up, v_ffn_dw_w, v_ffn_dw_b, v_ffn_w_down, v_final_norm):
    p = dict(locals())

    kinds = ("grad_", "delta_", "new_m_", "new_v_")

    def block(name, layer):
        return p[name][0 if layer is None else layer]

    srcs = [block(n, layer).astype(BF16) for n, layer, _ in MATMUL_BLOCKS] + [_pack([p[n] for n in SMALL_SHARDED], F32, 16)]
    gathered = _exchange(srcs, [True] * len(srcs), "gather_weights")
    w = {n: p[n] for n in REPLICATED}
    for (n, layer, by_cols), g in zip(MATMUL_BLOCKS, gathered):
        if by_cols:
            full = _join_cols(g, E_IN_PAD if n == 'e_w_in' else N_DEV * g.shape[2], f"join_{n}_{layer}")
        else:
            full = g.reshape(N_DEV * g.shape[1], g.shape[2])
        if layer is None:
            w[n] = full
        else:
            w.setdefault(n, [None, None])[layer] = full
    for n, piece in zip(SMALL_SHARDED, _unpack(gathered[-1], [p[n].shape for n in SMALL_SHARDED], lead=(N_DEV,))):
        w[n] = _join_shards(piece, SHARDED[n])

    loss, dx, grads = _local_step(x[0], loss_target[0], w)
    loss = lax.psum(loss, MESH_AXES)

    sends = []
    for n, layer, by_cols in MATMUL_BLOCKS:
        g = grads[n] if layer is None else grads[n][layer]
        if by_cols:
            sends.append(_split_cols(g, block(n, layer).shape[1], BF16, f"split_{n}_{layer}"))
        else:
            sends.append(g.astype(BF16).reshape(N_DEV, -1, g.shape[1]))
    sends.append(_pack([_split_shards(grads[n].reshape(p[n].shape[:SHARDED[n]] + (-1,) + p[n].shape[SHARDED[n] + 1:]), SHARDED[n])
                        for n in SMALL_SHARDED], F32, 128, lead=(N_DEV,)))
    sends.append(_pack([grads[n].reshape(p[n].shape) for n in REPLICATED], F32, 128))
    parts = _exchange(sends, [False] * (len(sends) - 1) + [True], "exchange_grads")

    out, by_layer = {}, {}
    for (n, layer, _), part in zip(MATMUL_BLOCKS, parts):
        by_layer.setdefault(n, {})[layer] = _adamw(part, *[block(pre + n, layer) for pre in ("", "m_", "v_")], f"adamw_{n}_{layer}")
    for n, res in by_layer.items():
        for i, kind in enumerate(kinds):
            out[kind + n] = res[None][i][None] if None in res else jnp.stack([res[0][i], res[1][i]], axis=0)
    for names, part, tag in ((SMALL_SHARDED, parts[-2], "small"), (REPLICATED, parts[-1], "replicated")):
        packed = [_pack([p[pre + n] for n in names], F32, 128) for pre in ("", "m_", "v_")]
        for kind, buf in zip(kinds, _adamw(part, *packed, "adamw_" + tag)):
            for n, a in zip(names, _unpack(buf, [p[n].shape for n in names])):
                out[kind + n] = a
    return (loss, dx[None], *[out[kind + n] for kind in kinds for n in WEIGHTS])
```

```python
import functools
import math

import jax
import jax.numpy as jnp
from jax import lax
from jax.experimental import pallas as pl
from jax.experimental.pallas import tpu as pltpu

F32, BF16 = jnp.float32, jnp.bfloat16
HIGHEST = lax.Precision.HIGHEST
N_DEV = 8
MESH_AXES = ("x", "y", "c")
VMEM_LIMIT = 48 * 1024 * 1024
LANES = 128
PACK_COLS = 1024

D_MODEL = 1024
EPS = 1e-6
RET_HEADS, RET_DK, RET_DV = 4, 128, 256
SSM_HEADS, SSM_P, SSM_N, SSM_GROUPS = 16, 64, 128, 2
SSM_HG = SSM_HEADS // SSM_GROUPS
S5_GROUPS, S5_GROUP, S5_STATE = 32, 16, 64
S5_COLS = S5_GROUPS * S5_STATE
D_FF = 2816
E_IN, E_IN_PAD = 5648, 5760
ADAM_LR, ADAM_B1, ADAM_B2, ADAM_EPS, ADAM_WD, ADAM_STEP = 0.001, 0.9, 0.999, 1e-08, 0.01, 10

WEIGHTS = ['mix_norm', 'e_w_in', 'e_conv_w', 'e_conv_b', 'e_dt_bias', 'e_a_log', 'e_d', 'e_ssm_norm', 'e_w_out', 'o_w_in', 'o_dw_w', 'o_dw_b', 'o_ln_g', 'o_ln_b', 'o_a_re', 'o_a_im', 'o_b_re', 'o_b_im', 'o_c_re', 'o_c_im', 'o_d', 'o_log_step', 'o_glu_w', 'o_w_out', 'ffn_norm', 'ffn_w_up', 'ffn_dw_w', 'ffn_dw_b', 'ffn_w_down', 'final_norm']
SHARDED = {'e_w_in': 2, 'e_conv_w': 2, 'e_w_out': 1, 'o_w_in': 2, 'o_dw_w': 2, 'o_dw_b': 1, 'o_ln_g': 1, 'o_ln_b': 1,
           'o_d': 1, 'o_glu_w': 1, 'o_w_out': 1, 'ffn_w_up': 2, 'ffn_dw_w': 2, 'ffn_w_down': 1}
MATMUL_WEIGHTS = ['e_w_in', 'e_w_out', 'o_w_in', 'o_glu_w', 'o_w_out', 'ffn_w_up', 'ffn_w_down']
MATMUL_BLOCKS = [('e_w_in', None, True), ('e_w_out', None, False), ('o_w_in', None, True), ('o_glu_w', None, False),
                 ('o_w_out', None, False), ('ffn_w_up', 0, True), ('ffn_w_up', 1, True), ('ffn_w_down', 0, False), ('ffn_w_down', 1, False)]
SMALL_SHARDED = [n for n in WEIGHTS if n in SHARDED and n not in MATMUL_WEIGHTS]
REPLICATED = [n for n in WEIGHTS if n not in SHARDED]


def _call(fn, grid, ins, outs, name):
    n_in = len(ins)

    def body(*refs):
        vals = fn(*[r[...] for r in refs[:n_in]])
        first = pl.program_id(0) == 0
        for r, v, o in zip(refs[n_in:], vals, outs):
            if o[4]:
                @pl.when(first)
                def _():
                    r[...] = jnp.zeros_like(r)
                r[...] += v.astype(r.dtype)
            else:
                r[...] = v.astype(r.dtype)

    return pl.pallas_call(
        body, grid=grid,
        in_specs=[pl.BlockSpec(b, m) for _, b, m in ins],
        out_specs=[pl.BlockSpec(o[2], o[3]) for o in outs],
        out_shape=[jax.ShapeDtypeStruct(o[0], o[1]) for o in outs],
        compiler_params=pltpu.CompilerParams(dimension_semantics=("arbitrary",) * len(grid), vmem_limit_bytes=VMEM_LIMIT),
        name=name)(*[a for a, _, _ in ins])


def _rows(fn, n_rows, row_ins, full_ins, row_outs, acc_outs, name, tm=256, bf16_outs=()):
    tm = min(tm, n_rows)
    ins = [(a, (tm, w), (lambda i, c=c: (i, c))) for a, w, c in row_ins]
    ins += [(a, a.shape, (lambda i, n=a.ndim: (0,) * n)) for a in full_ins]
    outs = [((n_rows, w), BF16 if k in bf16_outs else F32, (tm, w), (lambda i: (i, 0)), False) for k, w in enumerate(row_outs)]
    outs += [(tuple(s), F32, tuple(s), (lambda i, n=len(s): (0,) * n), True) for s in acc_outs]
    return _call(fn, (n_rows // tm,), ins, outs, name)


def _cols(fn, n_blocks, col_ins, out_leads, name, cb=LANES, bf16_outs=()):
    ins = [(a, a.shape[:-1] + (cb,), (lambda j, n=a.ndim, o=o: (0,) * (n - 1) + (j + o,))) for a, o in col_ins]
    outs = [(tuple(s) + (n_blocks * cb,), BF16 if k in bf16_outs else F32, tuple(s) + (cb,), (lambda j, n=len(s): (0,) * n + (j,)), False)
            for k, s in enumerate(out_leads)]
    return _call(fn, (n_blocks,), ins, outs, name)


def _grad_fn(f, n_diff, n_in):
    def g(*a):
        diff, consts, cts = a[:n_diff], a[n_diff:n_in], a[n_in:]
        _, vjp = jax.vjp(lambda *d: f(*d, *consts), *diff)
        return vjp(tuple(cts))
    return g


def _silu(x):
    return x * jax.nn.sigmoid(x)


def _rms(x, g):
    return x * lax.rsqrt(jnp.mean(x * x, axis=-1, keepdims=True) + EPS) * g


@jax.custom_vjp
def _softplus(x):
    return jnp.maximum(x, 0.0) + jnp.log(1.0 + jnp.exp(-jnp.abs(x)))


_softplus.defvjp(lambda x: (_softplus(x), x), lambda x, g: (g * jax.nn.sigmoid(x),))


@jax.custom_vjp
def _swap_halves(x):
    return pltpu.roll(x, 64, 1)


_swap_halves.defvjp(lambda x: (_swap_halves(x), None), lambda _, g: (_swap_halves(g),))


def _shift_rows(x, k, up):
    if k == 0:
        return x
    n = x.shape[0]
    t = lax.broadcasted_iota(jnp.int32, x.shape, 0)
    if up:
        return jnp.where(t < n - k, pltpu.roll(x, n - k, 0), 0.0)
    return jnp.where(t >= k, pltpu.roll(x, k, 0), 0.0)


@jax.custom_vjp
def _dwconv(x, w, b):
    k_taps = w.shape[0]
    y = b + w[k_taps - 1] * x
    for k in range(k_taps - 1):
        y = y + w[k] * _shift_rows(x, k_taps - 1 - k, False)
    return y


def _dwconv_fwd(x, w, b):
    return _dwconv(x, w, b), (x, w)


def _dwconv_bwd(saved, dy):
    x, w = saved
    k_taps = w.shape[0]
    dx = w[k_taps - 1] * dy
    dws = []
    for k in range(k_taps - 1):
        s = k_taps - 1 - k
        dx = dx + w[k] * _shift_rows(dy, s, True)
        dws.append(jnp.sum(dy * _shift_rows(x, s, False), axis=0, keepdims=True)[None])
    dws.append(jnp.sum(dy * x, axis=0, keepdims=True)[None])
    return dx, jnp.concatenate(dws, axis=0), jnp.sum(dy, axis=0, keepdims=True)


_dwconv.defvjp(_dwconv_fwd, _dwconv_bwd)


def _f_rms(x, g):
    return (_rms(x, g),)


def _rot(x, cosf, sins):
    outs = []
    for h in range(RET_HEADS):
        xh = x[:, h * RET_DK:(h + 1) * RET_DK]
        outs.append(xh * cosf + _swap_halves(xh) * sins)
    return jnp.concatenate(outs, axis=1)


def _f_retpre(q, k, cosf, sins):
    return _rot(q, cosf, sins), _rot(k, cosf, sins) * (RET_DK ** -0.5)


def _f_retpost(r, g):
    outs = []
    for h in range(RET_HEADS):
        rh = r[:, h * RET_DV:(h + 1) * RET_DV]
        rc = rh - jnp.mean(rh, axis=-1, keepdims=True)
        outs.append(_silu(g[:, h * RET_DV:(h + 1) * RET_DV]) * (rc * lax.rsqrt(jnp.mean(rc * rc, axis=-1, keepdims=True) + EPS)))
    return (jnp.concatenate(outs, axis=1),)


def _f_ssdconv(xbc, w, b):
    return (_silu(_dwconv(xbc, w, b)),)


def _f_ssdpre(xs, dtr, bias, alog, expand):
    dt = _softplus(dtr + bias)
    return xs * jnp.dot(dt, expand, precision=HIGHEST, preferred_element_type=F32), dt * (-jnp.exp(alog))


def _f_ssdpost(yc, xs, z, dskip, norm_w, expand):
    d_wide = jnp.dot(jnp.broadcast_to(dskip, (yc.shape[0], LANES)), expand, precision=HIGHEST, preferred_element_type=F32)
    y = (yc + d_wide * xs) * _silu(z)
    half = y.shape[1] // SSM_GROUPS
    outs = []
    for g in range(SSM_GROUPS):
        yg = y[:, g * half:(g + 1) * half]
        outs.append(yg * lax.rsqrt(jnp.mean(yg * yg, axis=-1, keepdims=True) + EPS))
    return (jnp.concatenate(outs, axis=1) * norm_w,)


def _f_ffnmid(gin, uin, wg, wu, bg, bu):
    return (_silu(_dwconv(gin, wg, bg)) * _dwconv(uin, wu, bu),)


def _f_confb(c1, g, b):
    mu = jnp.mean(c1, axis=-1, keepdims=True)
    xc = c1 - mu
    return (_silu(xc * lax.rsqrt(jnp.mean(xc * xc, axis=-1, keepdims=True) + EPS) * g + b),)


def _f_s5post(yc, u, dskip, glu_w):
    s = jax.nn.gelu(yc + dskip * u)
    z = jnp.dot(s.astype(BF16), glu_w.astype(BF16), preferred_element_type=F32)
    return (s * jax.nn.sigmoid(z),)


def _f_s5par(lr, li, ls, br, bi):
    step = jnp.exp(ls)
    mag = jnp.exp(lr * step)
    ab_re = mag * jnp.cos(li * step)
    ab_im = mag * jnp.sin(li * step)
    den = lr * lr + li * li
    f_re = ((ab_re - 1.0) * lr + ab_im * li) / den
    f_im = (ab_im * lr - (ab_re - 1.0) * li) / den
    return ab_re, ab_im, f_re[None] * br - f_im[None] * bi, f_re[None] * bi + f_im[None] * br


def _loss_step(x, tgt, g):
    def f(x_, g_):
        e = _rms(x_, g_) - tgt
        return 0.5 * jnp.sum(jnp.mean(e * e, axis=-1, keepdims=True), axis=0, keepdims=True)
    loss, vjp = jax.vjp(f, x, g)
    dx, dg = vjp(jnp.ones((1, 1), F32))
    return dx, dx, jnp.broadcast_to(loss, (8, LANES)), dg


def _tile(n, pref):
    if n <= pref:
        return n
    t = (pref // LANES) * LANES
    while n % t:
        t -= LANES
    return t


_DIMS = {"nn": (((1,), (0,)), ((), ())), "nt": (((1,), (1,)), ((), ())), "tn": (((0,), (0,)), ((), ()))}


def _mm(a, b, mode, name, res=None, out_dtype=F32, tm=1024, tn=1408, tk=1408):
    if mode == "nn":
        (m, k), n = a.shape, b.shape[1]
    elif mode == "nt":
        (m, k), n = a.shape, b.shape[0]
    else:
        (k, m), n = a.shape, b.shape[1]
    tm, tn, tk = _tile(m, tm), _tile(n, tn), _tile(k, tk)
    nk = k // tk
    a_spec = pl.BlockSpec((tk, tm), lambda i, j, kk: (kk, i)) if mode == "tn" else pl.BlockSpec((tm, tk), lambda i, j, kk: (i, kk))
    b_spec = pl.BlockSpec((tn, tk), lambda i, j, kk: (j, kk)) if mode == "nt" else pl.BlockSpec((tk, tn), lambda i, j, kk: (kk, j))
    o_spec = pl.BlockSpec((tm, tn), lambda i, j, kk: (i, j))
    has_res = res is not None

    def body(*refs):
        a_ref, b_ref = refs[0], refs[1]
        o_ref, acc = refs[-2], refs[-1]
        kk = pl.program_id(2)

        @pl.when(kk == 0)
        def _():
            acc[...] = jnp.zeros_like(acc)

        acc[...] += lax.dot_general(a_ref[...].astype(BF16), b_ref[...].astype(BF16), _DIMS[mode], preferred_element_type=F32)

        @pl.when(kk == nk - 1)
        def _():
            o_ref[...] = (acc[...] + refs[2][...] if has_res else acc[...]).astype(out_dtype)

    return pl.pallas_call(
        body, grid=(m // tm, n // tn, nk),
        in_specs=[a_spec, b_spec] + ([o_spec] if has_res else []),
        out_specs=o_spec, out_shape=jax.ShapeDtypeStruct((m, n), out_dtype),
        scratch_shapes=[pltpu.VMEM((tm, tn), F32)],
        compiler_params=pltpu.CompilerParams(dimension_semantics=("parallel", "parallel", "arbitrary"), vmem_limit_bytes=VMEM_LIMIT),
        name=name)(*([a, b] + ([res] if has_res else [])))


def _seq_block(t):
    return min(512, t)


def _causal_diff(i, j, blk):
    r = lax.broadcasted_iota(jnp.int32, (blk, blk), 0)
    c = lax.broadcasted_iota(jnp.int32, (blk, blk), 1)
    return (i - j) * blk + r - c


def _ret_decay(lg, i, j, blk):
    diff = _causal_diff(i, j, blk)
    return jnp.where(diff >= 0, jnp.exp(lg * jnp.maximum(diff, 0).astype(F32)), 0.0)


def _ret_row_decays(lg, i, j, blk):
    row = lax.broadcasted_iota(jnp.int32, (blk, RET_DK), 0)
    return jnp.exp(lg * row.astype(F32)), jnp.exp(lg * ((i - j) * blk - row).astype(F32))


def _ret_scaled(lg, i, j, blk, q_ref, k_ref):
    a, b = _ret_row_decays(lg, i, j, blk)
    return (q_ref[...] * a).astype(BF16), (k_ref[...] * b).astype(BF16)


def _ret_log_gamma():
    return jnp.log1p(-(2.0 ** (-5.0 - jnp.arange(RET_HEADS, dtype=F32))))


def _ret_fwd(qr, kr, proj, t):
    blk = _seq_block(t)
    nb = t // blk
    v_off = (2 * RET_HEADS * RET_DK) // RET_DV

    def body(lg_ref, q_ref, k_ref, v_ref, o_ref, acc):
        h, i, j = pl.program_id(0), pl.program_id(1), pl.program_id(2)

        @pl.when(j == 0)
        def _():
            acc[...] = jnp.zeros_like(acc)

        @pl.when(j < i)
        def _():
            qa, kb = _ret_scaled(lg_ref[h], i, j, blk, q_ref, k_ref)
            p = lax.dot_general(qa, kb, _DIMS["nt"], preferred_element_type=F32).astype(BF16)
            acc[...] += jnp.dot(p, v_ref[...].astype(BF16), preferred_element_type=F32)

        @pl.when(j == i)
        def _():
            s = lax.dot_general(q_ref[...].astype(BF16), k_ref[...].astype(BF16), _DIMS["nt"], preferred_element_type=F32)
            p = (s * _ret_decay(lg_ref[h], i, j, blk)).astype(BF16)
            o_ref[...] = acc[...] + jnp.dot(p, v_ref[...].astype(BF16), preferred_element_type=F32)

    return pl.pallas_call(
        body, grid=(RET_HEADS, nb, nb),
        in_specs=[pl.BlockSpec(memory_space=pltpu.SMEM),
                  pl.BlockSpec((blk, RET_DK), lambda h, i, j: (i, h)),
                  pl.BlockSpec((blk, RET_DK), lambda h, i, j: (jnp.minimum(j, i), h)),
                  pl.BlockSpec((blk, RET_DV), lambda h, i, j: (jnp.minimum(j, i), v_off + h))],
        out_specs=pl.BlockSpec((blk, RET_DV), lambda h, i, j: (i, h)),
        out_shape=jax.ShapeDtypeStruct((t, RET_HEADS * RET_DV), F32),
        scratch_shapes=[pltpu.VMEM((blk, RET_DV), F32)],
        compiler_params=pltpu.CompilerParams(dimension_semantics=("arbitrary",) * 3, vmem_limit_bytes=VMEM_LIMIT),
        name="ret_fwd")(_ret_log_gamma(), qr, kr, proj)


def _ret_bwd(qr, kr, proj, dr, t):
    blk = _seq_block(t)
    nb = t // blk
    v_off = (2 * RET_HEADS * RET_DK) // RET_DV

    def dq_body(lg_ref, q_ref, k_ref, v_ref, do_ref, dq_ref, acc):
        h, i, j = pl.program_id(0), pl.program_id(1), pl.program_id(2)

        @pl.when(j == 0)
        def _():
            acc[...] = jnp.zeros_like(acc)

        @pl.when(j < i)
        def _():
            a, b = _ret_row_decays(lg_ref[h], i, j, blk)
            ds = lax.dot_general(do_ref[...].astype(BF16), v_ref[...].astype(BF16), _DIMS["nt"], preferred_element_type=F32)
            acc[...] += a * jnp.dot(ds.astype(BF16), (k_ref[...] * b).astype(BF16), preferred_element_type=F32)

        @pl.when(j == i)
        def _():
            ds = lax.dot_general(do_ref[...].astype(BF16), v_ref[...].astype(BF16), _DIMS["nt"], preferred_element_type=F32)
            dsm = (ds * _ret_decay(lg_ref[h], i, j, blk)).astype(BF16)
            dq_ref[...] = acc[...] + jnp.dot(dsm, k_ref[...].astype(BF16), preferred_element_type=F32)

    dq = pl.pallas_call(
        dq_body, grid=(RET_HEADS, nb, nb),
        in_specs=[pl.BlockSpec(memory_space=pltpu.SMEM),
                  pl.BlockSpec((blk, RET_DK), lambda h, i, j: (i, h)),
                  pl.BlockSpec((blk, RET_DK), lambda h, i, j: (jnp.minimum(j, i), h)),
                  pl.BlockSpec((blk, RET_DV), lambda h, i, j: (jnp.minimum(j, i), v_off + h)),
                  pl.BlockSpec((blk, RET_DV), lambda h, i, j: (i, h))],
        out_specs=pl.BlockSpec((blk, RET_DK), lambda h, i, j: (i, h)),
        out_shape=jax.ShapeDtypeStruct((t, RET_HEADS * RET_DK), F32),
        scratch_shapes=[pltpu.VMEM((blk, RET_DK), F32)],
        compiler_params=pltpu.CompilerParams(dimension_semantics=("arbitrary",) * 3, vmem_limit_bytes=VMEM_LIMIT),
        name="ret_bwd_dq")(_ret_log_gamma(), qr, kr, proj, dr)

    def dkv_body(lg_ref, q_ref, k_ref, v_ref, do_ref, dk_ref, dv_ref, acc_k, acc_v):
        h, j, i = pl.program_id(0), pl.program_id(1), pl.program_id(2)

        @pl.when(i == 0)
        def _():
            acc_k[...] = jnp.zeros_like(acc_k)
            acc_v[...] = jnp.zeros_like(acc_v)

        @pl.when(i > j)
        def _():
            a, b = _ret_row_decays(lg_ref[h], i, j, blk)
            qa, kb = (q_ref[...] * a).astype(BF16), (k_ref[...] * b).astype(BF16)
            do = do_ref[...].astype(BF16)
            p = lax.dot_general(qa, kb, _DIMS["nt"], preferred_element_type=F32).astype(BF16)
            acc_v[...] += lax.dot_general(p, do, _DIMS["tn"], preferred_element_type=F32)
            ds = lax.dot_general(do, v_ref[...].astype(BF16), _DIMS["nt"], preferred_element_type=F32).astype(BF16)
            acc_k[...] += b * lax.dot_general(ds, qa, _DIMS["tn"], preferred_element_type=F32)

        @pl.when(i == j)
        def _():
            q = q_ref[...].astype(BF16)
            do = do_ref[...].astype(BF16)
            decay = _ret_decay(lg_ref[h], i, j, blk)
            s = lax.dot_general(q, k_ref[...].astype(BF16), _DIMS["nt"], preferred_element_type=F32)
            acc_v[...] += lax.dot_general((s * decay).astype(BF16), do, _DIMS["tn"], preferred_element_type=F32)
            ds = lax.dot_general(do, v_ref[...].astype(BF16), _DIMS["nt"], preferred_element_type=F32)
            acc_k[...] += lax.dot_general((ds * decay).astype(BF16), q, _DIMS["tn"], preferred_element_type=F32)

        @pl.when(i == nb - 1)
        def _():
            dk_ref[...] = acc_k[...]
            dv_ref[...] = acc_v[...].astype(BF16)

    dk, dv = pl.pallas_call(
        dkv_body, grid=(RET_HEADS, nb, nb),
        in_specs=[pl.BlockSpec(memory_space=pltpu.SMEM),
                  pl.BlockSpec((blk, RET_DK), lambda h, j, i: (jnp.maximum(i, j), h)),
                  pl.BlockSpec((blk, RET_DK), lambda h, j, i: (j, h)),
                  pl.BlockSpec((blk, RET_DV), lambda h, j, i: (j, v_off + h)),
                  pl.BlockSpec((blk, RET_DV), lambda h, j, i: (jnp.maximum(i, j), h))],
        out_specs=[pl.BlockSpec((blk, RET_DK), lambda h, j, i: (j, h)), pl.BlockSpec((blk, RET_DV), lambda h, j, i: (j, h))],
        out_shape=[jax.ShapeDtypeStruct((t, RET_HEADS * RET_DK), F32), jax.ShapeDtypeStruct((t, RET_HEADS * RET_DV), BF16)],
        scratch_shapes=[pltpu.VMEM((blk, RET_DK), F32), pltpu.VMEM((blk, RET_DV), F32)],
        compiler_params=pltpu.CompilerParams(dimension_semantics=("arbitrary",) * 3, vmem_limit_bytes=VMEM_LIMIT),
        name="ret_bwd_dkv")(_ret_log_gamma(), qr, kr, proj, dr)
    return dq, dk, dv


def _cumsum(x, reverse, name):
    t = x.shape[0]
    blk = _seq_block(t)
    nb = t // blk

    def body(x_ref, o_ref, carry):
        @pl.when(pl.program_id(0) == 0)
        def _():
            carry[...] = jnp.zeros_like(carry)

        r = lax.broadcasted_iota(jnp.int32, (blk, blk), 0)
        c = lax.broadcasted_iota(jnp.int32, (blk, blk), 1)
        tri = ((r <= c) if reverse else (r >= c)).astype(F32)
        o_ref[...] = jnp.dot(tri, x_ref[...], precision=HIGHEST, preferred_element_type=F32) + carry[...]
        carry[...] = o_ref[0:1, :] if reverse else o_ref[blk - 1:blk, :]

    idx = (lambda i: (nb - 1 - i, 0)) if reverse else (lambda i: (i, 0))
    return pl.pallas_call(
        body, grid=(nb,), in_specs=[pl.BlockSpec((blk, LANES), idx)], out_specs=pl.BlockSpec((blk, LANES), idx),
        out_shape=jax.ShapeDtypeStruct((t, LANES), F32), scratch_shapes=[pltpu.VMEM((1, LANES), F32)],
        compiler_params=pltpu.CompilerParams(dimension_semantics=("arbitrary",), vmem_limit_bytes=VMEM_LIMIT),
        name=name)(x)


def _ssd_decay(a_ref, at_ref, hh, mask):
    return jnp.exp(jnp.where(mask, a_ref[:, hh:hh + 1] - at_ref[hh:hh + 1, :], -jnp.inf))


def _head_lanes(s):
    lane = lax.broadcasted_iota(jnp.int32, (s.shape[0], LANES), 1)
    return jnp.concatenate([jnp.where(lane < SSM_P, s[:, 2 * p:2 * p + 1], s[:, 2 * p + 1:2 * p + 2])
                            for p in range(SSM_HEADS // 2)], axis=1)


_B_OFF, _C_OFF = 1024 // SSM_N, 1024 // SSM_N + SSM_GROUPS


def _ssd_fwd(xbc_act, xdt, a_cum, a_cum_t, t):
    blk = _seq_block(t)
    nb = t // blk

    def body(c0, c1, b0, b1, x_ref, ai_ref, aj_ref, at_ref, o_ref, acc):
        i, j = pl.program_id(0), pl.program_id(1)

        @pl.when(j == 0)
        def _():
            acc[...] = jnp.zeros_like(acc)

        @pl.when(j < i)
        def _():
            xv = (x_ref[...] * _head_lanes(jnp.exp(ai_ref[0:1, :] - aj_ref[...]))).astype(BF16)
            for g, (c_ref, b_ref) in enumerate(((c0, b0), (c1, b1))):
                cb = lax.dot_general(c_ref[...].astype(BF16), b_ref[...].astype(BF16), _DIMS["nt"], preferred_element_type=F32)
                cols = slice(g * half, (g + 1) * half)
                acc[:, cols] += jnp.dot(cb.astype(BF16), xv[:, cols], preferred_element_type=F32)

        @pl.when(j == i)
        def _():
            o_ref[...] = acc[...] * _head_lanes(jnp.exp(ai_ref[...] - ai_ref[0:1, :]))
            mask = _causal_diff(i, j, blk) >= 0
            for g, (c_ref, b_ref) in enumerate(((c0, b0), (c1, b1))):
                cb = lax.dot_general(c_ref[...].astype(BF16), b_ref[...].astype(BF16), _DIMS["nt"], preferred_element_type=F32)
                for h in range(SSM_HG):
                    hh = g * SSM_HG + h
                    cols = slice(hh * SSM_P, (hh + 1) * SSM_P)
                    m = (cb * _ssd_decay(ai_ref, at_ref, hh, mask)).astype(BF16)
                    o_ref[:, cols] += jnp.dot(m, x_ref[:, cols].astype(BF16), preferred_element_type=F32)

    half = SSM_HG * SSM_P
    row_i = lambda off: pl.BlockSpec((blk, SSM_N), lambda i, j, off=off: (i, off))
    row_j = lambda off: pl.BlockSpec((blk, SSM_N), lambda i, j, off=off: (jnp.minimum(j, i), off))
    return pl.pallas_call(
        body, grid=(nb, nb),
        in_specs=[row_i(_C_OFF), row_i(_C_OFF + 1), row_j(_B_OFF), row_j(_B_OFF + 1),
                  pl.BlockSpec((blk, SSM_HEADS * SSM_P), lambda i, j: (jnp.minimum(j, i), 0)),
                  pl.BlockSpec((blk, LANES), lambda i, j: (i, 0)),
                  pl.BlockSpec((blk, LANES), lambda i, j: (jnp.minimum(j, i), 0)),
                  pl.BlockSpec((SSM_HEADS, blk), lambda i, j: (0, jnp.minimum(j, i)))],
        out_specs=pl.BlockSpec((blk, SSM_HEADS * SSM_P), lambda i, j: (i, 0)),
        out_shape=jax.ShapeDtypeStruct((t, SSM_HEADS * SSM_P), F32),
        scratch_shapes=[pltpu.VMEM((blk, SSM_HEADS * SSM_P), F32)],
        compiler_params=pltpu.CompilerParams(dimension_semantics=("arbitrary",) * 2, vmem_limit_bytes=VMEM_LIMIT),
        name="ssd_fwd")(xbc_act, xbc_act, xbc_act, xbc_act, xdt, a_cum, a_cum, a_cum_t)


def _ssd_bwd(xbc_act, xdt, a_cum, a_cum_t, dy, expand, t):
    blk = _seq_block(t)
    nb = t // blk
    width = SSM_HEADS * SSM_P

    half = SSM_HG * SSM_P

    def head_sums(prod, e_ref):
        return lax.dot_general(prod, e_ref[...], _DIMS["nt"], precision=HIGHEST, preferred_element_type=F32)

    def q_body(c0, c1, b0, b1, x_ref, ai_ref, aj_ref, at_ref, dy_ref, e_ref, dc_ref, da_ref, acc_c, acc_a, acc_p, dyu):
        i, j = pl.program_id(0), pl.program_id(1)

        @pl.when(j == 0)
        def _():
            acc_c[...] = jnp.zeros_like(acc_c)
            acc_a[...] = jnp.zeros_like(acc_a)
            acc_p[...] = jnp.zeros_like(acc_p)
            dyu[...] = (dy_ref[...].astype(F32) * _head_lanes(jnp.exp(ai_ref[...] - ai_ref[0:1, :]))).astype(BF16)

        @pl.when(j < i)
        def _():
            xv = (x_ref[...] * _head_lanes(jnp.exp(ai_ref[0:1, :] - aj_ref[...]))).astype(BF16)
            for g, (c_ref, b_ref) in enumerate(((c0, b0), (c1, b1))):
                cols = slice(g * half, (g + 1) * half)
                bj = b_ref[...].astype(BF16)
                cb = lax.dot_general(c_ref[...].astype(BF16), bj, _DIMS["nt"], preferred_element_type=F32).astype(BF16)
                dcb = lax.dot_general(dyu[:, cols], xv[:, cols], _DIMS["nt"], preferred_element_type=F32)
                acc_c[:, g * SSM_N:(g + 1) * SSM_N] += jnp.dot(dcb.astype(BF16), bj, preferred_element_type=F32)
                acc_p[:, cols] += dyu[:, cols].astype(F32) * jnp.dot(cb, xv[:, cols], preferred_element_type=F32)

        @pl.when(j == i)
        def _():
            mask = _causal_diff(i, j, blk) >= 0
            for g, (c_ref, b_ref) in enumerate(((c0, b0), (c1, b1))):
                bj = b_ref[...].astype(BF16)
                cb = lax.dot_general(c_ref[...].astype(BF16), bj, _DIMS["nt"], preferred_element_type=F32)
                dcb = jnp.zeros((blk, blk), F32)
                for h in range(SSM_HG):
                    hh = g * SSM_HG + h
                    cols = slice(hh * SSM_P, (hh + 1) * SSM_P)
                    dm = lax.dot_general(dy_ref[:, cols].astype(BF16), x_ref[:, cols].astype(BF16), _DIMS["nt"],
                                         preferred_element_type=F32) * _ssd_decay(ai_ref, at_ref, hh, mask)
                    dcb = dcb + dm
                    acc_a[:, hh:hh + 1] += jnp.sum(dm * cb, axis=1, keepdims=True)
                acc_c[:, g * SSM_N:(g + 1) * SSM_N] += jnp.dot(dcb.astype(BF16), bj, preferred_element_type=F32)
            dc_ref[...] = acc_c[...]
            da_ref[...] = acc_a[...] + head_sums(acc_p[...], e_ref)

    row_i = lambda off: pl.BlockSpec((blk, SSM_N), lambda i, j, off=off: (i, off))
    row_j = lambda off: pl.BlockSpec((blk, SSM_N), lambda i, j, off=off: (jnp.minimum(j, i), off))
    e_spec = pl.BlockSpec((LANES, width), lambda i, j: (0, 0))
    dc, da_q = pl.pallas_call(
        q_body, grid=(nb, nb),
        in_specs=[row_i(_C_OFF), row_i(_C_OFF + 1), row_j(_B_OFF), row_j(_B_OFF + 1),
                  pl.BlockSpec((blk, width), lambda i, j: (jnp.minimum(j, i), 0)),
                  pl.BlockSpec((blk, LANES), lambda i, j: (i, 0)),
                  pl.BlockSpec((blk, LANES), lambda i, j: (jnp.minimum(j, i), 0)),
                  pl.BlockSpec((SSM_HEADS, blk), lambda i, j: (0, jnp.minimum(j, i))),
                  pl.BlockSpec((blk, width), lambda i, j: (i, 0)), e_spec],
        out_specs=[pl.BlockSpec((blk, SSM_GROUPS * SSM_N), lambda i, j: (i, 0)), pl.BlockSpec((blk, LANES), lambda i, j: (i, 0))],
        out_shape=[jax.ShapeDtypeStruct((t, SSM_GROUPS * SSM_N), F32), jax.ShapeDtypeStruct((t, LANES), F32)],
        scratch_shapes=[pltpu.VMEM((blk, SSM_GROUPS * SSM_N), F32), pltpu.VMEM((blk, LANES), F32), pltpu.VMEM((blk, width), F32),
                        pltpu.VMEM((blk, width), BF16)],
        compiler_params=pltpu.CompilerParams(dimension_semantics=("arbitrary",) * 2, vmem_limit_bytes=VMEM_LIMIT),
        name="ssd_bwd_q")(xbc_act, xbc_act, xbc_act, xbc_act, xdt, a_cum, a_cum, a_cum_t, dy, expand)

    def k_body(c0, c1, b0, b1, x_ref, ai_ref, aj_ref, at_ref, dy_ref, e_ref, db_ref, dx_ref, da_ref, dat_ref, acc_b, acc_x, acc_a, acc_p):
        j, i = pl.program_id(0), pl.program_id(1)

        @pl.when(i == 0)
        def _():
            acc_b[...] = jnp.zeros_like(acc_b)
            acc_x[...] = jnp.zeros_like(acc_x)
            acc_a[...] = jnp.zeros_like(acc_a)
            acc_p[...] = jnp.zeros_like(acc_p)

        @pl.when(i > j)
        def _():
            v = _head_lanes(jnp.exp(ai_ref[0:1, :] - aj_ref[...]))
            dyu_all = (dy_ref[...].astype(F32) * _head_lanes(jnp.exp(ai_ref[...] - ai_ref[0:1, :]))).astype(BF16)
            xv = (x_ref[...] * v).astype(BF16)
            for g, (c_ref, b_ref) in enumerate(((c0, b0), (c1, b1))):
                cols = slice(g * half, (g + 1) * half)
                ci = c_ref[...].astype(BF16)
                cb = lax.dot_general(ci, b_ref[...].astype(BF16), _DIMS["nt"], preferred_element_type=F32).astype(BF16)
                dcb = lax.dot_general(dyu_all[:, cols], xv[:, cols], _DIMS["nt"], preferred_element_type=F32)
                acc_b[:, g * SSM_N:(g + 1) * SSM_N] += lax.dot_general(dcb.astype(BF16), ci, _DIMS["tn"], preferred_element_type=F32)
                dxv = lax.dot_general(cb, dyu_all[:, cols], _DIMS["tn"], preferred_element_type=F32)
                acc_x[:, cols] += v[:, cols] * dxv
                acc_p[:, cols] += xv[:, cols].astype(F32) * dxv

        @pl.when(i == j)
        def _():
            mask = _causal_diff(i, j, blk) >= 0
            for g, (c_ref, b_ref) in enumerate(((c0, b0), (c1, b1))):
                ci = c_ref[...].astype(BF16)
                cb = lax.dot_general(ci, b_ref[...].astype(BF16), _DIMS["nt"], preferred_element_type=F32)
                dcb = jnp.zeros((blk, blk), F32)
                for h in range(SSM_HG):
                    hh = g * SSM_HG + h
                    cols = slice(hh * SSM_P, (hh + 1) * SSM_P)
                    decay = _ssd_decay(ai_ref, at_ref, hh, mask)
                    dyh = dy_ref[:, cols].astype(BF16)
                    acc_x[:, cols] += lax.dot_general((cb * decay).astype(BF16), dyh, _DIMS["tn"], preferred_element_type=F32)
                    dm = lax.dot_general(dyh, x_ref[:, cols].astype(BF16), _DIMS["nt"], preferred_element_type=F32) * decay
                    dcb = dcb + dm
                    acc_a[hh:hh + 1, :] += jnp.sum(dm * cb, axis=0, keepdims=True)
                acc_b[:, g * SSM_N:(g + 1) * SSM_N] += lax.dot_general(dcb.astype(BF16), ci, _DIMS["tn"], preferred_element_type=F32)

        @pl.when(i == nb - 1)
        def _():
            db_ref[...] = acc_b[...]
            dx_ref[...] = acc_x[...]
            dat_ref[...] = acc_a[...]
            da_ref[...] = head_sums(acc_p[...], e_ref)

    rowk_i = lambda off: pl.BlockSpec((blk, SSM_N), lambda j, i, off=off: (jnp.maximum(i, j), off))
    rowk_j = lambda off: pl.BlockSpec((blk, SSM_N), lambda j, i, off=off: (j, off))
    db, dx, da_k, da_k_t = pl.pallas_call(
        k_body, grid=(nb, nb),
        in_specs=[rowk_i(_C_OFF), rowk_i(_C_OFF + 1), rowk_j(_B_OFF), rowk_j(_B_OFF + 1),
                  pl.BlockSpec((blk, width), lambda j, i: (j, 0)),
                  pl.BlockSpec((blk, LANES), lambda j, i: (jnp.maximum(i, j), 0)),
                  pl.BlockSpec((blk, LANES), lambda j, i: (j, 0)),
                  pl.BlockSpec((SSM_HEADS, blk), lambda j, i: (0, j)),
                  pl.BlockSpec((blk, width), lambda j, i: (jnp.maximum(i, j), 0)), e_spec],
        out_specs=[pl.BlockSpec((blk, SSM_GROUPS * SSM_N), lambda j, i: (j, 0)), pl.BlockSpec((blk, width), lambda j, i: (j, 0)),
                   pl.BlockSpec((blk, LANES), lambda j, i: (j, 0)), pl.BlockSpec((SSM_HEADS, blk), lambda j, i: (0, j))],
        out_shape=[jax.ShapeDtypeStruct((t, SSM_GROUPS * SSM_N), F32), jax.ShapeDtypeStruct((t, width), F32),
                   jax.ShapeDtypeStruct((t, LANES), F32), jax.ShapeDtypeStruct((SSM_HEADS, t), F32)],
        scratch_shapes=[pltpu.VMEM((blk, SSM_GROUPS * SSM_N), F32), pltpu.VMEM((blk, width), F32), pltpu.VMEM((SSM_HEADS, blk), F32),
                        pltpu.VMEM((blk, width), F32)],
        compiler_params=pltpu.CompilerParams(dimension_semantics=("arbitrary",) * 2, vmem_limit_bytes=VMEM_LIMIT),
        name="ssd_bwd_k")(xbc_act, xbc_act, xbc_act, xbc_act, xdt, a_cum, a_cum, a_cum_t, dy, expand)
    return dc, da_q, db, dx, da_k, da_k_t


def _s5_scan(bu, a_re, a_im, reverse, x_prev=None):
    t = bu.shape[0]
    cb = LANES
    ncb = S5_COLS // cb
    pad = t // 2
    chunk = min(512, t)
    base = 0 if reverse else pad
    zero0 = t if reverse else 0

    def body(*refs):
        if reverse:
            br_ref, bi_ref, ar_ref, ai_ref, xr_ref, xi_ref, or_ref, oi_ref, dar_ref, dai_ref, sr, si = refs
        else:
            br_ref, bi_ref, ar_ref, ai_ref, or_ref, oi_ref, or_bf_ref, oi_bf_ref, sr, si = refs
        sr[pl.ds(zero0, pad), :] = jnp.zeros((pad, cb), F32)
        si[pl.ds(zero0, pad), :] = jnp.zeros((pad, cb), F32)
        sr[pl.ds(base, t), :] = br_ref[...]
        si[pl.ds(base, t), :] = bi_ref[...]
        pr = ar_ref[...]
        pi = -ai_ref[...] if reverse else ai_ref[...]
        s = 1
        while s < t:
            shift = s if reverse else -s

            def step(t0, pr=pr, pi=pi, shift=shift):
                cr, ci = sr[pl.ds(base + t0, chunk), :], si[pl.ds(base + t0, chunk), :]
                qr, qi = sr[pl.ds(base + t0 + shift, chunk), :], si[pl.ds(base + t0 + shift, chunk), :]
                sr[pl.ds(base + t0, chunk), :] = cr + pr * qr - pi * qi
                si[pl.ds(base + t0, chunk), :] = ci + pr * qi + pi * qr

            order = range(0, t, chunk) if reverse else range(t - chunk, -1, -chunk)
            for t0 in order:
                step(t0)
            pr, pi = pr * pr - pi * pi, 2.0 * pr * pi
            s *= 2
        or_ref[...] = sr[pl.ds(base, t), :].astype(or_ref.dtype)
        oi_ref[...] = si[pl.ds(base, t), :].astype(oi_ref.dtype)
        if not reverse:
            or_bf_ref[...] = sr[pl.ds(base, t), :].astype(BF16)
            oi_bf_ref[...] = si[pl.ds(base, t), :].astype(BF16)
        if reverse:
            dar = jnp.zeros((1, cb), F32)
            dai = jnp.zeros((1, cb), F32)
            for t0 in range(0, t, chunk):
                gr, gi = sr[pl.ds(t0 + 1, chunk), :], si[pl.ds(t0 + 1, chunk), :]
                xr, xi = xr_ref[pl.ds(t0, chunk), :], xi_ref[pl.ds(t0, chunk), :]
                dar = dar + jnp.sum(gr * xr + gi * xi, axis=0, keepdims=True)
                dai = dai + jnp.sum(gi * xr - gr * xi, axis=0, keepdims=True)
            dar_ref[...] = dar
            dai_ref[...] = dai

    re_spec = pl.BlockSpec((t, cb), lambda j: (0, j))
    im_spec = pl.BlockSpec((t, cb), lambda j: (0, j + ncb))
    a_spec = pl.BlockSpec((1, cb), lambda j: (0, j))
    ins, in_specs = [bu, bu, a_re, a_im], [re_spec, im_spec, a_spec, a_spec]
    seq_bf = jax.ShapeDtypeStruct((t, S5_COLS), BF16)
    if reverse:
        ins += [x_prev, x_prev]
        in_specs += [re_spec, im_spec]
        out_shape = [seq_bf, seq_bf] + [jax.ShapeDtypeStruct((1, S5_COLS), F32)] * 2
        out_specs = [re_spec, re_spec, a_spec, a_spec]
    else:
        out_shape = [jax.ShapeDtypeStruct((t, S5_COLS), F32)] * 2 + [seq_bf, seq_bf]
        out_specs = [re_spec] * 4
    return pl.pallas_call(
        body, grid=(ncb,), in_specs=in_specs, out_specs=out_specs, out_shape=out_shape,
        scratch_shapes=[pltpu.VMEM((t + pad, cb), F32), pltpu.VMEM((t + pad, cb), F32)],
        compiler_params=pltpu.CompilerParams(dimension_semantics=("arbitrary",), vmem_limit_bytes=VMEM_LIMIT),
        name="s5_scan_bwd" if reverse else "s5_scan_fwd")(*ins)


CONF_DIM = 512
CONF_K = 31
CONF_PAD = 32


def _conf_conv(proj_o, w, b, dc1=None):
    t = proj_o.shape[0]
    cb = LANES
    ncb = CONF_DIM // cb
    chunk = min(512, t)
    chunks = range(0, t, chunk)
    bwd = dc1 is not None

    def body(*refs):
        if bwd:
            ca_ref, cg_ref, w_ref, b_ref, dy_ref, dca_ref, dcg_ref, dw_ref, db_ref, xs, dys = refs
        else:
            ca_ref, cg_ref, w_ref, b_ref, o_ref, xs = refs
        xs[pl.ds(0, CONF_PAD), :] = jnp.zeros((CONF_PAD, cb), F32)
        for t0 in chunks:
            rows = pl.ds(t0, chunk)
            xs[pl.ds(CONF_PAD + t0, chunk), :] = ca_ref[rows, :] * jax.nn.sigmoid(cg_ref[rows, :])
        if not bwd:
            for t0 in chunks:
                acc = jnp.broadcast_to(b_ref[...], (chunk, cb))
                for k in range(CONF_K):
                    acc = acc + w_ref[k] * xs[pl.ds(CONF_PAD + t0 - (CONF_K - 1 - k), chunk), :]
                o_ref[pl.ds(t0, chunk), :] = acc
            return
        dys[pl.ds(t, CONF_PAD), :] = jnp.zeros((CONF_PAD, cb), F32)
        db = jnp.zeros((1, cb), F32)
        for t0 in chunks:
            dys[pl.ds(t0, chunk), :] = dy_ref[pl.ds(t0, chunk), :]
            db = db + jnp.sum(dy_ref[pl.ds(t0, chunk), :], axis=0, keepdims=True)
        db_ref[...] = db
        for t0 in chunks:
            rows = pl.ds(t0, chunk)
            acc = jnp.zeros((chunk, cb), F32)
            for k in range(CONF_K):
                acc = acc + w_ref[k] * dys[pl.ds(t0 + (CONF_K - 1 - k), chunk), :]
            sig = jax.nn.sigmoid(cg_ref[rows, :])
            dca_ref[rows, :] = (acc * sig).astype(BF16)
            dcg_ref[rows, :] = (acc * ca_ref[rows, :] * sig * (1.0 - sig)).astype(BF16)
        for k in range(CONF_K):
            dwk = jnp.zeros((1, cb), F32)
            for t0 in chunks:
                window = xs[pl.ds(CONF_PAD + t0 - (CONF_K - 1 - k), chunk), :]
                dwk = dwk + jnp.sum(dy_ref[pl.ds(t0, chunk), :] * window, axis=0, keepdims=True)
            dw_ref[k] = dwk

    col = lambda off: pl.BlockSpec((t, cb), lambda j, off=off: (0, j + off))
    w_spec = pl.BlockSpec((CONF_K, 1, cb), lambda j: (0, 0, j))
    b_spec = pl.BlockSpec((1, cb), lambda j: (0, j))
    seq = jax.ShapeDtypeStruct((t, CONF_DIM), F32)
    ins, in_specs = [proj_o, proj_o, w, b], [col(0), col(ncb), w_spec, b_spec]
    scratch = [pltpu.VMEM((CONF_PAD + t, cb), F32)]
    if bwd:
        ins, in_specs = ins + [dc1], in_specs + [col(0)]
        seq_bf = jax.ShapeDtypeStruct((t, CONF_DIM), BF16)
        out_shape, out_specs = [seq_bf, seq_bf, jax.ShapeDtypeStruct(w.shape, F32), jax.ShapeDtypeStruct(b.shape, F32)], [col(0), col(0), w_spec, b_spec]
        scratch = scratch + [pltpu.VMEM((t + CONF_PAD, cb), F32)]
    else:
        out_shape, out_specs = seq, col(0)
    return pl.pallas_call(
        body, grid=(ncb,), in_specs=in_specs, out_specs=out_specs, out_shape=out_shape, scratch_shapes=scratch,
        compiler_params=pltpu.CompilerParams(dimension_semantics=("arbitrary",), vmem_limit_bytes=VMEM_LIMIT),
        name="conf_conv_bwd" if bwd else "conf_conv")(*ins)


N_CHIPS = 4
_HBM = pl.BlockSpec(memory_space=pl.ANY)
_MESH_ID = pl.DeviceIdType.MESH


def _comm_call(body, srcs, out_shapes, n_sems, name):
    n = len(srcs)
    return pl.pallas_call(
        body, out_shape=out_shapes, in_specs=[_HBM] * n, out_specs=[_HBM] * n,
        scratch_shapes=[pltpu.SemaphoreType.DMA((n_sems,)), pltpu.SemaphoreType.DMA((n_sems,)), pltpu.SemaphoreType.DMA((n,))],
        compiler_params=pltpu.CompilerParams(has_side_effects=True), name=name)(*srcs)


def _gather(srcs, name):
    n = len(srcs)
    per = N_DEV - 1

    def body(*refs):
        src_refs, out_refs = refs[:n], refs[n:2 * n]
        send_sems, recv_sems, local_sems = refs[2 * n:]
        x, y, c = lax.axis_index("x"), lax.axis_index("y"), lax.axis_index("c")
        me, sibling = (x, y, c), (x, y, 1 - c)
        chips = [(1 - x, y), (x, 1 - y), (1 - x, 1 - y)]

        def copy(i, k, block, to, from_src=False):
            rows = out_refs[i].at[4 * block[0] + 2 * block[1] + block[2]]
            return pltpu.make_async_remote_copy(
                src_ref=src_refs[i] if from_src else rows, dst_ref=rows, send_sem=send_sems.at[per * i + k],
                recv_sem=recv_sems.at[per * i + k], device_id=to, device_id_type=_MESH_ID)

        local = [pltpu.make_async_copy(src_refs[i], out_refs[i].at[4 * x + 2 * y + c], local_sems.at[i]) for i in range(n)]
        for cp in local:
            cp.start()
        sends = []
        for i in range(n):
            sends.append(copy(i, 0, me, sibling, True))
            sends += [copy(i, 1 + j, me, (*chip, c), True) for j, chip in enumerate(chips)]
        for cp in sends:
            cp.start()
        for j, chip in enumerate(chips):
            for i in range(n):
                copy(i, 1 + j, (*chip, c), me).wait_recv()
                passed = copy(i, 4 + j, (*chip, c), sibling)
                passed.start()
                sends.append(passed)
        for i in range(n):
            copy(i, 0, sibling, me).wait_recv()
            for j, chip in enumerate(chips):
                copy(i, 4 + j, (*chip, 1 - c), me).wait_recv()
        for cp in sends:
            cp.wait_send()
        for cp in local:
            cp.wait()

    return _comm_call(body, srcs, [jax.ShapeDtypeStruct((N_DEV,) + s.shape, s.dtype) for s in srcs], per * n, name)


def _swap_sibling(srcs, name):
    n = len(srcs)

    def body(*refs):
        src_refs, out_refs = refs[:n], refs[n:2 * n]
        send_sems, recv_sems, _ = refs[2 * n:]
        sibling = (lax.axis_index("x"), lax.axis_index("y"), 1 - lax.axis_index("c"))
        copies = [pltpu.make_async_remote_copy(src_ref=src_refs[i], dst_ref=out_refs[i], send_sem=send_sems.at[i],
                                               recv_sem=recv_sems.at[i], device_id=sibling, device_id_type=_MESH_ID) for i in range(n)]
        for cp in copies:
            cp.start()
        for cp in copies:
            cp.wait_recv()
        for cp in copies:
            cp.wait_send()

    return _comm_call(body, srcs, [jax.ShapeDtypeStruct(s.shape, s.dtype) for s in srcs], n, name)


def _exchange_chips(srcs, name):
    n = len(srcs)
    per = N_CHIPS - 1

    def body(*refs):
        src_refs, out_refs = refs[:n], refs[n:2 * n]
        send_sems, recv_sems, local_sems = refs[2 * n:]
        x, y, c = lax.axis_index("x"), lax.axis_index("y"), lax.axis_index("c")
        mine = 2 * x + y
        local = [pltpu.make_async_copy(src_refs[i].at[mine], out_refs[i].at[mine], local_sems.at[i]) for i in range(n)]
        for cp in local:
            cp.start()
        copies = []
        for k in range(1, N_CHIPS):
            px, py = x ^ (k >> 1), y ^ (k & 1)
            peer = 2 * px + py
            for i in range(n):
                sem = per * i + k - 1
                send = pltpu.make_async_remote_copy(
                    src_ref=src_refs[i].at[peer], dst_ref=out_refs[i].at[mine], send_sem=send_sems.at[sem], recv_sem=recv_sems.at[sem],
                    device_id=(px, py, c), device_id_type=_MESH_ID)
                send.start()
                recv = pltpu.make_async_remote_copy(
                    src_ref=src_refs[i].at[peer], dst_ref=out_refs[i].at[peer], send_sem=send_sems.at[sem], recv_sem=recv_sems.at[sem],
                    device_id=(px, py, c), device_id_type=_MESH_ID)
                copies.append((send, recv))
        for _, recv in copies:
            recv.wait_recv()
        for send, _ in copies:
            send.wait_send()
        for cp in local:
            cp.wait()

    return _comm_call(body, srcs, [jax.ShapeDtypeStruct(s.shape, s.dtype) for s in srcs], per * n, name)


def _add(a, b, name):
    k, rows, cols = a.shape
    tr = _row_tile(rows)

    def body(a_ref, b_ref, o_ref):
        o_ref[...] = (a_ref[...].astype(F32) + b_ref[...].astype(F32)).astype(o_ref.dtype)

    spec = pl.BlockSpec((k, tr, cols), lambda i: (0, i, 0))
    return pl.pallas_call(
        body, grid=(rows // tr,), in_specs=[spec, spec], out_specs=spec, out_shape=jax.ShapeDtypeStruct(a.shape, a.dtype),
        compiler_params=pltpu.CompilerParams(dimension_semantics=("parallel",), vmem_limit_bytes=VMEM_LIMIT), name=name)(a, b)


def _row_tile(r, pref=256):
    if r <= pref:
        return r
    t = pref // 16 * 16
    while r % t:
        t -= 16
    return t


def _join_cols(g, width, name):
    _, rows, ws = g.shape
    tr = _row_tile(rows)
    tail = width - N_DEV * ws

    def body(g_ref, o_ref):
        for d in range(N_DEV):
            o_ref[:, pl.ds(d * ws, ws)] = g_ref[d]
        if tail:
            o_ref[:, pl.ds(N_DEV * ws, tail)] = jnp.zeros((tr, tail), g.dtype)

    return pl.pallas_call(
        body, grid=(rows // tr,), in_specs=[pl.BlockSpec((N_DEV, tr, ws), lambda i: (0, i, 0))],
        out_specs=pl.BlockSpec((tr, width), lambda i: (i, 0)), out_shape=jax.ShapeDtypeStruct((rows, width), g.dtype),
        compiler_params=pltpu.CompilerParams(dimension_semantics=("parallel",), vmem_limit_bytes=VMEM_LIMIT), name=name)(g)


def _split_cols(full, ws, dtype, name):
    rows, width = full.shape
    tr = _row_tile(rows)

    def body(x_ref, o_ref):
        for d in range(N_DEV):
            o_ref[d] = x_ref[:, pl.ds(d * ws, ws)].astype(dtype)

    return pl.pallas_call(
        body, grid=(rows // tr,), in_specs=[pl.BlockSpec((tr, width), lambda i: (i, 0))],
        out_specs=pl.BlockSpec((N_DEV, tr, ws), lambda i: (0, i, 0)), out_shape=jax.ShapeDtypeStruct((N_DEV, rows, ws), dtype),
        compiler_params=pltpu.CompilerParams(dimension_semantics=("parallel",), vmem_limit_bytes=VMEM_LIMIT), name=name)(full)


def _adamw(parts, w, m, v, name):
    r, c = w.shape
    n_parts = parts.shape[0]
    tr = _row_tile(r)

    def body(p_ref, w_ref, m_ref, v_ref, g_ref, d_ref, nm_ref, nv_ref):
        g = p_ref[0].astype(F32)
        for s in range(1, n_parts):
            g = g + p_ref[s].astype(F32)
        nm = ADAM_B1 * m_ref[...] + (1.0 - ADAM_B1) * g
        nv = ADAM_B2 * v_ref[...] + (1.0 - ADAM_B2) * (g * g)
        m_hat = nm / (1.0 - ADAM_B1 ** ADAM_STEP)
        v_hat = nv / (1.0 - ADAM_B2 ** ADAM_STEP)
        g_ref[...] = g
        nm_ref[...] = nm
        nv_ref[...] = nv
        d_ref[...] = -ADAM_LR * (m_hat / (jnp.sqrt(v_hat) + ADAM_EPS) + ADAM_WD * w_ref[...])

    spec = pl.BlockSpec((tr, c), lambda i: (i, 0))
    return pl.pallas_call(
        body, grid=(r // tr,), in_specs=[pl.BlockSpec((n_parts, tr, c), lambda i: (0, i, 0)), spec, spec, spec],
        out_specs=[spec] * 4, out_shape=[jax.ShapeDtypeStruct((r, c), F32)] * 4,
        compiler_params=pltpu.CompilerParams(dimension_semantics=("parallel",), vmem_limit_bytes=VMEM_LIMIT),
        name=name)(parts, w, m, v)


def _pack_rows(n_elems, mult):
    rows = -(-n_elems // PACK_COLS)
    return -(-rows // mult) * mult


def _pack(arrays, dtype, mult, lead=()):
    flat = jnp.concatenate([a.astype(dtype).reshape(lead + (-1,)) for a in arrays], axis=-1)
    rows = _pack_rows(flat.shape[-1], mult)
    flat = jnp.pad(flat, [(0, 0)] * len(lead) + [(0, rows * PACK_COLS - flat.shape[-1])])
    return flat.reshape(lead + (rows, PACK_COLS))


def _unpack(buf, shapes, lead=()):
    flat = buf.reshape(lead + (-1,))
    out, off = [], 0
    for s in shapes:
        n = math.prod(s)
        out.append(flat[..., off:off + n].reshape(lead + tuple(s)))
        off += n
    return out


def _join_shards(piece, axis):
    moved = jnp.moveaxis(piece, 0, axis)
    shape = moved.shape
    return moved.reshape(shape[:axis] + (shape[axis] * shape[axis + 1],) + shape[axis + 2:])


def _split_shards(full, axis):
    shape = full.shape
    return jnp.moveaxis(full.reshape(shape[:axis] + (N_DEV, shape[axis] // N_DEV) + shape[axis + 1:]), axis, 0)


def _block_diag(blocks):
    g, r, c = blocks.shape
    eye = jnp.eye(g, dtype=blocks.dtype)
    return (blocks[:, :, None, :] * eye[:, None, :, None]).reshape(g * r, g * c)


def _diag_blocks(mat, g):
    r, c = mat.shape[0] // g, mat.shape[1] // g
    eye = jnp.eye(g, dtype=mat.dtype)
    return jnp.sum(mat.reshape(g, r, g, c) * eye[:, None, :, None], axis=2)


def _pad_lanes(a):
    a = a.reshape(1, -1)
    return jnp.pad(a, ((0, 0), (0, LANES - a.shape[1])))


def _head_expand():
    h = jnp.arange(LANES)[:, None]
    ch = jnp.arange(SSM_HEADS * SSM_P)[None, :] // SSM_P
    return (h == ch).astype(F32)


def _rotary_tables(t):
    inv = 10000.0 ** (-jnp.arange(0, RET_DK, 2, dtype=F32) / RET_DK)
    ang = jnp.arange(t).astype(F32)[:, None] * inv[None, :]
    cos, sin = jnp.cos(ang), jnp.sin(ang)
    return jnp.concatenate([cos, cos], axis=1), jnp.concatenate([-sin, sin], axis=1)


def _rms_fwd(x, g, name):
    return _rows(_f_rms, x.shape[0], [(x, D_MODEL, 0)], [g], [D_MODEL], [], name, bf16_outs=(0,))[0]


def _rms_bwd(x, g, dh, dres, name):
    def fn(x_, dh_, dres_, g_):
        _, vjp = jax.vjp(lambda a, b: _rms(a, b), x_, g_)
        dx, dg = vjp(dh_)
        return dx + dres_, dx + dres_, dg
    return _rows(fn, x.shape[0], [(x, D_MODEL, 0), (dh, D_MODEL, 0), (dres, D_MODEL, 0)], [g], [D_MODEL, D_MODEL], [(1, D_MODEL)],
                 name, bf16_outs=(1,))


def _ffn_fwd(x, norm_g, w_up, dw_w, dw_b, w_down, tag):
    t = x.shape[0]
    nbk = D_FF // LANES
    h = _rms_fwd(x, norm_g, "ffn_norm_" + tag)
    up = _mm(h, w_up, "nn", "ffn_up_" + tag)
    mid = _cols(_f_ffnmid, nbk, [(up, 0), (up, nbk), (dw_w, 0), (dw_w, nbk), (dw_b, 0), (dw_b, nbk)], [(t,)], "ffn_mid_" + tag,
                bf16_outs=(0,))[0]
    out = _mm(mid, w_down, "nn", "ffn_down_" + tag, res=x)
    return out, (h, up, mid)


def _ffn_bwd(x, norm_g, w_up, dw_w, dw_b, w_down, saved, dout, dout_bf, tag):
    t = x.shape[0]
    nbk = D_FF // LANES
    h, up, mid = saved
    d_w_down = _mm(mid, dout_bf, "tn", "ffn_down_dw_" + tag, out_dtype=BF16)
    dmid = _mm(dout_bf, w_down, "nt", "ffn_down_dx_" + tag)
    dgin, duin, dwg, dwu, dbg, dbu = _cols(
        _grad_fn(_f_ffnmid, 6, 6), nbk,
        [(up, 0), (up, nbk), (dw_w, 0), (dw_w, nbk), (dw_b, 0), (dw_b, nbk), (dmid, 0)],
        [(t,), (t,), (3, 1), (3, 1), (1,), (1,)], "ffn_mid_bwd_" + tag, bf16_outs=(0, 1))
    dup = jnp.concatenate([dgin, duin], axis=1)
    d_w_up = _mm(h, dup, "tn", "ffn_up_dw_" + tag)
    dh = _mm(dup, w_up, "nt", "ffn_up_dx_" + tag)
    dx, dx_bf, dnorm = _rms_bwd(x, norm_g, dh, dout, "ffn_norm_bwd_" + tag)
    return dx, dx_bf, dict(norm=dnorm, w_up=d_w_up, dw_w=jnp.concatenate([dwg, dwu], axis=2)[:, 0],
                           dw_b=jnp.concatenate([dbg, dbu], axis=1), w_down=d_w_down)


def _local_step(x, tgt, w):
    t = x.shape[0]
    grads = {}
    expand = _head_expand()
    cosf, sins = _rotary_tables(t)
    mix_g = [w['mix_norm'][i:i + 1] for i in range(2)]
    ffn_g = [w['ffn_norm'][i:i + 1] for i in range(2)]
    ffn_dw_w = [w['ffn_dw_w'][i][:, None, :] for i in range(2)]
    ffn_dw_b = [w['ffn_dw_b'][i:i + 1] for i in range(2)]

    w_e = w['e_w_in']
    conv_w = w['e_conv_w'][0][:, None, :]
    conv_b = w['e_conv_b']
    dt_bias, a_log, d_skip = _pad_lanes(w['e_dt_bias']), _pad_lanes(w['e_a_log']), _pad_lanes(w['e_d'])
    h0 = _rms_fwd(x, mix_g[0], "mix_norm_0")
    proj = _mm(h0, w_e, "nn", "e_in")
    qr, kr = _rows(_f_retpre, t, [(proj, 512, 0), (proj, 512, 1), (cosf, LANES, 0), (sins, LANES, 0)], [], [512, 512], [], "ret_pre")
    r = _ret_fwd(qr, kr, proj, t)
    y_ret = _rows(_f_retpost, t, [(r, 1024, 0), (proj, 1024, 2)], [], [1024], [], "ret_post", bf16_outs=(0,))[0]
    xbc_act = _cols(_f_ssdconv, 12, [(proj, 32), (conv_w, 0), (conv_b, 0)], [(t,)], "ssd_conv")[0]
    xdt, da = _rows(_f_ssdpre, t, [(xbc_act, 1024, 0), (proj, LANES, 44)], [dt_bias, a_log, expand], [1024, LANES], [], "ssd_pre")
    a_cum = _cumsum(da, False, "ssd_cumsum")
    a_cum_t = a_cum[:, :SSM_HEADS].T
    yc = _ssd_fwd(xbc_act, xdt, a_cum, a_cum_t, t)
    y_ssm = _rows(_f_ssdpost, t, [(yc, 1024, 0), (xbc_act, 1024, 0), (proj, 1024, 3)], [d_skip, w['e_ssm_norm'], expand],
                  [1024], [], "ssd_post", bf16_outs=(0,))[0]
    mix_e = jnp.concatenate([y_ret, y_ssm], axis=1)
    x1 = _mm(mix_e, w['e_w_out'], "nn", "e_out", res=x)
    x2, ffn0 = _ffn_fwd(x1, ffn_g[0], w['ffn_w_up'][0], ffn_dw_w[0], ffn_dw_b[0], w['ffn_w_down'][0], "0")

    lr, li = w['o_a_re'][0], w['o_a_im'][0]
    ls = w['o_log_step'].reshape(S5_GROUPS, 1)
    b_re3, b_im3 = jnp.transpose(w['o_b_re'][0], (2, 0, 1)), jnp.transpose(w['o_b_im'][0], (2, 0, 1))
    par_ins = [lr, li, ls, b_re3, b_im3]
    whole = lambda a: (a, a.shape, (lambda i, n=a.ndim: (0,) * n))
    par_shapes = [(S5_GROUPS, S5_STATE)] * 2 + [(S5_GROUP, S5_GROUPS, S5_STATE)] * 2
    ab_re, ab_im, bb_re, bb_im = _call(_f_s5par, (1,), [whole(a) for a in par_ins],
                                       [(s, F32, s, (lambda i, n=len(s): (0,) * n), False) for s in par_shapes], "s5_params")
    w_b = jnp.concatenate([_block_diag(jnp.transpose(bb_re, (1, 0, 2))), _block_diag(jnp.transpose(bb_im, (1, 0, 2)))], axis=1)
    w_c = jnp.concatenate([_block_diag(jnp.transpose(w['o_c_re'][0], (0, 2, 1))),
                           -_block_diag(jnp.transpose(w['o_c_im'][0], (0, 2, 1)))], axis=0)
    a_re, a_im = ab_re.reshape(1, S5_COLS), ab_im.reshape(1, S5_COLS)
    dw_w = w['o_dw_w'][0][:, None, :]
    glu_w = w['o_glu_w'].astype(F32)

    h1 = _rms_fwd(x2, mix_g[1], "mix_norm_1")
    proj_o = _mm(h1, w['o_w_in'], "nn", "o_in")
    c1 = _conf_conv(proj_o, dw_w, w['o_dw_b'])
    c2 = _rows(_f_confb, t, [(c1, 512, 0)], [w['o_ln_g'], w['o_ln_b']], [512], [], "conf_norm", bf16_outs=(0,))[0]
    u = proj_o[:, 1024:].astype(BF16)
    bu = _mm(u, w_b, "nn", "s5_bu")
    xs_re, xs_im, xs_re_bf, xs_im_bf = _s5_scan(bu, a_re, a_im, False)
    xs_cat = jnp.concatenate([xs_re, xs_im], axis=1)
    xs_cat_bf = jnp.concatenate([xs_re_bf, xs_im_bf], axis=1)
    y_s5 = _mm(xs_cat_bf, w_c, "nn", "s5_cx")
    s_out = _rows(_f_s5post, t, [(y_s5, 512, 0), (proj_o, 512, 2)], [w['o_d'], glu_w], [512], [], "s5_post", bf16_outs=(0,))[0]
    mix_o = jnp.concatenate([c2, s_out], axis=1)
    x3 = _mm(mix_o, w['o_w_out'], "nn", "o_out", res=x2)
    x4, ffn1 = _ffn_fwd(x3, ffn_g[1], w['ffn_w_up'][1], ffn_dw_w[1], ffn_dw_b[1], w['ffn_w_down'][1], "1")

    dx4, dx4_bf, loss_blk, d_final = _rows(_loss_step, t, [(x4, D_MODEL, 0), (tgt, D_MODEL, 0)], [w['final_norm'].reshape(1, D_MODEL)],
                                           [D_MODEL, D_MODEL], [(8, LANES), (1, D_MODEL)], "loss_head", bf16_outs=(1,))
    loss = loss_blk[0, 0]
    grads['final_norm'] = d_final.reshape(D_MODEL)

    dx3, dx3_bf, g1 = _ffn_bwd(x3, ffn_g[1], w['ffn_w_up'][1], ffn_dw_w[1], ffn_dw_b[1], w['ffn_w_down'][1], ffn1, dx4, dx4_bf, "1")
    grads['o_w_out'] = _mm(mix_o, dx3_bf, "tn", "o_out_dw", out_dtype=BF16)
    dmix_o = _mm(dx3_bf, w['o_w_out'], "nt", "o_out_dx")
    dc1, d_ln_g, d_ln_b = _conf_norm_bwd(c1, w['o_ln_g'], w['o_ln_b'], dmix_o, t)
    dca, dcg, d_dw_w, d_dw_b = _conf_conv(proj_o, dw_w, w['o_dw_b'], dc1)
    dyc, du_skip, d_od, d_glu = _s5_post_bwd(y_s5, proj_o, w['o_d'], glu_w, dmix_o, t)
    grads['o_glu_w'] = d_glu
    d_w_c = _mm(xs_cat_bf, dyc, "tn", "s5_cx_dw")
    dxs = _mm(dyc, w_c, "nt", "s5_cx_dx")
    g_re, g_im, d_are, d_aim = _s5_scan(dxs, a_re, a_im, True, xs_cat)
    g_cat = jnp.concatenate([g_re, g_im], axis=1)
    d_w_b = _mm(u, g_cat, "tn", "s5_bu_dw")
    du = _mm(g_cat, w_b, "nt", "s5_bu_dx", res=du_skip, out_dtype=BF16)
    d_bb_re = jnp.transpose(_diag_blocks(d_w_b[:, :S5_COLS], S5_GROUPS), (1, 0, 2))
    d_bb_im = jnp.transpose(_diag_blocks(d_w_b[:, S5_COLS:], S5_GROUPS), (1, 0, 2))
    par_cts = [d_are.reshape(S5_GROUPS, S5_STATE), d_aim.reshape(S5_GROUPS, S5_STATE), d_bb_re, d_bb_im]
    in_shapes = [a.shape for a in par_ins]
    d_lr, d_li, d_ls, d_br3, d_bi3 = _call(_grad_fn(_f_s5par, 5, 5), (1,), [whole(a) for a in par_ins + par_cts],
                                           [(s, F32, s, (lambda i, n=len(s): (0,) * n), False) for s in in_shapes], "s5_params_bwd")
    grads['o_a_re'], grads['o_a_im'], grads['o_log_step'] = d_lr[None], d_li[None], d_ls.reshape(1, S5_GROUPS)
    grads['o_b_re'], grads['o_b_im'] = jnp.transpose(d_br3, (1, 2, 0))[None], jnp.transpose(d_bi3, (1, 2, 0))[None]
    grads['o_c_re'] = jnp.transpose(_diag_blocks(d_w_c[:S5_COLS], S5_GROUPS), (0, 2, 1))[None]
    grads['o_c_im'] = -jnp.transpose(_diag_blocks(d_w_c[S5_COLS:], S5_GROUPS), (0, 2, 1))[None]
    grads['o_d'], grads['o_ln_g'], grads['o_ln_b'] = d_od, d_ln_g, d_ln_b
    grads['o_dw_w'], grads['o_dw_b'] = d_dw_w[:, 0][None], d_dw_b
    dproj_o = jnp.concatenate([dca, dcg, du], axis=1)
    grads['o_w_in'] = _mm(h1, dproj_o, "tn", "o_in_dw")
    dh1 = _mm(dproj_o, w['o_w_in'], "nt", "o_in_dx")
    dx2, dx2_bf, d_mix1 = _rms_bwd(x2, mix_g[1], dh1, dx3, "mix_norm_bwd_1")

    dx1, dx1_bf, g0 = _ffn_bwd(x1, ffn_g[0], w['ffn_w_up'][0], ffn_dw_w[0], ffn_dw_b[0], w['ffn_w_down'][0], ffn0, dx2, dx2_bf, "0")
    grads['e_w_out'] = _mm(mix_e, dx1_bf, "tn", "e_out_dw", out_dtype=BF16)
    dmix_e = _mm(dx1_bf, w['e_w_out'], "nt", "e_out_dx")
    dr, dg = _rows(_grad_fn(_f_retpost, 2, 2), t, [(r, 1024, 0), (proj, 1024, 2), (dmix_e, 1024, 0)], [], [1024, 1024], [], "ret_post_bwd",
                   bf16_outs=(0, 1))
    dqr, dkr, dv = _ret_bwd(qr, kr, proj, dr, t)
    dq, dk = _ret_pre_bwd(proj, cosf, sins, dqr, dkr, t)
    dyc0, dxs1, dz, d_dskip, d_ssm_norm = _ssd_post_bwd(yc, xbc_act, proj, d_skip, w['e_ssm_norm'], expand, dmix_e, t)
    dcm, da_q, dbm, dxdt, da_k, da_k_t = _ssd_bwd(xbc_act, xdt, a_cum, a_cum_t, dyc0, expand, t)
    d_a_cum = da_q - da_k - jnp.pad(da_k_t.T, ((0, 0), (0, LANES - SSM_HEADS)))
    dda = _cumsum(d_a_cum, True, "ssd_cumsum_bwd")
    dxs, ddtr, d_dt_bias, d_a_log = _ssd_pre_bwd(xbc_act, proj, dt_bias, a_log, expand, dxdt, dda, dxs1, t)
    dxbc_act = jnp.concatenate([dxs, dbm, dcm], axis=1)
    dxbc, d_conv_w, d_conv_b = _cols(_grad_fn(_f_ssdconv, 3, 3), 12, [(proj, 32), (conv_w, 0), (conv_b, 0), (dxbc_act, 0)],
                                     [(t,), (4, 1), (1,)], "ssd_conv_bwd", bf16_outs=(0,))
    dproj = jnp.concatenate([dq, dk, dv, dg, dz, dxbc, ddtr], axis=1)
    grads['e_w_in'] = _mm(h0, dproj, "tn", "e_in_dw")
    dh0 = _mm(dproj, w_e, "nt", "e_in_dx")
    dx0, _, d_mix0 = _rms_bwd(x, mix_g[0], dh0, dx1, "mix_norm_bwd_0")

    grads['mix_norm'] = jnp.concatenate([d_mix0, d_mix1], axis=0)
    grads['e_conv_w'], grads['e_conv_b'] = d_conv_w[:, 0][None], d_conv_b
    grads['e_dt_bias'], grads['e_a_log'], grads['e_d'] = d_dt_bias[:, :SSM_HEADS], d_a_log[:, :SSM_HEADS], d_dskip[:, :SSM_HEADS]
    grads['e_ssm_norm'] = d_ssm_norm
    grads['ffn_norm'] = jnp.concatenate([g0['norm'], g1['norm']], axis=0)
    grads['ffn_w_up'], grads['ffn_w_down'] = [g0['w_up'], g1['w_up']], [g0['w_down'], g1['w_down']]
    grads['ffn_dw_w'] = jnp.stack([g0['dw_w'], g1['dw_w']], axis=0)
    grads['ffn_dw_b'] = jnp.concatenate([g0['dw_b'], g1['dw_b']], axis=0)
    return loss, dx0, grads


def _conf_norm_bwd(c1, ln_g, ln_b, dmix_o, t):
    def fn(c1_, dy_, g_, b_):
        _, vjp = jax.vjp(lambda a, b, c: _f_confb(a, b, c)[0], c1_, g_, b_)
        return vjp(dy_)
    return _rows(fn, t, [(c1, 512, 0), (dmix_o, 512, 0)], [ln_g, ln_b], [512], [(1, 512), (1, 512)], "conf_norm_bwd")


def _s5_post_bwd(y_s5, proj_o, d_skip, glu_w, dmix_o, t):
    def fn(yc_, u_, dy_, d_, gw_):
        _, vjp = jax.vjp(lambda a, b, c, e: _f_s5post(a, b, c, e)[0], yc_, u_, d_, gw_)
        return vjp(dy_)
    return _rows(fn, t, [(y_s5, 512, 0), (proj_o, 512, 2), (dmix_o, 512, 1)], [d_skip, glu_w], [512, 512], [(1, 512), (512, 512)],
                 "s5_post_bwd", bf16_outs=(0,))


def _ret_pre_bwd(proj, cosf, sins, dqr, dkr, t):
    def fn(q_, k_, cos_, sin_, dq_, dk_):
        _, vjp = jax.vjp(lambda a, b: _f_retpre(a, b, cos_, sin_), q_, k_)
        return vjp((dq_, dk_))
    return _rows(fn, t, [(proj, 512, 0), (proj, 512, 1), (cosf, LANES, 0), (sins, LANES, 0), (dqr, 512, 0), (dkr, 512, 0)], [],
                 [512, 512], [], "ret_pre_bwd", bf16_outs=(0, 1))


def _ssd_post_bwd(yc, xbc_act, proj, d_skip, norm_w, expand, dmix_e, t):
    def fn(yc_, xs_, z_, dy_, d_, nw_, e_):
        _, vjp = jax.vjp(lambda a, b, c, dd, n: _f_ssdpost(a, b, c, dd, n, e_)[0], yc_, xs_, z_, d_, nw_)
        return vjp(dy_)
    return _rows(fn, t, [(yc, 1024, 0), (xbc_act, 1024, 0), (proj, 1024, 3), (dmix_e, 1024, 1)], [d_skip, norm_w, expand],
                 [1024, 1024, 1024], [(1, LANES), (1, 1024)], "ssd_post_bwd", bf16_outs=(0, 2))


def _ssd_pre_bwd(xbc_act, proj, dt_bias, a_log, expand, dxdt, dda, dxs1, t):
    def fn(xs_, dtr_, dx_, dda_, dxs1_, bias_, alog_, e_):
        _, vjp = jax.vjp(lambda a, b, c, dd: _f_ssdpre(a, b, c, dd, e_), xs_, dtr_, bias_, alog_)
        dxs, ddtr, dbias, dalog = vjp((dx_, dda_))
        return dxs + dxs1_, ddtr, dbias, dalog
    return _rows(fn, t, [(xbc_act, 1024, 0), (proj, LANES, 44), (dxdt, 1024, 0), (dda, LANES, 0), (dxs1, 1024, 0)],
                 [dt_bias, a_log, expand], [1024, LANES], [(1, LANES), (1, LANES)], "ssd_pre_bwd", bf16_outs=(1,))


def kernel(x, mix_norm, e_w_in, e_conv_w, e_conv_b, e_dt_bias, e_a_log, e_d, e_ssm_norm, e_w_out, o_w_in, o_dw_w, o_dw_b, o_ln_g, o_ln_b, o_a_re, o_a_im, o_b_re, o_b_im, o_c_re, o_c_im, o_d, o_log_step, o_glu_w, o_w_out, ffn_norm, ffn_w_up, ffn_dw_w, ffn_dw_b, ffn_w_down, final_norm, loss_target, m_mix_norm, m_e_w_in, m_e_conv_w, m_e_conv_b, m_e_dt_bias, m_e_a_log, m_e_d, m_e_ssm_norm, m_e_w_out, m_o_w_in, m_o_dw_w, m_o_dw_b, m_o_ln_g, m_o_ln_b, m_o_a_re, m_o_a_im, m_o_b_re, m_o_b_im, m_o_c_re, m_o_c_im, m_o_d, m_o_log_step, m_o_glu_w, m_o_w_out, m_ffn_norm, m_ffn_w_up, m_ffn_dw_w, m_ffn_dw_b, m_ffn_w_down, m_final_norm, v_mix_norm, v_e_w_in, v_e_conv_w, v_e_conv_b, v_e_dt_bias, v_e_a_log, v_e_d, v_e_ssm_norm, v_e_w_out, v_o_w_in, v_o_dw_w, v_o_dw_b, v_o_ln_g, v_o_ln_b, v_o_a_re, v_o_a_im, v_o_b_re, v_o_b_im, v_o_c_re, v_o_c_im, v_o_d, v_o_log_step, v_o_glu_w, v_o_w_out, v_ffn_norm, v_ffn_w_up, v_ffn_dw_w, v_ffn_dw_b, v_ffn_w_down, v_final_norm):
    p = dict(locals())

    kinds = ("grad_", "delta_", "new_m_", "new_v_")

    def block(name, layer):
        return p[name][0 if layer is None else layer]

    srcs = [block(n, layer).astype(BF16) for n, layer, _ in MATMUL_BLOCKS] + [_pack([p[n] for n in SMALL_SHARDED], F32, 16)]
    gathered = _gather(srcs, "gather_weights")
    w = {n: p[n] for n in REPLICATED}
    for (n, layer, by_cols), g in zip(MATMUL_BLOCKS, gathered):
        if by_cols:
            full = _join_cols(g, E_IN_PAD if n == 'e_w_in' else N_DEV * g.shape[2], f"join_{n}_{layer}")
        else:
            full = g.reshape(N_DEV * g.shape[1], g.shape[2])
        if layer is None:
            w[n] = full
        else:
            w.setdefault(n, [None, None])[layer] = full
    for n, piece in zip(SMALL_SHARDED, _unpack(gathered[-1], [p[n].shape for n in SMALL_SHARDED], lead=(N_DEV,))):
        w[n] = _join_shards(piece, SHARDED[n])

    loss, dx, grads = _local_step(x[0], loss_target[0], w)
    loss = lax.psum(loss, MESH_AXES)

    sends = []
    for n, layer, by_cols in MATMUL_BLOCKS:
        g = grads[n] if layer is None else grads[n][layer]
        if by_cols:
            sends.append(_split_cols(g, block(n, layer).shape[1], BF16, f"split_{n}_{layer}"))
        else:
            sends.append(g.astype(BF16).reshape(N_DEV, -1, g.shape[1]))
    sends.append(_pack([_split_shards(grads[n].reshape(p[n].shape[:SHARDED[n]] + (-1,) + p[n].shape[SHARDED[n] + 1:]), SHARDED[n])
                        for n in SMALL_SHARDED], F32, 128, lead=(N_DEV,)))
    core = lax.axis_index("c")
    by_core = [s.reshape((N_CHIPS, 2) + s.shape[1:]) for s in sends]
    keep = [lax.dynamic_index_in_dim(s, core, axis=1, keepdims=False) for s in by_core]
    give = [lax.dynamic_index_in_dim(s, 1 - core, axis=1, keepdims=False) for s in by_core]
    got = _swap_sibling(give, "swap_sibling_grads")
    chip_sums = [_add(a, b, f"chip_sum_{i}") for i, (a, b) in enumerate(zip(keep, got))]
    parts = list(_exchange_chips(chip_sums, "exchange_chip_grads"))
    parts.append(_gather([_pack([grads[n].reshape(p[n].shape) for n in REPLICATED], F32, 128)], "gather_replicated_grads")[0])

    out, by_layer = {}, {}
    for (n, layer, _), part in zip(MATMUL_BLOCKS, parts):
        by_layer.setdefault(n, {})[layer] = _adamw(part, *[block(pre + n, layer) for pre in ("", "m_", "v_")], f"adamw_{n}_{layer}")
    for n, res in by_layer.items():
        for i, kind in enumerate(kinds):
            out[kind + n] = res[None][i][None] if None in res else jnp.stack([res[0][i], res[1][i]], axis=0)
    for names, part, tag in ((SMALL_SHARDED, parts[-2], "small"), (REPLICATED, parts[-1], "replicated")):
        packed = [_pack([p[pre + n] for n in names], F32, 128) for pre in ("", "m_", "v_")]
        for kind, buf in zip(kinds, _adamw(part, *packed, "adamw_" + tag)):
            for n, a in zip(names, _unpack(buf, [p[n].shape for n in names])):
                out[kind + n] = a
    return (loss, dx[None], *[out[kind + n] for kind in kinds for n in WEIGHTS])
```

```python
import functools
import math

import jax
import jax.numpy as jnp
from jax import lax
from jax.experimental import pallas as pl
from jax.experimental.pallas import tpu as pltpu

F32, BF16 = jnp.float32, jnp.bfloat16
HIGHEST = lax.Precision.HIGHEST
N_DEV = 8
MESH_AXES = ("x", "y", "c")
VMEM_LIMIT = 48 * 1024 * 1024
LANES = 128
PACK_COLS = 1024

D_MODEL = 1024
EPS = 1e-6
RET_HEADS, RET_DK, RET_DV = 4, 128, 256
SSM_HEADS, SSM_P, SSM_N, SSM_GROUPS = 16, 64, 128, 2
SSM_HG = SSM_HEADS // SSM_GROUPS
S5_GROUPS, S5_GROUP, S5_STATE = 32, 16, 64
S5_COLS = S5_GROUPS * S5_STATE
D_FF = 2816
E_IN, E_IN_PAD = 5648, 5760
ADAM_LR, ADAM_B1, ADAM_B2, ADAM_EPS, ADAM_WD, ADAM_STEP = 0.001, 0.9, 0.999, 1e-08, 0.01, 10

WEIGHTS = ['mix_norm', 'e_w_in', 'e_conv_w', 'e_conv_b', 'e_dt_bias', 'e_a_log', 'e_d', 'e_ssm_norm', 'e_w_out', 'o_w_in', 'o_dw_w', 'o_dw_b', 'o_ln_g', 'o_ln_b', 'o_a_re', 'o_a_im', 'o_b_re', 'o_b_im', 'o_c_re', 'o_c_im', 'o_d', 'o_log_step', 'o_glu_w', 'o_w_out', 'ffn_norm', 'ffn_w_up', 'ffn_dw_w', 'ffn_dw_b', 'ffn_w_down', 'final_norm']
SHARDED = {'e_w_in': 2, 'e_conv_w': 2, 'e_w_out': 1, 'o_w_in': 2, 'o_dw_w': 2, 'o_dw_b': 1, 'o_ln_g': 1, 'o_ln_b': 1,
           'o_d': 1, 'o_glu_w': 1, 'o_w_out': 1, 'ffn_w_up': 2, 'ffn_dw_w': 2, 'ffn_w_down': 1}
MATMUL_WEIGHTS = ['e_w_in', 'e_w_out', 'o_w_in', 'o_glu_w', 'o_w_out', 'ffn_w_up', 'ffn_w_down']
MATMUL_BLOCKS = [('e_w_in', None, True), ('e_w_out', None, False), ('o_w_in', None, True), ('o_glu_w', None, False),
                 ('o_w_out', None, False), ('ffn_w_up', 0, True), ('ffn_w_up', 1, True), ('ffn_w_down', 0, False), ('ffn_w_down', 1, False)]
SMALL_SHARDED = [n for n in WEIGHTS if n in SHARDED and n not in MATMUL_WEIGHTS]
REPLICATED = [n for n in WEIGHTS if n not in SHARDED]


def _call(fn, grid, ins, outs, name):
    n_in = len(ins)

    def body(*refs):
        vals = fn(*[r[...] for r in refs[:n_in]])
        first = pl.program_id(0) == 0
        for r, v, o in zip(refs[n_in:], vals, outs):
            if o[4]:
                @pl.when(first)
                def _():
                    r[...] = jnp.zeros_like(r)
                r[...] += v.astype(r.dtype)
            else:
                r[...] = v.astype(r.dtype)

    return pl.pallas_call(
        body, grid=grid,
        in_specs=[pl.BlockSpec(b, m) for _, b, m in ins],
        out_specs=[pl.BlockSpec(o[2], o[3]) for o in outs],
        out_shape=[jax.ShapeDtypeStruct(o[0], o[1]) for o in outs],
        compiler_params=pltpu.CompilerParams(dimension_semantics=("arbitrary",) * len(grid), vmem_limit_bytes=VMEM_LIMIT),
        name=name)(*[a for a, _, _ in ins])


def _rows(fn, n_rows, row_ins, full_ins, row_outs, acc_outs, name, tm=256, bf16_outs=()):
    tm = min(tm, n_rows)
    ins = [(a, (tm, w), (lambda i, c=c: (i, c))) for a, w, c in row_ins]
    ins += [(a, a.shape, (lambda i, n=a.ndim: (0,) * n)) for a in full_ins]
    outs = [((n_rows, w), BF16 if k in bf16_outs else F32, (tm, w), (lambda i: (i, 0)), False) for k, w in enumerate(row_outs)]
    outs += [(tuple(s), F32, tuple(s), (lambda i, n=len(s): (0,) * n), True) for s in acc_outs]
    return _call(fn, (n_rows // tm,), ins, outs, name)


def _cols(fn, n_blocks, col_ins, out_leads, name, cb=LANES, bf16_outs=()):
    ins = [(a, a.shape[:-1] + (cb,), (lambda j, n=a.ndim, o=o: (0,) * (n - 1) + (j + o,))) for a, o in col_ins]
    outs = [(tuple(s) + (n_blocks * cb,), BF16 if k in bf16_outs else F32, tuple(s) + (cb,), (lambda j, n=len(s): (0,) * n + (j,)), False)
            for k, s in enumerate(out_leads)]
    return _call(fn, (n_blocks,), ins, outs, name)


def _grad_fn(f, n_diff, n_in):
    def g(*a):
        diff, consts, cts = a[:n_diff], a[n_diff:n_in], a[n_in:]
        _, vjp = jax.vjp(lambda *d: f(*d, *consts), *diff)
        return vjp(tuple(cts))
    return g


def _silu(x):
    return x * jax.nn.sigmoid(x)


def _rms(x, g):
    return x * lax.rsqrt(jnp.mean(x * x, axis=-1, keepdims=True) + EPS) * g


@jax.custom_vjp
def _softplus(x):
    return jnp.maximum(x, 0.0) + jnp.log(1.0 + jnp.exp(-jnp.abs(x)))


_softplus.defvjp(lambda x: (_softplus(x), x), lambda x, g: (g * jax.nn.sigmoid(x),))


@jax.custom_vjp
def _swap_halves(x):
    return pltpu.roll(x, 64, 1)


_swap_halves.defvjp(lambda x: (_swap_halves(x), None), lambda _, g: (_swap_halves(g),))


def _shift_rows(x, k, up):
    if k == 0:
        return x
    n = x.shape[0]
    t = lax.broadcasted_iota(jnp.int32, x.shape, 0)
    if up:
        return jnp.where(t < n - k, pltpu.roll(x, n - k, 0), 0.0)
    return jnp.where(t >= k, pltpu.roll(x, k, 0), 0.0)


@jax.custom_vjp
def _dwconv(x, w, b):
    k_taps = w.shape[0]
    y = b + w[k_taps - 1] * x
    for k in range(k_taps - 1):
        y = y + w[k] * _shift_rows(x, k_taps - 1 - k, False)
    return y


def _dwconv_fwd(x, w, b):
    return _dwconv(x, w, b), (x, w)


def _dwconv_bwd(saved, dy):
    x, w = saved
    k_taps = w.shape[0]
    dx = w[k_taps - 1] * dy
    dws = []
    for k in range(k_taps - 1):
        s = k_taps - 1 - k
        dx = dx + w[k] * _shift_rows(dy, s, True)
        dws.append(jnp.sum(dy * _shift_rows(x, s, False), axis=0, keepdims=True)[None])
    dws.append(jnp.sum(dy * x, axis=0, keepdims=True)[None])
    return dx, jnp.concatenate(dws, axis=0), jnp.sum(dy, axis=0, keepdims=True)


_dwconv.defvjp(_dwconv_fwd, _dwconv_bwd)


def _f_rms(x, g):
    return (_rms(x, g),)


def _rot(x, cosf, sins):
    outs = []
    for h in range(RET_HEADS):
        xh = x[:, h * RET_DK:(h + 1) * RET_DK]
        outs.append(xh * cosf + _swap_halves(xh) * sins)
    return jnp.concatenate(outs, axis=1)


def _f_retpre(q, k, cosf, sins):
    return _rot(q, cosf, sins), _rot(k, cosf, sins) * (RET_DK ** -0.5)


def _f_retpost(r, g):
    outs = []
    for h in range(RET_HEADS):
        rh = r[:, h * RET_DV:(h + 1) * RET_DV]
        rc = rh - jnp.mean(rh, axis=-1, keepdims=True)
        outs.append(_silu(g[:, h * RET_DV:(h + 1) * RET_DV]) * (rc * lax.rsqrt(jnp.mean(rc * rc, axis=-1, keepdims=True) + EPS)))
    return (jnp.concatenate(outs, axis=1),)


def _f_ssdconv(xbc, w, b):
    return (_silu(_dwconv(xbc, w, b)),)


def _f_ssdpre(xs, dtr, bias, alog, expand):
    dt = _softplus(dtr + bias)
    return xs * jnp.dot(dt, expand, precision=HIGHEST, preferred_element_type=F32), dt * (-jnp.exp(alog))


def _f_ssdpost(yc, xs, z, dskip, norm_w, expand):
    d_wide = jnp.dot(jnp.broadcast_to(dskip, (yc.shape[0], LANES)), expand, precision=HIGHEST, preferred_element_type=F32)
    y = (yc + d_wide * xs) * _silu(z)
    half = y.shape[1] // SSM_GROUPS
    outs = []
    for g in range(SSM_GROUPS):
        yg = y[:, g * half:(g + 1) * half]
        outs.append(yg * lax.rsqrt(jnp.mean(yg * yg, axis=-1, keepdims=True) + EPS))
    return (jnp.concatenate(outs, axis=1) * norm_w,)


def _f_ffnmid(gin, uin, wg, wu, bg, bu):
    return (_silu(_dwconv(gin, wg, bg)) * _dwconv(uin, wu, bu),)


def _f_confb(c1, g, b):
    mu = jnp.mean(c1, axis=-1, keepdims=True)
    xc = c1 - mu
    return (_silu(xc * lax.rsqrt(jnp.mean(xc * xc, axis=-1, keepdims=True) + EPS) * g + b),)


def _f_s5post(yc, u, dskip, glu_w):
    s = jax.nn.gelu(yc + dskip * u)
    z = jnp.dot(s.astype(BF16), glu_w.astype(BF16), preferred_element_type=F32)
    return (s * jax.nn.sigmoid(z),)


def _f_s5par(lr, li, ls, br, bi):
    step = jnp.exp(ls)
    mag = jnp.exp(lr * step)
    ab_re = mag * jnp.cos(li * step)
    ab_im = mag * jnp.sin(li * step)
    den = lr * lr + li * li
    f_re = ((ab_re - 1.0) * lr + ab_im * li) / den
    f_im = (ab_im * lr - (ab_re - 1.0) * li) / den
    return ab_re, ab_im, f_re[None] * br - f_im[None] * bi, f_re[None] * bi + f_im[None] * br


def _loss_step(x, tgt, g):
    def f(x_, g_):
        e = _rms(x_, g_) - tgt
        return 0.5 * jnp.sum(jnp.mean(e * e, axis=-1, keepdims=True), axis=0, keepdims=True)
    loss, vjp = jax.vjp(f, x, g)
    dx, dg = vjp(jnp.ones((1, 1), F32))
    return dx, dx, jnp.broadcast_to(loss, (8, LANES)), dg


def _tile(n, pref):
    if n <= pref:
        return n
    t = (pref // LANES) * LANES
    while n % t:
        t -= LANES
    return t


_DIMS = {"nn": (((1,), (0,)), ((), ())), "nt": (((1,), (1,)), ((), ())), "tn": (((0,), (0,)), ((), ()))}


def _mm(a, b, mode, name, res=None, out_dtype=F32, tm=1024, tn=1408, tk=1408):
    if mode == "nn":
        (m, k), n = a.shape, b.shape[1]
    elif mode == "nt":
        (m, k), n = a.shape, b.shape[0]
    else:
        (k, m), n = a.shape, b.shape[1]
    tm, tn, tk = _tile(m, tm), _tile(n, tn), _tile(k, tk)
    nk = k // tk
    a_spec = pl.BlockSpec((tk, tm), lambda i, j, kk: (kk, i)) if mode == "tn" else pl.BlockSpec((tm, tk), lambda i, j, kk: (i, kk))
    b_spec = pl.BlockSpec((tn, tk), lambda i, j, kk: (j, kk)) if mode == "nt" else pl.BlockSpec((tk, tn), lambda i, j, kk: (kk, j))
    o_spec = pl.BlockSpec((tm, tn), lambda i, j, kk: (i, j))
    has_res = res is not None

    def body(*refs):
        a_ref, b_ref = refs[0], refs[1]
        o_ref, acc = refs[-2], refs[-1]
        kk = pl.program_id(2)

        @pl.when(kk == 0)
        def _():
            acc[...] = jnp.zeros_like(acc)

        acc[...] += lax.dot_general(a_ref[...].astype(BF16), b_ref[...].astype(BF16), _DIMS[mode], preferred_element_type=F32)

        @pl.when(kk == nk - 1)
        def _():
            o_ref[...] = (acc[...] + refs[2][...] if has_res else acc[...]).astype(out_dtype)

    return pl.pallas_call(
        body, grid=(m // tm, n // tn, nk),
        in_specs=[a_spec, b_spec] + ([o_spec] if has_res else []),
        out_specs=o_spec, out_shape=jax.ShapeDtypeStruct((m, n), out_dtype),
        scratch_shapes=[pltpu.VMEM((tm, tn), F32)],
        compiler_params=pltpu.CompilerParams(dimension_semantics=("parallel", "parallel", "arbitrary"), vmem_limit_bytes=VMEM_LIMIT),
        name=name)(*([a, b] + ([res] if has_res else [])))


def _seq_block(t):
    return min(512, t)


def _causal_diff(i, j, blk):
    r = lax.broadcasted_iota(jnp.int32, (blk, blk), 0)
    c = lax.broadcasted_iota(jnp.int32, (blk, blk), 1)
    return (i - j) * blk + r - c


def _ret_decay(lg, i, j, blk):
    diff = _causal_diff(i, j, blk)
    return jnp.where(diff >= 0, jnp.exp(lg * jnp.maximum(diff, 0).astype(F32)), 0.0)


def _ret_row_decays(lg, i, j, blk):
    row = lax.broadcasted_iota(jnp.int32, (blk, RET_DK), 0)
    return jnp.exp(lg * row.astype(F32)), jnp.exp(lg * ((i - j) * blk - row).astype(F32))


def _ret_scaled(lg, i, j, blk, q_ref, k_ref):
    a, b = _ret_row_decays(lg, i, j, blk)
    return (q_ref[...] * a).astype(BF16), (k_ref[...] * b).astype(BF16)


def _ret_log_gamma():
    return jnp.log1p(-(2.0 ** (-5.0 - jnp.arange(RET_HEADS, dtype=F32))))


def _ret_fwd(qr, kr, proj, t):
    blk = _seq_block(t)
    nb = t // blk
    v_off = (2 * RET_HEADS * RET_DK) // RET_DV

    def body(lg_ref, q_ref, k_ref, v_ref, o_ref, acc):
        h, i, j = pl.program_id(0), pl.program_id(1), pl.program_id(2)

        @pl.when(j == 0)
        def _():
            acc[...] = jnp.zeros_like(acc)

        @pl.when(j < i)
        def _():
            qa, kb = _ret_scaled(lg_ref[h], i, j, blk, q_ref, k_ref)
            p = lax.dot_general(qa, kb, _DIMS["nt"], preferred_element_type=F32).astype(BF16)
            acc[...] += jnp.dot(p, v_ref[...].astype(BF16), preferred_element_type=F32)

        @pl.when(j == i)
        def _():
            s = lax.dot_general(q_ref[...].astype(BF16), k_ref[...].astype(BF16), _DIMS["nt"], preferred_element_type=F32)
            p = (s * _ret_decay(lg_ref[h], i, j, blk)).astype(BF16)
            o_ref[...] = acc[...] + jnp.dot(p, v_ref[...].astype(BF16), preferred_element_type=F32)

    return pl.pallas_call(
        body, grid=(RET_HEADS, nb, nb),
        in_specs=[pl.BlockSpec(memory_space=pltpu.SMEM),
                  pl.BlockSpec((blk, RET_DK), lambda h, i, j: (i, h)),
                  pl.BlockSpec((blk, RET_DK), lambda h, i, j: (jnp.minimum(j, i), h)),
                  pl.BlockSpec((blk, RET_DV), lambda h, i, j: (jnp.minimum(j, i), v_off + h))],
        out_specs=pl.BlockSpec((blk, RET_DV), lambda h, i, j: (i, h)),
        out_shape=jax.ShapeDtypeStruct((t, RET_HEADS * RET_DV), F32),
        scratch_shapes=[pltpu.VMEM((blk, RET_DV), F32)],
        compiler_params=pltpu.CompilerParams(dimension_semantics=("arbitrary",) * 3, vmem_limit_bytes=VMEM_LIMIT),
        name="ret_fwd")(_ret_log_gamma(), qr, kr, proj)


def _ret_bwd(qr, kr, proj, dr, t):
    blk = _seq_block(t)
    nb = t // blk
    v_off = (2 * RET_HEADS * RET_DK) // RET_DV

    def dq_body(lg_ref, q_ref, k_ref, v_ref, do_ref, dq_ref, acc):
        h, i, j = pl.program_id(0), pl.program_id(1), pl.program_id(2)

        @pl.when(j == 0)
        def _():
            acc[...] = jnp.zeros_like(acc)

        @pl.when(j < i)
        def _():
            a, b = _ret_row_decays(lg_ref[h], i, j, blk)
            ds = lax.dot_general(do_ref[...].astype(BF16), v_ref[...].astype(BF16), _DIMS["nt"], preferred_element_type=F32)
            acc[...] += a * jnp.dot(ds.astype(BF16), (k_ref[...] * b).astype(BF16), preferred_element_type=F32)

        @pl.when(j == i)
        def _():
            ds = lax.dot_general(do_ref[...].astype(BF16), v_ref[...].astype(BF16), _DIMS["nt"], preferred_element_type=F32)
            dsm = (ds * _ret_decay(lg_ref[h], i, j, blk)).astype(BF16)
            dq_ref[...] = acc[...] + jnp.dot(dsm, k_ref[...].astype(BF16), preferred_element_type=F32)

    dq = pl.pallas_call(
        dq_body, grid=(RET_HEADS, nb, nb),
        in_specs=[pl.BlockSpec(memory_space=pltpu.SMEM),
                  pl.BlockSpec((blk, RET_DK), lambda h, i, j: (i, h)),
                  pl.BlockSpec((blk, RET_DK), lambda h, i, j: (jnp.minimum(j, i), h)),
                  pl.BlockSpec((blk, RET_DV), lambda h, i, j: (jnp.minimum(j, i), v_off + h)),
                  pl.BlockSpec((blk, RET_DV), lambda h, i, j: (i, h))],
        out_specs=pl.BlockSpec((blk, RET_DK), lambda h, i, j: (i, h)),
        out_shape=jax.ShapeDtypeStruct((t, RET_HEADS * RET_DK), F32),
        scratch_shapes=[pltpu.VMEM((blk, RET_DK), F32)],
        compiler_params=pltpu.CompilerParams(dimension_semantics=("arbitrary",) * 3, vmem_limit_bytes=VMEM_LIMIT),
        name="ret_bwd_dq")(_ret_log_gamma(), qr, kr, proj, dr)

    def dkv_body(lg_ref, q_ref, k_ref, v_ref, do_ref, dk_ref, dv_ref, acc_k, acc_v):
        h, j, i = pl.program_id(0), pl.program_id(1), pl.program_id(2)

        @pl.when(i == 0)
        def _():
            acc_k[...] = jnp.zeros_like(acc_k)
            acc_v[...] = jnp.zeros_like(acc_v)

        @pl.when(i > j)
        def _():
            a, b = _ret_row_decays(lg_ref[h], i, j, blk)
            qa, kb = (q_ref[...] * a).astype(BF16), (k_ref[...] * b).astype(BF16)
            do = do_ref[...].astype(BF16)
            p = lax.dot_general(qa, kb, _DIMS["nt"], preferred_element_type=F32).astype(BF16)
            acc_v[...] += lax.dot_general(p, do, _DIMS["tn"], preferred_element_type=F32)
            ds = lax.dot_general(do, v_ref[...].astype(BF16), _DIMS["nt"], preferred_element_type=F32).astype(BF16)
            acc_k[...] += b * lax.dot_general(ds, qa, _DIMS["tn"], preferred_element_type=F32)

        @pl.when(i == j)
        def _():
            q = q_ref[...].astype(BF16)
            do = do_ref[...].astype(BF16)
            decay = _ret_decay(lg_ref[h], i, j, blk)
            s = lax.dot_general(q, k_ref[...].astype(BF16), _DIMS["nt"], preferred_element_type=F32)
            acc_v[...] += lax.dot_general((s * decay).astype(BF16), do, _DIMS["tn"], preferred_element_type=F32)
            ds = lax.dot_general(do, v_ref[...].astype(BF16), _DIMS["nt"], preferred_element_type=F32)
            acc_k[...] += lax.dot_general((ds * decay).astype(BF16), q, _DIMS["tn"], preferred_element_type=F32)

        @pl.when(i == nb - 1)
        def _():
            dk_ref[...] = acc_k[...]
            dv_ref[...] = acc_v[...].astype(BF16)

    dk, dv = pl.pallas_call(
        dkv_body, grid=(RET_HEADS, nb, nb),
        in_specs=[pl.BlockSpec(memory_space=pltpu.SMEM),
                  pl.BlockSpec((blk, RET_DK), lambda h, j, i: (jnp.maximum(i, j), h)),
                  pl.BlockSpec((blk, RET_DK), lambda h, j, i: (j, h)),
                  pl.BlockSpec((blk, RET_DV), lambda h, j, i: (j, v_off + h)),
                  pl.BlockSpec((blk, RET_DV), lambda h, j, i: (jnp.maximum(i, j), h))],
        out_specs=[pl.BlockSpec((blk, RET_DK), lambda h, j, i: (j, h)), pl.BlockSpec((blk, RET_DV), lambda h, j, i: (j, h))],
        out_shape=[jax.ShapeDtypeStruct((t, RET_HEADS * RET_DK), F32), jax.ShapeDtypeStruct((t, RET_HEADS * RET_DV), BF16)],
        scratch_shapes=[pltpu.VMEM((blk, RET_DK), F32), pltpu.VMEM((blk, RET_DV), F32)],
        compiler_params=pltpu.CompilerParams(dimension_semantics=("arbitrary",) * 3, vmem_limit_bytes=VMEM_LIMIT),
        name="ret_bwd_dkv")(_ret_log_gamma(), qr, kr, proj, dr)
    return dq, dk, dv


def _cumsum(x, reverse, name):
    t = x.shape[0]
    blk = _seq_block(t)
    nb = t // blk

    def body(x_ref, o_ref, carry):
        @pl.when(pl.program_id(0) == 0)
        def _():
            carry[...] = jnp.zeros_like(carry)

        r = lax.broadcasted_iota(jnp.int32, (blk, blk), 0)
        c = lax.broadcasted_iota(jnp.int32, (blk, blk), 1)
        tri = ((r <= c) if reverse else (r >= c)).astype(F32)
        o_ref[...] = jnp.dot(tri, x_ref[...], precision=HIGHEST, preferred_element_type=F32) + carry[...]
        carry[...] = o_ref[0:1, :] if reverse else o_ref[blk - 1:blk, :]

    idx = (lambda i: (nb - 1 - i, 0)) if reverse else (lambda i: (i, 0))
    return pl.pallas_call(
        body, grid=(nb,), in_specs=[pl.BlockSpec((blk, LANES), idx)], out_specs=pl.BlockSpec((blk, LANES), idx),
        out_shape=jax.ShapeDtypeStruct((t, LANES), F32), scratch_shapes=[pltpu.VMEM((1, LANES), F32)],
        compiler_params=pltpu.CompilerParams(dimension_semantics=("arbitrary",), vmem_limit_bytes=VMEM_LIMIT),
        name=name)(x)


def _ssd_decay(a_ref, at_ref, hh, mask):
    return jnp.exp(jnp.where(mask, a_ref[:, hh:hh + 1] - at_ref[hh:hh + 1, :], -jnp.inf))


def _head_lanes(s):
    lane = lax.broadcasted_iota(jnp.int32, (s.shape[0], LANES), 1)
    return jnp.concatenate([jnp.where(lane < SSM_P, s[:, 2 * p:2 * p + 1], s[:, 2 * p + 1:2 * p + 2])
                            for p in range(SSM_HEADS // 2)], axis=1)


_B_OFF, _C_OFF = 1024 // SSM_N, 1024 // SSM_N + SSM_GROUPS


def _ssd_fwd(xbc_act, xdt, a_cum, a_cum_t, t):
    blk = _seq_block(t)
    nb = t // blk

    def body(c0, c1, b0, b1, x_ref, ai_ref, aj_ref, at_ref, o_ref, acc):
        i, j = pl.program_id(0), pl.program_id(1)

        @pl.when(j == 0)
        def _():
            acc[...] = jnp.zeros_like(acc)

        @pl.when(j < i)
        def _():
            xv = (x_ref[...] * _head_lanes(jnp.exp(ai_ref[0:1, :] - aj_ref[...]))).astype(BF16)
            for g, (c_ref, b_ref) in enumerate(((c0, b0), (c1, b1))):
                cb = lax.dot_general(c_ref[...].astype(BF16), b_ref[...].astype(BF16), _DIMS["nt"], preferred_element_type=F32)
                cols = slice(g * half, (g + 1) * half)
                acc[:, cols] += jnp.dot(cb.astype(BF16), xv[:, cols], preferred_element_type=F32)

        @pl.when(j == i)
        def _():
            o_ref[...] = acc[...] * _head_lanes(jnp.exp(ai_ref[...] - ai_ref[0:1, :]))
            mask = _causal_diff(i, j, blk) >= 0
            for g, (c_ref, b_ref) in enumerate(((c0, b0), (c1, b1))):
                cb = lax.dot_general(c_ref[...].astype(BF16), b_ref[...].astype(BF16), _DIMS["nt"], preferred_element_type=F32)
                for h in range(SSM_HG):
                    hh = g * SSM_HG + h
                    cols = slice(hh * SSM_P, (hh + 1) * SSM_P)
                    m = (cb * _ssd_decay(ai_ref, at_ref, hh, mask)).astype(BF16)
                    o_ref[:, cols] += jnp.dot(m, x_ref[:, cols].astype(BF16), preferred_element_type=F32)

    half = SSM_HG * SSM_P
    row_i = lambda off: pl.BlockSpec((blk, SSM_N), lambda i, j, off=off: (i, off))
    row_j = lambda off: pl.BlockSpec((blk, SSM_N), lambda i, j, off=off: (jnp.minimum(j, i), off))
    return pl.pallas_call(
        body, grid=(nb, nb),
        in_specs=[row_i(_C_OFF), row_i(_C_OFF + 1), row_j(_B_OFF), row_j(_B_OFF + 1),
                  pl.BlockSpec((blk, SSM_HEADS * SSM_P), lambda i, j: (jnp.minimum(j, i), 0)),
                  pl.BlockSpec((blk, LANES), lambda i, j: (i, 0)),
                  pl.BlockSpec((blk, LANES), lambda i, j: (jnp.minimum(j, i), 0)),
                  pl.BlockSpec((SSM_HEADS, blk), lambda i, j: (0, jnp.minimum(j, i)))],
        out_specs=pl.BlockSpec((blk, SSM_HEADS * SSM_P), lambda i, j: (i, 0)),
        out_shape=jax.ShapeDtypeStruct((t, SSM_HEADS * SSM_P), F32),
        scratch_shapes=[pltpu.VMEM((blk, SSM_HEADS * SSM_P), F32)],
        compiler_params=pltpu.CompilerParams(dimension_semantics=("arbitrary",) * 2, vmem_limit_bytes=VMEM_LIMIT),
        name="ssd_fwd")(xbc_act, xbc_act, xbc_act, xbc_act, xdt, a_cum, a_cum, a_cum_t)


def _ssd_bwd(xbc_act, xdt, a_cum, a_cum_t, dy, expand, t):
    blk = _seq_block(t)
    nb = t // blk
    width = SSM_HEADS * SSM_P

    half = SSM_HG * SSM_P

    def head_sums(prod, e_ref):
        return lax.dot_general(prod, e_ref[...], _DIMS["nt"], precision=HIGHEST, preferred_element_type=F32)

    def q_body(c0, c1, b0, b1, x_ref, ai_ref, aj_ref, at_ref, dy_ref, e_ref, dc_ref, da_ref, acc_c, acc_a, acc_p, dyu):
        i, j = pl.program_id(0), pl.program_id(1)

        @pl.when(j == 0)
        def _():
            acc_c[...] = jnp.zeros_like(acc_c)
            acc_a[...] = jnp.zeros_like(acc_a)
            acc_p[...] = jnp.zeros_like(acc_p)
            dyu[...] = (dy_ref[...].astype(F32) * _head_lanes(jnp.exp(ai_ref[...] - ai_ref[0:1, :]))).astype(BF16)

        @pl.when(j < i)
        def _():
            xv = (x_ref[...] * _head_lanes(jnp.exp(ai_ref[0:1, :] - aj_ref[...]))).astype(BF16)
            for g, (c_ref, b_ref) in enumerate(((c0, b0), (c1, b1))):
                cols = slice(g * half, (g + 1) * half)
                bj = b_ref[...].astype(BF16)
                cb = lax.dot_general(c_ref[...].astype(BF16), bj, _DIMS["nt"], preferred_element_type=F32).astype(BF16)
                dcb = lax.dot_general(dyu[:, cols], xv[:, cols], _DIMS["nt"], preferred_element_type=F32)
                acc_c[:, g * SSM_N:(g + 1) * SSM_N] += jnp.dot(dcb.astype(BF16), bj, preferred_element_type=F32)
                acc_p[:, cols] += dyu[:, cols].astype(F32) * jnp.dot(cb, xv[:, cols], preferred_element_type=F32)

        @pl.when(j == i)
        def _():
            mask = _causal_diff(i, j, blk) >= 0
            for g, (c_ref, b_ref) in enumerate(((c0, b0), (c1, b1))):
                bj = b_ref[...].astype(BF16)
                cb = lax.dot_general(c_ref[...].astype(BF16), bj, _DIMS["nt"], preferred_element_type=F32)
                dcb = jnp.zeros((blk, blk), F32)
                for h in range(SSM_HG):
                    hh = g * SSM_HG + h
                    cols = slice(hh * SSM_P, (hh + 1) * SSM_P)
                    dm = lax.dot_general(dy_ref[:, cols].astype(BF16), x_ref[:, cols].astype(BF16), _DIMS["nt"],
                                         preferred_element_type=F32) * _ssd_decay(ai_ref, at_ref, hh, mask)
                    dcb = dcb + dm
                    acc_a[:, hh:hh + 1] += jnp.sum(dm * cb, axis=1, keepdims=True)
                acc_c[:, g * SSM_N:(g + 1) * SSM_N] += jnp.dot(dcb.astype(BF16), bj, preferred_element_type=F32)
            dc_ref[...] = acc_c[...]
            da_ref[...] = acc_a[...] + head_sums(acc_p[...], e_ref)

    row_i = lambda off: pl.BlockSpec((blk, SSM_N), lambda i, j, off=off: (i, off))
    row_j = lambda off: pl.BlockSpec((blk, SSM_N), lambda i, j, off=off: (jnp.minimum(j, i), off))
    e_spec = pl.BlockSpec((LANES, width), lambda i, j: (0, 0))
    dc, da_q = pl.pallas_call(
        q_body, grid=(nb, nb),
        in_specs=[row_i(_C_OFF), row_i(_C_OFF + 1), row_j(_B_OFF), row_j(_B_OFF + 1),
                  pl.BlockSpec((blk, width), lambda i, j: (jnp.minimum(j, i), 0)),
                  pl.BlockSpec((blk, LANES), lambda i, j: (i, 0)),
                  pl.BlockSpec((blk, LANES), lambda i, j: (jnp.minimum(j, i), 0)),
                  pl.BlockSpec((SSM_HEADS, blk), lambda i, j: (0, jnp.minimum(j, i))),
                  pl.BlockSpec((blk, width), lambda i, j: (i, 0)), e_spec],
        out_specs=[pl.BlockSpec((blk, SSM_GROUPS * SSM_N), lambda i, j: (i, 0)), pl.BlockSpec((blk, LANES), lambda i, j: (i, 0))],
        out_shape=[jax.ShapeDtypeStruct((t, SSM_GROUPS * SSM_N), F32), jax.ShapeDtypeStruct((t, LANES), F32)],
        scratch_shapes=[pltpu.VMEM((blk, SSM_GROUPS * SSM_N), F32), pltpu.VMEM((blk, LANES), F32), pltpu.VMEM((blk, width), F32),
                        pltpu.VMEM((blk, width), BF16)],
        compiler_params=pltpu.CompilerParams(dimension_semantics=("arbitrary",) * 2, vmem_limit_bytes=VMEM_LIMIT),
        name="ssd_bwd_q")(xbc_act, xbc_act, xbc_act, xbc_act, xdt, a_cum, a_cum, a_cum_t, dy, expand)

    def k_body(c0, c1, b0, b1, x_ref, ai_ref, aj_ref, at_ref, dy_ref, e_ref, db_ref, dx_ref, da_ref, dat_ref, acc_b, acc_x, acc_a, acc_p):
        j, i = pl.program_id(0), pl.program_id(1)

        @pl.when(i == 0)
        def _():
            acc_b[...] = jnp.zeros_like(acc_b)
            acc_x[...] = jnp.zeros_like(acc_x)
            acc_a[...] = jnp.zeros_like(acc_a)
            acc_p[...] = jnp.zeros_like(acc_p)

        @pl.when(i > j)
        def _():
            v = _head_lanes(jnp.exp(ai_ref[0:1, :] - aj_ref[...]))
            dyu_all = (dy_ref[...].astype(F32) * _head_lanes(jnp.exp(ai_ref[...] - ai_ref[0:1, :]))).astype(BF16)
            xv = (x_ref[...] * v).astype(BF16)
            for g, (c_ref, b_ref) in enumerate(((c0, b0), (c1, b1))):
                cols = slice(g * half, (g + 1) * half)
                ci = c_ref[...].astype(BF16)
                cb = lax.dot_general(ci, b_ref[...].astype(BF16), _DIMS["nt"], preferred_element_type=F32).astype(BF16)
                dcb = lax.dot_general(dyu_all[:, cols], xv[:, cols], _DIMS["nt"], preferred_element_type=F32)
                acc_b[:, g * SSM_N:(g + 1) * SSM_N] += lax.dot_general(dcb.astype(BF16), ci, _DIMS["tn"], preferred_element_type=F32)
                dxv = lax.dot_general(cb, dyu_all[:, cols], _DIMS["tn"], preferred_element_type=F32)
                acc_x[:, cols] += v[:, cols] * dxv
                acc_p[:, cols] += xv[:, cols].astype(F32) * dxv

        @pl.when(i == j)
        def _():
            mask = _causal_diff(i, j, blk) >= 0
            for g, (c_ref, b_ref) in enumerate(((c0, b0), (c1, b1))):
                ci = c_ref[...].astype(BF16)
                cb = lax.dot_general(ci, b_ref[...].astype(BF16), _DIMS["nt"], preferred_element_type=F32)
                dcb = jnp.zeros((blk, blk), F32)
                for h in range(SSM_HG):
                    hh = g * SSM_HG + h
                    cols = slice(hh * SSM_P, (hh + 1) * SSM_P)
                    decay = _ssd_decay(ai_ref, at_ref, hh, mask)
                    dyh = dy_ref[:, cols].astype(BF16)
                    acc_x[:, cols] += lax.dot_general((cb * decay).astype(BF16), dyh, _DIMS["tn"], preferred_element_type=F32)
                    dm = lax.dot_general(dyh, x_ref[:, cols].astype(BF16), _DIMS["nt"], preferred_element_type=F32) * decay
                    dcb = dcb + dm
                    acc_a[hh:hh + 1, :] += jnp.sum(dm * cb, axis=0, keepdims=True)
                acc_b[:, g * SSM_N:(g + 1) * SSM_N] += lax.dot_general(dcb.astype(BF16), ci, _DIMS["tn"], preferred_element_type=F32)

        @pl.when(i == nb - 1)
        def _():
            db_ref[...] = acc_b[...]
            dx_ref[...] = acc_x[...]
            dat_ref[...] = acc_a[...]
            da_ref[...] = head_sums(acc_p[...], e_ref)

    rowk_i = lambda off: pl.BlockSpec((blk, SSM_N), lambda j, i, off=off: (jnp.maximum(i, j), off))
    rowk_j = lambda off: pl.BlockSpec((blk, SSM_N), lambda j, i, off=off: (j, off))
    db, dx, da_k, da_k_t = pl.pallas_call(
        k_body, grid=(nb, nb),
        in_specs=[rowk_i(_C_OFF), rowk_i(_C_OFF + 1), rowk_j(_B_OFF), rowk_j(_B_OFF + 1),
                  pl.BlockSpec((blk, width), lambda j, i: (j, 0)),
                  pl.BlockSpec((blk, LANES), lambda j, i: (jnp.maximum(i, j), 0)),
                  pl.BlockSpec((blk, LANES), lambda j, i: (j, 0)),
                  pl.BlockSpec((SSM_HEADS, blk), lambda j, i: (0, j)),
                  pl.BlockSpec((blk, width), lambda j, i: (jnp.maximum(i, j), 0)), e_spec],
        out_specs=[pl.BlockSpec((blk, SSM_GROUPS * SSM_N), lambda j, i: (j, 0)), pl.BlockSpec((blk, width), lambda j, i: (j, 0)),
                   pl.BlockSpec((blk, LANES), lambda j, i: (j, 0)), pl.BlockSpec((SSM_HEADS, blk), lambda j, i: (0, j))],
        out_shape=[jax.ShapeDtypeStruct((t, SSM_GROUPS * SSM_N), F32), jax.ShapeDtypeStruct((t, width), F32),
                   jax.ShapeDtypeStruct((t, LANES), F32), jax.ShapeDtypeStruct((SSM_HEADS, t), F32)],
        scratch_shapes=[pltpu.VMEM((blk, SSM_GROUPS * SSM_N), F32), pltpu.VMEM((blk, width), F32), pltpu.VMEM((SSM_HEADS, blk), F32),
                        pltpu.VMEM((blk, width), F32)],
        compiler_params=pltpu.CompilerParams(dimension_semantics=("arbitrary",) * 2, vmem_limit_bytes=VMEM_LIMIT),
        name="ssd_bwd_k")(xbc_act, xbc_act, xbc_act, xbc_act, xdt, a_cum, a_cum, a_cum_t, dy, expand)
    return dc, da_q, db, dx, da_k, da_k_t


def _s5_scan(bu, a_re, a_im, reverse, x_prev=None):
    t = bu.shape[0]
    cb = 512
    ncb = S5_COLS // cb
    tb = min(1024, t)
    ntb = t // tb
    sub = 8
    shape = (sub, cb)

    def cmul(ar, ai, br, bi):
        return ar * br - ai * bi, ar * bi + ai * br

    def body(*refs):
        if reverse:
            br_ref, bi_ref, ar_ref, ai_ref, xr_ref, xi_ref, or_ref, oi_ref, dar_ref, dai_ref, carry, dacc, states = refs
        else:
            br_ref, bi_ref, ar_ref, ai_ref, or_ref, oi_ref, or_bf_ref, oi_bf_ref, carry = refs

        @pl.when(pl.program_id(1) == 0)
        def _():
            carry[...] = jnp.zeros_like(carry)
            if reverse:
                dacc[...] = jnp.zeros_like(dacc)

        row = lax.broadcasted_iota(jnp.int32, shape, 0)
        a1 = (jnp.broadcast_to(ar_ref[...], shape), jnp.broadcast_to(-ai_ref[...] if reverse else ai_ref[...], shape))
        a2 = cmul(*a1, *a1)
        a4 = cmul(*a2, *a2)
        a8 = cmul(*a4, *a4)
        steps = (sub - row) if reverse else (row + 1)
        pw = (jnp.ones(shape, F32), jnp.zeros(shape, F32))
        for bit, p in ((1, a1), (2, a2), (4, a4), (8, a8)):
            on = (steps & bit) != 0
            pw = cmul(*pw, jnp.where(on, p[0], 1.0), jnp.where(on, p[1], 0.0))
        edge = 0 if reverse else sub - 1

        def shift(v, s):
            if reverse:
                return jnp.where(row < sub - s, pltpu.roll(v, sub - s, 0), 0.0)
            return jnp.where(row >= s, pltpu.roll(v, s, 0), 0.0)

        def tile(n, state):
            r0 = pl.multiple_of((tb // sub - 1 - n if reverse else n) * sub, sub)
            rows = pl.ds(r0, sub)
            xr, xi = br_ref[rows, :], bi_ref[rows, :]
            for s, p in ((1, a1), (2, a2), (4, a4)):
                dr, di = cmul(*p, shift(xr, s), shift(xi, s))
                xr, xi = xr + dr, xi + di
            cr, ci = jnp.broadcast_to(state[0], shape), jnp.broadcast_to(state[1], shape)
            dr, di = cmul(*pw, cr, ci)
            xr, xi = xr + dr, xi + di
            if reverse:
                states[0, rows, :], states[1, rows, :] = xr, xi
                gr = jnp.where(row < sub - 1, pltpu.roll(xr, sub - 1, 0), cr)
                gi = jnp.where(row < sub - 1, pltpu.roll(xi, sub - 1, 0), ci)
                pr, pi = xr_ref[rows, :], xi_ref[rows, :]
                dacc[0] += gr * pr + gi * pi
                dacc[1] += gi * pr - gr * pi
            else:
                or_ref[rows, :], oi_ref[rows, :] = xr, xi
            return (jnp.sum(jnp.where(row == edge, xr, 0.0), axis=0, keepdims=True),
                    jnp.sum(jnp.where(row == edge, xi, 0.0), axis=0, keepdims=True))

        last = lax.fori_loop(0, tb // sub, tile, (carry[0], carry[1]))
        carry[0], carry[1] = last
        if reverse:
            or_ref[...], oi_ref[...] = states[0].astype(BF16), states[1].astype(BF16)
            dar_ref[...] = jnp.sum(dacc[0], axis=0, keepdims=True)
            dai_ref[...] = jnp.sum(dacc[1], axis=0, keepdims=True)
        else:
            or_bf_ref[...], oi_bf_ref[...] = or_ref[...].astype(BF16), oi_ref[...].astype(BF16)

    tblock = (lambda k: ntb - 1 - k) if reverse else (lambda k: k)
    re_spec = pl.BlockSpec((tb, cb), lambda j, k: (tblock(k), j))
    im_spec = pl.BlockSpec((tb, cb), lambda j, k: (tblock(k), j + ncb))
    a_spec = pl.BlockSpec((1, cb), lambda j, k: (0, j))
    ins, in_specs = [bu, bu, a_re, a_im], [re_spec, im_spec, a_spec, a_spec]
    seq_bf = jax.ShapeDtypeStruct((t, S5_COLS), BF16)
    scratch = [pltpu.VMEM((2, 1, cb), F32)]
    if reverse:
        ins += [x_prev, x_prev]
        in_specs += [re_spec, im_spec]
        out_shape = [seq_bf, seq_bf] + [jax.ShapeDtypeStruct((1, S5_COLS), F32)] * 2
        out_specs = [re_spec, re_spec, a_spec, a_spec]
        scratch += [pltpu.VMEM((2, sub, cb), F32), pltpu.VMEM((2, tb, cb), F32)]
    else:
        out_shape = [jax.ShapeDtypeStruct((t, S5_COLS), F32)] * 2 + [seq_bf, seq_bf]
        out_specs = [re_spec] * 4
    return pl.pallas_call(
        body, grid=(ncb, ntb), in_specs=in_specs, out_specs=out_specs, out_shape=out_shape, scratch_shapes=scratch,
        compiler_params=pltpu.CompilerParams(dimension_semantics=("arbitrary", "arbitrary"), vmem_limit_bytes=VMEM_LIMIT),
        name="s5_scan_bwd" if reverse else "s5_scan_fwd")(*ins)


CONF_DIM = 512
CONF_K = 31
CONF_PAD = 32


def _conf_conv(proj_o, w, b, dc1=None):
    t = proj_o.shape[0]
    cb = LANES
    ncb = CONF_DIM // cb
    chunk = min(512, t)
    chunks = range(0, t, chunk)
    bwd = dc1 is not None

    def body(*refs):
        if bwd:
            ca_ref, cg_ref, w_ref, b_ref, dy_ref, dca_ref, dcg_ref, dw_ref, db_ref, xs, dys = refs
        else:
            ca_ref, cg_ref, w_ref, b_ref, o_ref, xs = refs
        xs[pl.ds(0, CONF_PAD), :] = jnp.zeros((CONF_PAD, cb), F32)
        for t0 in chunks:
            rows = pl.ds(t0, chunk)
            xs[pl.ds(CONF_PAD + t0, chunk), :] = ca_ref[rows, :] * jax.nn.sigmoid(cg_ref[rows, :])
        if not bwd:
            for t0 in chunks:
                acc = jnp.broadcast_to(b_ref[...], (chunk, cb))
                for k in range(CONF_K):
                    acc = acc + w_ref[k] * xs[pl.ds(CONF_PAD + t0 - (CONF_K - 1 - k), chunk), :]
                o_ref[pl.ds(t0, chunk), :] = acc
            return
        dys[pl.ds(t, CONF_PAD), :] = jnp.zeros((CONF_PAD, cb), F32)
        db = jnp.zeros((1, cb), F32)
        for t0 in chunks:
            dys[pl.ds(t0, chunk), :] = dy_ref[pl.ds(t0, chunk), :]
            db = db + jnp.sum(dy_ref[pl.ds(t0, chunk), :], axis=0, keepdims=True)
        db_ref[...] = db
        for t0 in chunks:
            rows = pl.ds(t0, chunk)
            acc = jnp.zeros((chunk, cb), F32)
            for k in range(CONF_K):
                acc = acc + w_ref[k] * dys[pl.ds(t0 + (CONF_K - 1 - k), chunk), :]
            sig = jax.nn.sigmoid(cg_ref[rows, :])
            dca_ref[rows, :] = (acc * sig).astype(BF16)
            dcg_ref[rows, :] = (acc * ca_ref[rows, :] * sig * (1.0 - sig)).astype(BF16)
        for k in range(CONF_K):
            dwk = jnp.zeros((1, cb), F32)
            for t0 in chunks:
                window = xs[pl.ds(CONF_PAD + t0 - (CONF_K - 1 - k), chunk), :]
                dwk = dwk + jnp.sum(dy_ref[pl.ds(t0, chunk), :] * window, axis=0, keepdims=True)
            dw_ref[k] = dwk

    col = lambda off: pl.BlockSpec((t, cb), lambda j, off=off: (0, j + off))
    w_spec = pl.BlockSpec((CONF_K, 1, cb), lambda j: (0, 0, j))
    b_spec = pl.BlockSpec((1, cb), lambda j: (0, j))
    seq = jax.ShapeDtypeStruct((t, CONF_DIM), F32)
    ins, in_specs = [proj_o, proj_o, w, b], [col(0), col(ncb), w_spec, b_spec]
    scratch = [pltpu.VMEM((CONF_PAD + t, cb), F32)]
    if bwd:
        ins, in_specs = ins + [dc1], in_specs + [col(0)]
        seq_bf = jax.ShapeDtypeStruct((t, CONF_DIM), BF16)
        out_shape, out_specs = [seq_bf, seq_bf, jax.ShapeDtypeStruct(w.shape, F32), jax.ShapeDtypeStruct(b.shape, F32)], [col(0), col(0), w_spec, b_spec]
        scratch = scratch + [pltpu.VMEM((t + CONF_PAD, cb), F32)]
    else:
        out_shape, out_specs = seq, col(0)
    return pl.pallas_call(
        body, grid=(ncb,), in_specs=in_specs, out_specs=out_specs, out_shape=out_shape, scratch_shapes=scratch,
        compiler_params=pltpu.CompilerParams(dimension_semantics=("arbitrary",), vmem_limit_bytes=VMEM_LIMIT),
        name="conf_conv_bwd" if bwd else "conf_conv")(*ins)


N_CHIPS = 4
_HBM = pl.BlockSpec(memory_space=pl.ANY)
_MESH_ID = pl.DeviceIdType.MESH


def _comm_call(body, srcs, out_shapes, n_sems, name):
    n = len(srcs)
    return pl.pallas_call(
        body, out_shape=out_shapes, in_specs=[_HBM] * n, out_specs=[_HBM] * n,
        scratch_shapes=[pltpu.SemaphoreType.DMA((n_sems,)), pltpu.SemaphoreType.DMA((n_sems,)), pltpu.SemaphoreType.DMA((n,))],
        compiler_params=pltpu.CompilerParams(has_side_effects=True), name=name)(*srcs)


def _gather(srcs, name):
    n = len(srcs)
    per = N_DEV - 1

    def body(*refs):
        src_refs, out_refs = refs[:n], refs[n:2 * n]
        send_sems, recv_sems, local_sems = refs[2 * n:]
        x, y, c = lax.axis_index("x"), lax.axis_index("y"), lax.axis_index("c")
        me, sibling = (x, y, c), (x, y, 1 - c)
        chips = [(1 - x, y), (x, 1 - y), (1 - x, 1 - y)]

        def copy(i, k, block, to, from_src=False):
            rows = out_refs[i].at[4 * block[0] + 2 * block[1] + block[2]]
            return pltpu.make_async_remote_copy(
                src_ref=src_refs[i] if from_src else rows, dst_ref=rows, send_sem=send_sems.at[per * i + k],
                recv_sem=recv_sems.at[per * i + k], device_id=to, device_id_type=_MESH_ID)

        local = [pltpu.make_async_copy(src_refs[i], out_refs[i].at[4 * x + 2 * y + c], local_sems.at[i]) for i in range(n)]
        for cp in local:
            cp.start()
        sends = []
        for i in range(n):
            sends.append(copy(i, 0, me, sibling, True))
            sends += [copy(i, 1 + j, me, (*chip, c), True) for j, chip in enumerate(chips)]
        for cp in sends:
            cp.start()
        for j, chip in enumerate(chips):
            for i in range(n):
                copy(i, 1 + j, (*chip, c), me).wait_recv()
                passed = copy(i, 4 + j, (*chip, c), sibling)
                passed.start()
                sends.append(passed)
        for i in range(n):
            copy(i, 0, sibling, me).wait_recv()
            for j, chip in enumerate(chips):
                copy(i, 4 + j, (*chip, 1 - c), me).wait_recv()
        for cp in sends:
            cp.wait_send()
        for cp in local:
            cp.wait()

    return _comm_call(body, srcs, [jax.ShapeDtypeStruct((N_DEV,) + s.shape, s.dtype) for s in srcs], per * n, name)


def _swap_sibling(srcs, name):
    n = len(srcs)

    def body(*refs):
        src_refs, out_refs = refs[:n], refs[n:2 * n]
        send_sems, recv_sems, _ = refs[2 * n:]
        sibling = (lax.axis_index("x"), lax.axis_index("y"), 1 - lax.axis_index("c"))
        copies = [pltpu.make_async_remote_copy(src_ref=src_refs[i], dst_ref=out_refs[i], send_sem=send_sems.at[i],
                                               recv_sem=recv_sems.at[i], device_id=sibling, device_id_type=_MESH_ID) for i in range(n)]
        for cp in copies:
            cp.start()
        for cp in copies:
            cp.wait_recv()
        for cp in copies:
            cp.wait_send()

    return _comm_call(body, srcs, [jax.ShapeDtypeStruct(s.shape, s.dtype) for s in srcs], n, name)


def _exchange_chips(srcs, name):
    n = len(srcs)
    per = N_CHIPS - 1

    def body(*refs):
        src_refs, out_refs = refs[:n], refs[n:2 * n]
        send_sems, recv_sems, local_sems = refs[2 * n:]
        x, y, c = lax.axis_index("x"), lax.axis_index("y"), lax.axis_index("c")
        mine = 2 * x + y
        local = [pltpu.make_async_copy(src_refs[i].at[mine], out_refs[i].at[mine], local_sems.at[i]) for i in range(n)]
        for cp in local:
            cp.start()
        copies = []
        for k in range(1, N_CHIPS):
            px, py = x ^ (k >> 1), y ^ (k & 1)
            peer = 2 * px + py
            for i in range(n):
                sem = per * i + k - 1
                send = pltpu.make_async_remote_copy(
                    src_ref=src_refs[i].at[peer], dst_ref=out_refs[i].at[mine], send_sem=send_sems.at[sem], recv_sem=recv_sems.at[sem],
                    device_id=(px, py, c), device_id_type=_MESH_ID)
                send.start()
                recv = pltpu.make_async_remote_copy(
                    src_ref=src_refs[i].at[peer], dst_ref=out_refs[i].at[peer], send_sem=send_sems.at[sem], recv_sem=recv_sems.at[sem],
                    device_id=(px, py, c), device_id_type=_MESH_ID)
                copies.append((send, recv))
        for _, recv in copies:
            recv.wait_recv()
        for send, _ in copies:
            send.wait_send()
        for cp in local:
            cp.wait()

    return _comm_call(body, srcs, [jax.ShapeDtypeStruct(s.shape, s.dtype) for s in srcs], per * n, name)


def _add(a, b, name):
    k, rows, cols = a.shape
    tr = _row_tile(rows)

    def body(a_ref, b_ref, o_ref):
        o_ref[...] = (a_ref[...].astype(F32) + b_ref[...].astype(F32)).astype(o_ref.dtype)

    spec = pl.BlockSpec((k, tr, cols), lambda i: (0, i, 0))
    return pl.pallas_call(
        body, grid=(rows // tr,), in_specs=[spec, spec], out_specs=spec, out_shape=jax.ShapeDtypeStruct(a.shape, a.dtype),
        compiler_params=pltpu.CompilerParams(dimension_semantics=("parallel",), vmem_limit_bytes=VMEM_LIMIT), name=name)(a, b)


def _row_tile(r, pref=256):
    if r <= pref:
        return r
    t = pref // 16 * 16
    while r % t:
        t -= 16
    return t


def _join_cols(g, width, name):
    _, rows, ws = g.shape
    tr = _row_tile(rows)
    tail = width - N_DEV * ws

    def body(g_ref, o_ref):
        for d in range(N_DEV):
            o_ref[:, pl.ds(d * ws, ws)] = g_ref[d]
        if tail:
            o_ref[:, pl.ds(N_DEV * ws, tail)] = jnp.zeros((tr, tail), g.dtype)

    return pl.pallas_call(
        body, grid=(rows // tr,), in_specs=[pl.BlockSpec((N_DEV, tr, ws), lambda i: (0, i, 0))],
        out_specs=pl.BlockSpec((tr, width), lambda i: (i, 0)), out_shape=jax.ShapeDtypeStruct((rows, width), g.dtype),
        compiler_params=pltpu.CompilerParams(dimension_semantics=("parallel",), vmem_limit_bytes=VMEM_LIMIT), name=name)(g)


def _split_cols(full, ws, dtype, name):
    rows, width = full.shape
    tr = _row_tile(rows)

    def body(x_ref, o_ref):
        for d in range(N_DEV):
            o_ref[d] = x_ref[:, pl.ds(d * ws, ws)].astype(dtype)

    return pl.pallas_call(
        body, grid=(rows // tr,), in_specs=[pl.BlockSpec((tr, width), lambda i: (i, 0))],
        out_specs=pl.BlockSpec((N_DEV, tr, ws), lambda i: (0, i, 0)), out_shape=jax.ShapeDtypeStruct((N_DEV, rows, ws), dtype),
        compiler_params=pltpu.CompilerParams(dimension_semantics=("parallel",), vmem_limit_bytes=VMEM_LIMIT), name=name)(full)


def _adamw(parts, w, m, v, name):
    r, c = w.shape
    n_parts = parts.shape[0]
    tr = _row_tile(r)

    def body(p_ref, w_ref, m_ref, v_ref, g_ref, d_ref, nm_ref, nv_ref):
        g = p_ref[0].astype(F32)
        for s in range(1, n_parts):
            g = g + p_ref[s].astype(F32)
        nm = ADAM_B1 * m_ref[...] + (1.0 - ADAM_B1) * g
        nv = ADAM_B2 * v_ref[...] + (1.0 - ADAM_B2) * (g * g)
        m_hat = nm / (1.0 - ADAM_B1 ** ADAM_STEP)
        v_hat = nv / (1.0 - ADAM_B2 ** ADAM_STEP)
        g_ref[...] = g
        nm_ref[...] = nm
        nv_ref[...] = nv
        d_ref[...] = -ADAM_LR * (m_hat / (jnp.sqrt(v_hat) + ADAM_EPS) + ADAM_WD * w_ref[...])

    spec = pl.BlockSpec((tr, c), lambda i: (i, 0))
    return pl.pallas_call(
        body, grid=(r // tr,), in_specs=[pl.BlockSpec((n_parts, tr, c), lambda i: (0, i, 0)), spec, spec, spec],
        out_specs=[spec] * 4, out_shape=[jax.ShapeDtypeStruct((r, c), F32)] * 4,
        compiler_params=pltpu.CompilerParams(dimension_semantics=("parallel",), vmem_limit_bytes=VMEM_LIMIT),
        name=name)(parts, w, m, v)


def _pack_rows(n_elems, mult):
    rows = -(-n_elems // PACK_COLS)
    return -(-rows // mult) * mult


def _pack(arrays, dtype, mult, lead=()):
    flat = jnp.concatenate([a.astype(dtype).reshape(lead + (-1,)) for a in arrays], axis=-1)
    rows = _pack_rows(flat.shape[-1], mult)
    flat = jnp.pad(flat, [(0, 0)] * len(lead) + [(0, rows * PACK_COLS - flat.shape[-1])])
    return flat.reshape(lead + (rows, PACK_COLS))


def _unpack(buf, shapes, lead=()):
    flat = buf.reshape(lead + (-1,))
    out, off = [], 0
    for s in shapes:
        n = math.prod(s)
        out.append(flat[..., off:off + n].reshape(lead + tuple(s)))
        off += n
    return out


def _join_shards(piece, axis):
    moved = jnp.moveaxis(piece, 0, axis)
    shape = moved.shape
    return moved.reshape(shape[:axis] + (shape[axis] * shape[axis + 1],) + shape[axis + 2:])


def _split_shards(full, axis):
    shape = full.shape
    return jnp.moveaxis(full.reshape(shape[:axis] + (N_DEV, shape[axis] // N_DEV) + shape[axis + 1:]), axis, 0)


def _block_diag(blocks):
    g, r, c = blocks.shape
    eye = jnp.eye(g, dtype=blocks.dtype)
    return (blocks[:, :, None, :] * eye[:, None, :, None]).reshape(g * r, g * c)


def _diag_blocks(mat, g):
    r, c = mat.shape[0] // g, mat.shape[1] // g
    eye = jnp.eye(g, dtype=mat.dtype)
    return jnp.sum(mat.reshape(g, r, g, c) * eye[:, None, :, None], axis=2)


def _pad_lanes(a):
    a = a.reshape(1, -1)
    return jnp.pad(a, ((0, 0), (0, LANES - a.shape[1])))


def _head_expand():
    h = jnp.arange(LANES)[:, None]
    ch = jnp.arange(SSM_HEADS * SSM_P)[None, :] // SSM_P
    return (h == ch).astype(F32)


def _rotary_tables(t):
    inv = 10000.0 ** (-jnp.arange(0, RET_DK, 2, dtype=F32) / RET_DK)
    ang = jnp.arange(t).astype(F32)[:, None] * inv[None, :]
    cos, sin = jnp.cos(ang), jnp.sin(ang)
    return jnp.concatenate([cos, cos], axis=1), jnp.concatenate([-sin, sin], axis=1)


def _rms_fwd(x, g, name):
    return _rows(_f_rms, x.shape[0], [(x, D_MODEL, 0)], [g], [D_MODEL], [], name, bf16_outs=(0,))[0]


def _rms_bwd(x, g, dh, dres, name):
    def fn(x_, dh_, dres_, g_):
        _, vjp = jax.vjp(lambda a, b: _rms(a, b), x_, g_)
        dx, dg = vjp(dh_)
        return dx + dres_, dx + dres_, dg
    return _rows(fn, x.shape[0], [(x, D_MODEL, 0), (dh, D_MODEL, 0), (dres, D_MODEL, 0)], [g], [D_MODEL, D_MODEL], [(1, D_MODEL)],
                 name, bf16_outs=(1,))


def _ffn_fwd(x, norm_g, w_up, dw_w, dw_b, w_down, tag):
    t = x.shape[0]
    nbk = D_FF // LANES
    h = _rms_fwd(x, norm_g, "ffn_norm_" + tag)
    up = _mm(h, w_up, "nn", "ffn_up_" + tag)
    mid = _cols(_f_ffnmid, nbk, [(up, 0), (up, nbk), (dw_w, 0), (dw_w, nbk), (dw_b, 0), (dw_b, nbk)], [(t,)], "ffn_mid_" + tag,
                bf16_outs=(0,))[0]
    out = _mm(mid, w_down, "nn", "ffn_down_" + tag, res=x)
    return out, (h, up, mid)


def _ffn_bwd(x, norm_g, w_up, dw_w, dw_b, w_down, saved, dout, dout_bf, tag):
    t = x.shape[0]
    nbk = D_FF // LANES
    h, up, mid = saved
    d_w_down = _mm(mid, dout_bf, "tn", "ffn_down_dw_" + tag, out_dtype=BF16)
    dmid = _mm(dout_bf, w_down, "nt", "ffn_down_dx_" + tag)
    dgin, duin, dwg, dwu, dbg, dbu = _cols(
        _grad_fn(_f_ffnmid, 6, 6), nbk,
        [(up, 0), (up, nbk), (dw_w, 0), (dw_w, nbk), (dw_b, 0), (dw_b, nbk), (dmid, 0)],
        [(t,), (t,), (3, 1), (3, 1), (1,), (1,)], "ffn_mid_bwd_" + tag, bf16_outs=(0, 1))
    dup = jnp.concatenate([dgin, duin], axis=1)
    d_w_up = _mm(h, dup, "tn", "ffn_up_dw_" + tag)
    dh = _mm(dup, w_up, "nt", "ffn_up_dx_" + tag)
    dx, dx_bf, dnorm = _rms_bwd(x, norm_g, dh, dout, "ffn_norm_bwd_" + tag)
    return dx, dx_bf, dict(norm=dnorm, w_up=d_w_up, dw_w=jnp.concatenate([dwg, dwu], axis=2)[:, 0],
                           dw_b=jnp.concatenate([dbg, dbu], axis=1), w_down=d_w_down)


def _local_step(x, tgt, w):
    t = x.shape[0]
    grads = {}
    expand = _head_expand()
    cosf, sins = _rotary_tables(t)
    mix_g = [w['mix_norm'][i:i + 1] for i in range(2)]
    ffn_g = [w['ffn_norm'][i:i + 1] for i in range(2)]
    ffn_dw_w = [w['ffn_dw_w'][i][:, None, :] for i in range(2)]
    ffn_dw_b = [w['ffn_dw_b'][i:i + 1] for i in range(2)]

    w_e = w['e_w_in']
    conv_w = w['e_conv_w'][0][:, None, :]
    conv_b = w['e_conv_b']
    dt_bias, a_log, d_skip = _pad_lanes(w['e_dt_bias']), _pad_lanes(w['e_a_log']), _pad_lanes(w['e_d'])
    h0 = _rms_fwd(x, mix_g[0], "mix_norm_0")
    proj = _mm(h0, w_e, "nn", "e_in")
    qr, kr = _rows(_f_retpre, t, [(proj, 512, 0), (proj, 512, 1), (cosf, LANES, 0), (sins, LANES, 0)], [], [512, 512], [], "ret_pre")
    r = _ret_fwd(qr, kr, proj, t)
    y_ret = _rows(_f_retpost, t, [(r, 1024, 0), (proj, 1024, 2)], [], [1024], [], "ret_post", bf16_outs=(0,))[0]
    xbc_act = _cols(_f_ssdconv, 12, [(proj, 32), (conv_w, 0), (conv_b, 0)], [(t,)], "ssd_conv")[0]
    xdt, da = _rows(_f_ssdpre, t, [(xbc_act, 1024, 0), (proj, LANES, 44)], [dt_bias, a_log, expand], [1024, LANES], [], "ssd_pre")
    a_cum = _cumsum(da, False, "ssd_cumsum")
    a_cum_t = a_cum[:, :SSM_HEADS].T
    yc = _ssd_fwd(xbc_act, xdt, a_cum, a_cum_t, t)
    y_ssm = _rows(_f_ssdpost, t, [(yc, 1024, 0), (xbc_act, 1024, 0), (proj, 1024, 3)], [d_skip, w['e_ssm_norm'], expand],
                  [1024], [], "ssd_post", bf16_outs=(0,))[0]
    mix_e = jnp.concatenate([y_ret, y_ssm], axis=1)
    x1 = _mm(mix_e, w['e_w_out'], "nn", "e_out", res=x)
    x2, ffn0 = _ffn_fwd(x1, ffn_g[0], w['ffn_w_up'][0], ffn_dw_w[0], ffn_dw_b[0], w['ffn_w_down'][0], "0")

    lr, li = w['o_a_re'][0], w['o_a_im'][0]
    ls = w['o_log_step'].reshape(S5_GROUPS, 1)
    b_re3, b_im3 = jnp.transpose(w['o_b_re'][0], (2, 0, 1)), jnp.transpose(w['o_b_im'][0], (2, 0, 1))
    par_ins = [lr, li, ls, b_re3, b_im3]
    whole = lambda a: (a, a.shape, (lambda i, n=a.ndim: (0,) * n))
    par_shapes = [(S5_GROUPS, S5_STATE)] * 2 + [(S5_GROUP, S5_GROUPS, S5_STATE)] * 2
    ab_re, ab_im, bb_re, bb_im = _call(_f_s5par, (1,), [whole(a) for a in par_ins],
                                       [(s, F32, s, (lambda i, n=len(s): (0,) * n), False) for s in par_shapes], "s5_params")
    w_b = jnp.concatenate([_block_diag(jnp.transpose(bb_re, (1, 0, 2))), _block_diag(jnp.transpose(bb_im, (1, 0, 2)))], axis=1)
    w_c = jnp.concatenate([_block_diag(jnp.transpose(w['o_c_re'][0], (0, 2, 1))),
                           -_block_diag(jnp.transpose(w['o_c_im'][0], (0, 2, 1)))], axis=0)
    a_re, a_im = ab_re.reshape(1, S5_COLS), ab_im.reshape(1, S5_COLS)
    dw_w = w['o_dw_w'][0][:, None, :]
    glu_w = w['o_glu_w'].astype(F32)

    h1 = _rms_fwd(x2, mix_g[1], "mix_norm_1")
    proj_o = _mm(h1, w['o_w_in'], "nn", "o_in")
    c1 = _conf_conv(proj_o, dw_w, w['o_dw_b'])
    c2 = _rows(_f_confb, t, [(c1, 512, 0)], [w['o_ln_g'], w['o_ln_b']], [512], [], "conf_norm", bf16_outs=(0,))[0]
    u = proj_o[:, 1024:].astype(BF16)
    bu = _mm(u, w_b, "nn", "s5_bu")
    xs_re, xs_im, xs_re_bf, xs_im_bf = _s5_scan(bu, a_re, a_im, False)
    xs_cat = jnp.concatenate([xs_re, xs_im], axis=1)
    xs_cat_bf = jnp.concatenate([xs_re_bf, xs_im_bf], axis=1)
    y_s5 = _mm(xs_cat_bf, w_c, "nn", "s5_cx")
    s_out = _rows(_f_s5post, t, [(y_s5, 512, 0), (proj_o, 512, 2)], [w['o_d'], glu_w], [512], [], "s5_post", bf16_outs=(0,))[0]
    mix_o = jnp.concatenate([c2, s_out], axis=1)
    x3 = _mm(mix_o, w['o_w_out'], "nn", "o_out", res=x2)
    x4, ffn1 = _ffn_fwd(x3, ffn_g[1], w['ffn_w_up'][1], ffn_dw_w[1], ffn_dw_b[1], w['ffn_w_down'][1], "1")

    dx4, dx4_bf, loss_blk, d_final = _rows(_loss_step, t, [(x4, D_MODEL, 0), (tgt, D_MODEL, 0)], [w['final_norm'].reshape(1, D_MODEL)],
                                           [D_MODEL, D_MODEL], [(8, LANES), (1, D_MODEL)], "loss_head", bf16_outs=(1,))
    loss = loss_blk[0, 0]
    grads['final_norm'] = d_final.reshape(D_MODEL)

    dx3, dx3_bf, g1 = _ffn_bwd(x3, ffn_g[1], w['ffn_w_up'][1], ffn_dw_w[1], ffn_dw_b[1], w['ffn_w_down'][1], ffn1, dx4, dx4_bf, "1")
    grads['o_w_out'] = _mm(mix_o, dx3_bf, "tn", "o_out_dw", out_dtype=BF16)
    dmix_o = _mm(dx3_bf, w['o_w_out'], "nt", "o_out_dx")
    dc1, d_ln_g, d_ln_b = _conf_norm_bwd(c1, w['o_ln_g'], w['o_ln_b'], dmix_o, t)
    dca, dcg, d_dw_w, d_dw_b = _conf_conv(proj_o, dw_w, w['o_dw_b'], dc1)
    dyc, du_skip, d_od, d_glu = _s5_post_bwd(y_s5, proj_o, w['o_d'], glu_w, dmix_o, t)
    grads['o_glu_w'] = d_glu
    d_w_c = _mm(xs_cat_bf, dyc, "tn", "s5_cx_dw")
    dxs = _mm(dyc, w_c, "nt", "s5_cx_dx")
    g_re, g_im, d_are, d_aim = _s5_scan(dxs, a_re, a_im, True, xs_cat)
    g_cat = jnp.concatenate([g_re, g_im], axis=1)
    d_w_b = _mm(u, g_cat, "tn", "s5_bu_dw")
    du = _mm(g_cat, w_b, "nt", "s5_bu_dx", res=du_skip, out_dtype=BF16)
    d_bb_re = jnp.transpose(_diag_blocks(d_w_b[:, :S5_COLS], S5_GROUPS), (1, 0, 2))
    d_bb_im = jnp.transpose(_diag_blocks(d_w_b[:, S5_COLS:], S5_GROUPS), (1, 0, 2))
    par_cts = [d_are.reshape(S5_GROUPS, S5_STATE), d_aim.reshape(S5_GROUPS, S5_STATE), d_bb_re, d_bb_im]
    in_shapes = [a.shape for a in par_ins]
    d_lr, d_li, d_ls, d_br3, d_bi3 = _call(_grad_fn(_f_s5par, 5, 5), (1,), [whole(a) for a in par_ins + par_cts],
                                           [(s, F32, s, (lambda i, n=len(s): (0,) * n), False) for s in in_shapes], "s5_params_bwd")
    grads['o_a_re'], grads['o_a_im'], grads['o_log_step'] = d_lr[None], d_li[None], d_ls.reshape(1, S5_GROUPS)
    grads['o_b_re'], grads['o_b_im'] = jnp.transpose(d_br3, (1, 2, 0))[None], jnp.transpose(d_bi3, (1, 2, 0))[None]
    grads['o_c_re'] = jnp.transpose(_diag_blocks(d_w_c[:S5_COLS], S5_GROUPS), (0, 2, 1))[None]
    grads['o_c_im'] = -jnp.transpose(_diag_blocks(d_w_c[S5_COLS:], S5_GROUPS), (0, 2, 1))[None]
    grads['o_d'], grads['o_ln_g'], grads['o_ln_b'] = d_od, d_ln_g, d_ln_b
    grads['o_dw_w'], grads['o_dw_b'] = d_dw_w[:, 0][None], d_dw_b
    dproj_o = jnp.concatenate([dca, dcg, du], axis=1)
    grads['o_w_in'] = _mm(h1, dproj_o, "tn", "o_in_dw")
    dh1 = _mm(dproj_o, w['o_w_in'], "nt", "o_in_dx")
    dx2, dx2_bf, d_mix1 = _rms_bwd(x2, mix_g[1], dh1, dx3, "mix_norm_bwd_1")

    dx1, dx1_bf, g0 = _ffn_bwd(x1, ffn_g[0], w['ffn_w_up'][0], ffn_dw_w[0], ffn_dw_b[0], w['ffn_w_down'][0], ffn0, dx2, dx2_bf, "0")
    grads['e_w_out'] = _mm(mix_e, dx1_bf, "tn", "e_out_dw", out_dtype=BF16)
    dmix_e = _mm(dx1_bf, w['e_w_out'], "nt", "e_out_dx")
    dr, dg = _rows(_grad_fn(_f_retpost, 2, 2), t, [(r, 1024, 0), (proj, 1024, 2), (dmix_e, 1024, 0)], [], [1024, 1024], [], "ret_post_bwd",
                   bf16_outs=(0, 1))
    dqr, dkr, dv = _ret_bwd(qr, kr, proj, dr, t)
    dq, dk = _ret_pre_bwd(proj, cosf, sins, dqr, dkr, t)
    dyc0, dxs1, dz, d_dskip, d_ssm_norm = _ssd_post_bwd(yc, xbc_act, proj, d_skip, w['e_ssm_norm'], expand, dmix_e, t)
    dcm, da_q, dbm, dxdt, da_k, da_k_t = _ssd_bwd(xbc_act, xdt, a_cum, a_cum_t, dyc0, expand, t)
    d_a_cum = da_q - da_k - jnp.pad(da_k_t.T, ((0, 0), (0, LANES - SSM_HEADS)))
    dda = _cumsum(d_a_cum, True, "ssd_cumsum_bwd")
    dxs, ddtr, d_dt_bias, d_a_log = _ssd_pre_bwd(xbc_act, proj, dt_bias, a_log, expand, dxdt, dda, dxs1, t)
    dxbc_act = jnp.concatenate([dxs, dbm, dcm], axis=1)
    dxbc, d_conv_w, d_conv_b = _cols(_grad_fn(_f_ssdconv, 3, 3), 12, [(proj, 32), (conv_w, 0), (conv_b, 0), (dxbc_act, 0)],
                                     [(t,), (4, 1), (1,)], "ssd_conv_bwd", bf16_outs=(0,))
    dproj = jnp.concatenate([dq, dk, dv, dg, dz, dxbc, ddtr], axis=1)
    grads['e_w_in'] = _mm(h0, dproj, "tn", "e_in_dw")
    dh0 = _mm(dproj, w_e, "nt", "e_in_dx")
    dx0, _, d_mix0 = _rms_bwd(x, mix_g[0], dh0, dx1, "mix_norm_bwd_0")

    grads['mix_norm'] = jnp.concatenate([d_mix0, d_mix1], axis=0)
    grads['e_conv_w'], grads['e_conv_b'] = d_conv_w[:, 0][None], d_conv_b
    grads['e_dt_bias'], grads['e_a_log'], grads['e_d'] = d_dt_bias[:, :SSM_HEADS], d_a_log[:, :SSM_HEADS], d_dskip[:, :SSM_HEADS]
    grads['e_ssm_norm'] = d_ssm_norm
    grads['ffn_norm'] = jnp.concatenate([g0['norm'], g1['norm']], axis=0)
    grads['ffn_w_up'], grads['ffn_w_down'] = [g0['w_up'], g1['w_up']], [g0['w_down'], g1['w_down']]
    grads['ffn_dw_w'] = jnp.stack([g0['dw_w'], g1['dw_w']], axis=0)
    grads['ffn_dw_b'] = jnp.concatenate([g0['dw_b'], g1['dw_b']], axis=0)
    return loss, dx0, grads


def _conf_norm_bwd(c1, ln_g, ln_b, dmix_o, t):
    def fn(c1_, dy_, g_, b_):
        _, vjp = jax.vjp(lambda a, b, c: _f_confb(a, b, c)[0], c1_, g_, b_)
        return vjp(dy_)
    return _rows(fn, t, [(c1, 512, 0), (dmix_o, 512, 0)], [ln_g, ln_b], [512], [(1, 512), (1, 512)], "conf_norm_bwd")


def _s5_post_bwd(y_s5, proj_o, d_skip, glu_w, dmix_o, t):
    def fn(yc_, u_, dy_, d_, gw_):
        _, vjp = jax.vjp(lambda a, b, c, e: _f_s5post(a, b, c, e)[0], yc_, u_, d_, gw_)
        return vjp(dy_)
    return _rows(fn, t, [(y_s5, 512, 0), (proj_o, 512, 2), (dmix_o, 512, 1)], [d_skip, glu_w], [512, 512], [(1, 512), (512, 512)],
                 "s5_post_bwd", bf16_outs=(0,))


def _ret_pre_bwd(proj, cosf, sins, dqr, dkr, t):
    def fn(q_, k_, cos_, sin_, dq_, dk_):
        _, vjp = jax.vjp(lambda a, b: _f_retpre(a, b, cos_, sin_), q_, k_)
        return vjp((dq_, dk_))
    return _rows(fn, t, [(proj, 512, 0), (proj, 512, 1), (cosf, LANES, 0), (sins, LANES, 0), (dqr, 512, 0), (dkr, 512, 0)], [],
                 [512, 512], [], "ret_pre_bwd", bf16_outs=(0, 1))


def _ssd_post_bwd(yc, xbc_act, proj, d_skip, norm_w, expand, dmix_e, t):
    def fn(yc_, xs_, z_, dy_, d_, nw_, e_):
        _, vjp = jax.vjp(lambda a, b, c, dd, n: _f_ssdpost(a, b, c, dd, n, e_)[0], yc_, xs_, z_, d_, nw_)
        return vjp(dy_)
    return _rows(fn, t, [(yc, 1024, 0), (xbc_act, 1024, 0), (proj, 1024, 3), (dmix_e, 1024, 1)], [d_skip, norm_w, expand],
                 [1024, 1024, 1024], [(1, LANES), (1, 1024)], "ssd_post_bwd", bf16_outs=(0, 2))


def _ssd_pre_bwd(xbc_act, proj, dt_bias, a_log, expand, dxdt, dda, dxs1, t):
    def fn(xs_, dtr_, dx_, dda_, dxs1_, bias_, alog_, e_):
        _, vjp = jax.vjp(lambda a, b, c, dd: _f_ssdpre(a, b, c, dd, e_), xs_, dtr_, bias_, alog_)
        dxs, ddtr, dbias, dalog = vjp((dx_, dda_))
        return dxs + dxs1_, ddtr, dbias, dalog
    return _rows(fn, t, [(xbc_act, 1024, 0), (proj, LANES, 44), (dxdt, 1024, 0), (dda, LANES, 0), (dxs1, 1024, 0)],
                 [dt_bias, a_log, expand], [1024, LANES], [(1, LANES), (1, LANES)], "ssd_pre_bwd", bf16_outs=(1,))


def kernel(x, mix_norm, e_w_in, e_conv_w, e_conv_b, e_dt_bias, e_a_log, e_d, e_ssm_norm, e_w_out, o_w_in, o_dw_w, o_dw_b, o_ln_g, o_ln_b, o_a_re, o_a_im, o_b_re, o_b_im, o_c_re, o_c_im, o_d, o_log_step, o_glu_w, o_w_out, ffn_norm, ffn_w_up, ffn_dw_w, ffn_dw_b, ffn_w_down, final_norm, loss_target, m_mix_norm, m_e_w_in, m_e_conv_w, m_e_conv_b, m_e_dt_bias, m_e_a_log, m_e_d, m_e_ssm_norm, m_e_w_out, m_o_w_in, m_o_dw_w, m_o_dw_b, m_o_ln_g, m_o_ln_b, m_o_a_re, m_o_a_im, m_o_b_re, m_o_b_im, m_o_c_re, m_o_c_im, m_o_d, m_o_log_step, m_o_glu_w, m_o_w_out, m_ffn_norm, m_ffn_w_up, m_ffn_dw_w, m_ffn_dw_b, m_ffn_w_down, m_final_norm, v_mix_norm, v_e_w_in, v_e_conv_w, v_e_conv_b, v_e_dt_bias, v_e_a_log, v_e_d, v_e_ssm_norm, v_e_w_out, v_o_w_in, v_o_dw_w, v_o_dw_b, v_o_ln_g, v_o_ln_b, v_o_a_re, v_o_a_im, v_o_b_re, v_o_b_im, v_o_c_re, v_o_c_im, v_o_d, v_o_log_step, v_o_glu_w, v_o_w_out, v_ffn_norm, v_ffn_w_up, v_ffn_dw_w, v_ffn_dw_b, v_ffn_w_down, v_final_norm):
    p = dict(locals())

    kinds = ("grad_", "delta_", "new_m_", "new_v_")

    def block(name, layer):
        return p[name][0 if layer is None else layer]

    srcs = [block(n, layer).astype(BF16) for n, layer, _ in MATMUL_BLOCKS] + [_pack([p[n] for n in SMALL_SHARDED], F32, 16)]
    gathered = _gather(srcs, "gather_weights")
    w = {n: p[n] for n in REPLICATED}
    for (n, layer, by_cols), g in zip(MATMUL_BLOCKS, gathered):
        if by_cols:
            full = _join_cols(g, E_IN_PAD if n == 'e_w_in' else N_DEV * g.shape[2], f"join_{n}_{layer}")
        else:
            full = g.reshape(N_DEV * g.shape[1], g.shape[2])
        if layer is None:
            w[n] = full
        else:
            w.setdefault(n, [None, None])[layer] = full
    for n, piece in zip(SMALL_SHARDED, _unpack(gathered[-1], [p[n].shape for n in SMALL_SHARDED], lead=(N_DEV,))):
        w[n] = _join_shards(piece, SHARDED[n])

    loss, dx, grads = _local_step(x[0], loss_target[0], w)
    loss = lax.psum(loss, MESH_AXES)

    sends = []
    for n, layer, by_cols in MATMUL_BLOCKS:
        g = grads[n] if layer is None else grads[n][layer]
        if by_cols:
            sends.append(_split_cols(g, block(n, layer).shape[1], BF16, f"split_{n}_{layer}"))
        else:
            sends.append(g.astype(BF16).reshape(N_DEV, -1, g.shape[1]))
    sends.append(_pack([_split_shards(grads[n].reshape(p[n].shape[:SHARDED[n]] + (-1,) + p[n].shape[SHARDED[n] + 1:]), SHARDED[n])
                        for n in SMALL_SHARDED], F32, 128, lead=(N_DEV,)))
    core = lax.axis_index("c")
    by_core = [s.reshape((N_CHIPS, 2) + s.shape[1:]) for s in sends]
    keep = [lax.dynamic_index_in_dim(s, core, axis=1, keepdims=False) for s in by_core]
    give = [lax.dynamic_index_in_dim(s, 1 - core, axis=1, keepdims=False) for s in by_core]
    got = _swap_sibling(give, "swap_sibling_grads")
    chip_sums = [_add(a, b, f"chip_sum_{i}") for i, (a, b) in enumerate(zip(keep, got))]
    parts = list(_exchange_chips(chip_sums, "exchange_chip_grads"))
    parts.append(_gather([_pack([grads[n].reshape(p[n].shape) for n in REPLICATED], F32, 128)], "gather_replicated_grads")[0])

    out, by_layer = {}, {}
    for (n, layer, _), part in zip(MATMUL_BLOCKS, parts):
        by_layer.setdefault(n, {})[layer] = _adamw(part, *[block(pre + n, layer) for pre in ("", "m_", "v_")], f"adamw_{n}_{layer}")
    for n, res in by_layer.items():
        for i, kind in enumerate(kinds):
            out[kind + n] = res[None][i][None] if None in res else jnp.stack([res[0][i], res[1][i]], axis=0)
    for names, part, tag in ((SMALL_SHARDED, parts[-2], "small"), (REPLICATED, parts[-1], "replicated")):
        packed = [_pack([p[pre + n] for n in names], F32, 128) for pre in ("", "m_", "v_")]
        for kind, buf in zip(kinds, _adamw(part, *packed, "adamw_" + tag)):
            for n, a in zip(names, _unpack(buf, [p[n].shape for n in names])):
                out[kind + n] = a
    return (loss, dx[None], *[out[kind + n] for kind in kinds for n in WEIGHTS])
```

```python
import functools
import math

import jax
import jax.numpy as jnp
from jax import lax
from jax.experimental import pallas as pl
from jax.experimental.pallas import tpu as pltpu

F32, BF16 = jnp.float32, jnp.bfloat16
HIGHEST = lax.Precision.HIGHEST
N_DEV = 8
MESH_AXES = ("x", "y", "c")
VMEM_LIMIT = 48 * 1024 * 1024
LANES = 128
PACK_COLS = 1024

D_MODEL = 1024
EPS = 1e-6
RET_HEADS, RET_DK, RET_DV = 4, 128, 256
SSM_HEADS, SSM_P, SSM_N, SSM_GROUPS = 16, 64, 128, 2
SSM_HG = SSM_HEADS // SSM_GROUPS
S5_GROUPS, S5_GROUP, S5_STATE = 32, 16, 64
S5_COLS = S5_GROUPS * S5_STATE
D_FF = 2816
E_IN, E_IN_PAD = 5648, 5760
ADAM_LR, ADAM_B1, ADAM_B2, ADAM_EPS, ADAM_WD, ADAM_STEP = 0.001, 0.9, 0.999, 1e-08, 0.01, 10

WEIGHTS = ['mix_norm', 'e_w_in', 'e_conv_w', 'e_conv_b', 'e_dt_bias', 'e_a_log', 'e_d', 'e_ssm_norm', 'e_w_out', 'o_w_in', 'o_dw_w', 'o_dw_b', 'o_ln_g', 'o_ln_b', 'o_a_re', 'o_a_im', 'o_b_re', 'o_b_im', 'o_c_re', 'o_c_im', 'o_d', 'o_log_step', 'o_glu_w', 'o_w_out', 'ffn_norm', 'ffn_w_up', 'ffn_dw_w', 'ffn_dw_b', 'ffn_w_down', 'final_norm']
SHARDED = {'e_w_in': 2, 'e_conv_w': 2, 'e_w_out': 1, 'o_w_in': 2, 'o_dw_w': 2, 'o_dw_b': 1, 'o_ln_g': 1, 'o_ln_b': 1,
           'o_d': 1, 'o_glu_w': 1, 'o_w_out': 1, 'ffn_w_up': 2, 'ffn_dw_w': 2, 'ffn_w_down': 1}
MATMUL_WEIGHTS = ['e_w_in', 'e_w_out', 'o_w_in', 'o_glu_w', 'o_w_out', 'ffn_w_up', 'ffn_w_down']
MATMUL_BLOCKS = [('e_w_in', None, True), ('e_w_out', None, False), ('o_w_in', None, True), ('o_glu_w', None, False),
                 ('o_w_out', None, False), ('ffn_w_up', 0, True), ('ffn_w_up', 1, True), ('ffn_w_down', 0, False), ('ffn_w_down', 1, False)]
SMALL_SHARDED = [n for n in WEIGHTS if n in SHARDED and n not in MATMUL_WEIGHTS]
REPLICATED = [n for n in WEIGHTS if n not in SHARDED]


def _call(fn, grid, ins, outs, name):
    n_in = len(ins)

    def body(*refs):
        vals = fn(*[r[...] for r in refs[:n_in]])
        first = pl.program_id(0) == 0
        for r, v, o in zip(refs[n_in:], vals, outs):
            if o[4]:
                @pl.when(first)
                def _():
                    r[...] = jnp.zeros_like(r)
                r[...] += v.astype(r.dtype)
            else:
                r[...] = v.astype(r.dtype)

    return pl.pallas_call(
        body, grid=grid,
        in_specs=[pl.BlockSpec(b, m) for _, b, m in ins],
        out_specs=[pl.BlockSpec(o[2], o[3]) for o in outs],
        out_shape=[jax.ShapeDtypeStruct(o[0], o[1]) for o in outs],
        compiler_params=pltpu.CompilerParams(dimension_semantics=("arbitrary",) * len(grid), vmem_limit_bytes=VMEM_LIMIT),
        name=name)(*[a for a, _, _ in ins])


def _rows(fn, n_rows, row_ins, full_ins, row_outs, acc_outs, name, tm=256, bf16_outs=()):
    tm = min(tm, n_rows)
    ins = [(a, (tm, w), (lambda i, c=c: (i, c))) for a, w, c in row_ins]
    ins += [(a, a.shape, (lambda i, n=a.ndim: (0,) * n)) for a in full_ins]
    outs = [((n_rows, w), BF16 if k in bf16_outs else F32, (tm, w), (lambda i: (i, 0)), False) for k, w in enumerate(row_outs)]
    outs += [(tuple(s), F32, tuple(s), (lambda i, n=len(s): (0,) * n), True) for s in acc_outs]
    return _call(fn, (n_rows // tm,), ins, outs, name)


def _cols(fn, n_blocks, col_ins, out_leads, name, cb=LANES, bf16_outs=()):
    ins = [(a, a.shape[:-1] + (cb,), (lambda j, n=a.ndim, o=o: (0,) * (n - 1) + (j + o,))) for a, o in col_ins]
    outs = [(tuple(s) + (n_blocks * cb,), BF16 if k in bf16_outs else F32, tuple(s) + (cb,), (lambda j, n=len(s): (0,) * n + (j,)), False)
            for k, s in enumerate(out_leads)]
    return _call(fn, (n_blocks,), ins, outs, name)


def _grad_fn(f, n_diff, n_in):
    def g(*a):
        diff, consts, cts = a[:n_diff], a[n_diff:n_in], a[n_in:]
        _, vjp = jax.vjp(lambda *d: f(*d, *consts), *diff)
        return vjp(tuple(cts))
    return g


def _silu(x):
    return x * jax.nn.sigmoid(x)


def _rms(x, g):
    return x * lax.rsqrt(jnp.mean(x * x, axis=-1, keepdims=True) + EPS) * g


@jax.custom_vjp
def _softplus(x):
    return jnp.maximum(x, 0.0) + jnp.log(1.0 + jnp.exp(-jnp.abs(x)))


_softplus.defvjp(lambda x: (_softplus(x), x), lambda x, g: (g * jax.nn.sigmoid(x),))


@jax.custom_vjp
def _swap_halves(x):
    return pltpu.roll(x, 64, 1)


_swap_halves.defvjp(lambda x: (_swap_halves(x), None), lambda _, g: (_swap_halves(g),))


def _shift_rows(x, k, up):
    if k == 0:
        return x
    n = x.shape[0]
    t = lax.broadcasted_iota(jnp.int32, x.shape, 0)
    if up:
        return jnp.where(t < n - k, pltpu.roll(x, n - k, 0), 0.0)
    return jnp.where(t >= k, pltpu.roll(x, k, 0), 0.0)


@jax.custom_vjp
def _dwconv(x, w, b):
    k_taps = w.shape[0]
    y = b + w[k_taps - 1] * x
    for k in range(k_taps - 1):
        y = y + w[k] * _shift_rows(x, k_taps - 1 - k, False)
    return y


def _dwconv_fwd(x, w, b):
    return _dwconv(x, w, b), (x, w)


def _dwconv_bwd(saved, dy):
    x, w = saved
    k_taps = w.shape[0]
    dx = w[k_taps - 1] * dy
    dws = []
    for k in range(k_taps - 1):
        s = k_taps - 1 - k
        dx = dx + w[k] * _shift_rows(dy, s, True)
        dws.append(jnp.sum(dy * _shift_rows(x, s, False), axis=0, keepdims=True)[None])
    dws.append(jnp.sum(dy * x, axis=0, keepdims=True)[None])
    return dx, jnp.concatenate(dws, axis=0), jnp.sum(dy, axis=0, keepdims=True)


_dwconv.defvjp(_dwconv_fwd, _dwconv_bwd)


def _f_rms(x, g):
    return (_rms(x, g),)


def _rot(x, cosf, sins):
    outs = []
    for h in range(RET_HEADS):
        xh = x[:, h * RET_DK:(h + 1) * RET_DK]
        outs.append(xh * cosf + _swap_halves(xh) * sins)
    return jnp.concatenate(outs, axis=1)


def _f_retpre(q, k, cosf, sins):
    return _rot(q, cosf, sins), _rot(k, cosf, sins) * (RET_DK ** -0.5)


def _f_retpost(r, g):
    outs = []
    for h in range(RET_HEADS):
        rh = r[:, h * RET_DV:(h + 1) * RET_DV]
        rc = rh - jnp.mean(rh, axis=-1, keepdims=True)
        outs.append(_silu(g[:, h * RET_DV:(h + 1) * RET_DV]) * (rc * lax.rsqrt(jnp.mean(rc * rc, axis=-1, keepdims=True) + EPS)))
    return (jnp.concatenate(outs, axis=1),)


def _f_ssdconv(xbc, w, b):
    return (_silu(_dwconv(xbc, w, b)),)


def _f_ssdpre(xs, dtr, bias, alog, expand):
    dt = _softplus(dtr + bias)
    return xs * jnp.dot(dt, expand, precision=HIGHEST, preferred_element_type=F32), dt * (-jnp.exp(alog))


def _f_ssdpost(yc, xs, z, dskip, norm_w, expand):
    d_wide = jnp.dot(jnp.broadcast_to(dskip, (yc.shape[0], LANES)), expand, precision=HIGHEST, preferred_element_type=F32)
    y = (yc + d_wide * xs) * _silu(z)
    half = y.shape[1] // SSM_GROUPS
    outs = []
    for g in range(SSM_GROUPS):
        yg = y[:, g * half:(g + 1) * half]
        outs.append(yg * lax.rsqrt(jnp.mean(yg * yg, axis=-1, keepdims=True) + EPS))
    return (jnp.concatenate(outs, axis=1) * norm_w,)


def _f_ffnmid(gin, uin, wg, wu, bg, bu):
    return (_silu(_dwconv(gin, wg, bg)) * _dwconv(uin, wu, bu),)


def _f_confb(c1, g, b):
    mu = jnp.mean(c1, axis=-1, keepdims=True)
    xc = c1 - mu
    return (_silu(xc * lax.rsqrt(jnp.mean(xc * xc, axis=-1, keepdims=True) + EPS) * g + b),)


def _f_s5post(yc, u, dskip, glu_w):
    s = jax.nn.gelu(yc + dskip * u)
    z = jnp.dot(s.astype(BF16), glu_w.astype(BF16), preferred_element_type=F32)
    return (s * jax.nn.sigmoid(z),)


def _f_s5par(lr, li, ls, br, bi):
    step = jnp.exp(ls)
    mag = jnp.exp(lr * step)
    ab_re = mag * jnp.cos(li * step)
    ab_im = mag * jnp.sin(li * step)
    den = lr * lr + li * li
    f_re = ((ab_re - 1.0) * lr + ab_im * li) / den
    f_im = (ab_im * lr - (ab_re - 1.0) * li) / den
    return ab_re, ab_im, f_re[None] * br - f_im[None] * bi, f_re[None] * bi + f_im[None] * br


def _loss_step(x, tgt, g):
    def f(x_, g_):
        e = _rms(x_, g_) - tgt
        return 0.5 * jnp.sum(jnp.mean(e * e, axis=-1, keepdims=True), axis=0, keepdims=True)
    loss, vjp = jax.vjp(f, x, g)
    dx, dg = vjp(jnp.ones((1, 1), F32))
    return dx, dx, jnp.broadcast_to(loss, (8, LANES)), dg


def _tile(n, pref):
    if n <= pref:
        return n
    t = (pref // LANES) * LANES
    while n % t:
        t -= LANES
    return t


_DIMS = {"nn": (((1,), (0,)), ((), ())), "nt": (((1,), (1,)), ((), ())), "tn": (((0,), (0,)), ((), ()))}


def _mm(a, b, mode, name, res=None, out_dtype=F32, tm=1024, tn=1408, tk=1408):
    if mode == "nn":
        (m, k), n = a.shape, b.shape[1]
    elif mode == "nt":
        (m, k), n = a.shape, b.shape[0]
    else:
        (k, m), n = a.shape, b.shape[1]
    tm, tn, tk = _tile(m, tm), _tile(n, tn), _tile(k, tk)
    nk = k // tk
    a_spec = pl.BlockSpec((tk, tm), lambda i, j, kk: (kk, i)) if mode == "tn" else pl.BlockSpec((tm, tk), lambda i, j, kk: (i, kk))
    b_spec = pl.BlockSpec((tn, tk), lambda i, j, kk: (j, kk)) if mode == "nt" else pl.BlockSpec((tk, tn), lambda i, j, kk: (kk, j))
    o_spec = pl.BlockSpec((tm, tn), lambda i, j, kk: (i, j))
    has_res = res is not None

    def body(*refs):
        a_ref, b_ref = refs[0], refs[1]
        o_ref, acc = refs[-2], refs[-1]
        kk = pl.program_id(2)

        @pl.when(kk == 0)
        def _():
            acc[...] = jnp.zeros_like(acc)

        acc[...] += lax.dot_general(a_ref[...].astype(BF16), b_ref[...].astype(BF16), _DIMS[mode], preferred_element_type=F32)

        @pl.when(kk == nk - 1)
        def _():
            o_ref[...] = (acc[...] + refs[2][...] if has_res else acc[...]).astype(out_dtype)

    return pl.pallas_call(
        body, grid=(m // tm, n // tn, nk),
        in_specs=[a_spec, b_spec] + ([o_spec] if has_res else []),
        out_specs=o_spec, out_shape=jax.ShapeDtypeStruct((m, n), out_dtype),
        scratch_shapes=[pltpu.VMEM((tm, tn), F32)],
        compiler_params=pltpu.CompilerParams(dimension_semantics=("parallel", "parallel", "arbitrary"), vmem_limit_bytes=VMEM_LIMIT),
        name=name)(*([a, b] + ([res] if has_res else [])))


def _seq_block(t):
    return min(512, t)


def _causal_diff(i, j, blk):
    r = lax.broadcasted_iota(jnp.int32, (blk, blk), 0)
    c = lax.broadcasted_iota(jnp.int32, (blk, blk), 1)
    return (i - j) * blk + r - c


def _ret_decay(lg, i, j, blk):
    diff = _causal_diff(i, j, blk)
    return jnp.where(diff >= 0, jnp.exp(lg * jnp.maximum(diff, 0).astype(F32)), 0.0)


def _pair_call(body, lead_grid, nb, key_major, in_specs, out_specs, out_shape, scratch, name, args):
    pairs = [(i, j) for j in range(nb) for i in range(j, nb)] if key_major else [(i, j) for i in range(nb) for j in range(i + 1)]
    tables = [jnp.array([p[k] for p in pairs], jnp.int32) for k in (0, 1)]
    lead = len(lead_grid)

    def on_pairs(spec):
        if spec.block_shape is None:
            return spec

        def index_map(*a):
            i, j = a[lead + 1][a[lead]], a[lead + 2][a[lead]]
            return spec.index_map(*a[:lead], *((j, i) if key_major else (i, j)))

        return pl.BlockSpec(spec.block_shape, index_map)

    def wrapped(i_ref, j_ref, *refs):
        p = pl.program_id(lead)
        body(i_ref[p], j_ref[p], *refs)

    many = isinstance(out_specs, (list, tuple))
    grid_spec = pltpu.PrefetchScalarGridSpec(
        num_scalar_prefetch=2, grid=tuple(lead_grid) + (len(pairs),), in_specs=[on_pairs(s) for s in in_specs],
        out_specs=[on_pairs(s) for s in out_specs] if many else on_pairs(out_specs), scratch_shapes=scratch)
    return pl.pallas_call(
        wrapped, grid_spec=grid_spec, out_shape=out_shape,
        compiler_params=pltpu.CompilerParams(dimension_semantics=("arbitrary",) * (lead + 1), vmem_limit_bytes=VMEM_LIMIT),
        name=name)(*tables, *args)


def _ret_row_decays(lg, i, j, blk):
    row = lax.broadcasted_iota(jnp.int32, (blk, RET_DK), 0)
    return jnp.exp(lg * row.astype(F32)), jnp.exp(lg * ((i - j) * blk - row).astype(F32))


def _ret_scaled(lg, i, j, blk, q_ref, k_ref):
    a, b = _ret_row_decays(lg, i, j, blk)
    return (q_ref[...] * a).astype(BF16), (k_ref[...] * b).astype(BF16)


def _ret_log_gamma():
    return jnp.log1p(-(2.0 ** (-5.0 - jnp.arange(RET_HEADS, dtype=F32))))


def _ret_fwd(qr, kr, proj, t):
    blk = _seq_block(t)
    nb = t // blk
    v_off = (2 * RET_HEADS * RET_DK) // RET_DV

    def body(i, j, lg_ref, q_ref, k_ref, v_ref, o_ref, acc):
        h = pl.program_id(0)

        @pl.when(j == 0)
        def _():
            acc[...] = jnp.zeros_like(acc)

        @pl.when(j < i)
        def _():
            qa, kb = _ret_scaled(lg_ref[h], i, j, blk, q_ref, k_ref)
            p = lax.dot_general(qa, kb, _DIMS["nt"], preferred_element_type=F32).astype(BF16)
            acc[...] += jnp.dot(p, v_ref[...].astype(BF16), preferred_element_type=F32)

        @pl.when(j == i)
        def _():
            s = lax.dot_general(q_ref[...].astype(BF16), k_ref[...].astype(BF16), _DIMS["nt"], preferred_element_type=F32)
            p = (s * _ret_decay(lg_ref[h], i, j, blk)).astype(BF16)
            o_ref[...] = acc[...] + jnp.dot(p, v_ref[...].astype(BF16), preferred_element_type=F32)

    return _pair_call(
        body, (RET_HEADS,), nb, False,
        [pl.BlockSpec(memory_space=pltpu.SMEM),
         pl.BlockSpec((blk, RET_DK), lambda h, i, j: (i, h)),
         pl.BlockSpec((blk, RET_DK), lambda h, i, j: (j, h)),
         pl.BlockSpec((blk, RET_DV), lambda h, i, j: (j, v_off + h))],
        pl.BlockSpec((blk, RET_DV), lambda h, i, j: (i, h)),
        jax.ShapeDtypeStruct((t, RET_HEADS * RET_DV), F32), [pltpu.VMEM((blk, RET_DV), F32)],
        "ret_fwd", (_ret_log_gamma(), qr, kr, proj))


def _ret_bwd(qr, kr, proj, dr, t):
    blk = _seq_block(t)
    nb = t // blk
    v_off = (2 * RET_HEADS * RET_DK) // RET_DV

    def dq_body(i, j, lg_ref, q_ref, k_ref, v_ref, do_ref, dq_ref, acc):
        h = pl.program_id(0)

        @pl.when(j == 0)
        def _():
            acc[...] = jnp.zeros_like(acc)

        @pl.when(j < i)
        def _():
            a, b = _ret_row_decays(lg_ref[h], i, j, blk)
            ds = lax.dot_general(do_ref[...].astype(BF16), v_ref[...].astype(BF16), _DIMS["nt"], preferred_element_type=F32)
            acc[...] += a * jnp.dot(ds.astype(BF16), (k_ref[...] * b).astype(BF16), preferred_element_type=F32)

        @pl.when(j == i)
        def _():
            ds = lax.dot_general(do_ref[...].astype(BF16), v_ref[...].astype(BF16), _DIMS["nt"], preferred_element_type=F32)
            dsm = (ds * _ret_decay(lg_ref[h], i, j, blk)).astype(BF16)
            dq_ref[...] = acc[...] + jnp.dot(dsm, k_ref[...].astype(BF16), preferred_element_type=F32)

    dq = _pair_call(
        dq_body, (RET_HEADS,), nb, False,
        [pl.BlockSpec(memory_space=pltpu.SMEM),
         pl.BlockSpec((blk, RET_DK), lambda h, i, j: (i, h)),
         pl.BlockSpec((blk, RET_DK), lambda h, i, j: (j, h)),
         pl.BlockSpec((blk, RET_DV), lambda h, i, j: (j, v_off + h)),
         pl.BlockSpec((blk, RET_DV), lambda h, i, j: (i, h))],
        pl.BlockSpec((blk, RET_DK), lambda h, i, j: (i, h)),
        jax.ShapeDtypeStruct((t, RET_HEADS * RET_DK), F32), [pltpu.VMEM((blk, RET_DK), F32)],
        "ret_bwd_dq", (_ret_log_gamma(), qr, kr, proj, dr))

    def dkv_body(i, j, lg_ref, q_ref, k_ref, v_ref, do_ref, dk_ref, dv_ref, acc_k, acc_v):
        h = pl.program_id(0)

        @pl.when(i == j)
        def _():
            acc_k[...] = jnp.zeros_like(acc_k)
            acc_v[...] = jnp.zeros_like(acc_v)

        @pl.when(i > j)
        def _():
            a, b = _ret_row_decays(lg_ref[h], i, j, blk)
            qa, kb = (q_ref[...] * a).astype(BF16), (k_ref[...] * b).astype(BF16)
            do = do_ref[...].astype(BF16)
            p = lax.dot_general(qa, kb, _DIMS["nt"], preferred_element_type=F32).astype(BF16)
            acc_v[...] += lax.dot_general(p, do, _DIMS["tn"], preferred_element_type=F32)
            ds = lax.dot_general(do, v_ref[...].astype(BF16), _DIMS["nt"], preferred_element_type=F32).astype(BF16)
            acc_k[...] += b * lax.dot_general(ds, qa, _DIMS["tn"], preferred_element_type=F32)

        @pl.when(i == j)
        def _():
            q = q_ref[...].astype(BF16)
            do = do_ref[...].astype(BF16)
            decay = _ret_decay(lg_ref[h], i, j, blk)
            s = lax.dot_general(q, k_ref[...].astype(BF16), _DIMS["nt"], preferred_element_type=F32)
            acc_v[...] += lax.dot_general((s * decay).astype(BF16), do, _DIMS["tn"], preferred_element_type=F32)
            ds = lax.dot_general(do, v_ref[...].astype(BF16), _DIMS["nt"], preferred_element_type=F32)
            acc_k[...] += lax.dot_general((ds * decay).astype(BF16), q, _DIMS["tn"], preferred_element_type=F32)

        @pl.when(i == nb - 1)
        def _():
            dk_ref[...] = acc_k[...]
            dv_ref[...] = acc_v[...].astype(BF16)

    dk, dv = _pair_call(
        dkv_body, (RET_HEADS,), nb, True,
        [pl.BlockSpec(memory_space=pltpu.SMEM),
         pl.BlockSpec((blk, RET_DK), lambda h, j, i: (i, h)),
         pl.BlockSpec((blk, RET_DK), lambda h, j, i: (j, h)),
         pl.BlockSpec((blk, RET_DV), lambda h, j, i: (j, v_off + h)),
         pl.BlockSpec((blk, RET_DV), lambda h, j, i: (i, h))],
        [pl.BlockSpec((blk, RET_DK), lambda h, j, i: (j, h)), pl.BlockSpec((blk, RET_DV), lambda h, j, i: (j, h))],
        [jax.ShapeDtypeStruct((t, RET_HEADS * RET_DK), F32), jax.ShapeDtypeStruct((t, RET_HEADS * RET_DV), BF16)],
        [pltpu.VMEM((blk, RET_DK), F32), pltpu.VMEM((blk, RET_DV), F32)],
        "ret_bwd_dkv", (_ret_log_gamma(), qr, kr, proj, dr))
    return dq, dk, dv


def _cumsum(x, reverse, name):
    t = x.shape[0]
    blk = _seq_block(t)
    nb = t // blk

    def body(x_ref, o_ref, carry):
        @pl.when(pl.program_id(0) == 0)
        def _():
            carry[...] = jnp.zeros_like(carry)

        r = lax.broadcasted_iota(jnp.int32, (blk, blk), 0)
        c = lax.broadcasted_iota(jnp.int32, (blk, blk), 1)
        tri = ((r <= c) if reverse else (r >= c)).astype(F32)
        o_ref[...] = jnp.dot(tri, x_ref[...], precision=HIGHEST, preferred_element_type=F32) + carry[...]
        carry[...] = o_ref[0:1, :] if reverse else o_ref[blk - 1:blk, :]

    idx = (lambda i: (nb - 1 - i, 0)) if reverse else (lambda i: (i, 0))
    return pl.pallas_call(
        body, grid=(nb,), in_specs=[pl.BlockSpec((blk, LANES), idx)], out_specs=pl.BlockSpec((blk, LANES), idx),
        out_shape=jax.ShapeDtypeStruct((t, LANES), F32), scratch_shapes=[pltpu.VMEM((1, LANES), F32)],
        compiler_params=pltpu.CompilerParams(dimension_semantics=("arbitrary",), vmem_limit_bytes=VMEM_LIMIT),
        name=name)(x)


def _ssd_decay(a_ref, at_ref, hh, mask):
    return jnp.exp(jnp.where(mask, a_ref[:, hh:hh + 1] - at_ref[hh:hh + 1, :], -jnp.inf))


def _head_lanes(s):
    lane = lax.broadcasted_iota(jnp.int32, (s.shape[0], LANES), 1)
    return jnp.concatenate([jnp.where(lane < SSM_P, s[:, 2 * p:2 * p + 1], s[:, 2 * p + 1:2 * p + 2])
                            for p in range(SSM_HEADS // 2)], axis=1)


_B_OFF, _C_OFF = 1024 // SSM_N, 1024 // SSM_N + SSM_GROUPS


def _ssd_fwd(xbc_act, xdt, a_cum, a_cum_t, t):
    blk = _seq_block(t)
    nb = t // blk

    def body(i, j, c0, c1, b0, b1, x_ref, ai_ref, aj_ref, at_ref, o_ref, acc):
        @pl.when(j == 0)
        def _():
            acc[...] = jnp.zeros_like(acc)

        @pl.when(j < i)
        def _():
            xv = (x_ref[...] * _head_lanes(jnp.exp(ai_ref[0:1, :] - aj_ref[...]))).astype(BF16)
            for g, (c_ref, b_ref) in enumerate(((c0, b0), (c1, b1))):
                cb = lax.dot_general(c_ref[...].astype(BF16), b_ref[...].astype(BF16), _DIMS["nt"], preferred_element_type=F32)
                cols = slice(g * half, (g + 1) * half)
                acc[:, cols] += jnp.dot(cb.astype(BF16), xv[:, cols], preferred_element_type=F32)

        @pl.when(j == i)
        def _():
            o_ref[...] = acc[...] * _head_lanes(jnp.exp(ai_ref[...] - ai_ref[0:1, :]))
            mask = _causal_diff(i, j, blk) >= 0
            for g, (c_ref, b_ref) in enumerate(((c0, b0), (c1, b1))):
                cb = lax.dot_general(c_ref[...].astype(BF16), b_ref[...].astype(BF16), _DIMS["nt"], preferred_element_type=F32)
                for h in range(SSM_HG):
                    hh = g * SSM_HG + h
                    cols = slice(hh * SSM_P, (hh + 1) * SSM_P)
                    m = (cb * _ssd_decay(ai_ref, at_ref, hh, mask)).astype(BF16)
                    o_ref[:, cols] += jnp.dot(m, x_ref[:, cols].astype(BF16), preferred_element_type=F32)

    half = SSM_HG * SSM_P
    row_i = lambda off: pl.BlockSpec((blk, SSM_N), lambda i, j, off=off: (i, off))
    row_j = lambda off: pl.BlockSpec((blk, SSM_N), lambda i, j, off=off: (jnp.minimum(j, i), off))
    return _pair_call(
        body, (), nb, False,
        [row_i(_C_OFF), row_i(_C_OFF + 1), row_j(_B_OFF), row_j(_B_OFF + 1),
         pl.BlockSpec((blk, SSM_HEADS * SSM_P), lambda i, j: (j, 0)),
         pl.BlockSpec((blk, LANES), lambda i, j: (i, 0)),
         pl.BlockSpec((blk, LANES), lambda i, j: (j, 0)),
         pl.BlockSpec((SSM_HEADS, blk), lambda i, j: (0, j))],
        pl.BlockSpec((blk, SSM_HEADS * SSM_P), lambda i, j: (i, 0)),
        jax.ShapeDtypeStruct((t, SSM_HEADS * SSM_P), F32), [pltpu.VMEM((blk, SSM_HEADS * SSM_P), F32)],
        "ssd_fwd", (xbc_act, xbc_act, xbc_act, xbc_act, xdt, a_cum, a_cum, a_cum_t))


def _ssd_bwd(xbc_act, xdt, a_cum, a_cum_t, dy, expand, t):
    blk = _seq_block(t)
    nb = t // blk
    width = SSM_HEADS * SSM_P

    half = SSM_HG * SSM_P

    def head_sums(prod, e_ref):
        return lax.dot_general(prod, e_ref[...], _DIMS["nt"], precision=HIGHEST, preferred_element_type=F32)

    def q_body(i, j, c0, c1, b0, b1, x_ref, ai_ref, aj_ref, at_ref, dy_ref, e_ref, dc_ref, da_ref, acc_c, acc_a, acc_p, dyu):
        @pl.when(j == 0)
        def _():
            acc_c[...] = jnp.zeros_like(acc_c)
            acc_a[...] = jnp.zeros_like(acc_a)
            acc_p[...] = jnp.zeros_like(acc_p)
            dyu[...] = (dy_ref[...].astype(F32) * _head_lanes(jnp.exp(ai_ref[...] - ai_ref[0:1, :]))).astype(BF16)

        @pl.when(j < i)
        def _():
            xv = (x_ref[...] * _head_lanes(jnp.exp(ai_ref[0:1, :] - aj_ref[...]))).astype(BF16)
            for g, (c_ref, b_ref) in enumerate(((c0, b0), (c1, b1))):
                cols = slice(g * half, (g + 1) * half)
                bj = b_ref[...].astype(BF16)
                cb = lax.dot_general(c_ref[...].astype(BF16), bj, _DIMS["nt"], preferred_element_type=F32).astype(BF16)
                dcb = lax.dot_general(dyu[:, cols], xv[:, cols], _DIMS["nt"], preferred_element_type=F32)
                acc_c[:, g * SSM_N:(g + 1) * SSM_N] += jnp.dot(dcb.astype(BF16), bj, preferred_element_type=F32)
                acc_p[:, cols] += dyu[:, cols].astype(F32) * jnp.dot(cb, xv[:, cols], preferred_element_type=F32)

        @pl.when(j == i)
        def _():
            mask = _causal_diff(i, j, blk) >= 0
            for g, (c_ref, b_ref) in enumerate(((c0, b0), (c1, b1))):
                bj = b_ref[...].astype(BF16)
                cb = lax.dot_general(c_ref[...].astype(BF16), bj, _DIMS["nt"], preferred_element_type=F32)
                dcb = jnp.zeros((blk, blk), F32)
                for h in range(SSM_HG):
                    hh = g * SSM_HG + h
                    cols = slice(hh * SSM_P, (hh + 1) * SSM_P)
                    dm = lax.dot_general(dy_ref[:, cols].astype(BF16), x_ref[:, cols].astype(BF16), _DIMS["nt"],
                                         preferred_element_type=F32) * _ssd_decay(ai_ref, at_ref, hh, mask)
                    dcb = dcb + dm
                    acc_a[:, hh:hh + 1] += jnp.sum(dm * cb, axis=1, keepdims=True)
                acc_c[:, g * SSM_N:(g + 1) * SSM_N] += jnp.dot(dcb.astype(BF16), bj, preferred_element_type=F32)
            dc_ref[...] = acc_c[...]
            da_ref[...] = acc_a[...] + head_sums(acc_p[...], e_ref)

    row_i = lambda off: pl.BlockSpec((blk, SSM_N), lambda i, j, off=off: (i, off))
    row_j = lambda off: pl.BlockSpec((blk, SSM_N), lambda i, j, off=off: (jnp.minimum(j, i), off))
    e_spec = pl.BlockSpec((LANES, width), lambda i, j: (0, 0))
    dc, da_q = _pair_call(
        q_body, (), nb, False,
        [row_i(_C_OFF), row_i(_C_OFF + 1), row_j(_B_OFF), row_j(_B_OFF + 1),
         pl.BlockSpec((blk, width), lambda i, j: (j, 0)),
         pl.BlockSpec((blk, LANES), lambda i, j: (i, 0)),
         pl.BlockSpec((blk, LANES), lambda i, j: (j, 0)),
         pl.BlockSpec((SSM_HEADS, blk), lambda i, j: (0, j)),
         pl.BlockSpec((blk, width), lambda i, j: (i, 0)), e_spec],
        [pl.BlockSpec((blk, SSM_GROUPS * SSM_N), lambda i, j: (i, 0)), pl.BlockSpec((blk, LANES), lambda i, j: (i, 0))],
        [jax.ShapeDtypeStruct((t, SSM_GROUPS * SSM_N), F32), jax.ShapeDtypeStruct((t, LANES), F32)],
        [pltpu.VMEM((blk, SSM_GROUPS * SSM_N), F32), pltpu.VMEM((blk, LANES), F32), pltpu.VMEM((blk, width), F32),
         pltpu.VMEM((blk, width), BF16)],
        "ssd_bwd_q", (xbc_act, xbc_act, xbc_act, xbc_act, xdt, a_cum, a_cum, a_cum_t, dy, expand))

    def k_body(i, j, c0, c1, b0, b1, x_ref, ai_ref, aj_ref, at_ref, dy_ref, e_ref, db_ref, dx_ref, da_ref, dat_ref, acc_b, acc_x, acc_a,
               acc_p):
        @pl.when(i == j)
        def _():
            acc_b[...] = jnp.zeros_like(acc_b)
            acc_x[...] = jnp.zeros_like(acc_x)
            acc_a[...] = jnp.zeros_like(acc_a)
            acc_p[...] = jnp.zeros_like(acc_p)

        @pl.when(i > j)
        def _():
            v = _head_lanes(jnp.exp(ai_ref[0:1, :] - aj_ref[...]))
            dyu_all = (dy_ref[...].astype(F32) * _head_lanes(jnp.exp(ai_ref[...] - ai_ref[0:1, :]))).astype(BF16)
            xv = (x_ref[...] * v).astype(BF16)
            for g, (c_ref, b_ref) in enumerate(((c0, b0), (c1, b1))):
                cols = slice(g * half, (g + 1) * half)
                ci = c_ref[...].astype(BF16)
                cb = lax.dot_general(ci, b_ref[...].astype(BF16), _DIMS["nt"], preferred_element_type=F32).astype(BF16)
                dcb = lax.dot_general(dyu_all[:, cols], xv[:, cols], _DIMS["nt"], preferred_element_type=F32)
                acc_b[:, g * SSM_N:(g + 1) * SSM_N] += lax.dot_general(dcb.astype(BF16), ci, _DIMS["tn"], preferred_element_type=F32)
                dxv = lax.dot_general(cb, dyu_all[:, cols], _DIMS["tn"], preferred_element_type=F32)
                acc_x[:, cols] += v[:, cols] * dxv
                acc_p[:, cols] += xv[:, cols].astype(F32) * dxv

        @pl.when(i == j)
        def _():
            mask = _causal_diff(i, j, blk) >= 0
            for g, (c_ref, b_ref) in enumerate(((c0, b0), (c1, b1))):
                ci = c_ref[...].astype(BF16)
                cb = lax.dot_general(ci, b_ref[...].astype(BF16), _DIMS["nt"], preferred_element_type=F32)
                dcb = jnp.zeros((blk, blk), F32)
                for h in range(SSM_HG):
                    hh = g * SSM_HG + h
                    cols = slice(hh * SSM_P, (hh + 1) * SSM_P)
                    decay = _ssd_decay(ai_ref, at_ref, hh, mask)
                    dyh = dy_ref[:, cols].astype(BF16)
                    acc_x[:, cols] += lax.dot_general((cb * decay).astype(BF16), dyh, _DIMS["tn"], preferred_element_type=F32)
                    dm = lax.dot_general(dyh, x_ref[:, cols].astype(BF16), _DIMS["nt"], preferred_element_type=F32) * decay
                    dcb = dcb + dm
                    acc_a[hh:hh + 1, :] += jnp.sum(dm * cb, axis=0, keepdims=True)
                acc_b[:, g * SSM_N:(g + 1) * SSM_N] += lax.dot_general(dcb.astype(BF16), ci, _DIMS["tn"], preferred_element_type=F32)

        @pl.when(i == nb - 1)
        def _():
            db_ref[...] = acc_b[...]
            dx_ref[...] = acc_x[...]
            dat_ref[...] = acc_a[...]
            da_ref[...] = head_sums(acc_p[...], e_ref)

    rowk_i = lambda off: pl.BlockSpec((blk, SSM_N), lambda j, i, off=off: (jnp.maximum(i, j), off))
    rowk_j = lambda off: pl.BlockSpec((blk, SSM_N), lambda j, i, off=off: (j, off))
    db, dx, da_k, da_k_t = _pair_call(
        k_body, (), nb, True,
        [rowk_i(_C_OFF), rowk_i(_C_OFF + 1), rowk_j(_B_OFF), rowk_j(_B_OFF + 1),
         pl.BlockSpec((blk, width), lambda j, i: (j, 0)),
         pl.BlockSpec((blk, LANES), lambda j, i: (i, 0)),
         pl.BlockSpec((blk, LANES), lambda j, i: (j, 0)),
         pl.BlockSpec((SSM_HEADS, blk), lambda j, i: (0, j)),
         pl.BlockSpec((blk, width), lambda j, i: (i, 0)), e_spec],
        [pl.BlockSpec((blk, SSM_GROUPS * SSM_N), lambda j, i: (j, 0)), pl.BlockSpec((blk, width), lambda j, i: (j, 0)),
         pl.BlockSpec((blk, LANES), lambda j, i: (j, 0)), pl.BlockSpec((SSM_HEADS, blk), lambda j, i: (0, j))],
        [jax.ShapeDtypeStruct((t, SSM_GROUPS * SSM_N), F32), jax.ShapeDtypeStruct((t, width), F32),
         jax.ShapeDtypeStruct((t, LANES), F32), jax.ShapeDtypeStruct((SSM_HEADS, t), F32)],
        [pltpu.VMEM((blk, SSM_GROUPS * SSM_N), F32), pltpu.VMEM((blk, width), F32), pltpu.VMEM((SSM_HEADS, blk), F32),
         pltpu.VMEM((blk, width), F32)],
        "ssd_bwd_k", (xbc_act, xbc_act, xbc_act, xbc_act, xdt, a_cum, a_cum, a_cum_t, dy, expand))
    return dc, da_q, db, dx, da_k, da_k_t


def _s5_scan(bu, a_re, a_im, reverse, x_prev=None):
    t = bu.shape[0]
    cb = 512
    ncb = S5_COLS // cb
    tb = min(1024, t)
    ntb = t // tb
    sub = 8
    shape = (sub, cb)

    def cmul(ar, ai, br, bi):
        return ar * br - ai * bi, ar * bi + ai * br

    def body(*refs):
        if reverse:
            br_ref, bi_ref, ar_ref, ai_ref, xr_ref, xi_ref, or_ref, oi_ref, dar_ref, dai_ref, carry, dacc, states = refs
        else:
            br_ref, bi_ref, ar_ref, ai_ref, or_ref, oi_ref, or_bf_ref, oi_bf_ref, carry = refs

        @pl.when(pl.program_id(1) == 0)
        def _():
            carry[...] = jnp.zeros_like(carry)
            if reverse:
                dacc[...] = jnp.zeros_like(dacc)

        row = lax.broadcasted_iota(jnp.int32, shape, 0)
        a1 = (jnp.broadcast_to(ar_ref[...], shape), jnp.broadcast_to(-ai_ref[...] if reverse else ai_ref[...], shape))
        a2 = cmul(*a1, *a1)
        a4 = cmul(*a2, *a2)
        a8 = cmul(*a4, *a4)
        steps = (sub - row) if reverse else (row + 1)
        pw = (jnp.ones(shape, F32), jnp.zeros(shape, F32))
        for bit, p in ((1, a1), (2, a2), (4, a4), (8, a8)):
            on = (steps & bit) != 0
            pw = cmul(*pw, jnp.where(on, p[0], 1.0), jnp.where(on, p[1], 0.0))
        edge = 0 if reverse else sub - 1

        def shift(v, s):
            if reverse:
                return jnp.where(row < sub - s, pltpu.roll(v, sub - s, 0), 0.0)
            return jnp.where(row >= s, pltpu.roll(v, s, 0), 0.0)

        def tile(n, state):
            r0 = pl.multiple_of((tb // sub - 1 - n if reverse else n) * sub, sub)
            rows = pl.ds(r0, sub)
            xr, xi = br_ref[rows, :], bi_ref[rows, :]
            for s, p in ((1, a1), (2, a2), (4, a4)):
                dr, di = cmul(*p, shift(xr, s), shift(xi, s))
                xr, xi = xr + dr, xi + di
            cr, ci = jnp.broadcast_to(state[0], shape), jnp.broadcast_to(state[1], shape)
            dr, di = cmul(*pw, cr, ci)
            xr, xi = xr + dr, xi + di
            if reverse:
                states[0, rows, :], states[1, rows, :] = xr, xi
                gr = jnp.where(row < sub - 1, pltpu.roll(xr, sub - 1, 0), cr)
                gi = jnp.where(row < sub - 1, pltpu.roll(xi, sub - 1, 0), ci)
                pr, pi = xr_ref[rows, :], xi_ref[rows, :]
                dacc[0] += gr * pr + gi * pi
                dacc[1] += gi * pr - gr * pi
            else:
                or_ref[rows, :], oi_ref[rows, :] = xr, xi
            return (jnp.sum(jnp.where(row == edge, xr, 0.0), axis=0, keepdims=True),
                    jnp.sum(jnp.where(row == edge, xi, 0.0), axis=0, keepdims=True))

        last = lax.fori_loop(0, tb // sub, tile, (carry[0], carry[1]))
        carry[0], carry[1] = last
        if reverse:
            or_ref[...], oi_ref[...] = states[0].astype(BF16), states[1].astype(BF16)
            dar_ref[...] = jnp.sum(dacc[0], axis=0, keepdims=True)
            dai_ref[...] = jnp.sum(dacc[1], axis=0, keepdims=True)
        else:
            or_bf_ref[...], oi_bf_ref[...] = or_ref[...].astype(BF16), oi_ref[...].astype(BF16)

    tblock = (lambda k: ntb - 1 - k) if reverse else (lambda k: k)
    re_spec = pl.BlockSpec((tb, cb), lambda j, k: (tblock(k), j))
    im_spec = pl.BlockSpec((tb, cb), lambda j, k: (tblock(k), j + ncb))
    a_spec = pl.BlockSpec((1, cb), lambda j, k: (0, j))
    ins, in_specs = [bu, bu, a_re, a_im], [re_spec, im_spec, a_spec, a_spec]
    seq_bf = jax.ShapeDtypeStruct((t, S5_COLS), BF16)
    scratch = [pltpu.VMEM((2, 1, cb), F32)]
    if reverse:
        ins += [x_prev, x_prev]
        in_specs += [re_spec, im_spec]
        out_shape = [seq_bf, seq_bf] + [jax.ShapeDtypeStruct((1, S5_COLS), F32)] * 2
        out_specs = [re_spec, re_spec, a_spec, a_spec]
        scratch += [pltpu.VMEM((2, sub, cb), F32), pltpu.VMEM((2, tb, cb), F32)]
    else:
        out_shape = [jax.ShapeDtypeStruct((t, S5_COLS), F32)] * 2 + [seq_bf, seq_bf]
        out_specs = [re_spec] * 4
    return pl.pallas_call(
        body, grid=(ncb, ntb), in_specs=in_specs, out_specs=out_specs, out_shape=out_shape, scratch_shapes=scratch,
        compiler_params=pltpu.CompilerParams(dimension_semantics=("arbitrary", "arbitrary"), vmem_limit_bytes=VMEM_LIMIT),
        name="s5_scan_bwd" if reverse else "s5_scan_fwd")(*ins)


CONF_DIM = 512
CONF_K = 31
CONF_PAD = 32


def _conf_conv(proj_o, w, b, dc1=None):
    t = proj_o.shape[0]
    cb = LANES
    ncb = CONF_DIM // cb
    chunk = min(512, t)
    chunks = range(0, t, chunk)
    bwd = dc1 is not None

    def body(*refs):
        if bwd:
            ca_ref, cg_ref, w_ref, b_ref, dy_ref, dca_ref, dcg_ref, dw_ref, db_ref, xs, dys = refs
        else:
            ca_ref, cg_ref, w_ref, b_ref, o_ref, xs = refs
        xs[pl.ds(0, CONF_PAD), :] = jnp.zeros((CONF_PAD, cb), F32)
        for t0 in chunks:
            rows = pl.ds(t0, chunk)
            xs[pl.ds(CONF_PAD + t0, chunk), :] = ca_ref[rows, :] * jax.nn.sigmoid(cg_ref[rows, :])
        if not bwd:
            for t0 in chunks:
                acc = jnp.broadcast_to(b_ref[...], (chunk, cb))
                for k in range(CONF_K):
                    acc = acc + w_ref[k] * xs[pl.ds(CONF_PAD + t0 - (CONF_K - 1 - k), chunk), :]
                o_ref[pl.ds(t0, chunk), :] = acc
            return
        dys[pl.ds(t, CONF_PAD), :] = jnp.zeros((CONF_PAD, cb), F32)
        db = jnp.zeros((1, cb), F32)
        for t0 in chunks:
            dys[pl.ds(t0, chunk), :] = dy_ref[pl.ds(t0, chunk), :]
            db = db + jnp.sum(dy_ref[pl.ds(t0, chunk), :], axis=0, keepdims=True)
        db_ref[...] = db
        for t0 in chunks:
            rows = pl.ds(t0, chunk)
            acc = jnp.zeros((chunk, cb), F32)
            for k in range(CONF_K):
                acc = acc + w_ref[k] * dys[pl.ds(t0 + (CONF_K - 1 - k), chunk), :]
            sig = jax.nn.sigmoid(cg_ref[rows, :])
            dca_ref[rows, :] = (acc * sig).astype(BF16)
            dcg_ref[rows, :] = (acc * ca_ref[rows, :] * sig * (1.0 - sig)).astype(BF16)
        for k in range(CONF_K):
            dwk = jnp.zeros((1, cb), F32)
            for t0 in chunks:
                window = xs[pl.ds(CONF_PAD + t0 - (CONF_K - 1 - k), chunk), :]
                dwk = dwk + jnp.sum(dy_ref[pl.ds(t0, chunk), :] * window, axis=0, keepdims=True)
            dw_ref[k] = dwk

    col = lambda off: pl.BlockSpec((t, cb), lambda j, off=off: (0, j + off))
    w_spec = pl.BlockSpec((CONF_K, 1, cb), lambda j: (0, 0, j))
    b_spec = pl.BlockSpec((1, cb), lambda j: (0, j))
    seq = jax.ShapeDtypeStruct((t, CONF_DIM), F32)
    ins, in_specs = [proj_o, proj_o, w, b], [col(0), col(ncb), w_spec, b_spec]
    scratch = [pltpu.VMEM((CONF_PAD + t, cb), F32)]
    if bwd:
        ins, in_specs = ins + [dc1], in_specs + [col(0)]
        seq_bf = jax.ShapeDtypeStruct((t, CONF_DIM), BF16)
        out_shape, out_specs = [seq_bf, seq_bf, jax.ShapeDtypeStruct(w.shape, F32), jax.ShapeDtypeStruct(b.shape, F32)], [col(0), col(0), w_spec, b_spec]
        scratch = scratch + [pltpu.VMEM((t + CONF_PAD, cb), F32)]
    else:
        out_shape, out_specs = seq, col(0)
    return pl.pallas_call(
        body, grid=(ncb,), in_specs=in_specs, out_specs=out_specs, out_shape=out_shape, scratch_shapes=scratch,
        compiler_params=pltpu.CompilerParams(dimension_semantics=("arbitrary",), vmem_limit_bytes=VMEM_LIMIT),
        name="conf_conv_bwd" if bwd else "conf_conv")(*ins)


N_CHIPS = 4
_HBM = pl.BlockSpec(memory_space=pl.ANY)
_MESH_ID = pl.DeviceIdType.MESH


def _comm_call(body, srcs, out_shapes, n_sems, name):
    n = len(srcs)
    return pl.pallas_call(
        body, out_shape=out_shapes, in_specs=[_HBM] * n, out_specs=[_HBM] * n,
        scratch_shapes=[pltpu.SemaphoreType.DMA((n_sems,)), pltpu.SemaphoreType.DMA((n_sems,)), pltpu.SemaphoreType.DMA((n,))],
        compiler_params=pltpu.CompilerParams(has_side_effects=True), name=name)(*srcs)


def _gather(srcs, name):
    n = len(srcs)
    per = N_DEV - 1

    def body(*refs):
        src_refs, out_refs = refs[:n], refs[n:2 * n]
        send_sems, recv_sems, local_sems = refs[2 * n:]
        x, y, c = lax.axis_index("x"), lax.axis_index("y"), lax.axis_index("c")
        me, sibling = (x, y, c), (x, y, 1 - c)
        chips = [(1 - x, y), (x, 1 - y), (1 - x, 1 - y)]

        def copy(i, k, block, to, from_src=False):
            rows = out_refs[i].at[4 * block[0] + 2 * block[1] + block[2]]
            return pltpu.make_async_remote_copy(
                src_ref=src_refs[i] if from_src else rows, dst_ref=rows, send_sem=send_sems.at[per * i + k],
                recv_sem=recv_sems.at[per * i + k], device_id=to, device_id_type=_MESH_ID)

        local = [pltpu.make_async_copy(src_refs[i], out_refs[i].at[4 * x + 2 * y + c], local_sems.at[i]) for i in range(n)]
        for cp in local:
            cp.start()
        sends = []
        for i in range(n):
            sends.append(copy(i, 0, me, sibling, True))
            sends += [copy(i, 1 + j, me, (*chip, c), True) for j, chip in enumerate(chips)]
        for cp in sends:
            cp.start()
        for j, chip in enumerate(chips):
            for i in range(n):
                copy(i, 1 + j, (*chip, c), me).wait_recv()
                passed = copy(i, 4 + j, (*chip, c), sibling)
                passed.start()
                sends.append(passed)
        for i in range(n):
            copy(i, 0, sibling, me).wait_recv()
            for j, chip in enumerate(chips):
                copy(i, 4 + j, (*chip, 1 - c), me).wait_recv()
        for cp in sends:
            cp.wait_send()
        for cp in local:
            cp.wait()

    return _comm_call(body, srcs, [jax.ShapeDtypeStruct((N_DEV,) + s.shape, s.dtype) for s in srcs], per * n, name)


def _swap_sibling(srcs, name):
    n = len(srcs)

    def body(*refs):
        src_refs, out_refs = refs[:n], refs[n:2 * n]
        send_sems, recv_sems, _ = refs[2 * n:]
        sibling = (lax.axis_index("x"), lax.axis_index("y"), 1 - lax.axis_index("c"))
        copies = [pltpu.make_async_remote_copy(src_ref=src_refs[i], dst_ref=out_refs[i], send_sem=send_sems.at[i],
                                               recv_sem=recv_sems.at[i], device_id=sibling, device_id_type=_MESH_ID) for i in range(n)]
        for cp in copies:
            cp.start()
        for cp in copies:
            cp.wait_recv()
        for cp in copies:
            cp.wait_send()

    return _comm_call(body, srcs, [jax.ShapeDtypeStruct(s.shape, s.dtype) for s in srcs], n, name)


def _exchange_chips(srcs, name):
    n = len(srcs)
    per = N_CHIPS - 1

    def body(*refs):
        src_refs, out_refs = refs[:n], refs[n:2 * n]
        send_sems, recv_sems, local_sems = refs[2 * n:]
        x, y, c = lax.axis_index("x"), lax.axis_index("y"), lax.axis_index("c")
        mine = 2 * x + y
        local = [pltpu.make_async_copy(src_refs[i].at[mine], out_refs[i].at[mine], local_sems.at[i]) for i in range(n)]
        for cp in local:
            cp.start()
        copies = []
        for k in range(1, N_CHIPS):
            px, py = x ^ (k >> 1), y ^ (k & 1)
            peer = 2 * px + py
            for i in range(n):
                sem = per * i + k - 1
                send = pltpu.make_async_remote_copy(
                    src_ref=src_refs[i].at[peer], dst_ref=out_refs[i].at[mine], send_sem=send_sems.at[sem], recv_sem=recv_sems.at[sem],
                    device_id=(px, py, c), device_id_type=_MESH_ID)
                send.start()
                recv = pltpu.make_async_remote_copy(
                    src_ref=src_refs[i].at[peer], dst_ref=out_refs[i].at[peer], send_sem=send_sems.at[sem], recv_sem=recv_sems.at[sem],
                    device_id=(px, py, c), device_id_type=_MESH_ID)
                copies.append((send, recv))
        for _, recv in copies:
            recv.wait_recv()
        for send, _ in copies:
            send.wait_send()
        for cp in local:
            cp.wait()

    return _comm_call(body, srcs, [jax.ShapeDtypeStruct(s.shape, s.dtype) for s in srcs], per * n, name)


def _add(a, b, name):
    k, rows, cols = a.shape
    tr = _row_tile(rows)

    def body(a_ref, b_ref, o_ref):
        o_ref[...] = (a_ref[...].astype(F32) + b_ref[...].astype(F32)).astype(o_ref.dtype)

    spec = pl.BlockSpec((k, tr, cols), lambda i: (0, i, 0))
    return pl.pallas_call(
        body, grid=(rows // tr,), in_specs=[spec, spec], out_specs=spec, out_shape=jax.ShapeDtypeStruct(a.shape, a.dtype),
        compiler_params=pltpu.CompilerParams(dimension_semantics=("parallel",), vmem_limit_bytes=VMEM_LIMIT), name=name)(a, b)


def _row_tile(r, pref=256):
    if r <= pref:
        return r
    t = pref // 16 * 16
    while r % t:
        t -= 16
    return t


def _join_cols(g, width, name):
    _, rows, ws = g.shape
    tr = _row_tile(rows)
    tail = width - N_DEV * ws

    def body(g_ref, o_ref):
        for d in range(N_DEV):
            o_ref[:, pl.ds(d * ws, ws)] = g_ref[d]
        if tail:
            o_ref[:, pl.ds(N_DEV * ws, tail)] = jnp.zeros((tr, tail), g.dtype)

    return pl.pallas_call(
        body, grid=(rows // tr,), in_specs=[pl.BlockSpec((N_DEV, tr, ws), lambda i: (0, i, 0))],
        out_specs=pl.BlockSpec((tr, width), lambda i: (i, 0)), out_shape=jax.ShapeDtypeStruct((rows, width), g.dtype),
        compiler_params=pltpu.CompilerParams(dimension_semantics=("parallel",), vmem_limit_bytes=VMEM_LIMIT), name=name)(g)


def _split_cols(full, ws, dtype, name):
    rows, width = full.shape
    tr = _row_tile(rows)

    def body(x_ref, o_ref):
        for d in range(N_DEV):
            o_ref[d] = x_ref[:, pl.ds(d * ws, ws)].astype(dtype)

    return pl.pallas_call(
        body, grid=(rows // tr,), in_specs=[pl.BlockSpec((tr, width), lambda i: (i, 0))],
        out_specs=pl.BlockSpec((N_DEV, tr, ws), lambda i: (0, i, 0)), out_shape=jax.ShapeDtypeStruct((N_DEV, rows, ws), dtype),
        compiler_params=pltpu.CompilerParams(dimension_semantics=("parallel",), vmem_limit_bytes=VMEM_LIMIT), name=name)(full)


def _adamw(parts, w, m, v, name):
    r, c = w.shape
    n_parts = parts.shape[0]
    tr = _row_tile(r)

    def body(p_ref, w_ref, m_ref, v_ref, g_ref, d_ref, nm_ref, nv_ref):
        g = p_ref[0].astype(F32)
        for s in range(1, n_parts):
            g = g + p_ref[s].astype(F32)
        nm = ADAM_B1 * m_ref[...] + (1.0 - ADAM_B1) * g
        nv = ADAM_B2 * v_ref[...] + (1.0 - ADAM_B2) * (g * g)
        m_hat = nm / (1.0 - ADAM_B1 ** ADAM_STEP)
        v_hat = nv / (1.0 - ADAM_B2 ** ADAM_STEP)
        g_ref[...] = g
        nm_ref[...] = nm
        nv_ref[...] = nv
        d_ref[...] = -ADAM_LR * (m_hat / (jnp.sqrt(v_hat) + ADAM_EPS) + ADAM_WD * w_ref[...])

    spec = pl.BlockSpec((tr, c), lambda i: (i, 0))
    return pl.pallas_call(
        body, grid=(r // tr,), in_specs=[pl.BlockSpec((n_parts, tr, c), lambda i: (0, i, 0)), spec, spec, spec],
        out_specs=[spec] * 4, out_shape=[jax.ShapeDtypeStruct((r, c), F32)] * 4,
        compiler_params=pltpu.CompilerParams(dimension_semantics=("parallel",), vmem_limit_bytes=VMEM_LIMIT),
        name=name)(parts, w, m, v)


def _pack_rows(n_elems, mult):
    rows = -(-n_elems // PACK_COLS)
    return -(-rows // mult) * mult


def _pack(arrays, dtype, mult, lead=()):
    flat = jnp.concatenate([a.astype(dtype).reshape(lead + (-1,)) for a in arrays], axis=-1)
    rows = _pack_rows(flat.shape[-1], mult)
    flat = jnp.pad(flat, [(0, 0)] * len(lead) + [(0, rows * PACK_COLS - flat.shape[-1])])
    return flat.reshape(lead + (rows, PACK_COLS))


def _unpack(buf, shapes, lead=()):
    flat = buf.reshape(lead + (-1,))
    out, off = [], 0
    for s in shapes:
        n = math.prod(s)
        out.append(flat[..., off:off + n].reshape(lead + tuple(s)))
        off += n
    return out


def _join_shards(piece, axis):
    moved = jnp.moveaxis(piece, 0, axis)
    shape = moved.shape
    return moved.reshape(shape[:axis] + (shape[axis] * shape[axis + 1],) + shape[axis + 2:])


def _split_shards(full, axis):
    shape = full.shape
    return jnp.moveaxis(full.reshape(shape[:axis] + (N_DEV, shape[axis] // N_DEV) + shape[axis + 1:]), axis, 0)


def _block_diag(blocks):
    g, r, c = blocks.shape
    eye = jnp.eye(g, dtype=blocks.dtype)
    return (blocks[:, :, None, :] * eye[:, None, :, None]).reshape(g * r, g * c)


def _diag_blocks(mat, g):
    r, c = mat.shape[0] // g, mat.shape[1] // g
    eye = jnp.eye(g, dtype=mat.dtype)
    return jnp.sum(mat.reshape(g, r, g, c) * eye[:, None, :, None], axis=2)


def _pad_lanes(a):
    a = a.reshape(1, -1)
    return jnp.pad(a, ((0, 0), (0, LANES - a.shape[1])))


def _head_expand():
    h = jnp.arange(LANES)[:, None]
    ch = jnp.arange(SSM_HEADS * SSM_P)[None, :] // SSM_P
    return (h == ch).astype(F32)


def _rotary_tables(t):
    inv = 10000.0 ** (-jnp.arange(0, RET_DK, 2, dtype=F32) / RET_DK)
    ang = jnp.arange(t).astype(F32)[:, None] * inv[None, :]
    cos, sin = jnp.cos(ang), jnp.sin(ang)
    return jnp.concatenate([cos, cos], axis=1), jnp.concatenate([-sin, sin], axis=1)


def _rms_fwd(x, g, name):
    return _rows(_f_rms, x.shape[0], [(x, D_MODEL, 0)], [g], [D_MODEL], [], name, bf16_outs=(0,))[0]


def _rms_bwd(x, g, dh, dres, name):
    def fn(x_, dh_, dres_, g_):
        _, vjp = jax.vjp(lambda a, b: _rms(a, b), x_, g_)
        dx, dg = vjp(dh_)
        return dx + dres_, dx + dres_, dg
    return _rows(fn, x.shape[0], [(x, D_MODEL, 0), (dh, D_MODEL, 0), (dres, D_MODEL, 0)], [g], [D_MODEL, D_MODEL], [(1, D_MODEL)],
                 name, bf16_outs=(1,))


def _ffn_fwd(x, norm_g, w_up, dw_w, dw_b, w_down, tag):
    t = x.shape[0]
    nbk = D_FF // LANES
    h = _rms_fwd(x, norm_g, "ffn_norm_" + tag)
    up = _mm(h, w_up, "nn", "ffn_up_" + tag)
    mid = _cols(_f_ffnmid, nbk, [(up, 0), (up, nbk), (dw_w, 0), (dw_w, nbk), (dw_b, 0), (dw_b, nbk)], [(t,)], "ffn_mid_" + tag,
                bf16_outs=(0,))[0]
    out = _mm(mid, w_down, "nn", "ffn_down_" + tag, res=x)
    return out, (h, up, mid)


def _ffn_bwd(x, norm_g, w_up, dw_w, dw_b, w_down, saved, dout, dout_bf, tag):
    t = x.shape[0]
    nbk = D_FF // LANES
    h, up, mid = saved
    d_w_down = _mm(mid, dout_bf, "tn", "ffn_down_dw_" + tag, out_dtype=BF16)
    dmid = _mm(dout_bf, w_down, "nt", "ffn_down_dx_" + tag)
    dgin, duin, dwg, dwu, dbg, dbu = _cols(
        _grad_fn(_f_ffnmid, 6, 6), nbk,
        [(up, 0), (up, nbk), (dw_w, 0), (dw_w, nbk), (dw_b, 0), (dw_b, nbk), (dmid, 0)],
        [(t,), (t,), (3, 1), (3, 1), (1,), (1,)], "ffn_mid_bwd_" + tag, bf16_outs=(0, 1))
    dup = jnp.concatenate([dgin, duin], axis=1)
    d_w_up = _mm(h, dup, "tn", "ffn_up_dw_" + tag)
    dh = _mm(dup, w_up, "nt", "ffn_up_dx_" + tag)
    dx, dx_bf, dnorm = _rms_bwd(x, norm_g, dh, dout, "ffn_norm_bwd_" + tag)
    return dx, dx_bf, dict(norm=dnorm, w_up=d_w_up, dw_w=jnp.concatenate([dwg, dwu], axis=2)[:, 0],
                           dw_b=jnp.concatenate([dbg, dbu], axis=1), w_down=d_w_down)


def _local_step(x, tgt, w):
    t = x.shape[0]
    grads = {}
    expand = _head_expand()
    cosf, sins = _rotary_tables(t)
    mix_g = [w['mix_norm'][i:i + 1] for i in range(2)]
    ffn_g = [w['ffn_norm'][i:i + 1] for i in range(2)]
    ffn_dw_w = [w['ffn_dw_w'][i][:, None, :] for i in range(2)]
    ffn_dw_b = [w['ffn_dw_b'][i:i + 1] for i in range(2)]

    w_e = w['e_w_in']
    conv_w = w['e_conv_w'][0][:, None, :]
    conv_b = w['e_conv_b']
    dt_bias, a_log, d_skip = _pad_lanes(w['e_dt_bias']), _pad_lanes(w['e_a_log']), _pad_lanes(w['e_d'])
    h0 = _rms_fwd(x, mix_g[0], "mix_norm_0")
    proj = _mm(h0, w_e, "nn", "e_in")
    qr, kr = _rows(_f_retpre, t, [(proj, 512, 0), (proj, 512, 1), (cosf, LANES, 0), (sins, LANES, 0)], [], [512, 512], [], "ret_pre")
    r = _ret_fwd(qr, kr, proj, t)
    y_ret = _rows(_f_retpost, t, [(r, 1024, 0), (proj, 1024, 2)], [], [1024], [], "ret_post", bf16_outs=(0,))[0]
    xbc_act = _cols(_f_ssdconv, 12, [(proj, 32), (conv_w, 0), (conv_b, 0)], [(t,)], "ssd_conv")[0]
    xdt, da = _rows(_f_ssdpre, t, [(xbc_act, 1024, 0), (proj, LANES, 44)], [dt_bias, a_log, expand], [1024, LANES], [], "ssd_pre")
    a_cum = _cumsum(da, False, "ssd_cumsum")
    a_cum_t = a_cum[:, :SSM_HEADS].T
    yc = _ssd_fwd(xbc_act, xdt, a_cum, a_cum_t, t)
    y_ssm = _rows(_f_ssdpost, t, [(yc, 1024, 0), (xbc_act, 1024, 0), (proj, 1024, 3)], [d_skip, w['e_ssm_norm'], expand],
                  [1024], [], "ssd_post", bf16_outs=(0,))[0]
    mix_e = jnp.concatenate([y_ret, y_ssm], axis=1)
    x1 = _mm(mix_e, w['e_w_out'], "nn", "e_out", res=x)
    x2, ffn0 = _ffn_fwd(x1, ffn_g[0], w['ffn_w_up'][0], ffn_dw_w[0], ffn_dw_b[0], w['ffn_w_down'][0], "0")

    lr, li = w['o_a_re'][0], w['o_a_im'][0]
    ls = w['o_log_step'].reshape(S5_GROUPS, 1)
    b_re3, b_im3 = jnp.transpose(w['o_b_re'][0], (2, 0, 1)), jnp.transpose(w['o_b_im'][0], (2, 0, 1))
    par_ins = [lr, li, ls, b_re3, b_im3]
    whole = lambda a: (a, a.shape, (lambda i, n=a.ndim: (0,) * n))
    par_shapes = [(S5_GROUPS, S5_STATE)] * 2 + [(S5_GROUP, S5_GROUPS, S5_STATE)] * 2
    ab_re, ab_im, bb_re, bb_im = _call(_f_s5par, (1,), [whole(a) for a in par_ins],
                                       [(s, F32, s, (lambda i, n=len(s): (0,) * n), False) for s in par_shapes], "s5_params")
    w_b = jnp.concatenate([_block_diag(jnp.transpose(bb_re, (1, 0, 2))), _block_diag(jnp.transpose(bb_im, (1, 0, 2)))], axis=1)
    w_c = jnp.concatenate([_block_diag(jnp.transpose(w['o_c_re'][0], (0, 2, 1))),
                           -_block_diag(jnp.transpose(w['o_c_im'][0], (0, 2, 1)))], axis=0)
    a_re, a_im = ab_re.reshape(1, S5_COLS), ab_im.reshape(1, S5_COLS)
    dw_w = w['o_dw_w'][0][:, None, :]
    glu_w = w['o_glu_w'].astype(F32)

    h1 = _rms_fwd(x2, mix_g[1], "mix_norm_1")
    proj_o = _mm(h1, w['o_w_in'], "nn", "o_in")
    c1 = _conf_conv(proj_o, dw_w, w['o_dw_b'])
    c2 = _rows(_f_confb, t, [(c1, 512, 0)], [w['o_ln_g'], w['o_ln_b']], [512], [], "conf_norm", bf16_outs=(0,))[0]
    u = proj_o[:, 1024:].astype(BF16)
    bu = _mm(u, w_b, "nn", "s5_bu")
    xs_re, xs_im, xs_re_bf, xs_im_bf = _s5_scan(bu, a_re, a_im, False)
    xs_cat = jnp.concatenate([xs_re, xs_im], axis=1)
    xs_cat_bf = jnp.concatenate([xs_re_bf, xs_im_bf], axis=1)
    y_s5 = _mm(xs_cat_bf, w_c, "nn", "s5_cx")
    s_out = _rows(_f_s5post, t, [(y_s5, 512, 0), (proj_o, 512, 2)], [w['o_d'], glu_w], [512], [], "s5_post", bf16_outs=(0,))[0]
    mix_o = jnp.concatenate([c2, s_out], axis=1)
    x3 = _mm(mix_o, w['o_w_out'], "nn", "o_out", res=x2)
    x4, ffn1 = _ffn_fwd(x3, ffn_g[1], w['ffn_w_up'][1], ffn_dw_w[1], ffn_dw_b[1], w['ffn_w_down'][1], "1")

    dx4, dx4_bf, loss_blk, d_final = _rows(_loss_step, t, [(x4, D_MODEL, 0), (tgt, D_MODEL, 0)], [w['final_norm'].reshape(1, D_MODEL)],
                                           [D_MODEL, D_MODEL], [(8, LANES), (1, D_MODEL)], "loss_head", bf16_outs=(1,))
    loss = loss_blk[0, 0]
    grads['final_norm'] = d_final.reshape(D_MODEL)

    dx3, dx3_bf, g1 = _ffn_bwd(x3, ffn_g[1], w['ffn_w_up'][1], ffn_dw_w[1], ffn_dw_b[1], w['ffn_w_down'][1], ffn1, dx4, dx4_bf, "1")
    grads['o_w_out'] = _mm(mix_o, dx3_bf, "tn", "o_out_dw", out_dtype=BF16)
    dmix_o = _mm(dx3_bf, w['o_w_out'], "nt", "o_out_dx")
    dc1, d_ln_g, d_ln_b = _conf_norm_bwd(c1, w['o_ln_g'], w['o_ln_b'], dmix_o, t)
    dca, dcg, d_dw_w, d_dw_b = _conf_conv(proj_o, dw_w, w['o_dw_b'], dc1)
    dyc, du_skip, d_od, d_glu = _s5_post_bwd(y_s5, proj_o, w['o_d'], glu_w, dmix_o, t)
    grads['o_glu_w'] = d_glu
    d_w_c = _mm(xs_cat_bf, dyc, "tn", "s5_cx_dw")
    dxs = _mm(dyc, w_c, "nt", "s5_cx_dx")
    g_re, g_im, d_are, d_aim = _s5_scan(dxs, a_re, a_im, True, xs_cat)
    g_cat = jnp.concatenate([g_re, g_im], axis=1)
    d_w_b = _mm(u, g_cat, "tn", "s5_bu_dw")
    du = _mm(g_cat, w_b, "nt", "s5_bu_dx", res=du_skip, out_dtype=BF16)
    d_bb_re = jnp.transpose(_diag_blocks(d_w_b[:, :S5_COLS], S5_GROUPS), (1, 0, 2))
    d_bb_im = jnp.transpose(_diag_blocks(d_w_b[:, S5_COLS:], S5_GROUPS), (1, 0, 2))
    par_cts = [d_are.reshape(S5_GROUPS, S5_STATE), d_aim.reshape(S5_GROUPS, S5_STATE), d_bb_re, d_bb_im]
    in_shapes = [a.shape for a in par_ins]
    d_lr, d_li, d_ls, d_br3, d_bi3 = _call(_grad_fn(_f_s5par, 5, 5), (1,), [whole(a) for a in par_ins + par_cts],
                                           [(s, F32, s, (lambda i, n=len(s): (0,) * n), False) for s in in_shapes], "s5_params_bwd")
    grads['o_a_re'], grads['o_a_im'], grads['o_log_step'] = d_lr[None], d_li[None], d_ls.reshape(1, S5_GROUPS)
    grads['o_b_re'], grads['o_b_im'] = jnp.transpose(d_br3, (1, 2, 0))[None], jnp.transpose(d_bi3, (1, 2, 0))[None]
    grads['o_c_re'] = jnp.transpose(_diag_blocks(d_w_c[:S5_COLS], S5_GROUPS), (0, 2, 1))[None]
    grads['o_c_im'] = -jnp.transpose(_diag_blocks(d_w_c[S5_COLS:], S5_GROUPS), (0, 2, 1))[None]
    grads['o_d'], grads['o_ln_g'], grads['o_ln_b'] = d_od, d_ln_g, d_ln_b
    grads['o_dw_w'], grads['o_dw_b'] = d_dw_w[:, 0][None], d_dw_b
    dproj_o = jnp.concatenate([dca, dcg, du], axis=1)
    grads['o_w_in'] = _mm(h1, dproj_o, "tn", "o_in_dw")
    dh1 = _mm(dproj_o, w['o_w_in'], "nt", "o_in_dx")
    dx2, dx2_bf, d_mix1 = _rms_bwd(x2, mix_g[1], dh1, dx3, "mix_norm_bwd_1")

    dx1, dx1_bf, g0 = _ffn_bwd(x1, ffn_g[0], w['ffn_w_up'][0], ffn_dw_w[0], ffn_dw_b[0], w['ffn_w_down'][0], ffn0, dx2, dx2_bf, "0")
    grads['e_w_out'] = _mm(mix_e, dx1_bf, "tn", "e_out_dw", out_dtype=BF16)
    dmix_e = _mm(dx1_bf, w['e_w_out'], "nt", "e_out_dx")
    dr, dg = _rows(_grad_fn(_f_retpost, 2, 2), t, [(r, 1024, 0), (proj, 1024, 2), (dmix_e, 1024, 0)], [], [1024, 1024], [], "ret_post_bwd",
                   bf16_outs=(0, 1))
    dqr, dkr, dv = _ret_bwd(qr, kr, proj, dr, t)
    dq, dk = _ret_pre_bwd(proj, cosf, sins, dqr, dkr, t)
    dyc0, dxs1, dz, d_dskip, d_ssm_norm = _ssd_post_bwd(yc, xbc_act, proj, d_skip, w['e_ssm_norm'], expand, dmix_e, t)
    dcm, da_q, dbm, dxdt, da_k, da_k_t = _ssd_bwd(xbc_act, xdt, a_cum, a_cum_t, dyc0, expand, t)
    d_a_cum = da_q - da_k - jnp.pad(da_k_t.T, ((0, 0), (0, LANES - SSM_HEADS)))
    dda = _cumsum(d_a_cum, True, "ssd_cumsum_bwd")
    dxs, ddtr, d_dt_bias, d_a_log = _ssd_pre_bwd(xbc_act, proj, dt_bias, a_log, expand, dxdt, dda, dxs1, t)
    dxbc_act = jnp.concatenate([dxs, dbm, dcm], axis=1)
    dxbc, d_conv_w, d_conv_b = _cols(_grad_fn(_f_ssdconv, 3, 3), 12, [(proj, 32), (conv_w, 0), (conv_b, 0), (dxbc_act, 0)],
                                     [(t,), (4, 1), (1,)], "ssd_conv_bwd", bf16_outs=(0,))
    dproj = jnp.concatenate([dq, dk, dv, dg, dz, dxbc, ddtr], axis=1)
    grads['e_w_in'] = _mm(h0, dproj, "tn", "e_in_dw")
    dh0 = _mm(dproj, w_e, "nt", "e_in_dx")
    dx0, _, d_mix0 = _rms_bwd(x, mix_g[0], dh0, dx1, "mix_norm_bwd_0")

    grads['mix_norm'] = jnp.concatenate([d_mix0, d_mix1], axis=0)
    grads['e_conv_w'], grads['e_conv_b'] = d_conv_w[:, 0][None], d_conv_b
    grads['e_dt_bias'], grads['e_a_log'], grads['e_d'] = d_dt_bias[:, :SSM_HEADS], d_a_log[:, :SSM_HEADS], d_dskip[:, :SSM_HEADS]
    grads['e_ssm_norm'] = d_ssm_norm
    grads['ffn_norm'] = jnp.concatenate([g0['norm'], g1['norm']], axis=0)
    grads['ffn_w_up'], grads['ffn_w_down'] = [g0['w_up'], g1['w_up']], [g0['w_down'], g1['w_down']]
    grads['ffn_dw_w'] = jnp.stack([g0['dw_w'], g1['dw_w']], axis=0)
    grads['ffn_dw_b'] = jnp.concatenate([g0['dw_b'], g1['dw_b']], axis=0)
    return loss, dx0, grads


def _conf_norm_bwd(c1, ln_g, ln_b, dmix_o, t):
    def fn(c1_, dy_, g_, b_):
        _, vjp = jax.vjp(lambda a, b, c: _f_confb(a, b, c)[0], c1_, g_, b_)
        return vjp(dy_)
    return _rows(fn, t, [(c1, 512, 0), (dmix_o, 512, 0)], [ln_g, ln_b], [512], [(1, 512), (1, 512)], "conf_norm_bwd")


def _s5_post_bwd(y_s5, proj_o, d_skip, glu_w, dmix_o, t):
    def fn(yc_, u_, dy_, d_, gw_):
        _, vjp = jax.vjp(lambda a, b, c, e: _f_s5post(a, b, c, e)[0], yc_, u_, d_, gw_)
        return vjp(dy_)
    return _rows(fn, t, [(y_s5, 512, 0), (proj_o, 512, 2), (dmix_o, 512, 1)], [d_skip, glu_w], [512, 512], [(1, 512), (512, 512)],
                 "s5_post_bwd", bf16_outs=(0,))


def _ret_pre_bwd(proj, cosf, sins, dqr, dkr, t):
    def fn(q_, k_, cos_, sin_, dq_, dk_):
        _, vjp = jax.vjp(lambda a, b: _f_retpre(a, b, cos_, sin_), q_, k_)
        return vjp((dq_, dk_))
    return _rows(fn, t, [(proj, 512, 0), (proj, 512, 1), (cosf, LANES, 0), (sins, LANES, 0), (dqr, 512, 0), (dkr, 512, 0)], [],
                 [512, 512], [], "ret_pre_bwd", bf16_outs=(0, 1))


def _ssd_post_bwd(yc, xbc_act, proj, d_skip, norm_w, expand, dmix_e, t):
    def fn(yc_, xs_, z_, dy_, d_, nw_, e_):
        _, vjp = jax.vjp(lambda a, b, c, dd, n: _f_ssdpost(a, b, c, dd, n, e_)[0], yc_, xs_, z_, d_, nw_)
        return vjp(dy_)
    return _rows(fn, t, [(yc, 1024, 0), (xbc_act, 1024, 0), (proj, 1024, 3), (dmix_e, 1024, 1)], [d_skip, norm_w, expand],
                 [1024, 1024, 1024], [(1, LANES), (1, 1024)], "ssd_post_bwd", bf16_outs=(0, 2))


def _ssd_pre_bwd(xbc_act, proj, dt_bias, a_log, expand, dxdt, dda, dxs1, t):
    def fn(xs_, dtr_, dx_, dda_, dxs1_, bias_, alog_, e_):
        _, vjp = jax.vjp(lambda a, b, c, dd: _f_ssdpre(a, b, c, dd, e_), xs_, dtr_, bias_, alog_)
        dxs, ddtr, dbias, dalog = vjp((dx_, dda_))
        return dxs + dxs1_, ddtr, dbias, dalog
    return _rows(fn, t, [(xbc_act, 1024, 0), (proj, LANES, 44), (dxdt, 1024, 0), (dda, LANES, 0), (dxs1, 1024, 0)],
                 [dt_bias, a_log, expand], [1024, LANES], [(1, LANES), (1, LANES)], "ssd_pre_bwd", bf16_outs=(1,))


def kernel(x, mix_norm, e_w_in, e_conv_w, e_conv_b, e_dt_bias, e_a_log, e_d, e_ssm_norm, e_w_out, o_w_in, o_dw_w, o_dw_b, o_ln_g, o_ln_b, o_a_re, o_a_im, o_b_re, o_b_im, o_c_re, o_c_im, o_d, o_log_step, o_glu_w, o_w_out, ffn_norm, ffn_w_up, ffn_dw_w, ffn_dw_b, ffn_w_down, final_norm, loss_target, m_mix_norm, m_e_w_in, m_e_conv_w, m_e_conv_b, m_e_dt_bias, m_e_a_log, m_e_d, m_e_ssm_norm, m_e_w_out, m_o_w_in, m_o_dw_w, m_o_dw_b, m_o_ln_g, m_o_ln_b, m_o_a_re, m_o_a_im, m_o_b_re, m_o_b_im, m_o_c_re, m_o_c_im, m_o_d, m_o_log_step, m_o_glu_w, m_o_w_out, m_ffn_norm, m_ffn_w_up, m_ffn_dw_w, m_ffn_dw_b, m_ffn_w_down, m_final_norm, v_mix_norm, v_e_w_in, v_e_conv_w, v_e_conv_b, v_e_dt_bias, v_e_a_log, v_e_d, v_e_ssm_norm, v_e_w_out, v_o_w_in, v_o_dw_w, v_o_dw_b, v_o_ln_g, v_o_ln_b, v_o_a_re, v_o_a_im, v_o_b_re, v_o_b_im, v_o_c_re, v_o_c_im, v_o_d, v_o_log_step, v_o_glu_w, v_o_w_out, v_ffn_norm, v_ffn_w_up, v_ffn_dw_w, v_ffn_dw_b, v_ffn_w_down, v_final_norm):
    p = dict(locals())

    kinds = ("grad_", "delta_", "new_m_", "new_v_")

    def block(name, layer):
        return p[name][0 if layer is None else layer]

    srcs = [block(n, layer).astype(BF16) for n, layer, _ in MATMUL_BLOCKS] + [_pack([p[n] for n in SMALL_SHARDED], F32, 16)]
    gathered = _gather(srcs, "gather_weights")
    w = {n: p[n] for n in REPLICATED}
    for (n, layer, by_cols), g in zip(MATMUL_BLOCKS, gathered):
        if by_cols:
            full = _join_cols(g, E_IN_PAD if n == 'e_w_in' else N_DEV * g.shape[2], f"join_{n}_{layer}")
        else:
            full = g.reshape(N_DEV * g.shape[1], g.shape[2])
        if layer is None:
            w[n] = full
        else:
            w.setdefault(n, [None, None])[layer] = full
    for n, piece in zip(SMALL_SHARDED, _unpack(gathered[-1], [p[n].shape for n in SMALL_SHARDED], lead=(N_DEV,))):
        w[n] = _join_shards(piece, SHARDED[n])

    loss, dx, grads = _local_step(x[0], loss_target[0], w)
    loss = lax.psum(loss, MESH_AXES)

    sends = []
    for n, layer, by_cols in MATMUL_BLOCKS:
        g = grads[n] if layer is None else grads[n][layer]
        if by_cols:
            sends.append(_split_cols(g, block(n, layer).shape[1], BF16, f"split_{n}_{layer}"))
        else:
            sends.append(g.astype(BF16).reshape(N_DEV, -1, g.shape[1]))
    sends.append(_pack([_split_shards(grads[n].reshape(p[n].shape[:SHARDED[n]] + (-1,) + p[n].shape[SHARDED[n] + 1:]), SHARDED[n])
                        for n in SMALL_SHARDED], F32, 128, lead=(N_DEV,)))
    core = lax.axis_index("c")
    by_core = [s.reshape((N_CHIPS, 2) + s.shape[1:]) for s in sends]
    keep = [lax.dynamic_index_in_dim(s, core, axis=1, keepdims=False) for s in by_core]
    give = [lax.dynamic_index_in_dim(s, 1 - core, axis=1, keepdims=False) for s in by_core]
    got = _swap_sibling(give, "swap_sibling_grads")
    chip_sums = [_add(a, b, f"chip_sum_{i}") for i, (a, b) in enumerate(zip(keep, got))]
    parts = list(_exchange_chips(chip_sums, "exchange_chip_grads"))
    parts.append(_gather([_pack([grads[n].reshape(p[n].shape) for n in REPLICATED], F32, 128)], "gather_replicated_grads")[0])

    out, by_layer = {}, {}
    for (n, layer, _), part in zip(MATMUL_BLOCKS, parts):
        by_layer.setdefault(n, {})[layer] = _adamw(part, *[block(pre + n, layer) for pre in ("", "m_", "v_")], f"adamw_{n}_{layer}")
    for n, res in by_layer.items():
        for i, kind in enumerate(kinds):
            out[kind + n] = res[None][i][None] if None in res else jnp.stack([res[0][i], res[1][i]], axis=0)
    for names, part, tag in ((SMALL_SHARDED, parts[-2], "small"), (REPLICATED, parts[-1], "replicated")):
        packed = [_pack([p[pre + n] for n in names], F32, 128) for pre in ("", "m_", "v_")]
        for kind, buf in zip(kinds, _adamw(part, *packed, "adamw_" + tag)):
            for n, a in zip(names, _unpack(buf, [p[n].shape for n in names])):
                out[kind + n] = a
    return (loss, dx[None], *[out[kind + n] for kind in kinds for n in WEIGHTS])
```

```python
import functools
import math

import jax
import jax.numpy as jnp
from jax import lax
from jax.experimental import pallas as pl
from jax.experimental.pallas import tpu as pltpu

F32, BF16 = jnp.float32, jnp.bfloat16
HIGHEST = lax.Precision.HIGHEST
N_DEV = 8
MESH_AXES = ("x", "y", "c")
VMEM_LIMIT = 48 * 1024 * 1024
LANES = 128
PACK_COLS = 1024

D_MODEL = 1024
EPS = 1e-6
RET_HEADS, RET_DK, RET_DV = 4, 128, 256
SSM_HEADS, SSM_P, SSM_N, SSM_GROUPS = 16, 64, 128, 2
SSM_HG = SSM_HEADS // SSM_GROUPS
S5_GROUPS, S5_GROUP, S5_STATE = 32, 16, 64
S5_COLS = S5_GROUPS * S5_STATE
D_FF = 2816
E_IN, E_IN_PAD = 5648, 5760
ADAM_LR, ADAM_B1, ADAM_B2, ADAM_EPS, ADAM_WD, ADAM_STEP = 0.001, 0.9, 0.999, 1e-08, 0.01, 10

WEIGHTS = ['mix_norm', 'e_w_in', 'e_conv_w', 'e_conv_b', 'e_dt_bias', 'e_a_log', 'e_d', 'e_ssm_norm', 'e_w_out', 'o_w_in', 'o_dw_w', 'o_dw_b', 'o_ln_g', 'o_ln_b', 'o_a_re', 'o_a_im', 'o_b_re', 'o_b_im', 'o_c_re', 'o_c_im', 'o_d', 'o_log_step', 'o_glu_w', 'o_w_out', 'ffn_norm', 'ffn_w_up', 'ffn_dw_w', 'ffn_dw_b', 'ffn_w_down', 'final_norm']
SHARDED = {'e_w_in': 2, 'e_conv_w': 2, 'e_w_out': 1, 'o_w_in': 2, 'o_dw_w': 2, 'o_dw_b': 1, 'o_ln_g': 1, 'o_ln_b': 1,
           'o_d': 1, 'o_glu_w': 1, 'o_w_out': 1, 'ffn_w_up': 2, 'ffn_dw_w': 2, 'ffn_w_down': 1}
MATMUL_WEIGHTS = ['e_w_in', 'e_w_out', 'o_w_in', 'o_glu_w', 'o_w_out', 'ffn_w_up', 'ffn_w_down']
MATMUL_BLOCKS = [('e_w_in', None, True), ('e_w_out', None, False), ('o_w_in', None, True), ('o_glu_w', None, False),
                 ('o_w_out', None, False), ('ffn_w_up', 0, True), ('ffn_w_up', 1, True), ('ffn_w_down', 0, False), ('ffn_w_down', 1, False)]
SMALL_SHARDED = [n for n in WEIGHTS if n in SHARDED and n not in MATMUL_WEIGHTS]
REPLICATED = [n for n in WEIGHTS if n not in SHARDED]


def _call(fn, grid, ins, outs, name):
    n_in = len(ins)

    def body(*refs):
        vals = fn(*[r[...] for r in refs[:n_in]])
        first = pl.program_id(0) == 0
        for r, v, o in zip(refs[n_in:], vals, outs):
            if o[4]:
                @pl.when(first)
                def _():
                    r[...] = jnp.zeros_like(r)
                r[...] += v.astype(r.dtype)
            else:
                r[...] = v.astype(r.dtype)

    return pl.pallas_call(
        body, grid=grid,
        in_specs=[pl.BlockSpec(b, m) for _, b, m in ins],
        out_specs=[pl.BlockSpec(o[2], o[3]) for o in outs],
        out_shape=[jax.ShapeDtypeStruct(o[0], o[1]) for o in outs],
        compiler_params=pltpu.CompilerParams(dimension_semantics=("arbitrary",) * len(grid), vmem_limit_bytes=VMEM_LIMIT),
        name=name)(*[a for a, _, _ in ins])


def _rows(fn, n_rows, row_ins, full_ins, row_outs, acc_outs, name, tm=256, bf16_outs=()):
    tm = min(tm, n_rows)
    ins = [(a, (tm, w), (lambda i, c=c: (i, c))) for a, w, c in row_ins]
    ins += [(a, a.shape, (lambda i, n=a.ndim: (0,) * n)) for a in full_ins]
    outs = [((n_rows, w), BF16 if k in bf16_outs else F32, (tm, w), (lambda i: (i, 0)), False) for k, w in enumerate(row_outs)]
    outs += [(tuple(s), F32, tuple(s), (lambda i, n=len(s): (0,) * n), True) for s in acc_outs]
    return _call(fn, (n_rows // tm,), ins, outs, name)


def _cols(fn, n_blocks, col_ins, out_leads, name, cb=LANES, bf16_outs=()):
    ins = [(a, a.shape[:-1] + (cb,), (lambda j, n=a.ndim, o=o: (0,) * (n - 1) + (j + o,))) for a, o in col_ins]
    outs = [(tuple(s) + (n_blocks * cb,), BF16 if k in bf16_outs else F32, tuple(s) + (cb,), (lambda j, n=len(s): (0,) * n + (j,)), False)
            for k, s in enumerate(out_leads)]
    return _call(fn, (n_blocks,), ins, outs, name)


def _grad_fn(f, n_diff, n_in):
    def g(*a):
        diff, consts, cts = a[:n_diff], a[n_diff:n_in], a[n_in:]
        _, vjp = jax.vjp(lambda *d: f(*d, *consts), *diff)
        return vjp(tuple(cts))
    return g


def _silu(x):
    return x * jax.nn.sigmoid(x)


def _rms(x, g):
    return x * lax.rsqrt(jnp.mean(x * x, axis=-1, keepdims=True) + EPS) * g


@jax.custom_vjp
def _softplus(x):
    return jnp.maximum(x, 0.0) + jnp.log(1.0 + jnp.exp(-jnp.abs(x)))


_softplus.defvjp(lambda x: (_softplus(x), x), lambda x, g: (g * jax.nn.sigmoid(x),))


@jax.custom_vjp
def _swap_halves(x):
    return pltpu.roll(x, 64, 1)


_swap_halves.defvjp(lambda x: (_swap_halves(x), None), lambda _, g: (_swap_halves(g),))


def _shift_rows(x, k, up):
    if k == 0:
        return x
    n = x.shape[0]
    t = lax.broadcasted_iota(jnp.int32, x.shape, 0)
    if up:
        return jnp.where(t < n - k, pltpu.roll(x, n - k, 0), 0.0)
    return jnp.where(t >= k, pltpu.roll(x, k, 0), 0.0)


@jax.custom_vjp
def _dwconv(x, w, b):
    k_taps = w.shape[0]
    y = b + w[k_taps - 1] * x
    for k in range(k_taps - 1):
        y = y + w[k] * _shift_rows(x, k_taps - 1 - k, False)
    return y


def _dwconv_fwd(x, w, b):
    return _dwconv(x, w, b), (x, w)


def _dwconv_bwd(saved, dy):
    x, w = saved
    k_taps = w.shape[0]
    dx = w[k_taps - 1] * dy
    dws = []
    for k in range(k_taps - 1):
        s = k_taps - 1 - k
        dx = dx + w[k] * _shift_rows(dy, s, True)
        dws.append(jnp.sum(dy * _shift_rows(x, s, False), axis=0, keepdims=True)[None])
    dws.append(jnp.sum(dy * x, axis=0, keepdims=True)[None])
    return dx, jnp.concatenate(dws, axis=0), jnp.sum(dy, axis=0, keepdims=True)


_dwconv.defvjp(_dwconv_fwd, _dwconv_bwd)


def _f_rms(x, g):
    return (_rms(x, g),)


def _rot(x, cosf, sins):
    outs = []
    for h in range(RET_HEADS):
        xh = x[:, h * RET_DK:(h + 1) * RET_DK]
        outs.append(xh * cosf + _swap_halves(xh) * sins)
    return jnp.concatenate(outs, axis=1)


def _f_retpre(q, k, cosf, sins):
    return _rot(q, cosf, sins), _rot(k, cosf, sins) * (RET_DK ** -0.5)


def _f_retpost(r, g):
    outs = []
    for h in range(RET_HEADS):
        rh = r[:, h * RET_DV:(h + 1) * RET_DV]
        rc = rh - jnp.mean(rh, axis=-1, keepdims=True)
        outs.append(_silu(g[:, h * RET_DV:(h + 1) * RET_DV]) * (rc * lax.rsqrt(jnp.mean(rc * rc, axis=-1, keepdims=True) + EPS)))
    return (jnp.concatenate(outs, axis=1),)


def _f_ssdconv(xbc, w, b):
    return (_silu(_dwconv(xbc, w, b)),)


def _f_ssdpre(xs, dtr, bias, alog, expand):
    dt = _softplus(dtr + bias)
    return xs * jnp.dot(dt, expand, precision=HIGHEST, preferred_element_type=F32), dt * (-jnp.exp(alog))


def _f_ssdpost(yc, xs, z, dskip, norm_w, expand):
    d_wide = jnp.dot(jnp.broadcast_to(dskip, (yc.shape[0], LANES)), expand, precision=HIGHEST, preferred_element_type=F32)
    y = (yc + d_wide * xs) * _silu(z)
    half = y.shape[1] // SSM_GROUPS
    outs = []
    for g in range(SSM_GROUPS):
        yg = y[:, g * half:(g + 1) * half]
        outs.append(yg * lax.rsqrt(jnp.mean(yg * yg, axis=-1, keepdims=True) + EPS))
    return (jnp.concatenate(outs, axis=1) * norm_w,)


def _f_ffnmid(gin, uin, wg, wu, bg, bu):
    return (_silu(_dwconv(gin, wg, bg)) * _dwconv(uin, wu, bu),)


def _f_confb(c1, g, b):
    mu = jnp.mean(c1, axis=-1, keepdims=True)
    xc = c1 - mu
    return (_silu(xc * lax.rsqrt(jnp.mean(xc * xc, axis=-1, keepdims=True) + EPS) * g + b),)


def _f_s5post(yc, u, dskip, glu_w):
    s = jax.nn.gelu(yc + dskip * u)
    z = jnp.dot(s.astype(BF16), glu_w.astype(BF16), preferred_element_type=F32)
    return (s * jax.nn.sigmoid(z),)


def _f_s5par(lr, li, ls, br, bi):
    step = jnp.exp(ls)
    mag = jnp.exp(lr * step)
    ab_re = mag * jnp.cos(li * step)
    ab_im = mag * jnp.sin(li * step)
    den = lr * lr + li * li
    f_re = ((ab_re - 1.0) * lr + ab_im * li) / den
    f_im = (ab_im * lr - (ab_re - 1.0) * li) / den
    return ab_re, ab_im, f_re[None] * br - f_im[None] * bi, f_re[None] * bi + f_im[None] * br


def _loss_step(x, tgt, g):
    def f(x_, g_):
        e = _rms(x_, g_) - tgt
        return 0.5 * jnp.sum(jnp.mean(e * e, axis=-1, keepdims=True), axis=0, keepdims=True)
    loss, vjp = jax.vjp(f, x, g)
    dx, dg = vjp(jnp.ones((1, 1), F32))
    return dx, dx, jnp.broadcast_to(loss, (8, LANES)), dg


def _tile(n, pref):
    if n <= pref:
        return n
    t = (pref // LANES) * LANES
    while n % t:
        t -= LANES
    return t


_DIMS = {"nn": (((1,), (0,)), ((), ())), "nt": (((1,), (1,)), ((), ())), "tn": (((0,), (0,)), ((), ()))}


def _mm(a, b, mode, name, res=None, out_dtype=F32, tm=1024, tn=1408, tk=1408):
    if mode == "nn":
        (m, k), n = a.shape, b.shape[1]
    elif mode == "nt":
        (m, k), n = a.shape, b.shape[0]
    else:
        (k, m), n = a.shape, b.shape[1]
    tm, tn, tk = _tile(m, tm), _tile(n, tn), _tile(k, tk)
    nk = k // tk
    a_spec = pl.BlockSpec((tk, tm), lambda i, j, kk: (kk, i)) if mode == "tn" else pl.BlockSpec((tm, tk), lambda i, j, kk: (i, kk))
    b_spec = pl.BlockSpec((tn, tk), lambda i, j, kk: (j, kk)) if mode == "nt" else pl.BlockSpec((tk, tn), lambda i, j, kk: (kk, j))
    o_spec = pl.BlockSpec((tm, tn), lambda i, j, kk: (i, j))
    has_res = res is not None

    def body(*refs):
        a_ref, b_ref = refs[0], refs[1]
        o_ref, acc = refs[-2], refs[-1]
        kk = pl.program_id(2)

        @pl.when(kk == 0)
        def _():
            acc[...] = jnp.zeros_like(acc)

        acc[...] += lax.dot_general(a_ref[...].astype(BF16), b_ref[...].astype(BF16), _DIMS[mode], preferred_element_type=F32)

        @pl.when(kk == nk - 1)
        def _():
            o_ref[...] = (acc[...] + refs[2][...] if has_res else acc[...]).astype(out_dtype)

    return pl.pallas_call(
        body, grid=(m // tm, n // tn, nk),
        in_specs=[a_spec, b_spec] + ([o_spec] if has_res else []),
        out_specs=o_spec, out_shape=jax.ShapeDtypeStruct((m, n), out_dtype),
        scratch_shapes=[pltpu.VMEM((tm, tn), F32)],
        compiler_params=pltpu.CompilerParams(dimension_semantics=("parallel", "parallel", "arbitrary"), vmem_limit_bytes=VMEM_LIMIT),
        name=name)(*([a, b] + ([res] if has_res else [])))


def _seq_block(t):
    return min(512, t)


def _causal_diff(i, j, blk):
    r = lax.broadcasted_iota(jnp.int32, (blk, blk), 0)
    c = lax.broadcasted_iota(jnp.int32, (blk, blk), 1)
    return (i - j) * blk + r - c


def _ret_decay(lg, i, j, blk):
    diff = _causal_diff(i, j, blk)
    return jnp.where(diff >= 0, jnp.exp(lg * jnp.maximum(diff, 0).astype(F32)), 0.0)


def _pair_call(body, lead_grid, nb, key_major, in_specs, out_specs, out_shape, scratch, name, args, gather=()):
    pairs = [(i, j) for j in range(nb) for i in range(j, nb)] if key_major else [(i, j) for i in range(nb) for j in range(i + 1)]
    tables = [jnp.array([p[k] for p in pairs], jnp.int32) for k in (0, 1)]
    lead = len(lead_grid)
    n_steps = math.prod(lead_grid) * len(pairs)
    n_bg = len(gather)

    def on_pairs(spec):
        if spec.block_shape is None:
            return spec

        def index_map(*a):
            i, j = a[lead + 1][a[lead]], a[lead + 2][a[lead]]
            return spec.index_map(*a[:lead], *((j, i) if key_major else (i, j)))

        return pl.BlockSpec(spec.block_shape, index_map)

    many = isinstance(out_specs, (list, tuple))
    out_specs, out_shape = (list(out_specs), list(out_shape)) if many else ([out_specs], [out_shape])
    n_in, n_out = len(in_specs), len(out_specs)

    def wrapped(i_ref, j_ref, *refs):
        p = pl.program_id(lead)
        own = refs[:n_in] + refs[n_in + n_bg:n_in + n_bg + n_out] + refs[n_in + 2 * n_bg + n_out:len(refs) - (3 if n_bg else 0)]
        if n_bg:
            step = p + (pl.program_id(0) * len(pairs) if lead else 0)
            start, forward, finish = _gather_phases(refs[n_in:n_in + n_bg], refs[n_in + n_bg + n_out:n_in + 2 * n_bg + n_out], *refs[-3:])
            pl.when(step == 0)(start)
        body(i_ref[p], j_ref[p], *own)
        if n_bg:
            pl.when(step == n_steps // 2)(forward)
            pl.when(step == n_steps - 1)(finish)

    sems = [pltpu.SemaphoreType.DMA((GATHER_SEMS * n_bg,))] * 2 + [pltpu.SemaphoreType.DMA((n_bg,))] if n_bg else []
    grid_spec = pltpu.PrefetchScalarGridSpec(
        num_scalar_prefetch=2, grid=tuple(lead_grid) + (len(pairs),), in_specs=[on_pairs(s) for s in in_specs] + [_HBM] * n_bg,
        out_specs=[on_pairs(s) for s in out_specs] + [_HBM] * n_bg, scratch_shapes=list(scratch) + sems)
    res = pl.pallas_call(
        wrapped, grid_spec=grid_spec, out_shape=out_shape + [jax.ShapeDtypeStruct((N_DEV,) + g.shape, g.dtype) for g in gather],
        compiler_params=pltpu.CompilerParams(dimension_semantics=("arbitrary",) * (lead + 1), vmem_limit_bytes=VMEM_LIMIT),
        name=name)(*tables, *args, *gather)
    return res if many or n_bg else res[0]


def _ret_row_decays(lg, i, j, blk):
    row = lax.broadcasted_iota(jnp.int32, (blk, RET_DK), 0)
    return jnp.exp(lg * row.astype(F32)), jnp.exp(lg * ((i - j) * blk - row).astype(F32))


def _ret_scaled(lg, i, j, blk, q_ref, k_ref):
    a, b = _ret_row_decays(lg, i, j, blk)
    return (q_ref[...] * a).astype(BF16), (k_ref[...] * b).astype(BF16)


def _ret_log_gamma():
    return jnp.log1p(-(2.0 ** (-5.0 - jnp.arange(RET_HEADS, dtype=F32))))


def _ret_fwd(qr, kr, proj, t, gather=()):
    blk = _seq_block(t)
    nb = t // blk
    v_off = (2 * RET_HEADS * RET_DK) // RET_DV

    def body(i, j, lg_ref, q_ref, k_ref, v_ref, o_ref, acc):
        h = pl.program_id(0)

        @pl.when(j == 0)
        def _():
            acc[...] = jnp.zeros_like(acc)

        @pl.when(j < i)
        def _():
            qa, kb = _ret_scaled(lg_ref[h], i, j, blk, q_ref, k_ref)
            p = lax.dot_general(qa, kb, _DIMS["nt"], preferred_element_type=F32).astype(BF16)
            acc[...] += jnp.dot(p, v_ref[...].astype(BF16), preferred_element_type=F32)

        @pl.when(j == i)
        def _():
            s = lax.dot_general(q_ref[...].astype(BF16), k_ref[...].astype(BF16), _DIMS["nt"], preferred_element_type=F32)
            p = (s * _ret_decay(lg_ref[h], i, j, blk)).astype(BF16)
            o_ref[...] = acc[...] + jnp.dot(p, v_ref[...].astype(BF16), preferred_element_type=F32)

    return _pair_call(
        body, (RET_HEADS,), nb, False,
        [pl.BlockSpec(memory_space=pltpu.SMEM),
         pl.BlockSpec((blk, RET_DK), lambda h, i, j: (i, h)),
         pl.BlockSpec((blk, RET_DK), lambda h, i, j: (j, h)),
         pl.BlockSpec((blk, RET_DV), lambda h, i, j: (j, v_off + h))],
        pl.BlockSpec((blk, RET_DV), lambda h, i, j: (i, h)),
        jax.ShapeDtypeStruct((t, RET_HEADS * RET_DV), F32), [pltpu.VMEM((blk, RET_DV), F32)],
        "ret_fwd", (_ret_log_gamma(), qr, kr, proj), gather)


def _ret_bwd(qr, kr, proj, dr, t):
    blk = _seq_block(t)
    nb = t // blk
    v_off = (2 * RET_HEADS * RET_DK) // RET_DV

    def dq_body(i, j, lg_ref, q_ref, k_ref, v_ref, do_ref, dq_ref, acc):
        h = pl.program_id(0)

        @pl.when(j == 0)
        def _():
            acc[...] = jnp.zeros_like(acc)

        @pl.when(j < i)
        def _():
            a, b = _ret_row_decays(lg_ref[h], i, j, blk)
            ds = lax.dot_general(do_ref[...].astype(BF16), v_ref[...].astype(BF16), _DIMS["nt"], preferred_element_type=F32)
            acc[...] += a * jnp.dot(ds.astype(BF16), (k_ref[...] * b).astype(BF16), preferred_element_type=F32)

        @pl.when(j == i)
        def _():
            ds = lax.dot_general(do_ref[...].astype(BF16), v_ref[...].astype(BF16), _DIMS["nt"], preferred_element_type=F32)
            dsm = (ds * _ret_decay(lg_ref[h], i, j, blk)).astype(BF16)
            dq_ref[...] = acc[...] + jnp.dot(dsm, k_ref[...].astype(BF16), preferred_element_type=F32)

    dq = _pair_call(
        dq_body, (RET_HEADS,), nb, False,
        [pl.BlockSpec(memory_space=pltpu.SMEM),
         pl.BlockSpec((blk, RET_DK), lambda h, i, j: (i, h)),
         pl.BlockSpec((blk, RET_DK), lambda h, i, j: (j, h)),
         pl.BlockSpec((blk, RET_DV), lambda h, i, j: (j, v_off + h)),
         pl.BlockSpec((blk, RET_DV), lambda h, i, j: (i, h))],
        pl.BlockSpec((blk, RET_DK), lambda h, i, j: (i, h)),
        jax.ShapeDtypeStruct((t, RET_HEADS * RET_DK), F32), [pltpu.VMEM((blk, RET_DK), F32)],
        "ret_bwd_dq", (_ret_log_gamma(), qr, kr, proj, dr))

    def dkv_body(i, j, lg_ref, q_ref, k_ref, v_ref, do_ref, dk_ref, dv_ref, acc_k, acc_v):
        h = pl.program_id(0)

        @pl.when(i == j)
        def _():
            acc_k[...] = jnp.zeros_like(acc_k)
            acc_v[...] = jnp.zeros_like(acc_v)

        @pl.when(i > j)
        def _():
            a, b = _ret_row_decays(lg_ref[h], i, j, blk)
            qa, kb = (q_ref[...] * a).astype(BF16), (k_ref[...] * b).astype(BF16)
            do = do_ref[...].astype(BF16)
            p = lax.dot_general(qa, kb, _DIMS["nt"], preferred_element_type=F32).astype(BF16)
            acc_v[...] += lax.dot_general(p, do, _DIMS["tn"], preferred_element_type=F32)
            ds = lax.dot_general(do, v_ref[...].astype(BF16), _DIMS["nt"], preferred_element_type=F32).astype(BF16)
            acc_k[...] += b * lax.dot_general(ds, qa, _DIMS["tn"], preferred_element_type=F32)

        @pl.when(i == j)
        def _():
            q = q_ref[...].astype(BF16)
            do = do_ref[...].astype(BF16)
            decay = _ret_decay(lg_ref[h], i, j, blk)
            s = lax.dot_general(q, k_ref[...].astype(BF16), _DIMS["nt"], preferred_element_type=F32)
            acc_v[...] += lax.dot_general((s * decay).astype(BF16), do, _DIMS["tn"], preferred_element_type=F32)
            ds = lax.dot_general(do, v_ref[...].astype(BF16), _DIMS["nt"], preferred_element_type=F32)
            acc_k[...] += lax.dot_general((ds * decay).astype(BF16), q, _DIMS["tn"], preferred_element_type=F32)

        @pl.when(i == nb - 1)
        def _():
            dk_ref[...] = acc_k[...]
            dv_ref[...] = acc_v[...].astype(BF16)

    dk, dv = _pair_call(
        dkv_body, (RET_HEADS,), nb, True,
        [pl.BlockSpec(memory_space=pltpu.SMEM),
         pl.BlockSpec((blk, RET_DK), lambda h, j, i: (i, h)),
         pl.BlockSpec((blk, RET_DK), lambda h, j, i: (j, h)),
         pl.BlockSpec((blk, RET_DV), lambda h, j, i: (j, v_off + h)),
         pl.BlockSpec((blk, RET_DV), lambda h, j, i: (i, h))],
        [pl.BlockSpec((blk, RET_DK), lambda h, j, i: (j, h)), pl.BlockSpec((blk, RET_DV), lambda h, j, i: (j, h))],
        [jax.ShapeDtypeStruct((t, RET_HEADS * RET_DK), F32), jax.ShapeDtypeStruct((t, RET_HEADS * RET_DV), BF16)],
        [pltpu.VMEM((blk, RET_DK), F32), pltpu.VMEM((blk, RET_DV), F32)],
        "ret_bwd_dkv", (_ret_log_gamma(), qr, kr, proj, dr))
    return dq, dk, dv


def _cumsum(x, reverse, name):
    t = x.shape[0]
    blk = _seq_block(t)
    nb = t // blk

    def body(x_ref, o_ref, carry):
        @pl.when(pl.program_id(0) == 0)
        def _():
            carry[...] = jnp.zeros_like(carry)

        r = lax.broadcasted_iota(jnp.int32, (blk, blk), 0)
        c = lax.broadcasted_iota(jnp.int32, (blk, blk), 1)
        tri = ((r <= c) if reverse else (r >= c)).astype(F32)
        o_ref[...] = jnp.dot(tri, x_ref[...], precision=HIGHEST, preferred_element_type=F32) + carry[...]
        carry[...] = o_ref[0:1, :] if reverse else o_ref[blk - 1:blk, :]

    idx = (lambda i: (nb - 1 - i, 0)) if reverse else (lambda i: (i, 0))
    return pl.pallas_call(
        body, grid=(nb,), in_specs=[pl.BlockSpec((blk, LANES), idx)], out_specs=pl.BlockSpec((blk, LANES), idx),
        out_shape=jax.ShapeDtypeStruct((t, LANES), F32), scratch_shapes=[pltpu.VMEM((1, LANES), F32)],
        compiler_params=pltpu.CompilerParams(dimension_semantics=("arbitrary",), vmem_limit_bytes=VMEM_LIMIT),
        name=name)(x)


def _ssd_decay(a_ref, at_ref, hh, mask):
    return jnp.exp(jnp.where(mask, a_ref[:, hh:hh + 1] - at_ref[hh:hh + 1, :], -jnp.inf))


def _head_lanes(s):
    lane = lax.broadcasted_iota(jnp.int32, (s.shape[0], LANES), 1)
    return jnp.concatenate([jnp.where(lane < SSM_P, s[:, 2 * p:2 * p + 1], s[:, 2 * p + 1:2 * p + 2])
                            for p in range(SSM_HEADS // 2)], axis=1)


_B_OFF, _C_OFF = 1024 // SSM_N, 1024 // SSM_N + SSM_GROUPS


def _ssd_fwd(xbc_act, xdt, a_cum, a_cum_t, t, gather=()):
    blk = _seq_block(t)
    nb = t // blk

    def body(i, j, c0, c1, b0, b1, x_ref, ai_ref, aj_ref, at_ref, o_ref, acc):
        @pl.when(j == 0)
        def _():
            acc[...] = jnp.zeros_like(acc)

        @pl.when(j < i)
        def _():
            xv = (x_ref[...] * _head_lanes(jnp.exp(ai_ref[0:1, :] - aj_ref[...]))).astype(BF16)
            for g, (c_ref, b_ref) in enumerate(((c0, b0), (c1, b1))):
                cb = lax.dot_general(c_ref[...].astype(BF16), b_ref[...].astype(BF16), _DIMS["nt"], preferred_element_type=F32)
                cols = slice(g * half, (g + 1) * half)
                acc[:, cols] += jnp.dot(cb.astype(BF16), xv[:, cols], preferred_element_type=F32)

        @pl.when(j == i)
        def _():
            o_ref[...] = acc[...] * _head_lanes(jnp.exp(ai_ref[...] - ai_ref[0:1, :]))
            mask = _causal_diff(i, j, blk) >= 0
            for g, (c_ref, b_ref) in enumerate(((c0, b0), (c1, b1))):
                cb = lax.dot_general(c_ref[...].astype(BF16), b_ref[...].astype(BF16), _DIMS["nt"], preferred_element_type=F32)
                for h in range(SSM_HG):
                    hh = g * SSM_HG + h
                    cols = slice(hh * SSM_P, (hh + 1) * SSM_P)
                    m = (cb * _ssd_decay(ai_ref, at_ref, hh, mask)).astype(BF16)
                    o_ref[:, cols] += jnp.dot(m, x_ref[:, cols].astype(BF16), preferred_element_type=F32)

    half = SSM_HG * SSM_P
    row_i = lambda off: pl.BlockSpec((blk, SSM_N), lambda i, j, off=off: (i, off))
    row_j = lambda off: pl.BlockSpec((blk, SSM_N), lambda i, j, off=off: (jnp.minimum(j, i), off))
    return _pair_call(
        body, (), nb, False,
        [row_i(_C_OFF), row_i(_C_OFF + 1), row_j(_B_OFF), row_j(_B_OFF + 1),
         pl.BlockSpec((blk, SSM_HEADS * SSM_P), lambda i, j: (j, 0)),
         pl.BlockSpec((blk, LANES), lambda i, j: (i, 0)),
         pl.BlockSpec((blk, LANES), lambda i, j: (j, 0)),
         pl.BlockSpec((SSM_HEADS, blk), lambda i, j: (0, j))],
        pl.BlockSpec((blk, SSM_HEADS * SSM_P), lambda i, j: (i, 0)),
        jax.ShapeDtypeStruct((t, SSM_HEADS * SSM_P), F32), [pltpu.VMEM((blk, SSM_HEADS * SSM_P), F32)],
        "ssd_fwd", (xbc_act, xbc_act, xbc_act, xbc_act, xdt, a_cum, a_cum, a_cum_t), gather)


def _ssd_bwd(xbc_act, xdt, a_cum, a_cum_t, dy, expand, t):
    blk = _seq_block(t)
    nb = t // blk
    width = SSM_HEADS * SSM_P

    half = SSM_HG * SSM_P

    def head_sums(prod, e_ref):
        return lax.dot_general(prod, e_ref[...], _DIMS["nt"], precision=HIGHEST, preferred_element_type=F32)

    def q_body(i, j, c0, c1, b0, b1, x_ref, ai_ref, aj_ref, at_ref, dy_ref, e_ref, dc_ref, da_ref, acc_c, acc_a, acc_p, dyu):
        @pl.when(j == 0)
        def _():
            acc_c[...] = jnp.zeros_like(acc_c)
            acc_a[...] = jnp.zeros_like(acc_a)
            acc_p[...] = jnp.zeros_like(acc_p)
            dyu[...] = (dy_ref[...].astype(F32) * _head_lanes(jnp.exp(ai_ref[...] - ai_ref[0:1, :]))).astype(BF16)

        @pl.when(j < i)
        def _():
            xv = (x_ref[...] * _head_lanes(jnp.exp(ai_ref[0:1, :] - aj_ref[...]))).astype(BF16)
            for g, (c_ref, b_ref) in enumerate(((c0, b0), (c1, b1))):
                cols = slice(g * half, (g + 1) * half)
                bj = b_ref[...].astype(BF16)
                cb = lax.dot_general(c_ref[...].astype(BF16), bj, _DIMS["nt"], preferred_element_type=F32).astype(BF16)
                dcb = lax.dot_general(dyu[:, cols], xv[:, cols], _DIMS["nt"], preferred_element_type=F32)
                acc_c[:, g * SSM_N:(g + 1) * SSM_N] += jnp.dot(dcb.astype(BF16), bj, preferred_element_type=F32)
                acc_p[:, cols] += dyu[:, cols].astype(F32) * jnp.dot(cb, xv[:, cols], preferred_element_type=F32)

        @pl.when(j == i)
        def _():
            mask = _causal_diff(i, j, blk) >= 0
            for g, (c_ref, b_ref) in enumerate(((c0, b0), (c1, b1))):
                bj = b_ref[...].astype(BF16)
                cb = lax.dot_general(c_ref[...].astype(BF16), bj, _DIMS["nt"], preferred_element_type=F32)
                dcb = jnp.zeros((blk, blk), F32)
                for h in range(SSM_HG):
                    hh = g * SSM_HG + h
                    cols = slice(hh * SSM_P, (hh + 1) * SSM_P)
                    dm = lax.dot_general(dy_ref[:, cols].astype(BF16), x_ref[:, cols].astype(BF16), _DIMS["nt"],
                                         preferred_element_type=F32) * _ssd_decay(ai_ref, at_ref, hh, mask)
                    dcb = dcb + dm
                    acc_a[:, hh:hh + 1] += jnp.sum(dm * cb, axis=1, keepdims=True)
                acc_c[:, g * SSM_N:(g + 1) * SSM_N] += jnp.dot(dcb.astype(BF16), bj, preferred_element_type=F32)
            dc_ref[...] = acc_c[...]
            da_ref[...] = acc_a[...] + head_sums(acc_p[...], e_ref)

    row_i = lambda off: pl.BlockSpec((blk, SSM_N), lambda i, j, off=off: (i, off))
    row_j = lambda off: pl.BlockSpec((blk, SSM_N), lambda i, j, off=off: (jnp.minimum(j, i), off))
    e_spec = pl.BlockSpec((LANES, width), lambda i, j: (0, 0))
    dc, da_q = _pair_call(
        q_body, (), nb, False,
        [row_i(_C_OFF), row_i(_C_OFF + 1), row_j(_B_OFF), row_j(_B_OFF + 1),
         pl.BlockSpec((blk, width), lambda i, j: (j, 0)),
         pl.BlockSpec((blk, LANES), lambda i, j: (i, 0)),
         pl.BlockSpec((blk, LANES), lambda i, j: (j, 0)),
         pl.BlockSpec((SSM_HEADS, blk), lambda i, j: (0, j)),
         pl.BlockSpec((blk, width), lambda i, j: (i, 0)), e_spec],
        [pl.BlockSpec((blk, SSM_GROUPS * SSM_N), lambda i, j: (i, 0)), pl.BlockSpec((blk, LANES), lambda i, j: (i, 0))],
        [jax.ShapeDtypeStruct((t, SSM_GROUPS * SSM_N), F32), jax.ShapeDtypeStruct((t, LANES), F32)],
        [pltpu.VMEM((blk, SSM_GROUPS * SSM_N), F32), pltpu.VMEM((blk, LANES), F32), pltpu.VMEM((blk, width), F32),
         pltpu.VMEM((blk, width), BF16)],
        "ssd_bwd_q", (xbc_act, xbc_act, xbc_act, xbc_act, xdt, a_cum, a_cum, a_cum_t, dy, expand))

    def k_body(i, j, c0, c1, b0, b1, x_ref, ai_ref, aj_ref, at_ref, dy_ref, e_ref, db_ref, dx_ref, da_ref, dat_ref, acc_b, acc_x, acc_a,
               acc_p):
        @pl.when(i == j)
        def _():
            acc_b[...] = jnp.zeros_like(acc_b)
            acc_x[...] = jnp.zeros_like(acc_x)
            acc_a[...] = jnp.zeros_like(acc_a)
            acc_p[...] = jnp.zeros_like(acc_p)

        @pl.when(i > j)
        def _():
            v = _head_lanes(jnp.exp(ai_ref[0:1, :] - aj_ref[...]))
            dyu_all = (dy_ref[...].astype(F32) * _head_lanes(jnp.exp(ai_ref[...] - ai_ref[0:1, :]))).astype(BF16)
            xv = (x_ref[...] * v).astype(BF16)
            for g, (c_ref, b_ref) in enumerate(((c0, b0), (c1, b1))):
                cols = slice(g * half, (g + 1) * half)
                ci = c_ref[...].astype(BF16)
                cb = lax.dot_general(ci, b_ref[...].astype(BF16), _DIMS["nt"], preferred_element_type=F32).astype(BF16)
                dcb = lax.dot_general(dyu_all[:, cols], xv[:, cols], _DIMS["nt"], preferred_element_type=F32)
                acc_b[:, g * SSM_N:(g + 1) * SSM_N] += lax.dot_general(dcb.astype(BF16), ci, _DIMS["tn"], preferred_element_type=F32)
                dxv = lax.dot_general(cb, dyu_all[:, cols], _DIMS["tn"], preferred_element_type=F32)
                acc_x[:, cols] += v[:, cols] * dxv
                acc_p[:, cols] += xv[:, cols].astype(F32) * dxv

        @pl.when(i == j)
        def _():
            mask = _causal_diff(i, j, blk) >= 0
            for g, (c_ref, b_ref) in enumerate(((c0, b0), (c1, b1))):
                ci = c_ref[...].astype(BF16)
                cb = lax.dot_general(ci, b_ref[...].astype(BF16), _DIMS["nt"], preferred_element_type=F32)
                dcb = jnp.zeros((blk, blk), F32)
                for h in range(SSM_HG):
                    hh = g * SSM_HG + h
                    cols = slice(hh * SSM_P, (hh + 1) * SSM_P)
                    decay = _ssd_decay(ai_ref, at_ref, hh, mask)
                    dyh = dy_ref[:, cols].astype(BF16)
                    acc_x[:, cols] += lax.dot_general((cb * decay).astype(BF16), dyh, _DIMS["tn"], preferred_element_type=F32)
                    dm = lax.dot_general(dyh, x_ref[:, cols].astype(BF16), _DIMS["nt"], preferred_element_type=F32) * decay
                    dcb = dcb + dm
                    acc_a[hh:hh + 1, :] += jnp.sum(dm * cb, axis=0, keepdims=True)
                acc_b[:, g * SSM_N:(g + 1) * SSM_N] += lax.dot_general(dcb.astype(BF16), ci, _DIMS["tn"], preferred_element_type=F32)

        @pl.when(i == nb - 1)
        def _():
            db_ref[...] = acc_b[...]
            dx_ref[...] = acc_x[...]
            dat_ref[...] = acc_a[...]
            da_ref[...] = head_sums(acc_p[...], e_ref)

    rowk_i = lambda off: pl.BlockSpec((blk, SSM_N), lambda j, i, off=off: (jnp.maximum(i, j), off))
    rowk_j = lambda off: pl.BlockSpec((blk, SSM_N), lambda j, i, off=off: (j, off))
    db, dx, da_k, da_k_t = _pair_call(
        k_body, (), nb, True,
        [rowk_i(_C_OFF), rowk_i(_C_OFF + 1), rowk_j(_B_OFF), rowk_j(_B_OFF + 1),
         pl.BlockSpec((blk, width), lambda j, i: (j, 0)),
         pl.BlockSpec((blk, LANES), lambda j, i: (i, 0)),
         pl.BlockSpec((blk, LANES), lambda j, i: (j, 0)),
         pl.BlockSpec((SSM_HEADS, blk), lambda j, i: (0, j)),
         pl.BlockSpec((blk, width), lambda j, i: (i, 0)), e_spec],
        [pl.BlockSpec((blk, SSM_GROUPS * SSM_N), lambda j, i: (j, 0)), pl.BlockSpec((blk, width), lambda j, i: (j, 0)),
         pl.BlockSpec((blk, LANES), lambda j, i: (j, 0)), pl.BlockSpec((SSM_HEADS, blk), lambda j, i: (0, j))],
        [jax.ShapeDtypeStruct((t, SSM_GROUPS * SSM_N), F32), jax.ShapeDtypeStruct((t, width), F32),
         jax.ShapeDtypeStruct((t, LANES), F32), jax.ShapeDtypeStruct((SSM_HEADS, t), F32)],
        [pltpu.VMEM((blk, SSM_GROUPS * SSM_N), F32), pltpu.VMEM((blk, width), F32), pltpu.VMEM((SSM_HEADS, blk), F32),
         pltpu.VMEM((blk, width), F32)],
        "ssd_bwd_k", (xbc_act, xbc_act, xbc_act, xbc_act, xdt, a_cum, a_cum, a_cum_t, dy, expand))
    return dc, da_q, db, dx, da_k, da_k_t


def _s5_scan(bu, a_re, a_im, reverse, x_prev=None):
    t = bu.shape[0]
    cb = 512
    ncb = S5_COLS // cb
    tb = min(1024, t)
    ntb = t // tb
    sub = 8
    shape = (sub, cb)

    def cmul(ar, ai, br, bi):
        return ar * br - ai * bi, ar * bi + ai * br

    def body(*refs):
        if reverse:
            br_ref, bi_ref, ar_ref, ai_ref, xr_ref, xi_ref, or_ref, oi_ref, dar_ref, dai_ref, carry, dacc, states = refs
        else:
            br_ref, bi_ref, ar_ref, ai_ref, or_ref, oi_ref, or_bf_ref, oi_bf_ref, carry = refs

        @pl.when(pl.program_id(1) == 0)
        def _():
            carry[...] = jnp.zeros_like(carry)
            if reverse:
                dacc[...] = jnp.zeros_like(dacc)

        row = lax.broadcasted_iota(jnp.int32, shape, 0)
        a1 = (jnp.broadcast_to(ar_ref[...], shape), jnp.broadcast_to(-ai_ref[...] if reverse else ai_ref[...], shape))
        a2 = cmul(*a1, *a1)
        a4 = cmul(*a2, *a2)
        a8 = cmul(*a4, *a4)
        steps = (sub - row) if reverse else (row + 1)
        pw = (jnp.ones(shape, F32), jnp.zeros(shape, F32))
        for bit, p in ((1, a1), (2, a2), (4, a4), (8, a8)):
            on = (steps & bit) != 0
            pw = cmul(*pw, jnp.where(on, p[0], 1.0), jnp.where(on, p[1], 0.0))
        edge = 0 if reverse else sub - 1

        def shift(v, s):
            if reverse:
                return jnp.where(row < sub - s, pltpu.roll(v, sub - s, 0), 0.0)
            return jnp.where(row >= s, pltpu.roll(v, s, 0), 0.0)

        def tile(n, state):
            r0 = pl.multiple_of((tb // sub - 1 - n if reverse else n) * sub, sub)
            rows = pl.ds(r0, sub)
            xr, xi = br_ref[rows, :], bi_ref[rows, :]
            for s, p in ((1, a1), (2, a2), (4, a4)):
                dr, di = cmul(*p, shift(xr, s), shift(xi, s))
                xr, xi = xr + dr, xi + di
            cr, ci = jnp.broadcast_to(state[0], shape), jnp.broadcast_to(state[1], shape)
            dr, di = cmul(*pw, cr, ci)
            xr, xi = xr + dr, xi + di
            if reverse:
                states[0, rows, :], states[1, rows, :] = xr, xi
                gr = jnp.where(row < sub - 1, pltpu.roll(xr, sub - 1, 0), cr)
                gi = jnp.where(row < sub - 1, pltpu.roll(xi, sub - 1, 0), ci)
                pr, pi = xr_ref[rows, :], xi_ref[rows, :]
                dacc[0] += gr * pr + gi * pi
                dacc[1] += gi * pr - gr * pi
            else:
                or_ref[rows, :], oi_ref[rows, :] = xr, xi
            return (jnp.sum(jnp.where(row == edge, xr, 0.0), axis=0, keepdims=True),
                    jnp.sum(jnp.where(row == edge, xi, 0.0), axis=0, keepdims=True))

        last = lax.fori_loop(0, tb // sub, tile, (carry[0], carry[1]))
        carry[0], carry[1] = last
        if reverse:
            or_ref[...], oi_ref[...] = states[0].astype(BF16), states[1].astype(BF16)
            dar_ref[...] = jnp.sum(dacc[0], axis=0, keepdims=True)
            dai_ref[...] = jnp.sum(dacc[1], axis=0, keepdims=True)
        else:
            or_bf_ref[...], oi_bf_ref[...] = or_ref[...].astype(BF16), oi_ref[...].astype(BF16)

    tblock = (lambda k: ntb - 1 - k) if reverse else (lambda k: k)
    re_spec = pl.BlockSpec((tb, cb), lambda j, k: (tblock(k), j))
    im_spec = pl.BlockSpec((tb, cb), lambda j, k: (tblock(k), j + ncb))
    a_spec = pl.BlockSpec((1, cb), lambda j, k: (0, j))
    ins, in_specs = [bu, bu, a_re, a_im], [re_spec, im_spec, a_spec, a_spec]
    seq_bf = jax.ShapeDtypeStruct((t, S5_COLS), BF16)
    scratch = [pltpu.VMEM((2, 1, cb), F32)]
    if reverse:
        ins += [x_prev, x_prev]
        in_specs += [re_spec, im_spec]
        out_shape = [seq_bf, seq_bf] + [jax.ShapeDtypeStruct((1, S5_COLS), F32)] * 2
        out_specs = [re_spec, re_spec, a_spec, a_spec]
        scratch += [pltpu.VMEM((2, sub, cb), F32), pltpu.VMEM((2, tb, cb), F32)]
    else:
        out_shape = [jax.ShapeDtypeStruct((t, S5_COLS), F32)] * 2 + [seq_bf, seq_bf]
        out_specs = [re_spec] * 4
    return pl.pallas_call(
        body, grid=(ncb, ntb), in_specs=in_specs, out_specs=out_specs, out_shape=out_shape, scratch_shapes=scratch,
        compiler_params=pltpu.CompilerParams(dimension_semantics=("arbitrary", "arbitrary"), vmem_limit_bytes=VMEM_LIMIT),
        name="s5_scan_bwd" if reverse else "s5_scan_fwd")(*ins)


CONF_DIM = 512
CONF_K = 31
CONF_PAD = 32


def _conf_conv(proj_o, w, b, dc1=None):
    t = proj_o.shape[0]
    cb = LANES
    ncb = CONF_DIM // cb
    chunk = min(512, t)
    chunks = range(0, t, chunk)
    bwd = dc1 is not None

    def body(*refs):
        if bwd:
            ca_ref, cg_ref, w_ref, b_ref, dy_ref, dca_ref, dcg_ref, dw_ref, db_ref, xs, dys = refs
        else:
            ca_ref, cg_ref, w_ref, b_ref, o_ref, xs = refs
        xs[pl.ds(0, CONF_PAD), :] = jnp.zeros((CONF_PAD, cb), F32)
        for t0 in chunks:
            rows = pl.ds(t0, chunk)
            xs[pl.ds(CONF_PAD + t0, chunk), :] = ca_ref[rows, :] * jax.nn.sigmoid(cg_ref[rows, :])
        if not bwd:
            for t0 in chunks:
                acc = jnp.broadcast_to(b_ref[...], (chunk, cb))
                for k in range(CONF_K):
                    acc = acc + w_ref[k] * xs[pl.ds(CONF_PAD + t0 - (CONF_K - 1 - k), chunk), :]
                o_ref[pl.ds(t0, chunk), :] = acc
            return
        dys[pl.ds(t, CONF_PAD), :] = jnp.zeros((CONF_PAD, cb), F32)
        db = jnp.zeros((1, cb), F32)
        for t0 in chunks:
            dys[pl.ds(t0, chunk), :] = dy_ref[pl.ds(t0, chunk), :]
            db = db + jnp.sum(dy_ref[pl.ds(t0, chunk), :], axis=0, keepdims=True)
        db_ref[...] = db
        for t0 in chunks:
            rows = pl.ds(t0, chunk)
            acc = jnp.zeros((chunk, cb), F32)
            for k in range(CONF_K):
                acc = acc + w_ref[k] * dys[pl.ds(t0 + (CONF_K - 1 - k), chunk), :]
            sig = jax.nn.sigmoid(cg_ref[rows, :])
            dca_ref[rows, :] = (acc * sig).astype(BF16)
            dcg_ref[rows, :] = (acc * ca_ref[rows, :] * sig * (1.0 - sig)).astype(BF16)
        for k in range(CONF_K):
            dwk = jnp.zeros((1, cb), F32)
            for t0 in chunks:
                window = xs[pl.ds(CONF_PAD + t0 - (CONF_K - 1 - k), chunk), :]
                dwk = dwk + jnp.sum(dy_ref[pl.ds(t0, chunk), :] * window, axis=0, keepdims=True)
            dw_ref[k] = dwk

    col = lambda off: pl.BlockSpec((t, cb), lambda j, off=off: (0, j + off))
    w_spec = pl.BlockSpec((CONF_K, 1, cb), lambda j: (0, 0, j))
    b_spec = pl.BlockSpec((1, cb), lambda j: (0, j))
    seq = jax.ShapeDtypeStruct((t, CONF_DIM), F32)
    ins, in_specs = [proj_o, proj_o, w, b], [col(0), col(ncb), w_spec, b_spec]
    scratch = [pltpu.VMEM((CONF_PAD + t, cb), F32)]
    if bwd:
        ins, in_specs = ins + [dc1], in_specs + [col(0)]
        seq_bf = jax.ShapeDtypeStruct((t, CONF_DIM), BF16)
        out_shape, out_specs = [seq_bf, seq_bf, jax.ShapeDtypeStruct(w.shape, F32), jax.ShapeDtypeStruct(b.shape, F32)], [col(0), col(0), w_spec, b_spec]
        scratch = scratch + [pltpu.VMEM((t + CONF_PAD, cb), F32)]
    else:
        out_shape, out_specs = seq, col(0)
    return pl.pallas_call(
        body, grid=(ncb,), in_specs=in_specs, out_specs=out_specs, out_shape=out_shape, scratch_shapes=scratch,
        compiler_params=pltpu.CompilerParams(dimension_semantics=("arbitrary",), vmem_limit_bytes=VMEM_LIMIT),
        name="conf_conv_bwd" if bwd else "conf_conv")(*ins)


N_CHIPS = 4
_HBM = pl.BlockSpec(memory_space=pl.ANY)
_MESH_ID = pl.DeviceIdType.MESH


def _comm_call(body, srcs, out_shapes, n_sems, name):
    n = len(srcs)
    return pl.pallas_call(
        body, out_shape=out_shapes, in_specs=[_HBM] * n, out_specs=[_HBM] * n,
        scratch_shapes=[pltpu.SemaphoreType.DMA((n_sems,)), pltpu.SemaphoreType.DMA((n_sems,)), pltpu.SemaphoreType.DMA((n,))],
        compiler_params=pltpu.CompilerParams(has_side_effects=True), name=name)(*srcs)


GATHER_SEMS = N_DEV - 1


def _gather_phases(src_refs, out_refs, send_sems, recv_sems, local_sems):
    n = len(src_refs)
    x, y, c = lax.axis_index("x"), lax.axis_index("y"), lax.axis_index("c")
    me, sibling = (x, y, c), (x, y, 1 - c)
    chips = [(1 - x, y), (x, 1 - y), (1 - x, 1 - y)]

    def copy(i, k, block, to, from_src=False):
        rows = out_refs[i].at[4 * block[0] + 2 * block[1] + block[2]]
        return pltpu.make_async_remote_copy(
            src_ref=src_refs[i] if from_src else rows, dst_ref=rows, send_sem=send_sems.at[GATHER_SEMS * i + k],
            recv_sem=recv_sems.at[GATHER_SEMS * i + k], device_id=to, device_id_type=_MESH_ID)

    def local(i):
        return pltpu.make_async_copy(src_refs[i], out_refs[i].at[4 * x + 2 * y + c], local_sems.at[i])

    def first(i):
        return [copy(i, 0, me, sibling, True)] + [copy(i, 1 + j, me, (*chip, c), True) for j, chip in enumerate(chips)]

    def passed(i, j):
        return copy(i, 4 + j, (*chips[j], c), sibling)

    def start():
        for i in range(n):
            local(i).start()
            for cp in first(i):
                cp.start()

    def forward():
        for j, chip in enumerate(chips):
            for i in range(n):
                copy(i, 1 + j, (*chip, c), me).wait_recv()
                passed(i, j).start()

    def finish():
        for i in range(n):
            copy(i, 0, sibling, me).wait_recv()
            for j, chip in enumerate(chips):
                copy(i, 4 + j, (*chip, 1 - c), me).wait_recv()
        for i in range(n):
            for cp in first(i) + [passed(i, j) for j in range(len(chips))]:
                cp.wait_send()
            local(i).wait()

    return start, forward, finish


def _gather(srcs, name):
    n = len(srcs)

    def body(*refs):
        for phase in _gather_phases(refs[:n], refs[n:2 * n], *refs[2 * n:]):
            phase()

    return _comm_call(body, srcs, [jax.ShapeDtypeStruct((N_DEV,) + s.shape, s.dtype) for s in srcs], GATHER_SEMS * n, name)


def _swap_sibling(srcs, name):
    n = len(srcs)

    def body(*refs):
        src_refs, out_refs = refs[:n], refs[n:2 * n]
        send_sems, recv_sems, _ = refs[2 * n:]
        sibling = (lax.axis_index("x"), lax.axis_index("y"), 1 - lax.axis_index("c"))
        copies = [pltpu.make_async_remote_copy(src_ref=src_refs[i], dst_ref=out_refs[i], send_sem=send_sems.at[i],
                                               recv_sem=recv_sems.at[i], device_id=sibling, device_id_type=_MESH_ID) for i in range(n)]
        for cp in copies:
            cp.start()
        for cp in copies:
            cp.wait_recv()
        for cp in copies:
            cp.wait_send()

    return _comm_call(body, srcs, [jax.ShapeDtypeStruct(s.shape, s.dtype) for s in srcs], n, name)


def _exchange_chips(srcs, name):
    n = len(srcs)
    per = N_CHIPS - 1

    def body(*refs):
        src_refs, out_refs = refs[:n], refs[n:2 * n]
        send_sems, recv_sems, local_sems = refs[2 * n:]
        x, y, c = lax.axis_index("x"), lax.axis_index("y"), lax.axis_index("c")
        mine = 2 * x + y
        local = [pltpu.make_async_copy(src_refs[i].at[mine], out_refs[i].at[mine], local_sems.at[i]) for i in range(n)]
        for cp in local:
            cp.start()
        copies = []
        for k in range(1, N_CHIPS):
            px, py = x ^ (k >> 1), y ^ (k & 1)
            peer = 2 * px + py
            for i in range(n):
                sem = per * i + k - 1
                send = pltpu.make_async_remote_copy(
                    src_ref=src_refs[i].at[peer], dst_ref=out_refs[i].at[mine], send_sem=send_sems.at[sem], recv_sem=recv_sems.at[sem],
                    device_id=(px, py, c), device_id_type=_MESH_ID)
                send.start()
                recv = pltpu.make_async_remote_copy(
                    src_ref=src_refs[i].at[peer], dst_ref=out_refs[i].at[peer], send_sem=send_sems.at[sem], recv_sem=recv_sems.at[sem],
                    device_id=(px, py, c), device_id_type=_MESH_ID)
                copies.append((send, recv))
        for _, recv in copies:
            recv.wait_recv()
        for send, _ in copies:
            send.wait_send()
        for cp in local:
            cp.wait()

    return _comm_call(body, srcs, [jax.ShapeDtypeStruct(s.shape, s.dtype) for s in srcs], per * n, name)


def _add(a, b, name):
    k, rows, cols = a.shape
    tr = _row_tile(rows)

    def body(a_ref, b_ref, o_ref):
        o_ref[...] = (a_ref[...].astype(F32) + b_ref[...].astype(F32)).astype(o_ref.dtype)

    spec = pl.BlockSpec((k, tr, cols), lambda i: (0, i, 0))
    return pl.pallas_call(
        body, grid=(rows // tr,), in_specs=[spec, spec], out_specs=spec, out_shape=jax.ShapeDtypeStruct(a.shape, a.dtype),
        compiler_params=pltpu.CompilerParams(dimension_semantics=("parallel",), vmem_limit_bytes=VMEM_LIMIT), name=name)(a, b)


def _row_tile(r, pref=256):
    if r <= pref:
        return r
    t = pref // 16 * 16
    while r % t:
        t -= 16
    return t


def _join_cols(g, width, name):
    _, rows, ws = g.shape
    tr = _row_tile(rows)
    tail = width - N_DEV * ws

    def body(g_ref, o_ref):
        for d in range(N_DEV):
            o_ref[:, pl.ds(d * ws, ws)] = g_ref[d]
        if tail:
            o_ref[:, pl.ds(N_DEV * ws, tail)] = jnp.zeros((tr, tail), g.dtype)

    return pl.pallas_call(
        body, grid=(rows // tr,), in_specs=[pl.BlockSpec((N_DEV, tr, ws), lambda i: (0, i, 0))],
        out_specs=pl.BlockSpec((tr, width), lambda i: (i, 0)), out_shape=jax.ShapeDtypeStruct((rows, width), g.dtype),
        compiler_params=pltpu.CompilerParams(dimension_semantics=("parallel",), vmem_limit_bytes=VMEM_LIMIT), name=name)(g)


def _split_cols(full, ws, dtype, name):
    rows, width = full.shape
    tr = _row_tile(rows)

    def body(x_ref, o_ref):
        for d in range(N_DEV):
            o_ref[d] = x_ref[:, pl.ds(d * ws, ws)].astype(dtype)

    return pl.pallas_call(
        body, grid=(rows // tr,), in_specs=[pl.BlockSpec((tr, width), lambda i: (i, 0))],
        out_specs=pl.BlockSpec((N_DEV, tr, ws), lambda i: (0, i, 0)), out_shape=jax.ShapeDtypeStruct((N_DEV, rows, ws), dtype),
        compiler_params=pltpu.CompilerParams(dimension_semantics=("parallel",), vmem_limit_bytes=VMEM_LIMIT), name=name)(full)


def _adamw(parts, w, m, v, name):
    r, c = w.shape
    n_parts = parts.shape[0]
    tr = _row_tile(r)

    def body(p_ref, w_ref, m_ref, v_ref, g_ref, d_ref, nm_ref, nv_ref):
        g = p_ref[0].astype(F32)
        for s in range(1, n_parts):
            g = g + p_ref[s].astype(F32)
        nm = ADAM_B1 * m_ref[...] + (1.0 - ADAM_B1) * g
        nv = ADAM_B2 * v_ref[...] + (1.0 - ADAM_B2) * (g * g)
        m_hat = nm / (1.0 - ADAM_B1 ** ADAM_STEP)
        v_hat = nv / (1.0 - ADAM_B2 ** ADAM_STEP)
        g_ref[...] = g
        nm_ref[...] = nm
        nv_ref[...] = nv
        d_ref[...] = -ADAM_LR * (m_hat / (jnp.sqrt(v_hat) + ADAM_EPS) + ADAM_WD * w_ref[...])

    spec = pl.BlockSpec((tr, c), lambda i: (i, 0))
    return pl.pallas_call(
        body, grid=(r // tr,), in_specs=[pl.BlockSpec((n_parts, tr, c), lambda i: (0, i, 0)), spec, spec, spec],
        out_specs=[spec] * 4, out_shape=[jax.ShapeDtypeStruct((r, c), F32)] * 4,
        compiler_params=pltpu.CompilerParams(dimension_semantics=("parallel",), vmem_limit_bytes=VMEM_LIMIT),
        name=name)(parts, w, m, v)


def _pack_rows(n_elems, mult):
    rows = -(-n_elems // PACK_COLS)
    return -(-rows // mult) * mult


def _pack(arrays, dtype, mult, lead=()):
    flat = jnp.concatenate([a.astype(dtype).reshape(lead + (-1,)) for a in arrays], axis=-1)
    rows = _pack_rows(flat.shape[-1], mult)
    flat = jnp.pad(flat, [(0, 0)] * len(lead) + [(0, rows * PACK_COLS - flat.shape[-1])])
    return flat.reshape(lead + (rows, PACK_COLS))


def _unpack(buf, shapes, lead=()):
    flat = buf.reshape(lead + (-1,))
    out, off = [], 0
    for s in shapes:
        n = math.prod(s)
        out.append(flat[..., off:off + n].reshape(lead + tuple(s)))
        off += n
    return out


def _join_shards(piece, axis):
    moved = jnp.moveaxis(piece, 0, axis)
    shape = moved.shape
    return moved.reshape(shape[:axis] + (shape[axis] * shape[axis + 1],) + shape[axis + 2:])


def _split_shards(full, axis):
    shape = full.shape
    return jnp.moveaxis(full.reshape(shape[:axis] + (N_DEV, shape[axis] // N_DEV) + shape[axis + 1:]), axis, 0)


def _block_diag(blocks):
    g, r, c = blocks.shape
    eye = jnp.eye(g, dtype=blocks.dtype)
    return (blocks[:, :, None, :] * eye[:, None, :, None]).reshape(g * r, g * c)


def _diag_blocks(mat, g):
    r, c = mat.shape[0] // g, mat.shape[1] // g
    eye = jnp.eye(g, dtype=mat.dtype)
    return jnp.sum(mat.reshape(g, r, g, c) * eye[:, None, :, None], axis=2)


def _pad_lanes(a):
    a = a.reshape(1, -1)
    return jnp.pad(a, ((0, 0), (0, LANES - a.shape[1])))


def _head_expand():
    h = jnp.arange(LANES)[:, None]
    ch = jnp.arange(SSM_HEADS * SSM_P)[None, :] // SSM_P
    return (h == ch).astype(F32)


def _rotary_tables(t):
    inv = 10000.0 ** (-jnp.arange(0, RET_DK, 2, dtype=F32) / RET_DK)
    ang = jnp.arange(t).astype(F32)[:, None] * inv[None, :]
    cos, sin = jnp.cos(ang), jnp.sin(ang)
    return jnp.concatenate([cos, cos], axis=1), jnp.concatenate([-sin, sin], axis=1)


def _rms_fwd(x, g, name):
    return _rows(_f_rms, x.shape[0], [(x, D_MODEL, 0)], [g], [D_MODEL], [], name, bf16_outs=(0,))[0]


def _rms_bwd(x, g, dh, dres, name):
    def fn(x_, dh_, dres_, g_):
        _, vjp = jax.vjp(lambda a, b: _rms(a, b), x_, g_)
        dx, dg = vjp(dh_)
        return dx + dres_, dx + dres_, dg
    return _rows(fn, x.shape[0], [(x, D_MODEL, 0), (dh, D_MODEL, 0), (dres, D_MODEL, 0)], [g], [D_MODEL, D_MODEL], [(1, D_MODEL)],
                 name, bf16_outs=(1,))


def _ffn_fwd(x, norm_g, w_up, dw_w, dw_b, w_down, tag):
    t = x.shape[0]
    nbk = D_FF // LANES
    h = _rms_fwd(x, norm_g, "ffn_norm_" + tag)
    up = _mm(h, w_up, "nn", "ffn_up_" + tag)
    mid = _cols(_f_ffnmid, nbk, [(up, 0), (up, nbk), (dw_w, 0), (dw_w, nbk), (dw_b, 0), (dw_b, nbk)], [(t,)], "ffn_mid_" + tag,
                bf16_outs=(0,))[0]
    out = _mm(mid, w_down, "nn", "ffn_down_" + tag, res=x)
    return out, (h, up, mid)


def _ffn_bwd(x, norm_g, w_up, dw_w, dw_b, w_down, saved, dout, dout_bf, tag):
    t = x.shape[0]
    nbk = D_FF // LANES
    h, up, mid = saved
    d_w_down = _mm(mid, dout_bf, "tn", "ffn_down_dw_" + tag, out_dtype=BF16)
    dmid = _mm(dout_bf, w_down, "nt", "ffn_down_dx_" + tag)
    dgin, duin, dwg, dwu, dbg, dbu = _cols(
        _grad_fn(_f_ffnmid, 6, 6), nbk,
        [(up, 0), (up, nbk), (dw_w, 0), (dw_w, nbk), (dw_b, 0), (dw_b, nbk), (dmid, 0)],
        [(t,), (t,), (3, 1), (3, 1), (1,), (1,)], "ffn_mid_bwd_" + tag, bf16_outs=(0, 1))
    dup = jnp.concatenate([dgin, duin], axis=1)
    d_w_up = _mm(h, dup, "tn", "ffn_up_dw_" + tag)
    dh = _mm(dup, w_up, "nt", "ffn_up_dx_" + tag)
    dx, dx_bf, dnorm = _rms_bwd(x, norm_g, dh, dout, "ffn_norm_bwd_" + tag)
    return dx, dx_bf, dict(norm=dnorm, w_up=d_w_up, dw_w=jnp.concatenate([dwg, dwu], axis=2)[:, 0],
                           dw_b=jnp.concatenate([dbg, dbu], axis=1), w_down=d_w_down)


def _local_step(x, tgt, w, late=None):
    t = x.shape[0]
    grads = {}
    expand = _head_expand()
    cosf, sins = _rotary_tables(t)
    mix_g = [w['mix_norm'][i:i + 1] for i in range(2)]
    ffn_g = [w['ffn_norm'][i:i + 1] for i in range(2)]
    ffn_dw_w = [w['ffn_dw_w'][i][:, None, :] for i in range(2)]
    ffn_dw_b = [w['ffn_dw_b'][i:i + 1] for i in range(2)]

    w_e = w['e_w_in']
    conv_w = w['e_conv_w'][0][:, None, :]
    conv_b = w['e_conv_b']
    dt_bias, a_log, d_skip = _pad_lanes(w['e_dt_bias']), _pad_lanes(w['e_a_log']), _pad_lanes(w['e_d'])
    h0 = _rms_fwd(x, mix_g[0], "mix_norm_0")
    proj = _mm(h0, w_e, "nn", "e_in")
    qr, kr = _rows(_f_retpre, t, [(proj, 512, 0), (proj, 512, 1), (cosf, LANES, 0), (sins, LANES, 0)], [], [512, 512], [], "ret_pre")
    if late:
        r, *landed = _ret_fwd(qr, kr, proj, t, late['ret_fwd'][0])
        late['ret_fwd'][1](w, landed)
    else:
        r = _ret_fwd(qr, kr, proj, t)
    y_ret = _rows(_f_retpost, t, [(r, 1024, 0), (proj, 1024, 2)], [], [1024], [], "ret_post", bf16_outs=(0,))[0]
    xbc_act = _cols(_f_ssdconv, 12, [(proj, 32), (conv_w, 0), (conv_b, 0)], [(t,)], "ssd_conv")[0]
    xdt, da = _rows(_f_ssdpre, t, [(xbc_act, 1024, 0), (proj, LANES, 44)], [dt_bias, a_log, expand], [1024, LANES], [], "ssd_pre")
    a_cum = _cumsum(da, False, "ssd_cumsum")
    a_cum_t = a_cum[:, :SSM_HEADS].T
    if late:
        yc, *landed = _ssd_fwd(xbc_act, xdt, a_cum, a_cum_t, t, late['ssd_fwd'][0])
        late['ssd_fwd'][1](w, landed)
    else:
        yc = _ssd_fwd(xbc_act, xdt, a_cum, a_cum_t, t)
    y_ssm = _rows(_f_ssdpost, t, [(yc, 1024, 0), (xbc_act, 1024, 0), (proj, 1024, 3)], [d_skip, w['e_ssm_norm'], expand],
                  [1024], [], "ssd_post", bf16_outs=(0,))[0]
    mix_e = jnp.concatenate([y_ret, y_ssm], axis=1)
    x1 = _mm(mix_e, w['e_w_out'], "nn", "e_out", res=x)
    x2, ffn0 = _ffn_fwd(x1, ffn_g[0], w['ffn_w_up'][0], ffn_dw_w[0], ffn_dw_b[0], w['ffn_w_down'][0], "0")

    lr, li = w['o_a_re'][0], w['o_a_im'][0]
    ls = w['o_log_step'].reshape(S5_GROUPS, 1)
    b_re3, b_im3 = jnp.transpose(w['o_b_re'][0], (2, 0, 1)), jnp.transpose(w['o_b_im'][0], (2, 0, 1))
    par_ins = [lr, li, ls, b_re3, b_im3]
    whole = lambda a: (a, a.shape, (lambda i, n=a.ndim: (0,) * n))
    par_shapes = [(S5_GROUPS, S5_STATE)] * 2 + [(S5_GROUP, S5_GROUPS, S5_STATE)] * 2
    ab_re, ab_im, bb_re, bb_im = _call(_f_s5par, (1,), [whole(a) for a in par_ins],
                                       [(s, F32, s, (lambda i, n=len(s): (0,) * n), False) for s in par_shapes], "s5_params")
    w_b = jnp.concatenate([_block_diag(jnp.transpose(bb_re, (1, 0, 2))), _block_diag(jnp.transpose(bb_im, (1, 0, 2)))], axis=1)
    w_c = jnp.concatenate([_block_diag(jnp.transpose(w['o_c_re'][0], (0, 2, 1))),
                           -_block_diag(jnp.transpose(w['o_c_im'][0], (0, 2, 1)))], axis=0)
    a_re, a_im = ab_re.reshape(1, S5_COLS), ab_im.reshape(1, S5_COLS)
    dw_w = w['o_dw_w'][0][:, None, :]
    glu_w = w['o_glu_w'].astype(F32)

    h1 = _rms_fwd(x2, mix_g[1], "mix_norm_1")
    proj_o = _mm(h1, w['o_w_in'], "nn", "o_in")
    c1 = _conf_conv(proj_o, dw_w, w['o_dw_b'])
    c2 = _rows(_f_confb, t, [(c1, 512, 0)], [w['o_ln_g'], w['o_ln_b']], [512], [], "conf_norm", bf16_outs=(0,))[0]
    u = proj_o[:, 1024:].astype(BF16)
    bu = _mm(u, w_b, "nn", "s5_bu")
    xs_re, xs_im, xs_re_bf, xs_im_bf = _s5_scan(bu, a_re, a_im, False)
    xs_cat = jnp.concatenate([xs_re, xs_im], axis=1)
    xs_cat_bf = jnp.concatenate([xs_re_bf, xs_im_bf], axis=1)
    y_s5 = _mm(xs_cat_bf, w_c, "nn", "s5_cx")
    s_out = _rows(_f_s5post, t, [(y_s5, 512, 0), (proj_o, 512, 2)], [w['o_d'], glu_w], [512], [], "s5_post", bf16_outs=(0,))[0]
    mix_o = jnp.concatenate([c2, s_out], axis=1)
    x3 = _mm(mix_o, w['o_w_out'], "nn", "o_out", res=x2)
    x4, ffn1 = _ffn_fwd(x3, ffn_g[1], w['ffn_w_up'][1], ffn_dw_w[1], ffn_dw_b[1], w['ffn_w_down'][1], "1")

    dx4, dx4_bf, loss_blk, d_final = _rows(_loss_step, t, [(x4, D_MODEL, 0), (tgt, D_MODEL, 0)], [w['final_norm'].reshape(1, D_MODEL)],
                                           [D_MODEL, D_MODEL], [(8, LANES), (1, D_MODEL)], "loss_head", bf16_outs=(1,))
    loss = loss_blk[0, 0]
    grads['final_norm'] = d_final.reshape(D_MODEL)

    dx3, dx3_bf, g1 = _ffn_bwd(x3, ffn_g[1], w['ffn_w_up'][1], ffn_dw_w[1], ffn_dw_b[1], w['ffn_w_down'][1], ffn1, dx4, dx4_bf, "1")
    grads['o_w_out'] = _mm(mix_o, dx3_bf, "tn", "o_out_dw", out_dtype=BF16)
    dmix_o = _mm(dx3_bf, w['o_w_out'], "nt", "o_out_dx")
    dc1, d_ln_g, d_ln_b = _conf_norm_bwd(c1, w['o_ln_g'], w['o_ln_b'], dmix_o, t)
    dca, dcg, d_dw_w, d_dw_b = _conf_conv(proj_o, dw_w, w['o_dw_b'], dc1)
    dyc, du_skip, d_od, d_glu = _s5_post_bwd(y_s5, proj_o, w['o_d'], glu_w, dmix_o, t)
    grads['o_glu_w'] = d_glu
    d_w_c = _mm(xs_cat_bf, dyc, "tn", "s5_cx_dw")
    dxs = _mm(dyc, w_c, "nt", "s5_cx_dx")
    g_re, g_im, d_are, d_aim = _s5_scan(dxs, a_re, a_im, True, xs_cat)
    g_cat = jnp.concatenate([g_re, g_im], axis=1)
    d_w_b = _mm(u, g_cat, "tn", "s5_bu_dw")
    du = _mm(g_cat, w_b, "nt", "s5_bu_dx", res=du_skip, out_dtype=BF16)
    d_bb_re = jnp.transpose(_diag_blocks(d_w_b[:, :S5_COLS], S5_GROUPS), (1, 0, 2))
    d_bb_im = jnp.transpose(_diag_blocks(d_w_b[:, S5_COLS:], S5_GROUPS), (1, 0, 2))
    par_cts = [d_are.reshape(S5_GROUPS, S5_STATE), d_aim.reshape(S5_GROUPS, S5_STATE), d_bb_re, d_bb_im]
    in_shapes = [a.shape for a in par_ins]
    d_lr, d_li, d_ls, d_br3, d_bi3 = _call(_grad_fn(_f_s5par, 5, 5), (1,), [whole(a) for a in par_ins + par_cts],
                                           [(s, F32, s, (lambda i, n=len(s): (0,) * n), False) for s in in_shapes], "s5_params_bwd")
    grads['o_a_re'], grads['o_a_im'], grads['o_log_step'] = d_lr[None], d_li[None], d_ls.reshape(1, S5_GROUPS)
    grads['o_b_re'], grads['o_b_im'] = jnp.transpose(d_br3, (1, 2, 0))[None], jnp.transpose(d_bi3, (1, 2, 0))[None]
    grads['o_c_re'] = jnp.transpose(_diag_blocks(d_w_c[:S5_COLS], S5_GROUPS), (0, 2, 1))[None]
    grads['o_c_im'] = -jnp.transpose(_diag_blocks(d_w_c[S5_COLS:], S5_GROUPS), (0, 2, 1))[None]
    grads['o_d'], grads['o_ln_g'], grads['o_ln_b'] = d_od, d_ln_g, d_ln_b
    grads['o_dw_w'], grads['o_dw_b'] = d_dw_w[:, 0][None], d_dw_b
    dproj_o = jnp.concatenate([dca, dcg, du], axis=1)
    grads['o_w_in'] = _mm(h1, dproj_o, "tn", "o_in_dw")
    dh1 = _mm(dproj_o, w['o_w_in'], "nt", "o_in_dx")
    dx2, dx2_bf, d_mix1 = _rms_bwd(x2, mix_g[1], dh1, dx3, "mix_norm_bwd_1")

    dx1, dx1_bf, g0 = _ffn_bwd(x1, ffn_g[0], w['ffn_w_up'][0], ffn_dw_w[0], ffn_dw_b[0], w['ffn_w_down'][0], ffn0, dx2, dx2_bf, "0")
    grads['e_w_out'] = _mm(mix_e, dx1_bf, "tn", "e_out_dw", out_dtype=BF16)
    dmix_e = _mm(dx1_bf, w['e_w_out'], "nt", "e_out_dx")
    dr, dg = _rows(_grad_fn(_f_retpost, 2, 2), t, [(r, 1024, 0), (proj, 1024, 2), (dmix_e, 1024, 0)], [], [1024, 1024], [], "ret_post_bwd",
                   bf16_outs=(0, 1))
    dqr, dkr, dv = _ret_bwd(qr, kr, proj, dr, t)
    dq, dk = _ret_pre_bwd(proj, cosf, sins, dqr, dkr, t)
    dyc0, dxs1, dz, d_dskip, d_ssm_norm = _ssd_post_bwd(yc, xbc_act, proj, d_skip, w['e_ssm_norm'], expand, dmix_e, t)
    dcm, da_q, dbm, dxdt, da_k, da_k_t = _ssd_bwd(xbc_act, xdt, a_cum, a_cum_t, dyc0, expand, t)
    d_a_cum = da_q - da_k - jnp.pad(da_k_t.T, ((0, 0), (0, LANES - SSM_HEADS)))
    dda = _cumsum(d_a_cum, True, "ssd_cumsum_bwd")
    dxs, ddtr, d_dt_bias, d_a_log = _ssd_pre_bwd(xbc_act, proj, dt_bias, a_log, expand, dxdt, dda, dxs1, t)
    dxbc_act = jnp.concatenate([dxs, dbm, dcm], axis=1)
    dxbc, d_conv_w, d_conv_b = _cols(_grad_fn(_f_ssdconv, 3, 3), 12, [(proj, 32), (conv_w, 0), (conv_b, 0), (dxbc_act, 0)],
                                     [(t,), (4, 1), (1,)], "ssd_conv_bwd", bf16_outs=(0,))
    dproj = jnp.concatenate([dq, dk, dv, dg, dz, dxbc, ddtr], axis=1)
    grads['e_w_in'] = _mm(h0, dproj, "tn", "e_in_dw")
    dh0 = _mm(dproj, w_e, "nt", "e_in_dx")
    dx0, _, d_mix0 = _rms_bwd(x, mix_g[0], dh0, dx1, "mix_norm_bwd_0")

    grads['mix_norm'] = jnp.concatenate([d_mix0, d_mix1], axis=0)
    grads['e_conv_w'], grads['e_conv_b'] = d_conv_w[:, 0][None], d_conv_b
    grads['e_dt_bias'], grads['e_a_log'], grads['e_d'] = d_dt_bias[:, :SSM_HEADS], d_a_log[:, :SSM_HEADS], d_dskip[:, :SSM_HEADS]
    grads['e_ssm_norm'] = d_ssm_norm
    grads['ffn_norm'] = jnp.concatenate([g0['norm'], g1['norm']], axis=0)
    grads['ffn_w_up'], grads['ffn_w_down'] = [g0['w_up'], g1['w_up']], [g0['w_down'], g1['w_down']]
    grads['ffn_dw_w'] = jnp.stack([g0['dw_w'], g1['dw_w']], axis=0)
    grads['ffn_dw_b'] = jnp.concatenate([g0['dw_b'], g1['dw_b']], axis=0)
    return loss, dx0, grads


def _conf_norm_bwd(c1, ln_g, ln_b, dmix_o, t):
    def fn(c1_, dy_, g_, b_):
        _, vjp = jax.vjp(lambda a, b, c: _f_confb(a, b, c)[0], c1_, g_, b_)
        return vjp(dy_)
    return _rows(fn, t, [(c1, 512, 0), (dmix_o, 512, 0)], [ln_g, ln_b], [512], [(1, 512), (1, 512)], "conf_norm_bwd")


def _s5_post_bwd(y_s5, proj_o, d_skip, glu_w, dmix_o, t):
    def fn(yc_, u_, dy_, d_, gw_):
        _, vjp = jax.vjp(lambda a, b, c, e: _f_s5post(a, b, c, e)[0], yc_, u_, d_, gw_)
        return vjp(dy_)
    return _rows(fn, t, [(y_s5, 512, 0), (proj_o, 512, 2), (dmix_o, 512, 1)], [d_skip, glu_w], [512, 512], [(1, 512), (512, 512)],
                 "s5_post_bwd", bf16_outs=(0,))


def _ret_pre_bwd(proj, cosf, sins, dqr, dkr, t):
    def fn(q_, k_, cos_, sin_, dq_, dk_):
        _, vjp = jax.vjp(lambda a, b: _f_retpre(a, b, cos_, sin_), q_, k_)
        return vjp((dq_, dk_))
    return _rows(fn, t, [(proj, 512, 0), (proj, 512, 1), (cosf, LANES, 0), (sins, LANES, 0), (dqr, 512, 0), (dkr, 512, 0)], [],
                 [512, 512], [], "ret_pre_bwd", bf16_outs=(0, 1))


def _ssd_post_bwd(yc, xbc_act, proj, d_skip, norm_w, expand, dmix_e, t):
    def fn(yc_, xs_, z_, dy_, d_, nw_, e_):
        _, vjp = jax.vjp(lambda a, b, c, dd, n: _f_ssdpost(a, b, c, dd, n, e_)[0], yc_, xs_, z_, d_, nw_)
        return vjp(dy_)
    return _rows(fn, t, [(yc, 1024, 0), (xbc_act, 1024, 0), (proj, 1024, 3), (dmix_e, 1024, 1)], [d_skip, norm_w, expand],
                 [1024, 1024, 1024], [(1, LANES), (1, 1024)], "ssd_post_bwd", bf16_outs=(0, 2))


def _ssd_pre_bwd(xbc_act, proj, dt_bias, a_log, expand, dxdt, dda, dxs1, t):
    def fn(xs_, dtr_, dx_, dda_, dxs1_, bias_, alog_, e_):
        _, vjp = jax.vjp(lambda a, b, c, dd: _f_ssdpre(a, b, c, dd, e_), xs_, dtr_, bias_, alog_)
        dxs, ddtr, dbias, dalog = vjp((dx_, dda_))
        return dxs + dxs1_, ddtr, dbias, dalog
    return _rows(fn, t, [(xbc_act, 1024, 0), (proj, LANES, 44), (dxdt, 1024, 0), (dda, LANES, 0), (dxs1, 1024, 0)],
                 [dt_bias, a_log, expand], [1024, LANES], [(1, LANES), (1, LANES)], "ssd_pre_bwd", bf16_outs=(1,))


def kernel(x, mix_norm, e_w_in, e_conv_w, e_conv_b, e_dt_bias, e_a_log, e_d, e_ssm_norm, e_w_out, o_w_in, o_dw_w, o_dw_b, o_ln_g, o_ln_b, o_a_re, o_a_im, o_b_re, o_b_im, o_c_re, o_c_im, o_d, o_log_step, o_glu_w, o_w_out, ffn_norm, ffn_w_up, ffn_dw_w, ffn_dw_b, ffn_w_down, final_norm, loss_target, m_mix_norm, m_e_w_in, m_e_conv_w, m_e_conv_b, m_e_dt_bias, m_e_a_log, m_e_d, m_e_ssm_norm, m_e_w_out, m_o_w_in, m_o_dw_w, m_o_dw_b, m_o_ln_g, m_o_ln_b, m_o_a_re, m_o_a_im, m_o_b_re, m_o_b_im, m_o_c_re, m_o_c_im, m_o_d, m_o_log_step, m_o_glu_w, m_o_w_out, m_ffn_norm, m_ffn_w_up, m_ffn_dw_w, m_ffn_dw_b, m_ffn_w_down, m_final_norm, v_mix_norm, v_e_w_in, v_e_conv_w, v_e_conv_b, v_e_dt_bias, v_e_a_log, v_e_d, v_e_ssm_norm, v_e_w_out, v_o_w_in, v_o_dw_w, v_o_dw_b, v_o_ln_g, v_o_ln_b, v_o_a_re, v_o_a_im, v_o_b_re, v_o_b_im, v_o_c_re, v_o_c_im, v_o_d, v_o_log_step, v_o_glu_w, v_o_w_out, v_ffn_norm, v_ffn_w_up, v_ffn_dw_w, v_ffn_dw_b, v_ffn_w_down, v_final_norm):
    p = dict(locals())

    kinds = ("grad_", "delta_", "new_m_", "new_v_")

    def block(name, layer):
        return p[name][0 if layer is None else layer]

    def shards(blocks):
        return [block(n, layer).astype(BF16) for n, layer, _ in blocks]

    def place(w, gathered, blocks):
        for (n, layer, by_cols), g in zip(blocks, gathered):
            if by_cols:
                full = _join_cols(g, E_IN_PAD if n == 'e_w_in' else N_DEV * g.shape[2], f"join_{n}_{layer}")
            else:
                full = g.reshape(N_DEV * g.shape[1], g.shape[2])
            if layer is None:
                w[n] = full
            else:
                w[n][layer] = full

    first = [b for b in MATMUL_BLOCKS if b[0].startswith('e_')]
    with_ret = [b for b in MATMUL_BLOCKS if b[0].startswith('ffn_') and b[1] == 0]
    with_ssd = [b for b in MATMUL_BLOCKS if b not in first and b not in with_ret]
    gathered = _gather(shards(first) + [_pack([p[n] for n in SMALL_SHARDED], F32, 16)], "gather_first_weights")
    w = {n: p[n] for n in REPLICATED}
    w['ffn_w_up'], w['ffn_w_down'] = [None, None], [None, None]
    place(w, gathered, first)
    for n, piece in zip(SMALL_SHARDED, _unpack(gathered[-1], [p[n].shape for n in SMALL_SHARDED], lead=(N_DEV,))):
        w[n] = _join_shards(piece, SHARDED[n])
    late = {'ret_fwd': (shards(with_ret), functools.partial(place, blocks=with_ret)),
            'ssd_fwd': (shards(with_ssd), functools.partial(place, blocks=with_ssd))}

    loss, dx, grads = _local_step(x[0], loss_target[0], w, late)
    loss = lax.psum(loss, MESH_AXES)

    sends = []
    for n, layer, by_cols in MATMUL_BLOCKS:
        g = grads[n] if layer is None else grads[n][layer]
        if by_cols:
            sends.append(_split_cols(g, block(n, layer).shape[1], BF16, f"split_{n}_{layer}"))
        else:
            sends.append(g.astype(BF16).reshape(N_DEV, -1, g.shape[1]))
    sends.append(_pack([_split_shards(grads[n].reshape(p[n].shape[:SHARDED[n]] + (-1,) + p[n].shape[SHARDED[n] + 1:]), SHARDED[n])
                        for n in SMALL_SHARDED], F32, 128, lead=(N_DEV,)))
    core = lax.axis_index("c")
    by_core = [s.reshape((N_CHIPS, 2) + s.shape[1:]) for s in sends]
    keep = [lax.dynamic_index_in_dim(s, core, axis=1, keepdims=False) for s in by_core]
    give = [lax.dynamic_index_in_dim(s, 1 - core, axis=1, keepdims=False) for s in by_core]
    got = _swap_sibling(give, "swap_sibling_grads")
    chip_sums = [_add(a, b, f"chip_sum_{i}") for i, (a, b) in enumerate(zip(keep, got))]
    parts = list(_exchange_chips(chip_sums, "exchange_chip_grads"))
    parts.append(_gather([_pack([grads[n].reshape(p[n].shape) for n in REPLICATED], F32, 128)], "gather_replicated_grads")[0])

    out, by_layer = {}, {}
    for (n, layer, _), part in zip(MATMUL_BLOCKS, parts):
        by_layer.setdefault(n, {})[layer] = _adamw(part, *[block(pre + n, layer) for pre in ("", "m_", "v_")], f"adamw_{n}_{layer}")
    for n, res in by_layer.items():
        for i, kind in enumerate(kinds):
            out[kind + n] = res[None][i][None] if None in res else jnp.stack([res[0][i], res[1][i]], axis=0)
    for names, part, tag in ((SMALL_SHARDED, parts[-2], "small"), (REPLICATED, parts[-1], "replicated")):
        packed = [_pack([p[pre + n] for n in names], F32, 128) for pre in ("", "m_", "v_")]
        for kind, buf in zip(kinds, _adamw(part, *packed, "adamw_" + tag)):
            for n, a in zip(names, _unpack(buf, [p[n].shape for n in names])):
                out[kind + n] = a
    return (loss, dx[None], *[out[kind + n] for kind in kinds for n in WEIGHTS])
```

```python
import functools
import math

import jax
import jax.numpy as jnp
from jax import lax
from jax.experimental import pallas as pl
from jax.experimental.pallas import tpu as pltpu

F32, BF16 = jnp.float32, jnp.bfloat16
HIGHEST = lax.Precision.HIGHEST
N_DEV = 8
MESH_AXES = ("x", "y", "c")
VMEM_LIMIT = 48 * 1024 * 1024
LANES = 128
PACK_COLS = 1024

D_MODEL = 1024
EPS = 1e-6
RET_HEADS, RET_DK, RET_DV = 4, 128, 256
SSM_HEADS, SSM_P, SSM_N, SSM_GROUPS = 16, 64, 128, 2
SSM_HG = SSM_HEADS // SSM_GROUPS
S5_GROUPS, S5_GROUP, S5_STATE = 32, 16, 64
S5_COLS = S5_GROUPS * S5_STATE
D_FF = 2816
E_IN, E_IN_PAD = 5648, 5760
ADAM_LR, ADAM_B1, ADAM_B2, ADAM_EPS, ADAM_WD, ADAM_STEP = 0.001, 0.9, 0.999, 1e-08, 0.01, 10

WEIGHTS = ['mix_norm', 'e_w_in', 'e_conv_w', 'e_conv_b', 'e_dt_bias', 'e_a_log', 'e_d', 'e_ssm_norm', 'e_w_out', 'o_w_in', 'o_dw_w', 'o_dw_b', 'o_ln_g', 'o_ln_b', 'o_a_re', 'o_a_im', 'o_b_re', 'o_b_im', 'o_c_re', 'o_c_im', 'o_d', 'o_log_step', 'o_glu_w', 'o_w_out', 'ffn_norm', 'ffn_w_up', 'ffn_dw_w', 'ffn_dw_b', 'ffn_w_down', 'final_norm']
SHARDED = {'e_w_in': 2, 'e_conv_w': 2, 'e_w_out': 1, 'o_w_in': 2, 'o_dw_w': 2, 'o_dw_b': 1, 'o_ln_g': 1, 'o_ln_b': 1,
           'o_d': 1, 'o_glu_w': 1, 'o_w_out': 1, 'ffn_w_up': 2, 'ffn_dw_w': 2, 'ffn_w_down': 1}
MATMUL_WEIGHTS = ['e_w_in', 'e_w_out', 'o_w_in', 'o_glu_w', 'o_w_out', 'ffn_w_up', 'ffn_w_down']
MATMUL_BLOCKS = [('e_w_in', None, True), ('e_w_out', None, False), ('o_w_in', None, True), ('o_glu_w', None, False),
                 ('o_w_out', None, False), ('ffn_w_up', 0, True), ('ffn_w_up', 1, True), ('ffn_w_down', 0, False), ('ffn_w_down', 1, False)]
SMALL_SHARDED = [n for n in WEIGHTS if n in SHARDED and n not in MATMUL_WEIGHTS]
REPLICATED = [n for n in WEIGHTS if n not in SHARDED]


def _call(fn, grid, ins, outs, name):
    n_in = len(ins)

    def body(*refs):
        vals = fn(*[r[...] for r in refs[:n_in]])
        first = pl.program_id(0) == 0
        for r, v, o in zip(refs[n_in:], vals, outs):
            if o[4]:
                @pl.when(first)
                def _():
                    r[...] = jnp.zeros_like(r)
                r[...] += v.astype(r.dtype)
            else:
                r[...] = v.astype(r.dtype)

    return pl.pallas_call(
        body, grid=grid,
        in_specs=[pl.BlockSpec(b, m) for _, b, m in ins],
        out_specs=[pl.BlockSpec(o[2], o[3]) for o in outs],
        out_shape=[jax.ShapeDtypeStruct(o[0], o[1]) for o in outs],
        compiler_params=pltpu.CompilerParams(dimension_semantics=("arbitrary",) * len(grid), vmem_limit_bytes=VMEM_LIMIT),
        name=name)(*[a for a, _, _ in ins])


def _rows(fn, n_rows, row_ins, full_ins, row_outs, acc_outs, name, tm=256, bf16_outs=()):
    tm = min(tm, n_rows)
    ins = [(a, (tm, w), (lambda i, c=c: (i, c))) for a, w, c in row_ins]
    ins += [(a, a.shape, (lambda i, n=a.ndim: (0,) * n)) for a in full_ins]
    outs = [((n_rows, w), BF16 if k in bf16_outs else F32, (tm, w), (lambda i: (i, 0)), False) for k, w in enumerate(row_outs)]
    outs += [(tuple(s), F32, tuple(s), (lambda i, n=len(s): (0,) * n), True) for s in acc_outs]
    return _call(fn, (n_rows // tm,), ins, outs, name)


def _cols(fn, n_blocks, col_ins, out_leads, name, cb=LANES, bf16_outs=()):
    ins = [(a, a.shape[:-1] + (cb,), (lambda j, n=a.ndim, o=o: (0,) * (n - 1) + (j + o,))) for a, o in col_ins]
    outs = [(tuple(s) + (n_blocks * cb,), BF16 if k in bf16_outs else F32, tuple(s) + (cb,), (lambda j, n=len(s): (0,) * n + (j,)), False)
            for k, s in enumerate(out_leads)]
    return _call(fn, (n_blocks,), ins, outs, name)


def _grad_fn(f, n_diff, n_in):
    def g(*a):
        diff, consts, cts = a[:n_diff], a[n_diff:n_in], a[n_in:]
        _, vjp = jax.vjp(lambda *d: f(*d, *consts), *diff)
        return vjp(tuple(cts))
    return g


def _silu(x):
    return x * jax.nn.sigmoid(x)


def _rms(x, g):
    return x * lax.rsqrt(jnp.mean(x * x, axis=-1, keepdims=True) + EPS) * g


@jax.custom_vjp
def _softplus(x):
    return jnp.maximum(x, 0.0) + jnp.log(1.0 + jnp.exp(-jnp.abs(x)))


_softplus.defvjp(lambda x: (_softplus(x), x), lambda x, g: (g * jax.nn.sigmoid(x),))


@jax.custom_vjp
def _swap_halves(x):
    return pltpu.roll(x, 64, 1)


_swap_halves.defvjp(lambda x: (_swap_halves(x), None), lambda _, g: (_swap_halves(g),))


def _shift_rows(x, k, up):
    if k == 0:
        return x
    n = x.shape[0]
    t = lax.broadcasted_iota(jnp.int32, x.shape, 0)
    if up:
        return jnp.where(t < n - k, pltpu.roll(x, n - k, 0), 0.0)
    return jnp.where(t >= k, pltpu.roll(x, k, 0), 0.0)


@jax.custom_vjp
def _dwconv(x, w, b):
    k_taps = w.shape[0]
    y = b + w[k_taps - 1] * x
    for k in range(k_taps - 1):
        y = y + w[k] * _shift_rows(x, k_taps - 1 - k, False)
    return y


def _dwconv_fwd(x, w, b):
    return _dwconv(x, w, b), (x, w)


def _dwconv_bwd(saved, dy):
    x, w = saved
    k_taps = w.shape[0]
    dx = w[k_taps - 1] * dy
    dws = []
    for k in range(k_taps - 1):
        s = k_taps - 1 - k
        dx = dx + w[k] * _shift_rows(dy, s, True)
        dws.append(jnp.sum(dy * _shift_rows(x, s, False), axis=0, keepdims=True)[None])
    dws.append(jnp.sum(dy * x, axis=0, keepdims=True)[None])
    return dx, jnp.concatenate(dws, axis=0), jnp.sum(dy, axis=0, keepdims=True)


_dwconv.defvjp(_dwconv_fwd, _dwconv_bwd)


def _f_rms(x, g):
    return (_rms(x, g),)


def _rot(x, cosf, sins):
    outs = []
    for h in range(RET_HEADS):
        xh = x[:, h * RET_DK:(h + 1) * RET_DK]
        outs.append(xh * cosf + _swap_halves(xh) * sins)
    return jnp.concatenate(outs, axis=1)


def _f_retpre(q, k, cosf, sins):
    return _rot(q, cosf, sins), _rot(k, cosf, sins) * (RET_DK ** -0.5)


def _f_retpost(r, g):
    outs = []
    for h in range(RET_HEADS):
        rh = r[:, h * RET_DV:(h + 1) * RET_DV]
        rc = rh - jnp.mean(rh, axis=-1, keepdims=True)
        outs.append(_silu(g[:, h * RET_DV:(h + 1) * RET_DV]) * (rc * lax.rsqrt(jnp.mean(rc * rc, axis=-1, keepdims=True) + EPS)))
    return (jnp.concatenate(outs, axis=1),)


def _f_ssdconv(xbc, w, b):
    return (_silu(_dwconv(xbc, w, b)),)


def _f_ssdpre(xs, dtr, bias, alog, expand):
    dt = _softplus(dtr + bias)
    return xs * jnp.dot(dt, expand, precision=HIGHEST, preferred_element_type=F32), dt * (-jnp.exp(alog))


def _f_ssdpost(yc, xs, z, dskip, norm_w, expand):
    d_wide = jnp.dot(jnp.broadcast_to(dskip, (yc.shape[0], LANES)), expand, precision=HIGHEST, preferred_element_type=F32)
    y = (yc + d_wide * xs) * _silu(z)
    half = y.shape[1] // SSM_GROUPS
    outs = []
    for g in range(SSM_GROUPS):
        yg = y[:, g * half:(g + 1) * half]
        outs.append(yg * lax.rsqrt(jnp.mean(yg * yg, axis=-1, keepdims=True) + EPS))
    return (jnp.concatenate(outs, axis=1) * norm_w,)


def _f_ffnmid(gin, uin, wg, wu, bg, bu):
    return (_silu(_dwconv(gin, wg, bg)) * _dwconv(uin, wu, bu),)


def _f_confb(c1, g, b):
    mu = jnp.mean(c1, axis=-1, keepdims=True)
    xc = c1 - mu
    return (_silu(xc * lax.rsqrt(jnp.mean(xc * xc, axis=-1, keepdims=True) + EPS) * g + b),)


def _f_s5post(yc, u, dskip, glu_w):
    s = jax.nn.gelu(yc + dskip * u)
    z = jnp.dot(s.astype(BF16), glu_w.astype(BF16), preferred_element_type=F32)
    return (s * jax.nn.sigmoid(z),)


def _f_s5par(lr, li, ls, br, bi):
    step = jnp.exp(ls)
    mag = jnp.exp(lr * step)
    ab_re = mag * jnp.cos(li * step)
    ab_im = mag * jnp.sin(li * step)
    den = lr * lr + li * li
    f_re = ((ab_re - 1.0) * lr + ab_im * li) / den
    f_im = (ab_im * lr - (ab_re - 1.0) * li) / den
    return ab_re, ab_im, f_re[None] * br - f_im[None] * bi, f_re[None] * bi + f_im[None] * br


def _loss_step(x, tgt, g):
    def f(x_, g_):
        e = _rms(x_, g_) - tgt
        return 0.5 * jnp.sum(jnp.mean(e * e, axis=-1, keepdims=True), axis=0, keepdims=True)
    loss, vjp = jax.vjp(f, x, g)
    dx, dg = vjp(jnp.ones((1, 1), F32))
    return dx, dx, jnp.broadcast_to(loss, (8, LANES)), dg


def _tile(n, pref):
    if n <= pref:
        return n
    t = (pref // LANES) * LANES
    while n % t:
        t -= LANES
    return t


_DIMS = {"nn": (((1,), (0,)), ((), ())), "nt": (((1,), (1,)), ((), ())), "tn": (((0,), (0,)), ((), ()))}


def _mm(a, b, mode, name, res=None, out_dtype=F32, tm=1024, tn=1408, tk=1408):
    if mode == "nn":
        (m, k), n = a.shape, b.shape[1]
    elif mode == "nt":
        (m, k), n = a.shape, b.shape[0]
    else:
        (k, m), n = a.shape, b.shape[1]
    tm, tn, tk = _tile(m, tm), _tile(n, tn), _tile(k, tk)
    nk = k // tk
    a_spec = pl.BlockSpec((tk, tm), lambda i, j, kk: (kk, i)) if mode == "tn" else pl.BlockSpec((tm, tk), lambda i, j, kk: (i, kk))
    b_spec = pl.BlockSpec((tn, tk), lambda i, j, kk: (j, kk)) if mode == "nt" else pl.BlockSpec((tk, tn), lambda i, j, kk: (kk, j))
    o_spec = pl.BlockSpec((tm, tn), lambda i, j, kk: (i, j))
    has_res = res is not None

    def body(*refs):
        a_ref, b_ref = refs[0], refs[1]
        o_ref, acc = refs[-2], refs[-1]
        kk = pl.program_id(2)

        @pl.when(kk == 0)
        def _():
            acc[...] = jnp.zeros_like(acc)

        acc[...] += lax.dot_general(a_ref[...].astype(BF16), b_ref[...].astype(BF16), _DIMS[mode], preferred_element_type=F32)

        @pl.when(kk == nk - 1)
        def _():
            o_ref[...] = (acc[...] + refs[2][...] if has_res else acc[...]).astype(out_dtype)

    return pl.pallas_call(
        body, grid=(m // tm, n // tn, nk),
        in_specs=[a_spec, b_spec] + ([o_spec] if has_res else []),
        out_specs=o_spec, out_shape=jax.ShapeDtypeStruct((m, n), out_dtype),
        scratch_shapes=[pltpu.VMEM((tm, tn), F32)],
        compiler_params=pltpu.CompilerParams(dimension_semantics=("parallel", "parallel", "arbitrary"), vmem_limit_bytes=VMEM_LIMIT),
        name=name)(*([a, b] + ([res] if has_res else [])))


def _seq_block(t):
    return min(512, t)


def _causal_diff(i, j, blk):
    r = lax.broadcasted_iota(jnp.int32, (blk, blk), 0)
    c = lax.broadcasted_iota(jnp.int32, (blk, blk), 1)
    return (i - j) * blk + r - c


def _ret_decay(lg, i, j, blk):
    diff = _causal_diff(i, j, blk)
    return jnp.where(diff >= 0, jnp.exp(lg * jnp.maximum(diff, 0).astype(F32)), 0.0)


def _pair_call(body, lead_grid, nb, key_major, in_specs, out_specs, out_shape, scratch, name, args, gather=()):
    pairs = [(i, j) for j in range(nb) for i in range(j, nb)] if key_major else [(i, j) for i in range(nb) for j in range(i + 1)]
    tables = [jnp.array([p[k] for p in pairs], jnp.int32) for k in (0, 1)]
    lead = len(lead_grid)
    n_steps = math.prod(lead_grid) * len(pairs)
    n_bg = len(gather)

    def on_pairs(spec):
        if spec.block_shape is None:
            return spec

        def index_map(*a):
            i, j = a[lead + 1][a[lead]], a[lead + 2][a[lead]]
            return spec.index_map(*a[:lead], *((j, i) if key_major else (i, j)))

        return pl.BlockSpec(spec.block_shape, index_map)

    many = isinstance(out_specs, (list, tuple))
    out_specs, out_shape = (list(out_specs), list(out_shape)) if many else ([out_specs], [out_shape])
    n_in, n_out = len(in_specs), len(out_specs)

    def wrapped(i_ref, j_ref, *refs):
        p = pl.program_id(lead)
        own = refs[:n_in] + refs[n_in + n_bg:n_in + n_bg + n_out] + refs[n_in + 2 * n_bg + n_out:len(refs) - (3 if n_bg else 0)]
        if n_bg:
            step = p + (pl.program_id(0) * len(pairs) if lead else 0)
            start, forward, finish = _gather_phases(refs[n_in:n_in + n_bg], refs[n_in + n_bg + n_out:n_in + 2 * n_bg + n_out], *refs[-3:])
            pl.when(step == 0)(start)
        body(i_ref[p], j_ref[p], *own)
        if n_bg:
            pl.when(step == n_steps * 3 // 4)(forward)
            pl.when(step == n_steps - 1)(finish)

    sems = [pltpu.SemaphoreType.DMA((GATHER_SEMS * n_bg,))] * 2 + [pltpu.SemaphoreType.DMA((n_bg,))] if n_bg else []
    grid_spec = pltpu.PrefetchScalarGridSpec(
        num_scalar_prefetch=2, grid=tuple(lead_grid) + (len(pairs),), in_specs=[on_pairs(s) for s in in_specs] + [_HBM] * n_bg,
        out_specs=[on_pairs(s) for s in out_specs] + [_HBM] * n_bg, scratch_shapes=list(scratch) + sems)
    res = pl.pallas_call(
        wrapped, grid_spec=grid_spec, out_shape=out_shape + [jax.ShapeDtypeStruct((N_DEV,) + g.shape, g.dtype) for g in gather],
        compiler_params=pltpu.CompilerParams(dimension_semantics=("arbitrary",) * (lead + 1), vmem_limit_bytes=VMEM_LIMIT),
        name=name)(*tables, *args, *gather)
    return res if many or n_bg else res[0]


def _ret_row_decays(lg, i, j, blk):
    row = lax.broadcasted_iota(jnp.int32, (blk, RET_DK), 0)
    return jnp.exp(lg * row.astype(F32)), jnp.exp(lg * ((i - j) * blk - row).astype(F32))


def _ret_scaled(lg, i, j, blk, q_ref, k_ref):
    a, b = _ret_row_decays(lg, i, j, blk)
    return (q_ref[...] * a).astype(BF16), (k_ref[...] * b).astype(BF16)


def _ret_log_gamma():
    return jnp.log1p(-(2.0 ** (-5.0 - jnp.arange(RET_HEADS, dtype=F32))))


def _ret_fwd(qr, kr, proj, t, gather=()):
    blk = _seq_block(t)
    nb = t // blk
    v_off = (2 * RET_HEADS * RET_DK) // RET_DV

    def body(i, j, lg_ref, q_ref, k_ref, v_ref, o_ref, acc):
        h = pl.program_id(0)

        @pl.when(j == 0)
        def _():
            acc[...] = jnp.zeros_like(acc)

        @pl.when(j < i)
        def _():
            qa, kb = _ret_scaled(lg_ref[h], i, j, blk, q_ref, k_ref)
            p = lax.dot_general(qa, kb, _DIMS["nt"], preferred_element_type=F32).astype(BF16)
            acc[...] += jnp.dot(p, v_ref[...].astype(BF16), preferred_element_type=F32)

        @pl.when(j == i)
        def _():
            s = lax.dot_general(q_ref[...].astype(BF16), k_ref[...].astype(BF16), _DIMS["nt"], preferred_element_type=F32)
            p = (s * _ret_decay(lg_ref[h], i, j, blk)).astype(BF16)
            o_ref[...] = acc[...] + jnp.dot(p, v_ref[...].astype(BF16), preferred_element_type=F32)

    return _pair_call(
        body, (RET_HEADS,), nb, False,
        [pl.BlockSpec(memory_space=pltpu.SMEM),
         pl.BlockSpec((blk, RET_DK), lambda h, i, j: (i, h)),
         pl.BlockSpec((blk, RET_DK), lambda h, i, j: (j, h)),
         pl.BlockSpec((blk, RET_DV), lambda h, i, j: (j, v_off + h))],
        pl.BlockSpec((blk, RET_DV), lambda h, i, j: (i, h)),
        jax.ShapeDtypeStruct((t, RET_HEADS * RET_DV), F32), [pltpu.VMEM((blk, RET_DV), F32)],
        "ret_fwd", (_ret_log_gamma(), qr, kr, proj), gather)


def _ret_bwd(qr, kr, proj, dr, t):
    blk = _seq_block(t)
    nb = t // blk
    v_off = (2 * RET_HEADS * RET_DK) // RET_DV

    def dq_body(i, j, lg_ref, q_ref, k_ref, v_ref, do_ref, dq_ref, acc):
        h = pl.program_id(0)

        @pl.when(j == 0)
        def _():
            acc[...] = jnp.zeros_like(acc)

        @pl.when(j < i)
        def _():
            a, b = _ret_row_decays(lg_ref[h], i, j, blk)
            ds = lax.dot_general(do_ref[...].astype(BF16), v_ref[...].astype(BF16), _DIMS["nt"], preferred_element_type=F32)
            acc[...] += a * jnp.dot(ds.astype(BF16), (k_ref[...] * b).astype(BF16), preferred_element_type=F32)

        @pl.when(j == i)
        def _():
            ds = lax.dot_general(do_ref[...].astype(BF16), v_ref[...].astype(BF16), _DIMS["nt"], preferred_element_type=F32)
            dsm = (ds * _ret_decay(lg_ref[h], i, j, blk)).astype(BF16)
            dq_ref[...] = acc[...] + jnp.dot(dsm, k_ref[...].astype(BF16), preferred_element_type=F32)

    dq = _pair_call(
        dq_body, (RET_HEADS,), nb, False,
        [pl.BlockSpec(memory_space=pltpu.SMEM),
         pl.BlockSpec((blk, RET_DK), lambda h, i, j: (i, h)),
         pl.BlockSpec((blk, RET_DK), lambda h, i, j: (j, h)),
         pl.BlockSpec((blk, RET_DV), lambda h, i, j: (j, v_off + h)),
         pl.BlockSpec((blk, RET_DV), lambda h, i, j: (i, h))],
        pl.BlockSpec((blk, RET_DK), lambda h, i, j: (i, h)),
        jax.ShapeDtypeStruct((t, RET_HEADS * RET_DK), F32), [pltpu.VMEM((blk, RET_DK), F32)],
        "ret_bwd_dq", (_ret_log_gamma(), qr, kr, proj, dr))

    def dkv_body(i, j, lg_ref, q_ref, k_ref, v_ref, do_ref, dk_ref, dv_ref, acc_k, acc_v):
        h = pl.program_id(0)

        @pl.when(i == j)
        def _():
            acc_k[...] = jnp.zeros_like(acc_k)
            acc_v[...] = jnp.zeros_like(acc_v)

        @pl.when(i > j)
        def _():
            a, b = _ret_row_decays(lg_ref[h], i, j, blk)
            qa, kb = (q_ref[...] * a).astype(BF16), (k_ref[...] * b).astype(BF16)
            do = do_ref[...].astype(BF16)
            p = lax.dot_general(qa, kb, _DIMS["nt"], preferred_element_type=F32).astype(BF16)
            acc_v[...] += lax.dot_general(p, do, _DIMS["tn"], preferred_element_type=F32)
            ds = lax.dot_general(do, v_ref[...].astype(BF16), _DIMS["nt"], preferred_element_type=F32).astype(BF16)
            acc_k[...] += b * lax.dot_general(ds, qa, _DIMS["tn"], preferred_element_type=F32)

        @pl.when(i == j)
        def _():
            q = q_ref[...].astype(BF16)
            do = do_ref[...].astype(BF16)
            decay = _ret_decay(lg_ref[h], i, j, blk)
            s = lax.dot_general(q, k_ref[...].astype(BF16), _DIMS["nt"], preferred_element_type=F32)
            acc_v[...] += lax.dot_general((s * decay).astype(BF16), do, _DIMS["tn"], preferred_element_type=F32)
            ds = lax.dot_general(do, v_ref[...].astype(BF16), _DIMS["nt"], preferred_element_type=F32)
            acc_k[...] += lax.dot_general((ds * decay).astype(BF16), q, _DIMS["tn"], preferred_element_type=F32)

        @pl.when(i == nb - 1)
        def _():
            dk_ref[...] = acc_k[...]
            dv_ref[...] = acc_v[...].astype(BF16)

    dk, dv = _pair_call(
        dkv_body, (RET_HEADS,), nb, True,
        [pl.BlockSpec(memory_space=pltpu.SMEM),
         pl.BlockSpec((blk, RET_DK), lambda h, j, i: (i, h)),
         pl.BlockSpec((blk, RET_DK), lambda h, j, i: (j, h)),
         pl.BlockSpec((blk, RET_DV), lambda h, j, i: (j, v_off + h)),
         pl.BlockSpec((blk, RET_DV), lambda h, j, i: (i, h))],
        [pl.BlockSpec((blk, RET_DK), lambda h, j, i: (j, h)), pl.BlockSpec((blk, RET_DV), lambda h, j, i: (j, h))],
        [jax.ShapeDtypeStruct((t, RET_HEADS * RET_DK), F32), jax.ShapeDtypeStruct((t, RET_HEADS * RET_DV), BF16)],
        [pltpu.VMEM((blk, RET_DK), F32), pltpu.VMEM((blk, RET_DV), F32)],
        "ret_bwd_dkv", (_ret_log_gamma(), qr, kr, proj, dr))
    return dq, dk, dv


def _cumsum(x, reverse, name):
    t = x.shape[0]
    blk = _seq_block(t)
    nb = t // blk

    def body(x_ref, o_ref, carry):
        @pl.when(pl.program_id(0) == 0)
        def _():
            carry[...] = jnp.zeros_like(carry)

        r = lax.broadcasted_iota(jnp.int32, (blk, blk), 0)
        c = lax.broadcasted_iota(jnp.int32, (blk, blk), 1)
        tri = ((r <= c) if reverse else (r >= c)).astype(F32)
        o_ref[...] = jnp.dot(tri, x_ref[...], precision=HIGHEST, preferred_element_type=F32) + carry[...]
        carry[...] = o_ref[0:1, :] if reverse else o_ref[blk - 1:blk, :]

    idx = (lambda i: (nb - 1 - i, 0)) if reverse else (lambda i: (i, 0))
    return pl.pallas_call(
        body, grid=(nb,), in_specs=[pl.BlockSpec((blk, LANES), idx)], out_specs=pl.BlockSpec((blk, LANES), idx),
        out_shape=jax.ShapeDtypeStruct((t, LANES), F32), scratch_shapes=[pltpu.VMEM((1, LANES), F32)],
        compiler_params=pltpu.CompilerParams(dimension_semantics=("arbitrary",), vmem_limit_bytes=VMEM_LIMIT),
        name=name)(x)


def _ssd_decay(a_ref, at_ref, hh, mask):
    return jnp.exp(jnp.where(mask, a_ref[:, hh:hh + 1] - at_ref[hh:hh + 1, :], -jnp.inf))


def _head_lanes(s):
    lane = lax.broadcasted_iota(jnp.int32, (s.shape[0], LANES), 1)
    return jnp.concatenate([jnp.where(lane < SSM_P, s[:, 2 * p:2 * p + 1], s[:, 2 * p + 1:2 * p + 2])
                            for p in range(SSM_HEADS // 2)], axis=1)


_B_OFF, _C_OFF = 1024 // SSM_N, 1024 // SSM_N + SSM_GROUPS


def _ssd_fwd(xbc_act, xdt, a_cum, a_cum_t, t, gather=()):
    blk = _seq_block(t)
    nb = t // blk

    def body(i, j, c0, c1, b0, b1, x_ref, ai_ref, aj_ref, at_ref, o_ref, acc):
        @pl.when(j == 0)
        def _():
            acc[...] = jnp.zeros_like(acc)

        @pl.when(j < i)
        def _():
            xv = (x_ref[...] * _head_lanes(jnp.exp(ai_ref[0:1, :] - aj_ref[...]))).astype(BF16)
            for g, (c_ref, b_ref) in enumerate(((c0, b0), (c1, b1))):
                cb = lax.dot_general(c_ref[...].astype(BF16), b_ref[...].astype(BF16), _DIMS["nt"], preferred_element_type=F32)
                cols = slice(g * half, (g + 1) * half)
                acc[:, cols] += jnp.dot(cb.astype(BF16), xv[:, cols], preferred_element_type=F32)

        @pl.when(j == i)
        def _():
            o_ref[...] = acc[...] * _head_lanes(jnp.exp(ai_ref[...] - ai_ref[0:1, :]))
            mask = _causal_diff(i, j, blk) >= 0
            for g, (c_ref, b_ref) in enumerate(((c0, b0), (c1, b1))):
                cb = lax.dot_general(c_ref[...].astype(BF16), b_ref[...].astype(BF16), _DIMS["nt"], preferred_element_type=F32)
                for h in range(SSM_HG):
                    hh = g * SSM_HG + h
                    cols = slice(hh * SSM_P, (hh + 1) * SSM_P)
                    m = (cb * _ssd_decay(ai_ref, at_ref, hh, mask)).astype(BF16)
                    o_ref[:, cols] += jnp.dot(m, x_ref[:, cols].astype(BF16), preferred_element_type=F32)

    half = SSM_HG * SSM_P
    row_i = lambda off: pl.BlockSpec((blk, SSM_N), lambda i, j, off=off: (i, off))
    row_j = lambda off: pl.BlockSpec((blk, SSM_N), lambda i, j, off=off: (jnp.minimum(j, i), off))
    return _pair_call(
        body, (), nb, False,
        [row_i(_C_OFF), row_i(_C_OFF + 1), row_j(_B_OFF), row_j(_B_OFF + 1),
         pl.BlockSpec((blk, SSM_HEADS * SSM_P), lambda i, j: (j, 0)),
         pl.BlockSpec((blk, LANES), lambda i, j: (i, 0)),
         pl.BlockSpec((blk, LANES), lambda i, j: (j, 0)),
         pl.BlockSpec((SSM_HEADS, blk), lambda i, j: (0, j))],
        pl.BlockSpec((blk, SSM_HEADS * SSM_P), lambda i, j: (i, 0)),
        jax.ShapeDtypeStruct((t, SSM_HEADS * SSM_P), F32), [pltpu.VMEM((blk, SSM_HEADS * SSM_P), F32)],
        "ssd_fwd", (xbc_act, xbc_act, xbc_act, xbc_act, xdt, a_cum, a_cum, a_cum_t), gather)


def _ssd_bwd(xbc_act, xdt, a_cum, a_cum_t, dy, expand, t):
    blk = _seq_block(t)
    nb = t // blk
    width = SSM_HEADS * SSM_P

    half = SSM_HG * SSM_P

    def head_sums(prod, e_ref):
        return lax.dot_general(prod, e_ref[...], _DIMS["nt"], precision=HIGHEST, preferred_element_type=F32)

    def q_body(i, j, c0, c1, b0, b1, x_ref, ai_ref, aj_ref, at_ref, dy_ref, e_ref, dc_ref, da_ref, acc_c, acc_a, acc_p, dyu):
        @pl.when(j == 0)
        def _():
            acc_c[...] = jnp.zeros_like(acc_c)
            acc_a[...] = jnp.zeros_like(acc_a)
            acc_p[...] = jnp.zeros_like(acc_p)
            dyu[...] = (dy_ref[...].astype(F32) * _head_lanes(jnp.exp(ai_ref[...] - ai_ref[0:1, :]))).astype(BF16)

        @pl.when(j < i)
        def _():
            xv = (x_ref[...] * _head_lanes(jnp.exp(ai_ref[0:1, :] - aj_ref[...]))).astype(BF16)
            for g, (c_ref, b_ref) in enumerate(((c0, b0), (c1, b1))):
                cols = slice(g * half, (g + 1) * half)
                bj = b_ref[...].astype(BF16)
                cb = lax.dot_general(c_ref[...].astype(BF16), bj, _DIMS["nt"], preferred_element_type=F32).astype(BF16)
                dcb = lax.dot_general(dyu[:, cols], xv[:, cols], _DIMS["nt"], preferred_element_type=F32)
                acc_c[:, g * SSM_N:(g + 1) * SSM_N] += jnp.dot(dcb.astype(BF16), bj, preferred_element_type=F32)
                acc_p[:, cols] += dyu[:, cols].astype(F32) * jnp.dot(cb, xv[:, cols], preferred_element_type=F32)

        @pl.when(j == i)
        def _():
            mask = _causal_diff(i, j, blk) >= 0
            for g, (c_ref, b_ref) in enumerate(((c0, b0), (c1, b1))):
                bj = b_ref[...].astype(BF16)
                cb = lax.dot_general(c_ref[...].astype(BF16), bj, _DIMS["nt"], preferred_element_type=F32)
                dcb = jnp.zeros((blk, blk), F32)
                for h in range(SSM_HG):
                    hh = g * SSM_HG + h
                    cols = slice(hh * SSM_P, (hh + 1) * SSM_P)
                    dm = lax.dot_general(dy_ref[:, cols].astype(BF16), x_ref[:, cols].astype(BF16), _DIMS["nt"],
                                         preferred_element_type=F32) * _ssd_decay(ai_ref, at_ref, hh, mask)
                    dcb = dcb + dm
                    acc_a[:, hh:hh + 1] += jnp.sum(dm * cb, axis=1, keepdims=True)
                acc_c[:, g * SSM_N:(g + 1) * SSM_N] += jnp.dot(dcb.astype(BF16), bj, preferred_element_type=F32)
            dc_ref[...] = acc_c[...]
            da_ref[...] = acc_a[...] + head_sums(acc_p[...], e_ref)

    row_i = lambda off: pl.BlockSpec((blk, SSM_N), lambda i, j, off=off: (i, off))
    row_j = lambda off: pl.BlockSpec((blk, SSM_N), lambda i, j, off=off: (jnp.minimum(j, i), off))
    e_spec = pl.BlockSpec((LANES, width), lambda i, j: (0, 0))
    dc, da_q = _pair_call(
        q_body, (), nb, False,
        [row_i(_C_OFF), row_i(_C_OFF + 1), row_j(_B_OFF), row_j(_B_OFF + 1),
         pl.BlockSpec((blk, width), lambda i, j: (j, 0)),
         pl.BlockSpec((blk, LANES), lambda i, j: (i, 0)),
         pl.BlockSpec((blk, LANES), lambda i, j: (j, 0)),
         pl.BlockSpec((SSM_HEADS, blk), lambda i, j: (0, j)),
         pl.BlockSpec((blk, width), lambda i, j: (i, 0)), e_spec],
        [pl.BlockSpec((blk, SSM_GROUPS * SSM_N), lambda i, j: (i, 0)), pl.BlockSpec((blk, LANES), lambda i, j: (i, 0))],
        [jax.ShapeDtypeStruct((t, SSM_GROUPS * SSM_N), F32), jax.ShapeDtypeStruct((t, LANES), F32)],
        [pltpu.VMEM((blk, SSM_GROUPS * SSM_N), F32), pltpu.VMEM((blk, LANES), F32), pltpu.VMEM((blk, width), F32),
         pltpu.VMEM((blk, width), BF16)],
        "ssd_bwd_q", (xbc_act, xbc_act, xbc_act, xbc_act, xdt, a_cum, a_cum, a_cum_t, dy, expand))

    def k_body(i, j, c0, c1, b0, b1, x_ref, ai_ref, aj_ref, at_ref, dy_ref, e_ref, db_ref, dx_ref, da_ref, dat_ref, acc_b, acc_x, acc_a,
               acc_p):
        @pl.when(i == j)
        def _():
            acc_b[...] = jnp.zeros_like(acc_b)
            acc_x[...] = jnp.zeros_like(acc_x)
            acc_a[...] = jnp.zeros_like(acc_a)
            acc_p[...] = jnp.zeros_like(acc_p)

        @pl.when(i > j)
        def _():
            v = _head_lanes(jnp.exp(ai_ref[0:1, :] - aj_ref[...]))
            dyu_all = (dy_ref[...].astype(F32) * _head_lanes(jnp.exp(ai_ref[...] - ai_ref[0:1, :]))).astype(BF16)
            xv = (x_ref[...] * v).astype(BF16)
            for g, (c_ref, b_ref) in enumerate(((c0, b0), (c1, b1))):
                cols = slice(g * half, (g + 1) * half)
                ci = c_ref[...].astype(BF16)
                cb = lax.dot_general(ci, b_ref[...].astype(BF16), _DIMS["nt"], preferred_element_type=F32).astype(BF16)
                dcb = lax.dot_general(dyu_all[:, cols], xv[:, cols], _DIMS["nt"], preferred_element_type=F32)
                acc_b[:, g * SSM_N:(g + 1) * SSM_N] += lax.dot_general(dcb.astype(BF16), ci, _DIMS["tn"], preferred_element_type=F32)
                dxv = lax.dot_general(cb, dyu_all[:, cols], _DIMS["tn"], preferred_element_type=F32)
                acc_x[:, cols] += v[:, cols] * dxv
                acc_p[:, cols] += xv[:, cols].astype(F32) * dxv

        @pl.when(i == j)
        def _():
            mask = _causal_diff(i, j, blk) >= 0
            for g, (c_ref, b_ref) in enumerate(((c0, b0), (c1, b1))):
                ci = c_ref[...].astype(BF16)
                cb = lax.dot_general(ci, b_ref[...].astype(BF16), _DIMS["nt"], preferred_element_type=F32)
                dcb = jnp.zeros((blk, blk), F32)
                for h in range(SSM_HG):
                    hh = g * SSM_HG + h
                    cols = slice(hh * SSM_P, (hh + 1) * SSM_P)
                    decay = _ssd_decay(ai_ref, at_ref, hh, mask)
                    dyh = dy_ref[:, cols].astype(BF16)
                    acc_x[:, cols] += lax.dot_general((cb * decay).astype(BF16), dyh, _DIMS["tn"], preferred_element_type=F32)
                    dm = lax.dot_general(dyh, x_ref[:, cols].astype(BF16), _DIMS["nt"], preferred_element_type=F32) * decay
                    dcb = dcb + dm
                    acc_a[hh:hh + 1, :] += jnp.sum(dm * cb, axis=0, keepdims=True)
                acc_b[:, g * SSM_N:(g + 1) * SSM_N] += lax.dot_general(dcb.astype(BF16), ci, _DIMS["tn"], preferred_element_type=F32)

        @pl.when(i == nb - 1)
        def _():
            db_ref[...] = acc_b[...]
            dx_ref[...] = acc_x[...]
            dat_ref[...] = acc_a[...]
            da_ref[...] = head_sums(acc_p[...], e_ref)

    rowk_i = lambda off: pl.BlockSpec((blk, SSM_N), lambda j, i, off=off: (jnp.maximum(i, j), off))
    rowk_j = lambda off: pl.BlockSpec((blk, SSM_N), lambda j, i, off=off: (j, off))
    db, dx, da_k, da_k_t = _pair_call(
        k_body, (), nb, True,
        [rowk_i(_C_OFF), rowk_i(_C_OFF + 1), rowk_j(_B_OFF), rowk_j(_B_OFF + 1),
         pl.BlockSpec((blk, width), lambda j, i: (j, 0)),
         pl.BlockSpec((blk, LANES), lambda j, i: (i, 0)),
         pl.BlockSpec((blk, LANES), lambda j, i: (j, 0)),
         pl.BlockSpec((SSM_HEADS, blk), lambda j, i: (0, j)),
         pl.BlockSpec((blk, width), lambda j, i: (i, 0)), e_spec],
        [pl.BlockSpec((blk, SSM_GROUPS * SSM_N), lambda j, i: (j, 0)), pl.BlockSpec((blk, width), lambda j, i: (j, 0)),
         pl.BlockSpec((blk, LANES), lambda j, i: (j, 0)), pl.BlockSpec((SSM_HEADS, blk), lambda j, i: (0, j))],
        [jax.ShapeDtypeStruct((t, SSM_GROUPS * SSM_N), F32), jax.ShapeDtypeStruct((t, width), F32),
         jax.ShapeDtypeStruct((t, LANES), F32), jax.ShapeDtypeStruct((SSM_HEADS, t), F32)],
        [pltpu.VMEM((blk, SSM_GROUPS * SSM_N), F32), pltpu.VMEM((blk, width), F32), pltpu.VMEM((SSM_HEADS, blk), F32),
         pltpu.VMEM((blk, width), F32)],
        "ssd_bwd_k", (xbc_act, xbc_act, xbc_act, xbc_act, xdt, a_cum, a_cum, a_cum_t, dy, expand))
    return dc, da_q, db, dx, da_k, da_k_t


S5_SUPER = 4
S5_NARROW, S5_WIDE = S5_GROUPS * S5_GROUP // S5_SUPER, S5_GROUPS * S5_STATE // S5_SUPER


def _s5_tiles(t):
    tm = min(1024, t)
    narrow = lambda off: pl.BlockSpec((tm, S5_NARROW), lambda s, i, off=off: (i, s + off))
    wide = pl.BlockSpec((tm, S5_WIDE), lambda s, i: (i, s))
    weight = lambda shape: pl.BlockSpec((1,) + shape, lambda s, i: (s, 0, 0))
    params = pltpu.CompilerParams(dimension_semantics=("arbitrary", "arbitrary"), vmem_limit_bytes=VMEM_LIMIT)
    return tm, narrow, wide, weight, params


def _s5_expand(a, a_off, w, w_is_wide_first, name):
    t = a.shape[0]
    tm, narrow, wide, weight, params = _s5_tiles(t)

    def body(a_ref, w_ref, re_ref, im_ref):
        dims = _DIMS["nt"] if w_is_wide_first else _DIMS["nn"]
        o = lax.dot_general(a_ref[...].astype(BF16), w_ref[0].astype(BF16), dims, preferred_element_type=F32)
        re_ref[...] = o[:, :S5_WIDE]
        im_ref[...] = o[:, S5_WIDE:]

    return pl.pallas_call(
        body, grid=(S5_SUPER, t // tm), in_specs=[narrow(a_off), weight(w.shape[1:])], out_specs=[wide, wide],
        out_shape=[jax.ShapeDtypeStruct((t, S5_COLS), F32)] * 2, compiler_params=params, name=name)(a, w)


def _s5_contract(x_re, x_im, w, w_is_wide_first, name, res=None, out_dtype=F32):
    t = x_re.shape[0]
    tm, narrow, wide, weight, params = _s5_tiles(t)
    has_res = res is not None

    def body(*refs):
        re_ref, im_ref, w_ref, o_ref = refs[0], refs[1], refs[2], refs[-1]
        x = jnp.concatenate([re_ref[...].astype(BF16), im_ref[...].astype(BF16)], axis=1)
        dims = _DIMS["nn"] if w_is_wide_first else _DIMS["nt"]
        o = lax.dot_general(x, w_ref[0].astype(BF16), dims, preferred_element_type=F32)
        o_ref[...] = (o + refs[3][...] if has_res else o).astype(out_dtype)

    return pl.pallas_call(
        body, grid=(S5_SUPER, t // tm), in_specs=[wide, wide, weight(w.shape[1:])] + ([narrow(0)] if has_res else []),
        out_specs=narrow(0), out_shape=jax.ShapeDtypeStruct((t, S5_SUPER * S5_NARROW), out_dtype), compiler_params=params,
        name=name)(*([x_re, x_im, w] + ([res] if has_res else [])))


def _s5_wgrad(a, a_off, x_re, x_im, wide_first, name):
    t = a.shape[0]
    tm, narrow, wide, weight, params = _s5_tiles(t)
    shape = (2 * S5_WIDE, S5_NARROW) if wide_first else (S5_NARROW, 2 * S5_WIDE)

    def body(a_ref, re_ref, im_ref, o_ref):
        @pl.when(pl.program_id(1) == 0)
        def _():
            o_ref[...] = jnp.zeros_like(o_ref)

        x = jnp.concatenate([re_ref[...].astype(BF16), im_ref[...].astype(BF16)], axis=1)
        av = a_ref[...].astype(BF16)
        o_ref[0] += lax.dot_general(x, av, _DIMS["tn"], preferred_element_type=F32) if wide_first else \
            lax.dot_general(av, x, _DIMS["tn"], preferred_element_type=F32)

    return pl.pallas_call(
        body, grid=(S5_SUPER, t // tm), in_specs=[narrow(a_off), wide, wide], out_specs=weight(shape),
        out_shape=jax.ShapeDtypeStruct((S5_SUPER,) + shape, F32), compiler_params=params, name=name)(a, x_re, x_im)


def _s5_scan(b_re, b_im, a_re, a_im, reverse, x_prev=None):
    t = b_re.shape[0]
    cb = 512
    ncb = S5_COLS // cb
    tb = min(1024, t)
    ntb = t // tb
    sub = 8
    shape = (sub, cb)

    def cmul(ar, ai, br, bi):
        return ar * br - ai * bi, ar * bi + ai * br

    def body(*refs):
        if reverse:
            br_ref, bi_ref, ar_ref, ai_ref, xr_ref, xi_ref, or_ref, oi_ref, dar_ref, dai_ref, carry, dacc, states = refs
        else:
            br_ref, bi_ref, ar_ref, ai_ref, or_ref, oi_ref, or_bf_ref, oi_bf_ref, carry = refs

        @pl.when(pl.program_id(1) == 0)
        def _():
            carry[...] = jnp.zeros_like(carry)
            if reverse:
                dacc[...] = jnp.zeros_like(dacc)

        row = lax.broadcasted_iota(jnp.int32, shape, 0)
        a1 = (jnp.broadcast_to(ar_ref[...], shape), jnp.broadcast_to(-ai_ref[...] if reverse else ai_ref[...], shape))
        a2 = cmul(*a1, *a1)
        a4 = cmul(*a2, *a2)
        a8 = cmul(*a4, *a4)
        steps = (sub - row) if reverse else (row + 1)
        pw = (jnp.ones(shape, F32), jnp.zeros(shape, F32))
        for bit, p in ((1, a1), (2, a2), (4, a4), (8, a8)):
            on = (steps & bit) != 0
            pw = cmul(*pw, jnp.where(on, p[0], 1.0), jnp.where(on, p[1], 0.0))
        edge = 0 if reverse else sub - 1

        def shift(v, s):
            if reverse:
                return jnp.where(row < sub - s, pltpu.roll(v, sub - s, 0), 0.0)
            return jnp.where(row >= s, pltpu.roll(v, s, 0), 0.0)

        def tile(n, state):
            r0 = pl.multiple_of((tb // sub - 1 - n if reverse else n) * sub, sub)
            rows = pl.ds(r0, sub)
            xr, xi = br_ref[rows, :], bi_ref[rows, :]
            for s, p in ((1, a1), (2, a2), (4, a4)):
                dr, di = cmul(*p, shift(xr, s), shift(xi, s))
                xr, xi = xr + dr, xi + di
            cr, ci = jnp.broadcast_to(state[0], shape), jnp.broadcast_to(state[1], shape)
            dr, di = cmul(*pw, cr, ci)
            xr, xi = xr + dr, xi + di
            if reverse:
                states[0, rows, :], states[1, rows, :] = xr, xi
                gr = jnp.where(row < sub - 1, pltpu.roll(xr, sub - 1, 0), cr)
                gi = jnp.where(row < sub - 1, pltpu.roll(xi, sub - 1, 0), ci)
                pr, pi = xr_ref[rows, :], xi_ref[rows, :]
                dacc[0] += gr * pr + gi * pi
                dacc[1] += gi * pr - gr * pi
            else:
                or_ref[rows, :], oi_ref[rows, :] = xr, xi
            return (jnp.sum(jnp.where(row == edge, xr, 0.0), axis=0, keepdims=True),
                    jnp.sum(jnp.where(row == edge, xi, 0.0), axis=0, keepdims=True))

        last = lax.fori_loop(0, tb // sub, tile, (carry[0], carry[1]))
        carry[0], carry[1] = last
        if reverse:
            or_ref[...], oi_ref[...] = states[0].astype(BF16), states[1].astype(BF16)
            dar_ref[...] = jnp.sum(dacc[0], axis=0, keepdims=True)
            dai_ref[...] = jnp.sum(dacc[1], axis=0, keepdims=True)
        else:
            or_bf_ref[...], oi_bf_ref[...] = or_ref[...].astype(BF16), oi_ref[...].astype(BF16)

    tblock = (lambda k: ntb - 1 - k) if reverse else (lambda k: k)
    re_spec = pl.BlockSpec((tb, cb), lambda j, k: (tblock(k), j))
    a_spec = pl.BlockSpec((1, cb), lambda j, k: (0, j))
    ins, in_specs = [b_re, b_im, a_re, a_im], [re_spec, re_spec, a_spec, a_spec]
    seq_bf = jax.ShapeDtypeStruct((t, S5_COLS), BF16)
    scratch = [pltpu.VMEM((2, 1, cb), F32)]
    if reverse:
        ins += list(x_prev)
        in_specs += [re_spec, re_spec]
        out_shape = [seq_bf, seq_bf] + [jax.ShapeDtypeStruct((1, S5_COLS), F32)] * 2
        out_specs = [re_spec, re_spec, a_spec, a_spec]
        scratch += [pltpu.VMEM((2, sub, cb), F32), pltpu.VMEM((2, tb, cb), F32)]
    else:
        out_shape = [jax.ShapeDtypeStruct((t, S5_COLS), F32)] * 2 + [seq_bf, seq_bf]
        out_specs = [re_spec] * 4
    return pl.pallas_call(
        body, grid=(ncb, ntb), in_specs=in_specs, out_specs=out_specs, out_shape=out_shape, scratch_shapes=scratch,
        compiler_params=pltpu.CompilerParams(dimension_semantics=("arbitrary", "arbitrary"), vmem_limit_bytes=VMEM_LIMIT),
        name="s5_scan_bwd" if reverse else "s5_scan_fwd")(*ins)


CONF_DIM = 512
CONF_K = 31
CONF_PAD = 32


def _conf_conv(proj_o, w, b, dc1=None):
    t = proj_o.shape[0]
    cb = LANES
    ncb = CONF_DIM // cb
    chunk = min(512, t)
    chunks = range(0, t, chunk)
    bwd = dc1 is not None

    def body(*refs):
        if bwd:
            ca_ref, cg_ref, w_ref, b_ref, dy_ref, dca_ref, dcg_ref, dw_ref, db_ref, xs, dys = refs
        else:
            ca_ref, cg_ref, w_ref, b_ref, o_ref, xs = refs
        xs[pl.ds(0, CONF_PAD), :] = jnp.zeros((CONF_PAD, cb), F32)
        for t0 in chunks:
            rows = pl.ds(t0, chunk)
            xs[pl.ds(CONF_PAD + t0, chunk), :] = ca_ref[rows, :] * jax.nn.sigmoid(cg_ref[rows, :])
        if not bwd:
            for t0 in chunks:
                acc = jnp.broadcast_to(b_ref[...], (chunk, cb))
                for k in range(CONF_K):
                    acc = acc + w_ref[k] * xs[pl.ds(CONF_PAD + t0 - (CONF_K - 1 - k), chunk), :]
                o_ref[pl.ds(t0, chunk), :] = acc
            return
        dys[pl.ds(t, CONF_PAD), :] = jnp.zeros((CONF_PAD, cb), F32)
        db = jnp.zeros((1, cb), F32)
        for t0 in chunks:
            dys[pl.ds(t0, chunk), :] = dy_ref[pl.ds(t0, chunk), :]
            db = db + jnp.sum(dy_ref[pl.ds(t0, chunk), :], axis=0, keepdims=True)
        db_ref[...] = db
        for t0 in chunks:
            rows = pl.ds(t0, chunk)
            acc = jnp.zeros((chunk, cb), F32)
            for k in range(CONF_K):
                acc = acc + w_ref[k] * dys[pl.ds(t0 + (CONF_K - 1 - k), chunk), :]
            sig = jax.nn.sigmoid(cg_ref[rows, :])
            dca_ref[rows, :] = (acc * sig).astype(BF16)
            dcg_ref[rows, :] = (acc * ca_ref[rows, :] * sig * (1.0 - sig)).astype(BF16)
        for k in range(CONF_K):
            dwk = jnp.zeros((1, cb), F32)
            for t0 in chunks:
                window = xs[pl.ds(CONF_PAD + t0 - (CONF_K - 1 - k), chunk), :]
                dwk = dwk + jnp.sum(dy_ref[pl.ds(t0, chunk), :] * window, axis=0, keepdims=True)
            dw_ref[k] = dwk

    col = lambda off: pl.BlockSpec((t, cb), lambda j, off=off: (0, j + off))
    w_spec = pl.BlockSpec((CONF_K, 1, cb), lambda j: (0, 0, j))
    b_spec = pl.BlockSpec((1, cb), lambda j: (0, j))
    seq = jax.ShapeDtypeStruct((t, CONF_DIM), F32)
    ins, in_specs = [proj_o, proj_o, w, b], [col(0), col(ncb), w_spec, b_spec]
    scratch = [pltpu.VMEM((CONF_PAD + t, cb), F32)]
    if bwd:
        ins, in_specs = ins + [dc1], in_specs + [col(0)]
        seq_bf = jax.ShapeDtypeStruct((t, CONF_DIM), BF16)
        out_shape, out_specs = [seq_bf, seq_bf, jax.ShapeDtypeStruct(w.shape, F32), jax.ShapeDtypeStruct(b.shape, F32)], [col(0), col(0), w_spec, b_spec]
        scratch = scratch + [pltpu.VMEM((t + CONF_PAD, cb), F32)]
    else:
        out_shape, out_specs = seq, col(0)
    return pl.pallas_call(
        body, grid=(ncb,), in_specs=in_specs, out_specs=out_specs, out_shape=out_shape, scratch_shapes=scratch,
        compiler_params=pltpu.CompilerParams(dimension_semantics=("arbitrary",), vmem_limit_bytes=VMEM_LIMIT),
        name="conf_conv_bwd" if bwd else "conf_conv")(*ins)


N_CHIPS = 4
_HBM = pl.BlockSpec(memory_space=pl.ANY)
_MESH_ID = pl.DeviceIdType.MESH


def _comm_call(body, srcs, out_shapes, n_sems, name):
    n = len(srcs)
    return pl.pallas_call(
        body, out_shape=out_shapes, in_specs=[_HBM] * n, out_specs=[_HBM] * n,
        scratch_shapes=[pltpu.SemaphoreType.DMA((n_sems,)), pltpu.SemaphoreType.DMA((n_sems,)), pltpu.SemaphoreType.DMA((n,))],
        compiler_params=pltpu.CompilerParams(has_side_effects=True), name=name)(*srcs)


GATHER_SEMS = N_DEV - 1


def _gather_phases(src_refs, out_refs, send_sems, recv_sems, local_sems):
    n = len(src_refs)
    x, y, c = lax.axis_index("x"), lax.axis_index("y"), lax.axis_index("c")
    me, sibling = (x, y, c), (x, y, 1 - c)
    chips = [(1 - x, y), (x, 1 - y), (1 - x, 1 - y)]

    def copy(i, k, block, to, from_src=False):
        rows = out_refs[i].at[4 * block[0] + 2 * block[1] + block[2]]
        return pltpu.make_async_remote_copy(
            src_ref=src_refs[i] if from_src else rows, dst_ref=rows, send_sem=send_sems.at[GATHER_SEMS * i + k],
            recv_sem=recv_sems.at[GATHER_SEMS * i + k], device_id=to, device_id_type=_MESH_ID)

    def local(i):
        return pltpu.make_async_copy(src_refs[i], out_refs[i].at[4 * x + 2 * y + c], local_sems.at[i])

    def first(i):
        return [copy(i, 0, me, sibling, True)] + [copy(i, 1 + j, me, (*chip, c), True) for j, chip in enumerate(chips)]

    def passed(i, j):
        return copy(i, 4 + j, (*chips[j], c), sibling)

    def start():
        for i in range(n):
            local(i).start()
            for cp in first(i):
                cp.start()

    def forward():
        for j, chip in enumerate(chips):
            for i in range(n):
                copy(i, 1 + j, (*chip, c), me).wait_recv()
                passed(i, j).start()

    def finish():
        for i in range(n):
            copy(i, 0, sibling, me).wait_recv()
            for j, chip in enumerate(chips):
                copy(i, 4 + j, (*chip, 1 - c), me).wait_recv()
        for i in range(n):
            for cp in first(i) + [passed(i, j) for j in range(len(chips))]:
                cp.wait_send()
            local(i).wait()

    return start, forward, finish


def _gather(srcs, name):
    n = len(srcs)

    def body(*refs):
        for phase in _gather_phases(refs[:n], refs[n:2 * n], *refs[2 * n:]):
            phase()

    return _comm_call(body, srcs, [jax.ShapeDtypeStruct((N_DEV,) + s.shape, s.dtype) for s in srcs], GATHER_SEMS * n, name)


def _swap_sibling(srcs, name):
    n = len(srcs)

    def body(*refs):
        src_refs, out_refs = refs[:n], refs[n:2 * n]
        send_sems, recv_sems, _ = refs[2 * n:]
        sibling = (lax.axis_index("x"), lax.axis_index("y"), 1 - lax.axis_index("c"))
        copies = [pltpu.make_async_remote_copy(src_ref=src_refs[i], dst_ref=out_refs[i], send_sem=send_sems.at[i],
                                               recv_sem=recv_sems.at[i], device_id=sibling, device_id_type=_MESH_ID) for i in range(n)]
        for cp in copies:
            cp.start()
        for cp in copies:
            cp.wait_recv()
        for cp in copies:
            cp.wait_send()

    return _comm_call(body, srcs, [jax.ShapeDtypeStruct(s.shape, s.dtype) for s in srcs], n, name)


def _exchange_chips(srcs, name):
    n = len(srcs)
    per = N_CHIPS - 1

    def body(*refs):
        src_refs, out_refs = refs[:n], refs[n:2 * n]
        send_sems, recv_sems, local_sems = refs[2 * n:]
        x, y, c = lax.axis_index("x"), lax.axis_index("y"), lax.axis_index("c")
        mine = 2 * x + y
        local = [pltpu.make_async_copy(src_refs[i].at[mine], out_refs[i].at[mine], local_sems.at[i]) for i in range(n)]
        for cp in local:
            cp.start()
        copies = []
        for k in range(1, N_CHIPS):
            px, py = x ^ (k >> 1), y ^ (k & 1)
            peer = 2 * px + py
            for i in range(n):
                sem = per * i + k - 1
                send = pltpu.make_async_remote_copy(
                    src_ref=src_refs[i].at[peer], dst_ref=out_refs[i].at[mine], send_sem=send_sems.at[sem], recv_sem=recv_sems.at[sem],
                    device_id=(px, py, c), device_id_type=_MESH_ID)
                send.start()
                recv = pltpu.make_async_remote_copy(
                    src_ref=src_refs[i].at[peer], dst_ref=out_refs[i].at[peer], send_sem=send_sems.at[sem], recv_sem=recv_sems.at[sem],
                    device_id=(px, py, c), device_id_type=_MESH_ID)
                copies.append((send, recv))
        for _, recv in copies:
            recv.wait_recv()
        for send, _ in copies:
            send.wait_send()
        for cp in local:
            cp.wait()

    return _comm_call(body, srcs, [jax.ShapeDtypeStruct(s.shape, s.dtype) for s in srcs], per * n, name)


def _add(a, b, name):
    k, rows, cols = a.shape
    tr = _row_tile(rows)

    def body(a_ref, b_ref, o_ref):
        o_ref[...] = (a_ref[...].astype(F32) + b_ref[...].astype(F32)).astype(o_ref.dtype)

    spec = pl.BlockSpec((k, tr, cols), lambda i: (0, i, 0))
    return pl.pallas_call(
        body, grid=(rows // tr,), in_specs=[spec, spec], out_specs=spec, out_shape=jax.ShapeDtypeStruct(a.shape, a.dtype),
        compiler_params=pltpu.CompilerParams(dimension_semantics=("parallel",), vmem_limit_bytes=VMEM_LIMIT), name=name)(a, b)


def _row_tile(r, pref=256):
    if r <= pref:
        return r
    t = pref // 16 * 16
    while r % t:
        t -= 16
    return t


def _join_cols(g, width, name):
    _, rows, ws = g.shape
    tr = _row_tile(rows)
    tail = width - N_DEV * ws

    def body(g_ref, o_ref):
        for d in range(N_DEV):
            o_ref[:, pl.ds(d * ws, ws)] = g_ref[d]
        if tail:
            o_ref[:, pl.ds(N_DEV * ws, tail)] = jnp.zeros((tr, tail), g.dtype)

    return pl.pallas_call(
        body, grid=(rows // tr,), in_specs=[pl.BlockSpec((N_DEV, tr, ws), lambda i: (0, i, 0))],
        out_specs=pl.BlockSpec((tr, width), lambda i: (i, 0)), out_shape=jax.ShapeDtypeStruct((rows, width), g.dtype),
        compiler_params=pltpu.CompilerParams(dimension_semantics=("parallel",), vmem_limit_bytes=VMEM_LIMIT), name=name)(g)


def _split_cols(full, ws, dtype, name):
    rows, width = full.shape
    tr = _row_tile(rows)

    def body(x_ref, o_ref):
        for d in range(N_DEV):
            o_ref[d] = x_ref[:, pl.ds(d * ws, ws)].astype(dtype)

    return pl.pallas_call(
        body, grid=(rows // tr,), in_specs=[pl.BlockSpec((tr, width), lambda i: (i, 0))],
        out_specs=pl.BlockSpec((N_DEV, tr, ws), lambda i: (0, i, 0)), out_shape=jax.ShapeDtypeStruct((N_DEV, rows, ws), dtype),
        compiler_params=pltpu.CompilerParams(dimension_semantics=("parallel",), vmem_limit_bytes=VMEM_LIMIT), name=name)(full)


def _adamw(parts, w, m, v, name):
    r, c = w.shape
    n_parts = parts.shape[0]
    tr = _row_tile(r)

    def body(p_ref, w_ref, m_ref, v_ref, g_ref, d_ref, nm_ref, nv_ref):
        g = p_ref[0].astype(F32)
        for s in range(1, n_parts):
            g = g + p_ref[s].astype(F32)
        nm = ADAM_B1 * m_ref[...] + (1.0 - ADAM_B1) * g
        nv = ADAM_B2 * v_ref[...] + (1.0 - ADAM_B2) * (g * g)
        m_hat = nm / (1.0 - ADAM_B1 ** ADAM_STEP)
        v_hat = nv / (1.0 - ADAM_B2 ** ADAM_STEP)
        g_ref[...] = g
        nm_ref[...] = nm
        nv_ref[...] = nv
        d_ref[...] = -ADAM_LR * (m_hat / (jnp.sqrt(v_hat) + ADAM_EPS) + ADAM_WD * w_ref[...])

    spec = pl.BlockSpec((tr, c), lambda i: (i, 0))
    return pl.pallas_call(
        body, grid=(r // tr,), in_specs=[pl.BlockSpec((n_parts, tr, c), lambda i: (0, i, 0)), spec, spec, spec],
        out_specs=[spec] * 4, out_shape=[jax.ShapeDtypeStruct((r, c), F32)] * 4,
        compiler_params=pltpu.CompilerParams(dimension_semantics=("parallel",), vmem_limit_bytes=VMEM_LIMIT),
        name=name)(parts, w, m, v)


def _pack_rows(n_elems, mult):
    rows = -(-n_elems // PACK_COLS)
    return -(-rows // mult) * mult


def _pack(arrays, dtype, mult, lead=()):
    flat = jnp.concatenate([a.astype(dtype).reshape(lead + (-1,)) for a in arrays], axis=-1)
    rows = _pack_rows(flat.shape[-1], mult)
    flat = jnp.pad(flat, [(0, 0)] * len(lead) + [(0, rows * PACK_COLS - flat.shape[-1])])
    return flat.reshape(lead + (rows, PACK_COLS))


def _unpack(buf, shapes, lead=()):
    flat = buf.reshape(lead + (-1,))
    out, off = [], 0
    for s in shapes:
        n = math.prod(s)
        out.append(flat[..., off:off + n].reshape(lead + tuple(s)))
        off += n
    return out


def _join_shards(piece, axis):
    moved = jnp.moveaxis(piece, 0, axis)
    shape = moved.shape
    return moved.reshape(shape[:axis] + (shape[axis] * shape[axis + 1],) + shape[axis + 2:])


def _split_shards(full, axis):
    shape = full.shape
    return jnp.moveaxis(full.reshape(shape[:axis] + (N_DEV, shape[axis] // N_DEV) + shape[axis + 1:]), axis, 0)


def _block_diag(blocks):
    g, r, c = blocks.shape
    k = g // S5_SUPER
    eye = jnp.eye(k, dtype=blocks.dtype)
    return (blocks.reshape(S5_SUPER, k, r, 1, c) * eye[None, :, None, :, None]).reshape(S5_SUPER, k * r, k * c)


def _diag_blocks(mat, r, c):
    k = mat.shape[1] // r
    eye = jnp.eye(k, dtype=mat.dtype)
    return jnp.sum(mat.reshape(S5_SUPER, k, r, k, c) * eye[None, :, None, :, None], axis=3).reshape(S5_SUPER * k, r, c)


def _pad_lanes(a):
    a = a.reshape(1, -1)
    return jnp.pad(a, ((0, 0), (0, LANES - a.shape[1])))


def _head_expand():
    h = jnp.arange(LANES)[:, None]
    ch = jnp.arange(SSM_HEADS * SSM_P)[None, :] // SSM_P
    return (h == ch).astype(F32)


def _rotary_tables(t):
    inv = 10000.0 ** (-jnp.arange(0, RET_DK, 2, dtype=F32) / RET_DK)
    ang = jnp.arange(t).astype(F32)[:, None] * inv[None, :]
    cos, sin = jnp.cos(ang), jnp.sin(ang)
    return jnp.concatenate([cos, cos], axis=1), jnp.concatenate([-sin, sin], axis=1)


def _rms_fwd(x, g, name):
    return _rows(_f_rms, x.shape[0], [(x, D_MODEL, 0)], [g], [D_MODEL], [], name, bf16_outs=(0,))[0]


def _rms_bwd(x, g, dh, dres, name):
    def fn(x_, dh_, dres_, g_):
        _, vjp = jax.vjp(lambda a, b: _rms(a, b), x_, g_)
        dx, dg = vjp(dh_)
        return dx + dres_, dx + dres_, dg
    return _rows(fn, x.shape[0], [(x, D_MODEL, 0), (dh, D_MODEL, 0), (dres, D_MODEL, 0)], [g], [D_MODEL, D_MODEL], [(1, D_MODEL)],
                 name, bf16_outs=(1,))


def _ffn_fwd(x, norm_g, w_up, dw_w, dw_b, w_down, tag):
    t = x.shape[0]
    nbk = D_FF // LANES
    h = _rms_fwd(x, norm_g, "ffn_norm_" + tag)
    up = _mm(h, w_up, "nn", "ffn_up_" + tag)
    mid = _cols(_f_ffnmid, nbk, [(up, 0), (up, nbk), (dw_w, 0), (dw_w, nbk), (dw_b, 0), (dw_b, nbk)], [(t,)], "ffn_mid_" + tag,
                bf16_outs=(0,))[0]
    out = _mm(mid, w_down, "nn", "ffn_down_" + tag, res=x)
    return out, (h, up, mid)


def _ffn_bwd(x, norm_g, w_up, dw_w, dw_b, w_down, saved, dout, dout_bf, tag):
    t = x.shape[0]
    nbk = D_FF // LANES
    h, up, mid = saved
    d_w_down = _mm(mid, dout_bf, "tn", "ffn_down_dw_" + tag, out_dtype=BF16)
    dmid = _mm(dout_bf, w_down, "nt", "ffn_down_dx_" + tag)
    dgin, duin, dwg, dwu, dbg, dbu = _cols(
        _grad_fn(_f_ffnmid, 6, 6), nbk,
        [(up, 0), (up, nbk), (dw_w, 0), (dw_w, nbk), (dw_b, 0), (dw_b, nbk), (dmid, 0)],
        [(t,), (t,), (3, 1), (3, 1), (1,), (1,)], "ffn_mid_bwd_" + tag, bf16_outs=(0, 1))
    dup = jnp.concatenate([dgin, duin], axis=1)
    d_w_up = _mm(h, dup, "tn", "ffn_up_dw_" + tag)
    dh = _mm(dup, w_up, "nt", "ffn_up_dx_" + tag)
    dx, dx_bf, dnorm = _rms_bwd(x, norm_g, dh, dout, "ffn_norm_bwd_" + tag)
    return dx, dx_bf, dict(norm=dnorm, w_up=d_w_up, dw_w=jnp.concatenate([dwg, dwu], axis=2)[:, 0],
                           dw_b=jnp.concatenate([dbg, dbu], axis=1), w_down=d_w_down)


def _local_step(x, tgt, w, late=None):
    t = x.shape[0]
    grads = {}
    expand = _head_expand()
    cosf, sins = _rotary_tables(t)
    mix_g = [w['mix_norm'][i:i + 1] for i in range(2)]
    ffn_g = [w['ffn_norm'][i:i + 1] for i in range(2)]
    ffn_dw_w = [w['ffn_dw_w'][i][:, None, :] for i in range(2)]
    ffn_dw_b = [w['ffn_dw_b'][i:i + 1] for i in range(2)]

    w_e = w['e_w_in']
    conv_w = w['e_conv_w'][0][:, None, :]
    conv_b = w['e_conv_b']
    dt_bias, a_log, d_skip = _pad_lanes(w['e_dt_bias']), _pad_lanes(w['e_a_log']), _pad_lanes(w['e_d'])
    h0 = _rms_fwd(x, mix_g[0], "mix_norm_0")
    proj = _mm(h0, w_e, "nn", "e_in")
    qr, kr = _rows(_f_retpre, t, [(proj, 512, 0), (proj, 512, 1), (cosf, LANES, 0), (sins, LANES, 0)], [], [512, 512], [], "ret_pre")
    if late:
        r, *landed = _ret_fwd(qr, kr, proj, t, late['ret_fwd'][0])
        late['ret_fwd'][1](w, landed)
    else:
        r = _ret_fwd(qr, kr, proj, t)
    y_ret = _rows(_f_retpost, t, [(r, 1024, 0), (proj, 1024, 2)], [], [1024], [], "ret_post", bf16_outs=(0,))[0]
    xbc_act = _cols(_f_ssdconv, 12, [(proj, 32), (conv_w, 0), (conv_b, 0)], [(t,)], "ssd_conv")[0]
    xdt, da = _rows(_f_ssdpre, t, [(xbc_act, 1024, 0), (proj, LANES, 44)], [dt_bias, a_log, expand], [1024, LANES], [], "ssd_pre")
    a_cum = _cumsum(da, False, "ssd_cumsum")
    a_cum_t = a_cum[:, :SSM_HEADS].T
    if late:
        yc, *landed = _ssd_fwd(xbc_act, xdt, a_cum, a_cum_t, t, late['ssd_fwd'][0])
        late['ssd_fwd'][1](w, landed)
    else:
        yc = _ssd_fwd(xbc_act, xdt, a_cum, a_cum_t, t)
    y_ssm = _rows(_f_ssdpost, t, [(yc, 1024, 0), (xbc_act, 1024, 0), (proj, 1024, 3)], [d_skip, w['e_ssm_norm'], expand],
                  [1024], [], "ssd_post", bf16_outs=(0,))[0]
    mix_e = jnp.concatenate([y_ret, y_ssm], axis=1)
    x1 = _mm(mix_e, w['e_w_out'], "nn", "e_out", res=x)
    x2, ffn0 = _ffn_fwd(x1, ffn_g[0], w['ffn_w_up'][0], ffn_dw_w[0], ffn_dw_b[0], w['ffn_w_down'][0], "0")

    lr, li = w['o_a_re'][0], w['o_a_im'][0]
    ls = w['o_log_step'].reshape(S5_GROUPS, 1)
    b_re3, b_im3 = jnp.transpose(w['o_b_re'][0], (2, 0, 1)), jnp.transpose(w['o_b_im'][0], (2, 0, 1))
    par_ins = [lr, li, ls, b_re3, b_im3]
    whole = lambda a: (a, a.shape, (lambda i, n=a.ndim: (0,) * n))
    par_shapes = [(S5_GROUPS, S5_STATE)] * 2 + [(S5_GROUP, S5_GROUPS, S5_STATE)] * 2
    ab_re, ab_im, bb_re, bb_im = _call(_f_s5par, (1,), [whole(a) for a in par_ins],
                                       [(s, F32, s, (lambda i, n=len(s): (0,) * n), False) for s in par_shapes], "s5_params")
    w_b = jnp.concatenate([_block_diag(jnp.transpose(bb_re, (1, 0, 2))), _block_diag(jnp.transpose(bb_im, (1, 0, 2)))], axis=2)
    w_c = jnp.concatenate([_block_diag(jnp.transpose(w['o_c_re'][0], (0, 2, 1))),
                           -_block_diag(jnp.transpose(w['o_c_im'][0], (0, 2, 1)))], axis=1)
    a_re, a_im = ab_re.reshape(1, S5_COLS), ab_im.reshape(1, S5_COLS)
    dw_w = w['o_dw_w'][0][:, None, :]
    glu_w = w['o_glu_w'].astype(F32)

    h1 = _rms_fwd(x2, mix_g[1], "mix_norm_1")
    proj_o = _mm(h1, w['o_w_in'], "nn", "o_in")
    c1 = _conf_conv(proj_o, dw_w, w['o_dw_b'])
    c2 = _rows(_f_confb, t, [(c1, 512, 0)], [w['o_ln_g'], w['o_ln_b']], [512], [], "conf_norm", bf16_outs=(0,))[0]
    u_off = 2 * CONF_DIM // S5_NARROW
    bu_re, bu_im = _s5_expand(proj_o, u_off, w_b, False, "s5_bu")
    xs_re, xs_im, xs_re_bf, xs_im_bf = _s5_scan(bu_re, bu_im, a_re, a_im, False)
    y_s5 = _s5_contract(xs_re_bf, xs_im_bf, w_c, True, "s5_cx")
    s_out = _rows(_f_s5post, t, [(y_s5, 512, 0), (proj_o, 512, 2)], [w['o_d'], glu_w], [512], [], "s5_post", bf16_outs=(0,))[0]
    mix_o = jnp.concatenate([c2, s_out], axis=1)
    x3 = _mm(mix_o, w['o_w_out'], "nn", "o_out", res=x2)
    x4, ffn1 = _ffn_fwd(x3, ffn_g[1], w['ffn_w_up'][1], ffn_dw_w[1], ffn_dw_b[1], w['ffn_w_down'][1], "1")

    dx4, dx4_bf, loss_blk, d_final = _rows(_loss_step, t, [(x4, D_MODEL, 0), (tgt, D_MODEL, 0)], [w['final_norm'].reshape(1, D_MODEL)],
                                           [D_MODEL, D_MODEL], [(8, LANES), (1, D_MODEL)], "loss_head", bf16_outs=(1,))
    loss = loss_blk[0, 0]
    grads['final_norm'] = d_final.reshape(D_MODEL)

    dx3, dx3_bf, g1 = _ffn_bwd(x3, ffn_g[1], w['ffn_w_up'][1], ffn_dw_w[1], ffn_dw_b[1], w['ffn_w_down'][1], ffn1, dx4, dx4_bf, "1")
    grads['o_w_out'] = _mm(mix_o, dx3_bf, "tn", "o_out_dw", out_dtype=BF16)
    dmix_o = _mm(dx3_bf, w['o_w_out'], "nt", "o_out_dx")
    dc1, d_ln_g, d_ln_b = _conf_norm_bwd(c1, w['o_ln_g'], w['o_ln_b'], dmix_o, t)
    dca, dcg, d_dw_w, d_dw_b = _conf_conv(proj_o, dw_w, w['o_dw_b'], dc1)
    dyc, du_skip, d_od, d_glu = _s5_post_bwd(y_s5, proj_o, w['o_d'], glu_w, dmix_o, t)
    grads['o_glu_w'] = d_glu
    d_w_c = _s5_wgrad(dyc, 0, xs_re_bf, xs_im_bf, True, "s5_cx_dw")
    dxs_re, dxs_im = _s5_expand(dyc, 0, w_c, True, "s5_cx_dx")
    g_re, g_im, d_are, d_aim = _s5_scan(dxs_re, dxs_im, a_re, a_im, True, (xs_re, xs_im))
    d_w_b = _s5_wgrad(proj_o, u_off, g_re, g_im, False, "s5_bu_dw")
    du = _s5_contract(g_re, g_im, w_b, False, "s5_bu_dx", res=du_skip, out_dtype=BF16)
    d_bb_re = jnp.transpose(_diag_blocks(d_w_b[:, :, :S5_WIDE], S5_GROUP, S5_STATE), (1, 0, 2))
    d_bb_im = jnp.transpose(_diag_blocks(d_w_b[:, :, S5_WIDE:], S5_GROUP, S5_STATE), (1, 0, 2))
    par_cts = [d_are.reshape(S5_GROUPS, S5_STATE), d_aim.reshape(S5_GROUPS, S5_STATE), d_bb_re, d_bb_im]
    in_shapes = [a.shape for a in par_ins]
    d_lr, d_li, d_ls, d_br3, d_bi3 = _call(_grad_fn(_f_s5par, 5, 5), (1,), [whole(a) for a in par_ins + par_cts],
                                           [(s, F32, s, (lambda i, n=len(s): (0,) * n), False) for s in in_shapes], "s5_params_bwd")
    grads['o_a_re'], grads['o_a_im'], grads['o_log_step'] = d_lr[None], d_li[None], d_ls.reshape(1, S5_GROUPS)
    grads['o_b_re'], grads['o_b_im'] = jnp.transpose(d_br3, (1, 2, 0))[None], jnp.transpose(d_bi3, (1, 2, 0))[None]
    grads['o_c_re'] = jnp.transpose(_diag_blocks(d_w_c[:, :S5_WIDE], S5_STATE, S5_GROUP), (0, 2, 1))[None]
    grads['o_c_im'] = -jnp.transpose(_diag_blocks(d_w_c[:, S5_WIDE:], S5_STATE, S5_GROUP), (0, 2, 1))[None]
    grads['o_d'], grads['o_ln_g'], grads['o_ln_b'] = d_od, d_ln_g, d_ln_b
    grads['o_dw_w'], grads['o_dw_b'] = d_dw_w[:, 0][None], d_dw_b
    dproj_o = jnp.concatenate([dca, dcg, du], axis=1)
    grads['o_w_in'] = _mm(h1, dproj_o, "tn", "o_in_dw")
    dh1 = _mm(dproj_o, w['o_w_in'], "nt", "o_in_dx")
    dx2, dx2_bf, d_mix1 = _rms_bwd(x2, mix_g[1], dh1, dx3, "mix_norm_bwd_1")

    dx1, dx1_bf, g0 = _ffn_bwd(x1, ffn_g[0], w['ffn_w_up'][0], ffn_dw_w[0], ffn_dw_b[0], w['ffn_w_down'][0], ffn0, dx2, dx2_bf, "0")
    grads['e_w_out'] = _mm(mix_e, dx1_bf, "tn", "e_out_dw", out_dtype=BF16)
    dmix_e = _mm(dx1_bf, w['e_w_out'], "nt", "e_out_dx")
    dr, dg = _rows(_grad_fn(_f_retpost, 2, 2), t, [(r, 1024, 0), (proj, 1024, 2), (dmix_e, 1024, 0)], [], [1024, 1024], [], "ret_post_bwd",
                   bf16_outs=(0, 1))
    dqr, dkr, dv = _ret_bwd(qr, kr, proj, dr, t)
    dq, dk = _ret_pre_bwd(proj, cosf, sins, dqr, dkr, t)
    dyc0, dxs1, dz, d_dskip, d_ssm_norm = _ssd_post_bwd(yc, xbc_act, proj, d_skip, w['e_ssm_norm'], expand, dmix_e, t)
    dcm, da_q, dbm, dxdt, da_k, da_k_t = _ssd_bwd(xbc_act, xdt, a_cum, a_cum_t, dyc0, expand, t)
    d_a_cum = da_q - da_k - jnp.pad(da_k_t.T, ((0, 0), (0, LANES - SSM_HEADS)))
    dda = _cumsum(d_a_cum, True, "ssd_cumsum_bwd")
    dxs, ddtr, d_dt_bias, d_a_log = _ssd_pre_bwd(xbc_act, proj, dt_bias, a_log, expand, dxdt, dda, dxs1, t)
    dxbc_act = jnp.concatenate([dxs, dbm, dcm], axis=1)
    dxbc, d_conv_w, d_conv_b = _cols(_grad_fn(_f_ssdconv, 3, 3), 12, [(proj, 32), (conv_w, 0), (conv_b, 0), (dxbc_act, 0)],
                                     [(t,), (4, 1), (1,)], "ssd_conv_bwd", bf16_outs=(0,))
    dproj = jnp.concatenate([dq, dk, dv, dg, dz, dxbc, ddtr], axis=1)
    grads['e_w_in'] = _mm(h0, dproj, "tn", "e_in_dw")
    dh0 = _mm(dproj, w_e, "nt", "e_in_dx")
    dx0, _, d_mix0 = _rms_bwd(x, mix_g[0], dh0, dx1, "mix_norm_bwd_0")

    grads['mix_norm'] = jnp.concatenate([d_mix0, d_mix1], axis=0)
    grads['e_conv_w'], grads['e_conv_b'] = d_conv_w[:, 0][None], d_conv_b
    grads['e_dt_bias'], grads['e_a_log'], grads['e_d'] = d_dt_bias[:, :SSM_HEADS], d_a_log[:, :SSM_HEADS], d_dskip[:, :SSM_HEADS]
    grads['e_ssm_norm'] = d_ssm_norm
    grads['ffn_norm'] = jnp.concatenate([g0['norm'], g1['norm']], axis=0)
    grads['ffn_w_up'], grads['ffn_w_down'] = [g0['w_up'], g1['w_up']], [g0['w_down'], g1['w_down']]
    grads['ffn_dw_w'] = jnp.stack([g0['dw_w'], g1['dw_w']], axis=0)
    grads['ffn_dw_b'] = jnp.concatenate([g0['dw_b'], g1['dw_b']], axis=0)
    return loss, dx0, grads


def _conf_norm_bwd(c1, ln_g, ln_b, dmix_o, t):
    def fn(c1_, dy_, g_, b_):
        _, vjp = jax.vjp(lambda a, b, c: _f_confb(a, b, c)[0], c1_, g_, b_)
        return vjp(dy_)
    return _rows(fn, t, [(c1, 512, 0), (dmix_o, 512, 0)], [ln_g, ln_b], [512], [(1, 512), (1, 512)], "conf_norm_bwd")


def _s5_post_bwd(y_s5, proj_o, d_skip, glu_w, dmix_o, t):
    def fn(yc_, u_, dy_, d_, gw_):
        _, vjp = jax.vjp(lambda a, b, c, e: _f_s5post(a, b, c, e)[0], yc_, u_, d_, gw_)
        return vjp(dy_)
    return _rows(fn, t, [(y_s5, 512, 0), (proj_o, 512, 2), (dmix_o, 512, 1)], [d_skip, glu_w], [512, 512], [(1, 512), (512, 512)],
                 "s5_post_bwd", bf16_outs=(0,))


def _ret_pre_bwd(proj, cosf, sins, dqr, dkr, t):
    def fn(q_, k_, cos_, sin_, dq_, dk_):
        _, vjp = jax.vjp(lambda a, b: _f_retpre(a, b, cos_, sin_), q_, k_)
        return vjp((dq_, dk_))
    return _rows(fn, t, [(proj, 512, 0), (proj, 512, 1), (cosf, LANES, 0), (sins, LANES, 0), (dqr, 512, 0), (dkr, 512, 0)], [],
                 [512, 512], [], "ret_pre_bwd", bf16_outs=(0, 1))


def _ssd_post_bwd(yc, xbc_act, proj, d_skip, norm_w, expand, dmix_e, t):
    def fn(yc_, xs_, z_, dy_, d_, nw_, e_):
        _, vjp = jax.vjp(lambda a, b, c, dd, n: _f_ssdpost(a, b, c, dd, n, e_)[0], yc_, xs_, z_, d_, nw_)
        return vjp(dy_)
    return _rows(fn, t, [(yc, 1024, 0), (xbc_act, 1024, 0), (proj, 1024, 3), (dmix_e, 1024, 1)], [d_skip, norm_w, expand],
                 [1024, 1024, 1024], [(1, LANES), (1, 1024)], "ssd_post_bwd", bf16_outs=(0, 2))


def _ssd_pre_bwd(xbc_act, proj, dt_bias, a_log, expand, dxdt, dda, dxs1, t):
    def fn(xs_, dtr_, dx_, dda_, dxs1_, bias_, alog_, e_):
        _, vjp = jax.vjp(lambda a, b, c, dd: _f_ssdpre(a, b, c, dd, e_), xs_, dtr_, bias_, alog_)
        dxs, ddtr, dbias, dalog = vjp((dx_, dda_))
        return dxs + dxs1_, ddtr, dbias, dalog
    return _rows(fn, t, [(xbc_act, 1024, 0), (proj, LANES, 44), (dxdt, 1024, 0), (dda, LANES, 0), (dxs1, 1024, 0)],
                 [dt_bias, a_log, expand], [1024, LANES], [(1, LANES), (1, LANES)], "ssd_pre_bwd", bf16_outs=(1,))


def kernel(x, mix_norm, e_w_in, e_conv_w, e_conv_b, e_dt_bias, e_a_log, e_d, e_ssm_norm, e_w_out, o_w_in, o_dw_w, o_dw_b, o_ln_g, o_ln_b, o_a_re, o_a_im, o_b_re, o_b_im, o_c_re, o_c_im, o_d, o_log_step, o_glu_w, o_w_out, ffn_norm, ffn_w_up, ffn_dw_w, ffn_dw_b, ffn_w_down, final_norm, loss_target, m_mix_norm, m_e_w_in, m_e_conv_w, m_e_conv_b, m_e_dt_bias, m_e_a_log, m_e_d, m_e_ssm_norm, m_e_w_out, m_o_w_in, m_o_dw_w, m_o_dw_b, m_o_ln_g, m_o_ln_b, m_o_a_re, m_o_a_im, m_o_b_re, m_o_b_im, m_o_c_re, m_o_c_im, m_o_d, m_o_log_step, m_o_glu_w, m_o_w_out, m_ffn_norm, m_ffn_w_up, m_ffn_dw_w, m_ffn_dw_b, m_ffn_w_down, m_final_norm, v_mix_norm, v_e_w_in, v_e_conv_w, v_e_conv_b, v_e_dt_bias, v_e_a_log, v_e_d, v_e_ssm_norm, v_e_w_out, v_o_w_in, v_o_dw_w, v_o_dw_b, v_o_ln_g, v_o_ln_b, v_o_a_re, v_o_a_im, v_o_b_re, v_o_b_im, v_o_c_re, v_o_c_im, v_o_d, v_o_log_step, v_o_glu_w, v_o_w_out, v_ffn_norm, v_ffn_w_up, v_ffn_dw_w, v_ffn_dw_b, v_ffn_w_down, v_final_norm):
    p = dict(locals())

    kinds = ("grad_", "delta_", "new_m_", "new_v_")

    def block(name, layer):
        return p[name][0 if layer is None else layer]

    def shards(blocks):
        return [block(n, layer).astype(BF16) for n, layer, _ in blocks]

    def place(w, gathered, blocks):
        for (n, layer, by_cols), g in zip(blocks, gathered):
            if by_cols:
                full = _join_cols(g, E_IN_PAD if n == 'e_w_in' else N_DEV * g.shape[2], f"join_{n}_{layer}")
            else:
                full = g.reshape(N_DEV * g.shape[1], g.shape[2])
            if layer is None:
                w[n] = full
            else:
                w[n][layer] = full

    first = [b for b in MATMUL_BLOCKS if b[0].startswith('e_')]
    with_ret = [b for b in MATMUL_BLOCKS if b[0].startswith('ffn_') and b[1] == 0]
    with_ssd = [b for b in MATMUL_BLOCKS if b not in first and b not in with_ret]
    gathered = _gather(shards(first) + [_pack([p[n] for n in SMALL_SHARDED], F32, 16)], "gather_first_weights")
    w = {n: p[n] for n in REPLICATED}
    w['ffn_w_up'], w['ffn_w_down'] = [None, None], [None, None]
    place(w, gathered, first)
    for n, piece in zip(SMALL_SHARDED, _unpack(gathered[-1], [p[n].shape for n in SMALL_SHARDED], lead=(N_DEV,))):
        w[n] = _join_shards(piece, SHARDED[n])
    late = {'ret_fwd': (shards(with_ret), functools.partial(place, blocks=with_ret)),
            'ssd_fwd': (shards(with_ssd), functools.partial(place, blocks=with_ssd))}

    loss, dx, grads = _local_step(x[0], loss_target[0], w, late)
    loss = lax.psum(loss, MESH_AXES)

    sends = []
    for n, layer, by_cols in MATMUL_BLOCKS:
        g = grads[n] if layer is None else grads[n][layer]
        if by_cols:
            sends.append(_split_cols(g, block(n, layer).shape[1], BF16, f"split_{n}_{layer}"))
        else:
            sends.append(g.astype(BF16).reshape(N_DEV, -1, g.shape[1]))
    sends.append(_pack([_split_shards(grads[n].reshape(p[n].shape[:SHARDED[n]] + (-1,) + p[n].shape[SHARDED[n] + 1:]), SHARDED[n])
                        for n in SMALL_SHARDED], F32, 128, lead=(N_DEV,)))
    core = lax.axis_index("c")
    by_core = [s.reshape((N_CHIPS, 2) + s.shape[1:]) for s in sends]
    keep = [lax.dynamic_index_in_dim(s, core, axis=1, keepdims=False) for s in by_core]
    give = [lax.dynamic_index_in_dim(s, 1 - core, axis=1, keepdims=False) for s in by_core]
    got = _swap_sibling(give, "swap_sibling_grads")
    chip_sums = [_add(a, b, f"chip_sum_{i}") for i, (a, b) in enumerate(zip(keep, got))]
    parts = list(_exchange_chips(chip_sums, "exchange_chip_grads"))
    parts.append(_gather([_pack([grads[n].reshape(p[n].shape) for n in REPLICATED], F32, 128)], "gather_replicated_grads")[0])

    out, by_layer = {}, {}
    for (n, layer, _), part in zip(MATMUL_BLOCKS, parts):
        by_layer.setdefault(n, {})[layer] = _adamw(part, *[block(pre + n, layer) for pre in ("", "m_", "v_")], f"adamw_{n}_{layer}")
    for n, res in by_layer.items():
        for i, kind in enumerate(kinds):
            out[kind + n] = res[None][i][None] if None in res else jnp.stack([res[0][i], res[1][i]], axis=0)
    for names, part, tag in ((SMALL_SHARDED, parts[-2], "small"), (REPLICATED, parts[-1], "replicated")):
        packed = [_pack([p[pre + n] for n in names], F32, 128) for pre in ("", "m_", "v_")]
        for kind, buf in zip(kinds, _adamw(part, *packed, "adamw_" + tag)):
            for n, a in zip(names, _unpack(buf, [p[n].shape for n in names])):
                out[kind + n] = a
    return (loss, dx[None], *[out[kind + n] for kind in kinds for n in WEIGHTS])
```

```python
import functools
import math

import jax
import jax.numpy as jnp
from jax import lax
from jax.experimental import pallas as pl
from jax.experimental.pallas import tpu as pltpu

F32, BF16 = jnp.float32, jnp.bfloat16
HIGHEST = lax.Precision.HIGHEST
N_DEV = 8
MESH_AXES = ("x", "y", "c")
VMEM_LIMIT = 48 * 1024 * 1024
LANES = 128
PACK_COLS = 1024

D_MODEL = 1024
EPS = 1e-6
RET_HEADS, RET_DK, RET_DV = 4, 128, 256
SSM_HEADS, SSM_P, SSM_N, SSM_GROUPS = 16, 64, 128, 2
SSM_HG = SSM_HEADS // SSM_GROUPS
S5_GROUPS, S5_GROUP, S5_STATE = 32, 16, 64
S5_COLS = S5_GROUPS * S5_STATE
D_FF = 2816
E_IN, E_IN_PAD = 5648, 5760
ADAM_LR, ADAM_B1, ADAM_B2, ADAM_EPS, ADAM_WD, ADAM_STEP = 0.001, 0.9, 0.999, 1e-08, 0.01, 10

WEIGHTS = ['mix_norm', 'e_w_in', 'e_conv_w', 'e_conv_b', 'e_dt_bias', 'e_a_log', 'e_d', 'e_ssm_norm', 'e_w_out', 'o_w_in', 'o_dw_w', 'o_dw_b', 'o_ln_g', 'o_ln_b', 'o_a_re', 'o_a_im', 'o_b_re', 'o_b_im', 'o_c_re', 'o_c_im', 'o_d', 'o_log_step', 'o_glu_w', 'o_w_out', 'ffn_norm', 'ffn_w_up', 'ffn_dw_w', 'ffn_dw_b', 'ffn_w_down', 'final_norm']
SHARDED = {'e_w_in': 2, 'e_conv_w': 2, 'e_w_out': 1, 'o_w_in': 2, 'o_dw_w': 2, 'o_dw_b': 1, 'o_ln_g': 1, 'o_ln_b': 1,
           'o_d': 1, 'o_glu_w': 1, 'o_w_out': 1, 'ffn_w_up': 2, 'ffn_dw_w': 2, 'ffn_w_down': 1}
MATMUL_WEIGHTS = ['e_w_in', 'e_w_out', 'o_w_in', 'o_glu_w', 'o_w_out', 'ffn_w_up', 'ffn_w_down']
MATMUL_BLOCKS = [('e_w_in', None, True), ('e_w_out', None, False), ('o_w_in', None, True), ('o_glu_w', None, False),
                 ('o_w_out', None, False), ('ffn_w_up', 0, True), ('ffn_w_up', 1, True), ('ffn_w_down', 0, False), ('ffn_w_down', 1, False)]
SMALL_SHARDED = [n for n in WEIGHTS if n in SHARDED and n not in MATMUL_WEIGHTS]
REPLICATED = [n for n in WEIGHTS if n not in SHARDED]


def _call(fn, grid, ins, outs, name):
    n_in = len(ins)

    def body(*refs):
        vals = fn(*[r[...] for r in refs[:n_in]])
        first = pl.program_id(0) == 0
        for r, v, o in zip(refs[n_in:], vals, outs):
            if o[4]:
                @pl.when(first)
                def _():
                    r[...] = jnp.zeros_like(r)
                r[...] += v.astype(r.dtype)
            else:
                r[...] = v.astype(r.dtype)

    return pl.pallas_call(
        body, grid=grid,
        in_specs=[pl.BlockSpec(b, m) for _, b, m in ins],
        out_specs=[pl.BlockSpec(o[2], o[3]) for o in outs],
        out_shape=[jax.ShapeDtypeStruct(o[0], o[1]) for o in outs],
        compiler_params=pltpu.CompilerParams(dimension_semantics=("arbitrary",) * len(grid), vmem_limit_bytes=VMEM_LIMIT),
        name=name)(*[a for a, _, _ in ins])


def _rows(fn, n_rows, row_ins, full_ins, row_outs, acc_outs, name, tm=256, bf16_outs=()):
    tm = min(tm, n_rows)
    ins = [(a, (tm, w), (lambda i, c=c: (i, c))) for a, w, c in row_ins]
    ins += [(a, a.shape, (lambda i, n=a.ndim: (0,) * n)) for a in full_ins]
    outs = [((n_rows, w), BF16 if k in bf16_outs else F32, (tm, w), (lambda i: (i, 0)), False) for k, w in enumerate(row_outs)]
    outs += [(tuple(s), F32, tuple(s), (lambda i, n=len(s): (0,) * n), True) for s in acc_outs]
    return _call(fn, (n_rows // tm,), ins, outs, name)


def _cols(fn, n_blocks, col_ins, out_leads, name, cb=LANES, bf16_outs=()):
    ins = [(a, a.shape[:-1] + (cb,), (lambda j, n=a.ndim, o=o: (0,) * (n - 1) + (j + o,))) for a, o in col_ins]
    outs = [(tuple(s) + (n_blocks * cb,), BF16 if k in bf16_outs else F32, tuple(s) + (cb,), (lambda j, n=len(s): (0,) * n + (j,)), False)
            for k, s in enumerate(out_leads)]
    return _call(fn, (n_blocks,), ins, outs, name)


def _grad_fn(f, n_diff, n_in):
    def g(*a):
        diff, consts, cts = a[:n_diff], a[n_diff:n_in], a[n_in:]
        _, vjp = jax.vjp(lambda *d: f(*d, *consts), *diff)
        return vjp(tuple(cts))
    return g


def _silu(x):
    return x * jax.nn.sigmoid(x)


def _rms(x, g):
    return x * lax.rsqrt(jnp.mean(x * x, axis=-1, keepdims=True) + EPS) * g


@jax.custom_vjp
def _softplus(x):
    return jnp.maximum(x, 0.0) + jnp.log(1.0 + jnp.exp(-jnp.abs(x)))


_softplus.defvjp(lambda x: (_softplus(x), x), lambda x, g: (g * jax.nn.sigmoid(x),))


@jax.custom_vjp
def _swap_halves(x):
    return pltpu.roll(x, 64, 1)


_swap_halves.defvjp(lambda x: (_swap_halves(x), None), lambda _, g: (_swap_halves(g),))


def _shift_rows(x, k, up):
    if k == 0:
        return x
    n = x.shape[0]
    t = lax.broadcasted_iota(jnp.int32, x.shape, 0)
    if up:
        return jnp.where(t < n - k, pltpu.roll(x, n - k, 0), 0.0)
    return jnp.where(t >= k, pltpu.roll(x, k, 0), 0.0)


@jax.custom_vjp
def _dwconv(x, w, b):
    k_taps = w.shape[0]
    y = b + w[k_taps - 1] * x
    for k in range(k_taps - 1):
        y = y + w[k] * _shift_rows(x, k_taps - 1 - k, False)
    return y


def _dwconv_fwd(x, w, b):
    return _dwconv(x, w, b), (x, w)


def _dwconv_bwd(saved, dy):
    x, w = saved
    k_taps = w.shape[0]
    dx = w[k_taps - 1] * dy
    dws = []
    for k in range(k_taps - 1):
        s = k_taps - 1 - k
        dx = dx + w[k] * _shift_rows(dy, s, True)
        dws.append(jnp.sum(dy * _shift_rows(x, s, False), axis=0, keepdims=True)[None])
    dws.append(jnp.sum(dy * x, axis=0, keepdims=True)[None])
    return dx, jnp.concatenate(dws, axis=0), jnp.sum(dy, axis=0, keepdims=True)


_dwconv.defvjp(_dwconv_fwd, _dwconv_bwd)


def _f_rms(x, g):
    return (_rms(x, g),)


def _rot(x, cosf, sins):
    outs = []
    for h in range(RET_HEADS):
        xh = x[:, h * RET_DK:(h + 1) * RET_DK]
        outs.append(xh * cosf + _swap_halves(xh) * sins)
    return jnp.concatenate(outs, axis=1)


def _f_retpre(q, k, cosf, sins):
    return _rot(q, cosf, sins), _rot(k, cosf, sins) * (RET_DK ** -0.5)


def _f_retpost(r, g):
    outs = []
    for h in range(RET_HEADS):
        rh = r[:, h * RET_DV:(h + 1) * RET_DV]
        rc = rh - jnp.mean(rh, axis=-1, keepdims=True)
        outs.append(_silu(g[:, h * RET_DV:(h + 1) * RET_DV]) * (rc * lax.rsqrt(jnp.mean(rc * rc, axis=-1, keepdims=True) + EPS)))
    return (jnp.concatenate(outs, axis=1),)


def _f_ssdconv(xbc, w, b):
    return (_silu(_dwconv(xbc, w, b)),)


def _f_ssdpre(xs, dtr, bias, alog, expand):
    dt = _softplus(dtr + bias)
    return xs * jnp.dot(dt, expand, precision=HIGHEST, preferred_element_type=F32), dt * (-jnp.exp(alog))


def _f_ssdpost(yc, xs, z, dskip, norm_w, expand):
    d_wide = jnp.dot(jnp.broadcast_to(dskip, (yc.shape[0], LANES)), expand, precision=HIGHEST, preferred_element_type=F32)
    y = (yc + d_wide * xs) * _silu(z)
    half = y.shape[1] // SSM_GROUPS
    outs = []
    for g in range(SSM_GROUPS):
        yg = y[:, g * half:(g + 1) * half]
        outs.append(yg * lax.rsqrt(jnp.mean(yg * yg, axis=-1, keepdims=True) + EPS))
    return (jnp.concatenate(outs, axis=1) * norm_w,)


def _f_ffnmid(gin, uin, wg, wu, bg, bu):
    return (_silu(_dwconv(gin, wg, bg)) * _dwconv(uin, wu, bu),)


def _f_confb(c1, g, b):
    mu = jnp.mean(c1, axis=-1, keepdims=True)
    xc = c1 - mu
    return (_silu(xc * lax.rsqrt(jnp.mean(xc * xc, axis=-1, keepdims=True) + EPS) * g + b),)


def _f_s5post(yc, u, dskip, glu_w):
    s = jax.nn.gelu(yc + dskip * u)
    z = jnp.dot(s.astype(BF16), glu_w.astype(BF16), preferred_element_type=F32)
    return (s * jax.nn.sigmoid(z),)


def _f_s5par(lr, li, ls, br, bi):
    step = jnp.exp(ls)
    mag = jnp.exp(lr * step)
    ab_re = mag * jnp.cos(li * step)
    ab_im = mag * jnp.sin(li * step)
    den = lr * lr + li * li
    f_re = ((ab_re - 1.0) * lr + ab_im * li) / den
    f_im = (ab_im * lr - (ab_re - 1.0) * li) / den
    return ab_re, ab_im, f_re[None] * br - f_im[None] * bi, f_re[None] * bi + f_im[None] * br


def _loss_step(x, tgt, g):
    def f(x_, g_):
        e = _rms(x_, g_) - tgt
        return 0.5 * jnp.sum(jnp.mean(e * e, axis=-1, keepdims=True), axis=0, keepdims=True)
    loss, vjp = jax.vjp(f, x, g)
    dx, dg = vjp(jnp.ones((1, 1), F32))
    return dx, dx, jnp.broadcast_to(loss, (8, LANES)), dg


def _tile(n, pref):
    if n <= pref:
        return n
    t = (pref // LANES) * LANES
    while n % t:
        t -= LANES
    return t


_DIMS = {"nn": (((1,), (0,)), ((), ())), "nt": (((1,), (1,)), ((), ())), "tn": (((0,), (0,)), ((), ()))}


def _mm(a, b, mode, name, res=None, out_dtype=F32, tm=1024, tn=1408, tk=1408):
    if mode == "nn":
        (m, k), n = a.shape, b.shape[1]
    elif mode == "nt":
        (m, k), n = a.shape, b.shape[0]
    else:
        (k, m), n = a.shape, b.shape[1]
    tm, tn, tk = _tile(m, tm), _tile(n, tn), _tile(k, tk)
    nk = k // tk
    a_spec = pl.BlockSpec((tk, tm), lambda i, j, kk: (kk, i)) if mode == "tn" else pl.BlockSpec((tm, tk), lambda i, j, kk: (i, kk))
    b_spec = pl.BlockSpec((tn, tk), lambda i, j, kk: (j, kk)) if mode == "nt" else pl.BlockSpec((tk, tn), lambda i, j, kk: (kk, j))
    o_spec = pl.BlockSpec((tm, tn), lambda i, j, kk: (i, j))
    has_res = res is not None

    def body(*refs):
        a_ref, b_ref = refs[0], refs[1]
        o_ref, acc = refs[-2], refs[-1]
        kk = pl.program_id(2)

        @pl.when(kk == 0)
        def _():
            acc[...] = jnp.zeros_like(acc)

        acc[...] += lax.dot_general(a_ref[...].astype(BF16), b_ref[...].astype(BF16), _DIMS[mode], preferred_element_type=F32)

        @pl.when(kk == nk - 1)
        def _():
            o_ref[...] = (acc[...] + refs[2][...] if has_res else acc[...]).astype(out_dtype)

    return pl.pallas_call(
        body, grid=(m // tm, n // tn, nk),
        in_specs=[a_spec, b_spec] + ([o_spec] if has_res else []),
        out_specs=o_spec, out_shape=jax.ShapeDtypeStruct((m, n), out_dtype),
        scratch_shapes=[pltpu.VMEM((tm, tn), F32)],
        compiler_params=pltpu.CompilerParams(dimension_semantics=("parallel", "parallel", "arbitrary"), vmem_limit_bytes=VMEM_LIMIT),
        name=name)(*([a, b] + ([res] if has_res else [])))


def _seq_block(t):
    return min(512, t)


def _causal_diff(i, j, blk):
    r = lax.broadcasted_iota(jnp.int32, (blk, blk), 0)
    c = lax.broadcasted_iota(jnp.int32, (blk, blk), 1)
    return (i - j) * blk + r - c


def _ret_decay(lg, i, j, blk):
    diff = _causal_diff(i, j, blk)
    return jnp.where(diff >= 0, jnp.exp(lg * jnp.maximum(diff, 0).astype(F32)), 0.0)


def _pair_call(body, lead_grid, nb, key_major, in_specs, out_specs, out_shape, scratch, name, args, gather=(), exchange=()):
    pairs = [(i, j) for j in range(nb) for i in range(j, nb)] if key_major else [(i, j) for i in range(nb) for j in range(i + 1)]
    tables = [jnp.array([p[k] for p in pairs], jnp.int32) for k in (0, 1)]
    lead = len(lead_grid)
    n_steps = math.prod(lead_grid) * len(pairs)
    behind = list(gather) or list(exchange)
    n_bg = len(behind)
    phases, per_array = (_gather_phases, GATHER_SEMS) if gather else (_exchange_phases, EXCHANGE_SEMS)

    def on_pairs(spec):
        if spec.block_shape is None:
            return spec

        def index_map(*a):
            i, j = a[lead + 1][a[lead]], a[lead + 2][a[lead]]
            return spec.index_map(*a[:lead], *((j, i) if key_major else (i, j)))

        return pl.BlockSpec(spec.block_shape, index_map)

    many = isinstance(out_specs, (list, tuple))
    out_specs, out_shape = (list(out_specs), list(out_shape)) if many else ([out_specs], [out_shape])
    n_in, n_out = len(in_specs), len(out_specs)

    def wrapped(i_ref, j_ref, *refs):
        p = pl.program_id(lead)
        own = refs[:n_in] + refs[n_in + n_bg:n_in + n_bg + n_out] + refs[n_in + 2 * n_bg + n_out:len(refs) - (3 if n_bg else 0)]
        if n_bg:
            step = p + (pl.program_id(0) * len(pairs) if lead else 0)
            start, forward, finish = phases(refs[n_in:n_in + n_bg], refs[n_in + n_bg + n_out:n_in + 2 * n_bg + n_out], *refs[-3:])
            pl.when(step == 0)(start)
        body(i_ref[p], j_ref[p], *own)
        if n_bg:
            pl.when(step == n_steps * 3 // 4)(forward)
            pl.when(step == n_steps - 1)(finish)

    sems = [pltpu.SemaphoreType.DMA((per_array * n_bg,))] * 2 + [pltpu.SemaphoreType.DMA((n_bg,))] if n_bg else []
    landed = [jax.ShapeDtypeStruct(((N_DEV,) + g.shape) if gather else g.shape, g.dtype) for g in behind]
    grid_spec = pltpu.PrefetchScalarGridSpec(
        num_scalar_prefetch=2, grid=tuple(lead_grid) + (len(pairs),), in_specs=[on_pairs(s) for s in in_specs] + [_HBM] * n_bg,
        out_specs=[on_pairs(s) for s in out_specs] + [_HBM] * n_bg, scratch_shapes=list(scratch) + sems)
    res = pl.pallas_call(
        wrapped, grid_spec=grid_spec, out_shape=out_shape + landed,
        compiler_params=pltpu.CompilerParams(dimension_semantics=("arbitrary",) * (lead + 1), vmem_limit_bytes=VMEM_LIMIT),
        name=name)(*tables, *args, *behind)
    return res if many or n_bg else res[0]


def _ret_row_decays(lg, i, j, blk):
    row = lax.broadcasted_iota(jnp.int32, (blk, RET_DK), 0)
    return jnp.exp(lg * row.astype(F32)), jnp.exp(lg * ((i - j) * blk - row).astype(F32))


def _ret_scaled(lg, i, j, blk, q_ref, k_ref):
    a, b = _ret_row_decays(lg, i, j, blk)
    return (q_ref[...] * a).astype(BF16), (k_ref[...] * b).astype(BF16)


def _ret_log_gamma():
    return jnp.log1p(-(2.0 ** (-5.0 - jnp.arange(RET_HEADS, dtype=F32))))


def _ret_fwd(qr, kr, proj, t, gather=()):
    blk = _seq_block(t)
    nb = t // blk
    v_off = (2 * RET_HEADS * RET_DK) // RET_DV

    def body(i, j, lg_ref, q_ref, k_ref, v_ref, o_ref, acc):
        h = pl.program_id(0)

        @pl.when(j == 0)
        def _():
            acc[...] = jnp.zeros_like(acc)

        @pl.when(j < i)
        def _():
            qa, kb = _ret_scaled(lg_ref[h], i, j, blk, q_ref, k_ref)
            p = lax.dot_general(qa, kb, _DIMS["nt"], preferred_element_type=F32).astype(BF16)
            acc[...] += jnp.dot(p, v_ref[...].astype(BF16), preferred_element_type=F32)

        @pl.when(j == i)
        def _():
            s = lax.dot_general(q_ref[...].astype(BF16), k_ref[...].astype(BF16), _DIMS["nt"], preferred_element_type=F32)
            p = (s * _ret_decay(lg_ref[h], i, j, blk)).astype(BF16)
            o_ref[...] = acc[...] + jnp.dot(p, v_ref[...].astype(BF16), preferred_element_type=F32)

    return _pair_call(
        body, (RET_HEADS,), nb, False,
        [pl.BlockSpec(memory_space=pltpu.SMEM),
         pl.BlockSpec((blk, RET_DK), lambda h, i, j: (i, h)),
         pl.BlockSpec((blk, RET_DK), lambda h, i, j: (j, h)),
         pl.BlockSpec((blk, RET_DV), lambda h, i, j: (j, v_off + h))],
        pl.BlockSpec((blk, RET_DV), lambda h, i, j: (i, h)),
        jax.ShapeDtypeStruct((t, RET_HEADS * RET_DV), F32), [pltpu.VMEM((blk, RET_DV), F32)],
        "ret_fwd", (_ret_log_gamma(), qr, kr, proj), gather)


def _ret_bwd(qr, kr, proj, dr, t):
    blk = _seq_block(t)
    nb = t // blk
    v_off = (2 * RET_HEADS * RET_DK) // RET_DV

    def dq_body(i, j, lg_ref, q_ref, k_ref, v_ref, do_ref, dq_ref, acc):
        h = pl.program_id(0)

        @pl.when(j == 0)
        def _():
            acc[...] = jnp.zeros_like(acc)

        @pl.when(j < i)
        def _():
            a, b = _ret_row_decays(lg_ref[h], i, j, blk)
            ds = lax.dot_general(do_ref[...].astype(BF16), v_ref[...].astype(BF16), _DIMS["nt"], preferred_element_type=F32)
            acc[...] += a * jnp.dot(ds.astype(BF16), (k_ref[...] * b).astype(BF16), preferred_element_type=F32)

        @pl.when(j == i)
        def _():
            ds = lax.dot_general(do_ref[...].astype(BF16), v_ref[...].astype(BF16), _DIMS["nt"], preferred_element_type=F32)
            dsm = (ds * _ret_decay(lg_ref[h], i, j, blk)).astype(BF16)
            dq_ref[...] = acc[...] + jnp.dot(dsm, k_ref[...].astype(BF16), preferred_element_type=F32)

    dq = _pair_call(
        dq_body, (RET_HEADS,), nb, False,
        [pl.BlockSpec(memory_space=pltpu.SMEM),
         pl.BlockSpec((blk, RET_DK), lambda h, i, j: (i, h)),
         pl.BlockSpec((blk, RET_DK), lambda h, i, j: (j, h)),
         pl.BlockSpec((blk, RET_DV), lambda h, i, j: (j, v_off + h)),
         pl.BlockSpec((blk, RET_DV), lambda h, i, j: (i, h))],
        pl.BlockSpec((blk, RET_DK), lambda h, i, j: (i, h)),
        jax.ShapeDtypeStruct((t, RET_HEADS * RET_DK), F32), [pltpu.VMEM((blk, RET_DK), F32)],
        "ret_bwd_dq", (_ret_log_gamma(), qr, kr, proj, dr))

    def dkv_body(i, j, lg_ref, q_ref, k_ref, v_ref, do_ref, dk_ref, dv_ref, acc_k, acc_v):
        h = pl.program_id(0)

        @pl.when(i == j)
        def _():
            acc_k[...] = jnp.zeros_like(acc_k)
            acc_v[...] = jnp.zeros_like(acc_v)

        @pl.when(i > j)
        def _():
            a, b = _ret_row_decays(lg_ref[h], i, j, blk)
            qa, kb = (q_ref[...] * a).astype(BF16), (k_ref[...] * b).astype(BF16)
            do = do_ref[...].astype(BF16)
            p = lax.dot_general(qa, kb, _DIMS["nt"], preferred_element_type=F32).astype(BF16)
            acc_v[...] += lax.dot_general(p, do, _DIMS["tn"], preferred_element_type=F32)
            ds = lax.dot_general(do, v_ref[...].astype(BF16), _DIMS["nt"], preferred_element_type=F32).astype(BF16)
            acc_k[...] += b * lax.dot_general(ds, qa, _DIMS["tn"], preferred_element_type=F32)

        @pl.when(i == j)
        def _():
            q = q_ref[...].astype(BF16)
            do = do_ref[...].astype(BF16)
            decay = _ret_decay(lg_ref[h], i, j, blk)
            s = lax.dot_general(q, k_ref[...].astype(BF16), _DIMS["nt"], preferred_element_type=F32)
            acc_v[...] += lax.dot_general((s * decay).astype(BF16), do, _DIMS["tn"], preferred_element_type=F32)
            ds = lax.dot_general(do, v_ref[...].astype(BF16), _DIMS["nt"], preferred_element_type=F32)
            acc_k[...] += lax.dot_general((ds * decay).astype(BF16), q, _DIMS["tn"], preferred_element_type=F32)

        @pl.when(i == nb - 1)
        def _():
            dk_ref[...] = acc_k[...]
            dv_ref[...] = acc_v[...].astype(BF16)

    dk, dv = _pair_call(
        dkv_body, (RET_HEADS,), nb, True,
        [pl.BlockSpec(memory_space=pltpu.SMEM),
         pl.BlockSpec((blk, RET_DK), lambda h, j, i: (i, h)),
         pl.BlockSpec((blk, RET_DK), lambda h, j, i: (j, h)),
         pl.BlockSpec((blk, RET_DV), lambda h, j, i: (j, v_off + h)),
         pl.BlockSpec((blk, RET_DV), lambda h, j, i: (i, h))],
        [pl.BlockSpec((blk, RET_DK), lambda h, j, i: (j, h)), pl.BlockSpec((blk, RET_DV), lambda h, j, i: (j, h))],
        [jax.ShapeDtypeStruct((t, RET_HEADS * RET_DK), F32), jax.ShapeDtypeStruct((t, RET_HEADS * RET_DV), BF16)],
        [pltpu.VMEM((blk, RET_DK), F32), pltpu.VMEM((blk, RET_DV), F32)],
        "ret_bwd_dkv", (_ret_log_gamma(), qr, kr, proj, dr))
    return dq, dk, dv


def _cumsum(x, reverse, name):
    t = x.shape[0]
    blk = _seq_block(t)
    nb = t // blk

    def body(x_ref, o_ref, carry):
        @pl.when(pl.program_id(0) == 0)
        def _():
            carry[...] = jnp.zeros_like(carry)

        r = lax.broadcasted_iota(jnp.int32, (blk, blk), 0)
        c = lax.broadcasted_iota(jnp.int32, (blk, blk), 1)
        tri = ((r <= c) if reverse else (r >= c)).astype(F32)
        o_ref[...] = jnp.dot(tri, x_ref[...], precision=HIGHEST, preferred_element_type=F32) + carry[...]
        carry[...] = o_ref[0:1, :] if reverse else o_ref[blk - 1:blk, :]

    idx = (lambda i: (nb - 1 - i, 0)) if reverse else (lambda i: (i, 0))
    return pl.pallas_call(
        body, grid=(nb,), in_specs=[pl.BlockSpec((blk, LANES), idx)], out_specs=pl.BlockSpec((blk, LANES), idx),
        out_shape=jax.ShapeDtypeStruct((t, LANES), F32), scratch_shapes=[pltpu.VMEM((1, LANES), F32)],
        compiler_params=pltpu.CompilerParams(dimension_semantics=("arbitrary",), vmem_limit_bytes=VMEM_LIMIT),
        name=name)(x)


def _ssd_decay(a_ref, at_ref, hh, mask):
    return jnp.exp(jnp.where(mask, a_ref[:, hh:hh + 1] - at_ref[hh:hh + 1, :], -jnp.inf))


def _head_lanes(s):
    lane = lax.broadcasted_iota(jnp.int32, (s.shape[0], LANES), 1)
    return jnp.concatenate([jnp.where(lane < SSM_P, s[:, 2 * p:2 * p + 1], s[:, 2 * p + 1:2 * p + 2])
                            for p in range(SSM_HEADS // 2)], axis=1)


_B_OFF, _C_OFF = 1024 // SSM_N, 1024 // SSM_N + SSM_GROUPS


def _ssd_fwd(xbc_act, xdt, a_cum, a_cum_t, t, gather=()):
    blk = _seq_block(t)
    nb = t // blk

    def body(i, j, c0, c1, b0, b1, x_ref, ai_ref, aj_ref, at_ref, o_ref, acc):
        @pl.when(j == 0)
        def _():
            acc[...] = jnp.zeros_like(acc)

        @pl.when(j < i)
        def _():
            xv = (x_ref[...] * _head_lanes(jnp.exp(ai_ref[0:1, :] - aj_ref[...]))).astype(BF16)
            for g, (c_ref, b_ref) in enumerate(((c0, b0), (c1, b1))):
                cb = lax.dot_general(c_ref[...].astype(BF16), b_ref[...].astype(BF16), _DIMS["nt"], preferred_element_type=F32)
                cols = slice(g * half, (g + 1) * half)
                acc[:, cols] += jnp.dot(cb.astype(BF16), xv[:, cols], preferred_element_type=F32)

        @pl.when(j == i)
        def _():
            o_ref[...] = acc[...] * _head_lanes(jnp.exp(ai_ref[...] - ai_ref[0:1, :]))
            mask = _causal_diff(i, j, blk) >= 0
            for g, (c_ref, b_ref) in enumerate(((c0, b0), (c1, b1))):
                cb = lax.dot_general(c_ref[...].astype(BF16), b_ref[...].astype(BF16), _DIMS["nt"], preferred_element_type=F32)
                for h in range(SSM_HG):
                    hh = g * SSM_HG + h
                    cols = slice(hh * SSM_P, (hh + 1) * SSM_P)
                    m = (cb * _ssd_decay(ai_ref, at_ref, hh, mask)).astype(BF16)
                    o_ref[:, cols] += jnp.dot(m, x_ref[:, cols].astype(BF16), preferred_element_type=F32)

    half = SSM_HG * SSM_P
    row_i = lambda off: pl.BlockSpec((blk, SSM_N), lambda i, j, off=off: (i, off))
    row_j = lambda off: pl.BlockSpec((blk, SSM_N), lambda i, j, off=off: (jnp.minimum(j, i), off))
    return _pair_call(
        body, (), nb, False,
        [row_i(_C_OFF), row_i(_C_OFF + 1), row_j(_B_OFF), row_j(_B_OFF + 1),
         pl.BlockSpec((blk, SSM_HEADS * SSM_P), lambda i, j: (j, 0)),
         pl.BlockSpec((blk, LANES), lambda i, j: (i, 0)),
         pl.BlockSpec((blk, LANES), lambda i, j: (j, 0)),
         pl.BlockSpec((SSM_HEADS, blk), lambda i, j: (0, j))],
        pl.BlockSpec((blk, SSM_HEADS * SSM_P), lambda i, j: (i, 0)),
        jax.ShapeDtypeStruct((t, SSM_HEADS * SSM_P), F32), [pltpu.VMEM((blk, SSM_HEADS * SSM_P), F32)],
        "ssd_fwd", (xbc_act, xbc_act, xbc_act, xbc_act, xdt, a_cum, a_cum, a_cum_t), gather)


def _ssd_bwd(xbc_act, xdt, a_cum, a_cum_t, dy, expand, t, exchange=()):
    blk = _seq_block(t)
    nb = t // blk
    width = SSM_HEADS * SSM_P

    half = SSM_HG * SSM_P

    def head_sums(prod, e_ref):
        return lax.dot_general(prod, e_ref[...], _DIMS["nt"], precision=HIGHEST, preferred_element_type=F32)

    def q_body(i, j, c0, c1, b0, b1, x_ref, ai_ref, aj_ref, at_ref, dy_ref, e_ref, dc_ref, da_ref, acc_c, acc_a, acc_p, dyu):
        @pl.when(j == 0)
        def _():
            acc_c[...] = jnp.zeros_like(acc_c)
            acc_a[...] = jnp.zeros_like(acc_a)
            acc_p[...] = jnp.zeros_like(acc_p)
            dyu[...] = (dy_ref[...].astype(F32) * _head_lanes(jnp.exp(ai_ref[...] - ai_ref[0:1, :]))).astype(BF16)

        @pl.when(j < i)
        def _():
            xv = (x_ref[...] * _head_lanes(jnp.exp(ai_ref[0:1, :] - aj_ref[...]))).astype(BF16)
            for g, (c_ref, b_ref) in enumerate(((c0, b0), (c1, b1))):
                cols = slice(g * half, (g + 1) * half)
                bj = b_ref[...].astype(BF16)
                cb = lax.dot_general(c_ref[...].astype(BF16), bj, _DIMS["nt"], preferred_element_type=F32).astype(BF16)
                dcb = lax.dot_general(dyu[:, cols], xv[:, cols], _DIMS["nt"], preferred_element_type=F32)
                acc_c[:, g * SSM_N:(g + 1) * SSM_N] += jnp.dot(dcb.astype(BF16), bj, preferred_element_type=F32)
                acc_p[:, cols] += dyu[:, cols].astype(F32) * jnp.dot(cb, xv[:, cols], preferred_element_type=F32)

        @pl.when(j == i)
        def _():
            mask = _causal_diff(i, j, blk) >= 0
            for g, (c_ref, b_ref) in enumerate(((c0, b0), (c1, b1))):
                bj = b_ref[...].astype(BF16)
                cb = lax.dot_general(c_ref[...].astype(BF16), bj, _DIMS["nt"], preferred_element_type=F32)
                dcb = jnp.zeros((blk, blk), F32)
                for h in range(SSM_HG):
                    hh = g * SSM_HG + h
                    cols = slice(hh * SSM_P, (hh + 1) * SSM_P)
                    dm = lax.dot_general(dy_ref[:, cols].astype(BF16), x_ref[:, cols].astype(BF16), _DIMS["nt"],
                                         preferred_element_type=F32) * _ssd_decay(ai_ref, at_ref, hh, mask)
                    dcb = dcb + dm
                    acc_a[:, hh:hh + 1] += jnp.sum(dm * cb, axis=1, keepdims=True)
                acc_c[:, g * SSM_N:(g + 1) * SSM_N] += jnp.dot(dcb.astype(BF16), bj, preferred_element_type=F32)
            dc_ref[...] = acc_c[...]
            da_ref[...] = acc_a[...] + head_sums(acc_p[...], e_ref)

    row_i = lambda off: pl.BlockSpec((blk, SSM_N), lambda i, j, off=off: (i, off))
    row_j = lambda off: pl.BlockSpec((blk, SSM_N), lambda i, j, off=off: (jnp.minimum(j, i), off))
    e_spec = pl.BlockSpec((LANES, width), lambda i, j: (0, 0))
    dc, da_q = _pair_call(
        q_body, (), nb, False,
        [row_i(_C_OFF), row_i(_C_OFF + 1), row_j(_B_OFF), row_j(_B_OFF + 1),
         pl.BlockSpec((blk, width), lambda i, j: (j, 0)),
         pl.BlockSpec((blk, LANES), lambda i, j: (i, 0)),
         pl.BlockSpec((blk, LANES), lambda i, j: (j, 0)),
         pl.BlockSpec((SSM_HEADS, blk), lambda i, j: (0, j)),
         pl.BlockSpec((blk, width), lambda i, j: (i, 0)), e_spec],
        [pl.BlockSpec((blk, SSM_GROUPS * SSM_N), lambda i, j: (i, 0)), pl.BlockSpec((blk, LANES), lambda i, j: (i, 0))],
        [jax.ShapeDtypeStruct((t, SSM_GROUPS * SSM_N), F32), jax.ShapeDtypeStruct((t, LANES), F32)],
        [pltpu.VMEM((blk, SSM_GROUPS * SSM_N), F32), pltpu.VMEM((blk, LANES), F32), pltpu.VMEM((blk, width), F32),
         pltpu.VMEM((blk, width), BF16)],
        "ssd_bwd_q", (xbc_act, xbc_act, xbc_act, xbc_act, xdt, a_cum, a_cum, a_cum_t, dy, expand))

    def k_body(i, j, c0, c1, b0, b1, x_ref, ai_ref, aj_ref, at_ref, dy_ref, e_ref, db_ref, dx_ref, da_ref, dat_ref, acc_b, acc_x, acc_a,
               acc_p):
        @pl.when(i == j)
        def _():
            acc_b[...] = jnp.zeros_like(acc_b)
            acc_x[...] = jnp.zeros_like(acc_x)
            acc_a[...] = jnp.zeros_like(acc_a)
            acc_p[...] = jnp.zeros_like(acc_p)

        @pl.when(i > j)
        def _():
            v = _head_lanes(jnp.exp(ai_ref[0:1, :] - aj_ref[...]))
            dyu_all = (dy_ref[...].astype(F32) * _head_lanes(jnp.exp(ai_ref[...] - ai_ref[0:1, :]))).astype(BF16)
            xv = (x_ref[...] * v).astype(BF16)
            for g, (c_ref, b_ref) in enumerate(((c0, b0), (c1, b1))):
                cols = slice(g * half, (g + 1) * half)
                ci = c_ref[...].astype(BF16)
                cb = lax.dot_general(ci, b_ref[...].astype(BF16), _DIMS["nt"], preferred_element_type=F32).astype(BF16)
                dcb = lax.dot_general(dyu_all[:, cols], xv[:, cols], _DIMS["nt"], preferred_element_type=F32)
                acc_b[:, g * SSM_N:(g + 1) * SSM_N] += lax.dot_general(dcb.astype(BF16), ci, _DIMS["tn"], preferred_element_type=F32)
                dxv = lax.dot_general(cb, dyu_all[:, cols], _DIMS["tn"], preferred_element_type=F32)
                acc_x[:, cols] += v[:, cols] * dxv
                acc_p[:, cols] += xv[:, cols].astype(F32) * dxv

        @pl.when(i == j)
        def _():
            mask = _causal_diff(i, j, blk) >= 0
            for g, (c_ref, b_ref) in enumerate(((c0, b0), (c1, b1))):
                ci = c_ref[...].astype(BF16)
                cb = lax.dot_general(ci, b_ref[...].astype(BF16), _DIMS["nt"], preferred_element_type=F32)
                dcb = jnp.zeros((blk, blk), F32)
                for h in range(SSM_HG):
                    hh = g * SSM_HG + h
                    cols = slice(hh * SSM_P, (hh + 1) * SSM_P)
                    decay = _ssd_decay(ai_ref, at_ref, hh, mask)
                    dyh = dy_ref[:, cols].astype(BF16)
                    acc_x[:, cols] += lax.dot_general((cb * decay).astype(BF16), dyh, _DIMS["tn"], preferred_element_type=F32)
                    dm = lax.dot_general(dyh, x_ref[:, cols].astype(BF16), _DIMS["nt"], preferred_element_type=F32) * decay
                    dcb = dcb + dm
                    acc_a[hh:hh + 1, :] += jnp.sum(dm * cb, axis=0, keepdims=True)
                acc_b[:, g * SSM_N:(g + 1) * SSM_N] += lax.dot_general(dcb.astype(BF16), ci, _DIMS["tn"], preferred_element_type=F32)

        @pl.when(i == nb - 1)
        def _():
            db_ref[...] = acc_b[...]
            dx_ref[...] = acc_x[...]
            dat_ref[...] = acc_a[...]
            da_ref[...] = head_sums(acc_p[...], e_ref)

    rowk_i = lambda off: pl.BlockSpec((blk, SSM_N), lambda j, i, off=off: (jnp.maximum(i, j), off))
    rowk_j = lambda off: pl.BlockSpec((blk, SSM_N), lambda j, i, off=off: (j, off))
    db, dx, da_k, da_k_t, *landed = _pair_call(
        k_body, (), nb, True,
        [rowk_i(_C_OFF), rowk_i(_C_OFF + 1), rowk_j(_B_OFF), rowk_j(_B_OFF + 1),
         pl.BlockSpec((blk, width), lambda j, i: (j, 0)),
         pl.BlockSpec((blk, LANES), lambda j, i: (i, 0)),
         pl.BlockSpec((blk, LANES), lambda j, i: (j, 0)),
         pl.BlockSpec((SSM_HEADS, blk), lambda j, i: (0, j)),
         pl.BlockSpec((blk, width), lambda j, i: (i, 0)), e_spec],
        [pl.BlockSpec((blk, SSM_GROUPS * SSM_N), lambda j, i: (j, 0)), pl.BlockSpec((blk, width), lambda j, i: (j, 0)),
         pl.BlockSpec((blk, LANES), lambda j, i: (j, 0)), pl.BlockSpec((SSM_HEADS, blk), lambda j, i: (0, j))],
        [jax.ShapeDtypeStruct((t, SSM_GROUPS * SSM_N), F32), jax.ShapeDtypeStruct((t, width), F32),
         jax.ShapeDtypeStruct((t, LANES), F32), jax.ShapeDtypeStruct((SSM_HEADS, t), F32)],
        [pltpu.VMEM((blk, SSM_GROUPS * SSM_N), F32), pltpu.VMEM((blk, width), F32), pltpu.VMEM((SSM_HEADS, blk), F32),
         pltpu.VMEM((blk, width), F32)],
        "ssd_bwd_k", (xbc_act, xbc_act, xbc_act, xbc_act, xdt, a_cum, a_cum, a_cum_t, dy, expand), exchange=exchange)
    return dc, da_q, db, dx, da_k, da_k_t, landed


S5_SUPER = 4
S5_NARROW, S5_WIDE = S5_GROUPS * S5_GROUP // S5_SUPER, S5_GROUPS * S5_STATE // S5_SUPER


def _s5_tiles(t):
    tm = min(1024, t)
    narrow = lambda off: pl.BlockSpec((tm, S5_NARROW), lambda s, i, off=off: (i, s + off))
    wide = pl.BlockSpec((tm, S5_WIDE), lambda s, i: (i, s))
    weight = lambda shape: pl.BlockSpec((1,) + shape, lambda s, i: (s, 0, 0))
    params = pltpu.CompilerParams(dimension_semantics=("arbitrary", "arbitrary"), vmem_limit_bytes=VMEM_LIMIT)
    return tm, narrow, wide, weight, params


def _s5_expand(a, a_off, w, w_is_wide_first, name):
    t = a.shape[0]
    tm, narrow, wide, weight, params = _s5_tiles(t)

    def body(a_ref, w_ref, re_ref, im_ref):
        dims = _DIMS["nt"] if w_is_wide_first else _DIMS["nn"]
        o = lax.dot_general(a_ref[...].astype(BF16), w_ref[0].astype(BF16), dims, preferred_element_type=F32)
        re_ref[...] = o[:, :S5_WIDE]
        im_ref[...] = o[:, S5_WIDE:]

    return pl.pallas_call(
        body, grid=(S5_SUPER, t // tm), in_specs=[narrow(a_off), weight(w.shape[1:])], out_specs=[wide, wide],
        out_shape=[jax.ShapeDtypeStruct((t, S5_COLS), F32)] * 2, compiler_params=params, name=name)(a, w)


def _s5_contract(x_re, x_im, w, w_is_wide_first, name, res=None, out_dtype=F32):
    t = x_re.shape[0]
    tm, narrow, wide, weight, params = _s5_tiles(t)
    has_res = res is not None

    def body(*refs):
        re_ref, im_ref, w_ref, o_ref = refs[0], refs[1], refs[2], refs[-1]
        x = jnp.concatenate([re_ref[...].astype(BF16), im_ref[...].astype(BF16)], axis=1)
        dims = _DIMS["nn"] if w_is_wide_first else _DIMS["nt"]
        o = lax.dot_general(x, w_ref[0].astype(BF16), dims, preferred_element_type=F32)
        o_ref[...] = (o + refs[3][...] if has_res else o).astype(out_dtype)

    return pl.pallas_call(
        body, grid=(S5_SUPER, t // tm), in_specs=[wide, wide, weight(w.shape[1:])] + ([narrow(0)] if has_res else []),
        out_specs=narrow(0), out_shape=jax.ShapeDtypeStruct((t, S5_SUPER * S5_NARROW), out_dtype), compiler_params=params,
        name=name)(*([x_re, x_im, w] + ([res] if has_res else [])))


def _s5_wgrad(a, a_off, x_re, x_im, wide_first, name):
    t = a.shape[0]
    tm, narrow, wide, weight, params = _s5_tiles(t)
    shape = (2 * S5_WIDE, S5_NARROW) if wide_first else (S5_NARROW, 2 * S5_WIDE)

    def body(a_ref, re_ref, im_ref, o_ref):
        @pl.when(pl.program_id(1) == 0)
        def _():
            o_ref[...] = jnp.zeros_like(o_ref)

        x = jnp.concatenate([re_ref[...].astype(BF16), im_ref[...].astype(BF16)], axis=1)
        av = a_ref[...].astype(BF16)
        o_ref[0] += lax.dot_general(x, av, _DIMS["tn"], preferred_element_type=F32) if wide_first else \
            lax.dot_general(av, x, _DIMS["tn"], preferred_element_type=F32)

    return pl.pallas_call(
        body, grid=(S5_SUPER, t // tm), in_specs=[narrow(a_off), wide, wide], out_specs=weight(shape),
        out_shape=jax.ShapeDtypeStruct((S5_SUPER,) + shape, F32), compiler_params=params, name=name)(a, x_re, x_im)


def _s5_scan(b_re, b_im, a_re, a_im, reverse, x_prev=None):
    t = b_re.shape[0]
    cb = 512
    ncb = S5_COLS // cb
    tb = min(1024, t)
    ntb = t // tb
    sub = 8
    shape = (sub, cb)

    def cmul(ar, ai, br, bi):
        return ar * br - ai * bi, ar * bi + ai * br

    def body(*refs):
        if reverse:
            br_ref, bi_ref, ar_ref, ai_ref, xr_ref, xi_ref, or_ref, oi_ref, dar_ref, dai_ref, carry, dacc, states = refs
        else:
            br_ref, bi_ref, ar_ref, ai_ref, or_ref, oi_ref, or_bf_ref, oi_bf_ref, carry = refs

        @pl.when(pl.program_id(1) == 0)
        def _():
            carry[...] = jnp.zeros_like(carry)
            if reverse:
                dacc[...] = jnp.zeros_like(dacc)

        row = lax.broadcasted_iota(jnp.int32, shape, 0)
        a1 = (jnp.broadcast_to(ar_ref[...], shape), jnp.broadcast_to(-ai_ref[...] if reverse else ai_ref[...], shape))
        a2 = cmul(*a1, *a1)
        a4 = cmul(*a2, *a2)
        a8 = cmul(*a4, *a4)
        steps = (sub - row) if reverse else (row + 1)
        pw = (jnp.ones(shape, F32), jnp.zeros(shape, F32))
        for bit, p in ((1, a1), (2, a2), (4, a4), (8, a8)):
            on = (steps & bit) != 0
            pw = cmul(*pw, jnp.where(on, p[0], 1.0), jnp.where(on, p[1], 0.0))
        edge = 0 if reverse else sub - 1

        def shift(v, s):
            if reverse:
                return jnp.where(row < sub - s, pltpu.roll(v, sub - s, 0), 0.0)
            return jnp.where(row >= s, pltpu.roll(v, s, 0), 0.0)

        def tile(n, state):
            r0 = pl.multiple_of((tb // sub - 1 - n if reverse else n) * sub, sub)
            rows = pl.ds(r0, sub)
            xr, xi = br_ref[rows, :], bi_ref[rows, :]
            for s, p in ((1, a1), (2, a2), (4, a4)):
                dr, di = cmul(*p, shift(xr, s), shift(xi, s))
                xr, xi = xr + dr, xi + di
            cr, ci = jnp.broadcast_to(state[0], shape), jnp.broadcast_to(state[1], shape)
            dr, di = cmul(*pw, cr, ci)
            xr, xi = xr + dr, xi + di
            if reverse:
                states[0, rows, :], states[1, rows, :] = xr, xi
                gr = jnp.where(row < sub - 1, pltpu.roll(xr, sub - 1, 0), cr)
                gi = jnp.where(row < sub - 1, pltpu.roll(xi, sub - 1, 0), ci)
                pr, pi = xr_ref[rows, :], xi_ref[rows, :]
                dacc[0] += gr * pr + gi * pi
                dacc[1] += gi * pr - gr * pi
            else:
                or_ref[rows, :], oi_ref[rows, :] = xr, xi
            return (jnp.sum(jnp.where(row == edge, xr, 0.0), axis=0, keepdims=True),
                    jnp.sum(jnp.where(row == edge, xi, 0.0), axis=0, keepdims=True))

        last = lax.fori_loop(0, tb // sub, tile, (carry[0], carry[1]))
        carry[0], carry[1] = last
        if reverse:
            or_ref[...], oi_ref[...] = states[0].astype(BF16), states[1].astype(BF16)
            dar_ref[...] = jnp.sum(dacc[0], axis=0, keepdims=True)
            dai_ref[...] = jnp.sum(dacc[1], axis=0, keepdims=True)
        else:
            or_bf_ref[...], oi_bf_ref[...] = or_ref[...].astype(BF16), oi_ref[...].astype(BF16)

    tblock = (lambda k: ntb - 1 - k) if reverse else (lambda k: k)
    re_spec = pl.BlockSpec((tb, cb), lambda j, k: (tblock(k), j))
    a_spec = pl.BlockSpec((1, cb), lambda j, k: (0, j))
    ins, in_specs = [b_re, b_im, a_re, a_im], [re_spec, re_spec, a_spec, a_spec]
    seq_bf = jax.ShapeDtypeStruct((t, S5_COLS), BF16)
    scratch = [pltpu.VMEM((2, 1, cb), F32)]
    if reverse:
        ins += list(x_prev)
        in_specs += [re_spec, re_spec]
        out_shape = [seq_bf, seq_bf] + [jax.ShapeDtypeStruct((1, S5_COLS), F32)] * 2
        out_specs = [re_spec, re_spec, a_spec, a_spec]
        scratch += [pltpu.VMEM((2, sub, cb), F32), pltpu.VMEM((2, tb, cb), F32)]
    else:
        out_shape = [jax.ShapeDtypeStruct((t, S5_COLS), F32)] * 2 + [seq_bf, seq_bf]
        out_specs = [re_spec] * 4
    return pl.pallas_call(
        body, grid=(ncb, ntb), in_specs=in_specs, out_specs=out_specs, out_shape=out_shape, scratch_shapes=scratch,
        compiler_params=pltpu.CompilerParams(dimension_semantics=("arbitrary", "arbitrary"), vmem_limit_bytes=VMEM_LIMIT),
        name="s5_scan_bwd" if reverse else "s5_scan_fwd")(*ins)


CONF_DIM = 512
CONF_K = 31
CONF_PAD = 32


def _conf_conv(proj_o, w, b, dc1=None):
    t = proj_o.shape[0]
    cb = LANES
    ncb = CONF_DIM // cb
    chunk = min(512, t)
    chunks = range(0, t, chunk)
    bwd = dc1 is not None

    def body(*refs):
        if bwd:
            ca_ref, cg_ref, w_ref, b_ref, dy_ref, dca_ref, dcg_ref, dw_ref, db_ref, xs, dys = refs
        else:
            ca_ref, cg_ref, w_ref, b_ref, o_ref, xs = refs
        xs[pl.ds(0, CONF_PAD), :] = jnp.zeros((CONF_PAD, cb), F32)
        for t0 in chunks:
            rows = pl.ds(t0, chunk)
            xs[pl.ds(CONF_PAD + t0, chunk), :] = ca_ref[rows, :] * jax.nn.sigmoid(cg_ref[rows, :])
        if not bwd:
            for t0 in chunks:
                acc = jnp.broadcast_to(b_ref[...], (chunk, cb))
                for k in range(CONF_K):
                    acc = acc + w_ref[k] * xs[pl.ds(CONF_PAD + t0 - (CONF_K - 1 - k), chunk), :]
                o_ref[pl.ds(t0, chunk), :] = acc
            return
        dys[pl.ds(t, CONF_PAD), :] = jnp.zeros((CONF_PAD, cb), F32)
        db = jnp.zeros((1, cb), F32)
        for t0 in chunks:
            dys[pl.ds(t0, chunk), :] = dy_ref[pl.ds(t0, chunk), :]
            db = db + jnp.sum(dy_ref[pl.ds(t0, chunk), :], axis=0, keepdims=True)
        db_ref[...] = db
        for t0 in chunks:
            rows = pl.ds(t0, chunk)
            acc = jnp.zeros((chunk, cb), F32)
            for k in range(CONF_K):
                acc = acc + w_ref[k] * dys[pl.ds(t0 + (CONF_K - 1 - k), chunk), :]
            sig = jax.nn.sigmoid(cg_ref[rows, :])
            dca_ref[rows, :] = (acc * sig).astype(BF16)
            dcg_ref[rows, :] = (acc * ca_ref[rows, :] * sig * (1.0 - sig)).astype(BF16)
        for k in range(CONF_K):
            dwk = jnp.zeros((1, cb), F32)
            for t0 in chunks:
                window = xs[pl.ds(CONF_PAD + t0 - (CONF_K - 1 - k), chunk), :]
                dwk = dwk + jnp.sum(dy_ref[pl.ds(t0, chunk), :] * window, axis=0, keepdims=True)
            dw_ref[k] = dwk

    col = lambda off: pl.BlockSpec((t, cb), lambda j, off=off: (0, j + off))
    w_spec = pl.BlockSpec((CONF_K, 1, cb), lambda j: (0, 0, j))
    b_spec = pl.BlockSpec((1, cb), lambda j: (0, j))
    seq = jax.ShapeDtypeStruct((t, CONF_DIM), F32)
    ins, in_specs = [proj_o, proj_o, w, b], [col(0), col(ncb), w_spec, b_spec]
    scratch = [pltpu.VMEM((CONF_PAD + t, cb), F32)]
    if bwd:
        ins, in_specs = ins + [dc1], in_specs + [col(0)]
        seq_bf = jax.ShapeDtypeStruct((t, CONF_DIM), BF16)
        out_shape, out_specs = [seq_bf, seq_bf, jax.ShapeDtypeStruct(w.shape, F32), jax.ShapeDtypeStruct(b.shape, F32)], [col(0), col(0), w_spec, b_spec]
        scratch = scratch + [pltpu.VMEM((t + CONF_PAD, cb), F32)]
    else:
        out_shape, out_specs = seq, col(0)
    return pl.pallas_call(
        body, grid=(ncb,), in_specs=in_specs, out_specs=out_specs, out_shape=out_shape, scratch_shapes=scratch,
        compiler_params=pltpu.CompilerParams(dimension_semantics=("arbitrary",), vmem_limit_bytes=VMEM_LIMIT),
        name="conf_conv_bwd" if bwd else "conf_conv")(*ins)


N_CHIPS = 4
_HBM = pl.BlockSpec(memory_space=pl.ANY)
_MESH_ID = pl.DeviceIdType.MESH


def _comm_call(body, srcs, out_shapes, n_sems, name):
    n = len(srcs)
    return pl.pallas_call(
        body, out_shape=out_shapes, in_specs=[_HBM] * n, out_specs=[_HBM] * n,
        scratch_shapes=[pltpu.SemaphoreType.DMA((n_sems,)), pltpu.SemaphoreType.DMA((n_sems,)), pltpu.SemaphoreType.DMA((n,))],
        compiler_params=pltpu.CompilerParams(has_side_effects=True), name=name)(*srcs)


GATHER_SEMS = N_DEV - 1


def _gather_phases(src_refs, out_refs, send_sems, recv_sems, local_sems):
    n = len(src_refs)
    x, y, c = lax.axis_index("x"), lax.axis_index("y"), lax.axis_index("c")
    me, sibling = (x, y, c), (x, y, 1 - c)
    chips = [(1 - x, y), (x, 1 - y), (1 - x, 1 - y)]

    def copy(i, k, block, to, from_src=False):
        rows = out_refs[i].at[4 * block[0] + 2 * block[1] + block[2]]
        return pltpu.make_async_remote_copy(
            src_ref=src_refs[i] if from_src else rows, dst_ref=rows, send_sem=send_sems.at[GATHER_SEMS * i + k],
            recv_sem=recv_sems.at[GATHER_SEMS * i + k], device_id=to, device_id_type=_MESH_ID)

    def local(i):
        return pltpu.make_async_copy(src_refs[i], out_refs[i].at[4 * x + 2 * y + c], local_sems.at[i])

    def first(i):
        return [copy(i, 0, me, sibling, True)] + [copy(i, 1 + j, me, (*chip, c), True) for j, chip in enumerate(chips)]

    def passed(i, j):
        return copy(i, 4 + j, (*chips[j], c), sibling)

    def start():
        for i in range(n):
            local(i).start()
            for cp in first(i):
                cp.start()

    def forward():
        for j, chip in enumerate(chips):
            for i in range(n):
                copy(i, 1 + j, (*chip, c), me).wait_recv()
                passed(i, j).start()

    def finish():
        for i in range(n):
            copy(i, 0, sibling, me).wait_recv()
            for j, chip in enumerate(chips):
                copy(i, 4 + j, (*chip, 1 - c), me).wait_recv()
        for i in range(n):
            for cp in first(i) + [passed(i, j) for j in range(len(chips))]:
                cp.wait_send()
            local(i).wait()

    return start, forward, finish


def _gather(srcs, name):
    n = len(srcs)

    def body(*refs):
        for phase in _gather_phases(refs[:n], refs[n:2 * n], *refs[2 * n:]):
            phase()

    return _comm_call(body, srcs, [jax.ShapeDtypeStruct((N_DEV,) + s.shape, s.dtype) for s in srcs], GATHER_SEMS * n, name)


def _swap_sibling(srcs, name):
    n = len(srcs)

    def body(*refs):
        src_refs, out_refs = refs[:n], refs[n:2 * n]
        send_sems, recv_sems, _ = refs[2 * n:]
        sibling = (lax.axis_index("x"), lax.axis_index("y"), 1 - lax.axis_index("c"))
        copies = [pltpu.make_async_remote_copy(src_ref=src_refs[i], dst_ref=out_refs[i], send_sem=send_sems.at[i],
                                               recv_sem=recv_sems.at[i], device_id=sibling, device_id_type=_MESH_ID) for i in range(n)]
        for cp in copies:
            cp.start()
        for cp in copies:
            cp.wait_recv()
        for cp in copies:
            cp.wait_send()

    return _comm_call(body, srcs, [jax.ShapeDtypeStruct(s.shape, s.dtype) for s in srcs], n, name)


EXCHANGE_SEMS = N_CHIPS - 1


def _exchange_phases(src_refs, out_refs, send_sems, recv_sems, local_sems):
    n = len(src_refs)
    x, y, c = lax.axis_index("x"), lax.axis_index("y"), lax.axis_index("c")
    mine = 2 * x + y

    def local(i):
        return pltpu.make_async_copy(src_refs[i].at[mine], out_refs[i].at[mine], local_sems.at[i])

    def copies(i, landing):
        out = []
        for k in range(1, N_CHIPS):
            px, py = x ^ (k >> 1), y ^ (k & 1)
            peer = 2 * px + py
            sem = EXCHANGE_SEMS * i + k - 1
            out.append(pltpu.make_async_remote_copy(
                src_ref=src_refs[i].at[peer], dst_ref=out_refs[i].at[peer if landing else mine], send_sem=send_sems.at[sem],
                recv_sem=recv_sems.at[sem], device_id=(px, py, c), device_id_type=_MESH_ID))
        return out

    def start():
        for i in range(n):
            local(i).start()
            for send in copies(i, False):
                send.start()

    def finish():
        for i in range(n):
            for recv in copies(i, True):
                recv.wait_recv()
        for i in range(n):
            for send in copies(i, False):
                send.wait_send()
            local(i).wait()

    return start, (lambda: None), finish


def _exchange_chips(srcs, name):
    n = len(srcs)

    def body(*refs):
        for phase in _exchange_phases(refs[:n], refs[n:2 * n], *refs[2 * n:]):
            phase()

    return _comm_call(body, srcs, [jax.ShapeDtypeStruct(s.shape, s.dtype) for s in srcs], EXCHANGE_SEMS * n, name)


def _add(a, b, name):
    k, rows, cols = a.shape
    tr = _row_tile(rows)

    def body(a_ref, b_ref, o_ref):
        o_ref[...] = (a_ref[...].astype(F32) + b_ref[...].astype(F32)).astype(o_ref.dtype)

    spec = pl.BlockSpec((k, tr, cols), lambda i: (0, i, 0))
    return pl.pallas_call(
        body, grid=(rows // tr,), in_specs=[spec, spec], out_specs=spec, out_shape=jax.ShapeDtypeStruct(a.shape, a.dtype),
        compiler_params=pltpu.CompilerParams(dimension_semantics=("parallel",), vmem_limit_bytes=VMEM_LIMIT), name=name)(a, b)


def _row_tile(r, pref=256):
    if r <= pref:
        return r
    t = pref // 16 * 16
    while r % t:
        t -= 16
    return t


def _join_cols(g, width, name):
    _, rows, ws = g.shape
    tr = _row_tile(rows)
    tail = width - N_DEV * ws

    def body(g_ref, o_ref):
        for d in range(N_DEV):
            o_ref[:, pl.ds(d * ws, ws)] = g_ref[d]
        if tail:
            o_ref[:, pl.ds(N_DEV * ws, tail)] = jnp.zeros((tr, tail), g.dtype)

    return pl.pallas_call(
        body, grid=(rows // tr,), in_specs=[pl.BlockSpec((N_DEV, tr, ws), lambda i: (0, i, 0))],
        out_specs=pl.BlockSpec((tr, width), lambda i: (i, 0)), out_shape=jax.ShapeDtypeStruct((rows, width), g.dtype),
        compiler_params=pltpu.CompilerParams(dimension_semantics=("parallel",), vmem_limit_bytes=VMEM_LIMIT), name=name)(g)


def _split_cols(full, ws, dtype, name):
    rows, width = full.shape
    tr = _row_tile(rows)

    def body(x_ref, o_ref):
        for d in range(N_DEV):
            o_ref[d] = x_ref[:, pl.ds(d * ws, ws)].astype(dtype)

    return pl.pallas_call(
        body, grid=(rows // tr,), in_specs=[pl.BlockSpec((tr, width), lambda i: (i, 0))],
        out_specs=pl.BlockSpec((N_DEV, tr, ws), lambda i: (0, i, 0)), out_shape=jax.ShapeDtypeStruct((N_DEV, rows, ws), dtype),
        compiler_params=pltpu.CompilerParams(dimension_semantics=("parallel",), vmem_limit_bytes=VMEM_LIMIT), name=name)(full)


def _adamw(parts, w, m, v, name):
    r, c = w.shape
    n_parts = parts.shape[0]
    tr = _row_tile(r)

    def body(p_ref, w_ref, m_ref, v_ref, g_ref, d_ref, nm_ref, nv_ref):
        g = p_ref[0].astype(F32)
        for s in range(1, n_parts):
            g = g + p_ref[s].astype(F32)
        nm = ADAM_B1 * m_ref[...] + (1.0 - ADAM_B1) * g
        nv = ADAM_B2 * v_ref[...] + (1.0 - ADAM_B2) * (g * g)
        m_hat = nm / (1.0 - ADAM_B1 ** ADAM_STEP)
        v_hat = nv / (1.0 - ADAM_B2 ** ADAM_STEP)
        g_ref[...] = g
        nm_ref[...] = nm
        nv_ref[...] = nv
        d_ref[...] = -ADAM_LR * (m_hat / (jnp.sqrt(v_hat) + ADAM_EPS) + ADAM_WD * w_ref[...])

    spec = pl.BlockSpec((tr, c), lambda i: (i, 0))
    return pl.pallas_call(
        body, grid=(r // tr,), in_specs=[pl.BlockSpec((n_parts, tr, c), lambda i: (0, i, 0)), spec, spec, spec],
        out_specs=[spec] * 4, out_shape=[jax.ShapeDtypeStruct((r, c), F32)] * 4,
        compiler_params=pltpu.CompilerParams(dimension_semantics=("parallel",), vmem_limit_bytes=VMEM_LIMIT),
        name=name)(parts, w, m, v)


def _pack_rows(n_elems, mult):
    rows = -(-n_elems // PACK_COLS)
    return -(-rows // mult) * mult


def _pack(arrays, dtype, mult, lead=()):
    flat = jnp.concatenate([a.astype(dtype).reshape(lead + (-1,)) for a in arrays], axis=-1)
    rows = _pack_rows(flat.shape[-1], mult)
    flat = jnp.pad(flat, [(0, 0)] * len(lead) + [(0, rows * PACK_COLS - flat.shape[-1])])
    return flat.reshape(lead + (rows, PACK_COLS))


def _unpack(buf, shapes, lead=()):
    flat = buf.reshape(lead + (-1,))
    out, off = [], 0
    for s in shapes:
        n = math.prod(s)
        out.append(flat[..., off:off + n].reshape(lead + tuple(s)))
        off += n
    return out


def _join_shards(piece, axis):
    moved = jnp.moveaxis(piece, 0, axis)
    shape = moved.shape
    return moved.reshape(shape[:axis] + (shape[axis] * shape[axis + 1],) + shape[axis + 2:])


def _split_shards(full, axis):
    shape = full.shape
    return jnp.moveaxis(full.reshape(shape[:axis] + (N_DEV, shape[axis] // N_DEV) + shape[axis + 1:]), axis, 0)


def _block_diag(blocks):
    g, r, c = blocks.shape
    k = g // S5_SUPER
    eye = jnp.eye(k, dtype=blocks.dtype)
    return (blocks.reshape(S5_SUPER, k, r, 1, c) * eye[None, :, None, :, None]).reshape(S5_SUPER, k * r, k * c)


def _diag_blocks(mat, r, c):
    k = mat.shape[1] // r
    eye = jnp.eye(k, dtype=mat.dtype)
    return jnp.sum(mat.reshape(S5_SUPER, k, r, k, c) * eye[None, :, None, :, None], axis=3).reshape(S5_SUPER * k, r, c)


def _pad_lanes(a):
    a = a.reshape(1, -1)
    return jnp.pad(a, ((0, 0), (0, LANES - a.shape[1])))


def _head_expand():
    h = jnp.arange(LANES)[:, None]
    ch = jnp.arange(SSM_HEADS * SSM_P)[None, :] // SSM_P
    return (h == ch).astype(F32)


def _rotary_tables(t):
    inv = 10000.0 ** (-jnp.arange(0, RET_DK, 2, dtype=F32) / RET_DK)
    ang = jnp.arange(t).astype(F32)[:, None] * inv[None, :]
    cos, sin = jnp.cos(ang), jnp.sin(ang)
    return jnp.concatenate([cos, cos], axis=1), jnp.concatenate([-sin, sin], axis=1)


def _rms_fwd(x, g, name):
    return _rows(_f_rms, x.shape[0], [(x, D_MODEL, 0)], [g], [D_MODEL], [], name, bf16_outs=(0,))[0]


def _rms_bwd(x, g, dh, dres, name):
    def fn(x_, dh_, dres_, g_):
        _, vjp = jax.vjp(lambda a, b: _rms(a, b), x_, g_)
        dx, dg = vjp(dh_)
        return dx + dres_, dx + dres_, dg
    return _rows(fn, x.shape[0], [(x, D_MODEL, 0), (dh, D_MODEL, 0), (dres, D_MODEL, 0)], [g], [D_MODEL, D_MODEL], [(1, D_MODEL)],
                 name, bf16_outs=(1,))


def _ffn_fwd(x, norm_g, w_up, dw_w, dw_b, w_down, tag):
    t = x.shape[0]
    nbk = D_FF // LANES
    h = _rms_fwd(x, norm_g, "ffn_norm_" + tag)
    up = _mm(h, w_up, "nn", "ffn_up_" + tag)
    mid = _cols(_f_ffnmid, nbk, [(up, 0), (up, nbk), (dw_w, 0), (dw_w, nbk), (dw_b, 0), (dw_b, nbk)], [(t,)], "ffn_mid_" + tag,
                bf16_outs=(0,))[0]
    out = _mm(mid, w_down, "nn", "ffn_down_" + tag, res=x)
    return out, (h, up, mid)


def _ffn_bwd(x, norm_g, w_up, dw_w, dw_b, w_down, saved, dout, dout_bf, tag):
    t = x.shape[0]
    nbk = D_FF // LANES
    h, up, mid = saved
    d_w_down = _mm(mid, dout_bf, "tn", "ffn_down_dw_" + tag, out_dtype=BF16)
    dmid = _mm(dout_bf, w_down, "nt", "ffn_down_dx_" + tag)
    dgin, duin, dwg, dwu, dbg, dbu = _cols(
        _grad_fn(_f_ffnmid, 6, 6), nbk,
        [(up, 0), (up, nbk), (dw_w, 0), (dw_w, nbk), (dw_b, 0), (dw_b, nbk), (dmid, 0)],
        [(t,), (t,), (3, 1), (3, 1), (1,), (1,)], "ffn_mid_bwd_" + tag, bf16_outs=(0, 1))
    dup = jnp.concatenate([dgin, duin], axis=1)
    d_w_up = _mm(h, dup, "tn", "ffn_up_dw_" + tag)
    dh = _mm(dup, w_up, "nt", "ffn_up_dx_" + tag)
    dx, dx_bf, dnorm = _rms_bwd(x, norm_g, dh, dout, "ffn_norm_bwd_" + tag)
    return dx, dx_bf, dict(norm=dnorm, w_up=d_w_up, dw_w=jnp.concatenate([dwg, dwu], axis=2)[:, 0],
                           dw_b=jnp.concatenate([dbg, dbu], axis=1), w_down=d_w_down)


def _local_step(x, tgt, w, late=None):
    t = x.shape[0]
    grads = {}
    expand = _head_expand()
    cosf, sins = _rotary_tables(t)
    mix_g = [w['mix_norm'][i:i + 1] for i in range(2)]
    ffn_g = [w['ffn_norm'][i:i + 1] for i in range(2)]
    ffn_dw_w = [w['ffn_dw_w'][i][:, None, :] for i in range(2)]
    ffn_dw_b = [w['ffn_dw_b'][i:i + 1] for i in range(2)]

    w_e = w['e_w_in']
    conv_w = w['e_conv_w'][0][:, None, :]
    conv_b = w['e_conv_b']
    dt_bias, a_log, d_skip = _pad_lanes(w['e_dt_bias']), _pad_lanes(w['e_a_log']), _pad_lanes(w['e_d'])
    h0 = _rms_fwd(x, mix_g[0], "mix_norm_0")
    proj = _mm(h0, w_e, "nn", "e_in")
    qr, kr = _rows(_f_retpre, t, [(proj, 512, 0), (proj, 512, 1), (cosf, LANES, 0), (sins, LANES, 0)], [], [512, 512], [], "ret_pre")
    if late:
        r, *landed = _ret_fwd(qr, kr, proj, t, late['ret_fwd'][0])
        late['ret_fwd'][1](w, landed)
    else:
        r = _ret_fwd(qr, kr, proj, t)
    y_ret = _rows(_f_retpost, t, [(r, 1024, 0), (proj, 1024, 2)], [], [1024], [], "ret_post", bf16_outs=(0,))[0]
    xbc_act = _cols(_f_ssdconv, 12, [(proj, 32), (conv_w, 0), (conv_b, 0)], [(t,)], "ssd_conv")[0]
    xdt, da = _rows(_f_ssdpre, t, [(xbc_act, 1024, 0), (proj, LANES, 44)], [dt_bias, a_log, expand], [1024, LANES], [], "ssd_pre")
    a_cum = _cumsum(da, False, "ssd_cumsum")
    a_cum_t = a_cum[:, :SSM_HEADS].T
    if late:
        yc, *landed = _ssd_fwd(xbc_act, xdt, a_cum, a_cum_t, t, late['ssd_fwd'][0])
        late['ssd_fwd'][1](w, landed)
    else:
        yc = _ssd_fwd(xbc_act, xdt, a_cum, a_cum_t, t)
    y_ssm = _rows(_f_ssdpost, t, [(yc, 1024, 0), (xbc_act, 1024, 0), (proj, 1024, 3)], [d_skip, w['e_ssm_norm'], expand],
                  [1024], [], "ssd_post", bf16_outs=(0,))[0]
    mix_e = jnp.concatenate([y_ret, y_ssm], axis=1)
    x1 = _mm(mix_e, w['e_w_out'], "nn", "e_out", res=x)
    x2, ffn0 = _ffn_fwd(x1, ffn_g[0], w['ffn_w_up'][0], ffn_dw_w[0], ffn_dw_b[0], w['ffn_w_down'][0], "0")

    lr, li = w['o_a_re'][0], w['o_a_im'][0]
    ls = w['o_log_step'].reshape(S5_GROUPS, 1)
    b_re3, b_im3 = jnp.transpose(w['o_b_re'][0], (2, 0, 1)), jnp.transpose(w['o_b_im'][0], (2, 0, 1))
    par_ins = [lr, li, ls, b_re3, b_im3]
    whole = lambda a: (a, a.shape, (lambda i, n=a.ndim: (0,) * n))
    par_shapes = [(S5_GROUPS, S5_STATE)] * 2 + [(S5_GROUP, S5_GROUPS, S5_STATE)] * 2
    ab_re, ab_im, bb_re, bb_im = _call(_f_s5par, (1,), [whole(a) for a in par_ins],
                                       [(s, F32, s, (lambda i, n=len(s): (0,) * n), False) for s in par_shapes], "s5_params")
    w_b = jnp.concatenate([_block_diag(jnp.transpose(bb_re, (1, 0, 2))), _block_diag(jnp.transpose(bb_im, (1, 0, 2)))], axis=2)
    w_c = jnp.concatenate([_block_diag(jnp.transpose(w['o_c_re'][0], (0, 2, 1))),
                           -_block_diag(jnp.transpose(w['o_c_im'][0], (0, 2, 1)))], axis=1)
    a_re, a_im = ab_re.reshape(1, S5_COLS), ab_im.reshape(1, S5_COLS)
    dw_w = w['o_dw_w'][0][:, None, :]
    glu_w = w['o_glu_w'].astype(F32)

    h1 = _rms_fwd(x2, mix_g[1], "mix_norm_1")
    proj_o = _mm(h1, w['o_w_in'], "nn", "o_in")
    c1 = _conf_conv(proj_o, dw_w, w['o_dw_b'])
    c2 = _rows(_f_confb, t, [(c1, 512, 0)], [w['o_ln_g'], w['o_ln_b']], [512], [], "conf_norm", bf16_outs=(0,))[0]
    u_off = 2 * CONF_DIM // S5_NARROW
    bu_re, bu_im = _s5_expand(proj_o, u_off, w_b, False, "s5_bu")
    xs_re, xs_im, xs_re_bf, xs_im_bf = _s5_scan(bu_re, bu_im, a_re, a_im, False)
    y_s5 = _s5_contract(xs_re_bf, xs_im_bf, w_c, True, "s5_cx")
    s_out = _rows(_f_s5post, t, [(y_s5, 512, 0), (proj_o, 512, 2)], [w['o_d'], glu_w], [512], [], "s5_post", bf16_outs=(0,))[0]
    mix_o = jnp.concatenate([c2, s_out], axis=1)
    x3 = _mm(mix_o, w['o_w_out'], "nn", "o_out", res=x2)
    x4, ffn1 = _ffn_fwd(x3, ffn_g[1], w['ffn_w_up'][1], ffn_dw_w[1], ffn_dw_b[1], w['ffn_w_down'][1], "1")

    dx4, dx4_bf, loss_blk, d_final = _rows(_loss_step, t, [(x4, D_MODEL, 0), (tgt, D_MODEL, 0)], [w['final_norm'].reshape(1, D_MODEL)],
                                           [D_MODEL, D_MODEL], [(8, LANES), (1, D_MODEL)], "loss_head", bf16_outs=(1,))
    loss = loss_blk[0, 0]
    grads['final_norm'] = d_final.reshape(D_MODEL)

    dx3, dx3_bf, g1 = _ffn_bwd(x3, ffn_g[1], w['ffn_w_up'][1], ffn_dw_w[1], ffn_dw_b[1], w['ffn_w_down'][1], ffn1, dx4, dx4_bf, "1")
    grads['o_w_out'] = _mm(mix_o, dx3_bf, "tn", "o_out_dw", out_dtype=BF16)
    dmix_o = _mm(dx3_bf, w['o_w_out'], "nt", "o_out_dx")
    dc1, d_ln_g, d_ln_b = _conf_norm_bwd(c1, w['o_ln_g'], w['o_ln_b'], dmix_o, t)
    dca, dcg, d_dw_w, d_dw_b = _conf_conv(proj_o, dw_w, w['o_dw_b'], dc1)
    dyc, du_skip, d_od, d_glu = _s5_post_bwd(y_s5, proj_o, w['o_d'], glu_w, dmix_o, t)
    grads['o_glu_w'] = d_glu
    d_w_c = _s5_wgrad(dyc, 0, xs_re_bf, xs_im_bf, True, "s5_cx_dw")
    dxs_re, dxs_im = _s5_expand(dyc, 0, w_c, True, "s5_cx_dx")
    g_re, g_im, d_are, d_aim = _s5_scan(dxs_re, dxs_im, a_re, a_im, True, (xs_re, xs_im))
    d_w_b = _s5_wgrad(proj_o, u_off, g_re, g_im, False, "s5_bu_dw")
    du = _s5_contract(g_re, g_im, w_b, False, "s5_bu_dx", res=du_skip, out_dtype=BF16)
    d_bb_re = jnp.transpose(_diag_blocks(d_w_b[:, :, :S5_WIDE], S5_GROUP, S5_STATE), (1, 0, 2))
    d_bb_im = jnp.transpose(_diag_blocks(d_w_b[:, :, S5_WIDE:], S5_GROUP, S5_STATE), (1, 0, 2))
    par_cts = [d_are.reshape(S5_GROUPS, S5_STATE), d_aim.reshape(S5_GROUPS, S5_STATE), d_bb_re, d_bb_im]
    in_shapes = [a.shape for a in par_ins]
    d_lr, d_li, d_ls, d_br3, d_bi3 = _call(_grad_fn(_f_s5par, 5, 5), (1,), [whole(a) for a in par_ins + par_cts],
                                           [(s, F32, s, (lambda i, n=len(s): (0,) * n), False) for s in in_shapes], "s5_params_bwd")
    grads['o_a_re'], grads['o_a_im'], grads['o_log_step'] = d_lr[None], d_li[None], d_ls.reshape(1, S5_GROUPS)
    grads['o_b_re'], grads['o_b_im'] = jnp.transpose(d_br3, (1, 2, 0))[None], jnp.transpose(d_bi3, (1, 2, 0))[None]
    grads['o_c_re'] = jnp.transpose(_diag_blocks(d_w_c[:, :S5_WIDE], S5_STATE, S5_GROUP), (0, 2, 1))[None]
    grads['o_c_im'] = -jnp.transpose(_diag_blocks(d_w_c[:, S5_WIDE:], S5_STATE, S5_GROUP), (0, 2, 1))[None]
    grads['o_d'], grads['o_ln_g'], grads['o_ln_b'] = d_od, d_ln_g, d_ln_b
    grads['o_dw_w'], grads['o_dw_b'] = d_dw_w[:, 0][None], d_dw_b
    dproj_o = jnp.concatenate([dca, dcg, du], axis=1)
    grads['o_w_in'] = _mm(h1, dproj_o, "tn", "o_in_dw")
    dh1 = _mm(dproj_o, w['o_w_in'], "nt", "o_in_dx")
    dx2, dx2_bf, d_mix1 = _rms_bwd(x2, mix_g[1], dh1, dx3, "mix_norm_bwd_1")
    in_flight = late['grads_ready']({**grads, 'ffn_w_up': [None, g1['w_up']], 'ffn_w_down': [None, g1['w_down']]}) if late else ()

    dx1, dx1_bf, g0 = _ffn_bwd(x1, ffn_g[0], w['ffn_w_up'][0], ffn_dw_w[0], ffn_dw_b[0], w['ffn_w_down'][0], ffn0, dx2, dx2_bf, "0")
    grads['e_w_out'] = _mm(mix_e, dx1_bf, "tn", "e_out_dw", out_dtype=BF16)
    dmix_e = _mm(dx1_bf, w['e_w_out'], "nt", "e_out_dx")
    dr, dg = _rows(_grad_fn(_f_retpost, 2, 2), t, [(r, 1024, 0), (proj, 1024, 2), (dmix_e, 1024, 0)], [], [1024, 1024], [], "ret_post_bwd",
                   bf16_outs=(0, 1))
    dqr, dkr, dv = _ret_bwd(qr, kr, proj, dr, t)
    dq, dk = _ret_pre_bwd(proj, cosf, sins, dqr, dkr, t)
    dyc0, dxs1, dz, d_dskip, d_ssm_norm = _ssd_post_bwd(yc, xbc_act, proj, d_skip, w['e_ssm_norm'], expand, dmix_e, t)
    dcm, da_q, dbm, dxdt, da_k, da_k_t, landed = _ssd_bwd(xbc_act, xdt, a_cum, a_cum_t, dyc0, expand, t, exchange=in_flight)
    if late:
        late['grads_landed'](landed)
    d_a_cum = da_q - da_k - jnp.pad(da_k_t.T, ((0, 0), (0, LANES - SSM_HEADS)))
    dda = _cumsum(d_a_cum, True, "ssd_cumsum_bwd")
    dxs, ddtr, d_dt_bias, d_a_log = _ssd_pre_bwd(xbc_act, proj, dt_bias, a_log, expand, dxdt, dda, dxs1, t)
    dxbc_act = jnp.concatenate([dxs, dbm, dcm], axis=1)
    dxbc, d_conv_w, d_conv_b = _cols(_grad_fn(_f_ssdconv, 3, 3), 12, [(proj, 32), (conv_w, 0), (conv_b, 0), (dxbc_act, 0)],
                                     [(t,), (4, 1), (1,)], "ssd_conv_bwd", bf16_outs=(0,))
    dproj = jnp.concatenate([dq, dk, dv, dg, dz, dxbc, ddtr], axis=1)
    grads['e_w_in'] = _mm(h0, dproj, "tn", "e_in_dw")
    dh0 = _mm(dproj, w_e, "nt", "e_in_dx")
    dx0, _, d_mix0 = _rms_bwd(x, mix_g[0], dh0, dx1, "mix_norm_bwd_0")

    grads['mix_norm'] = jnp.concatenate([d_mix0, d_mix1], axis=0)
    grads['e_conv_w'], grads['e_conv_b'] = d_conv_w[:, 0][None], d_conv_b
    grads['e_dt_bias'], grads['e_a_log'], grads['e_d'] = d_dt_bias[:, :SSM_HEADS], d_a_log[:, :SSM_HEADS], d_dskip[:, :SSM_HEADS]
    grads['e_ssm_norm'] = d_ssm_norm
    grads['ffn_norm'] = jnp.concatenate([g0['norm'], g1['norm']], axis=0)
    grads['ffn_w_up'], grads['ffn_w_down'] = [g0['w_up'], g1['w_up']], [g0['w_down'], g1['w_down']]
    grads['ffn_dw_w'] = jnp.stack([g0['dw_w'], g1['dw_w']], axis=0)
    grads['ffn_dw_b'] = jnp.concatenate([g0['dw_b'], g1['dw_b']], axis=0)
    return loss, dx0, grads


def _conf_norm_bwd(c1, ln_g, ln_b, dmix_o, t):
    def fn(c1_, dy_, g_, b_):
        _, vjp = jax.vjp(lambda a, b, c: _f_confb(a, b, c)[0], c1_, g_, b_)
        return vjp(dy_)
    return _rows(fn, t, [(c1, 512, 0), (dmix_o, 512, 0)], [ln_g, ln_b], [512], [(1, 512), (1, 512)], "conf_norm_bwd")


def _s5_post_bwd(y_s5, proj_o, d_skip, glu_w, dmix_o, t):
    def fn(yc_, u_, dy_, d_, gw_):
        _, vjp = jax.vjp(lambda a, b, c, e: _f_s5post(a, b, c, e)[0], yc_, u_, d_, gw_)
        return vjp(dy_)
    return _rows(fn, t, [(y_s5, 512, 0), (proj_o, 512, 2), (dmix_o, 512, 1)], [d_skip, glu_w], [512, 512], [(1, 512), (512, 512)],
                 "s5_post_bwd", bf16_outs=(0,))


def _ret_pre_bwd(proj, cosf, sins, dqr, dkr, t):
    def fn(q_, k_, cos_, sin_, dq_, dk_):
        _, vjp = jax.vjp(lambda a, b: _f_retpre(a, b, cos_, sin_), q_, k_)
        return vjp((dq_, dk_))
    return _rows(fn, t, [(proj, 512, 0), (proj, 512, 1), (cosf, LANES, 0), (sins, LANES, 0), (dqr, 512, 0), (dkr, 512, 0)], [],
                 [512, 512], [], "ret_pre_bwd", bf16_outs=(0, 1))


def _ssd_post_bwd(yc, xbc_act, proj, d_skip, norm_w, expand, dmix_e, t):
    def fn(yc_, xs_, z_, dy_, d_, nw_, e_):
        _, vjp = jax.vjp(lambda a, b, c, dd, n: _f_ssdpost(a, b, c, dd, n, e_)[0], yc_, xs_, z_, d_, nw_)
        return vjp(dy_)
    return _rows(fn, t, [(yc, 1024, 0), (xbc_act, 1024, 0), (proj, 1024, 3), (dmix_e, 1024, 1)], [d_skip, norm_w, expand],
                 [1024, 1024, 1024], [(1, LANES), (1, 1024)], "ssd_post_bwd", bf16_outs=(0, 2))


def _ssd_pre_bwd(xbc_act, proj, dt_bias, a_log, expand, dxdt, dda, dxs1, t):
    def fn(xs_, dtr_, dx_, dda_, dxs1_, bias_, alog_, e_):
        _, vjp = jax.vjp(lambda a, b, c, dd: _f_ssdpre(a, b, c, dd, e_), xs_, dtr_, bias_, alog_)
        dxs, ddtr, dbias, dalog = vjp((dx_, dda_))
        return dxs + dxs1_, ddtr, dbias, dalog
    return _rows(fn, t, [(xbc_act, 1024, 0), (proj, LANES, 44), (dxdt, 1024, 0), (dda, LANES, 0), (dxs1, 1024, 0)],
                 [dt_bias, a_log, expand], [1024, LANES], [(1, LANES), (1, LANES)], "ssd_pre_bwd", bf16_outs=(1,))


def kernel(x, mix_norm, e_w_in, e_conv_w, e_conv_b, e_dt_bias, e_a_log, e_d, e_ssm_norm, e_w_out, o_w_in, o_dw_w, o_dw_b, o_ln_g, o_ln_b, o_a_re, o_a_im, o_b_re, o_b_im, o_c_re, o_c_im, o_d, o_log_step, o_glu_w, o_w_out, ffn_norm, ffn_w_up, ffn_dw_w, ffn_dw_b, ffn_w_down, final_norm, loss_target, m_mix_norm, m_e_w_in, m_e_conv_w, m_e_conv_b, m_e_dt_bias, m_e_a_log, m_e_d, m_e_ssm_norm, m_e_w_out, m_o_w_in, m_o_dw_w, m_o_dw_b, m_o_ln_g, m_o_ln_b, m_o_a_re, m_o_a_im, m_o_b_re, m_o_b_im, m_o_c_re, m_o_c_im, m_o_d, m_o_log_step, m_o_glu_w, m_o_w_out, m_ffn_norm, m_ffn_w_up, m_ffn_dw_w, m_ffn_dw_b, m_ffn_w_down, m_final_norm, v_mix_norm, v_e_w_in, v_e_conv_w, v_e_conv_b, v_e_dt_bias, v_e_a_log, v_e_d, v_e_ssm_norm, v_e_w_out, v_o_w_in, v_o_dw_w, v_o_dw_b, v_o_ln_g, v_o_ln_b, v_o_a_re, v_o_a_im, v_o_b_re, v_o_b_im, v_o_c_re, v_o_c_im, v_o_d, v_o_log_step, v_o_glu_w, v_o_w_out, v_ffn_norm, v_ffn_w_up, v_ffn_dw_w, v_ffn_dw_b, v_ffn_w_down, v_final_norm):
    p = dict(locals())

    kinds = ("grad_", "delta_", "new_m_", "new_v_")

    def block(name, layer):
        return p[name][0 if layer is None else layer]

    def shards(blocks):
        return [block(n, layer).astype(BF16) for n, layer, _ in blocks]

    def place(w, gathered, blocks):
        for (n, layer, by_cols), g in zip(blocks, gathered):
            if by_cols:
                full = _join_cols(g, E_IN_PAD if n == 'e_w_in' else N_DEV * g.shape[2], f"join_{n}_{layer}")
            else:
                full = g.reshape(N_DEV * g.shape[1], g.shape[2])
            if layer is None:
                w[n] = full
            else:
                w[n][layer] = full

    first = [b for b in MATMUL_BLOCKS if b[0].startswith('e_')]
    with_ret = [b for b in MATMUL_BLOCKS if b[0].startswith('ffn_') and b[1] == 0]
    with_ssd = [b for b in MATMUL_BLOCKS if b not in first and b not in with_ret]
    gathered = _gather(shards(first) + [_pack([p[n] for n in SMALL_SHARDED], F32, 16)], "gather_first_weights")
    w = {n: p[n] for n in REPLICATED}
    w['ffn_w_up'], w['ffn_w_down'] = [None, None], [None, None]
    place(w, gathered, first)
    for n, piece in zip(SMALL_SHARDED, _unpack(gathered[-1], [p[n].shape for n in SMALL_SHARDED], lead=(N_DEV,))):
        w[n] = _join_shards(piece, SHARDED[n])
    late = {'ret_fwd': (shards(with_ret), functools.partial(place, blocks=with_ret)),
            'ssd_fwd': (shards(with_ssd), functools.partial(place, blocks=with_ssd))}

    def chip_sums(grads, blocks, extra, tag):
        sends = []
        for n, layer, by_cols in blocks:
            g = grads[n] if layer is None else grads[n][layer]
            if by_cols:
                sends.append(_split_cols(g, block(n, layer).shape[1], BF16, f"split_{n}_{layer}"))
            else:
                sends.append(g.astype(BF16).reshape(N_DEV, -1, g.shape[1]))
        core = lax.axis_index("c")
        by_core = [s.reshape((N_CHIPS, 2) + s.shape[1:]) for s in sends + extra]
        keep = [lax.dynamic_index_in_dim(s, core, axis=1, keepdims=False) for s in by_core]
        give = [lax.dynamic_index_in_dim(s, 1 - core, axis=1, keepdims=False) for s in by_core]
        got = _swap_sibling(give, "swap_sibling_grads_" + tag)
        return [_add(a, b, f"chip_sum_{tag}_{i}") for i, (a, b) in enumerate(zip(keep, got))]

    landed_late = []
    late['grads_ready'] = lambda grads: chip_sums(grads, with_ssd, [], "late")
    late['grads_landed'] = landed_late.extend

    loss, dx, grads = _local_step(x[0], loss_target[0], w, late)
    loss = lax.psum(loss, MESH_AXES)

    early = first + with_ret
    small_send = _pack([_split_shards(grads[n].reshape(p[n].shape[:SHARDED[n]] + (-1,) + p[n].shape[SHARDED[n] + 1:]), SHARDED[n])
                        for n in SMALL_SHARDED], F32, 128, lead=(N_DEV,))
    landed_early = list(_exchange_chips(chip_sums(grads, early, [small_send], "early"), "exchange_chip_grads"))
    part_of = {(n, layer): part for (n, layer, _), part in zip(early + with_ssd, landed_early[:-1] + landed_late)}
    parts = [landed_early[-1], _gather([_pack([grads[n].reshape(p[n].shape) for n in REPLICATED], F32, 128)], "gather_replicated_grads")[0]]

    out, by_layer = {}, {}
    for n, layer, _ in MATMUL_BLOCKS:
        by_layer.setdefault(n, {})[layer] = _adamw(part_of[(n, layer)], *[block(pre + n, layer) for pre in ("", "m_", "v_")],
                                                   f"adamw_{n}_{layer}")
    for n, res in by_layer.items():
        for i, kind in enumerate(kinds):
            out[kind + n] = res[None][i][None] if None in res else jnp.stack([res[0][i], res[1][i]], axis=0)
    for names, part, tag in ((SMALL_SHARDED, parts[-2], "small"), (REPLICATED, parts[-1], "replicated")):
        packed = [_pack([p[pre + n] for n in names], F32, 128) for pre in ("", "m_", "v_")]
        for kind, buf in zip(kinds, _adamw(part, *packed, "adamw_" + tag)):
            for n, a in zip(names, _unpack(buf, [p[n].shape for n in names])):
                out[kind + n] = a
    return (loss, dx[None], *[out[kind + n] for kind in kinds for n in WEIGHTS])
```

```python
import functools
import math

import jax
import jax.numpy as jnp
from jax import lax
from jax.experimental import pallas as pl
from jax.experimental.pallas import tpu as pltpu

F32, BF16 = jnp.float32, jnp.bfloat16
HIGHEST = lax.Precision.HIGHEST
N_DEV = 8
MESH_AXES = ("x", "y", "c")
VMEM_LIMIT = 48 * 1024 * 1024
LANES = 128
PACK_COLS = 1024

D_MODEL = 1024
EPS = 1e-6
RET_HEADS, RET_DK, RET_DV = 4, 128, 256
SSM_HEADS, SSM_P, SSM_N, SSM_GROUPS = 16, 64, 128, 2
SSM_HG = SSM_HEADS // SSM_GROUPS
S5_GROUPS, S5_GROUP, S5_STATE = 32, 16, 64
S5_COLS = S5_GROUPS * S5_STATE
D_FF = 2816
E_IN, E_IN_PAD = 5648, 5760
ADAM_LR, ADAM_B1, ADAM_B2, ADAM_EPS, ADAM_WD, ADAM_STEP = 0.001, 0.9, 0.999, 1e-08, 0.01, 10

WEIGHTS = ['mix_norm', 'e_w_in', 'e_conv_w', 'e_conv_b', 'e_dt_bias', 'e_a_log', 'e_d', 'e_ssm_norm', 'e_w_out', 'o_w_in', 'o_dw_w', 'o_dw_b', 'o_ln_g', 'o_ln_b', 'o_a_re', 'o_a_im', 'o_b_re', 'o_b_im', 'o_c_re', 'o_c_im', 'o_d', 'o_log_step', 'o_glu_w', 'o_w_out', 'ffn_norm', 'ffn_w_up', 'ffn_dw_w', 'ffn_dw_b', 'ffn_w_down', 'final_norm']
SHARDED = {'e_w_in': 2, 'e_conv_w': 2, 'e_w_out': 1, 'o_w_in': 2, 'o_dw_w': 2, 'o_dw_b': 1, 'o_ln_g': 1, 'o_ln_b': 1,
           'o_d': 1, 'o_glu_w': 1, 'o_w_out': 1, 'ffn_w_up': 2, 'ffn_dw_w': 2, 'ffn_w_down': 1}
MATMUL_WEIGHTS = ['e_w_in', 'e_w_out', 'o_w_in', 'o_glu_w', 'o_w_out', 'ffn_w_up', 'ffn_w_down']
MATMUL_BLOCKS = [('e_w_in', None, True), ('e_w_out', None, False), ('o_w_in', None, True), ('o_glu_w', None, False),
                 ('o_w_out', None, False), ('ffn_w_up', 0, True), ('ffn_w_up', 1, True), ('ffn_w_down', 0, False), ('ffn_w_down', 1, False)]
SMALL_SHARDED = [n for n in WEIGHTS if n in SHARDED and n not in MATMUL_WEIGHTS]
REPLICATED = [n for n in WEIGHTS if n not in SHARDED]


def _call(fn, grid, ins, outs, name):
    n_in = len(ins)

    def body(*refs):
        vals = fn(*[r[...] for r in refs[:n_in]])
        first = pl.program_id(0) == 0
        for r, v, o in zip(refs[n_in:], vals, outs):
            if o[4]:
                @pl.when(first)
                def _():
                    r[...] = jnp.zeros_like(r)
                r[...] += v.astype(r.dtype)
            else:
                r[...] = v.astype(r.dtype)

    return pl.pallas_call(
        body, grid=grid,
        in_specs=[pl.BlockSpec(b, m) for _, b, m in ins],
        out_specs=[pl.BlockSpec(o[2], o[3]) for o in outs],
        out_shape=[jax.ShapeDtypeStruct(o[0], o[1]) for o in outs],
        compiler_params=pltpu.CompilerParams(dimension_semantics=("arbitrary",) * len(grid), vmem_limit_bytes=VMEM_LIMIT),
        name=name)(*[a for a, _, _ in ins])


def _rows(fn, n_rows, row_ins, full_ins, row_outs, acc_outs, name, tm=256, bf16_outs=()):
    tm = min(tm, n_rows)
    ins = [(a, (tm, w), (lambda i, c=c: (i, c))) for a, w, c in row_ins]
    ins += [(a, a.shape, (lambda i, n=a.ndim: (0,) * n)) for a in full_ins]
    outs = [((n_rows, w), BF16 if k in bf16_outs else F32, (tm, w), (lambda i: (i, 0)), False) for k, w in enumerate(row_outs)]
    outs += [(tuple(s), F32, tuple(s), (lambda i, n=len(s): (0,) * n), True) for s in acc_outs]
    return _call(fn, (n_rows // tm,), ins, outs, name)


def _cols(fn, n_blocks, col_ins, out_leads, name, cb=LANES, bf16_outs=()):
    ins = [(a, a.shape[:-1] + (cb,), (lambda j, n=a.ndim, o=o: (0,) * (n - 1) + (j + o,))) for a, o in col_ins]
    outs = [(tuple(s) + (n_blocks * cb,), BF16 if k in bf16_outs else F32, tuple(s) + (cb,), (lambda j, n=len(s): (0,) * n + (j,)), False)
            for k, s in enumerate(out_leads)]
    return _call(fn, (n_blocks,), ins, outs, name)


def _grad_fn(f, n_diff, n_in):
    def g(*a):
        diff, consts, cts = a[:n_diff], a[n_diff:n_in], a[n_in:]
        _, vjp = jax.vjp(lambda *d: f(*d, *consts), *diff)
        return vjp(tuple(cts))
    return g


def _silu(x):
    return x * jax.nn.sigmoid(x)


def _rms(x, g):
    return x * lax.rsqrt(jnp.mean(x * x, axis=-1, keepdims=True) + EPS) * g


@jax.custom_vjp
def _softplus(x):
    return jnp.maximum(x, 0.0) + jnp.log(1.0 + jnp.exp(-jnp.abs(x)))


_softplus.defvjp(lambda x: (_softplus(x), x), lambda x, g: (g * jax.nn.sigmoid(x),))


@jax.custom_vjp
def _swap_halves(x):
    return pltpu.roll(x, 64, 1)


_swap_halves.defvjp(lambda x: (_swap_halves(x), None), lambda _, g: (_swap_halves(g),))


def _shift_rows(x, k, up):
    if k == 0:
        return x
    n = x.shape[0]
    t = lax.broadcasted_iota(jnp.int32, x.shape, 0)
    if up:
        return jnp.where(t < n - k, pltpu.roll(x, n - k, 0), 0.0)
    return jnp.where(t >= k, pltpu.roll(x, k, 0), 0.0)


@jax.custom_vjp
def _dwconv(x, w, b):
    k_taps = w.shape[0]
    y = b + w[k_taps - 1] * x
    for k in range(k_taps - 1):
        y = y + w[k] * _shift_rows(x, k_taps - 1 - k, False)
    return y


def _dwconv_fwd(x, w, b):
    return _dwconv(x, w, b), (x, w)


def _dwconv_bwd(saved, dy):
    x, w = saved
    k_taps = w.shape[0]
    dx = w[k_taps - 1] * dy
    dws = []
    for k in range(k_taps - 1):
        s = k_taps - 1 - k
        dx = dx + w[k] * _shift_rows(dy, s, True)
        dws.append(jnp.sum(dy * _shift_rows(x, s, False), axis=0, keepdims=True)[None])
    dws.append(jnp.sum(dy * x, axis=0, keepdims=True)[None])
    return dx, jnp.concatenate(dws, axis=0), jnp.sum(dy, axis=0, keepdims=True)


_dwconv.defvjp(_dwconv_fwd, _dwconv_bwd)


def _f_rms(x, g):
    return (_rms(x, g),)


def _rot(x, cosf, sins):
    outs = []
    for h in range(RET_HEADS):
        xh = x[:, h * RET_DK:(h + 1) * RET_DK]
        outs.append(xh * cosf + _swap_halves(xh) * sins)
    return jnp.concatenate(outs, axis=1)


def _f_retpre(q, k, cosf, sins):
    return _rot(q, cosf, sins), _rot(k, cosf, sins) * (RET_DK ** -0.5)


def _f_retpost(r, g):
    outs = []
    for h in range(RET_HEADS):
        rh = r[:, h * RET_DV:(h + 1) * RET_DV]
        rc = rh - jnp.mean(rh, axis=-1, keepdims=True)
        outs.append(_silu(g[:, h * RET_DV:(h + 1) * RET_DV]) * (rc * lax.rsqrt(jnp.mean(rc * rc, axis=-1, keepdims=True) + EPS)))
    return (jnp.concatenate(outs, axis=1),)


def _f_ssdconv(xbc, w, b):
    return (_silu(_dwconv(xbc, w, b)),)


def _f_ssdpre(xs, dtr, bias, alog, expand):
    dt = _softplus(dtr + bias)
    return xs * jnp.dot(dt, expand, precision=HIGHEST, preferred_element_type=F32), dt * (-jnp.exp(alog))


def _f_ssdpost(yc, xs, z, dskip, norm_w, expand):
    d_wide = jnp.dot(jnp.broadcast_to(dskip, (yc.shape[0], LANES)), expand, precision=HIGHEST, preferred_element_type=F32)
    y = (yc + d_wide * xs) * _silu(z)
    half = y.shape[1] // SSM_GROUPS
    outs = []
    for g in range(SSM_GROUPS):
        yg = y[:, g * half:(g + 1) * half]
        outs.append(yg * lax.rsqrt(jnp.mean(yg * yg, axis=-1, keepdims=True) + EPS))
    return (jnp.concatenate(outs, axis=1) * norm_w,)


def _f_ffnmid(gin, uin, wg, wu, bg, bu):
    return (_silu(_dwconv(gin, wg, bg)) * _dwconv(uin, wu, bu),)


def _f_confb(c1, g, b):
    mu = jnp.mean(c1, axis=-1, keepdims=True)
    xc = c1 - mu
    return (_silu(xc * lax.rsqrt(jnp.mean(xc * xc, axis=-1, keepdims=True) + EPS) * g + b),)


def _f_s5post(yc, u, dskip, glu_w):
    s = jax.nn.gelu(yc + dskip * u)
    z = jnp.dot(s.astype(BF16), glu_w.astype(BF16), preferred_element_type=F32)
    return (s * jax.nn.sigmoid(z),)


def _f_s5par(lr, li, ls, br, bi):
    step = jnp.exp(ls)
    mag = jnp.exp(lr * step)
    ab_re = mag * jnp.cos(li * step)
    ab_im = mag * jnp.sin(li * step)
    den = lr * lr + li * li
    f_re = ((ab_re - 1.0) * lr + ab_im * li) / den
    f_im = (ab_im * lr - (ab_re - 1.0) * li) / den
    return ab_re, ab_im, f_re[None] * br - f_im[None] * bi, f_re[None] * bi + f_im[None] * br


def _loss_step(x, tgt, g):
    def f(x_, g_):
        e = _rms(x_, g_) - tgt
        return 0.5 * jnp.sum(jnp.mean(e * e, axis=-1, keepdims=True), axis=0, keepdims=True)
    loss, vjp = jax.vjp(f, x, g)
    dx, dg = vjp(jnp.ones((1, 1), F32))
    return dx, dx, jnp.broadcast_to(loss, (8, LANES)), dg


def _tile(n, pref):
    if n <= pref:
        return n
    t = (pref // LANES) * LANES
    while n % t:
        t -= LANES
    return t


_DIMS = {"nn": (((1,), (0,)), ((), ())), "nt": (((1,), (1,)), ((), ())), "tn": (((0,), (0,)), ((), ()))}


def _mm(a, b, mode, name, res=None, out_dtype=F32, tm=1024, tn=1408, tk=1408):
    if mode == "nn":
        (m, k), n = a.shape, b.shape[1]
    elif mode == "nt":
        (m, k), n = a.shape, b.shape[0]
    else:
        (k, m), n = a.shape, b.shape[1]
    tm, tn, tk = _tile(m, tm), _tile(n, tn), _tile(k, tk)
    nk = k // tk
    a_spec = pl.BlockSpec((tk, tm), lambda i, j, kk: (kk, i)) if mode == "tn" else pl.BlockSpec((tm, tk), lambda i, j, kk: (i, kk))
    b_spec = pl.BlockSpec((tn, tk), lambda i, j, kk: (j, kk)) if mode == "nt" else pl.BlockSpec((tk, tn), lambda i, j, kk: (kk, j))
    o_spec = pl.BlockSpec((tm, tn), lambda i, j, kk: (i, j))
    has_res = res is not None

    def body(*refs):
        a_ref, b_ref = refs[0], refs[1]
        o_ref, acc = refs[-2], refs[-1]
        kk = pl.program_id(2)

        @pl.when(kk == 0)
        def _():
            acc[...] = jnp.zeros_like(acc)

        acc[...] += lax.dot_general(a_ref[...].astype(BF16), b_ref[...].astype(BF16), _DIMS[mode], preferred_element_type=F32)

        @pl.when(kk == nk - 1)
        def _():
            o_ref[...] = (acc[...] + refs[2][...] if has_res else acc[...]).astype(out_dtype)

    return pl.pallas_call(
        body, grid=(m // tm, n // tn, nk),
        in_specs=[a_spec, b_spec] + ([o_spec] if has_res else []),
        out_specs=o_spec, out_shape=jax.ShapeDtypeStruct((m, n), out_dtype),
        scratch_shapes=[pltpu.VMEM((tm, tn), F32)],
        compiler_params=pltpu.CompilerParams(dimension_semantics=("parallel", "parallel", "arbitrary"), vmem_limit_bytes=VMEM_LIMIT),
        name=name)(*([a, b] + ([res] if has_res else [])))


def _seq_block(t):
    return min(512, t)


def _causal_diff(i, j, blk):
    r = lax.broadcasted_iota(jnp.int32, (blk, blk), 0)
    c = lax.broadcasted_iota(jnp.int32, (blk, blk), 1)
    return (i - j) * blk + r - c


def _ret_decay(lg, i, j, blk):
    diff = _causal_diff(i, j, blk)
    return jnp.where(diff >= 0, jnp.exp(lg * jnp.maximum(diff, 0).astype(F32)), 0.0)


def _pair_call(body, lead_grid, nb, key_major, in_specs, out_specs, out_shape, scratch, name, args, gather=(), exchange=()):
    pairs = [(i, j) for j in range(nb) for i in range(j, nb)] if key_major else [(i, j) for i in range(nb) for j in range(i + 1)]
    tables = [jnp.array([p[k] for p in pairs], jnp.int32) for k in (0, 1)]
    lead = len(lead_grid)
    n_steps = math.prod(lead_grid) * len(pairs)
    behind = list(gather) or list(exchange)
    n_bg = len(behind)
    phases, per_array = (_gather_phases, GATHER_SEMS) if gather else (_exchange_phases, EXCHANGE_SEMS)

    def on_pairs(spec):
        if spec.block_shape is None:
            return spec

        def index_map(*a):
            i, j = a[lead + 1][a[lead]], a[lead + 2][a[lead]]
            return spec.index_map(*a[:lead], *((j, i) if key_major else (i, j)))

        return pl.BlockSpec(spec.block_shape, index_map)

    many = isinstance(out_specs, (list, tuple))
    out_specs, out_shape = (list(out_specs), list(out_shape)) if many else ([out_specs], [out_shape])
    n_in, n_out = len(in_specs), len(out_specs)

    def wrapped(i_ref, j_ref, *refs):
        p = pl.program_id(lead)
        own = refs[:n_in] + refs[n_in + n_bg:n_in + n_bg + n_out] + refs[n_in + 2 * n_bg + n_out:len(refs) - (3 if n_bg else 0)]
        if n_bg:
            step = p + (pl.program_id(0) * len(pairs) if lead else 0)
            start, forward, finish = phases(refs[n_in:n_in + n_bg], refs[n_in + n_bg + n_out:n_in + 2 * n_bg + n_out], *refs[-3:])
            pl.when(step == 0)(start)
        body(i_ref[p], j_ref[p], *own)
        if n_bg:
            pl.when(step == n_steps * 3 // 4)(forward)
            pl.when(step == n_steps - 1)(finish)

    sems = [pltpu.SemaphoreType.DMA((per_array * n_bg,))] * 2 + [pltpu.SemaphoreType.DMA((n_bg,))] if n_bg else []
    landed = [jax.ShapeDtypeStruct(((N_DEV,) + g.shape) if gather else g.shape, g.dtype) for g in behind]
    grid_spec = pltpu.PrefetchScalarGridSpec(
        num_scalar_prefetch=2, grid=tuple(lead_grid) + (len(pairs),), in_specs=[on_pairs(s) for s in in_specs] + [_HBM] * n_bg,
        out_specs=[on_pairs(s) for s in out_specs] + [_HBM] * n_bg, scratch_shapes=list(scratch) + sems)
    res = pl.pallas_call(
        wrapped, grid_spec=grid_spec, out_shape=out_shape + landed,
        compiler_params=pltpu.CompilerParams(dimension_semantics=("arbitrary",) * (lead + 1), vmem_limit_bytes=VMEM_LIMIT),
        name=name)(*tables, *args, *behind)
    return res if many or n_bg else res[0]


def _ret_row_decays(lg, i, j, blk):
    row = lax.broadcasted_iota(jnp.int32, (blk, RET_DK), 0)
    return jnp.exp(lg * row.astype(F32)), jnp.exp(lg * ((i - j) * blk - row).astype(F32))


def _ret_scaled(lg, i, j, blk, q_ref, k_ref):
    a, b = _ret_row_decays(lg, i, j, blk)
    return (q_ref[...] * a).astype(BF16), (k_ref[...] * b).astype(BF16)


def _ret_log_gamma():
    return jnp.log1p(-(2.0 ** (-5.0 - jnp.arange(RET_HEADS, dtype=F32))))


def _ret_fwd(qr, kr, proj, t, gather=()):
    blk = _seq_block(t)
    nb = t // blk
    v_off = (2 * RET_HEADS * RET_DK) // RET_DV

    def body(i, j, lg_ref, q_ref, k_ref, v_ref, o_ref, acc):
        h = pl.program_id(0)

        @pl.when(j == 0)
        def _():
            acc[...] = jnp.zeros_like(acc)

        @pl.when(j < i)
        def _():
            qa, kb = _ret_scaled(lg_ref[h], i, j, blk, q_ref, k_ref)
            p = lax.dot_general(qa, kb, _DIMS["nt"], preferred_element_type=F32).astype(BF16)
            acc[...] += jnp.dot(p, v_ref[...].astype(BF16), preferred_element_type=F32)

        @pl.when(j == i)
        def _():
            s = lax.dot_general(q_ref[...].astype(BF16), k_ref[...].astype(BF16), _DIMS["nt"], preferred_element_type=F32)
            p = (s * _ret_decay(lg_ref[h], i, j, blk)).astype(BF16)
            o_ref[...] = acc[...] + jnp.dot(p, v_ref[...].astype(BF16), preferred_element_type=F32)

    return _pair_call(
        body, (RET_HEADS,), nb, False,
        [pl.BlockSpec(memory_space=pltpu.SMEM),
         pl.BlockSpec((blk, RET_DK), lambda h, i, j: (i, h)),
         pl.BlockSpec((blk, RET_DK), lambda h, i, j: (j, h)),
         pl.BlockSpec((blk, RET_DV), lambda h, i, j: (j, v_off + h))],
        pl.BlockSpec((blk, RET_DV), lambda h, i, j: (i, h)),
        jax.ShapeDtypeStruct((t, RET_HEADS * RET_DV), F32), [pltpu.VMEM((blk, RET_DV), F32)],
        "ret_fwd", (_ret_log_gamma(), qr, kr, proj), gather)


def _ret_bwd(qr, kr, proj, dr, t, exchange=()):
    blk = _seq_block(t)
    nb = t // blk
    v_off = (2 * RET_HEADS * RET_DK) // RET_DV

    def dq_body(i, j, lg_ref, q_ref, k_ref, v_ref, do_ref, dq_ref, acc):
        h = pl.program_id(0)

        @pl.when(j == 0)
        def _():
            acc[...] = jnp.zeros_like(acc)

        @pl.when(j < i)
        def _():
            a, b = _ret_row_decays(lg_ref[h], i, j, blk)
            ds = lax.dot_general(do_ref[...].astype(BF16), v_ref[...].astype(BF16), _DIMS["nt"], preferred_element_type=F32)
            acc[...] += a * jnp.dot(ds.astype(BF16), (k_ref[...] * b).astype(BF16), preferred_element_type=F32)

        @pl.when(j == i)
        def _():
            ds = lax.dot_general(do_ref[...].astype(BF16), v_ref[...].astype(BF16), _DIMS["nt"], preferred_element_type=F32)
            dsm = (ds * _ret_decay(lg_ref[h], i, j, blk)).astype(BF16)
            dq_ref[...] = acc[...] + jnp.dot(dsm, k_ref[...].astype(BF16), preferred_element_type=F32)

    dq = _pair_call(
        dq_body, (RET_HEADS,), nb, False,
        [pl.BlockSpec(memory_space=pltpu.SMEM),
         pl.BlockSpec((blk, RET_DK), lambda h, i, j: (i, h)),
         pl.BlockSpec((blk, RET_DK), lambda h, i, j: (j, h)),
         pl.BlockSpec((blk, RET_DV), lambda h, i, j: (j, v_off + h)),
         pl.BlockSpec((blk, RET_DV), lambda h, i, j: (i, h))],
        pl.BlockSpec((blk, RET_DK), lambda h, i, j: (i, h)),
        jax.ShapeDtypeStruct((t, RET_HEADS * RET_DK), F32), [pltpu.VMEM((blk, RET_DK), F32)],
        "ret_bwd_dq", (_ret_log_gamma(), qr, kr, proj, dr))

    def dkv_body(i, j, lg_ref, q_ref, k_ref, v_ref, do_ref, dk_ref, dv_ref, acc_k, acc_v):
        h = pl.program_id(0)

        @pl.when(i == j)
        def _():
            acc_k[...] = jnp.zeros_like(acc_k)
            acc_v[...] = jnp.zeros_like(acc_v)

        @pl.when(i > j)
        def _():
            a, b = _ret_row_decays(lg_ref[h], i, j, blk)
            qa, kb = (q_ref[...] * a).astype(BF16), (k_ref[...] * b).astype(BF16)
            do = do_ref[...].astype(BF16)
            p = lax.dot_general(qa, kb, _DIMS["nt"], preferred_element_type=F32).astype(BF16)
            acc_v[...] += lax.dot_general(p, do, _DIMS["tn"], preferred_element_type=F32)
            ds = lax.dot_general(do, v_ref[...].astype(BF16), _DIMS["nt"], preferred_element_type=F32).astype(BF16)
            acc_k[...] += b * lax.dot_general(ds, qa, _DIMS["tn"], preferred_element_type=F32)

        @pl.when(i == j)
        def _():
            q = q_ref[...].astype(BF16)
            do = do_ref[...].astype(BF16)
            decay = _ret_decay(lg_ref[h], i, j, blk)
            s = lax.dot_general(q, k_ref[...].astype(BF16), _DIMS["nt"], preferred_element_type=F32)
            acc_v[...] += lax.dot_general((s * decay).astype(BF16), do, _DIMS["tn"], preferred_element_type=F32)
            ds = lax.dot_general(do, v_ref[...].astype(BF16), _DIMS["nt"], preferred_element_type=F32)
            acc_k[...] += lax.dot_general((ds * decay).astype(BF16), q, _DIMS["tn"], preferred_element_type=F32)

        @pl.when(i == nb - 1)
        def _():
            dk_ref[...] = acc_k[...]
            dv_ref[...] = acc_v[...].astype(BF16)

    dk, dv, *landed = _pair_call(
        dkv_body, (RET_HEADS,), nb, True,
        [pl.BlockSpec(memory_space=pltpu.SMEM),
         pl.BlockSpec((blk, RET_DK), lambda h, j, i: (i, h)),
         pl.BlockSpec((blk, RET_DK), lambda h, j, i: (j, h)),
         pl.BlockSpec((blk, RET_DV), lambda h, j, i: (j, v_off + h)),
         pl.BlockSpec((blk, RET_DV), lambda h, j, i: (i, h))],
        [pl.BlockSpec((blk, RET_DK), lambda h, j, i: (j, h)), pl.BlockSpec((blk, RET_DV), lambda h, j, i: (j, h))],
        [jax.ShapeDtypeStruct((t, RET_HEADS * RET_DK), F32), jax.ShapeDtypeStruct((t, RET_HEADS * RET_DV), BF16)],
        [pltpu.VMEM((blk, RET_DK), F32), pltpu.VMEM((blk, RET_DV), F32)],
        "ret_bwd_dkv", (_ret_log_gamma(), qr, kr, proj, dr), exchange=exchange)
    return dq, dk, dv, landed


def _cumsum(x, reverse, name):
    t = x.shape[0]
    blk = _seq_block(t)
    nb = t // blk

    def body(x_ref, o_ref, carry):
        @pl.when(pl.program_id(0) == 0)
        def _():
            carry[...] = jnp.zeros_like(carry)

        r = lax.broadcasted_iota(jnp.int32, (blk, blk), 0)
        c = lax.broadcasted_iota(jnp.int32, (blk, blk), 1)
        tri = ((r <= c) if reverse else (r >= c)).astype(F32)
        o_ref[...] = jnp.dot(tri, x_ref[...], precision=HIGHEST, preferred_element_type=F32) + carry[...]
        carry[...] = o_ref[0:1, :] if reverse else o_ref[blk - 1:blk, :]

    idx = (lambda i: (nb - 1 - i, 0)) if reverse else (lambda i: (i, 0))
    return pl.pallas_call(
        body, grid=(nb,), in_specs=[pl.BlockSpec((blk, LANES), idx)], out_specs=pl.BlockSpec((blk, LANES), idx),
        out_shape=jax.ShapeDtypeStruct((t, LANES), F32), scratch_shapes=[pltpu.VMEM((1, LANES), F32)],
        compiler_params=pltpu.CompilerParams(dimension_semantics=("arbitrary",), vmem_limit_bytes=VMEM_LIMIT),
        name=name)(x)


def _ssd_decay(a_ref, at_ref, hh, mask):
    return jnp.exp(jnp.where(mask, a_ref[:, hh:hh + 1] - at_ref[hh:hh + 1, :], -jnp.inf))


def _head_lanes(s):
    lane = lax.broadcasted_iota(jnp.int32, (s.shape[0], LANES), 1)
    return jnp.concatenate([jnp.where(lane < SSM_P, s[:, 2 * p:2 * p + 1], s[:, 2 * p + 1:2 * p + 2])
                            for p in range(SSM_HEADS // 2)], axis=1)


_B_OFF, _C_OFF = 1024 // SSM_N, 1024 // SSM_N + SSM_GROUPS


def _ssd_fwd(xbc_act, xdt, a_cum, a_cum_t, t, gather=()):
    blk = _seq_block(t)
    nb = t // blk

    def body(i, j, c0, c1, b0, b1, x_ref, ai_ref, aj_ref, at_ref, o_ref, acc):
        @pl.when(j == 0)
        def _():
            acc[...] = jnp.zeros_like(acc)

        @pl.when(j < i)
        def _():
            xv = (x_ref[...] * _head_lanes(jnp.exp(ai_ref[0:1, :] - aj_ref[...]))).astype(BF16)
            for g, (c_ref, b_ref) in enumerate(((c0, b0), (c1, b1))):
                cb = lax.dot_general(c_ref[...].astype(BF16), b_ref[...].astype(BF16), _DIMS["nt"], preferred_element_type=F32)
                cols = slice(g * half, (g + 1) * half)
                acc[:, cols] += jnp.dot(cb.astype(BF16), xv[:, cols], preferred_element_type=F32)

        @pl.when(j == i)
        def _():
            o_ref[...] = acc[...] * _head_lanes(jnp.exp(ai_ref[...] - ai_ref[0:1, :]))
            mask = _causal_diff(i, j, blk) >= 0
            for g, (c_ref, b_ref) in enumerate(((c0, b0), (c1, b1))):
                cb = lax.dot_general(c_ref[...].astype(BF16), b_ref[...].astype(BF16), _DIMS["nt"], preferred_element_type=F32)
                for h in range(SSM_HG):
                    hh = g * SSM_HG + h
                    cols = slice(hh * SSM_P, (hh + 1) * SSM_P)
                    m = (cb * _ssd_decay(ai_ref, at_ref, hh, mask)).astype(BF16)
                    o_ref[:, cols] += jnp.dot(m, x_ref[:, cols].astype(BF16), preferred_element_type=F32)

    half = SSM_HG * SSM_P
    row_i = lambda off: pl.BlockSpec((blk, SSM_N), lambda i, j, off=off: (i, off))
    row_j = lambda off: pl.BlockSpec((blk, SSM_N), lambda i, j, off=off: (jnp.minimum(j, i), off))
    return _pair_call(
        body, (), nb, False,
        [row_i(_C_OFF), row_i(_C_OFF + 1), row_j(_B_OFF), row_j(_B_OFF + 1),
         pl.BlockSpec((blk, SSM_HEADS * SSM_P), lambda i, j: (j, 0)),
         pl.BlockSpec((blk, LANES), lambda i, j: (i, 0)),
         pl.BlockSpec((blk, LANES), lambda i, j: (j, 0)),
         pl.BlockSpec((SSM_HEADS, blk), lambda i, j: (0, j))],
        pl.BlockSpec((blk, SSM_HEADS * SSM_P), lambda i, j: (i, 0)),
        jax.ShapeDtypeStruct((t, SSM_HEADS * SSM_P), F32), [pltpu.VMEM((blk, SSM_HEADS * SSM_P), F32)],
        "ssd_fwd", (xbc_act, xbc_act, xbc_act, xbc_act, xdt, a_cum, a_cum, a_cum_t), gather)


def _ssd_bwd(xbc_act, xdt, a_cum, a_cum_t, dy, expand, t, exchange=()):
    blk = _seq_block(t)
    nb = t // blk
    width = SSM_HEADS * SSM_P

    half = SSM_HG * SSM_P

    def head_sums(prod, e_ref):
        return lax.dot_general(prod, e_ref[...], _DIMS["nt"], precision=HIGHEST, preferred_element_type=F32)

    def q_body(i, j, c0, c1, b0, b1, x_ref, ai_ref, aj_ref, at_ref, dy_ref, e_ref, dc_ref, da_ref, acc_c, acc_a, acc_p, dyu):
        @pl.when(j == 0)
        def _():
            acc_c[...] = jnp.zeros_like(acc_c)
            acc_a[...] = jnp.zeros_like(acc_a)
            acc_p[...] = jnp.zeros_like(acc_p)
            dyu[...] = (dy_ref[...].astype(F32) * _head_lanes(jnp.exp(ai_ref[...] - ai_ref[0:1, :]))).astype(BF16)

        @pl.when(j < i)
        def _():
            xv = (x_ref[...] * _head_lanes(jnp.exp(ai_ref[0:1, :] - aj_ref[...]))).astype(BF16)
            for g, (c_ref, b_ref) in enumerate(((c0, b0), (c1, b1))):
                cols = slice(g * half, (g + 1) * half)
                bj = b_ref[...].astype(BF16)
                cb = lax.dot_general(c_ref[...].astype(BF16), bj, _DIMS["nt"], preferred_element_type=F32).astype(BF16)
                dcb = lax.dot_general(dyu[:, cols], xv[:, cols], _DIMS["nt"], preferred_element_type=F32)
                acc_c[:, g * SSM_N:(g + 1) * SSM_N] += jnp.dot(dcb.astype(BF16), bj, preferred_element_type=F32)
                acc_p[:, cols] += dyu[:, cols].astype(F32) * jnp.dot(cb, xv[:, cols], preferred_element_type=F32)

        @pl.when(j == i)
        def _():
            mask = _causal_diff(i, j, blk) >= 0
            for g, (c_ref, b_ref) in enumerate(((c0, b0), (c1, b1))):
                bj = b_ref[...].astype(BF16)
                cb = lax.dot_general(c_ref[...].astype(BF16), bj, _DIMS["nt"], preferred_element_type=F32)
                dcb = jnp.zeros((blk, blk), F32)
                for h in range(SSM_HG):
                    hh = g * SSM_HG + h
                    cols = slice(hh * SSM_P, (hh + 1) * SSM_P)
                    dm = lax.dot_general(dy_ref[:, cols].astype(BF16), x_ref[:, cols].astype(BF16), _DIMS["nt"],
                                         preferred_element_type=F32) * _ssd_decay(ai_ref, at_ref, hh, mask)
                    dcb = dcb + dm
                    acc_a[:, hh:hh + 1] += jnp.sum(dm * cb, axis=1, keepdims=True)
                acc_c[:, g * SSM_N:(g + 1) * SSM_N] += jnp.dot(dcb.astype(BF16), bj, preferred_element_type=F32)
            dc_ref[...] = acc_c[...]
            da_ref[...] = acc_a[...] + head_sums(acc_p[...], e_ref)

    row_i = lambda off: pl.BlockSpec((blk, SSM_N), lambda i, j, off=off: (i, off))
    row_j = lambda off: pl.BlockSpec((blk, SSM_N), lambda i, j, off=off: (jnp.minimum(j, i), off))
    e_spec = pl.BlockSpec((LANES, width), lambda i, j: (0, 0))
    dc, da_q = _pair_call(
        q_body, (), nb, False,
        [row_i(_C_OFF), row_i(_C_OFF + 1), row_j(_B_OFF), row_j(_B_OFF + 1),
         pl.BlockSpec((blk, width), lambda i, j: (j, 0)),
         pl.BlockSpec((blk, LANES), lambda i, j: (i, 0)),
         pl.BlockSpec((blk, LANES), lambda i, j: (j, 0)),
         pl.BlockSpec((SSM_HEADS, blk), lambda i, j: (0, j)),
         pl.BlockSpec((blk, width), lambda i, j: (i, 0)), e_spec],
        [pl.BlockSpec((blk, SSM_GROUPS * SSM_N), lambda i, j: (i, 0)), pl.BlockSpec((blk, LANES), lambda i, j: (i, 0))],
        [jax.ShapeDtypeStruct((t, SSM_GROUPS * SSM_N), F32), jax.ShapeDtypeStruct((t, LANES), F32)],
        [pltpu.VMEM((blk, SSM_GROUPS * SSM_N), F32), pltpu.VMEM((blk, LANES), F32), pltpu.VMEM((blk, width), F32),
         pltpu.VMEM((blk, width), BF16)],
        "ssd_bwd_q", (xbc_act, xbc_act, xbc_act, xbc_act, xdt, a_cum, a_cum, a_cum_t, dy, expand))

    def k_body(i, j, c0, c1, b0, b1, x_ref, ai_ref, aj_ref, at_ref, dy_ref, e_ref, db_ref, dx_ref, da_ref, dat_ref, acc_b, acc_x, acc_a,
               acc_p):
        @pl.when(i == j)
        def _():
            acc_b[...] = jnp.zeros_like(acc_b)
            acc_x[...] = jnp.zeros_like(acc_x)
            acc_a[...] = jnp.zeros_like(acc_a)
            acc_p[...] = jnp.zeros_like(acc_p)

        @pl.when(i > j)
        def _():
            v = _head_lanes(jnp.exp(ai_ref[0:1, :] - aj_ref[...]))
            dyu_all = (dy_ref[...].astype(F32) * _head_lanes(jnp.exp(ai_ref[...] - ai_ref[0:1, :]))).astype(BF16)
            xv = (x_ref[...] * v).astype(BF16)
            for g, (c_ref, b_ref) in enumerate(((c0, b0), (c1, b1))):
                cols = slice(g * half, (g + 1) * half)
                ci = c_ref[...].astype(BF16)
                cb = lax.dot_general(ci, b_ref[...].astype(BF16), _DIMS["nt"], preferred_element_type=F32).astype(BF16)
                dcb = lax.dot_general(dyu_all[:, cols], xv[:, cols], _DIMS["nt"], preferred_element_type=F32)
                acc_b[:, g * SSM_N:(g + 1) * SSM_N] += lax.dot_general(dcb.astype(BF16), ci, _DIMS["tn"], preferred_element_type=F32)
                dxv = lax.dot_general(cb, dyu_all[:, cols], _DIMS["tn"], preferred_element_type=F32)
                acc_x[:, cols] += v[:, cols] * dxv
                acc_p[:, cols] += xv[:, cols].astype(F32) * dxv

        @pl.when(i == j)
        def _():
            mask = _causal_diff(i, j, blk) >= 0
            for g, (c_ref, b_ref) in enumerate(((c0, b0), (c1, b1))):
                ci = c_ref[...].astype(BF16)
                cb = lax.dot_general(ci, b_ref[...].astype(BF16), _DIMS["nt"], preferred_element_type=F32)
                dcb = jnp.zeros((blk, blk), F32)
                for h in range(SSM_HG):
                    hh = g * SSM_HG + h
                    cols = slice(hh * SSM_P, (hh + 1) * SSM_P)
                    decay = _ssd_decay(ai_ref, at_ref, hh, mask)
                    dyh = dy_ref[:, cols].astype(BF16)
                    acc_x[:, cols] += lax.dot_general((cb * decay).astype(BF16), dyh, _DIMS["tn"], preferred_element_type=F32)
                    dm = lax.dot_general(dyh, x_ref[:, cols].astype(BF16), _DIMS["nt"], preferred_element_type=F32) * decay
                    dcb = dcb + dm
                    acc_a[hh:hh + 1, :] += jnp.sum(dm * cb, axis=0, keepdims=True)
                acc_b[:, g * SSM_N:(g + 1) * SSM_N] += lax.dot_general(dcb.astype(BF16), ci, _DIMS["tn"], preferred_element_type=F32)

        @pl.when(i == nb - 1)
        def _():
            db_ref[...] = acc_b[...]
            dx_ref[...] = acc_x[...]
            dat_ref[...] = acc_a[...]
            da_ref[...] = head_sums(acc_p[...], e_ref)

    rowk_i = lambda off: pl.BlockSpec((blk, SSM_N), lambda j, i, off=off: (jnp.maximum(i, j), off))
    rowk_j = lambda off: pl.BlockSpec((blk, SSM_N), lambda j, i, off=off: (j, off))
    db, dx, da_k, da_k_t, *landed = _pair_call(
        k_body, (), nb, True,
        [rowk_i(_C_OFF), rowk_i(_C_OFF + 1), rowk_j(_B_OFF), rowk_j(_B_OFF + 1),
         pl.BlockSpec((blk, width), lambda j, i: (j, 0)),
         pl.BlockSpec((blk, LANES), lambda j, i: (i, 0)),
         pl.BlockSpec((blk, LANES), lambda j, i: (j, 0)),
         pl.BlockSpec((SSM_HEADS, blk), lambda j, i: (0, j)),
         pl.BlockSpec((blk, width), lambda j, i: (i, 0)), e_spec],
        [pl.BlockSpec((blk, SSM_GROUPS * SSM_N), lambda j, i: (j, 0)), pl.BlockSpec((blk, width), lambda j, i: (j, 0)),
         pl.BlockSpec((blk, LANES), lambda j, i: (j, 0)), pl.BlockSpec((SSM_HEADS, blk), lambda j, i: (0, j))],
        [jax.ShapeDtypeStruct((t, SSM_GROUPS * SSM_N), F32), jax.ShapeDtypeStruct((t, width), F32),
         jax.ShapeDtypeStruct((t, LANES), F32), jax.ShapeDtypeStruct((SSM_HEADS, t), F32)],
        [pltpu.VMEM((blk, SSM_GROUPS * SSM_N), F32), pltpu.VMEM((blk, width), F32), pltpu.VMEM((SSM_HEADS, blk), F32),
         pltpu.VMEM((blk, width), F32)],
        "ssd_bwd_k", (xbc_act, xbc_act, xbc_act, xbc_act, xdt, a_cum, a_cum, a_cum_t, dy, expand), exchange=exchange)
    return dc, da_q, db, dx, da_k, da_k_t, landed


S5_SUPER = 4
S5_NARROW, S5_WIDE = S5_GROUPS * S5_GROUP // S5_SUPER, S5_GROUPS * S5_STATE // S5_SUPER


def _s5_tiles(t):
    tm = min(1024, t)
    narrow = lambda off: pl.BlockSpec((tm, S5_NARROW), lambda s, i, off=off: (i, s + off))
    wide = pl.BlockSpec((tm, S5_WIDE), lambda s, i: (i, s))
    weight = lambda shape: pl.BlockSpec((1,) + shape, lambda s, i: (s, 0, 0))
    params = pltpu.CompilerParams(dimension_semantics=("arbitrary", "arbitrary"), vmem_limit_bytes=VMEM_LIMIT)
    return tm, narrow, wide, weight, params


def _s5_expand(a, a_off, w, w_is_wide_first, name):
    t = a.shape[0]
    tm, narrow, wide, weight, params = _s5_tiles(t)

    def body(a_ref, w_ref, re_ref, im_ref):
        dims = _DIMS["nt"] if w_is_wide_first else _DIMS["nn"]
        o = lax.dot_general(a_ref[...].astype(BF16), w_ref[0].astype(BF16), dims, preferred_element_type=F32)
        re_ref[...] = o[:, :S5_WIDE]
        im_ref[...] = o[:, S5_WIDE:]

    return pl.pallas_call(
        body, grid=(S5_SUPER, t // tm), in_specs=[narrow(a_off), weight(w.shape[1:])], out_specs=[wide, wide],
        out_shape=[jax.ShapeDtypeStruct((t, S5_COLS), F32)] * 2, compiler_params=params, name=name)(a, w)


def _s5_contract(x_re, x_im, w, w_is_wide_first, name, res=None, out_dtype=F32):
    t = x_re.shape[0]
    tm, narrow, wide, weight, params = _s5_tiles(t)
    has_res = res is not None

    def body(*refs):
        re_ref, im_ref, w_ref, o_ref = refs[0], refs[1], refs[2], refs[-1]
        x = jnp.concatenate([re_ref[...].astype(BF16), im_ref[...].astype(BF16)], axis=1)
        dims = _DIMS["nn"] if w_is_wide_first else _DIMS["nt"]
        o = lax.dot_general(x, w_ref[0].astype(BF16), dims, preferred_element_type=F32)
        o_ref[...] = (o + refs[3][...] if has_res else o).astype(out_dtype)

    return pl.pallas_call(
        body, grid=(S5_SUPER, t // tm), in_specs=[wide, wide, weight(w.shape[1:])] + ([narrow(0)] if has_res else []),
        out_specs=narrow(0), out_shape=jax.ShapeDtypeStruct((t, S5_SUPER * S5_NARROW), out_dtype), compiler_params=params,
        name=name)(*([x_re, x_im, w] + ([res] if has_res else [])))


def _s5_wgrad(a, a_off, x_re, x_im, wide_first, name):
    t = a.shape[0]
    tm, narrow, wide, weight, params = _s5_tiles(t)
    shape = (2 * S5_WIDE, S5_NARROW) if wide_first else (S5_NARROW, 2 * S5_WIDE)

    def body(a_ref, re_ref, im_ref, o_ref):
        @pl.when(pl.program_id(1) == 0)
        def _():
            o_ref[...] = jnp.zeros_like(o_ref)

        x = jnp.concatenate([re_ref[...].astype(BF16), im_ref[...].astype(BF16)], axis=1)
        av = a_ref[...].astype(BF16)
        o_ref[0] += lax.dot_general(x, av, _DIMS["tn"], preferred_element_type=F32) if wide_first else \
            lax.dot_general(av, x, _DIMS["tn"], preferred_element_type=F32)

    return pl.pallas_call(
        body, grid=(S5_SUPER, t // tm), in_specs=[narrow(a_off), wide, wide], out_specs=weight(shape),
        out_shape=jax.ShapeDtypeStruct((S5_SUPER,) + shape, F32), compiler_params=params, name=name)(a, x_re, x_im)


def _s5_scan(b_re, b_im, a_re, a_im, reverse, x_prev=None):
    t = b_re.shape[0]
    cb = 512
    ncb = S5_COLS // cb
    tb = min(1024, t)
    ntb = t // tb
    sub = 8
    shape = (sub, cb)

    def cmul(ar, ai, br, bi):
        return ar * br - ai * bi, ar * bi + ai * br

    def body(*refs):
        if reverse:
            br_ref, bi_ref, ar_ref, ai_ref, xr_ref, xi_ref, or_ref, oi_ref, dar_ref, dai_ref, carry, dacc, states = refs
        else:
            br_ref, bi_ref, ar_ref, ai_ref, or_ref, oi_ref, or_bf_ref, oi_bf_ref, carry = refs

        @pl.when(pl.program_id(1) == 0)
        def _():
            carry[...] = jnp.zeros_like(carry)
            if reverse:
                dacc[...] = jnp.zeros_like(dacc)

        row = lax.broadcasted_iota(jnp.int32, shape, 0)
        a1 = (jnp.broadcast_to(ar_ref[...], shape), jnp.broadcast_to(-ai_ref[...] if reverse else ai_ref[...], shape))
        a2 = cmul(*a1, *a1)
        a4 = cmul(*a2, *a2)
        a8 = cmul(*a4, *a4)
        steps = (sub - row) if reverse else (row + 1)
        pw = (jnp.ones(shape, F32), jnp.zeros(shape, F32))
        for bit, p in ((1, a1), (2, a2), (4, a4), (8, a8)):
            on = (steps & bit) != 0
            pw = cmul(*pw, jnp.where(on, p[0], 1.0), jnp.where(on, p[1], 0.0))
        edge = 0 if reverse else sub - 1

        def shift(v, s):
            if reverse:
                return jnp.where(row < sub - s, pltpu.roll(v, sub - s, 0), 0.0)
            return jnp.where(row >= s, pltpu.roll(v, s, 0), 0.0)

        def tile(n, state):
            r0 = pl.multiple_of((tb // sub - 1 - n if reverse else n) * sub, sub)
            rows = pl.ds(r0, sub)
            xr, xi = br_ref[rows, :], bi_ref[rows, :]
            for s, p in ((1, a1), (2, a2), (4, a4)):
                dr, di = cmul(*p, shift(xr, s), shift(xi, s))
                xr, xi = xr + dr, xi + di
            cr, ci = jnp.broadcast_to(state[0], shape), jnp.broadcast_to(state[1], shape)
            dr, di = cmul(*pw, cr, ci)
            xr, xi = xr + dr, xi + di
            if reverse:
                states[0, rows, :], states[1, rows, :] = xr, xi
                gr = jnp.where(row < sub - 1, pltpu.roll(xr, sub - 1, 0), cr)
                gi = jnp.where(row < sub - 1, pltpu.roll(xi, sub - 1, 0), ci)
                pr, pi = xr_ref[rows, :], xi_ref[rows, :]
                dacc[0] += gr * pr + gi * pi
                dacc[1] += gi * pr - gr * pi
            else:
                or_ref[rows, :], oi_ref[rows, :] = xr, xi
            return (jnp.sum(jnp.where(row == edge, xr, 0.0), axis=0, keepdims=True),
                    jnp.sum(jnp.where(row == edge, xi, 0.0), axis=0, keepdims=True))

        last = lax.fori_loop(0, tb // sub, tile, (carry[0], carry[1]))
        carry[0], carry[1] = last
        if reverse:
            or_ref[...], oi_ref[...] = states[0].astype(BF16), states[1].astype(BF16)
            dar_ref[...] = jnp.sum(dacc[0], axis=0, keepdims=True)
            dai_ref[...] = jnp.sum(dacc[1], axis=0, keepdims=True)
        else:
            or_bf_ref[...], oi_bf_ref[...] = or_ref[...].astype(BF16), oi_ref[...].astype(BF16)

    tblock = (lambda k: ntb - 1 - k) if reverse else (lambda k: k)
    re_spec = pl.BlockSpec((tb, cb), lambda j, k: (tblock(k), j))
    a_spec = pl.BlockSpec((1, cb), lambda j, k: (0, j))
    ins, in_specs = [b_re, b_im, a_re, a_im], [re_spec, re_spec, a_spec, a_spec]
    seq_bf = jax.ShapeDtypeStruct((t, S5_COLS), BF16)
    scratch = [pltpu.VMEM((2, 1, cb), F32)]
    if reverse:
        ins += list(x_prev)
        in_specs += [re_spec, re_spec]
        out_shape = [seq_bf, seq_bf] + [jax.ShapeDtypeStruct((1, S5_COLS), F32)] * 2
        out_specs = [re_spec, re_spec, a_spec, a_spec]
        scratch += [pltpu.VMEM((2, sub, cb), F32), pltpu.VMEM((2, tb, cb), F32)]
    else:
        out_shape = [jax.ShapeDtypeStruct((t, S5_COLS), F32)] * 2 + [seq_bf, seq_bf]
        out_specs = [re_spec] * 4
    return pl.pallas_call(
        body, grid=(ncb, ntb), in_specs=in_specs, out_specs=out_specs, out_shape=out_shape, scratch_shapes=scratch,
        compiler_params=pltpu.CompilerParams(dimension_semantics=("arbitrary", "arbitrary"), vmem_limit_bytes=VMEM_LIMIT),
        name="s5_scan_bwd" if reverse else "s5_scan_fwd")(*ins)


CONF_DIM = 512
CONF_K = 31
CONF_PAD = 32


def _conf_conv(proj_o, w, b, dc1=None):
    t = proj_o.shape[0]
    cb = LANES
    ncb = CONF_DIM // cb
    chunk = min(512, t)
    chunks = range(0, t, chunk)
    bwd = dc1 is not None

    def body(*refs):
        if bwd:
            ca_ref, cg_ref, w_ref, b_ref, dy_ref, dca_ref, dcg_ref, dw_ref, db_ref, xs, dys = refs
        else:
            ca_ref, cg_ref, w_ref, b_ref, o_ref, xs = refs
        xs[pl.ds(0, CONF_PAD), :] = jnp.zeros((CONF_PAD, cb), F32)
        for t0 in chunks:
            rows = pl.ds(t0, chunk)
            xs[pl.ds(CONF_PAD + t0, chunk), :] = ca_ref[rows, :] * jax.nn.sigmoid(cg_ref[rows, :])
        if not bwd:
            for t0 in chunks:
                acc = jnp.broadcast_to(b_ref[...], (chunk, cb))
                for k in range(CONF_K):
                    acc = acc + w_ref[k] * xs[pl.ds(CONF_PAD + t0 - (CONF_K - 1 - k), chunk), :]
                o_ref[pl.ds(t0, chunk), :] = acc
            return
        dys[pl.ds(t, CONF_PAD), :] = jnp.zeros((CONF_PAD, cb), F32)
        db = jnp.zeros((1, cb), F32)
        for t0 in chunks:
            dys[pl.ds(t0, chunk), :] = dy_ref[pl.ds(t0, chunk), :]
            db = db + jnp.sum(dy_ref[pl.ds(t0, chunk), :], axis=0, keepdims=True)
        db_ref[...] = db
        for t0 in chunks:
            rows = pl.ds(t0, chunk)
            acc = jnp.zeros((chunk, cb), F32)
            for k in range(CONF_K):
                acc = acc + w_ref[k] * dys[pl.ds(t0 + (CONF_K - 1 - k), chunk), :]
            sig = jax.nn.sigmoid(cg_ref[rows, :])
            dca_ref[rows, :] = (acc * sig).astype(BF16)
            dcg_ref[rows, :] = (acc * ca_ref[rows, :] * sig * (1.0 - sig)).astype(BF16)
        for k in range(CONF_K):
            dwk = jnp.zeros((1, cb), F32)
            for t0 in chunks:
                window = xs[pl.ds(CONF_PAD + t0 - (CONF_K - 1 - k), chunk), :]
                dwk = dwk + jnp.sum(dy_ref[pl.ds(t0, chunk), :] * window, axis=0, keepdims=True)
            dw_ref[k] = dwk

    col = lambda off: pl.BlockSpec((t, cb), lambda j, off=off: (0, j + off))
    w_spec = pl.BlockSpec((CONF_K, 1, cb), lambda j: (0, 0, j))
    b_spec = pl.BlockSpec((1, cb), lambda j: (0, j))
    seq = jax.ShapeDtypeStruct((t, CONF_DIM), F32)
    ins, in_specs = [proj_o, proj_o, w, b], [col(0), col(ncb), w_spec, b_spec]
    scratch = [pltpu.VMEM((CONF_PAD + t, cb), F32)]
    if bwd:
        ins, in_specs = ins + [dc1], in_specs + [col(0)]
        seq_bf = jax.ShapeDtypeStruct((t, CONF_DIM), BF16)
        out_shape, out_specs = [seq_bf, seq_bf, jax.ShapeDtypeStruct(w.shape, F32), jax.ShapeDtypeStruct(b.shape, F32)], [col(0), col(0), w_spec, b_spec]
        scratch = scratch + [pltpu.VMEM((t + CONF_PAD, cb), F32)]
    else:
        out_shape, out_specs = seq, col(0)
    return pl.pallas_call(
        body, grid=(ncb,), in_specs=in_specs, out_specs=out_specs, out_shape=out_shape, scratch_shapes=scratch,
        compiler_params=pltpu.CompilerParams(dimension_semantics=("arbitrary",), vmem_limit_bytes=VMEM_LIMIT),
        name="conf_conv_bwd" if bwd else "conf_conv")(*ins)


N_CHIPS = 4
_HBM = pl.BlockSpec(memory_space=pl.ANY)
_MESH_ID = pl.DeviceIdType.MESH


def _comm_call(body, srcs, out_shapes, n_sems, name):
    n = len(srcs)
    return pl.pallas_call(
        body, out_shape=out_shapes, in_specs=[_HBM] * n, out_specs=[_HBM] * n,
        scratch_shapes=[pltpu.SemaphoreType.DMA((n_sems,)), pltpu.SemaphoreType.DMA((n_sems,)), pltpu.SemaphoreType.DMA((n,))],
        compiler_params=pltpu.CompilerParams(has_side_effects=True), name=name)(*srcs)


GATHER_SEMS = N_DEV - 1


def _gather_phases(src_refs, out_refs, send_sems, recv_sems, local_sems):
    n = len(src_refs)
    x, y, c = lax.axis_index("x"), lax.axis_index("y"), lax.axis_index("c")
    me, sibling = (x, y, c), (x, y, 1 - c)
    chips = [(1 - x, y), (x, 1 - y), (1 - x, 1 - y)]

    def copy(i, k, block, to, from_src=False):
        rows = out_refs[i].at[4 * block[0] + 2 * block[1] + block[2]]
        return pltpu.make_async_remote_copy(
            src_ref=src_refs[i] if from_src else rows, dst_ref=rows, send_sem=send_sems.at[GATHER_SEMS * i + k],
            recv_sem=recv_sems.at[GATHER_SEMS * i + k], device_id=to, device_id_type=_MESH_ID)

    def local(i):
        return pltpu.make_async_copy(src_refs[i], out_refs[i].at[4 * x + 2 * y + c], local_sems.at[i])

    def first(i):
        return [copy(i, 0, me, sibling, True)] + [copy(i, 1 + j, me, (*chip, c), True) for j, chip in enumerate(chips)]

    def passed(i, j):
        return copy(i, 4 + j, (*chips[j], c), sibling)

    def start():
        for i in range(n):
            local(i).start()
            for cp in first(i):
                cp.start()

    def forward():
        for j, chip in enumerate(chips):
            for i in range(n):
                copy(i, 1 + j, (*chip, c), me).wait_recv()
                passed(i, j).start()

    def finish():
        for i in range(n):
            copy(i, 0, sibling, me).wait_recv()
            for j, chip in enumerate(chips):
                copy(i, 4 + j, (*chip, 1 - c), me).wait_recv()
        for i in range(n):
            for cp in first(i) + [passed(i, j) for j in range(len(chips))]:
                cp.wait_send()
            local(i).wait()

    return start, forward, finish


def _gather(srcs, name):
    n = len(srcs)

    def body(*refs):
        for phase in _gather_phases(refs[:n], refs[n:2 * n], *refs[2 * n:]):
            phase()

    return _comm_call(body, srcs, [jax.ShapeDtypeStruct((N_DEV,) + s.shape, s.dtype) for s in srcs], GATHER_SEMS * n, name)


def _swap_sibling(srcs, name):
    n = len(srcs)

    def body(*refs):
        src_refs, out_refs = refs[:n], refs[n:2 * n]
        send_sems, recv_sems, _ = refs[2 * n:]
        sibling = (lax.axis_index("x"), lax.axis_index("y"), 1 - lax.axis_index("c"))
        copies = [pltpu.make_async_remote_copy(src_ref=src_refs[i], dst_ref=out_refs[i], send_sem=send_sems.at[i],
                                               recv_sem=recv_sems.at[i], device_id=sibling, device_id_type=_MESH_ID) for i in range(n)]
        for cp in copies:
            cp.start()
        for cp in copies:
            cp.wait_recv()
        for cp in copies:
            cp.wait_send()

    return _comm_call(body, srcs, [jax.ShapeDtypeStruct(s.shape, s.dtype) for s in srcs], n, name)


EXCHANGE_SEMS = N_CHIPS - 1


def _exchange_phases(src_refs, out_refs, send_sems, recv_sems, local_sems):
    n = len(src_refs)
    x, y, c = lax.axis_index("x"), lax.axis_index("y"), lax.axis_index("c")
    mine = 2 * x + y

    def local(i):
        return pltpu.make_async_copy(src_refs[i].at[mine], out_refs[i].at[mine], local_sems.at[i])

    def copies(i, landing):
        out = []
        for k in range(1, N_CHIPS):
            px, py = x ^ (k >> 1), y ^ (k & 1)
            peer = 2 * px + py
            sem = EXCHANGE_SEMS * i + k - 1
            out.append(pltpu.make_async_remote_copy(
                src_ref=src_refs[i].at[peer], dst_ref=out_refs[i].at[peer if landing else mine], send_sem=send_sems.at[sem],
                recv_sem=recv_sems.at[sem], device_id=(px, py, c), device_id_type=_MESH_ID))
        return out

    def start():
        for i in range(n):
            local(i).start()
            for send in copies(i, False):
                send.start()

    def finish():
        for i in range(n):
            for recv in copies(i, True):
                recv.wait_recv()
        for i in range(n):
            for send in copies(i, False):
                send.wait_send()
            local(i).wait()

    return start, (lambda: None), finish


def _exchange_chips(srcs, name):
    n = len(srcs)

    def body(*refs):
        for phase in _exchange_phases(refs[:n], refs[n:2 * n], *refs[2 * n:]):
            phase()

    return _comm_call(body, srcs, [jax.ShapeDtypeStruct(s.shape, s.dtype) for s in srcs], EXCHANGE_SEMS * n, name)


def _add(a, b, name):
    k, rows, cols = a.shape
    tr = _row_tile(rows)

    def body(a_ref, b_ref, o_ref):
        o_ref[...] = (a_ref[...].astype(F32) + b_ref[...].astype(F32)).astype(o_ref.dtype)

    spec = pl.BlockSpec((k, tr, cols), lambda i: (0, i, 0))
    return pl.pallas_call(
        body, grid=(rows // tr,), in_specs=[spec, spec], out_specs=spec, out_shape=jax.ShapeDtypeStruct(a.shape, a.dtype),
        compiler_params=pltpu.CompilerParams(dimension_semantics=("parallel",), vmem_limit_bytes=VMEM_LIMIT), name=name)(a, b)


def _row_tile(r, pref=256):
    if r <= pref:
        return r
    t = pref // 16 * 16
    while r % t:
        t -= 16
    return t


def _join_cols(g, width, name):
    _, rows, ws = g.shape
    tr = _row_tile(rows)
    tail = width - N_DEV * ws

    def body(g_ref, o_ref):
        for d in range(N_DEV):
            o_ref[:, pl.ds(d * ws, ws)] = g_ref[d]
        if tail:
            o_ref[:, pl.ds(N_DEV * ws, tail)] = jnp.zeros((tr, tail), g.dtype)

    return pl.pallas_call(
        body, grid=(rows // tr,), in_specs=[pl.BlockSpec((N_DEV, tr, ws), lambda i: (0, i, 0))],
        out_specs=pl.BlockSpec((tr, width), lambda i: (i, 0)), out_shape=jax.ShapeDtypeStruct((rows, width), g.dtype),
        compiler_params=pltpu.CompilerParams(dimension_semantics=("parallel",), vmem_limit_bytes=VMEM_LIMIT), name=name)(g)


def _split_cols(full, ws, dtype, name):
    rows, width = full.shape
    tr = _row_tile(rows)

    def body(x_ref, o_ref):
        for d in range(N_DEV):
            o_ref[d] = x_ref[:, pl.ds(d * ws, ws)].astype(dtype)

    return pl.pallas_call(
        body, grid=(rows // tr,), in_specs=[pl.BlockSpec((tr, width), lambda i: (i, 0))],
        out_specs=pl.BlockSpec((N_DEV, tr, ws), lambda i: (0, i, 0)), out_shape=jax.ShapeDtypeStruct((N_DEV, rows, ws), dtype),
        compiler_params=pltpu.CompilerParams(dimension_semantics=("parallel",), vmem_limit_bytes=VMEM_LIMIT), name=name)(full)


def _adamw(parts, w, m, v, name):
    r, c = w.shape
    n_parts = parts.shape[0]
    tr = _row_tile(r)

    def body(p_ref, w_ref, m_ref, v_ref, g_ref, d_ref, nm_ref, nv_ref):
        g = p_ref[0].astype(F32)
        for s in range(1, n_parts):
            g = g + p_ref[s].astype(F32)
        nm = ADAM_B1 * m_ref[...] + (1.0 - ADAM_B1) * g
        nv = ADAM_B2 * v_ref[...] + (1.0 - ADAM_B2) * (g * g)
        m_hat = nm / (1.0 - ADAM_B1 ** ADAM_STEP)
        v_hat = nv / (1.0 - ADAM_B2 ** ADAM_STEP)
        g_ref[...] = g
        nm_ref[...] = nm
        nv_ref[...] = nv
        d_ref[...] = -ADAM_LR * (m_hat / (jnp.sqrt(v_hat) + ADAM_EPS) + ADAM_WD * w_ref[...])

    spec = pl.BlockSpec((tr, c), lambda i: (i, 0))
    return pl.pallas_call(
        body, grid=(r // tr,), in_specs=[pl.BlockSpec((n_parts, tr, c), lambda i: (0, i, 0)), spec, spec, spec],
        out_specs=[spec] * 4, out_shape=[jax.ShapeDtypeStruct((r, c), F32)] * 4,
        compiler_params=pltpu.CompilerParams(dimension_semantics=("parallel",), vmem_limit_bytes=VMEM_LIMIT),
        name=name)(parts, w, m, v)


def _pack_rows(n_elems, mult):
    rows = -(-n_elems // PACK_COLS)
    return -(-rows // mult) * mult


def _pack(arrays, dtype, mult, lead=()):
    flat = jnp.concatenate([a.astype(dtype).reshape(lead + (-1,)) for a in arrays], axis=-1)
    rows = _pack_rows(flat.shape[-1], mult)
    flat = jnp.pad(flat, [(0, 0)] * len(lead) + [(0, rows * PACK_COLS - flat.shape[-1])])
    return flat.reshape(lead + (rows, PACK_COLS))


def _unpack(buf, shapes, lead=()):
    flat = buf.reshape(lead + (-1,))
    out, off = [], 0
    for s in shapes:
        n = math.prod(s)
        out.append(flat[..., off:off + n].reshape(lead + tuple(s)))
        off += n
    return out


def _join_shards(piece, axis):
    moved = jnp.moveaxis(piece, 0, axis)
    shape = moved.shape
    return moved.reshape(shape[:axis] + (shape[axis] * shape[axis + 1],) + shape[axis + 2:])


def _split_shards(full, axis):
    shape = full.shape
    return jnp.moveaxis(full.reshape(shape[:axis] + (N_DEV, shape[axis] // N_DEV) + shape[axis + 1:]), axis, 0)


def _block_diag(blocks):
    g, r, c = blocks.shape
    k = g // S5_SUPER
    eye = jnp.eye(k, dtype=blocks.dtype)
    return (blocks.reshape(S5_SUPER, k, r, 1, c) * eye[None, :, None, :, None]).reshape(S5_SUPER, k * r, k * c)


def _diag_blocks(mat, r, c):
    k = mat.shape[1] // r
    eye = jnp.eye(k, dtype=mat.dtype)
    return jnp.sum(mat.reshape(S5_SUPER, k, r, k, c) * eye[None, :, None, :, None], axis=3).reshape(S5_SUPER * k, r, c)


def _pad_lanes(a):
    a = a.reshape(1, -1)
    return jnp.pad(a, ((0, 0), (0, LANES - a.shape[1])))


def _head_expand():
    h = jnp.arange(LANES)[:, None]
    ch = jnp.arange(SSM_HEADS * SSM_P)[None, :] // SSM_P
    return (h == ch).astype(F32)


def _rotary_tables(t):
    inv = 10000.0 ** (-jnp.arange(0, RET_DK, 2, dtype=F32) / RET_DK)
    ang = jnp.arange(t).astype(F32)[:, None] * inv[None, :]
    cos, sin = jnp.cos(ang), jnp.sin(ang)
    return jnp.concatenate([cos, cos], axis=1), jnp.concatenate([-sin, sin], axis=1)


def _rms_fwd(x, g, name):
    return _rows(_f_rms, x.shape[0], [(x, D_MODEL, 0)], [g], [D_MODEL], [], name, bf16_outs=(0,))[0]


def _rms_bwd(x, g, dh, dres, name):
    def fn(x_, dh_, dres_, g_):
        _, vjp = jax.vjp(lambda a, b: _rms(a, b), x_, g_)
        dx, dg = vjp(dh_)
        return dx + dres_, dx + dres_, dg
    return _rows(fn, x.shape[0], [(x, D_MODEL, 0), (dh, D_MODEL, 0), (dres, D_MODEL, 0)], [g], [D_MODEL, D_MODEL], [(1, D_MODEL)],
                 name, bf16_outs=(1,))


def _ffn_fwd(x, norm_g, w_up, dw_w, dw_b, w_down, tag):
    t = x.shape[0]
    nbk = D_FF // LANES
    h = _rms_fwd(x, norm_g, "ffn_norm_" + tag)
    up = _mm(h, w_up, "nn", "ffn_up_" + tag)
    mid = _cols(_f_ffnmid, nbk, [(up, 0), (up, nbk), (dw_w, 0), (dw_w, nbk), (dw_b, 0), (dw_b, nbk)], [(t,)], "ffn_mid_" + tag,
                bf16_outs=(0,))[0]
    out = _mm(mid, w_down, "nn", "ffn_down_" + tag, res=x)
    return out, (h, up, mid)


def _ffn_bwd(x, norm_g, w_up, dw_w, dw_b, w_down, saved, dout, dout_bf, tag):
    t = x.shape[0]
    nbk = D_FF // LANES
    h, up, mid = saved
    d_w_down = _mm(mid, dout_bf, "tn", "ffn_down_dw_" + tag, out_dtype=BF16)
    dmid = _mm(dout_bf, w_down, "nt", "ffn_down_dx_" + tag)
    dgin, duin, dwg, dwu, dbg, dbu = _cols(
        _grad_fn(_f_ffnmid, 6, 6), nbk,
        [(up, 0), (up, nbk), (dw_w, 0), (dw_w, nbk), (dw_b, 0), (dw_b, nbk), (dmid, 0)],
        [(t,), (t,), (3, 1), (3, 1), (1,), (1,)], "ffn_mid_bwd_" + tag, bf16_outs=(0, 1))
    dup = jnp.concatenate([dgin, duin], axis=1)
    d_w_up = _mm(h, dup, "tn", "ffn_up_dw_" + tag)
    dh = _mm(dup, w_up, "nt", "ffn_up_dx_" + tag)
    dx, dx_bf, dnorm = _rms_bwd(x, norm_g, dh, dout, "ffn_norm_bwd_" + tag)
    return dx, dx_bf, dict(norm=dnorm, w_up=d_w_up, dw_w=jnp.concatenate([dwg, dwu], axis=2)[:, 0],
                           dw_b=jnp.concatenate([dbg, dbu], axis=1), w_down=d_w_down)


def _local_step(x, tgt, w, late=None):
    t = x.shape[0]
    grads = {}
    expand = _head_expand()
    cosf, sins = _rotary_tables(t)
    mix_g = [w['mix_norm'][i:i + 1] for i in range(2)]
    ffn_g = [w['ffn_norm'][i:i + 1] for i in range(2)]
    ffn_dw_w = [w['ffn_dw_w'][i][:, None, :] for i in range(2)]
    ffn_dw_b = [w['ffn_dw_b'][i:i + 1] for i in range(2)]

    w_e = w['e_w_in']
    conv_w = w['e_conv_w'][0][:, None, :]
    conv_b = w['e_conv_b']
    dt_bias, a_log, d_skip = _pad_lanes(w['e_dt_bias']), _pad_lanes(w['e_a_log']), _pad_lanes(w['e_d'])
    h0 = _rms_fwd(x, mix_g[0], "mix_norm_0")
    proj = _mm(h0, w_e, "nn", "e_in")
    qr, kr = _rows(_f_retpre, t, [(proj, 512, 0), (proj, 512, 1), (cosf, LANES, 0), (sins, LANES, 0)], [], [512, 512], [], "ret_pre")
    if late:
        r, *landed = _ret_fwd(qr, kr, proj, t, late['ret_fwd'][0])
        late['ret_fwd'][1](w, landed)
    else:
        r = _ret_fwd(qr, kr, proj, t)
    y_ret = _rows(_f_retpost, t, [(r, 1024, 0), (proj, 1024, 2)], [], [1024], [], "ret_post", bf16_outs=(0,))[0]
    xbc_act = _cols(_f_ssdconv, 12, [(proj, 32), (conv_w, 0), (conv_b, 0)], [(t,)], "ssd_conv")[0]
    xdt, da = _rows(_f_ssdpre, t, [(xbc_act, 1024, 0), (proj, LANES, 44)], [dt_bias, a_log, expand], [1024, LANES], [], "ssd_pre")
    a_cum = _cumsum(da, False, "ssd_cumsum")
    a_cum_t = a_cum[:, :SSM_HEADS].T
    if late:
        yc, *landed = _ssd_fwd(xbc_act, xdt, a_cum, a_cum_t, t, late['ssd_fwd'][0])
        late['ssd_fwd'][1](w, landed)
    else:
        yc = _ssd_fwd(xbc_act, xdt, a_cum, a_cum_t, t)
    y_ssm = _rows(_f_ssdpost, t, [(yc, 1024, 0), (xbc_act, 1024, 0), (proj, 1024, 3)], [d_skip, w['e_ssm_norm'], expand],
                  [1024], [], "ssd_post", bf16_outs=(0,))[0]
    mix_e = jnp.concatenate([y_ret, y_ssm], axis=1)
    x1 = _mm(mix_e, w['e_w_out'], "nn", "e_out", res=x)
    x2, ffn0 = _ffn_fwd(x1, ffn_g[0], w['ffn_w_up'][0], ffn_dw_w[0], ffn_dw_b[0], w['ffn_w_down'][0], "0")

    lr, li = w['o_a_re'][0], w['o_a_im'][0]
    ls = w['o_log_step'].reshape(S5_GROUPS, 1)
    b_re3, b_im3 = jnp.transpose(w['o_b_re'][0], (2, 0, 1)), jnp.transpose(w['o_b_im'][0], (2, 0, 1))
    par_ins = [lr, li, ls, b_re3, b_im3]
    whole = lambda a: (a, a.shape, (lambda i, n=a.ndim: (0,) * n))
    par_shapes = [(S5_GROUPS, S5_STATE)] * 2 + [(S5_GROUP, S5_GROUPS, S5_STATE)] * 2
    ab_re, ab_im, bb_re, bb_im = _call(_f_s5par, (1,), [whole(a) for a in par_ins],
                                       [(s, F32, s, (lambda i, n=len(s): (0,) * n), False) for s in par_shapes], "s5_params")
    w_b = jnp.concatenate([_block_diag(jnp.transpose(bb_re, (1, 0, 2))), _block_diag(jnp.transpose(bb_im, (1, 0, 2)))], axis=2)
    w_c = jnp.concatenate([_block_diag(jnp.transpose(w['o_c_re'][0], (0, 2, 1))),
                           -_block_diag(jnp.transpose(w['o_c_im'][0], (0, 2, 1)))], axis=1)
    a_re, a_im = ab_re.reshape(1, S5_COLS), ab_im.reshape(1, S5_COLS)
    dw_w = w['o_dw_w'][0][:, None, :]
    glu_w = w['o_glu_w'].astype(F32)

    h1 = _rms_fwd(x2, mix_g[1], "mix_norm_1")
    proj_o = _mm(h1, w['o_w_in'], "nn", "o_in")
    c1 = _conf_conv(proj_o, dw_w, w['o_dw_b'])
    c2 = _rows(_f_confb, t, [(c1, 512, 0)], [w['o_ln_g'], w['o_ln_b']], [512], [], "conf_norm", bf16_outs=(0,))[0]
    u_off = 2 * CONF_DIM // S5_NARROW
    bu_re, bu_im = _s5_expand(proj_o, u_off, w_b, False, "s5_bu")
    xs_re, xs_im, xs_re_bf, xs_im_bf = _s5_scan(bu_re, bu_im, a_re, a_im, False)
    y_s5 = _s5_contract(xs_re_bf, xs_im_bf, w_c, True, "s5_cx")
    s_out = _rows(_f_s5post, t, [(y_s5, 512, 0), (proj_o, 512, 2)], [w['o_d'], glu_w], [512], [], "s5_post", bf16_outs=(0,))[0]
    mix_o = jnp.concatenate([c2, s_out], axis=1)
    x3 = _mm(mix_o, w['o_w_out'], "nn", "o_out", res=x2)
    x4, ffn1 = _ffn_fwd(x3, ffn_g[1], w['ffn_w_up'][1], ffn_dw_w[1], ffn_dw_b[1], w['ffn_w_down'][1], "1")

    dx4, dx4_bf, loss_blk, d_final = _rows(_loss_step, t, [(x4, D_MODEL, 0), (tgt, D_MODEL, 0)], [w['final_norm'].reshape(1, D_MODEL)],
                                           [D_MODEL, D_MODEL], [(8, LANES), (1, D_MODEL)], "loss_head", bf16_outs=(1,))
    loss = loss_blk[0, 0]
    grads['final_norm'] = d_final.reshape(D_MODEL)

    dx3, dx3_bf, g1 = _ffn_bwd(x3, ffn_g[1], w['ffn_w_up'][1], ffn_dw_w[1], ffn_dw_b[1], w['ffn_w_down'][1], ffn1, dx4, dx4_bf, "1")
    grads['o_w_out'] = _mm(mix_o, dx3_bf, "tn", "o_out_dw", out_dtype=BF16)
    dmix_o = _mm(dx3_bf, w['o_w_out'], "nt", "o_out_dx")
    dc1, d_ln_g, d_ln_b = _conf_norm_bwd(c1, w['o_ln_g'], w['o_ln_b'], dmix_o, t)
    dca, dcg, d_dw_w, d_dw_b = _conf_conv(proj_o, dw_w, w['o_dw_b'], dc1)
    dyc, du_skip, d_od, d_glu = _s5_post_bwd(y_s5, proj_o, w['o_d'], glu_w, dmix_o, t)
    grads['o_glu_w'] = d_glu
    d_w_c = _s5_wgrad(dyc, 0, xs_re_bf, xs_im_bf, True, "s5_cx_dw")
    dxs_re, dxs_im = _s5_expand(dyc, 0, w_c, True, "s5_cx_dx")
    g_re, g_im, d_are, d_aim = _s5_scan(dxs_re, dxs_im, a_re, a_im, True, (xs_re, xs_im))
    d_w_b = _s5_wgrad(proj_o, u_off, g_re, g_im, False, "s5_bu_dw")
    du = _s5_contract(g_re, g_im, w_b, False, "s5_bu_dx", res=du_skip, out_dtype=BF16)
    d_bb_re = jnp.transpose(_diag_blocks(d_w_b[:, :, :S5_WIDE], S5_GROUP, S5_STATE), (1, 0, 2))
    d_bb_im = jnp.transpose(_diag_blocks(d_w_b[:, :, S5_WIDE:], S5_GROUP, S5_STATE), (1, 0, 2))
    par_cts = [d_are.reshape(S5_GROUPS, S5_STATE), d_aim.reshape(S5_GROUPS, S5_STATE), d_bb_re, d_bb_im]
    in_shapes = [a.shape for a in par_ins]
    d_lr, d_li, d_ls, d_br3, d_bi3 = _call(_grad_fn(_f_s5par, 5, 5), (1,), [whole(a) for a in par_ins + par_cts],
                                           [(s, F32, s, (lambda i, n=len(s): (0,) * n), False) for s in in_shapes], "s5_params_bwd")
    grads['o_a_re'], grads['o_a_im'], grads['o_log_step'] = d_lr[None], d_li[None], d_ls.reshape(1, S5_GROUPS)
    grads['o_b_re'], grads['o_b_im'] = jnp.transpose(d_br3, (1, 2, 0))[None], jnp.transpose(d_bi3, (1, 2, 0))[None]
    grads['o_c_re'] = jnp.transpose(_diag_blocks(d_w_c[:, :S5_WIDE], S5_STATE, S5_GROUP), (0, 2, 1))[None]
    grads['o_c_im'] = -jnp.transpose(_diag_blocks(d_w_c[:, S5_WIDE:], S5_STATE, S5_GROUP), (0, 2, 1))[None]
    grads['o_d'], grads['o_ln_g'], grads['o_ln_b'] = d_od, d_ln_g, d_ln_b
    grads['o_dw_w'], grads['o_dw_b'] = d_dw_w[:, 0][None], d_dw_b
    dproj_o = jnp.concatenate([dca, dcg, du], axis=1)
    grads['o_w_in'] = _mm(h1, dproj_o, "tn", "o_in_dw")
    dh1 = _mm(dproj_o, w['o_w_in'], "nt", "o_in_dx")
    dx2, dx2_bf, d_mix1 = _rms_bwd(x2, mix_g[1], dh1, dx3, "mix_norm_bwd_1")
    in_flight = late['grads_ready']({**grads, 'ffn_w_up': [None, g1['w_up']], 'ffn_w_down': [None, g1['w_down']]}) if late else ()

    dx1, dx1_bf, g0 = _ffn_bwd(x1, ffn_g[0], w['ffn_w_up'][0], ffn_dw_w[0], ffn_dw_b[0], w['ffn_w_down'][0], ffn0, dx2, dx2_bf, "0")
    in_flight_ffn = late['ffn_grads_ready']({'ffn_w_up': [g0['w_up'], None], 'ffn_w_down': [g0['w_down'], None]}) if late else ()
    grads['e_w_out'] = _mm(mix_e, dx1_bf, "tn", "e_out_dw", out_dtype=BF16)
    dmix_e = _mm(dx1_bf, w['e_w_out'], "nt", "e_out_dx")
    dr, dg = _rows(_grad_fn(_f_retpost, 2, 2), t, [(r, 1024, 0), (proj, 1024, 2), (dmix_e, 1024, 0)], [], [1024, 1024], [], "ret_post_bwd",
                   bf16_outs=(0, 1))
    dqr, dkr, dv, landed = _ret_bwd(qr, kr, proj, dr, t, exchange=in_flight_ffn)
    if late:
        late['ffn_grads_landed'](landed)
    dq, dk = _ret_pre_bwd(proj, cosf, sins, dqr, dkr, t)
    dyc0, dxs1, dz, d_dskip, d_ssm_norm = _ssd_post_bwd(yc, xbc_act, proj, d_skip, w['e_ssm_norm'], expand, dmix_e, t)
    dcm, da_q, dbm, dxdt, da_k, da_k_t, landed = _ssd_bwd(xbc_act, xdt, a_cum, a_cum_t, dyc0, expand, t, exchange=in_flight)
    if late:
        late['grads_landed'](landed)
    d_a_cum = da_q - da_k - jnp.pad(da_k_t.T, ((0, 0), (0, LANES - SSM_HEADS)))
    dda = _cumsum(d_a_cum, True, "ssd_cumsum_bwd")
    dxs, ddtr, d_dt_bias, d_a_log = _ssd_pre_bwd(xbc_act, proj, dt_bias, a_log, expand, dxdt, dda, dxs1, t)
    dxbc_act = jnp.concatenate([dxs, dbm, dcm], axis=1)
    dxbc, d_conv_w, d_conv_b = _cols(_grad_fn(_f_ssdconv, 3, 3), 12, [(proj, 32), (conv_w, 0), (conv_b, 0), (dxbc_act, 0)],
                                     [(t,), (4, 1), (1,)], "ssd_conv_bwd", bf16_outs=(0,))
    dproj = jnp.concatenate([dq, dk, dv, dg, dz, dxbc, ddtr], axis=1)
    grads['e_w_in'] = _mm(h0, dproj, "tn", "e_in_dw")
    dh0 = _mm(dproj, w_e, "nt", "e_in_dx")
    dx0, _, d_mix0 = _rms_bwd(x, mix_g[0], dh0, dx1, "mix_norm_bwd_0")

    grads['mix_norm'] = jnp.concatenate([d_mix0, d_mix1], axis=0)
    grads['e_conv_w'], grads['e_conv_b'] = d_conv_w[:, 0][None], d_conv_b
    grads['e_dt_bias'], grads['e_a_log'], grads['e_d'] = d_dt_bias[:, :SSM_HEADS], d_a_log[:, :SSM_HEADS], d_dskip[:, :SSM_HEADS]
    grads['e_ssm_norm'] = d_ssm_norm
    grads['ffn_norm'] = jnp.concatenate([g0['norm'], g1['norm']], axis=0)
    grads['ffn_w_up'], grads['ffn_w_down'] = [g0['w_up'], g1['w_up']], [g0['w_down'], g1['w_down']]
    grads['ffn_dw_w'] = jnp.stack([g0['dw_w'], g1['dw_w']], axis=0)
    grads['ffn_dw_b'] = jnp.concatenate([g0['dw_b'], g1['dw_b']], axis=0)
    return loss, dx0, grads


def _conf_norm_bwd(c1, ln_g, ln_b, dmix_o, t):
    def fn(c1_, dy_, g_, b_):
        _, vjp = jax.vjp(lambda a, b, c: _f_confb(a, b, c)[0], c1_, g_, b_)
        return vjp(dy_)
    return _rows(fn, t, [(c1, 512, 0), (dmix_o, 512, 0)], [ln_g, ln_b], [512], [(1, 512), (1, 512)], "conf_norm_bwd")


def _s5_post_bwd(y_s5, proj_o, d_skip, glu_w, dmix_o, t):
    def fn(yc_, u_, dy_, d_, gw_):
        _, vjp = jax.vjp(lambda a, b, c, e: _f_s5post(a, b, c, e)[0], yc_, u_, d_, gw_)
        return vjp(dy_)
    return _rows(fn, t, [(y_s5, 512, 0), (proj_o, 512, 2), (dmix_o, 512, 1)], [d_skip, glu_w], [512, 512], [(1, 512), (512, 512)],
                 "s5_post_bwd", bf16_outs=(0,))


def _ret_pre_bwd(proj, cosf, sins, dqr, dkr, t):
    def fn(q_, k_, cos_, sin_, dq_, dk_):
        _, vjp = jax.vjp(lambda a, b: _f_retpre(a, b, cos_, sin_), q_, k_)
        return vjp((dq_, dk_))
    return _rows(fn, t, [(proj, 512, 0), (proj, 512, 1), (cosf, LANES, 0), (sins, LANES, 0), (dqr, 512, 0), (dkr, 512, 0)], [],
                 [512, 512], [], "ret_pre_bwd", bf16_outs=(0, 1))


def _ssd_post_bwd(yc, xbc_act, proj, d_skip, norm_w, expand, dmix_e, t):
    def fn(yc_, xs_, z_, dy_, d_, nw_, e_):
        _, vjp = jax.vjp(lambda a, b, c, dd, n: _f_ssdpost(a, b, c, dd, n, e_)[0], yc_, xs_, z_, d_, nw_)
        return vjp(dy_)
    return _rows(fn, t, [(yc, 1024, 0), (xbc_act, 1024, 0), (proj, 1024, 3), (dmix_e, 1024, 1)], [d_skip, norm_w, expand],
                 [1024, 1024, 1024], [(1, LANES), (1, 1024)], "ssd_post_bwd", bf16_outs=(0, 2))


def _ssd_pre_bwd(xbc_act, proj, dt_bias, a_log, expand, dxdt, dda, dxs1, t):
    def fn(xs_, dtr_, dx_, dda_, dxs1_, bias_, alog_, e_):
        _, vjp = jax.vjp(lambda a, b, c, dd: _f_ssdpre(a, b, c, dd, e_), xs_, dtr_, bias_, alog_)
        dxs, ddtr, dbias, dalog = vjp((dx_, dda_))
        return dxs + dxs1_, ddtr, dbias, dalog
    return _rows(fn, t, [(xbc_act, 1024, 0), (proj, LANES, 44), (dxdt, 1024, 0), (dda, LANES, 0), (dxs1, 1024, 0)],
                 [dt_bias, a_log, expand], [1024, LANES], [(1, LANES), (1, LANES)], "ssd_pre_bwd", bf16_outs=(1,))


def kernel(x, mix_norm, e_w_in, e_conv_w, e_conv_b, e_dt_bias, e_a_log, e_d, e_ssm_norm, e_w_out, o_w_in, o_dw_w, o_dw_b, o_ln_g, o_ln_b, o_a_re, o_a_im, o_b_re, o_b_im, o_c_re, o_c_im, o_d, o_log_step, o_glu_w, o_w_out, ffn_norm, ffn_w_up, ffn_dw_w, ffn_dw_b, ffn_w_down, final_norm, loss_target, m_mix_norm, m_e_w_in, m_e_conv_w, m_e_conv_b, m_e_dt_bias, m_e_a_log, m_e_d, m_e_ssm_norm, m_e_w_out, m_o_w_in, m_o_dw_w, m_o_dw_b, m_o_ln_g, m_o_ln_b, m_o_a_re, m_o_a_im, m_o_b_re, m_o_b_im, m_o_c_re, m_o_c_im, m_o_d, m_o_log_step, m_o_glu_w, m_o_w_out, m_ffn_norm, m_ffn_w_up, m_ffn_dw_w, m_ffn_dw_b, m_ffn_w_down, m_final_norm, v_mix_norm, v_e_w_in, v_e_conv_w, v_e_conv_b, v_e_dt_bias, v_e_a_log, v_e_d, v_e_ssm_norm, v_e_w_out, v_o_w_in, v_o_dw_w, v_o_dw_b, v_o_ln_g, v_o_ln_b, v_o_a_re, v_o_a_im, v_o_b_re, v_o_b_im, v_o_c_re, v_o_c_im, v_o_d, v_o_log_step, v_o_glu_w, v_o_w_out, v_ffn_norm, v_ffn_w_up, v_ffn_dw_w, v_ffn_dw_b, v_ffn_w_down, v_final_norm):
    p = dict(locals())

    kinds = ("grad_", "delta_", "new_m_", "new_v_")

    def block(name, layer):
        return p[name][0 if layer is None else layer]

    def shards(blocks):
        return [block(n, layer).astype(BF16) for n, layer, _ in blocks]

    def place(w, gathered, blocks):
        for (n, layer, by_cols), g in zip(blocks, gathered):
            if by_cols:
                full = _join_cols(g, E_IN_PAD if n == 'e_w_in' else N_DEV * g.shape[2], f"join_{n}_{layer}")
            else:
                full = g.reshape(N_DEV * g.shape[1], g.shape[2])
            if layer is None:
                w[n] = full
            else:
                w[n][layer] = full

    first = [b for b in MATMUL_BLOCKS if b[0].startswith('e_')]
    with_ret = [b for b in MATMUL_BLOCKS if b[0].startswith('ffn_') and b[1] == 0]
    with_ssd = [b for b in MATMUL_BLOCKS if b not in first and b not in with_ret]
    gathered = _gather(shards(first) + [_pack([p[n] for n in SMALL_SHARDED], F32, 16)], "gather_first_weights")
    w = {n: p[n] for n in REPLICATED}
    w['ffn_w_up'], w['ffn_w_down'] = [None, None], [None, None]
    place(w, gathered, first)
    for n, piece in zip(SMALL_SHARDED, _unpack(gathered[-1], [p[n].shape for n in SMALL_SHARDED], lead=(N_DEV,))):
        w[n] = _join_shards(piece, SHARDED[n])
    late = {'ret_fwd': (shards(with_ret), functools.partial(place, blocks=with_ret)),
            'ssd_fwd': (shards(with_ssd), functools.partial(place, blocks=with_ssd))}

    def chip_sums(grads, blocks, extra, tag):
        sends = []
        for n, layer, by_cols in blocks:
            g = grads[n] if layer is None else grads[n][layer]
            if by_cols:
                sends.append(_split_cols(g, block(n, layer).shape[1], BF16, f"split_{n}_{layer}"))
            else:
                sends.append(g.astype(BF16).reshape(N_DEV, -1, g.shape[1]))
        core = lax.axis_index("c")
        by_core = [s.reshape((N_CHIPS, 2) + s.shape[1:]) for s in sends + extra]
        keep = [lax.dynamic_index_in_dim(s, core, axis=1, keepdims=False) for s in by_core]
        give = [lax.dynamic_index_in_dim(s, 1 - core, axis=1, keepdims=False) for s in by_core]
        got = _swap_sibling(give, "swap_sibling_grads_" + tag)
        return [_add(a, b, f"chip_sum_{tag}_{i}") for i, (a, b) in enumerate(zip(keep, got))]

    landed_late = []
    late['grads_ready'] = lambda grads: chip_sums(grads, with_ssd, [], "late")
    late['grads_landed'] = landed_late.extend
    landed_ffn = []
    late['ffn_grads_ready'] = lambda grads: chip_sums(grads, with_ret, [], "ffn")
    late['ffn_grads_landed'] = landed_ffn.extend

    loss, dx, grads = _local_step(x[0], loss_target[0], w, late)
    loss = lax.psum(loss, MESH_AXES)

    early = first
    small_send = _pack([_split_shards(grads[n].reshape(p[n].shape[:SHARDED[n]] + (-1,) + p[n].shape[SHARDED[n] + 1:]), SHARDED[n])
                        for n in SMALL_SHARDED], F32, 128, lead=(N_DEV,))
    landed_early = list(_exchange_chips(chip_sums(grads, early, [small_send], "early"), "exchange_chip_grads"))
    part_of = {(n, layer): part for (n, layer, _), part in zip(early + with_ret + with_ssd, landed_early[:-1] + landed_ffn + landed_late)}
    parts = [landed_early[-1], _gather([_pack([grads[n].reshape(p[n].shape) for n in REPLICATED], F32, 128)], "gather_replicated_grads")[0]]

    out, by_layer = {}, {}
    for n, layer, _ in MATMUL_BLOCKS:
        by_layer.setdefault(n, {})[layer] = _adamw(part_of[(n, layer)], *[block(pre + n, layer) for pre in ("", "m_", "v_")],
                                                   f"adamw_{n}_{layer}")
    for n, res in by_layer.items():
        for i, kind in enumerate(kinds):
            out[kind + n] = res[None][i][None] if None in res else jnp.stack([res[0][i], res[1][i]], axis=0)
    for names, part, tag in ((SMALL_SHARDED, parts[-2], "small"), (REPLICATED, parts[-1], "replicated")):
        packed = [_pack([p[pre + n] for n in names], F32, 128) for pre in ("", "m_", "v_")]
        for kind, buf in zip(kinds, _adamw(part, *packed, "adamw_" + tag)):
            for n, a in zip(names, _unpack(buf, [p[n].shape for n in names])):
                out[kind + n] = a
    return (loss, dx[None], *[out[kind + n] for kind in kinds for n in WEIGHTS])
```

```python
import functools
import math

import jax
import jax.numpy as jnp
from jax import lax
from jax.experimental import pallas as pl
from jax.experimental.pallas import tpu as pltpu

F32, BF16 = jnp.float32, jnp.bfloat16
HIGHEST = lax.Precision.HIGHEST
N_DEV = 8
MESH_AXES = ("x", "y", "c")
VMEM_LIMIT = 48 * 1024 * 1024
LANES = 128
PACK_COLS = 1024

D_MODEL = 1024
EPS = 1e-6
RET_HEADS, RET_DK, RET_DV = 4, 128, 256
SSM_HEADS, SSM_P, SSM_N, SSM_GROUPS = 16, 64, 128, 2
SSM_HG = SSM_HEADS // SSM_GROUPS
S5_GROUPS, S5_GROUP, S5_STATE = 32, 16, 64
S5_COLS = S5_GROUPS * S5_STATE
D_FF = 2816
E_IN, E_IN_PAD = 5648, 5760
ADAM_LR, ADAM_B1, ADAM_B2, ADAM_EPS, ADAM_WD, ADAM_STEP = 0.001, 0.9, 0.999, 1e-08, 0.01, 10

WEIGHTS = ['mix_norm', 'e_w_in', 'e_conv_w', 'e_conv_b', 'e_dt_bias', 'e_a_log', 'e_d', 'e_ssm_norm', 'e_w_out', 'o_w_in', 'o_dw_w', 'o_dw_b', 'o_ln_g', 'o_ln_b', 'o_a_re', 'o_a_im', 'o_b_re', 'o_b_im', 'o_c_re', 'o_c_im', 'o_d', 'o_log_step', 'o_glu_w', 'o_w_out', 'ffn_norm', 'ffn_w_up', 'ffn_dw_w', 'ffn_dw_b', 'ffn_w_down', 'final_norm']
SHARDED = {'e_w_in': 2, 'e_conv_w': 2, 'e_w_out': 1, 'o_w_in': 2, 'o_dw_w': 2, 'o_dw_b': 1, 'o_ln_g': 1, 'o_ln_b': 1,
           'o_d': 1, 'o_glu_w': 1, 'o_w_out': 1, 'ffn_w_up': 2, 'ffn_dw_w': 2, 'ffn_w_down': 1}
MATMUL_WEIGHTS = ['e_w_in', 'e_w_out', 'o_w_in', 'o_glu_w', 'o_w_out', 'ffn_w_up', 'ffn_w_down']
MATMUL_BLOCKS = [('e_w_in', None, True), ('e_w_out', None, False), ('o_w_in', None, True), ('o_glu_w', None, False),
                 ('o_w_out', None, False), ('ffn_w_up', 0, True), ('ffn_w_up', 1, True), ('ffn_w_down', 0, False), ('ffn_w_down', 1, False)]
SMALL_SHARDED = [n for n in WEIGHTS if n in SHARDED and n not in MATMUL_WEIGHTS]
REPLICATED = [n for n in WEIGHTS if n not in SHARDED]


def _call(fn, grid, ins, outs, name):
    n_in = len(ins)

    def body(*refs):
        vals = fn(*[r[...] for r in refs[:n_in]])
        first = pl.program_id(0) == 0
        for r, v, o in zip(refs[n_in:], vals, outs):
            if o[4]:
                @pl.when(first)
                def _():
                    r[...] = jnp.zeros_like(r)
                r[...] += v.astype(r.dtype)
            else:
                r[...] = v.astype(r.dtype)

    return pl.pallas_call(
        body, grid=grid,
        in_specs=[pl.BlockSpec(b, m) for _, b, m in ins],
        out_specs=[pl.BlockSpec(o[2], o[3]) for o in outs],
        out_shape=[jax.ShapeDtypeStruct(o[0], o[1]) for o in outs],
        compiler_params=pltpu.CompilerParams(dimension_semantics=("arbitrary",) * len(grid), vmem_limit_bytes=VMEM_LIMIT),
        name=name)(*[a for a, _, _ in ins])


def _rows(fn, n_rows, row_ins, full_ins, row_outs, acc_outs, name, tm=512, bf16_outs=()):
    tm = min(tm, n_rows)
    ins = [(a, (tm, w), (lambda i, c=c: (i, c))) for a, w, c in row_ins]
    ins += [(a, a.shape, (lambda i, n=a.ndim: (0,) * n)) for a in full_ins]
    outs = [((n_rows, w), BF16 if k in bf16_outs else F32, (tm, w), (lambda i: (i, 0)), False) for k, w in enumerate(row_outs)]
    outs += [(tuple(s), F32, tuple(s), (lambda i, n=len(s): (0,) * n), True) for s in acc_outs]
    return _call(fn, (n_rows // tm,), ins, outs, name)


def _cols(fn, n_blocks, col_ins, out_leads, name, cb=LANES, bf16_outs=()):
    ins = [(a, a.shape[:-1] + (cb,), (lambda j, n=a.ndim, o=o: (0,) * (n - 1) + (j + o,))) for a, o in col_ins]
    outs = [(tuple(s) + (n_blocks * cb,), BF16 if k in bf16_outs else F32, tuple(s) + (cb,), (lambda j, n=len(s): (0,) * n + (j,)), False)
            for k, s in enumerate(out_leads)]
    return _call(fn, (n_blocks,), ins, outs, name)


def _grad_fn(f, n_diff, n_in):
    def g(*a):
        diff, consts, cts = a[:n_diff], a[n_diff:n_in], a[n_in:]
        _, vjp = jax.vjp(lambda *d: f(*d, *consts), *diff)
        return vjp(tuple(cts))
    return g


def _silu(x):
    return x * jax.nn.sigmoid(x)


def _rms(x, g):
    return x * lax.rsqrt(jnp.mean(x * x, axis=-1, keepdims=True) + EPS) * g


@jax.custom_vjp
def _softplus(x):
    return jnp.maximum(x, 0.0) + jnp.log(1.0 + jnp.exp(-jnp.abs(x)))


_softplus.defvjp(lambda x: (_softplus(x), x), lambda x, g: (g * jax.nn.sigmoid(x),))


@jax.custom_vjp
def _swap_halves(x):
    return pltpu.roll(x, 64, 1)


_swap_halves.defvjp(lambda x: (_swap_halves(x), None), lambda _, g: (_swap_halves(g),))


def _shift_rows(x, k, up):
    if k == 0:
        return x
    n = x.shape[0]
    t = lax.broadcasted_iota(jnp.int32, x.shape, 0)
    if up:
        return jnp.where(t < n - k, pltpu.roll(x, n - k, 0), 0.0)
    return jnp.where(t >= k, pltpu.roll(x, k, 0), 0.0)


@jax.custom_vjp
def _dwconv(x, w, b):
    k_taps = w.shape[0]
    y = b + w[k_taps - 1] * x
    for k in range(k_taps - 1):
        y = y + w[k] * _shift_rows(x, k_taps - 1 - k, False)
    return y


def _dwconv_fwd(x, w, b):
    return _dwconv(x, w, b), (x, w)


def _dwconv_bwd(saved, dy):
    x, w = saved
    k_taps = w.shape[0]
    dx = w[k_taps - 1] * dy
    dws = []
    for k in range(k_taps - 1):
        s = k_taps - 1 - k
        dx = dx + w[k] * _shift_rows(dy, s, True)
        dws.append(jnp.sum(dy * _shift_rows(x, s, False), axis=0, keepdims=True)[None])
    dws.append(jnp.sum(dy * x, axis=0, keepdims=True)[None])
    return dx, jnp.concatenate(dws, axis=0), jnp.sum(dy, axis=0, keepdims=True)


_dwconv.defvjp(_dwconv_fwd, _dwconv_bwd)


def _f_rms(x, g):
    return (_rms(x, g),)


def _rot(x, cosf, sins):
    outs = []
    for h in range(RET_HEADS):
        xh = x[:, h * RET_DK:(h + 1) * RET_DK]
        outs.append(xh * cosf + _swap_halves(xh) * sins)
    return jnp.concatenate(outs, axis=1)


def _f_retpre(q, k, cosf, sins):
    return _rot(q, cosf, sins), _rot(k, cosf, sins) * (RET_DK ** -0.5)


def _f_retpost(r, g):
    outs = []
    for h in range(RET_HEADS):
        rh = r[:, h * RET_DV:(h + 1) * RET_DV]
        rc = rh - jnp.mean(rh, axis=-1, keepdims=True)
        outs.append(_silu(g[:, h * RET_DV:(h + 1) * RET_DV]) * (rc * lax.rsqrt(jnp.mean(rc * rc, axis=-1, keepdims=True) + EPS)))
    return (jnp.concatenate(outs, axis=1),)


def _f_ssdconv(xbc, w, b):
    return (_silu(_dwconv(xbc, w, b)),)


def _f_ssdpre(xs, dtr, bias, alog, expand):
    dt = _softplus(dtr + bias)
    return xs * jnp.dot(dt, expand, precision=HIGHEST, preferred_element_type=F32), dt * (-jnp.exp(alog))


def _f_ssdpost(yc, xs, z, dskip, norm_w, expand):
    d_wide = jnp.dot(jnp.broadcast_to(dskip, (yc.shape[0], LANES)), expand, precision=HIGHEST, preferred_element_type=F32)
    y = (yc + d_wide * xs) * _silu(z)
    half = y.shape[1] // SSM_GROUPS
    outs = []
    for g in range(SSM_GROUPS):
        yg = y[:, g * half:(g + 1) * half]
        outs.append(yg * lax.rsqrt(jnp.mean(yg * yg, axis=-1, keepdims=True) + EPS))
    return (jnp.concatenate(outs, axis=1) * norm_w,)


def _f_ffnmid(gin, uin, wg, wu, bg, bu):
    return (_silu(_dwconv(gin, wg, bg)) * _dwconv(uin, wu, bu),)


def _f_confb(c1, g, b):
    mu = jnp.mean(c1, axis=-1, keepdims=True)
    xc = c1 - mu
    return (_silu(xc * lax.rsqrt(jnp.mean(xc * xc, axis=-1, keepdims=True) + EPS) * g + b),)


def _f_s5post(yc, u, dskip, glu_w):
    s = jax.nn.gelu(yc + dskip * u)
    z = jnp.dot(s.astype(BF16), glu_w.astype(BF16), preferred_element_type=F32)
    return (s * jax.nn.sigmoid(z),)


def _f_s5par(lr, li, ls, br, bi):
    step = jnp.exp(ls)
    mag = jnp.exp(lr * step)
    ab_re = mag * jnp.cos(li * step)
    ab_im = mag * jnp.sin(li * step)
    den = lr * lr + li * li
    f_re = ((ab_re - 1.0) * lr + ab_im * li) / den
    f_im = (ab_im * lr - (ab_re - 1.0) * li) / den
    return ab_re, ab_im, f_re[None] * br - f_im[None] * bi, f_re[None] * bi + f_im[None] * br


def _loss_step(x, tgt, g):
    def f(x_, g_):
        e = _rms(x_, g_) - tgt
        return 0.5 * jnp.sum(jnp.mean(e * e, axis=-1, keepdims=True), axis=0, keepdims=True)
    loss, vjp = jax.vjp(f, x, g)
    dx, dg = vjp(jnp.ones((1, 1), F32))
    return dx, dx, jnp.broadcast_to(loss, (8, LANES)), dg


def _tile(n, pref):
    if n <= pref:
        return n
    t = (pref // LANES) * LANES
    while n % t:
        t -= LANES
    return t


_DIMS = {"nn": (((1,), (0,)), ((), ())), "nt": (((1,), (1,)), ((), ())), "tn": (((0,), (0,)), ((), ()))}


def _mm(a, b, mode, name, res=None, out_dtype=F32, tm=1024, tn=1408, tk=1408):
    if mode == "nn":
        (m, k), n = a.shape, b.shape[1]
    elif mode == "nt":
        (m, k), n = a.shape, b.shape[0]
    else:
        (k, m), n = a.shape, b.shape[1]
    tm, tn, tk = _tile(m, tm), _tile(n, tn), _tile(k, tk)
    nk = k // tk
    a_spec = pl.BlockSpec((tk, tm), lambda i, j, kk: (kk, i)) if mode == "tn" else pl.BlockSpec((tm, tk), lambda i, j, kk: (i, kk))
    b_spec = pl.BlockSpec((tn, tk), lambda i, j, kk: (j, kk)) if mode == "nt" else pl.BlockSpec((tk, tn), lambda i, j, kk: (kk, j))
    o_spec = pl.BlockSpec((tm, tn), lambda i, j, kk: (i, j))
    has_res = res is not None

    def body(*refs):
        a_ref, b_ref = refs[0], refs[1]
        o_ref, acc = refs[-2], refs[-1]
        kk = pl.program_id(2)

        @pl.when(kk == 0)
        def _():
            acc[...] = jnp.zeros_like(acc)

        acc[...] += lax.dot_general(a_ref[...].astype(BF16), b_ref[...].astype(BF16), _DIMS[mode], preferred_element_type=F32)

        @pl.when(kk == nk - 1)
        def _():
            o_ref[...] = (acc[...] + refs[2][...] if has_res else acc[...]).astype(out_dtype)

    return pl.pallas_call(
        body, grid=(m // tm, n // tn, nk),
        in_specs=[a_spec, b_spec] + ([o_spec] if has_res else []),
        out_specs=o_spec, out_shape=jax.ShapeDtypeStruct((m, n), out_dtype),
        scratch_shapes=[pltpu.VMEM((tm, tn), F32)],
        compiler_params=pltpu.CompilerParams(dimension_semantics=("parallel", "parallel", "arbitrary"), vmem_limit_bytes=VMEM_LIMIT),
        name=name)(*([a, b] + ([res] if has_res else [])))


def _seq_block(t):
    return min(512, t)


def _causal_diff(i, j, blk):
    r = lax.broadcasted_iota(jnp.int32, (blk, blk), 0)
    c = lax.broadcasted_iota(jnp.int32, (blk, blk), 1)
    return (i - j) * blk + r - c


def _ret_decay(lg, i, j, blk):
    diff = _causal_diff(i, j, blk)
    return jnp.where(diff >= 0, jnp.exp(lg * jnp.maximum(diff, 0).astype(F32)), 0.0)


def _pair_call(body, lead_grid, nb, key_major, in_specs, out_specs, out_shape, scratch, name, args, gather=(), exchange=()):
    pairs = [(i, j) for j in range(nb) for i in range(j, nb)] if key_major else [(i, j) for i in range(nb) for j in range(i + 1)]
    tables = [jnp.array([p[k] for p in pairs], jnp.int32) for k in (0, 1)]
    lead = len(lead_grid)
    n_steps = math.prod(lead_grid) * len(pairs)
    behind = list(gather) or list(exchange)
    n_bg = len(behind)
    phases, per_array = (_gather_phases, GATHER_SEMS) if gather else (_exchange_phases, EXCHANGE_SEMS)

    def on_pairs(spec):
        if spec.block_shape is None:
            return spec

        def index_map(*a):
            i, j = a[lead + 1][a[lead]], a[lead + 2][a[lead]]
            return spec.index_map(*a[:lead], *((j, i) if key_major else (i, j)))

        return pl.BlockSpec(spec.block_shape, index_map)

    many = isinstance(out_specs, (list, tuple))
    out_specs, out_shape = (list(out_specs), list(out_shape)) if many else ([out_specs], [out_shape])
    n_in, n_out = len(in_specs), len(out_specs)

    def wrapped(i_ref, j_ref, *refs):
        p = pl.program_id(lead)
        own = refs[:n_in] + refs[n_in + n_bg:n_in + n_bg + n_out] + refs[n_in + 2 * n_bg + n_out:len(refs) - (3 if n_bg else 0)]
        if n_bg:
            step = p + (pl.program_id(0) * len(pairs) if lead else 0)
            start, forward, finish = phases(refs[n_in:n_in + n_bg], refs[n_in + n_bg + n_out:n_in + 2 * n_bg + n_out], *refs[-3:])
            pl.when(step == 0)(start)
        body(i_ref[p], j_ref[p], *own)
        if n_bg:
            pl.when(step == n_steps * 3 // 4)(forward)
            pl.when(step == n_steps - 1)(finish)

    sems = [pltpu.SemaphoreType.DMA((per_array * n_bg,))] * 2 + [pltpu.SemaphoreType.DMA((n_bg,))] if n_bg else []
    landed = [jax.ShapeDtypeStruct(((N_DEV,) + g.shape) if gather else g.shape, g.dtype) for g in behind]
    grid_spec = pltpu.PrefetchScalarGridSpec(
        num_scalar_prefetch=2, grid=tuple(lead_grid) + (len(pairs),), in_specs=[on_pairs(s) for s in in_specs] + [_HBM] * n_bg,
        out_specs=[on_pairs(s) for s in out_specs] + [_HBM] * n_bg, scratch_shapes=list(scratch) + sems)
    res = pl.pallas_call(
        wrapped, grid_spec=grid_spec, out_shape=out_shape + landed,
        compiler_params=pltpu.CompilerParams(dimension_semantics=("arbitrary",) * (lead + 1), vmem_limit_bytes=VMEM_LIMIT),
        name=name)(*tables, *args, *behind)
    return res if many or n_bg else res[0]


def _ret_row_decays(lg, i, j, blk):
    row = lax.broadcasted_iota(jnp.int32, (blk, RET_DK), 0)
    return jnp.exp(lg * row.astype(F32)), jnp.exp(lg * ((i - j) * blk - row).astype(F32))


def _ret_scaled(lg, i, j, blk, q_ref, k_ref):
    a, b = _ret_row_decays(lg, i, j, blk)
    return (q_ref[...] * a).astype(BF16), (k_ref[...] * b).astype(BF16)


def _ret_log_gamma():
    return jnp.log1p(-(2.0 ** (-5.0 - jnp.arange(RET_HEADS, dtype=F32))))


def _ret_fwd(qr, kr, proj, t, gather=()):
    blk = _seq_block(t)
    nb = t // blk
    v_off = (2 * RET_HEADS * RET_DK) // RET_DV

    def body(i, j, lg_ref, q_ref, k_ref, v_ref, o_ref, acc):
        h = pl.program_id(0)

        @pl.when(j == 0)
        def _():
            acc[...] = jnp.zeros_like(acc)

        @pl.when(j < i)
        def _():
            qa, kb = _ret_scaled(lg_ref[h], i, j, blk, q_ref, k_ref)
            p = lax.dot_general(qa, kb, _DIMS["nt"], preferred_element_type=F32).astype(BF16)
            acc[...] += jnp.dot(p, v_ref[...].astype(BF16), preferred_element_type=F32)

        @pl.when(j == i)
        def _():
            s = lax.dot_general(q_ref[...].astype(BF16), k_ref[...].astype(BF16), _DIMS["nt"], preferred_element_type=F32)
            p = (s * _ret_decay(lg_ref[h], i, j, blk)).astype(BF16)
            o_ref[...] = acc[...] + jnp.dot(p, v_ref[...].astype(BF16), preferred_element_type=F32)

    return _pair_call(
        body, (RET_HEADS,), nb, False,
        [pl.BlockSpec(memory_space=pltpu.SMEM),
         pl.BlockSpec((blk, RET_DK), lambda h, i, j: (i, h)),
         pl.BlockSpec((blk, RET_DK), lambda h, i, j: (j, h)),
         pl.BlockSpec((blk, RET_DV), lambda h, i, j: (j, v_off + h))],
        pl.BlockSpec((blk, RET_DV), lambda h, i, j: (i, h)),
        jax.ShapeDtypeStruct((t, RET_HEADS * RET_DV), F32), [pltpu.VMEM((blk, RET_DV), F32)],
        "ret_fwd", (_ret_log_gamma(), qr, kr, proj), gather)


def _ret_bwd(qr, kr, proj, dr, t, exchange=()):
    blk = _seq_block(t)
    nb = t // blk
    v_off = (2 * RET_HEADS * RET_DK) // RET_DV

    def dq_body(i, j, lg_ref, q_ref, k_ref, v_ref, do_ref, dq_ref, acc):
        h = pl.program_id(0)

        @pl.when(j == 0)
        def _():
            acc[...] = jnp.zeros_like(acc)

        @pl.when(j < i)
        def _():
            a, b = _ret_row_decays(lg_ref[h], i, j, blk)
            ds = lax.dot_general(do_ref[...].astype(BF16), v_ref[...].astype(BF16), _DIMS["nt"], preferred_element_type=F32)
            acc[...] += a * jnp.dot(ds.astype(BF16), (k_ref[...] * b).astype(BF16), preferred_element_type=F32)

        @pl.when(j == i)
        def _():
            ds = lax.dot_general(do_ref[...].astype(BF16), v_ref[...].astype(BF16), _DIMS["nt"], preferred_element_type=F32)
            dsm = (ds * _ret_decay(lg_ref[h], i, j, blk)).astype(BF16)
            dq_ref[...] = acc[...] + jnp.dot(dsm, k_ref[...].astype(BF16), preferred_element_type=F32)

    dq = _pair_call(
        dq_body, (RET_HEADS,), nb, False,
        [pl.BlockSpec(memory_space=pltpu.SMEM),
         pl.BlockSpec((blk, RET_DK), lambda h, i, j: (i, h)),
         pl.BlockSpec((blk, RET_DK), lambda h, i, j: (j, h)),
         pl.BlockSpec((blk, RET_DV), lambda h, i, j: (j, v_off + h)),
         pl.BlockSpec((blk, RET_DV), lambda h, i, j: (i, h))],
        pl.BlockSpec((blk, RET_DK), lambda h, i, j: (i, h)),
        jax.ShapeDtypeStruct((t, RET_HEADS * RET_DK), F32), [pltpu.VMEM((blk, RET_DK), F32)],
        "ret_bwd_dq", (_ret_log_gamma(), qr, kr, proj, dr))

    def dkv_body(i, j, lg_ref, q_ref, k_ref, v_ref, do_ref, dk_ref, dv_ref, acc_k, acc_v):
        h = pl.program_id(0)

        @pl.when(i == j)
        def _():
            acc_k[...] = jnp.zeros_like(acc_k)
            acc_v[...] = jnp.zeros_like(acc_v)

        @pl.when(i > j)
        def _():
            a, b = _ret_row_decays(lg_ref[h], i, j, blk)
            qa, kb = (q_ref[...] * a).astype(BF16), (k_ref[...] * b).astype(BF16)
            do = do_ref[...].astype(BF16)
            p = lax.dot_general(qa, kb, _DIMS["nt"], preferred_element_type=F32).astype(BF16)
            acc_v[...] += lax.dot_general(p, do, _DIMS["tn"], preferred_element_type=F32)
            ds = lax.dot_general(do, v_ref[...].astype(BF16), _DIMS["nt"], preferred_element_type=F32).astype(BF16)
            acc_k[...] += b * lax.dot_general(ds, qa, _DIMS["tn"], preferred_element_type=F32)

        @pl.when(i == j)
        def _():
            q = q_ref[...].astype(BF16)
            do = do_ref[...].astype(BF16)
            decay = _ret_decay(lg_ref[h], i, j, blk)
            s = lax.dot_general(q, k_ref[...].astype(BF16), _DIMS["nt"], preferred_element_type=F32)
            acc_v[...] += lax.dot_general((s * decay).astype(BF16), do, _DIMS["tn"], preferred_element_type=F32)
            ds = lax.dot_general(do, v_ref[...].astype(BF16), _DIMS["nt"], preferred_element_type=F32)
            acc_k[...] += lax.dot_general((ds * decay).astype(BF16), q, _DIMS["tn"], preferred_element_type=F32)

        @pl.when(i == nb - 1)
        def _():
            dk_ref[...] = acc_k[...]
            dv_ref[...] = acc_v[...].astype(BF16)

    dk, dv, *landed = _pair_call(
        dkv_body, (RET_HEADS,), nb, True,
        [pl.BlockSpec(memory_space=pltpu.SMEM),
         pl.BlockSpec((blk, RET_DK), lambda h, j, i: (i, h)),
         pl.BlockSpec((blk, RET_DK), lambda h, j, i: (j, h)),
         pl.BlockSpec((blk, RET_DV), lambda h, j, i: (j, v_off + h)),
         pl.BlockSpec((blk, RET_DV), lambda h, j, i: (i, h))],
        [pl.BlockSpec((blk, RET_DK), lambda h, j, i: (j, h)), pl.BlockSpec((blk, RET_DV), lambda h, j, i: (j, h))],
        [jax.ShapeDtypeStruct((t, RET_HEADS * RET_DK), F32), jax.ShapeDtypeStruct((t, RET_HEADS * RET_DV), BF16)],
        [pltpu.VMEM((blk, RET_DK), F32), pltpu.VMEM((blk, RET_DV), F32)],
        "ret_bwd_dkv", (_ret_log_gamma(), qr, kr, proj, dr), exchange=exchange)
    return dq, dk, dv, landed


def _cumsum(x, reverse, name):
    t = x.shape[0]
    blk = _seq_block(t)
    nb = t // blk

    def body(x_ref, o_ref, carry):
        @pl.when(pl.program_id(0) == 0)
        def _():
            carry[...] = jnp.zeros_like(carry)

        r = lax.broadcasted_iota(jnp.int32, (blk, blk), 0)
        c = lax.broadcasted_iota(jnp.int32, (blk, blk), 1)
        tri = ((r <= c) if reverse else (r >= c)).astype(F32)
        o_ref[...] = jnp.dot(tri, x_ref[...], precision=HIGHEST, preferred_element_type=F32) + carry[...]
        carry[...] = o_ref[0:1, :] if reverse else o_ref[blk - 1:blk, :]

    idx = (lambda i: (nb - 1 - i, 0)) if reverse else (lambda i: (i, 0))
    return pl.pallas_call(
        body, grid=(nb,), in_specs=[pl.BlockSpec((blk, LANES), idx)], out_specs=pl.BlockSpec((blk, LANES), idx),
        out_shape=jax.ShapeDtypeStruct((t, LANES), F32), scratch_shapes=[pltpu.VMEM((1, LANES), F32)],
        compiler_params=pltpu.CompilerParams(dimension_semantics=("arbitrary",), vmem_limit_bytes=VMEM_LIMIT),
        name=name)(x)


def _ssd_decay(a_ref, at_ref, hh, mask):
    return jnp.exp(jnp.where(mask, a_ref[:, hh:hh + 1] - at_ref[hh:hh + 1, :], -jnp.inf))


def _head_lanes(s):
    lane = lax.broadcasted_iota(jnp.int32, (s.shape[0], LANES), 1)
    return jnp.concatenate([jnp.where(lane < SSM_P, s[:, 2 * p:2 * p + 1], s[:, 2 * p + 1:2 * p + 2])
                            for p in range(SSM_HEADS // 2)], axis=1)


_B_OFF, _C_OFF = 1024 // SSM_N, 1024 // SSM_N + SSM_GROUPS


def _ssd_fwd(xbc_act, xdt, a_cum, a_cum_t, t, gather=()):
    blk = _seq_block(t)
    nb = t // blk

    def body(i, j, c0, c1, b0, b1, x_ref, ai_ref, aj_ref, at_ref, o_ref, acc):
        @pl.when(j == 0)
        def _():
            acc[...] = jnp.zeros_like(acc)

        @pl.when(j < i)
        def _():
            xv = (x_ref[...] * _head_lanes(jnp.exp(ai_ref[0:1, :] - aj_ref[...]))).astype(BF16)
            for g, (c_ref, b_ref) in enumerate(((c0, b0), (c1, b1))):
                cb = lax.dot_general(c_ref[...].astype(BF16), b_ref[...].astype(BF16), _DIMS["nt"], preferred_element_type=F32)
                cols = slice(g * half, (g + 1) * half)
                acc[:, cols] += jnp.dot(cb.astype(BF16), xv[:, cols], preferred_element_type=F32)

        @pl.when(j == i)
        def _():
            o_ref[...] = acc[...] * _head_lanes(jnp.exp(ai_ref[...] - ai_ref[0:1, :]))
            mask = _causal_diff(i, j, blk) >= 0
            for g, (c_ref, b_ref) in enumerate(((c0, b0), (c1, b1))):
                cb = lax.dot_general(c_ref[...].astype(BF16), b_ref[...].astype(BF16), _DIMS["nt"], preferred_element_type=F32)
                for h in range(SSM_HG):
                    hh = g * SSM_HG + h
                    cols = slice(hh * SSM_P, (hh + 1) * SSM_P)
                    m = (cb * _ssd_decay(ai_ref, at_ref, hh, mask)).astype(BF16)
                    o_ref[:, cols] += jnp.dot(m, x_ref[:, cols].astype(BF16), preferred_element_type=F32)

    half = SSM_HG * SSM_P
    row_i = lambda off: pl.BlockSpec((blk, SSM_N), lambda i, j, off=off: (i, off))
    row_j = lambda off: pl.BlockSpec((blk, SSM_N), lambda i, j, off=off: (jnp.minimum(j, i), off))
    return _pair_call(
        body, (), nb, False,
        [row_i(_C_OFF), row_i(_C_OFF + 1), row_j(_B_OFF), row_j(_B_OFF + 1),
         pl.BlockSpec((blk, SSM_HEADS * SSM_P), lambda i, j: (j, 0)),
         pl.BlockSpec((blk, LANES), lambda i, j: (i, 0)),
         pl.BlockSpec((blk, LANES), lambda i, j: (j, 0)),
         pl.BlockSpec((SSM_HEADS, blk), lambda i, j: (0, j))],
        pl.BlockSpec((blk, SSM_HEADS * SSM_P), lambda i, j: (i, 0)),
        jax.ShapeDtypeStruct((t, SSM_HEADS * SSM_P), F32), [pltpu.VMEM((blk, SSM_HEADS * SSM_P), F32)],
        "ssd_fwd", (xbc_act, xbc_act, xbc_act, xbc_act, xdt, a_cum, a_cum, a_cum_t), gather)


def _ssd_bwd(xbc_act, xdt, a_cum, a_cum_t, dy, expand, t, exchange=()):
    blk = _seq_block(t)
    nb = t // blk
    width = SSM_HEADS * SSM_P

    half = SSM_HG * SSM_P

    def head_sums(prod, e_ref):
        return lax.dot_general(prod, e_ref[...], _DIMS["nt"], precision=HIGHEST, preferred_element_type=F32)

    def q_body(i, j, c0, c1, b0, b1, x_ref, ai_ref, aj_ref, at_ref, dy_ref, e_ref, dc_ref, da_ref, acc_c, acc_a, acc_p, dyu):
        @pl.when(j == 0)
        def _():
            acc_c[...] = jnp.zeros_like(acc_c)
            acc_a[...] = jnp.zeros_like(acc_a)
            acc_p[...] = jnp.zeros_like(acc_p)
            dyu[...] = (dy_ref[...].astype(F32) * _head_lanes(jnp.exp(ai_ref[...] - ai_ref[0:1, :]))).astype(BF16)

        @pl.when(j < i)
        def _():
            xv = (x_ref[...] * _head_lanes(jnp.exp(ai_ref[0:1, :] - aj_ref[...]))).astype(BF16)
            for g, (c_ref, b_ref) in enumerate(((c0, b0), (c1, b1))):
                cols = slice(g * half, (g + 1) * half)
                bj = b_ref[...].astype(BF16)
                cb = lax.dot_general(c_ref[...].astype(BF16), bj, _DIMS["nt"], preferred_element_type=F32).astype(BF16)
                dcb = lax.dot_general(dyu[:, cols], xv[:, cols], _DIMS["nt"], preferred_element_type=F32)
                acc_c[:, g * SSM_N:(g + 1) * SSM_N] += jnp.dot(dcb.astype(BF16), bj, preferred_element_type=F32)
                acc_p[:, cols] += dyu[:, cols].astype(F32) * jnp.dot(cb, xv[:, cols], preferred_element_type=F32)

        @pl.when(j == i)
        def _():
            mask = _causal_diff(i, j, blk) >= 0
            for g, (c_ref, b_ref) in enumerate(((c0, b0), (c1, b1))):
                bj = b_ref[...].astype(BF16)
                cb = lax.dot_general(c_ref[...].astype(BF16), bj, _DIMS["nt"], preferred_element_type=F32)
                dcb = jnp.zeros((blk, blk), F32)
                for h in range(SSM_HG):
                    hh = g * SSM_HG + h
                    cols = slice(hh * SSM_P, (hh + 1) * SSM_P)
                    dm = lax.dot_general(dy_ref[:, cols].astype(BF16), x_ref[:, cols].astype(BF16), _DIMS["nt"],
                                         preferred_element_type=F32) * _ssd_decay(ai_ref, at_ref, hh, mask)
                    dcb = dcb + dm
                    acc_a[:, hh:hh + 1] += jnp.sum(dm * cb, axis=1, keepdims=True)
                acc_c[:, g * SSM_N:(g + 1) * SSM_N] += jnp.dot(dcb.astype(BF16), bj, preferred_element_type=F32)
            dc_ref[...] = acc_c[...]
            da_ref[...] = acc_a[...] + head_sums(acc_p[...], e_ref)

    row_i = lambda off: pl.BlockSpec((blk, SSM_N), lambda i, j, off=off: (i, off))
    row_j = lambda off: pl.BlockSpec((blk, SSM_N), lambda i, j, off=off: (jnp.minimum(j, i), off))
    e_spec = pl.BlockSpec((LANES, width), lambda i, j: (0, 0))
    dc, da_q = _pair_call(
        q_body, (), nb, False,
        [row_i(_C_OFF), row_i(_C_OFF + 1), row_j(_B_OFF), row_j(_B_OFF + 1),
         pl.BlockSpec((blk, width), lambda i, j: (j, 0)),
         pl.BlockSpec((blk, LANES), lambda i, j: (i, 0)),
         pl.BlockSpec((blk, LANES), lambda i, j: (j, 0)),
         pl.BlockSpec((SSM_HEADS, blk), lambda i, j: (0, j)),
         pl.BlockSpec((blk, width), lambda i, j: (i, 0)), e_spec],
        [pl.BlockSpec((blk, SSM_GROUPS * SSM_N), lambda i, j: (i, 0)), pl.BlockSpec((blk, LANES), lambda i, j: (i, 0))],
        [jax.ShapeDtypeStruct((t, SSM_GROUPS * SSM_N), F32), jax.ShapeDtypeStruct((t, LANES), F32)],
        [pltpu.VMEM((blk, SSM_GROUPS * SSM_N), F32), pltpu.VMEM((blk, LANES), F32), pltpu.VMEM((blk, width), F32),
         pltpu.VMEM((blk, width), BF16)],
        "ssd_bwd_q", (xbc_act, xbc_act, xbc_act, xbc_act, xdt, a_cum, a_cum, a_cum_t, dy, expand))

    def k_body(i, j, c0, c1, b0, b1, x_ref, ai_ref, aj_ref, at_ref, dy_ref, e_ref, db_ref, dx_ref, da_ref, dat_ref, acc_b, acc_x, acc_a,
               acc_p):
        @pl.when(i == j)
        def _():
            acc_b[...] = jnp.zeros_like(acc_b)
            acc_x[...] = jnp.zeros_like(acc_x)
            acc_a[...] = jnp.zeros_like(acc_a)
            acc_p[...] = jnp.zeros_like(acc_p)

        @pl.when(i > j)
        def _():
            v = _head_lanes(jnp.exp(ai_ref[0:1, :] - aj_ref[...]))
            dyu_all = (dy_ref[...].astype(F32) * _head_lanes(jnp.exp(ai_ref[...] - ai_ref[0:1, :]))).astype(BF16)
            xv = (x_ref[...] * v).astype(BF16)
            for g, (c_ref, b_ref) in enumerate(((c0, b0), (c1, b1))):
                cols = slice(g * half, (g + 1) * half)
                ci = c_ref[...].astype(BF16)
                cb = lax.dot_general(ci, b_ref[...].astype(BF16), _DIMS["nt"], preferred_element_type=F32).astype(BF16)
                dcb = lax.dot_general(dyu_all[:, cols], xv[:, cols], _DIMS["nt"], preferred_element_type=F32)
                acc_b[:, g * SSM_N:(g + 1) * SSM_N] += lax.dot_general(dcb.astype(BF16), ci, _DIMS["tn"], preferred_element_type=F32)
                dxv = lax.dot_general(cb, dyu_all[:, cols], _DIMS["tn"], preferred_element_type=F32)
                acc_x[:, cols] += v[:, cols] * dxv
                acc_p[:, cols] += xv[:, cols].astype(F32) * dxv

        @pl.when(i == j)
        def _():
            mask = _causal_diff(i, j, blk) >= 0
            for g, (c_ref, b_ref) in enumerate(((c0, b0), (c1, b1))):
                ci = c_ref[...].astype(BF16)
                cb = lax.dot_general(ci, b_ref[...].astype(BF16), _DIMS["nt"], preferred_element_type=F32)
                dcb = jnp.zeros((blk, blk), F32)
                for h in range(SSM_HG):
                    hh = g * SSM_HG + h
                    cols = slice(hh * SSM_P, (hh + 1) * SSM_P)
                    decay = _ssd_decay(ai_ref, at_ref, hh, mask)
                    dyh = dy_ref[:, cols].astype(BF16)
                    acc_x[:, cols] += lax.dot_general((cb * decay).astype(BF16), dyh, _DIMS["tn"], preferred_element_type=F32)
                    dm = lax.dot_general(dyh, x_ref[:, cols].astype(BF16), _DIMS["nt"], preferred_element_type=F32) * decay
                    dcb = dcb + dm
                    acc_a[hh:hh + 1, :] += jnp.sum(dm * cb, axis=0, keepdims=True)
                acc_b[:, g * SSM_N:(g + 1) * SSM_N] += lax.dot_general(dcb.astype(BF16), ci, _DIMS["tn"], preferred_element_type=F32)

        @pl.when(i == nb - 1)
        def _():
            db_ref[...] = acc_b[...]
            dx_ref[...] = acc_x[...]
            dat_ref[...] = acc_a[...]
            da_ref[...] = head_sums(acc_p[...], e_ref)

    rowk_i = lambda off: pl.BlockSpec((blk, SSM_N), lambda j, i, off=off: (jnp.maximum(i, j), off))
    rowk_j = lambda off: pl.BlockSpec((blk, SSM_N), lambda j, i, off=off: (j, off))
    db, dx, da_k, da_k_t, *landed = _pair_call(
        k_body, (), nb, True,
        [rowk_i(_C_OFF), rowk_i(_C_OFF + 1), rowk_j(_B_OFF), rowk_j(_B_OFF + 1),
         pl.BlockSpec((blk, width), lambda j, i: (j, 0)),
         pl.BlockSpec((blk, LANES), lambda j, i: (i, 0)),
         pl.BlockSpec((blk, LANES), lambda j, i: (j, 0)),
         pl.BlockSpec((SSM_HEADS, blk), lambda j, i: (0, j)),
         pl.BlockSpec((blk, width), lambda j, i: (i, 0)), e_spec],
        [pl.BlockSpec((blk, SSM_GROUPS * SSM_N), lambda j, i: (j, 0)), pl.BlockSpec((blk, width), lambda j, i: (j, 0)),
         pl.BlockSpec((blk, LANES), lambda j, i: (j, 0)), pl.BlockSpec((SSM_HEADS, blk), lambda j, i: (0, j))],
        [jax.ShapeDtypeStruct((t, SSM_GROUPS * SSM_N), F32), jax.ShapeDtypeStruct((t, width), F32),
         jax.ShapeDtypeStruct((t, LANES), F32), jax.ShapeDtypeStruct((SSM_HEADS, t), F32)],
        [pltpu.VMEM((blk, SSM_GROUPS * SSM_N), F32), pltpu.VMEM((blk, width), F32), pltpu.VMEM((SSM_HEADS, blk), F32),
         pltpu.VMEM((blk, width), F32)],
        "ssd_bwd_k", (xbc_act, xbc_act, xbc_act, xbc_act, xdt, a_cum, a_cum, a_cum_t, dy, expand), exchange=exchange)
    return dc, da_q, db, dx, da_k, da_k_t, landed


S5_SUPER = 4
S5_NARROW, S5_WIDE = S5_GROUPS * S5_GROUP // S5_SUPER, S5_GROUPS * S5_STATE // S5_SUPER


def _s5_tiles(t):
    tm = min(1024, t)
    narrow = lambda off: pl.BlockSpec((tm, S5_NARROW), lambda s, i, off=off: (i, s + off))
    wide = pl.BlockSpec((tm, S5_WIDE), lambda s, i: (i, s))
    weight = lambda shape: pl.BlockSpec((1,) + shape, lambda s, i: (s, 0, 0))
    params = pltpu.CompilerParams(dimension_semantics=("arbitrary", "arbitrary"), vmem_limit_bytes=VMEM_LIMIT)
    return tm, narrow, wide, weight, params


def _s5_expand(a, a_off, w, w_is_wide_first, name):
    t = a.shape[0]
    tm, narrow, wide, weight, params = _s5_tiles(t)

    def body(a_ref, w_ref, re_ref, im_ref):
        dims = _DIMS["nt"] if w_is_wide_first else _DIMS["nn"]
        o = lax.dot_general(a_ref[...].astype(BF16), w_ref[0].astype(BF16), dims, preferred_element_type=F32)
        re_ref[...] = o[:, :S5_WIDE]
        im_ref[...] = o[:, S5_WIDE:]

    return pl.pallas_call(
        body, grid=(S5_SUPER, t // tm), in_specs=[narrow(a_off), weight(w.shape[1:])], out_specs=[wide, wide],
        out_shape=[jax.ShapeDtypeStruct((t, S5_COLS), F32)] * 2, compiler_params=params, name=name)(a, w)


def _s5_contract(x_re, x_im, w, w_is_wide_first, name, res=None, out_dtype=F32):
    t = x_re.shape[0]
    tm, narrow, wide, weight, params = _s5_tiles(t)
    has_res = res is not None

    def body(*refs):
        re_ref, im_ref, w_ref, o_ref = refs[0], refs[1], refs[2], refs[-1]
        x = jnp.concatenate([re_ref[...].astype(BF16), im_ref[...].astype(BF16)], axis=1)
        dims = _DIMS["nn"] if w_is_wide_first else _DIMS["nt"]
        o = lax.dot_general(x, w_ref[0].astype(BF16), dims, preferred_element_type=F32)
        o_ref[...] = (o + refs[3][...] if has_res else o).astype(out_dtype)

    return pl.pallas_call(
        body, grid=(S5_SUPER, t // tm), in_specs=[wide, wide, weight(w.shape[1:])] + ([narrow(0)] if has_res else []),
        out_specs=narrow(0), out_shape=jax.ShapeDtypeStruct((t, S5_SUPER * S5_NARROW), out_dtype), compiler_params=params,
        name=name)(*([x_re, x_im, w] + ([res] if has_res else [])))


def _s5_wgrad(a, a_off, x_re, x_im, wide_first, name):
    t = a.shape[0]
    tm, narrow, wide, weight, params = _s5_tiles(t)
    shape = (2 * S5_WIDE, S5_NARROW) if wide_first else (S5_NARROW, 2 * S5_WIDE)

    def body(a_ref, re_ref, im_ref, o_ref):
        @pl.when(pl.program_id(1) == 0)
        def _():
            o_ref[...] = jnp.zeros_like(o_ref)

        x = jnp.concatenate([re_ref[...].astype(BF16), im_ref[...].astype(BF16)], axis=1)
        av = a_ref[...].astype(BF16)
        o_ref[0] += lax.dot_general(x, av, _DIMS["tn"], preferred_element_type=F32) if wide_first else \
            lax.dot_general(av, x, _DIMS["tn"], preferred_element_type=F32)

    return pl.pallas_call(
        body, grid=(S5_SUPER, t // tm), in_specs=[narrow(a_off), wide, wide], out_specs=weight(shape),
        out_shape=jax.ShapeDtypeStruct((S5_SUPER,) + shape, F32), compiler_params=params, name=name)(a, x_re, x_im)


def _s5_scan(b_re, b_im, a_re, a_im, reverse, x_prev=None):
    t = b_re.shape[0]
    cb = 512
    ncb = S5_COLS // cb
    tb = min(1024, t)
    ntb = t // tb
    sub = 8
    shape = (sub, cb)

    def cmul(ar, ai, br, bi):
        return ar * br - ai * bi, ar * bi + ai * br

    def body(*refs):
        if reverse:
            br_ref, bi_ref, ar_ref, ai_ref, xr_ref, xi_ref, or_ref, oi_ref, dar_ref, dai_ref, carry, dacc, states = refs
        else:
            br_ref, bi_ref, ar_ref, ai_ref, or_ref, oi_ref, or_bf_ref, oi_bf_ref, carry = refs

        @pl.when(pl.program_id(1) == 0)
        def _():
            carry[...] = jnp.zeros_like(carry)
            if reverse:
                dacc[...] = jnp.zeros_like(dacc)

        row = lax.broadcasted_iota(jnp.int32, shape, 0)
        a1 = (jnp.broadcast_to(ar_ref[...], shape), jnp.broadcast_to(-ai_ref[...] if reverse else ai_ref[...], shape))
        a2 = cmul(*a1, *a1)
        a4 = cmul(*a2, *a2)
        a8 = cmul(*a4, *a4)
        steps = (sub - row) if reverse else (row + 1)
        pw = (jnp.ones(shape, F32), jnp.zeros(shape, F32))
        for bit, p in ((1, a1), (2, a2), (4, a4), (8, a8)):
            on = (steps & bit) != 0
            pw = cmul(*pw, jnp.where(on, p[0], 1.0), jnp.where(on, p[1], 0.0))
        edge = 0 if reverse else sub - 1

        def shift(v, s):
            if reverse:
                return jnp.where(row < sub - s, pltpu.roll(v, sub - s, 0), 0.0)
            return jnp.where(row >= s, pltpu.roll(v, s, 0), 0.0)

        def tile(n, state):
            r0 = pl.multiple_of((tb // sub - 1 - n if reverse else n) * sub, sub)
            rows = pl.ds(r0, sub)
            xr, xi = br_ref[rows, :], bi_ref[rows, :]
            for s, p in ((1, a1), (2, a2), (4, a4)):
                dr, di = cmul(*p, shift(xr, s), shift(xi, s))
                xr, xi = xr + dr, xi + di
            cr, ci = jnp.broadcast_to(state[0], shape), jnp.broadcast_to(state[1], shape)
            dr, di = cmul(*pw, cr, ci)
            xr, xi = xr + dr, xi + di
            if reverse:
                states[0, rows, :], states[1, rows, :] = xr, xi
                gr = jnp.where(row < sub - 1, pltpu.roll(xr, sub - 1, 0), cr)
                gi = jnp.where(row < sub - 1, pltpu.roll(xi, sub - 1, 0), ci)
                pr, pi = xr_ref[rows, :], xi_ref[rows, :]
                dacc[0] += gr * pr + gi * pi
                dacc[1] += gi * pr - gr * pi
            else:
                or_ref[rows, :], oi_ref[rows, :] = xr, xi
            return (jnp.sum(jnp.where(row == edge, xr, 0.0), axis=0, keepdims=True),
                    jnp.sum(jnp.where(row == edge, xi, 0.0), axis=0, keepdims=True))

        last = lax.fori_loop(0, tb // sub, tile, (carry[0], carry[1]))
        carry[0], carry[1] = last
        if reverse:
            or_ref[...], oi_ref[...] = states[0].astype(BF16), states[1].astype(BF16)
            dar_ref[...] = jnp.sum(dacc[0], axis=0, keepdims=True)
            dai_ref[...] = jnp.sum(dacc[1], axis=0, keepdims=True)
        else:
            or_bf_ref[...], oi_bf_ref[...] = or_ref[...].astype(BF16), oi_ref[...].astype(BF16)

    tblock = (lambda k: ntb - 1 - k) if reverse else (lambda k: k)
    re_spec = pl.BlockSpec((tb, cb), lambda j, k: (tblock(k), j))
    a_spec = pl.BlockSpec((1, cb), lambda j, k: (0, j))
    ins, in_specs = [b_re, b_im, a_re, a_im], [re_spec, re_spec, a_spec, a_spec]
    seq_bf = jax.ShapeDtypeStruct((t, S5_COLS), BF16)
    scratch = [pltpu.VMEM((2, 1, cb), F32)]
    if reverse:
        ins += list(x_prev)
        in_specs += [re_spec, re_spec]
        out_shape = [seq_bf, seq_bf] + [jax.ShapeDtypeStruct((1, S5_COLS), F32)] * 2
        out_specs = [re_spec, re_spec, a_spec, a_spec]
        scratch += [pltpu.VMEM((2, sub, cb), F32), pltpu.VMEM((2, tb, cb), F32)]
    else:
        out_shape = [jax.ShapeDtypeStruct((t, S5_COLS), F32)] * 2 + [seq_bf, seq_bf]
        out_specs = [re_spec] * 4
    return pl.pallas_call(
        body, grid=(ncb, ntb), in_specs=in_specs, out_specs=out_specs, out_shape=out_shape, scratch_shapes=scratch,
        compiler_params=pltpu.CompilerParams(dimension_semantics=("arbitrary", "arbitrary"), vmem_limit_bytes=VMEM_LIMIT),
        name="s5_scan_bwd" if reverse else "s5_scan_fwd")(*ins)


CONF_DIM = 512
CONF_K = 31
CONF_PAD = 32


def _conf_conv(proj_o, w, b, dc1=None):
    t = proj_o.shape[0]
    cb = LANES
    ncb = CONF_DIM // cb
    chunk = min(512, t)
    chunks = range(0, t, chunk)
    bwd = dc1 is not None

    def body(*refs):
        if bwd:
            ca_ref, cg_ref, w_ref, b_ref, dy_ref, dca_ref, dcg_ref, dw_ref, db_ref, xs, dys = refs
        else:
            ca_ref, cg_ref, w_ref, b_ref, o_ref, xs = refs
        xs[pl.ds(0, CONF_PAD), :] = jnp.zeros((CONF_PAD, cb), F32)
        for t0 in chunks:
            rows = pl.ds(t0, chunk)
            xs[pl.ds(CONF_PAD + t0, chunk), :] = ca_ref[rows, :] * jax.nn.sigmoid(cg_ref[rows, :])
        if not bwd:
            for t0 in chunks:
                acc = jnp.broadcast_to(b_ref[...], (chunk, cb))
                for k in range(CONF_K):
                    acc = acc + w_ref[k] * xs[pl.ds(CONF_PAD + t0 - (CONF_K - 1 - k), chunk), :]
                o_ref[pl.ds(t0, chunk), :] = acc
            return
        dys[pl.ds(t, CONF_PAD), :] = jnp.zeros((CONF_PAD, cb), F32)
        db = jnp.zeros((1, cb), F32)
        for t0 in chunks:
            dys[pl.ds(t0, chunk), :] = dy_ref[pl.ds(t0, chunk), :]
            db = db + jnp.sum(dy_ref[pl.ds(t0, chunk), :], axis=0, keepdims=True)
        db_ref[...] = db
        for t0 in chunks:
            rows = pl.ds(t0, chunk)
            acc = jnp.zeros((chunk, cb), F32)
            for k in range(CONF_K):
                acc = acc + w_ref[k] * dys[pl.ds(t0 + (CONF_K - 1 - k), chunk), :]
            sig = jax.nn.sigmoid(cg_ref[rows, :])
            dca_ref[rows, :] = (acc * sig).astype(BF16)
            dcg_ref[rows, :] = (acc * ca_ref[rows, :] * sig * (1.0 - sig)).astype(BF16)
        for k in range(CONF_K):
            dwk = jnp.zeros((1, cb), F32)
            for t0 in chunks:
                window = xs[pl.ds(CONF_PAD + t0 - (CONF_K - 1 - k), chunk), :]
                dwk = dwk + jnp.sum(dy_ref[pl.ds(t0, chunk), :] * window, axis=0, keepdims=True)
            dw_ref[k] = dwk

    col = lambda off: pl.BlockSpec((t, cb), lambda j, off=off: (0, j + off))
    w_spec = pl.BlockSpec((CONF_K, 1, cb), lambda j: (0, 0, j))
    b_spec = pl.BlockSpec((1, cb), lambda j: (0, j))
    seq = jax.ShapeDtypeStruct((t, CONF_DIM), F32)
    ins, in_specs = [proj_o, proj_o, w, b], [col(0), col(ncb), w_spec, b_spec]
    scratch = [pltpu.VMEM((CONF_PAD + t, cb), F32)]
    if bwd:
        ins, in_specs = ins + [dc1], in_specs + [col(0)]
        seq_bf = jax.ShapeDtypeStruct((t, CONF_DIM), BF16)
        out_shape, out_specs = [seq_bf, seq_bf, jax.ShapeDtypeStruct(w.shape, F32), jax.ShapeDtypeStruct(b.shape, F32)], [col(0), col(0), w_spec, b_spec]
        scratch = scratch + [pltpu.VMEM((t + CONF_PAD, cb), F32)]
    else:
        out_shape, out_specs = seq, col(0)
    return pl.pallas_call(
        body, grid=(ncb,), in_specs=in_specs, out_specs=out_specs, out_shape=out_shape, scratch_shapes=scratch,
        compiler_params=pltpu.CompilerParams(dimension_semantics=("arbitrary",), vmem_limit_bytes=VMEM_LIMIT),
        name="conf_conv_bwd" if bwd else "conf_conv")(*ins)


N_CHIPS = 4
_HBM = pl.BlockSpec(memory_space=pl.ANY)
_MESH_ID = pl.DeviceIdType.MESH


def _comm_call(body, srcs, out_shapes, n_sems, name):
    n = len(srcs)
    return pl.pallas_call(
        body, out_shape=out_shapes, in_specs=[_HBM] * n, out_specs=[_HBM] * n,
        scratch_shapes=[pltpu.SemaphoreType.DMA((n_sems,)), pltpu.SemaphoreType.DMA((n_sems,)), pltpu.SemaphoreType.DMA((n,))],
        compiler_params=pltpu.CompilerParams(has_side_effects=True), name=name)(*srcs)


GATHER_SEMS = N_DEV - 1


def _gather_phases(src_refs, out_refs, send_sems, recv_sems, local_sems):
    n = len(src_refs)
    x, y, c = lax.axis_index("x"), lax.axis_index("y"), lax.axis_index("c")
    me, sibling = (x, y, c), (x, y, 1 - c)
    chips = [(1 - x, y), (x, 1 - y), (1 - x, 1 - y)]

    def copy(i, k, block, to, from_src=False):
        rows = out_refs[i].at[4 * block[0] + 2 * block[1] + block[2]]
        return pltpu.make_async_remote_copy(
            src_ref=src_refs[i] if from_src else rows, dst_ref=rows, send_sem=send_sems.at[GATHER_SEMS * i + k],
            recv_sem=recv_sems.at[GATHER_SEMS * i + k], device_id=to, device_id_type=_MESH_ID)

    def local(i):
        return pltpu.make_async_copy(src_refs[i], out_refs[i].at[4 * x + 2 * y + c], local_sems.at[i])

    def first(i):
        return [copy(i, 0, me, sibling, True)] + [copy(i, 1 + j, me, (*chip, c), True) for j, chip in enumerate(chips)]

    def passed(i, j):
        return copy(i, 4 + j, (*chips[j], c), sibling)

    def start():
        for i in range(n):
            local(i).start()
            for cp in first(i):
                cp.start()

    def forward():
        for j, chip in enumerate(chips):
            for i in range(n):
                copy(i, 1 + j, (*chip, c), me).wait_recv()
                passed(i, j).start()

    def finish():
        for i in range(n):
            copy(i, 0, sibling, me).wait_recv()
            for j, chip in enumerate(chips):
                copy(i, 4 + j, (*chip, 1 - c), me).wait_recv()
        for i in range(n):
            for cp in first(i) + [passed(i, j) for j in range(len(chips))]:
                cp.wait_send()
            local(i).wait()

    return start, forward, finish


def _gather(srcs, name):
    n = len(srcs)

    def body(*refs):
        for phase in _gather_phases(refs[:n], refs[n:2 * n], *refs[2 * n:]):
            phase()

    return _comm_call(body, srcs, [jax.ShapeDtypeStruct((N_DEV,) + s.shape, s.dtype) for s in srcs], GATHER_SEMS * n, name)


def _swap_sibling(srcs, name):
    n = len(srcs)

    def body(*refs):
        src_refs, out_refs = refs[:n], refs[n:2 * n]
        send_sems, recv_sems, _ = refs[2 * n:]
        sibling = (lax.axis_index("x"), lax.axis_index("y"), 1 - lax.axis_index("c"))
        copies = [pltpu.make_async_remote_copy(src_ref=src_refs[i], dst_ref=out_refs[i], send_sem=send_sems.at[i],
                                               recv_sem=recv_sems.at[i], device_id=sibling, device_id_type=_MESH_ID) for i in range(n)]
        for cp in copies:
            cp.start()
        for cp in copies:
            cp.wait_recv()
        for cp in copies:
            cp.wait_send()

    return _comm_call(body, srcs, [jax.ShapeDtypeStruct(s.shape, s.dtype) for s in srcs], n, name)


EXCHANGE_SEMS = N_CHIPS - 1


def _exchange_phases(src_refs, out_refs, send_sems, recv_sems, local_sems):
    n = len(src_refs)
    x, y, c = lax.axis_index("x"), lax.axis_index("y"), lax.axis_index("c")
    mine = 2 * x + y

    def local(i):
        return pltpu.make_async_copy(src_refs[i].at[mine], out_refs[i].at[mine], local_sems.at[i])

    def copies(i, landing):
        out = []
        for k in range(1, N_CHIPS):
            px, py = x ^ (k >> 1), y ^ (k & 1)
            peer = 2 * px + py
            sem = EXCHANGE_SEMS * i + k - 1
            out.append(pltpu.make_async_remote_copy(
                src_ref=src_refs[i].at[peer], dst_ref=out_refs[i].at[peer if landing else mine], send_sem=send_sems.at[sem],
                recv_sem=recv_sems.at[sem], device_id=(px, py, c), device_id_type=_MESH_ID))
        return out

    def start():
        for i in range(n):
            local(i).start()
            for send in copies(i, False):
                send.start()

    def finish():
        for i in range(n):
            for recv in copies(i, True):
                recv.wait_recv()
        for i in range(n):
            for send in copies(i, False):
                send.wait_send()
            local(i).wait()

    return start, (lambda: None), finish


def _exchange_chips(srcs, name):
    n = len(srcs)

    def body(*refs):
        for phase in _exchange_phases(refs[:n], refs[n:2 * n], *refs[2 * n:]):
            phase()

    return _comm_call(body, srcs, [jax.ShapeDtypeStruct(s.shape, s.dtype) for s in srcs], EXCHANGE_SEMS * n, name)


def _add(a, b, name):
    k, rows, cols = a.shape
    tr = _row_tile(rows)

    def body(a_ref, b_ref, o_ref):
        o_ref[...] = (a_ref[...].astype(F32) + b_ref[...].astype(F32)).astype(o_ref.dtype)

    spec = pl.BlockSpec((k, tr, cols), lambda i: (0, i, 0))
    return pl.pallas_call(
        body, grid=(rows // tr,), in_specs=[spec, spec], out_specs=spec, out_shape=jax.ShapeDtypeStruct(a.shape, a.dtype),
        compiler_params=pltpu.CompilerParams(dimension_semantics=("parallel",), vmem_limit_bytes=VMEM_LIMIT), name=name)(a, b)


def _row_tile(r, pref=256):
    if r <= pref:
        return r
    t = pref // 16 * 16
    while r % t:
        t -= 16
    return t


def _join_cols(g, width, name):
    _, rows, ws = g.shape
    tr = _row_tile(rows)
    tail = width - N_DEV * ws

    def body(g_ref, o_ref):
        for d in range(N_DEV):
            o_ref[:, pl.ds(d * ws, ws)] = g_ref[d]
        if tail:
            o_ref[:, pl.ds(N_DEV * ws, tail)] = jnp.zeros((tr, tail), g.dtype)

    return pl.pallas_call(
        body, grid=(rows // tr,), in_specs=[pl.BlockSpec((N_DEV, tr, ws), lambda i: (0, i, 0))],
        out_specs=pl.BlockSpec((tr, width), lambda i: (i, 0)), out_shape=jax.ShapeDtypeStruct((rows, width), g.dtype),
        compiler_params=pltpu.CompilerParams(dimension_semantics=("parallel",), vmem_limit_bytes=VMEM_LIMIT), name=name)(g)


def _split_cols(full, ws, dtype, name):
    rows, width = full.shape
    tr = _row_tile(rows)

    def body(x_ref, o_ref):
        for d in range(N_DEV):
            o_ref[d] = x_ref[:, pl.ds(d * ws, ws)].astype(dtype)

    return pl.pallas_call(
        body, grid=(rows // tr,), in_specs=[pl.BlockSpec((tr, width), lambda i: (i, 0))],
        out_specs=pl.BlockSpec((N_DEV, tr, ws), lambda i: (0, i, 0)), out_shape=jax.ShapeDtypeStruct((N_DEV, rows, ws), dtype),
        compiler_params=pltpu.CompilerParams(dimension_semantics=("parallel",), vmem_limit_bytes=VMEM_LIMIT), name=name)(full)


def _adamw(parts, w, m, v, name):
    r, c = w.shape
    n_parts = parts.shape[0]
    tr = _row_tile(r)

    def body(p_ref, w_ref, m_ref, v_ref, g_ref, d_ref, nm_ref, nv_ref):
        g = p_ref[0].astype(F32)
        for s in range(1, n_parts):
            g = g + p_ref[s].astype(F32)
        nm = ADAM_B1 * m_ref[...] + (1.0 - ADAM_B1) * g
        nv = ADAM_B2 * v_ref[...] + (1.0 - ADAM_B2) * (g * g)
        m_hat = nm / (1.0 - ADAM_B1 ** ADAM_STEP)
        v_hat = nv / (1.0 - ADAM_B2 ** ADAM_STEP)
        g_ref[...] = g
        nm_ref[...] = nm
        nv_ref[...] = nv
        d_ref[...] = -ADAM_LR * (m_hat / (jnp.sqrt(v_hat) + ADAM_EPS) + ADAM_WD * w_ref[...])

    spec = pl.BlockSpec((tr, c), lambda i: (i, 0))
    return pl.pallas_call(
        body, grid=(r // tr,), in_specs=[pl.BlockSpec((n_parts, tr, c), lambda i: (0, i, 0)), spec, spec, spec],
        out_specs=[spec] * 4, out_shape=[jax.ShapeDtypeStruct((r, c), F32)] * 4,
        compiler_params=pltpu.CompilerParams(dimension_semantics=("parallel",), vmem_limit_bytes=VMEM_LIMIT),
        name=name)(parts, w, m, v)


def _pack_rows(n_elems, mult):
    rows = -(-n_elems // PACK_COLS)
    return -(-rows // mult) * mult


def _pack(arrays, dtype, mult, lead=()):
    flat = jnp.concatenate([a.astype(dtype).reshape(lead + (-1,)) for a in arrays], axis=-1)
    rows = _pack_rows(flat.shape[-1], mult)
    flat = jnp.pad(flat, [(0, 0)] * len(lead) + [(0, rows * PACK_COLS - flat.shape[-1])])
    return flat.reshape(lead + (rows, PACK_COLS))


def _unpack(buf, shapes, lead=()):
    flat = buf.reshape(lead + (-1,))
    out, off = [], 0
    for s in shapes:
        n = math.prod(s)
        out.append(flat[..., off:off + n].reshape(lead + tuple(s)))
        off += n
    return out


def _join_shards(piece, axis):
    moved = jnp.moveaxis(piece, 0, axis)
    shape = moved.shape
    return moved.reshape(shape[:axis] + (shape[axis] * shape[axis + 1],) + shape[axis + 2:])


def _split_shards(full, axis):
    shape = full.shape
    return jnp.moveaxis(full.reshape(shape[:axis] + (N_DEV, shape[axis] // N_DEV) + shape[axis + 1:]), axis, 0)


def _block_diag(blocks):
    g, r, c = blocks.shape
    k = g // S5_SUPER
    eye = jnp.eye(k, dtype=blocks.dtype)
    return (blocks.reshape(S5_SUPER, k, r, 1, c) * eye[None, :, None, :, None]).reshape(S5_SUPER, k * r, k * c)


def _diag_blocks(mat, r, c):
    k = mat.shape[1] // r
    eye = jnp.eye(k, dtype=mat.dtype)
    return jnp.sum(mat.reshape(S5_SUPER, k, r, k, c) * eye[None, :, None, :, None], axis=3).reshape(S5_SUPER * k, r, c)


def _pad_lanes(a):
    a = a.reshape(1, -1)
    return jnp.pad(a, ((0, 0), (0, LANES - a.shape[1])))


def _head_expand():
    h = jnp.arange(LANES)[:, None]
    ch = jnp.arange(SSM_HEADS * SSM_P)[None, :] // SSM_P
    return (h == ch).astype(F32)


def _rotary_tables(t):
    inv = 10000.0 ** (-jnp.arange(0, RET_DK, 2, dtype=F32) / RET_DK)
    ang = jnp.arange(t).astype(F32)[:, None] * inv[None, :]
    cos, sin = jnp.cos(ang), jnp.sin(ang)
    return jnp.concatenate([cos, cos], axis=1), jnp.concatenate([-sin, sin], axis=1)


def _rms_fwd(x, g, name):
    return _rows(_f_rms, x.shape[0], [(x, D_MODEL, 0)], [g], [D_MODEL], [], name, bf16_outs=(0,))[0]


def _rms_bwd(x, g, dh, dres, name):
    def fn(x_, dh_, dres_, g_):
        _, vjp = jax.vjp(lambda a, b: _rms(a, b), x_, g_)
        dx, dg = vjp(dh_)
        return dx + dres_, dx + dres_, dg
    return _rows(fn, x.shape[0], [(x, D_MODEL, 0), (dh, D_MODEL, 0), (dres, D_MODEL, 0)], [g], [D_MODEL, D_MODEL], [(1, D_MODEL)],
                 name, bf16_outs=(1,))


def _ffn_fwd(x, norm_g, w_up, dw_w, dw_b, w_down, tag):
    t = x.shape[0]
    nbk = D_FF // LANES
    h = _rms_fwd(x, norm_g, "ffn_norm_" + tag)
    up = _mm(h, w_up, "nn", "ffn_up_" + tag)
    mid = _cols(_f_ffnmid, nbk, [(up, 0), (up, nbk), (dw_w, 0), (dw_w, nbk), (dw_b, 0), (dw_b, nbk)], [(t,)], "ffn_mid_" + tag,
                bf16_outs=(0,))[0]
    out = _mm(mid, w_down, "nn", "ffn_down_" + tag, res=x)
    return out, (h, up, mid)


def _ffn_bwd(x, norm_g, w_up, dw_w, dw_b, w_down, saved, dout, dout_bf, tag):
    t = x.shape[0]
    nbk = D_FF // LANES
    h, up, mid = saved
    d_w_down = _mm(mid, dout_bf, "tn", "ffn_down_dw_" + tag, out_dtype=BF16)
    dmid = _mm(dout_bf, w_down, "nt", "ffn_down_dx_" + tag)
    dgin, duin, dwg, dwu, dbg, dbu = _cols(
        _grad_fn(_f_ffnmid, 6, 6), nbk,
        [(up, 0), (up, nbk), (dw_w, 0), (dw_w, nbk), (dw_b, 0), (dw_b, nbk), (dmid, 0)],
        [(t,), (t,), (3, 1), (3, 1), (1,), (1,)], "ffn_mid_bwd_" + tag, bf16_outs=(0, 1))
    dup = jnp.concatenate([dgin, duin], axis=1)
    d_w_up = _mm(h, dup, "tn", "ffn_up_dw_" + tag)
    dh = _mm(dup, w_up, "nt", "ffn_up_dx_" + tag)
    dx, dx_bf, dnorm = _rms_bwd(x, norm_g, dh, dout, "ffn_norm_bwd_" + tag)
    return dx, dx_bf, dict(norm=dnorm, w_up=d_w_up, dw_w=jnp.concatenate([dwg, dwu], axis=2)[:, 0],
                           dw_b=jnp.concatenate([dbg, dbu], axis=1), w_down=d_w_down)


def _local_step(x, tgt, w, late=None):
    t = x.shape[0]
    grads = {}
    expand = _head_expand()
    cosf, sins = _rotary_tables(t)
    mix_g = [w['mix_norm'][i:i + 1] for i in range(2)]
    ffn_g = [w['ffn_norm'][i:i + 1] for i in range(2)]
    ffn_dw_w = [w['ffn_dw_w'][i][:, None, :] for i in range(2)]
    ffn_dw_b = [w['ffn_dw_b'][i:i + 1] for i in range(2)]

    w_e = w['e_w_in']
    conv_w = w['e_conv_w'][0][:, None, :]
    conv_b = w['e_conv_b']
    dt_bias, a_log, d_skip = _pad_lanes(w['e_dt_bias']), _pad_lanes(w['e_a_log']), _pad_lanes(w['e_d'])
    h0 = _rms_fwd(x, mix_g[0], "mix_norm_0")
    proj = _mm(h0, w_e, "nn", "e_in")
    qr, kr = _rows(_f_retpre, t, [(proj, 512, 0), (proj, 512, 1), (cosf, LANES, 0), (sins, LANES, 0)], [], [512, 512], [], "ret_pre")
    if late:
        r, *landed = _ret_fwd(qr, kr, proj, t, late['ret_fwd'][0])
        late['ret_fwd'][1](w, landed)
    else:
        r = _ret_fwd(qr, kr, proj, t)
    y_ret = _rows(_f_retpost, t, [(r, 1024, 0), (proj, 1024, 2)], [], [1024], [], "ret_post", bf16_outs=(0,))[0]
    xbc_act = _cols(_f_ssdconv, 12, [(proj, 32), (conv_w, 0), (conv_b, 0)], [(t,)], "ssd_conv")[0]
    xdt, da = _rows(_f_ssdpre, t, [(xbc_act, 1024, 0), (proj, LANES, 44)], [dt_bias, a_log, expand], [1024, LANES], [], "ssd_pre")
    a_cum = _cumsum(da, False, "ssd_cumsum")
    a_cum_t = a_cum[:, :SSM_HEADS].T
    if late:
        yc, *landed = _ssd_fwd(xbc_act, xdt, a_cum, a_cum_t, t, late['ssd_fwd'][0])
        late['ssd_fwd'][1](w, landed)
    else:
        yc = _ssd_fwd(xbc_act, xdt, a_cum, a_cum_t, t)
    y_ssm = _rows(_f_ssdpost, t, [(yc, 1024, 0), (xbc_act, 1024, 0), (proj, 1024, 3)], [d_skip, w['e_ssm_norm'], expand],
                  [1024], [], "ssd_post", bf16_outs=(0,))[0]
    mix_e = jnp.concatenate([y_ret, y_ssm], axis=1)
    x1 = _mm(mix_e, w['e_w_out'], "nn", "e_out", res=x)
    x2, ffn0 = _ffn_fwd(x1, ffn_g[0], w['ffn_w_up'][0], ffn_dw_w[0], ffn_dw_b[0], w['ffn_w_down'][0], "0")

    lr, li = w['o_a_re'][0], w['o_a_im'][0]
    ls = w['o_log_step'].reshape(S5_GROUPS, 1)
    b_re3, b_im3 = jnp.transpose(w['o_b_re'][0], (2, 0, 1)), jnp.transpose(w['o_b_im'][0], (2, 0, 1))
    par_ins = [lr, li, ls, b_re3, b_im3]
    whole = lambda a: (a, a.shape, (lambda i, n=a.ndim: (0,) * n))
    par_shapes = [(S5_GROUPS, S5_STATE)] * 2 + [(S5_GROUP, S5_GROUPS, S5_STATE)] * 2
    ab_re, ab_im, bb_re, bb_im = _call(_f_s5par, (1,), [whole(a) for a in par_ins],
                                       [(s, F32, s, (lambda i, n=len(s): (0,) * n), False) for s in par_shapes], "s5_params")
    w_b = jnp.concatenate([_block_diag(jnp.transpose(bb_re, (1, 0, 2))), _block_diag(jnp.transpose(bb_im, (1, 0, 2)))], axis=2)
    w_c = jnp.concatenate([_block_diag(jnp.transpose(w['o_c_re'][0], (0, 2, 1))),
                           -_block_diag(jnp.transpose(w['o_c_im'][0], (0, 2, 1)))], axis=1)
    a_re, a_im = ab_re.reshape(1, S5_COLS), ab_im.reshape(1, S5_COLS)
    dw_w = w['o_dw_w'][0][:, None, :]
    glu_w = w['o_glu_w'].astype(F32)

    h1 = _rms_fwd(x2, mix_g[1], "mix_norm_1")
    proj_o = _mm(h1, w['o_w_in'], "nn", "o_in")
    c1 = _conf_conv(proj_o, dw_w, w['o_dw_b'])
    c2 = _rows(_f_confb, t, [(c1, 512, 0)], [w['o_ln_g'], w['o_ln_b']], [512], [], "conf_norm", bf16_outs=(0,))[0]
    u_off = 2 * CONF_DIM // S5_NARROW
    bu_re, bu_im = _s5_expand(proj_o, u_off, w_b, False, "s5_bu")
    xs_re, xs_im, xs_re_bf, xs_im_bf = _s5_scan(bu_re, bu_im, a_re, a_im, False)
    y_s5 = _s5_contract(xs_re_bf, xs_im_bf, w_c, True, "s5_cx")
    s_out = _rows(_f_s5post, t, [(y_s5, 512, 0), (proj_o, 512, 2)], [w['o_d'], glu_w], [512], [], "s5_post", bf16_outs=(0,))[0]
    mix_o = jnp.concatenate([c2, s_out], axis=1)
    x3 = _mm(mix_o, w['o_w_out'], "nn", "o_out", res=x2)
    x4, ffn1 = _ffn_fwd(x3, ffn_g[1], w['ffn_w_up'][1], ffn_dw_w[1], ffn_dw_b[1], w['ffn_w_down'][1], "1")

    dx4, dx4_bf, loss_blk, d_final = _rows(_loss_step, t, [(x4, D_MODEL, 0), (tgt, D_MODEL, 0)], [w['final_norm'].reshape(1, D_MODEL)],
                                           [D_MODEL, D_MODEL], [(8, LANES), (1, D_MODEL)], "loss_head", bf16_outs=(1,))
    loss = loss_blk[0, 0]
    grads['final_norm'] = d_final.reshape(D_MODEL)

    dx3, dx3_bf, g1 = _ffn_bwd(x3, ffn_g[1], w['ffn_w_up'][1], ffn_dw_w[1], ffn_dw_b[1], w['ffn_w_down'][1], ffn1, dx4, dx4_bf, "1")
    grads['o_w_out'] = _mm(mix_o, dx3_bf, "tn", "o_out_dw", out_dtype=BF16)
    dmix_o = _mm(dx3_bf, w['o_w_out'], "nt", "o_out_dx")
    dc1, d_ln_g, d_ln_b = _conf_norm_bwd(c1, w['o_ln_g'], w['o_ln_b'], dmix_o, t)
    dca, dcg, d_dw_w, d_dw_b = _conf_conv(proj_o, dw_w, w['o_dw_b'], dc1)
    dyc, du_skip, d_od, d_glu = _s5_post_bwd(y_s5, proj_o, w['o_d'], glu_w, dmix_o, t)
    grads['o_glu_w'] = d_glu
    d_w_c = _s5_wgrad(dyc, 0, xs_re_bf, xs_im_bf, True, "s5_cx_dw")
    dxs_re, dxs_im = _s5_expand(dyc, 0, w_c, True, "s5_cx_dx")
    g_re, g_im, d_are, d_aim = _s5_scan(dxs_re, dxs_im, a_re, a_im, True, (xs_re, xs_im))
    d_w_b = _s5_wgrad(proj_o, u_off, g_re, g_im, False, "s5_bu_dw")
    du = _s5_contract(g_re, g_im, w_b, False, "s5_bu_dx", res=du_skip, out_dtype=BF16)
    d_bb_re = jnp.transpose(_diag_blocks(d_w_b[:, :, :S5_WIDE], S5_GROUP, S5_STATE), (1, 0, 2))
    d_bb_im = jnp.transpose(_diag_blocks(d_w_b[:, :, S5_WIDE:], S5_GROUP, S5_STATE), (1, 0, 2))
    par_cts = [d_are.reshape(S5_GROUPS, S5_STATE), d_aim.reshape(S5_GROUPS, S5_STATE), d_bb_re, d_bb_im]
    in_shapes = [a.shape for a in par_ins]
    d_lr, d_li, d_ls, d_br3, d_bi3 = _call(_grad_fn(_f_s5par, 5, 5), (1,), [whole(a) for a in par_ins + par_cts],
                                           [(s, F32, s, (lambda i, n=len(s): (0,) * n), False) for s in in_shapes], "s5_params_bwd")
    grads['o_a_re'], grads['o_a_im'], grads['o_log_step'] = d_lr[None], d_li[None], d_ls.reshape(1, S5_GROUPS)
    grads['o_b_re'], grads['o_b_im'] = jnp.transpose(d_br3, (1, 2, 0))[None], jnp.transpose(d_bi3, (1, 2, 0))[None]
    grads['o_c_re'] = jnp.transpose(_diag_blocks(d_w_c[:, :S5_WIDE], S5_STATE, S5_GROUP), (0, 2, 1))[None]
    grads['o_c_im'] = -jnp.transpose(_diag_blocks(d_w_c[:, S5_WIDE:], S5_STATE, S5_GROUP), (0, 2, 1))[None]
    grads['o_d'], grads['o_ln_g'], grads['o_ln_b'] = d_od, d_ln_g, d_ln_b
    grads['o_dw_w'], grads['o_dw_b'] = d_dw_w[:, 0][None], d_dw_b
    dproj_o = jnp.concatenate([dca, dcg, du], axis=1)
    grads['o_w_in'] = _mm(h1, dproj_o, "tn", "o_in_dw")
    dh1 = _mm(dproj_o, w['o_w_in'], "nt", "o_in_dx")
    dx2, dx2_bf, d_mix1 = _rms_bwd(x2, mix_g[1], dh1, dx3, "mix_norm_bwd_1")
    in_flight = late['grads_ready']({**grads, 'ffn_w_up': [None, g1['w_up']], 'ffn_w_down': [None, g1['w_down']]}) if late else ()

    dx1, dx1_bf, g0 = _ffn_bwd(x1, ffn_g[0], w['ffn_w_up'][0], ffn_dw_w[0], ffn_dw_b[0], w['ffn_w_down'][0], ffn0, dx2, dx2_bf, "0")
    in_flight_ffn = late['ffn_grads_ready']({'ffn_w_up': [g0['w_up'], None], 'ffn_w_down': [g0['w_down'], None]}) if late else ()
    grads['e_w_out'] = _mm(mix_e, dx1_bf, "tn", "e_out_dw", out_dtype=BF16)
    dmix_e = _mm(dx1_bf, w['e_w_out'], "nt", "e_out_dx")
    dr, dg = _rows(_grad_fn(_f_retpost, 2, 2), t, [(r, 1024, 0), (proj, 1024, 2), (dmix_e, 1024, 0)], [], [1024, 1024], [], "ret_post_bwd",
                   bf16_outs=(0, 1))
    dqr, dkr, dv, landed = _ret_bwd(qr, kr, proj, dr, t, exchange=in_flight_ffn)
    if late:
        late['ffn_grads_landed'](landed)
    dq, dk = _ret_pre_bwd(proj, cosf, sins, dqr, dkr, t)
    dyc0, dxs1, dz, d_dskip, d_ssm_norm = _ssd_post_bwd(yc, xbc_act, proj, d_skip, w['e_ssm_norm'], expand, dmix_e, t)
    dcm, da_q, dbm, dxdt, da_k, da_k_t, landed = _ssd_bwd(xbc_act, xdt, a_cum, a_cum_t, dyc0, expand, t, exchange=in_flight)
    if late:
        late['grads_landed'](landed)
    d_a_cum = da_q - da_k - jnp.pad(da_k_t.T, ((0, 0), (0, LANES - SSM_HEADS)))
    dda = _cumsum(d_a_cum, True, "ssd_cumsum_bwd")
    dxs, ddtr, d_dt_bias, d_a_log = _ssd_pre_bwd(xbc_act, proj, dt_bias, a_log, expand, dxdt, dda, dxs1, t)
    dxbc_act = jnp.concatenate([dxs, dbm, dcm], axis=1)
    dxbc, d_conv_w, d_conv_b = _cols(_grad_fn(_f_ssdconv, 3, 3), 12, [(proj, 32), (conv_w, 0), (conv_b, 0), (dxbc_act, 0)],
                                     [(t,), (4, 1), (1,)], "ssd_conv_bwd", bf16_outs=(0,))
    dproj = jnp.concatenate([dq, dk, dv, dg, dz, dxbc, ddtr], axis=1)
    grads['e_w_in'] = _mm(h0, dproj, "tn", "e_in_dw")
    dh0 = _mm(dproj, w_e, "nt", "e_in_dx")
    dx0, _, d_mix0 = _rms_bwd(x, mix_g[0], dh0, dx1, "mix_norm_bwd_0")

    grads['mix_norm'] = jnp.concatenate([d_mix0, d_mix1], axis=0)
    grads['e_conv_w'], grads['e_conv_b'] = d_conv_w[:, 0][None], d_conv_b
    grads['e_dt_bias'], grads['e_a_log'], grads['e_d'] = d_dt_bias[:, :SSM_HEADS], d_a_log[:, :SSM_HEADS], d_dskip[:, :SSM_HEADS]
    grads['e_ssm_norm'] = d_ssm_norm
    grads['ffn_norm'] = jnp.concatenate([g0['norm'], g1['norm']], axis=0)
    grads['ffn_w_up'], grads['ffn_w_down'] = [g0['w_up'], g1['w_up']], [g0['w_down'], g1['w_down']]
    grads['ffn_dw_w'] = jnp.stack([g0['dw_w'], g1['dw_w']], axis=0)
    grads['ffn_dw_b'] = jnp.concatenate([g0['dw_b'], g1['dw_b']], axis=0)
    return loss, dx0, grads


def _conf_norm_bwd(c1, ln_g, ln_b, dmix_o, t):
    def fn(c1_, dy_, g_, b_):
        _, vjp = jax.vjp(lambda a, b, c: _f_confb(a, b, c)[0], c1_, g_, b_)
        return vjp(dy_)
    return _rows(fn, t, [(c1, 512, 0), (dmix_o, 512, 0)], [ln_g, ln_b], [512], [(1, 512), (1, 512)], "conf_norm_bwd")


def _s5_post_bwd(y_s5, proj_o, d_skip, glu_w, dmix_o, t):
    def fn(yc_, u_, dy_, d_, gw_):
        _, vjp = jax.vjp(lambda a, b, c, e: _f_s5post(a, b, c, e)[0], yc_, u_, d_, gw_)
        return vjp(dy_)
    return _rows(fn, t, [(y_s5, 512, 0), (proj_o, 512, 2), (dmix_o, 512, 1)], [d_skip, glu_w], [512, 512], [(1, 512), (512, 512)],
                 "s5_post_bwd", bf16_outs=(0,))


def _ret_pre_bwd(proj, cosf, sins, dqr, dkr, t):
    def fn(q_, k_, cos_, sin_, dq_, dk_):
        _, vjp = jax.vjp(lambda a, b: _f_retpre(a, b, cos_, sin_), q_, k_)
        return vjp((dq_, dk_))
    return _rows(fn, t, [(proj, 512, 0), (proj, 512, 1), (cosf, LANES, 0), (sins, LANES, 0), (dqr, 512, 0), (dkr, 512, 0)], [],
                 [512, 512], [], "ret_pre_bwd", bf16_outs=(0, 1))


def _ssd_post_bwd(yc, xbc_act, proj, d_skip, norm_w, expand, dmix_e, t):
    def fn(yc_, xs_, z_, dy_, d_, nw_, e_):
        _, vjp = jax.vjp(lambda a, b, c, dd, n: _f_ssdpost(a, b, c, dd, n, e_)[0], yc_, xs_, z_, d_, nw_)
        return vjp(dy_)
    return _rows(fn, t, [(yc, 1024, 0), (xbc_act, 1024, 0), (proj, 1024, 3), (dmix_e, 1024, 1)], [d_skip, norm_w, expand],
                 [1024, 1024, 1024], [(1, LANES), (1, 1024)], "ssd_post_bwd", bf16_outs=(0, 2))


def _ssd_pre_bwd(xbc_act, proj, dt_bias, a_log, expand, dxdt, dda, dxs1, t):
    def fn(xs_, dtr_, dx_, dda_, dxs1_, bias_, alog_, e_):
        _, vjp = jax.vjp(lambda a, b, c, dd: _f_ssdpre(a, b, c, dd, e_), xs_, dtr_, bias_, alog_)
        dxs, ddtr, dbias, dalog = vjp((dx_, dda_))
        return dxs + dxs1_, ddtr, dbias, dalog
    return _rows(fn, t, [(xbc_act, 1024, 0), (proj, LANES, 44), (dxdt, 1024, 0), (dda, LANES, 0), (dxs1, 1024, 0)],
                 [dt_bias, a_log, expand], [1024, LANES], [(1, LANES), (1, LANES)], "ssd_pre_bwd", bf16_outs=(1,))


def kernel(x, mix_norm, e_w_in, e_conv_w, e_conv_b, e_dt_bias, e_a_log, e_d, e_ssm_norm, e_w_out, o_w_in, o_dw_w, o_dw_b, o_ln_g, o_ln_b, o_a_re, o_a_im, o_b_re, o_b_im, o_c_re, o_c_im, o_d, o_log_step, o_glu_w, o_w_out, ffn_norm, ffn_w_up, ffn_dw_w, ffn_dw_b, ffn_w_down, final_norm, loss_target, m_mix_norm, m_e_w_in, m_e_conv_w, m_e_conv_b, m_e_dt_bias, m_e_a_log, m_e_d, m_e_ssm_norm, m_e_w_out, m_o_w_in, m_o_dw_w, m_o_dw_b, m_o_ln_g, m_o_ln_b, m_o_a_re, m_o_a_im, m_o_b_re, m_o_b_im, m_o_c_re, m_o_c_im, m_o_d, m_o_log_step, m_o_glu_w, m_o_w_out, m_ffn_norm, m_ffn_w_up, m_ffn_dw_w, m_ffn_dw_b, m_ffn_w_down, m_final_norm, v_mix_norm, v_e_w_in, v_e_conv_w, v_e_conv_b, v_e_dt_bias, v_e_a_log, v_e_d, v_e_ssm_norm, v_e_w_out, v_o_w_in, v_o_dw_w, v_o_dw_b, v_o_ln_g, v_o_ln_b, v_o_a_re, v_o_a_im, v_o_b_re, v_o_b_im, v_o_c_re, v_o_c_im, v_o_d, v_o_log_step, v_o_glu_w, v_o_w_out, v_ffn_norm, v_ffn_w_up, v_ffn_dw_w, v_ffn_dw_b, v_ffn_w_down, v_final_norm):
    p = dict(locals())

    kinds = ("grad_", "delta_", "new_m_", "new_v_")

    def block(name, layer):
        return p[name][0 if layer is None else layer]

    def shards(blocks):
        return [block(n, layer).astype(BF16) for n, layer, _ in blocks]

    def place(w, gathered, blocks):
        for (n, layer, by_cols), g in zip(blocks, gathered):
            if by_cols:
                full = _join_cols(g, E_IN_PAD if n == 'e_w_in' else N_DEV * g.shape[2], f"join_{n}_{layer}")
            else:
                full = g.reshape(N_DEV * g.shape[1], g.shape[2])
            if layer is None:
                w[n] = full
            else:
                w[n][layer] = full

    first = [b for b in MATMUL_BLOCKS if b[0].startswith('e_')]
    with_ret = [b for b in MATMUL_BLOCKS if b[0].startswith('ffn_') and b[1] == 0]
    with_ssd = [b for b in MATMUL_BLOCKS if b not in first and b not in with_ret]
    gathered = _gather(shards(first) + [_pack([p[n] for n in SMALL_SHARDED], F32, 16)], "gather_first_weights")
    w = {n: p[n] for n in REPLICATED}
    w['ffn_w_up'], w['ffn_w_down'] = [None, None], [None, None]
    place(w, gathered, first)
    for n, piece in zip(SMALL_SHARDED, _unpack(gathered[-1], [p[n].shape for n in SMALL_SHARDED], lead=(N_DEV,))):
        w[n] = _join_shards(piece, SHARDED[n])
    late = {'ret_fwd': (shards(with_ret), functools.partial(place, blocks=with_ret)),
            'ssd_fwd': (shards(with_ssd), functools.partial(place, blocks=with_ssd))}

    def chip_sums(grads, blocks, extra, tag):
        sends = []
        for n, layer, by_cols in blocks:
            g = grads[n] if layer is None else grads[n][layer]
            if by_cols:
                sends.append(_split_cols(g, block(n, layer).shape[1], BF16, f"split_{n}_{layer}"))
            else:
                sends.append(g.astype(BF16).reshape(N_DEV, -1, g.shape[1]))
        core = lax.axis_index("c")
        by_core = [s.reshape((N_CHIPS, 2) + s.shape[1:]) for s in sends + extra]
        keep = [lax.dynamic_index_in_dim(s, core, axis=1, keepdims=False) for s in by_core]
        give = [lax.dynamic_index_in_dim(s, 1 - core, axis=1, keepdims=False) for s in by_core]
        got = _swap_sibling(give, "swap_sibling_grads_" + tag)
        return [_add(a, b, f"chip_sum_{tag}_{i}") for i, (a, b) in enumerate(zip(keep, got))]

    landed_late = []
    late['grads_ready'] = lambda grads: chip_sums(grads, with_ssd, [], "late")
    late['grads_landed'] = landed_late.extend
    landed_ffn = []
    late['ffn_grads_ready'] = lambda grads: chip_sums(grads, with_ret, [], "ffn")
    late['ffn_grads_landed'] = landed_ffn.extend

    loss, dx, grads = _local_step(x[0], loss_target[0], w, late)
    loss = lax.psum(loss, MESH_AXES)

    early = first
    small_send = _pack([_split_shards(grads[n].reshape(p[n].shape[:SHARDED[n]] + (-1,) + p[n].shape[SHARDED[n] + 1:]), SHARDED[n])
                        for n in SMALL_SHARDED], F32, 128, lead=(N_DEV,))
    landed_early = list(_exchange_chips(chip_sums(grads, early, [small_send], "early"), "exchange_chip_grads"))
    part_of = {(n, layer): part for (n, layer, _), part in zip(early + with_ret + with_ssd, landed_early[:-1] + landed_ffn + landed_late)}
    parts = [landed_early[-1], _gather([_pack([grads[n].reshape(p[n].shape) for n in REPLICATED], F32, 128)], "gather_replicated_grads")[0]]

    out, by_layer = {}, {}
    for n, layer, _ in MATMUL_BLOCKS:
        by_layer.setdefault(n, {})[layer] = _adamw(part_of[(n, layer)], *[block(pre + n, layer) for pre in ("", "m_", "v_")],
                                                   f"adamw_{n}_{layer}")
    for n, res in by_layer.items():
        for i, kind in enumerate(kinds):
            out[kind + n] = res[None][i][None] if None in res else jnp.stack([res[0][i], res[1][i]], axis=0)
    for names, part, tag in ((SMALL_SHARDED, parts[-2], "small"), (REPLICATED, parts[-1], "replicated")):
        packed = [_pack([p[pre + n] for n in names], F32, 128) for pre in ("", "m_", "v_")]
        for kind, buf in zip(kinds, _adamw(part, *packed, "adamw_" + tag)):
            for n, a in zip(names, _unpack(buf, [p[n].shape for n in names])):
                out[kind + n] = a
    return (loss, dx[None], *[out[kind + n] for kind in kinds for n in WEIGHTS])
```

```python
import functools
import math

import jax
import jax.numpy as jnp
from jax import lax
from jax.experimental import pallas as pl
from jax.experimental.pallas import tpu as pltpu

F32, BF16 = jnp.float32, jnp.bfloat16
HIGHEST = lax.Precision.HIGHEST
N_DEV = 8
MESH_AXES = ("x", "y", "c")
VMEM_LIMIT = 48 * 1024 * 1024
LANES = 128
PACK_COLS = 1024

D_MODEL = 1024
EPS = 1e-6
RET_HEADS, RET_DK, RET_DV = 4, 128, 256
SSM_HEADS, SSM_P, SSM_N, SSM_GROUPS = 16, 64, 128, 2
SSM_HG = SSM_HEADS // SSM_GROUPS
S5_GROUPS, S5_GROUP, S5_STATE = 32, 16, 64
S5_COLS = S5_GROUPS * S5_STATE
D_FF = 2816
E_IN, E_IN_PAD = 5648, 5760
ADAM_LR, ADAM_B1, ADAM_B2, ADAM_EPS, ADAM_WD, ADAM_STEP = 0.001, 0.9, 0.999, 1e-08, 0.01, 10

WEIGHTS = ['mix_norm', 'e_w_in', 'e_conv_w', 'e_conv_b', 'e_dt_bias', 'e_a_log', 'e_d', 'e_ssm_norm', 'e_w_out', 'o_w_in', 'o_dw_w', 'o_dw_b', 'o_ln_g', 'o_ln_b', 'o_a_re', 'o_a_im', 'o_b_re', 'o_b_im', 'o_c_re', 'o_c_im', 'o_d', 'o_log_step', 'o_glu_w', 'o_w_out', 'ffn_norm', 'ffn_w_up', 'ffn_dw_w', 'ffn_dw_b', 'ffn_w_down', 'final_norm']
SHARDED = {'e_w_in': 2, 'e_conv_w': 2, 'e_w_out': 1, 'o_w_in': 2, 'o_dw_w': 2, 'o_dw_b': 1, 'o_ln_g': 1, 'o_ln_b': 1,
           'o_d': 1, 'o_glu_w': 1, 'o_w_out': 1, 'ffn_w_up': 2, 'ffn_dw_w': 2, 'ffn_w_down': 1}
MATMUL_WEIGHTS = ['e_w_in', 'e_w_out', 'o_w_in', 'o_glu_w', 'o_w_out', 'ffn_w_up', 'ffn_w_down']
MATMUL_BLOCKS = [('e_w_in', None, True), ('e_w_out', None, False), ('o_w_in', None, True), ('o_glu_w', None, False),
                 ('o_w_out', None, False), ('ffn_w_up', 0, True), ('ffn_w_up', 1, True), ('ffn_w_down', 0, False), ('ffn_w_down', 1, False)]
SMALL_SHARDED = [n for n in WEIGHTS if n in SHARDED and n not in MATMUL_WEIGHTS]
REPLICATED = [n for n in WEIGHTS if n not in SHARDED]


def _call(fn, grid, ins, outs, name):
    n_in = len(ins)

    def body(*refs):
        vals = fn(*[r[...] for r in refs[:n_in]])
        first = pl.program_id(0) == 0
        for r, v, o in zip(refs[n_in:], vals, outs):
            if o[4]:
                @pl.when(first)
                def _():
                    r[...] = jnp.zeros_like(r)
                r[...] += v.astype(r.dtype)
            else:
                r[...] = v.astype(r.dtype)

    return pl.pallas_call(
        body, grid=grid,
        in_specs=[pl.BlockSpec(b, m) for _, b, m in ins],
        out_specs=[pl.BlockSpec(o[2], o[3]) for o in outs],
        out_shape=[jax.ShapeDtypeStruct(o[0], o[1]) for o in outs],
        compiler_params=pltpu.CompilerParams(dimension_semantics=("arbitrary",) * len(grid), vmem_limit_bytes=VMEM_LIMIT),
        name=name)(*[a for a, _, _ in ins])


def _rows(fn, n_rows, row_ins, full_ins, row_outs, acc_outs, name, tm=512, bf16_outs=()):
    tm = min(tm, n_rows)
    ins = [(a, (tm, w), (lambda i, c=c: (i, c))) for a, w, c in row_ins]
    ins += [(a, a.shape, (lambda i, n=a.ndim: (0,) * n)) for a in full_ins]
    outs = [((n_rows, w), BF16 if k in bf16_outs else F32, (tm, w), (lambda i: (i, 0)), False) for k, w in enumerate(row_outs)]
    outs += [(tuple(s), F32, tuple(s), (lambda i, n=len(s): (0,) * n), True) for s in acc_outs]
    return _call(fn, (n_rows // tm,), ins, outs, name)


def _cols(fn, n_blocks, col_ins, out_leads, name, cb=LANES, bf16_outs=()):
    ins = [(a, a.shape[:-1] + (cb,), (lambda j, n=a.ndim, o=o: (0,) * (n - 1) + (j + o,))) for a, o in col_ins]
    outs = [(tuple(s) + (n_blocks * cb,), BF16 if k in bf16_outs else F32, tuple(s) + (cb,), (lambda j, n=len(s): (0,) * n + (j,)), False)
            for k, s in enumerate(out_leads)]
    return _call(fn, (n_blocks,), ins, outs, name)


def _grad_fn(f, n_diff, n_in):
    def g(*a):
        diff, consts, cts = a[:n_diff], a[n_diff:n_in], a[n_in:]
        _, vjp = jax.vjp(lambda *d: f(*d, *consts), *diff)
        return vjp(tuple(cts))
    return g


def _silu(x):
    return x * jax.nn.sigmoid(x)


def _rms(x, g):
    return x * lax.rsqrt(jnp.mean(x * x, axis=-1, keepdims=True) + EPS) * g


@jax.custom_vjp
def _softplus(x):
    return jnp.maximum(x, 0.0) + jnp.log(1.0 + jnp.exp(-jnp.abs(x)))


_softplus.defvjp(lambda x: (_softplus(x), x), lambda x, g: (g * jax.nn.sigmoid(x),))


@jax.custom_vjp
def _swap_halves(x):
    return pltpu.roll(x, 64, 1)


_swap_halves.defvjp(lambda x: (_swap_halves(x), None), lambda _, g: (_swap_halves(g),))


def _shift_rows(x, k, up):
    if k == 0:
        return x
    n = x.shape[0]
    t = lax.broadcasted_iota(jnp.int32, x.shape, 0)
    if up:
        return jnp.where(t < n - k, pltpu.roll(x, n - k, 0), 0.0)
    return jnp.where(t >= k, pltpu.roll(x, k, 0), 0.0)


@jax.custom_vjp
def _dwconv(x, w, b):
    k_taps = w.shape[0]
    y = b + w[k_taps - 1] * x
    for k in range(k_taps - 1):
        y = y + w[k] * _shift_rows(x, k_taps - 1 - k, False)
    return y


def _dwconv_fwd(x, w, b):
    return _dwconv(x, w, b), (x, w)


def _dwconv_bwd(saved, dy):
    x, w = saved
    k_taps = w.shape[0]
    dx = w[k_taps - 1] * dy
    dws = []
    for k in range(k_taps - 1):
        s = k_taps - 1 - k
        dx = dx + w[k] * _shift_rows(dy, s, True)
        dws.append(jnp.sum(dy * _shift_rows(x, s, False), axis=0, keepdims=True)[None])
    dws.append(jnp.sum(dy * x, axis=0, keepdims=True)[None])
    return dx, jnp.concatenate(dws, axis=0), jnp.sum(dy, axis=0, keepdims=True)


_dwconv.defvjp(_dwconv_fwd, _dwconv_bwd)


def _f_rms(x, g):
    return (_rms(x, g),)


def _rot(x, cosf, sins):
    outs = []
    for h in range(RET_HEADS):
        xh = x[:, h * RET_DK:(h + 1) * RET_DK]
        outs.append(xh * cosf + _swap_halves(xh) * sins)
    return jnp.concatenate(outs, axis=1)


def _f_retpre(q, k, cosf, sins):
    return _rot(q, cosf, sins), _rot(k, cosf, sins) * (RET_DK ** -0.5)


def _f_retpost(r, g):
    outs = []
    for h in range(RET_HEADS):
        rh = r[:, h * RET_DV:(h + 1) * RET_DV]
        rc = rh - jnp.mean(rh, axis=-1, keepdims=True)
        outs.append(_silu(g[:, h * RET_DV:(h + 1) * RET_DV]) * (rc * lax.rsqrt(jnp.mean(rc * rc, axis=-1, keepdims=True) + EPS)))
    return (jnp.concatenate(outs, axis=1),)


def _f_ssdconv(xbc, w, b):
    return (_silu(_dwconv(xbc, w, b)),)


def _f_ssdpre(xs, dtr, bias, alog, expand):
    dt = _softplus(dtr + bias)
    return xs * jnp.dot(dt, expand, precision=HIGHEST, preferred_element_type=F32), dt * (-jnp.exp(alog))


def _f_ssdpost(yc, xs, z, dskip, norm_w, expand):
    d_wide = jnp.dot(jnp.broadcast_to(dskip, (yc.shape[0], LANES)), expand, precision=HIGHEST, preferred_element_type=F32)
    y = (yc + d_wide * xs) * _silu(z)
    half = y.shape[1] // SSM_GROUPS
    outs = []
    for g in range(SSM_GROUPS):
        yg = y[:, g * half:(g + 1) * half]
        outs.append(yg * lax.rsqrt(jnp.mean(yg * yg, axis=-1, keepdims=True) + EPS))
    return (jnp.concatenate(outs, axis=1) * norm_w,)


def _f_ffnmid(gin, uin, wg, wu, bg, bu):
    return (_silu(_dwconv(gin, wg, bg)) * _dwconv(uin, wu, bu),)


def _f_confb(c1, g, b):
    mu = jnp.mean(c1, axis=-1, keepdims=True)
    xc = c1 - mu
    return (_silu(xc * lax.rsqrt(jnp.mean(xc * xc, axis=-1, keepdims=True) + EPS) * g + b),)


def _f_s5post(yc, u, dskip, glu_w):
    s = jax.nn.gelu(yc + dskip * u)
    z = jnp.dot(s.astype(BF16), glu_w.astype(BF16), preferred_element_type=F32)
    return (s * jax.nn.sigmoid(z),)


def _f_s5par(lr, li, ls, br, bi):
    step = jnp.exp(ls)
    mag = jnp.exp(lr * step)
    ab_re = mag * jnp.cos(li * step)
    ab_im = mag * jnp.sin(li * step)
    den = lr * lr + li * li
    f_re = ((ab_re - 1.0) * lr + ab_im * li) / den
    f_im = (ab_im * lr - (ab_re - 1.0) * li) / den
    return ab_re, ab_im, f_re[None] * br - f_im[None] * bi, f_re[None] * bi + f_im[None] * br


def _loss_step(x, tgt, g):
    def f(x_, g_):
        e = _rms(x_, g_) - tgt
        return 0.5 * jnp.sum(jnp.mean(e * e, axis=-1, keepdims=True), axis=0, keepdims=True)
    loss, vjp = jax.vjp(f, x, g)
    dx, dg = vjp(jnp.ones((1, 1), F32))
    return dx, dx, jnp.broadcast_to(loss, (8, LANES)), dg


def _tile(n, pref):
    if n <= pref:
        return n
    t = (pref // LANES) * LANES
    while n % t:
        t -= LANES
    return t


_DIMS = {"nn": (((1,), (0,)), ((), ())), "nt": (((1,), (1,)), ((), ())), "tn": (((0,), (0,)), ((), ()))}


def _mm(a, b, mode, name, res=None, out_dtype=F32, tm=1024, tn=1408, tk=1408):
    if mode == "nn":
        (m, k), n = a.shape, b.shape[1]
    elif mode == "nt":
        (m, k), n = a.shape, b.shape[0]
    else:
        (k, m), n = a.shape, b.shape[1]
    tm, tn, tk = _tile(m, tm), _tile(n, tn), _tile(k, tk)
    nk = k // tk
    a_spec = pl.BlockSpec((tk, tm), lambda i, j, kk: (kk, i)) if mode == "tn" else pl.BlockSpec((tm, tk), lambda i, j, kk: (i, kk))
    b_spec = pl.BlockSpec((tn, tk), lambda i, j, kk: (j, kk)) if mode == "nt" else pl.BlockSpec((tk, tn), lambda i, j, kk: (kk, j))
    o_spec = pl.BlockSpec((tm, tn), lambda i, j, kk: (i, j))
    has_res = res is not None

    def body(*refs):
        a_ref, b_ref = refs[0], refs[1]
        o_ref, acc = refs[-2], refs[-1]
        kk = pl.program_id(2)

        @pl.when(kk == 0)
        def _():
            acc[...] = jnp.zeros_like(acc)

        acc[...] += lax.dot_general(a_ref[...].astype(BF16), b_ref[...].astype(BF16), _DIMS[mode], preferred_element_type=F32)

        @pl.when(kk == nk - 1)
        def _():
            o_ref[...] = (acc[...] + refs[2][...] if has_res else acc[...]).astype(out_dtype)

    return pl.pallas_call(
        body, grid=(m // tm, n // tn, nk),
        in_specs=[a_spec, b_spec] + ([o_spec] if has_res else []),
        out_specs=o_spec, out_shape=jax.ShapeDtypeStruct((m, n), out_dtype),
        scratch_shapes=[pltpu.VMEM((tm, tn), F32)],
        compiler_params=pltpu.CompilerParams(dimension_semantics=("parallel", "parallel", "arbitrary"), vmem_limit_bytes=VMEM_LIMIT),
        name=name)(*([a, b] + ([res] if has_res else [])))


def _seq_block(t):
    return min(512, t)


def _ret_block(t):
    return min(1024, t)


def _causal_diff(i, j, blk):
    r = lax.broadcasted_iota(jnp.int32, (blk, blk), 0)
    c = lax.broadcasted_iota(jnp.int32, (blk, blk), 1)
    return (i - j) * blk + r - c


def _ret_decay(lg, i, j, blk):
    diff = _causal_diff(i, j, blk)
    return jnp.where(diff >= 0, jnp.exp(lg * jnp.maximum(diff, 0).astype(F32)), 0.0)


def _pair_call(body, lead_grid, nb, key_major, in_specs, out_specs, out_shape, scratch, name, args, gather=(), exchange=()):
    pairs = [(i, j) for j in range(nb) for i in range(j, nb)] if key_major else [(i, j) for i in range(nb) for j in range(i + 1)]
    tables = [jnp.array([p[k] for p in pairs], jnp.int32) for k in (0, 1)]
    lead = len(lead_grid)
    n_steps = math.prod(lead_grid) * len(pairs)
    behind = list(gather) or list(exchange)
    n_bg = len(behind)
    phases, per_array = (_gather_phases, GATHER_SEMS) if gather else (_exchange_phases, EXCHANGE_SEMS)

    def on_pairs(spec):
        if spec.block_shape is None:
            return spec

        def index_map(*a):
            i, j = a[lead + 1][a[lead]], a[lead + 2][a[lead]]
            return spec.index_map(*a[:lead], *((j, i) if key_major else (i, j)))

        return pl.BlockSpec(spec.block_shape, index_map)

    many = isinstance(out_specs, (list, tuple))
    out_specs, out_shape = (list(out_specs), list(out_shape)) if many else ([out_specs], [out_shape])
    n_in, n_out = len(in_specs), len(out_specs)

    def wrapped(i_ref, j_ref, *refs):
        p = pl.program_id(lead)
        own = refs[:n_in] + refs[n_in + n_bg:n_in + n_bg + n_out] + refs[n_in + 2 * n_bg + n_out:len(refs) - (3 if n_bg else 0)]
        if n_bg:
            step = p + (pl.program_id(0) * len(pairs) if lead else 0)
            start, forward, finish = phases(refs[n_in:n_in + n_bg], refs[n_in + n_bg + n_out:n_in + 2 * n_bg + n_out], *refs[-3:])
            pl.when(step == 0)(start)
        body(i_ref[p], j_ref[p], *own)
        if n_bg:
            pl.when(step == n_steps * 3 // 4)(forward)
            pl.when(step == n_steps - 1)(finish)

    sems = [pltpu.SemaphoreType.DMA((per_array * n_bg,))] * 2 + [pltpu.SemaphoreType.DMA((n_bg,))] if n_bg else []
    landed = [jax.ShapeDtypeStruct(((N_DEV,) + g.shape) if gather else g.shape, g.dtype) for g in behind]
    grid_spec = pltpu.PrefetchScalarGridSpec(
        num_scalar_prefetch=2, grid=tuple(lead_grid) + (len(pairs),), in_specs=[on_pairs(s) for s in in_specs] + [_HBM] * n_bg,
        out_specs=[on_pairs(s) for s in out_specs] + [_HBM] * n_bg, scratch_shapes=list(scratch) + sems)
    res = pl.pallas_call(
        wrapped, grid_spec=grid_spec, out_shape=out_shape + landed,
        compiler_params=pltpu.CompilerParams(dimension_semantics=("arbitrary",) * (lead + 1), vmem_limit_bytes=VMEM_LIMIT),
        name=name)(*tables, *args, *behind)
    return res if many or n_bg else res[0]


def _ret_row_decays(lg, i, j, blk):
    row = lax.broadcasted_iota(jnp.int32, (blk, RET_DK), 0)
    return jnp.exp(lg * row.astype(F32)), jnp.exp(lg * ((i - j) * blk - row).astype(F32))


def _ret_scaled(lg, i, j, blk, q_ref, k_ref):
    a, b = _ret_row_decays(lg, i, j, blk)
    return (q_ref[...] * a).astype(BF16), (k_ref[...] * b).astype(BF16)


def _ret_log_gamma():
    return jnp.log1p(-(2.0 ** (-5.0 - jnp.arange(RET_HEADS, dtype=F32))))


def _ret_fwd(qr, kr, proj, t, gather=()):
    blk = _ret_block(t)
    nb = t // blk
    v_off = (2 * RET_HEADS * RET_DK) // RET_DV

    def body(i, j, lg_ref, q_ref, k_ref, v_ref, o_ref, acc):
        h = pl.program_id(0)

        @pl.when(j == 0)
        def _():
            acc[...] = jnp.zeros_like(acc)

        @pl.when(j < i)
        def _():
            qa, kb = _ret_scaled(lg_ref[h], i, j, blk, q_ref, k_ref)
            p = lax.dot_general(qa, kb, _DIMS["nt"], preferred_element_type=F32).astype(BF16)
            acc[...] += jnp.dot(p, v_ref[...].astype(BF16), preferred_element_type=F32)

        @pl.when(j == i)
        def _():
            s = lax.dot_general(q_ref[...].astype(BF16), k_ref[...].astype(BF16), _DIMS["nt"], preferred_element_type=F32)
            p = (s * _ret_decay(lg_ref[h], i, j, blk)).astype(BF16)
            o_ref[...] = acc[...] + jnp.dot(p, v_ref[...].astype(BF16), preferred_element_type=F32)

    return _pair_call(
        body, (RET_HEADS,), nb, False,
        [pl.BlockSpec(memory_space=pltpu.SMEM),
         pl.BlockSpec((blk, RET_DK), lambda h, i, j: (i, h)),
         pl.BlockSpec((blk, RET_DK), lambda h, i, j: (j, h)),
         pl.BlockSpec((blk, RET_DV), lambda h, i, j: (j, v_off + h))],
        pl.BlockSpec((blk, RET_DV), lambda h, i, j: (i, h)),
        jax.ShapeDtypeStruct((t, RET_HEADS * RET_DV), F32), [pltpu.VMEM((blk, RET_DV), F32)],
        "ret_fwd", (_ret_log_gamma(), qr, kr, proj), gather)


def _ret_bwd(qr, kr, proj, dr, t, exchange=()):
    blk = _ret_block(t)
    nb = t // blk
    v_off = (2 * RET_HEADS * RET_DK) // RET_DV

    def dq_body(i, j, lg_ref, q_ref, k_ref, v_ref, do_ref, dq_ref, acc):
        h = pl.program_id(0)

        @pl.when(j == 0)
        def _():
            acc[...] = jnp.zeros_like(acc)

        @pl.when(j < i)
        def _():
            a, b = _ret_row_decays(lg_ref[h], i, j, blk)
            ds = lax.dot_general(do_ref[...].astype(BF16), v_ref[...].astype(BF16), _DIMS["nt"], preferred_element_type=F32)
            acc[...] += a * jnp.dot(ds.astype(BF16), (k_ref[...] * b).astype(BF16), preferred_element_type=F32)

        @pl.when(j == i)
        def _():
            ds = lax.dot_general(do_ref[...].astype(BF16), v_ref[...].astype(BF16), _DIMS["nt"], preferred_element_type=F32)
            dsm = (ds * _ret_decay(lg_ref[h], i, j, blk)).astype(BF16)
            dq_ref[...] = acc[...] + jnp.dot(dsm, k_ref[...].astype(BF16), preferred_element_type=F32)

    dq = _pair_call(
        dq_body, (RET_HEADS,), nb, False,
        [pl.BlockSpec(memory_space=pltpu.SMEM),
         pl.BlockSpec((blk, RET_DK), lambda h, i, j: (i, h)),
         pl.BlockSpec((blk, RET_DK), lambda h, i, j: (j, h)),
         pl.BlockSpec((blk, RET_DV), lambda h, i, j: (j, v_off + h)),
         pl.BlockSpec((blk, RET_DV), lambda h, i, j: (i, h))],
        pl.BlockSpec((blk, RET_DK), lambda h, i, j: (i, h)),
        jax.ShapeDtypeStruct((t, RET_HEADS * RET_DK), F32), [pltpu.VMEM((blk, RET_DK), F32)],
        "ret_bwd_dq", (_ret_log_gamma(), qr, kr, proj, dr))

    def dkv_body(i, j, lg_ref, q_ref, k_ref, v_ref, do_ref, dk_ref, dv_ref, acc_k, acc_v):
        h = pl.program_id(0)

        @pl.when(i == j)
        def _():
            acc_k[...] = jnp.zeros_like(acc_k)
            acc_v[...] = jnp.zeros_like(acc_v)

        @pl.when(i > j)
        def _():
            a, b = _ret_row_decays(lg_ref[h], i, j, blk)
            qa, kb = (q_ref[...] * a).astype(BF16), (k_ref[...] * b).astype(BF16)
            do = do_ref[...].astype(BF16)
            p = lax.dot_general(qa, kb, _DIMS["nt"], preferred_element_type=F32).astype(BF16)
            acc_v[...] += lax.dot_general(p, do, _DIMS["tn"], preferred_element_type=F32)
            ds = lax.dot_general(do, v_ref[...].astype(BF16), _DIMS["nt"], preferred_element_type=F32).astype(BF16)
            acc_k[...] += b * lax.dot_general(ds, qa, _DIMS["tn"], preferred_element_type=F32)

        @pl.when(i == j)
        def _():
            q = q_ref[...].astype(BF16)
            do = do_ref[...].astype(BF16)
            decay = _ret_decay(lg_ref[h], i, j, blk)
            s = lax.dot_general(q, k_ref[...].astype(BF16), _DIMS["nt"], preferred_element_type=F32)
            acc_v[...] += lax.dot_general((s * decay).astype(BF16), do, _DIMS["tn"], preferred_element_type=F32)
            ds = lax.dot_general(do, v_ref[...].astype(BF16), _DIMS["nt"], preferred_element_type=F32)
            acc_k[...] += lax.dot_general((ds * decay).astype(BF16), q, _DIMS["tn"], preferred_element_type=F32)

        @pl.when(i == nb - 1)
        def _():
            dk_ref[...] = acc_k[...]
            dv_ref[...] = acc_v[...].astype(BF16)

    dk, dv, *landed = _pair_call(
        dkv_body, (RET_HEADS,), nb, True,
        [pl.BlockSpec(memory_space=pltpu.SMEM),
         pl.BlockSpec((blk, RET_DK), lambda h, j, i: (i, h)),
         pl.BlockSpec((blk, RET_DK), lambda h, j, i: (j, h)),
         pl.BlockSpec((blk, RET_DV), lambda h, j, i: (j, v_off + h)),
         pl.BlockSpec((blk, RET_DV), lambda h, j, i: (i, h))],
        [pl.BlockSpec((blk, RET_DK), lambda h, j, i: (j, h)), pl.BlockSpec((blk, RET_DV), lambda h, j, i: (j, h))],
        [jax.ShapeDtypeStruct((t, RET_HEADS * RET_DK), F32), jax.ShapeDtypeStruct((t, RET_HEADS * RET_DV), BF16)],
        [pltpu.VMEM((blk, RET_DK), F32), pltpu.VMEM((blk, RET_DV), F32)],
        "ret_bwd_dkv", (_ret_log_gamma(), qr, kr, proj, dr), exchange=exchange)
    return dq, dk, dv, landed


def _cumsum(x, reverse, name):
    t = x.shape[0]
    blk = _seq_block(t)
    nb = t // blk

    def body(x_ref, o_ref, carry):
        @pl.when(pl.program_id(0) == 0)
        def _():
            carry[...] = jnp.zeros_like(carry)

        r = lax.broadcasted_iota(jnp.int32, (blk, blk), 0)
        c = lax.broadcasted_iota(jnp.int32, (blk, blk), 1)
        tri = ((r <= c) if reverse else (r >= c)).astype(F32)
        o_ref[...] = jnp.dot(tri, x_ref[...], precision=HIGHEST, preferred_element_type=F32) + carry[...]
        carry[...] = o_ref[0:1, :] if reverse else o_ref[blk - 1:blk, :]

    idx = (lambda i: (nb - 1 - i, 0)) if reverse else (lambda i: (i, 0))
    return pl.pallas_call(
        body, grid=(nb,), in_specs=[pl.BlockSpec((blk, LANES), idx)], out_specs=pl.BlockSpec((blk, LANES), idx),
        out_shape=jax.ShapeDtypeStruct((t, LANES), F32), scratch_shapes=[pltpu.VMEM((1, LANES), F32)],
        compiler_params=pltpu.CompilerParams(dimension_semantics=("arbitrary",), vmem_limit_bytes=VMEM_LIMIT),
        name=name)(x)


def _ssd_decay(a_ref, at_ref, hh, mask):
    return jnp.exp(jnp.where(mask, a_ref[:, hh:hh + 1] - at_ref[hh:hh + 1, :], -jnp.inf))


def _head_lanes(s):
    lane = lax.broadcasted_iota(jnp.int32, (s.shape[0], LANES), 1)
    return jnp.concatenate([jnp.where(lane < SSM_P, s[:, 2 * p:2 * p + 1], s[:, 2 * p + 1:2 * p + 2])
                            for p in range(SSM_HEADS // 2)], axis=1)


_B_OFF, _C_OFF = 1024 // SSM_N, 1024 // SSM_N + SSM_GROUPS


def _ssd_fwd(xbc_act, xdt, a_cum, a_cum_t, t, gather=()):
    blk = _seq_block(t)
    nb = t // blk

    def body(i, j, c0, c1, b0, b1, x_ref, ai_ref, aj_ref, at_ref, o_ref, acc):
        @pl.when(j == 0)
        def _():
            acc[...] = jnp.zeros_like(acc)

        @pl.when(j < i)
        def _():
            xv = (x_ref[...] * _head_lanes(jnp.exp(ai_ref[0:1, :] - aj_ref[...]))).astype(BF16)
            for g, (c_ref, b_ref) in enumerate(((c0, b0), (c1, b1))):
                cb = lax.dot_general(c_ref[...].astype(BF16), b_ref[...].astype(BF16), _DIMS["nt"], preferred_element_type=F32)
                cols = slice(g * half, (g + 1) * half)
                acc[:, cols] += jnp.dot(cb.astype(BF16), xv[:, cols], preferred_element_type=F32)

        @pl.when(j == i)
        def _():
            o_ref[...] = acc[...] * _head_lanes(jnp.exp(ai_ref[...] - ai_ref[0:1, :]))
            mask = _causal_diff(i, j, blk) >= 0
            for g, (c_ref, b_ref) in enumerate(((c0, b0), (c1, b1))):
                cb = lax.dot_general(c_ref[...].astype(BF16), b_ref[...].astype(BF16), _DIMS["nt"], preferred_element_type=F32)
                for h in range(SSM_HG):
                    hh = g * SSM_HG + h
                    cols = slice(hh * SSM_P, (hh + 1) * SSM_P)
                    m = (cb * _ssd_decay(ai_ref, at_ref, hh, mask)).astype(BF16)
                    o_ref[:, cols] += jnp.dot(m, x_ref[:, cols].astype(BF16), preferred_element_type=F32)

    half = SSM_HG * SSM_P
    row_i = lambda off: pl.BlockSpec((blk, SSM_N), lambda i, j, off=off: (i, off))
    row_j = lambda off: pl.BlockSpec((blk, SSM_N), lambda i, j, off=off: (jnp.minimum(j, i), off))
    return _pair_call(
        body, (), nb, False,
        [row_i(_C_OFF), row_i(_C_OFF + 1), row_j(_B_OFF), row_j(_B_OFF + 1),
         pl.BlockSpec((blk, SSM_HEADS * SSM_P), lambda i, j: (j, 0)),
         pl.BlockSpec((blk, LANES), lambda i, j: (i, 0)),
         pl.BlockSpec((blk, LANES), lambda i, j: (j, 0)),
         pl.BlockSpec((SSM_HEADS, blk), lambda i, j: (0, j))],
        pl.BlockSpec((blk, SSM_HEADS * SSM_P), lambda i, j: (i, 0)),
        jax.ShapeDtypeStruct((t, SSM_HEADS * SSM_P), F32), [pltpu.VMEM((blk, SSM_HEADS * SSM_P), F32)],
        "ssd_fwd", (xbc_act, xbc_act, xbc_act, xbc_act, xdt, a_cum, a_cum, a_cum_t), gather)


def _ssd_bwd(xbc_act, xdt, a_cum, a_cum_t, dy, expand, t, exchange=()):
    blk = _seq_block(t)
    nb = t // blk
    width = SSM_HEADS * SSM_P

    half = SSM_HG * SSM_P

    def head_sums(prod, e_ref):
        return lax.dot_general(prod, e_ref[...], _DIMS["nt"], precision=HIGHEST, preferred_element_type=F32)

    def q_body(i, j, c0, c1, b0, b1, x_ref, ai_ref, aj_ref, at_ref, dy_ref, e_ref, dc_ref, da_ref, acc_c, acc_a, acc_p, dyu):
        @pl.when(j == 0)
        def _():
            acc_c[...] = jnp.zeros_like(acc_c)
            acc_a[...] = jnp.zeros_like(acc_a)
            acc_p[...] = jnp.zeros_like(acc_p)
            dyu[...] = (dy_ref[...].astype(F32) * _head_lanes(jnp.exp(ai_ref[...] - ai_ref[0:1, :]))).astype(BF16)

        @pl.when(j < i)
        def _():
            xv = (x_ref[...] * _head_lanes(jnp.exp(ai_ref[0:1, :] - aj_ref[...]))).astype(BF16)
            for g, (c_ref, b_ref) in enumerate(((c0, b0), (c1, b1))):
                cols = slice(g * half, (g + 1) * half)
                bj = b_ref[...].astype(BF16)
                cb = lax.dot_general(c_ref[...].astype(BF16), bj, _DIMS["nt"], preferred_element_type=F32).astype(BF16)
                dcb = lax.dot_general(dyu[:, cols], xv[:, cols], _DIMS["nt"], preferred_element_type=F32)
                acc_c[:, g * SSM_N:(g + 1) * SSM_N] += jnp.dot(dcb.astype(BF16), bj, preferred_element_type=F32)
                acc_p[:, cols] += dyu[:, cols].astype(F32) * jnp.dot(cb, xv[:, cols], preferred_element_type=F32)

        @pl.when(j == i)
        def _():
            mask = _causal_diff(i, j, blk) >= 0
            for g, (c_ref, b_ref) in enumerate(((c0, b0), (c1, b1))):
                bj = b_ref[...].astype(BF16)
                cb = lax.dot_general(c_ref[...].astype(BF16), bj, _DIMS["nt"], preferred_element_type=F32)
                dcb = jnp.zeros((blk, blk), F32)
                for h in range(SSM_HG):
                    hh = g * SSM_HG + h
                    cols = slice(hh * SSM_P, (hh + 1) * SSM_P)
                    dm = lax.dot_general(dy_ref[:, cols].astype(BF16), x_ref[:, cols].astype(BF16), _DIMS["nt"],
                                         preferred_element_type=F32) * _ssd_decay(ai_ref, at_ref, hh, mask)
                    dcb = dcb + dm
                    acc_a[:, hh:hh + 1] += jnp.sum(dm * cb, axis=1, keepdims=True)
                acc_c[:, g * SSM_N:(g + 1) * SSM_N] += jnp.dot(dcb.astype(BF16), bj, preferred_element_type=F32)
            dc_ref[...] = acc_c[...]
            da_ref[...] = acc_a[...] + head_sums(acc_p[...], e_ref)

    row_i = lambda off: pl.BlockSpec((blk, SSM_N), lambda i, j, off=off: (i, off))
    row_j = lambda off: pl.BlockSpec((blk, SSM_N), lambda i, j, off=off: (jnp.minimum(j, i), off))
    e_spec = pl.BlockSpec((LANES, width), lambda i, j: (0, 0))
    dc, da_q = _pair_call(
        q_body, (), nb, False,
        [row_i(_C_OFF), row_i(_C_OFF + 1), row_j(_B_OFF), row_j(_B_OFF + 1),
         pl.BlockSpec((blk, width), lambda i, j: (j, 0)),
         pl.BlockSpec((blk, LANES), lambda i, j: (i, 0)),
         pl.BlockSpec((blk, LANES), lambda i, j: (j, 0)),
         pl.BlockSpec((SSM_HEADS, blk), lambda i, j: (0, j)),
         pl.BlockSpec((blk, width), lambda i, j: (i, 0)), e_spec],
        [pl.BlockSpec((blk, SSM_GROUPS * SSM_N), lambda i, j: (i, 0)), pl.BlockSpec((blk, LANES), lambda i, j: (i, 0))],
        [jax.ShapeDtypeStruct((t, SSM_GROUPS * SSM_N), F32), jax.ShapeDtypeStruct((t, LANES), F32)],
        [pltpu.VMEM((blk, SSM_GROUPS * SSM_N), F32), pltpu.VMEM((blk, LANES), F32), pltpu.VMEM((blk, width), F32),
         pltpu.VMEM((blk, width), BF16)],
        "ssd_bwd_q", (xbc_act, xbc_act, xbc_act, xbc_act, xdt, a_cum, a_cum, a_cum_t, dy, expand))

    def k_body(i, j, c0, c1, b0, b1, x_ref, ai_ref, aj_ref, at_ref, dy_ref, e_ref, db_ref, dx_ref, da_ref, dat_ref, acc_b, acc_x, acc_a,
               acc_p):
        @pl.when(i == j)
        def _():
            acc_b[...] = jnp.zeros_like(acc_b)
            acc_x[...] = jnp.zeros_like(acc_x)
            acc_a[...] = jnp.zeros_like(acc_a)
            acc_p[...] = jnp.zeros_like(acc_p)

        @pl.when(i > j)
        def _():
            v = _head_lanes(jnp.exp(ai_ref[0:1, :] - aj_ref[...]))
            dyu_all = (dy_ref[...].astype(F32) * _head_lanes(jnp.exp(ai_ref[...] - ai_ref[0:1, :]))).astype(BF16)
            xv = (x_ref[...] * v).astype(BF16)
            for g, (c_ref, b_ref) in enumerate(((c0, b0), (c1, b1))):
                cols = slice(g * half, (g + 1) * half)
                ci = c_ref[...].astype(BF16)
                cb = lax.dot_general(ci, b_ref[...].astype(BF16), _DIMS["nt"], preferred_element_type=F32).astype(BF16)
                dcb = lax.dot_general(dyu_all[:, cols], xv[:, cols], _DIMS["nt"], preferred_element_type=F32)
                acc_b[:, g * SSM_N:(g + 1) * SSM_N] += lax.dot_general(dcb.astype(BF16), ci, _DIMS["tn"], preferred_element_type=F32)
                dxv = lax.dot_general(cb, dyu_all[:, cols], _DIMS["tn"], preferred_element_type=F32)
                acc_x[:, cols] += v[:, cols] * dxv
                acc_p[:, cols] += xv[:, cols].astype(F32) * dxv

        @pl.when(i == j)
        def _():
            mask = _causal_diff(i, j, blk) >= 0
            for g, (c_ref, b_ref) in enumerate(((c0, b0), (c1, b1))):
                ci = c_ref[...].astype(BF16)
                cb = lax.dot_general(ci, b_ref[...].astype(BF16), _DIMS["nt"], preferred_element_type=F32)
                dcb = jnp.zeros((blk, blk), F32)
                for h in range(SSM_HG):
                    hh = g * SSM_HG + h
                    cols = slice(hh * SSM_P, (hh + 1) * SSM_P)
                    decay = _ssd_decay(ai_ref, at_ref, hh, mask)
                    dyh = dy_ref[:, cols].astype(BF16)
                    acc_x[:, cols] += lax.dot_general((cb * decay).astype(BF16), dyh, _DIMS["tn"], preferred_element_type=F32)
                    dm = lax.dot_general(dyh, x_ref[:, cols].astype(BF16), _DIMS["nt"], preferred_element_type=F32) * decay
                    dcb = dcb + dm
                    acc_a[hh:hh + 1, :] += jnp.sum(dm * cb, axis=0, keepdims=True)
                acc_b[:, g * SSM_N:(g + 1) * SSM_N] += lax.dot_general(dcb.astype(BF16), ci, _DIMS["tn"], preferred_element_type=F32)

        @pl.when(i == nb - 1)
        def _():
            db_ref[...] = acc_b[...]
            dx_ref[...] = acc_x[...]
            dat_ref[...] = acc_a[...]
            da_ref[...] = head_sums(acc_p[...], e_ref)

    rowk_i = lambda off: pl.BlockSpec((blk, SSM_N), lambda j, i, off=off: (jnp.maximum(i, j), off))
    rowk_j = lambda off: pl.BlockSpec((blk, SSM_N), lambda j, i, off=off: (j, off))
    db, dx, da_k, da_k_t, *landed = _pair_call(
        k_body, (), nb, True,
        [rowk_i(_C_OFF), rowk_i(_C_OFF + 1), rowk_j(_B_OFF), rowk_j(_B_OFF + 1),
         pl.BlockSpec((blk, width), lambda j, i: (j, 0)),
         pl.BlockSpec((blk, LANES), lambda j, i: (i, 0)),
         pl.BlockSpec((blk, LANES), lambda j, i: (j, 0)),
         pl.BlockSpec((SSM_HEADS, blk), lambda j, i: (0, j)),
         pl.BlockSpec((blk, width), lambda j, i: (i, 0)), e_spec],
        [pl.BlockSpec((blk, SSM_GROUPS * SSM_N), lambda j, i: (j, 0)), pl.BlockSpec((blk, width), lambda j, i: (j, 0)),
         pl.BlockSpec((blk, LANES), lambda j, i: (j, 0)), pl.BlockSpec((SSM_HEADS, blk), lambda j, i: (0, j))],
        [jax.ShapeDtypeStruct((t, SSM_GROUPS * SSM_N), F32), jax.ShapeDtypeStruct((t, width), F32),
         jax.ShapeDtypeStruct((t, LANES), F32), jax.ShapeDtypeStruct((SSM_HEADS, t), F32)],
        [pltpu.VMEM((blk, SSM_GROUPS * SSM_N), F32), pltpu.VMEM((blk, width), F32), pltpu.VMEM((SSM_HEADS, blk), F32),
         pltpu.VMEM((blk, width), F32)],
        "ssd_bwd_k", (xbc_act, xbc_act, xbc_act, xbc_act, xdt, a_cum, a_cum, a_cum_t, dy, expand), exchange=exchange)
    return dc, da_q, db, dx, da_k, da_k_t, landed


S5_SUPER = 4
S5_NARROW, S5_WIDE = S5_GROUPS * S5_GROUP // S5_SUPER, S5_GROUPS * S5_STATE // S5_SUPER


def _s5_tiles(t):
    tm = min(1024, t)
    narrow = lambda off: pl.BlockSpec((tm, S5_NARROW), lambda s, i, off=off: (i, s + off))
    wide = pl.BlockSpec((tm, S5_WIDE), lambda s, i: (i, s))
    weight = lambda shape: pl.BlockSpec((1,) + shape, lambda s, i: (s, 0, 0))
    params = pltpu.CompilerParams(dimension_semantics=("arbitrary", "arbitrary"), vmem_limit_bytes=VMEM_LIMIT)
    return tm, narrow, wide, weight, params


def _s5_expand(a, a_off, w, w_is_wide_first, name):
    t = a.shape[0]
    tm, narrow, wide, weight, params = _s5_tiles(t)

    def body(a_ref, w_ref, re_ref, im_ref):
        dims = _DIMS["nt"] if w_is_wide_first else _DIMS["nn"]
        o = lax.dot_general(a_ref[...].astype(BF16), w_ref[0].astype(BF16), dims, preferred_element_type=F32)
        re_ref[...] = o[:, :S5_WIDE]
        im_ref[...] = o[:, S5_WIDE:]

    return pl.pallas_call(
        body, grid=(S5_SUPER, t // tm), in_specs=[narrow(a_off), weight(w.shape[1:])], out_specs=[wide, wide],
        out_shape=[jax.ShapeDtypeStruct((t, S5_COLS), F32)] * 2, compiler_params=params, name=name)(a, w)


def _s5_contract(x_re, x_im, w, w_is_wide_first, name, res=None, out_dtype=F32):
    t = x_re.shape[0]
    tm, narrow, wide, weight, params = _s5_tiles(t)
    has_res = res is not None

    def body(*refs):
        re_ref, im_ref, w_ref, o_ref = refs[0], refs[1], refs[2], refs[-1]
        x = jnp.concatenate([re_ref[...].astype(BF16), im_ref[...].astype(BF16)], axis=1)
        dims = _DIMS["nn"] if w_is_wide_first else _DIMS["nt"]
        o = lax.dot_general(x, w_ref[0].astype(BF16), dims, preferred_element_type=F32)
        o_ref[...] = (o + refs[3][...] if has_res else o).astype(out_dtype)

    return pl.pallas_call(
        body, grid=(S5_SUPER, t // tm), in_specs=[wide, wide, weight(w.shape[1:])] + ([narrow(0)] if has_res else []),
        out_specs=narrow(0), out_shape=jax.ShapeDtypeStruct((t, S5_SUPER * S5_NARROW), out_dtype), compiler_params=params,
        name=name)(*([x_re, x_im, w] + ([res] if has_res else [])))


def _s5_wgrad(a, a_off, x_re, x_im, wide_first, name):
    t = a.shape[0]
    tm, narrow, wide, weight, params = _s5_tiles(t)
    shape = (2 * S5_WIDE, S5_NARROW) if wide_first else (S5_NARROW, 2 * S5_WIDE)

    def body(a_ref, re_ref, im_ref, o_ref):
        @pl.when(pl.program_id(1) == 0)
        def _():
            o_ref[...] = jnp.zeros_like(o_ref)

        x = jnp.concatenate([re_ref[...].astype(BF16), im_ref[...].astype(BF16)], axis=1)
        av = a_ref[...].astype(BF16)
        o_ref[0] += lax.dot_general(x, av, _DIMS["tn"], preferred_element_type=F32) if wide_first else \
            lax.dot_general(av, x, _DIMS["tn"], preferred_element_type=F32)

    return pl.pallas_call(
        body, grid=(S5_SUPER, t // tm), in_specs=[narrow(a_off), wide, wide], out_specs=weight(shape),
        out_shape=jax.ShapeDtypeStruct((S5_SUPER,) + shape, F32), compiler_params=params, name=name)(a, x_re, x_im)


def _s5_scan(b_re, b_im, a_re, a_im, reverse, x_prev=None):
    t = b_re.shape[0]
    cb = 512
    ncb = S5_COLS // cb
    tb = min(1024, t)
    ntb = t // tb
    sub = 8
    shape = (sub, cb)

    def cmul(ar, ai, br, bi):
        return ar * br - ai * bi, ar * bi + ai * br

    def body(*refs):
        if reverse:
            br_ref, bi_ref, ar_ref, ai_ref, xr_ref, xi_ref, or_ref, oi_ref, dar_ref, dai_ref, carry, dacc, states = refs
        else:
            br_ref, bi_ref, ar_ref, ai_ref, or_ref, oi_ref, or_bf_ref, oi_bf_ref, carry = refs

        @pl.when(pl.program_id(1) == 0)
        def _():
            carry[...] = jnp.zeros_like(carry)
            if reverse:
                dacc[...] = jnp.zeros_like(dacc)

        row = lax.broadcasted_iota(jnp.int32, shape, 0)
        a1 = (jnp.broadcast_to(ar_ref[...], shape), jnp.broadcast_to(-ai_ref[...] if reverse else ai_ref[...], shape))
        a2 = cmul(*a1, *a1)
        a4 = cmul(*a2, *a2)
        a8 = cmul(*a4, *a4)
        steps = (sub - row) if reverse else (row + 1)
        pw = (jnp.ones(shape, F32), jnp.zeros(shape, F32))
        for bit, p in ((1, a1), (2, a2), (4, a4), (8, a8)):
            on = (steps & bit) != 0
            pw = cmul(*pw, jnp.where(on, p[0], 1.0), jnp.where(on, p[1], 0.0))
        edge = 0 if reverse else sub - 1

        def shift(v, s):
            if reverse:
                return jnp.where(row < sub - s, pltpu.roll(v, sub - s, 0), 0.0)
            return jnp.where(row >= s, pltpu.roll(v, s, 0), 0.0)

        def tile(n, state):
            r0 = pl.multiple_of((tb // sub - 1 - n if reverse else n) * sub, sub)
            rows = pl.ds(r0, sub)
            xr, xi = br_ref[rows, :], bi_ref[rows, :]
            for s, p in ((1, a1), (2, a2), (4, a4)):
                dr, di = cmul(*p, shift(xr, s), shift(xi, s))
                xr, xi = xr + dr, xi + di
            cr, ci = jnp.broadcast_to(state[0], shape), jnp.broadcast_to(state[1], shape)
            dr, di = cmul(*pw, cr, ci)
            xr, xi = xr + dr, xi + di
            if reverse:
                states[0, rows, :], states[1, rows, :] = xr, xi
                gr = jnp.where(row < sub - 1, pltpu.roll(xr, sub - 1, 0), cr)
                gi = jnp.where(row < sub - 1, pltpu.roll(xi, sub - 1, 0), ci)
                pr, pi = xr_ref[rows, :], xi_ref[rows, :]
                dacc[0] += gr * pr + gi * pi
                dacc[1] += gi * pr - gr * pi
            else:
                or_ref[rows, :], oi_ref[rows, :] = xr, xi
            return (jnp.sum(jnp.where(row == edge, xr, 0.0), axis=0, keepdims=True),
                    jnp.sum(jnp.where(row == edge, xi, 0.0), axis=0, keepdims=True))

        last = lax.fori_loop(0, tb // sub, tile, (carry[0], carry[1]))
        carry[0], carry[1] = last
        if reverse:
            or_ref[...], oi_ref[...] = states[0].astype(BF16), states[1].astype(BF16)
            dar_ref[...] = jnp.sum(dacc[0], axis=0, keepdims=True)
            dai_ref[...] = jnp.sum(dacc[1], axis=0, keepdims=True)
        else:
            or_bf_ref[...], oi_bf_ref[...] = or_ref[...].astype(BF16), oi_ref[...].astype(BF16)

    tblock = (lambda k: ntb - 1 - k) if reverse else (lambda k: k)
    re_spec = pl.BlockSpec((tb, cb), lambda j, k: (tblock(k), j))
    a_spec = pl.BlockSpec((1, cb), lambda j, k: (0, j))
    ins, in_specs = [b_re, b_im, a_re, a_im], [re_spec, re_spec, a_spec, a_spec]
    seq_bf = jax.ShapeDtypeStruct((t, S5_COLS), BF16)
    scratch = [pltpu.VMEM((2, 1, cb), F32)]
    if reverse:
        ins += list(x_prev)
        in_specs += [re_spec, re_spec]
        out_shape = [seq_bf, seq_bf] + [jax.ShapeDtypeStruct((1, S5_COLS), F32)] * 2
        out_specs = [re_spec, re_spec, a_spec, a_spec]
        scratch += [pltpu.VMEM((2, sub, cb), F32), pltpu.VMEM((2, tb, cb), F32)]
    else:
        out_shape = [jax.ShapeDtypeStruct((t, S5_COLS), F32)] * 2 + [seq_bf, seq_bf]
        out_specs = [re_spec] * 4
    return pl.pallas_call(
        body, grid=(ncb, ntb), in_specs=in_specs, out_specs=out_specs, out_shape=out_shape, scratch_shapes=scratch,
        compiler_params=pltpu.CompilerParams(dimension_semantics=("arbitrary", "arbitrary"), vmem_limit_bytes=VMEM_LIMIT),
        name="s5_scan_bwd" if reverse else "s5_scan_fwd")(*ins)


CONF_DIM = 512
CONF_K = 31
CONF_PAD = 32


def _conf_conv(proj_o, w, b, dc1=None):
    t = proj_o.shape[0]
    cb = LANES
    ncb = CONF_DIM // cb
    chunk = min(512, t)
    chunks = range(0, t, chunk)
    bwd = dc1 is not None

    def body(*refs):
        if bwd:
            ca_ref, cg_ref, w_ref, b_ref, dy_ref, dca_ref, dcg_ref, dw_ref, db_ref, xs, dys = refs
        else:
            ca_ref, cg_ref, w_ref, b_ref, o_ref, xs = refs
        xs[pl.ds(0, CONF_PAD), :] = jnp.zeros((CONF_PAD, cb), F32)
        for t0 in chunks:
            rows = pl.ds(t0, chunk)
            xs[pl.ds(CONF_PAD + t0, chunk), :] = ca_ref[rows, :] * jax.nn.sigmoid(cg_ref[rows, :])
        if not bwd:
            for t0 in chunks:
                acc = jnp.broadcast_to(b_ref[...], (chunk, cb))
                for k in range(CONF_K):
                    acc = acc + w_ref[k] * xs[pl.ds(CONF_PAD + t0 - (CONF_K - 1 - k), chunk), :]
                o_ref[pl.ds(t0, chunk), :] = acc
            return
        dys[pl.ds(t, CONF_PAD), :] = jnp.zeros((CONF_PAD, cb), F32)
        db = jnp.zeros((1, cb), F32)
        for t0 in chunks:
            dys[pl.ds(t0, chunk), :] = dy_ref[pl.ds(t0, chunk), :]
            db = db + jnp.sum(dy_ref[pl.ds(t0, chunk), :], axis=0, keepdims=True)
        db_ref[...] = db
        for t0 in chunks:
            rows = pl.ds(t0, chunk)
            acc = jnp.zeros((chunk, cb), F32)
            for k in range(CONF_K):
                acc = acc + w_ref[k] * dys[pl.ds(t0 + (CONF_K - 1 - k), chunk), :]
            sig = jax.nn.sigmoid(cg_ref[rows, :])
            dca_ref[rows, :] = (acc * sig).astype(BF16)
            dcg_ref[rows, :] = (acc * ca_ref[rows, :] * sig * (1.0 - sig)).astype(BF16)
        for k in range(CONF_K):
            dwk = jnp.zeros((1, cb), F32)
            for t0 in chunks:
                window = xs[pl.ds(CONF_PAD + t0 - (CONF_K - 1 - k), chunk), :]
                dwk = dwk + jnp.sum(dy_ref[pl.ds(t0, chunk), :] * window, axis=0, keepdims=True)
            dw_ref[k] = dwk

    col = lambda off: pl.BlockSpec((t, cb), lambda j, off=off: (0, j + off))
    w_spec = pl.BlockSpec((CONF_K, 1, cb), lambda j: (0, 0, j))
    b_spec = pl.BlockSpec((1, cb), lambda j: (0, j))
    seq = jax.ShapeDtypeStruct((t, CONF_DIM), F32)
    ins, in_specs = [proj_o, proj_o, w, b], [col(0), col(ncb), w_spec, b_spec]
    scratch = [pltpu.VMEM((CONF_PAD + t, cb), F32)]
    if bwd:
        ins, in_specs = ins + [dc1], in_specs + [col(0)]
        seq_bf = jax.ShapeDtypeStruct((t, CONF_DIM), BF16)
        out_shape, out_specs = [seq_bf, seq_bf, jax.ShapeDtypeStruct(w.shape, F32), jax.ShapeDtypeStruct(b.shape, F32)], [col(0), col(0), w_spec, b_spec]
        scratch = scratch + [pltpu.VMEM((t + CONF_PAD, cb), F32)]
    else:
        out_shape, out_specs = seq, col(0)
    return pl.pallas_call(
        body, grid=(ncb,), in_specs=in_specs, out_specs=out_specs, out_shape=out_shape, scratch_shapes=scratch,
        compiler_params=pltpu.CompilerParams(dimension_semantics=("arbitrary",), vmem_limit_bytes=VMEM_LIMIT),
        name="conf_conv_bwd" if bwd else "conf_conv")(*ins)


N_CHIPS = 4
_HBM = pl.BlockSpec(memory_space=pl.ANY)
_MESH_ID = pl.DeviceIdType.MESH


def _comm_call(body, srcs, out_shapes, n_sems, name):
    n = len(srcs)
    return pl.pallas_call(
        body, out_shape=out_shapes, in_specs=[_HBM] * n, out_specs=[_HBM] * n,
        scratch_shapes=[pltpu.SemaphoreType.DMA((n_sems,)), pltpu.SemaphoreType.DMA((n_sems,)), pltpu.SemaphoreType.DMA((n,))],
        compiler_params=pltpu.CompilerParams(has_side_effects=True), name=name)(*srcs)


GATHER_SEMS = N_DEV - 1


def _gather_phases(src_refs, out_refs, send_sems, recv_sems, local_sems):
    n = len(src_refs)
    x, y, c = lax.axis_index("x"), lax.axis_index("y"), lax.axis_index("c")
    me, sibling = (x, y, c), (x, y, 1 - c)
    chips = [(1 - x, y), (x, 1 - y), (1 - x, 1 - y)]

    def copy(i, k, block, to, from_src=False):
        rows = out_refs[i].at[4 * block[0] + 2 * block[1] + block[2]]
        return pltpu.make_async_remote_copy(
            src_ref=src_refs[i] if from_src else rows, dst_ref=rows, send_sem=send_sems.at[GATHER_SEMS * i + k],
            recv_sem=recv_sems.at[GATHER_SEMS * i + k], device_id=to, device_id_type=_MESH_ID)

    def local(i):
        return pltpu.make_async_copy(src_refs[i], out_refs[i].at[4 * x + 2 * y + c], local_sems.at[i])

    def first(i):
        return [copy(i, 0, me, sibling, True)] + [copy(i, 1 + j, me, (*chip, c), True) for j, chip in enumerate(chips)]

    def passed(i, j):
        return copy(i, 4 + j, (*chips[j], c), sibling)

    def start():
        for i in range(n):
            local(i).start()
            for cp in first(i):
                cp.start()

    def forward():
        for j, chip in enumerate(chips):
            for i in range(n):
                copy(i, 1 + j, (*chip, c), me).wait_recv()
                passed(i, j).start()

    def finish():
        for i in range(n):
            copy(i, 0, sibling, me).wait_recv()
            for j, chip in enumerate(chips):
                copy(i, 4 + j, (*chip, 1 - c), me).wait_recv()
        for i in range(n):
            for cp in first(i) + [passed(i, j) for j in range(len(chips))]:
                cp.wait_send()
            local(i).wait()

    return start, forward, finish


def _gather(srcs, name):
    n = len(srcs)

    def body(*refs):
        for phase in _gather_phases(refs[:n], refs[n:2 * n], *refs[2 * n:]):
            phase()

    return _comm_call(body, srcs, [jax.ShapeDtypeStruct((N_DEV,) + s.shape, s.dtype) for s in srcs], GATHER_SEMS * n, name)


def _swap_sibling(srcs, name):
    n = len(srcs)

    def body(*refs):
        src_refs, out_refs = refs[:n], refs[n:2 * n]
        send_sems, recv_sems, _ = refs[2 * n:]
        sibling = (lax.axis_index("x"), lax.axis_index("y"), 1 - lax.axis_index("c"))
        copies = [pltpu.make_async_remote_copy(src_ref=src_refs[i], dst_ref=out_refs[i], send_sem=send_sems.at[i],
                                               recv_sem=recv_sems.at[i], device_id=sibling, device_id_type=_MESH_ID) for i in range(n)]
        for cp in copies:
            cp.start()
        for cp in copies:
            cp.wait_recv()
        for cp in copies:
            cp.wait_send()

    return _comm_call(body, srcs, [jax.ShapeDtypeStruct(s.shape, s.dtype) for s in srcs], n, name)


EXCHANGE_SEMS = N_CHIPS - 1


def _exchange_phases(src_refs, out_refs, send_sems, recv_sems, local_sems):
    n = len(src_refs)
    x, y, c = lax.axis_index("x"), lax.axis_index("y"), lax.axis_index("c")
    mine = 2 * x + y

    def local(i):
        return pltpu.make_async_copy(src_refs[i].at[mine], out_refs[i].at[mine], local_sems.at[i])

    def copies(i, landing):
        out = []
        for k in range(1, N_CHIPS):
            px, py = x ^ (k >> 1), y ^ (k & 1)
            peer = 2 * px + py
            sem = EXCHANGE_SEMS * i + k - 1
            out.append(pltpu.make_async_remote_copy(
                src_ref=src_refs[i].at[peer], dst_ref=out_refs[i].at[peer if landing else mine], send_sem=send_sems.at[sem],
                recv_sem=recv_sems.at[sem], device_id=(px, py, c), device_id_type=_MESH_ID))
        return out

    def start():
        for i in range(n):
            local(i).start()
            for send in copies(i, False):
                send.start()

    def finish():
        for i in range(n):
            for recv in copies(i, True):
                recv.wait_recv()
        for i in range(n):
            for send in copies(i, False):
                send.wait_send()
            local(i).wait()

    return start, (lambda: None), finish


def _exchange_chips(srcs, name):
    n = len(srcs)

    def body(*refs):
        for phase in _exchange_phases(refs[:n], refs[n:2 * n], *refs[2 * n:]):
            phase()

    return _comm_call(body, srcs, [jax.ShapeDtypeStruct(s.shape, s.dtype) for s in srcs], EXCHANGE_SEMS * n, name)


def _add(a, b, name):
    k, rows, cols = a.shape
    tr = _row_tile(rows)

    def body(a_ref, b_ref, o_ref):
        o_ref[...] = (a_ref[...].astype(F32) + b_ref[...].astype(F32)).astype(o_ref.dtype)

    spec = pl.BlockSpec((k, tr, cols), lambda i: (0, i, 0))
    return pl.pallas_call(
        body, grid=(rows // tr,), in_specs=[spec, spec], out_specs=spec, out_shape=jax.ShapeDtypeStruct(a.shape, a.dtype),
        compiler_params=pltpu.CompilerParams(dimension_semantics=("parallel",), vmem_limit_bytes=VMEM_LIMIT), name=name)(a, b)


def _row_tile(r, pref=256):
    if r <= pref:
        return r
    t = pref // 16 * 16
    while r % t:
        t -= 16
    return t


def _join_cols(g, width, name):
    _, rows, ws = g.shape
    tr = _row_tile(rows)
    tail = width - N_DEV * ws

    def body(g_ref, o_ref):
        for d in range(N_DEV):
            o_ref[:, pl.ds(d * ws, ws)] = g_ref[d]
        if tail:
            o_ref[:, pl.ds(N_DEV * ws, tail)] = jnp.zeros((tr, tail), g.dtype)

    return pl.pallas_call(
        body, grid=(rows // tr,), in_specs=[pl.BlockSpec((N_DEV, tr, ws), lambda i: (0, i, 0))],
        out_specs=pl.BlockSpec((tr, width), lambda i: (i, 0)), out_shape=jax.ShapeDtypeStruct((rows, width), g.dtype),
        compiler_params=pltpu.CompilerParams(dimension_semantics=("parallel",), vmem_limit_bytes=VMEM_LIMIT), name=name)(g)


def _split_cols(full, ws, dtype, name):
    rows, width = full.shape
    tr = _row_tile(rows)

    def body(x_ref, o_ref):
        for d in range(N_DEV):
            o_ref[d] = x_ref[:, pl.ds(d * ws, ws)].astype(dtype)

    return pl.pallas_call(
        body, grid=(rows // tr,), in_specs=[pl.BlockSpec((tr, width), lambda i: (i, 0))],
        out_specs=pl.BlockSpec((N_DEV, tr, ws), lambda i: (0, i, 0)), out_shape=jax.ShapeDtypeStruct((N_DEV, rows, ws), dtype),
        compiler_params=pltpu.CompilerParams(dimension_semantics=("parallel",), vmem_limit_bytes=VMEM_LIMIT), name=name)(full)


def _adamw(parts, w, m, v, name):
    r, c = w.shape
    n_parts = parts.shape[0]
    tr = _row_tile(r)

    def body(p_ref, w_ref, m_ref, v_ref, g_ref, d_ref, nm_ref, nv_ref):
        g = p_ref[0].astype(F32)
        for s in range(1, n_parts):
            g = g + p_ref[s].astype(F32)
        nm = ADAM_B1 * m_ref[...] + (1.0 - ADAM_B1) * g
        nv = ADAM_B2 * v_ref[...] + (1.0 - ADAM_B2) * (g * g)
        m_hat = nm / (1.0 - ADAM_B1 ** ADAM_STEP)
        v_hat = nv / (1.0 - ADAM_B2 ** ADAM_STEP)
        g_ref[...] = g
        nm_ref[...] = nm
        nv_ref[...] = nv
        d_ref[...] = -ADAM_LR * (m_hat / (jnp.sqrt(v_hat) + ADAM_EPS) + ADAM_WD * w_ref[...])

    spec = pl.BlockSpec((tr, c), lambda i: (i, 0))
    return pl.pallas_call(
        body, grid=(r // tr,), in_specs=[pl.BlockSpec((n_parts, tr, c), lambda i: (0, i, 0)), spec, spec, spec],
        out_specs=[spec] * 4, out_shape=[jax.ShapeDtypeStruct((r, c), F32)] * 4,
        compiler_params=pltpu.CompilerParams(dimension_semantics=("parallel",), vmem_limit_bytes=VMEM_LIMIT),
        name=name)(parts, w, m, v)


def _pack_rows(n_elems, mult):
    rows = -(-n_elems // PACK_COLS)
    return -(-rows // mult) * mult


def _pack(arrays, dtype, mult, lead=()):
    flat = jnp.concatenate([a.astype(dtype).reshape(lead + (-1,)) for a in arrays], axis=-1)
    rows = _pack_rows(flat.shape[-1], mult)
    flat = jnp.pad(flat, [(0, 0)] * len(lead) + [(0, rows * PACK_COLS - flat.shape[-1])])
    return flat.reshape(lead + (rows, PACK_COLS))


def _unpack(buf, shapes, lead=()):
    flat = buf.reshape(lead + (-1,))
    out, off = [], 0
    for s in shapes:
        n = math.prod(s)
        out.append(flat[..., off:off + n].reshape(lead + tuple(s)))
        off += n
    return out


def _join_shards(piece, axis):
    moved = jnp.moveaxis(piece, 0, axis)
    shape = moved.shape
    return moved.reshape(shape[:axis] + (shape[axis] * shape[axis + 1],) + shape[axis + 2:])


def _split_shards(full, axis):
    shape = full.shape
    return jnp.moveaxis(full.reshape(shape[:axis] + (N_DEV, shape[axis] // N_DEV) + shape[axis + 1:]), axis, 0)


def _block_diag(blocks):
    g, r, c = blocks.shape
    k = g // S5_SUPER
    eye = jnp.eye(k, dtype=blocks.dtype)
    return (blocks.reshape(S5_SUPER, k, r, 1, c) * eye[None, :, None, :, None]).reshape(S5_SUPER, k * r, k * c)


def _diag_blocks(mat, r, c):
    k = mat.shape[1] // r
    eye = jnp.eye(k, dtype=mat.dtype)
    return jnp.sum(mat.reshape(S5_SUPER, k, r, k, c) * eye[None, :, None, :, None], axis=3).reshape(S5_SUPER * k, r, c)


def _pad_lanes(a):
    a = a.reshape(1, -1)
    return jnp.pad(a, ((0, 0), (0, LANES - a.shape[1])))


def _head_expand():
    h = jnp.arange(LANES)[:, None]
    ch = jnp.arange(SSM_HEADS * SSM_P)[None, :] // SSM_P
    return (h == ch).astype(F32)


def _rotary_tables(t):
    inv = 10000.0 ** (-jnp.arange(0, RET_DK, 2, dtype=F32) / RET_DK)
    ang = jnp.arange(t).astype(F32)[:, None] * inv[None, :]
    cos, sin = jnp.cos(ang), jnp.sin(ang)
    return jnp.concatenate([cos, cos], axis=1), jnp.concatenate([-sin, sin], axis=1)


def _rms_fwd(x, g, name):
    return _rows(_f_rms, x.shape[0], [(x, D_MODEL, 0)], [g], [D_MODEL], [], name, bf16_outs=(0,))[0]


def _rms_bwd(x, g, dh, dres, name):
    def fn(x_, dh_, dres_, g_):
        _, vjp = jax.vjp(lambda a, b: _rms(a, b), x_, g_)
        dx, dg = vjp(dh_)
        return dx + dres_, dx + dres_, dg
    return _rows(fn, x.shape[0], [(x, D_MODEL, 0), (dh, D_MODEL, 0), (dres, D_MODEL, 0)], [g], [D_MODEL, D_MODEL], [(1, D_MODEL)],
                 name, bf16_outs=(1,))


def _ffn_fwd(x, norm_g, w_up, dw_w, dw_b, w_down, tag):
    t = x.shape[0]
    nbk = D_FF // LANES
    h = _rms_fwd(x, norm_g, "ffn_norm_" + tag)
    up = _mm(h, w_up, "nn", "ffn_up_" + tag)
    mid = _cols(_f_ffnmid, nbk, [(up, 0), (up, nbk), (dw_w, 0), (dw_w, nbk), (dw_b, 0), (dw_b, nbk)], [(t,)], "ffn_mid_" + tag,
                bf16_outs=(0,))[0]
    out = _mm(mid, w_down, "nn", "ffn_down_" + tag, res=x)
    return out, (h, up, mid)


def _ffn_bwd(x, norm_g, w_up, dw_w, dw_b, w_down, saved, dout, dout_bf, tag):
    t = x.shape[0]
    nbk = D_FF // LANES
    h, up, mid = saved
    d_w_down = _mm(mid, dout_bf, "tn", "ffn_down_dw_" + tag, out_dtype=BF16)
    dmid = _mm(dout_bf, w_down, "nt", "ffn_down_dx_" + tag)
    dgin, duin, dwg, dwu, dbg, dbu = _cols(
        _grad_fn(_f_ffnmid, 6, 6), nbk,
        [(up, 0), (up, nbk), (dw_w, 0), (dw_w, nbk), (dw_b, 0), (dw_b, nbk), (dmid, 0)],
        [(t,), (t,), (3, 1), (3, 1), (1,), (1,)], "ffn_mid_bwd_" + tag, bf16_outs=(0, 1))
    dup = jnp.concatenate([dgin, duin], axis=1)
    d_w_up = _mm(h, dup, "tn", "ffn_up_dw_" + tag)
    dh = _mm(dup, w_up, "nt", "ffn_up_dx_" + tag)
    dx, dx_bf, dnorm = _rms_bwd(x, norm_g, dh, dout, "ffn_norm_bwd_" + tag)
    return dx, dx_bf, dict(norm=dnorm, w_up=d_w_up, dw_w=jnp.concatenate([dwg, dwu], axis=2)[:, 0],
                           dw_b=jnp.concatenate([dbg, dbu], axis=1), w_down=d_w_down)


def _local_step(x, tgt, w, late=None):
    t = x.shape[0]
    grads = {}
    expand = _head_expand()
    cosf, sins = _rotary_tables(t)
    mix_g = [w['mix_norm'][i:i + 1] for i in range(2)]
    ffn_g = [w['ffn_norm'][i:i + 1] for i in range(2)]
    ffn_dw_w = [w['ffn_dw_w'][i][:, None, :] for i in range(2)]
    ffn_dw_b = [w['ffn_dw_b'][i:i + 1] for i in range(2)]

    w_e = w['e_w_in']
    conv_w = w['e_conv_w'][0][:, None, :]
    conv_b = w['e_conv_b']
    dt_bias, a_log, d_skip = _pad_lanes(w['e_dt_bias']), _pad_lanes(w['e_a_log']), _pad_lanes(w['e_d'])
    h0 = _rms_fwd(x, mix_g[0], "mix_norm_0")
    proj = _mm(h0, w_e, "nn", "e_in")
    qr, kr = _rows(_f_retpre, t, [(proj, 512, 0), (proj, 512, 1), (cosf, LANES, 0), (sins, LANES, 0)], [], [512, 512], [], "ret_pre")
    if late:
        r, *landed = _ret_fwd(qr, kr, proj, t, late['ret_fwd'][0])
        late['ret_fwd'][1](w, landed)
    else:
        r = _ret_fwd(qr, kr, proj, t)
    y_ret = _rows(_f_retpost, t, [(r, 1024, 0), (proj, 1024, 2)], [], [1024], [], "ret_post", bf16_outs=(0,))[0]
    xbc_act = _cols(_f_ssdconv, 12, [(proj, 32), (conv_w, 0), (conv_b, 0)], [(t,)], "ssd_conv")[0]
    xdt, da = _rows(_f_ssdpre, t, [(xbc_act, 1024, 0), (proj, LANES, 44)], [dt_bias, a_log, expand], [1024, LANES], [], "ssd_pre")
    a_cum = _cumsum(da, False, "ssd_cumsum")
    a_cum_t = a_cum[:, :SSM_HEADS].T
    if late:
        yc, *landed = _ssd_fwd(xbc_act, xdt, a_cum, a_cum_t, t, late['ssd_fwd'][0])
        late['ssd_fwd'][1](w, landed)
    else:
        yc = _ssd_fwd(xbc_act, xdt, a_cum, a_cum_t, t)
    y_ssm = _rows(_f_ssdpost, t, [(yc, 1024, 0), (xbc_act, 1024, 0), (proj, 1024, 3)], [d_skip, w['e_ssm_norm'], expand],
                  [1024], [], "ssd_post", bf16_outs=(0,))[0]
    mix_e = jnp.concatenate([y_ret, y_ssm], axis=1)
    x1 = _mm(mix_e, w['e_w_out'], "nn", "e_out", res=x)
    x2, ffn0 = _ffn_fwd(x1, ffn_g[0], w['ffn_w_up'][0], ffn_dw_w[0], ffn_dw_b[0], w['ffn_w_down'][0], "0")

    lr, li = w['o_a_re'][0], w['o_a_im'][0]
    ls = w['o_log_step'].reshape(S5_GROUPS, 1)
    b_re3, b_im3 = jnp.transpose(w['o_b_re'][0], (2, 0, 1)), jnp.transpose(w['o_b_im'][0], (2, 0, 1))
    par_ins = [lr, li, ls, b_re3, b_im3]
    whole = lambda a: (a, a.shape, (lambda i, n=a.ndim: (0,) * n))
    par_shapes = [(S5_GROUPS, S5_STATE)] * 2 + [(S5_GROUP, S5_GROUPS, S5_STATE)] * 2
    ab_re, ab_im, bb_re, bb_im = _call(_f_s5par, (1,), [whole(a) for a in par_ins],
                                       [(s, F32, s, (lambda i, n=len(s): (0,) * n), False) for s in par_shapes], "s5_params")
    w_b = jnp.concatenate([_block_diag(jnp.transpose(bb_re, (1, 0, 2))), _block_diag(jnp.transpose(bb_im, (1, 0, 2)))], axis=2)
    w_c = jnp.concatenate([_block_diag(jnp.transpose(w['o_c_re'][0], (0, 2, 1))),
                           -_block_diag(jnp.transpose(w['o_c_im'][0], (0, 2, 1)))], axis=1)
    a_re, a_im = ab_re.reshape(1, S5_COLS), ab_im.reshape(1, S5_COLS)
    dw_w = w['o_dw_w'][0][:, None, :]
    glu_w = w['o_glu_w'].astype(F32)

    h1 = _rms_fwd(x2, mix_g[1], "mix_norm_1")
    proj_o = _mm(h1, w['o_w_in'], "nn", "o_in")
    c1 = _conf_conv(proj_o, dw_w, w['o_dw_b'])
    c2 = _rows(_f_confb, t, [(c1, 512, 0)], [w['o_ln_g'], w['o_ln_b']], [512], [], "conf_norm", bf16_outs=(0,))[0]
    u_off = 2 * CONF_DIM // S5_NARROW
    bu_re, bu_im = _s5_expand(proj_o, u_off, w_b, False, "s5_bu")
    xs_re, xs_im, xs_re_bf, xs_im_bf = _s5_scan(bu_re, bu_im, a_re, a_im, False)
    y_s5 = _s5_contract(xs_re_bf, xs_im_bf, w_c, True, "s5_cx")
    s_out = _rows(_f_s5post, t, [(y_s5, 512, 0), (proj_o, 512, 2)], [w['o_d'], glu_w], [512], [], "s5_post", bf16_outs=(0,))[0]
    mix_o = jnp.concatenate([c2, s_out], axis=1)
    x3 = _mm(mix_o, w['o_w_out'], "nn", "o_out", res=x2)
    x4, ffn1 = _ffn_fwd(x3, ffn_g[1], w['ffn_w_up'][1], ffn_dw_w[1], ffn_dw_b[1], w['ffn_w_down'][1], "1")

    dx4, dx4_bf, loss_blk, d_final = _rows(_loss_step, t, [(x4, D_MODEL, 0), (tgt, D_MODEL, 0)], [w['final_norm'].reshape(1, D_MODEL)],
                                           [D_MODEL, D_MODEL], [(8, LANES), (1, D_MODEL)], "loss_head", bf16_outs=(1,))
    loss = loss_blk[0, 0]
    grads['final_norm'] = d_final.reshape(D_MODEL)

    dx3, dx3_bf, g1 = _ffn_bwd(x3, ffn_g[1], w['ffn_w_up'][1], ffn_dw_w[1], ffn_dw_b[1], w['ffn_w_down'][1], ffn1, dx4, dx4_bf, "1")
    grads['o_w_out'] = _mm(mix_o, dx3_bf, "tn", "o_out_dw", out_dtype=BF16)
    dmix_o = _mm(dx3_bf, w['o_w_out'], "nt", "o_out_dx")
    dc1, d_ln_g, d_ln_b = _conf_norm_bwd(c1, w['o_ln_g'], w['o_ln_b'], dmix_o, t)
    dca, dcg, d_dw_w, d_dw_b = _conf_conv(proj_o, dw_w, w['o_dw_b'], dc1)
    dyc, du_skip, d_od, d_glu = _s5_post_bwd(y_s5, proj_o, w['o_d'], glu_w, dmix_o, t)
    grads['o_glu_w'] = d_glu
    d_w_c = _s5_wgrad(dyc, 0, xs_re_bf, xs_im_bf, True, "s5_cx_dw")
    dxs_re, dxs_im = _s5_expand(dyc, 0, w_c, True, "s5_cx_dx")
    g_re, g_im, d_are, d_aim = _s5_scan(dxs_re, dxs_im, a_re, a_im, True, (xs_re, xs_im))
    d_w_b = _s5_wgrad(proj_o, u_off, g_re, g_im, False, "s5_bu_dw")
    du = _s5_contract(g_re, g_im, w_b, False, "s5_bu_dx", res=du_skip, out_dtype=BF16)
    d_bb_re = jnp.transpose(_diag_blocks(d_w_b[:, :, :S5_WIDE], S5_GROUP, S5_STATE), (1, 0, 2))
    d_bb_im = jnp.transpose(_diag_blocks(d_w_b[:, :, S5_WIDE:], S5_GROUP, S5_STATE), (1, 0, 2))
    par_cts = [d_are.reshape(S5_GROUPS, S5_STATE), d_aim.reshape(S5_GROUPS, S5_STATE), d_bb_re, d_bb_im]
    in_shapes = [a.shape for a in par_ins]
    d_lr, d_li, d_ls, d_br3, d_bi3 = _call(_grad_fn(_f_s5par, 5, 5), (1,), [whole(a) for a in par_ins + par_cts],
                                           [(s, F32, s, (lambda i, n=len(s): (0,) * n), False) for s in in_shapes], "s5_params_bwd")
    grads['o_a_re'], grads['o_a_im'], grads['o_log_step'] = d_lr[None], d_li[None], d_ls.reshape(1, S5_GROUPS)
    grads['o_b_re'], grads['o_b_im'] = jnp.transpose(d_br3, (1, 2, 0))[None], jnp.transpose(d_bi3, (1, 2, 0))[None]
    grads['o_c_re'] = jnp.transpose(_diag_blocks(d_w_c[:, :S5_WIDE], S5_STATE, S5_GROUP), (0, 2, 1))[None]
    grads['o_c_im'] = -jnp.transpose(_diag_blocks(d_w_c[:, S5_WIDE:], S5_STATE, S5_GROUP), (0, 2, 1))[None]
    grads['o_d'], grads['o_ln_g'], grads['o_ln_b'] = d_od, d_ln_g, d_ln_b
    grads['o_dw_w'], grads['o_dw_b'] = d_dw_w[:, 0][None], d_dw_b
    dproj_o = jnp.concatenate([dca, dcg, du], axis=1)
    grads['o_w_in'] = _mm(h1, dproj_o, "tn", "o_in_dw")
    dh1 = _mm(dproj_o, w['o_w_in'], "nt", "o_in_dx")
    dx2, dx2_bf, d_mix1 = _rms_bwd(x2, mix_g[1], dh1, dx3, "mix_norm_bwd_1")
    in_flight = late['grads_ready']({**grads, 'ffn_w_up': [None, g1['w_up']], 'ffn_w_down': [None, g1['w_down']]}) if late else ()

    dx1, dx1_bf, g0 = _ffn_bwd(x1, ffn_g[0], w['ffn_w_up'][0], ffn_dw_w[0], ffn_dw_b[0], w['ffn_w_down'][0], ffn0, dx2, dx2_bf, "0")
    in_flight_ffn = late['ffn_grads_ready']({'ffn_w_up': [g0['w_up'], None], 'ffn_w_down': [g0['w_down'], None]}) if late else ()
    grads['e_w_out'] = _mm(mix_e, dx1_bf, "tn", "e_out_dw", out_dtype=BF16)
    dmix_e = _mm(dx1_bf, w['e_w_out'], "nt", "e_out_dx")
    dr, dg = _rows(_grad_fn(_f_retpost, 2, 2), t, [(r, 1024, 0), (proj, 1024, 2), (dmix_e, 1024, 0)], [], [1024, 1024], [], "ret_post_bwd",
                   bf16_outs=(0, 1))
    dqr, dkr, dv, landed = _ret_bwd(qr, kr, proj, dr, t, exchange=in_flight_ffn)
    if late:
        late['ffn_grads_landed'](landed)
    dq, dk = _ret_pre_bwd(proj, cosf, sins, dqr, dkr, t)
    dyc0, dxs1, dz, d_dskip, d_ssm_norm = _ssd_post_bwd(yc, xbc_act, proj, d_skip, w['e_ssm_norm'], expand, dmix_e, t)
    dcm, da_q, dbm, dxdt, da_k, da_k_t, landed = _ssd_bwd(xbc_act, xdt, a_cum, a_cum_t, dyc0, expand, t, exchange=in_flight)
    if late:
        late['grads_landed'](landed)
    d_a_cum = da_q - da_k - jnp.pad(da_k_t.T, ((0, 0), (0, LANES - SSM_HEADS)))
    dda = _cumsum(d_a_cum, True, "ssd_cumsum_bwd")
    dxs, ddtr, d_dt_bias, d_a_log = _ssd_pre_bwd(xbc_act, proj, dt_bias, a_log, expand, dxdt, dda, dxs1, t)
    dxbc_act = jnp.concatenate([dxs, dbm, dcm], axis=1)
    dxbc, d_conv_w, d_conv_b = _cols(_grad_fn(_f_ssdconv, 3, 3), 12, [(proj, 32), (conv_w, 0), (conv_b, 0), (dxbc_act, 0)],
                                     [(t,), (4, 1), (1,)], "ssd_conv_bwd", bf16_outs=(0,))
    dproj = jnp.concatenate([dq, dk, dv, dg, dz, dxbc, ddtr], axis=1)
    grads['e_w_in'] = _mm(h0, dproj, "tn", "e_in_dw")
    dh0 = _mm(dproj, w_e, "nt", "e_in_dx")
    dx0, _, d_mix0 = _rms_bwd(x, mix_g[0], dh0, dx1, "mix_norm_bwd_0")

    grads['mix_norm'] = jnp.concatenate([d_mix0, d_mix1], axis=0)
    grads['e_conv_w'], grads['e_conv_b'] = d_conv_w[:, 0][None], d_conv_b
    grads['e_dt_bias'], grads['e_a_log'], grads['e_d'] = d_dt_bias[:, :SSM_HEADS], d_a_log[:, :SSM_HEADS], d_dskip[:, :SSM_HEADS]
    grads['e_ssm_norm'] = d_ssm_norm
    grads['ffn_norm'] = jnp.concatenate([g0['norm'], g1['norm']], axis=0)
    grads['ffn_w_up'], grads['ffn_w_down'] = [g0['w_up'], g1['w_up']], [g0['w_down'], g1['w_down']]
    grads['ffn_dw_w'] = jnp.stack([g0['dw_w'], g1['dw_w']], axis=0)
    grads['ffn_dw_b'] = jnp.concatenate([g0['dw_b'], g1['dw_b']], axis=0)
    return loss, dx0, grads


def _conf_norm_bwd(c1, ln_g, ln_b, dmix_o, t):
    def fn(c1_, dy_, g_, b_):
        _, vjp = jax.vjp(lambda a, b, c: _f_confb(a, b, c)[0], c1_, g_, b_)
        return vjp(dy_)
    return _rows(fn, t, [(c1, 512, 0), (dmix_o, 512, 0)], [ln_g, ln_b], [512], [(1, 512), (1, 512)], "conf_norm_bwd")


def _s5_post_bwd(y_s5, proj_o, d_skip, glu_w, dmix_o, t):
    def fn(yc_, u_, dy_, d_, gw_):
        _, vjp = jax.vjp(lambda a, b, c, e: _f_s5post(a, b, c, e)[0], yc_, u_, d_, gw_)
        return vjp(dy_)
    return _rows(fn, t, [(y_s5, 512, 0), (proj_o, 512, 2), (dmix_o, 512, 1)], [d_skip, glu_w], [512, 512], [(1, 512), (512, 512)],
                 "s5_post_bwd", bf16_outs=(0,))


def _ret_pre_bwd(proj, cosf, sins, dqr, dkr, t):
    def fn(q_, k_, cos_, sin_, dq_, dk_):
        _, vjp = jax.vjp(lambda a, b: _f_retpre(a, b, cos_, sin_), q_, k_)
        return vjp((dq_, dk_))
    return _rows(fn, t, [(proj, 512, 0), (proj, 512, 1), (cosf, LANES, 0), (sins, LANES, 0), (dqr, 512, 0), (dkr, 512, 0)], [],
                 [512, 512], [], "ret_pre_bwd", bf16_outs=(0, 1))


def _ssd_post_bwd(yc, xbc_act, proj, d_skip, norm_w, expand, dmix_e, t):
    def fn(yc_, xs_, z_, dy_, d_, nw_, e_):
        _, vjp = jax.vjp(lambda a, b, c, dd, n: _f_ssdpost(a, b, c, dd, n, e_)[0], yc_, xs_, z_, d_, nw_)
        return vjp(dy_)
    return _rows(fn, t, [(yc, 1024, 0), (xbc_act, 1024, 0), (proj, 1024, 3), (dmix_e, 1024, 1)], [d_skip, norm_w, expand],
                 [1024, 1024, 1024], [(1, LANES), (1, 1024)], "ssd_post_bwd", bf16_outs=(0, 2))


def _ssd_pre_bwd(xbc_act, proj, dt_bias, a_log, expand, dxdt, dda, dxs1, t):
    def fn(xs_, dtr_, dx_, dda_, dxs1_, bias_, alog_, e_):
        _, vjp = jax.vjp(lambda a, b, c, dd: _f_ssdpre(a, b, c, dd, e_), xs_, dtr_, bias_, alog_)
        dxs, ddtr, dbias, dalog = vjp((dx_, dda_))
        return dxs + dxs1_, ddtr, dbias, dalog
    return _rows(fn, t, [(xbc_act, 1024, 0), (proj, LANES, 44), (dxdt, 1024, 0), (dda, LANES, 0), (dxs1, 1024, 0)],
                 [dt_bias, a_log, expand], [1024, LANES], [(1, LANES), (1, LANES)], "ssd_pre_bwd", bf16_outs=(1,))


def kernel(x, mix_norm, e_w_in, e_conv_w, e_conv_b, e_dt_bias, e_a_log, e_d, e_ssm_norm, e_w_out, o_w_in, o_dw_w, o_dw_b, o_ln_g, o_ln_b, o_a_re, o_a_im, o_b_re, o_b_im, o_c_re, o_c_im, o_d, o_log_step, o_glu_w, o_w_out, ffn_norm, ffn_w_up, ffn_dw_w, ffn_dw_b, ffn_w_down, final_norm, loss_target, m_mix_norm, m_e_w_in, m_e_conv_w, m_e_conv_b, m_e_dt_bias, m_e_a_log, m_e_d, m_e_ssm_norm, m_e_w_out, m_o_w_in, m_o_dw_w, m_o_dw_b, m_o_ln_g, m_o_ln_b, m_o_a_re, m_o_a_im, m_o_b_re, m_o_b_im, m_o_c_re, m_o_c_im, m_o_d, m_o_log_step, m_o_glu_w, m_o_w_out, m_ffn_norm, m_ffn_w_up, m_ffn_dw_w, m_ffn_dw_b, m_ffn_w_down, m_final_norm, v_mix_norm, v_e_w_in, v_e_conv_w, v_e_conv_b, v_e_dt_bias, v_e_a_log, v_e_d, v_e_ssm_norm, v_e_w_out, v_o_w_in, v_o_dw_w, v_o_dw_b, v_o_ln_g, v_o_ln_b, v_o_a_re, v_o_a_im, v_o_b_re, v_o_b_im, v_o_c_re, v_o_c_im, v_o_d, v_o_log_step, v_o_glu_w, v_o_w_out, v_ffn_norm, v_ffn_w_up, v_ffn_dw_w, v_ffn_dw_b, v_ffn_w_down, v_final_norm):
    p = dict(locals())

    kinds = ("grad_", "delta_", "new_m_", "new_v_")

    def block(name, layer):
        return p[name][0 if layer is None else layer]

    def shards(blocks):
        return [block(n, layer).astype(BF16) for n, layer, _ in blocks]

    def place(w, gathered, blocks):
        for (n, layer, by_cols), g in zip(blocks, gathered):
            if by_cols:
                full = _join_cols(g, E_IN_PAD if n == 'e_w_in' else N_DEV * g.shape[2], f"join_{n}_{layer}")
            else:
                full = g.reshape(N_DEV * g.shape[1], g.shape[2])
            if layer is None:
                w[n] = full
            else:
                w[n][layer] = full

    first = [b for b in MATMUL_BLOCKS if b[0].startswith('e_')]
    with_ret = [b for b in MATMUL_BLOCKS if b[0].startswith('ffn_') and b[1] == 0]
    with_ssd = [b for b in MATMUL_BLOCKS if b not in first and b not in with_ret]
    gathered = _gather(shards(first) + [_pack([p[n] for n in SMALL_SHARDED], F32, 16)], "gather_first_weights")
    w = {n: p[n] for n in REPLICATED}
    w['ffn_w_up'], w['ffn_w_down'] = [None, None], [None, None]
    place(w, gathered, first)
    for n, piece in zip(SMALL_SHARDED, _unpack(gathered[-1], [p[n].shape for n in SMALL_SHARDED], lead=(N_DEV,))):
        w[n] = _join_shards(piece, SHARDED[n])
    late = {'ret_fwd': (shards(with_ret), functools.partial(place, blocks=with_ret)),
            'ssd_fwd': (shards(with_ssd), functools.partial(place, blocks=with_ssd))}

    def chip_sums(grads, blocks, extra, tag):
        sends = []
        for n, layer, by_cols in blocks:
            g = grads[n] if layer is None else grads[n][layer]
            if by_cols:
                sends.append(_split_cols(g, block(n, layer).shape[1], BF16, f"split_{n}_{layer}"))
            else:
                sends.append(g.astype(BF16).reshape(N_DEV, -1, g.shape[1]))
        core = lax.axis_index("c")
        by_core = [s.reshape((N_CHIPS, 2) + s.shape[1:]) for s in sends + extra]
        keep = [lax.dynamic_index_in_dim(s, core, axis=1, keepdims=False) for s in by_core]
        give = [lax.dynamic_index_in_dim(s, 1 - core, axis=1, keepdims=False) for s in by_core]
        got = _swap_sibling(give, "swap_sibling_grads_" + tag)
        return [_add(a, b, f"chip_sum_{tag}_{i}") for i, (a, b) in enumerate(zip(keep, got))]

    landed_late = []
    late['grads_ready'] = lambda grads: chip_sums(grads, with_ssd, [], "late")
    late['grads_landed'] = landed_late.extend
    landed_ffn = []
    late['ffn_grads_ready'] = lambda grads: chip_sums(grads, with_ret, [], "ffn")
    late['ffn_grads_landed'] = landed_ffn.extend

    loss, dx, grads = _local_step(x[0], loss_target[0], w, late)
    loss = lax.psum(loss, MESH_AXES)

    early = first
    small_send = _pack([_split_shards(grads[n].reshape(p[n].shape[:SHARDED[n]] + (-1,) + p[n].shape[SHARDED[n] + 1:]), SHARDED[n])
                        for n in SMALL_SHARDED], F32, 128, lead=(N_DEV,))
    landed_early = list(_exchange_chips(chip_sums(grads, early, [small_send], "early"), "exchange_chip_grads"))
    part_of = {(n, layer): part for (n, layer, _), part in zip(early + with_ret + with_ssd, landed_early[:-1] + landed_ffn + landed_late)}
    parts = [landed_early[-1], _gather([_pack([grads[n].reshape(p[n].shape) for n in REPLICATED], F32, 128)], "gather_replicated_grads")[0]]

    out, by_layer = {}, {}
    for n, layer, _ in MATMUL_BLOCKS:
        by_layer.setdefault(n, {})[layer] = _adamw(part_of[(n, layer)], *[block(pre + n, layer) for pre in ("", "m_", "v_")],
                                                   f"adamw_{n}_{layer}")
    for n, res in by_layer.items():
        for i, kind in enumerate(kinds):
            out[kind + n] = res[None][i][None] if None in res else jnp.stack([res[0][i], res[1][i]], axis=0)
    for names, part, tag in ((SMALL_SHARDED, parts[-2], "small"), (REPLICATED, parts[-1], "replicated")):
        packed = [_pack([p[pre + n] for n in names], F32, 128) for pre in ("", "m_", "v_")]
        for kind, buf in zip(kinds, _adamw(part, *packed, "adamw_" + tag)):
            for n, a in zip(names, _unpack(buf, [p[n].shape for n in names])):
                out[kind + n] = a
    return (loss, dx[None], *[out[kind + n] for kind in kinds for n in WEIGHTS])
```

```python
import functools
import math

import jax
import jax.numpy as jnp
from jax import lax
from jax.experimental import pallas as pl
from jax.experimental.pallas import tpu as pltpu

F32, BF16 = jnp.float32, jnp.bfloat16
HIGHEST = lax.Precision.HIGHEST
N_DEV = 8
MESH_AXES = ("x", "y", "c")
VMEM_LIMIT = 48 * 1024 * 1024
LANES = 128
PACK_COLS = 1024

D_MODEL = 1024
EPS = 1e-6
RET_HEADS, RET_DK, RET_DV = 4, 128, 256
SSM_HEADS, SSM_P, SSM_N, SSM_GROUPS = 16, 64, 128, 2
SSM_HG = SSM_HEADS // SSM_GROUPS
S5_GROUPS, S5_GROUP, S5_STATE = 32, 16, 64
S5_COLS = S5_GROUPS * S5_STATE
D_FF = 2816
E_IN, E_IN_PAD = 5648, 5760
ADAM_LR, ADAM_B1, ADAM_B2, ADAM_EPS, ADAM_WD, ADAM_STEP = 0.001, 0.9, 0.999, 1e-08, 0.01, 10

WEIGHTS = ['mix_norm', 'e_w_in', 'e_conv_w', 'e_conv_b', 'e_dt_bias', 'e_a_log', 'e_d', 'e_ssm_norm', 'e_w_out', 'o_w_in', 'o_dw_w', 'o_dw_b', 'o_ln_g', 'o_ln_b', 'o_a_re', 'o_a_im', 'o_b_re', 'o_b_im', 'o_c_re', 'o_c_im', 'o_d', 'o_log_step', 'o_glu_w', 'o_w_out', 'ffn_norm', 'ffn_w_up', 'ffn_dw_w', 'ffn_dw_b', 'ffn_w_down', 'final_norm']
SHARDED = {'e_w_in': 2, 'e_conv_w': 2, 'e_w_out': 1, 'o_w_in': 2, 'o_dw_w': 2, 'o_dw_b': 1, 'o_ln_g': 1, 'o_ln_b': 1,
           'o_d': 1, 'o_glu_w': 1, 'o_w_out': 1, 'ffn_w_up': 2, 'ffn_dw_w': 2, 'ffn_w_down': 1}
MATMUL_WEIGHTS = ['e_w_in', 'e_w_out', 'o_w_in', 'o_glu_w', 'o_w_out', 'ffn_w_up', 'ffn_w_down']
MATMUL_BLOCKS = [('e_w_in', None, True), ('e_w_out', None, False), ('o_w_in', None, True), ('o_glu_w', None, False),
                 ('o_w_out', None, False), ('ffn_w_up', 0, True), ('ffn_w_up', 1, True), ('ffn_w_down', 0, False), ('ffn_w_down', 1, False)]
SMALL_SHARDED = [n for n in WEIGHTS if n in SHARDED and n not in MATMUL_WEIGHTS]
REPLICATED = [n for n in WEIGHTS if n not in SHARDED]


def _call(fn, grid, ins, outs, name):
    n_in = len(ins)

    def body(*refs):
        vals = fn(*[r[...] for r in refs[:n_in]])
        first = pl.program_id(0) == 0
        for r, v, o in zip(refs[n_in:], vals, outs):
            if o[4]:
                @pl.when(first)
                def _():
                    r[...] = jnp.zeros_like(r)
                r[...] += v.astype(r.dtype)
            else:
                r[...] = v.astype(r.dtype)

    return pl.pallas_call(
        body, grid=grid,
        in_specs=[pl.BlockSpec(b, m) for _, b, m in ins],
        out_specs=[pl.BlockSpec(o[2], o[3]) for o in outs],
        out_shape=[jax.ShapeDtypeStruct(o[0], o[1]) for o in outs],
        compiler_params=pltpu.CompilerParams(dimension_semantics=("arbitrary",) * len(grid), vmem_limit_bytes=VMEM_LIMIT),
        name=name)(*[a for a, _, _ in ins])


def _rows(fn, n_rows, row_ins, full_ins, row_outs, acc_outs, name, tm=512, bf16_outs=()):
    tm = min(tm, n_rows)
    ins = [(a, (tm, w), (lambda i, c=c: (i, c))) for a, w, c in row_ins]
    ins += [(a, a.shape, (lambda i, n=a.ndim: (0,) * n)) for a in full_ins]
    outs = [((n_rows, w), BF16 if k in bf16_outs else F32, (tm, w), (lambda i: (i, 0)), False) for k, w in enumerate(row_outs)]
    outs += [(tuple(s), F32, tuple(s), (lambda i, n=len(s): (0,) * n), True) for s in acc_outs]
    return _call(fn, (n_rows // tm,), ins, outs, name)


def _cols(fn, n_blocks, col_ins, out_leads, name, cb=LANES, bf16_outs=()):
    ins = [(a, a.shape[:-1] + (cb,), (lambda j, n=a.ndim, o=o: (0,) * (n - 1) + (j + o,))) for a, o in col_ins]
    outs = [(tuple(s) + (n_blocks * cb,), BF16 if k in bf16_outs else F32, tuple(s) + (cb,), (lambda j, n=len(s): (0,) * n + (j,)), False)
            for k, s in enumerate(out_leads)]
    return _call(fn, (n_blocks,), ins, outs, name)


def _grad_fn(f, n_diff, n_in):
    def g(*a):
        diff, consts, cts = a[:n_diff], a[n_diff:n_in], a[n_in:]
        _, vjp = jax.vjp(lambda *d: f(*d, *consts), *diff)
        return vjp(tuple(cts))
    return g


def _silu(x):
    return x * jax.nn.sigmoid(x)


def _rms(x, g):
    return x * lax.rsqrt(jnp.mean(x * x, axis=-1, keepdims=True) + EPS) * g


@jax.custom_vjp
def _softplus(x):
    return jnp.maximum(x, 0.0) + jnp.log(1.0 + jnp.exp(-jnp.abs(x)))


_softplus.defvjp(lambda x: (_softplus(x), x), lambda x, g: (g * jax.nn.sigmoid(x),))


@jax.custom_vjp
def _swap_halves(x):
    return pltpu.roll(x, 64, 1)


_swap_halves.defvjp(lambda x: (_swap_halves(x), None), lambda _, g: (_swap_halves(g),))


def _shift_rows(x, k, up):
    if k == 0:
        return x
    n = x.shape[0]
    t = lax.broadcasted_iota(jnp.int32, x.shape, 0)
    if up:
        return jnp.where(t < n - k, pltpu.roll(x, n - k, 0), 0.0)
    return jnp.where(t >= k, pltpu.roll(x, k, 0), 0.0)


@jax.custom_vjp
def _dwconv(x, w, b):
    k_taps = w.shape[0]
    y = b + w[k_taps - 1] * x
    for k in range(k_taps - 1):
        y = y + w[k] * _shift_rows(x, k_taps - 1 - k, False)
    return y


def _dwconv_fwd(x, w, b):
    return _dwconv(x, w, b), (x, w)


def _dwconv_bwd(saved, dy):
    x, w = saved
    k_taps = w.shape[0]
    dx = w[k_taps - 1] * dy
    dws = []
    for k in range(k_taps - 1):
        s = k_taps - 1 - k
        dx = dx + w[k] * _shift_rows(dy, s, True)
        dws.append(jnp.sum(dy * _shift_rows(x, s, False), axis=0, keepdims=True)[None])
    dws.append(jnp.sum(dy * x, axis=0, keepdims=True)[None])
    return dx, jnp.concatenate(dws, axis=0), jnp.sum(dy, axis=0, keepdims=True)


_dwconv.defvjp(_dwconv_fwd, _dwconv_bwd)


def _f_rms(x, g):
    return (_rms(x, g),)


def _rot(x, cosf, sins):
    outs = []
    for h in range(RET_HEADS):
        xh = x[:, h * RET_DK:(h + 1) * RET_DK]
        outs.append(xh * cosf + _swap_halves(xh) * sins)
    return jnp.concatenate(outs, axis=1)


def _f_retpre(q, k, cosf, sins):
    return _rot(q, cosf, sins), _rot(k, cosf, sins) * (RET_DK ** -0.5)


def _f_retpost(r, g):
    outs = []
    for h in range(RET_HEADS):
        rh = r[:, h * RET_DV:(h + 1) * RET_DV]
        rc = rh - jnp.mean(rh, axis=-1, keepdims=True)
        outs.append(_silu(g[:, h * RET_DV:(h + 1) * RET_DV]) * (rc * lax.rsqrt(jnp.mean(rc * rc, axis=-1, keepdims=True) + EPS)))
    return (jnp.concatenate(outs, axis=1),)


def _f_ssdconv(xbc, w, b):
    return (_silu(_dwconv(xbc, w, b)),)


def _f_ssdpre(xs, dtr, bias, alog, expand):
    dt = _softplus(dtr + bias)
    return xs * jnp.dot(dt, expand, precision=HIGHEST, preferred_element_type=F32), dt * (-jnp.exp(alog))


def _f_ssdpost(yc, xs, z, dskip, norm_w, expand):
    d_wide = jnp.dot(jnp.broadcast_to(dskip, (yc.shape[0], LANES)), expand, precision=HIGHEST, preferred_element_type=F32)
    y = (yc + d_wide * xs) * _silu(z)
    half = y.shape[1] // SSM_GROUPS
    outs = []
    for g in range(SSM_GROUPS):
        yg = y[:, g * half:(g + 1) * half]
        outs.append(yg * lax.rsqrt(jnp.mean(yg * yg, axis=-1, keepdims=True) + EPS))
    return (jnp.concatenate(outs, axis=1) * norm_w,)


def _f_ffnmid(gin, uin, wg, wu, bg, bu):
    return (_silu(_dwconv(gin, wg, bg)) * _dwconv(uin, wu, bu),)


def _f_confb(c1, g, b):
    mu = jnp.mean(c1, axis=-1, keepdims=True)
    xc = c1 - mu
    return (_silu(xc * lax.rsqrt(jnp.mean(xc * xc, axis=-1, keepdims=True) + EPS) * g + b),)


def _f_s5post(yc, u, dskip, glu_w):
    s = jax.nn.gelu(yc + dskip * u)
    z = jnp.dot(s.astype(BF16), glu_w.astype(BF16), preferred_element_type=F32)
    return (s * jax.nn.sigmoid(z),)


def _f_s5par(lr, li, ls, br, bi):
    step = jnp.exp(ls)
    mag = jnp.exp(lr * step)
    ab_re = mag * jnp.cos(li * step)
    ab_im = mag * jnp.sin(li * step)
    den = lr * lr + li * li
    f_re = ((ab_re - 1.0) * lr + ab_im * li) / den
    f_im = (ab_im * lr - (ab_re - 1.0) * li) / den
    return ab_re, ab_im, f_re[None] * br - f_im[None] * bi, f_re[None] * bi + f_im[None] * br


def _loss_step(x, tgt, g):
    def f(x_, g_):
        e = _rms(x_, g_) - tgt
        return 0.5 * jnp.sum(jnp.mean(e * e, axis=-1, keepdims=True), axis=0, keepdims=True)
    loss, vjp = jax.vjp(f, x, g)
    dx, dg = vjp(jnp.ones((1, 1), F32))
    return dx, dx, jnp.broadcast_to(loss, (8, LANES)), dg


def _tile(n, pref):
    if n <= pref:
        return n
    t = (pref // LANES) * LANES
    while n % t:
        t -= LANES
    return t


_DIMS = {"nn": (((1,), (0,)), ((), ())), "nt": (((1,), (1,)), ((), ())), "tn": (((0,), (0,)), ((), ()))}


def _mm(a, b, mode, name, res=None, out_dtype=F32, tm=1024, tn=1408, tk=1408):
    if mode == "nn":
        (m, k), n = a.shape, b.shape[1]
    elif mode == "nt":
        (m, k), n = a.shape, b.shape[0]
    else:
        (k, m), n = a.shape, b.shape[1]
    tm, tn, tk = _tile(m, tm), _tile(n, tn), _tile(k, tk)
    nk = k // tk
    a_spec = pl.BlockSpec((tk, tm), lambda i, j, kk: (kk, i)) if mode == "tn" else pl.BlockSpec((tm, tk), lambda i, j, kk: (i, kk))
    b_spec = pl.BlockSpec((tn, tk), lambda i, j, kk: (j, kk)) if mode == "nt" else pl.BlockSpec((tk, tn), lambda i, j, kk: (kk, j))
    o_spec = pl.BlockSpec((tm, tn), lambda i, j, kk: (i, j))
    has_res = res is not None

    def body(*refs):
        a_ref, b_ref = refs[0], refs[1]
        o_ref, acc = refs[-2], refs[-1]
        kk = pl.program_id(2)

        @pl.when(kk == 0)
        def _():
            acc[...] = jnp.zeros_like(acc)

        acc[...] += lax.dot_general(a_ref[...].astype(BF16), b_ref[...].astype(BF16), _DIMS[mode], preferred_element_type=F32)

        @pl.when(kk == nk - 1)
        def _():
            o_ref[...] = (acc[...] + refs[2][...] if has_res else acc[...]).astype(out_dtype)

    return pl.pallas_call(
        body, grid=(m // tm, n // tn, nk),
        in_specs=[a_spec, b_spec] + ([o_spec] if has_res else []),
        out_specs=o_spec, out_shape=jax.ShapeDtypeStruct((m, n), out_dtype),
        scratch_shapes=[pltpu.VMEM((tm, tn), F32)],
        compiler_params=pltpu.CompilerParams(dimension_semantics=("parallel", "parallel", "arbitrary"), vmem_limit_bytes=VMEM_LIMIT),
        name=name)(*([a, b] + ([res] if has_res else [])))


def _seq_block(t):
    return min(512, t)


def _ret_block(t):
    return min(2048, t)


def _causal_diff(i, j, blk):
    r = lax.broadcasted_iota(jnp.int32, (blk, blk), 0)
    c = lax.broadcasted_iota(jnp.int32, (blk, blk), 1)
    return (i - j) * blk + r - c


def _ret_decay(lg, i, j, blk):
    diff = _causal_diff(i, j, blk)
    return jnp.where(diff >= 0, jnp.exp(lg * jnp.maximum(diff, 0).astype(F32)), 0.0)


def _pair_call(body, lead_grid, nb, key_major, in_specs, out_specs, out_shape, scratch, name, args, gather=(), exchange=()):
    pairs = [(i, j) for j in range(nb) for i in range(j, nb)] if key_major else [(i, j) for i in range(nb) for j in range(i + 1)]
    tables = [jnp.array([p[k] for p in pairs], jnp.int32) for k in (0, 1)]
    lead = len(lead_grid)
    n_steps = math.prod(lead_grid) * len(pairs)
    behind = list(gather) or list(exchange)
    n_bg = len(behind)
    phases, per_array = (_gather_phases, GATHER_SEMS) if gather else (_exchange_phases, EXCHANGE_SEMS)

    def on_pairs(spec):
        if spec.block_shape is None:
            return spec

        def index_map(*a):
            i, j = a[lead + 1][a[lead]], a[lead + 2][a[lead]]
            return spec.index_map(*a[:lead], *((j, i) if key_major else (i, j)))

        return pl.BlockSpec(spec.block_shape, index_map)

    many = isinstance(out_specs, (list, tuple))
    out_specs, out_shape = (list(out_specs), list(out_shape)) if many else ([out_specs], [out_shape])
    n_in, n_out = len(in_specs), len(out_specs)

    def wrapped(i_ref, j_ref, *refs):
        p = pl.program_id(lead)
        own = refs[:n_in] + refs[n_in + n_bg:n_in + n_bg + n_out] + refs[n_in + 2 * n_bg + n_out:len(refs) - (3 if n_bg else 0)]
        if n_bg:
            step = p + (pl.program_id(0) * len(pairs) if lead else 0)
            start, forward, finish = phases(refs[n_in:n_in + n_bg], refs[n_in + n_bg + n_out:n_in + 2 * n_bg + n_out], *refs[-3:])
            pl.when(step == 0)(start)
        body(i_ref[p], j_ref[p], *own)
        if n_bg:
            pl.when(step == n_steps * 3 // 4)(forward)
            pl.when(step == n_steps - 1)(finish)

    sems = [pltpu.SemaphoreType.DMA((per_array * n_bg,))] * 2 + [pltpu.SemaphoreType.DMA((n_bg,))] if n_bg else []
    landed = [jax.ShapeDtypeStruct(((N_DEV,) + g.shape) if gather else g.shape, g.dtype) for g in behind]
    grid_spec = pltpu.PrefetchScalarGridSpec(
        num_scalar_prefetch=2, grid=tuple(lead_grid) + (len(pairs),), in_specs=[on_pairs(s) for s in in_specs] + [_HBM] * n_bg,
        out_specs=[on_pairs(s) for s in out_specs] + [_HBM] * n_bg, scratch_shapes=list(scratch) + sems)
    res = pl.pallas_call(
        wrapped, grid_spec=grid_spec, out_shape=out_shape + landed,
        compiler_params=pltpu.CompilerParams(dimension_semantics=("arbitrary",) * (lead + 1), vmem_limit_bytes=VMEM_LIMIT),
        name=name)(*tables, *args, *behind)
    return res if many or n_bg else res[0]


def _ret_row_decays(lg, i, j, blk):
    row = lax.broadcasted_iota(jnp.int32, (blk, RET_DK), 0)
    return jnp.exp(lg * row.astype(F32)), jnp.exp(lg * ((i - j) * blk - row).astype(F32))


def _ret_scaled(lg, i, j, blk, q_ref, k_ref):
    a, b = _ret_row_decays(lg, i, j, blk)
    return (q_ref[...] * a).astype(BF16), (k_ref[...] * b).astype(BF16)


def _ret_log_gamma():
    return jnp.log1p(-(2.0 ** (-5.0 - jnp.arange(RET_HEADS, dtype=F32))))


def _ret_fwd(qr, kr, proj, t, gather=()):
    blk = _ret_block(t)
    nb = t // blk
    v_off = (2 * RET_HEADS * RET_DK) // RET_DV

    def body(i, j, lg_ref, q_ref, k_ref, v_ref, o_ref, acc):
        h = pl.program_id(0)

        @pl.when(j == 0)
        def _():
            acc[...] = jnp.zeros_like(acc)

        @pl.when(j < i)
        def _():
            qa, kb = _ret_scaled(lg_ref[h], i, j, blk, q_ref, k_ref)
            p = lax.dot_general(qa, kb, _DIMS["nt"], preferred_element_type=F32).astype(BF16)
            acc[...] += jnp.dot(p, v_ref[...].astype(BF16), preferred_element_type=F32)

        @pl.when(j == i)
        def _():
            s = lax.dot_general(q_ref[...].astype(BF16), k_ref[...].astype(BF16), _DIMS["nt"], preferred_element_type=F32)
            p = (s * _ret_decay(lg_ref[h], i, j, blk)).astype(BF16)
            o_ref[...] = acc[...] + jnp.dot(p, v_ref[...].astype(BF16), preferred_element_type=F32)

    return _pair_call(
        body, (RET_HEADS,), nb, False,
        [pl.BlockSpec(memory_space=pltpu.SMEM),
         pl.BlockSpec((blk, RET_DK), lambda h, i, j: (i, h)),
         pl.BlockSpec((blk, RET_DK), lambda h, i, j: (j, h)),
         pl.BlockSpec((blk, RET_DV), lambda h, i, j: (j, v_off + h))],
        pl.BlockSpec((blk, RET_DV), lambda h, i, j: (i, h)),
        jax.ShapeDtypeStruct((t, RET_HEADS * RET_DV), F32), [pltpu.VMEM((blk, RET_DV), F32)],
        "ret_fwd", (_ret_log_gamma(), qr, kr, proj), gather)


def _ret_bwd(qr, kr, proj, dr, t, exchange=()):
    blk = _ret_block(t)
    nb = t // blk
    v_off = (2 * RET_HEADS * RET_DK) // RET_DV

    def dq_body(i, j, lg_ref, q_ref, k_ref, v_ref, do_ref, dq_ref, acc):
        h = pl.program_id(0)

        @pl.when(j == 0)
        def _():
            acc[...] = jnp.zeros_like(acc)

        @pl.when(j < i)
        def _():
            a, b = _ret_row_decays(lg_ref[h], i, j, blk)
            ds = lax.dot_general(do_ref[...].astype(BF16), v_ref[...].astype(BF16), _DIMS["nt"], preferred_element_type=F32)
            acc[...] += a * jnp.dot(ds.astype(BF16), (k_ref[...] * b).astype(BF16), preferred_element_type=F32)

        @pl.when(j == i)
        def _():
            ds = lax.dot_general(do_ref[...].astype(BF16), v_ref[...].astype(BF16), _DIMS["nt"], preferred_element_type=F32)
            dsm = (ds * _ret_decay(lg_ref[h], i, j, blk)).astype(BF16)
            dq_ref[...] = acc[...] + jnp.dot(dsm, k_ref[...].astype(BF16), preferred_element_type=F32)

    dq = _pair_call(
        dq_body, (RET_HEADS,), nb, False,
        [pl.BlockSpec(memory_space=pltpu.SMEM),
         pl.BlockSpec((blk, RET_DK), lambda h, i, j: (i, h)),
         pl.BlockSpec((blk, RET_DK), lambda h, i, j: (j, h)),
         pl.BlockSpec((blk, RET_DV), lambda h, i, j: (j, v_off + h)),
         pl.BlockSpec((blk, RET_DV), lambda h, i, j: (i, h))],
        pl.BlockSpec((blk, RET_DK), lambda h, i, j: (i, h)),
        jax.ShapeDtypeStruct((t, RET_HEADS * RET_DK), F32), [pltpu.VMEM((blk, RET_DK), F32)],
        "ret_bwd_dq", (_ret_log_gamma(), qr, kr, proj, dr))

    def dkv_body(i, j, lg_ref, q_ref, k_ref, v_ref, do_ref, dk_ref, dv_ref, acc_k, acc_v):
        h = pl.program_id(0)

        @pl.when(i == j)
        def _():
            acc_k[...] = jnp.zeros_like(acc_k)
            acc_v[...] = jnp.zeros_like(acc_v)

        @pl.when(i > j)
        def _():
            a, b = _ret_row_decays(lg_ref[h], i, j, blk)
            qa, kb = (q_ref[...] * a).astype(BF16), (k_ref[...] * b).astype(BF16)
            do = do_ref[...].astype(BF16)
            p = lax.dot_general(qa, kb, _DIMS["nt"], preferred_element_type=F32).astype(BF16)
            acc_v[...] += lax.dot_general(p, do, _DIMS["tn"], preferred_element_type=F32)
            ds = lax.dot_general(do, v_ref[...].astype(BF16), _DIMS["nt"], preferred_element_type=F32).astype(BF16)
            acc_k[...] += b * lax.dot_general(ds, qa, _DIMS["tn"], preferred_element_type=F32)

        @pl.when(i == j)
        def _():
            q = q_ref[...].astype(BF16)
            do = do_ref[...].astype(BF16)
            decay = _ret_decay(lg_ref[h], i, j, blk)
            s = lax.dot_general(q, k_ref[...].astype(BF16), _DIMS["nt"], preferred_element_type=F32)
            acc_v[...] += lax.dot_general((s * decay).astype(BF16), do, _DIMS["tn"], preferred_element_type=F32)
            ds = lax.dot_general(do, v_ref[...].astype(BF16), _DIMS["nt"], preferred_element_type=F32)
            acc_k[...] += lax.dot_general((ds * decay).astype(BF16), q, _DIMS["tn"], preferred_element_type=F32)

        @pl.when(i == nb - 1)
        def _():
            dk_ref[...] = acc_k[...]
            dv_ref[...] = acc_v[...].astype(BF16)

    dk, dv, *landed = _pair_call(
        dkv_body, (RET_HEADS,), nb, True,
        [pl.BlockSpec(memory_space=pltpu.SMEM),
         pl.BlockSpec((blk, RET_DK), lambda h, j, i: (i, h)),
         pl.BlockSpec((blk, RET_DK), lambda h, j, i: (j, h)),
         pl.BlockSpec((blk, RET_DV), lambda h, j, i: (j, v_off + h)),
         pl.BlockSpec((blk, RET_DV), lambda h, j, i: (i, h))],
        [pl.BlockSpec((blk, RET_DK), lambda h, j, i: (j, h)), pl.BlockSpec((blk, RET_DV), lambda h, j, i: (j, h))],
        [jax.ShapeDtypeStruct((t, RET_HEADS * RET_DK), F32), jax.ShapeDtypeStruct((t, RET_HEADS * RET_DV), BF16)],
        [pltpu.VMEM((blk, RET_DK), F32), pltpu.VMEM((blk, RET_DV), F32)],
        "ret_bwd_dkv", (_ret_log_gamma(), qr, kr, proj, dr), exchange=exchange)
    return dq, dk, dv, landed


def _cumsum(x, reverse, name):
    t = x.shape[0]
    blk = _seq_block(t)
    nb = t // blk

    def body(x_ref, o_ref, carry):
        @pl.when(pl.program_id(0) == 0)
        def _():
            carry[...] = jnp.zeros_like(carry)

        r = lax.broadcasted_iota(jnp.int32, (blk, blk), 0)
        c = lax.broadcasted_iota(jnp.int32, (blk, blk), 1)
        tri = ((r <= c) if reverse else (r >= c)).astype(F32)
        o_ref[...] = jnp.dot(tri, x_ref[...], precision=HIGHEST, preferred_element_type=F32) + carry[...]
        carry[...] = o_ref[0:1, :] if reverse else o_ref[blk - 1:blk, :]

    idx = (lambda i: (nb - 1 - i, 0)) if reverse else (lambda i: (i, 0))
    return pl.pallas_call(
        body, grid=(nb,), in_specs=[pl.BlockSpec((blk, LANES), idx)], out_specs=pl.BlockSpec((blk, LANES), idx),
        out_shape=jax.ShapeDtypeStruct((t, LANES), F32), scratch_shapes=[pltpu.VMEM((1, LANES), F32)],
        compiler_params=pltpu.CompilerParams(dimension_semantics=("arbitrary",), vmem_limit_bytes=VMEM_LIMIT),
        name=name)(x)


def _ssd_decay(a_ref, at_ref, hh, mask):
    return jnp.exp(jnp.where(mask, a_ref[:, hh:hh + 1] - at_ref[hh:hh + 1, :], -jnp.inf))


def _head_lanes(s):
    lane = lax.broadcasted_iota(jnp.int32, (s.shape[0], LANES), 1)
    return jnp.concatenate([jnp.where(lane < SSM_P, s[:, 2 * p:2 * p + 1], s[:, 2 * p + 1:2 * p + 2])
                            for p in range(SSM_HEADS // 2)], axis=1)


_B_OFF, _C_OFF = 1024 // SSM_N, 1024 // SSM_N + SSM_GROUPS


def _ssd_fwd(xbc_act, xdt, a_cum, a_cum_t, t, gather=()):
    blk = _seq_block(t)
    nb = t // blk

    def body(i, j, c0, c1, b0, b1, x_ref, ai_ref, aj_ref, at_ref, o_ref, acc):
        @pl.when(j == 0)
        def _():
            acc[...] = jnp.zeros_like(acc)

        @pl.when(j < i)
        def _():
            xv = (x_ref[...] * _head_lanes(jnp.exp(ai_ref[0:1, :] - aj_ref[...]))).astype(BF16)
            for g, (c_ref, b_ref) in enumerate(((c0, b0), (c1, b1))):
                cb = lax.dot_general(c_ref[...].astype(BF16), b_ref[...].astype(BF16), _DIMS["nt"], preferred_element_type=F32)
                cols = slice(g * half, (g + 1) * half)
                acc[:, cols] += jnp.dot(cb.astype(BF16), xv[:, cols], preferred_element_type=F32)

        @pl.when(j == i)
        def _():
            o_ref[...] = acc[...] * _head_lanes(jnp.exp(ai_ref[...] - ai_ref[0:1, :]))
            mask = _causal_diff(i, j, blk) >= 0
            for g, (c_ref, b_ref) in enumerate(((c0, b0), (c1, b1))):
                cb = lax.dot_general(c_ref[...].astype(BF16), b_ref[...].astype(BF16), _DIMS["nt"], preferred_element_type=F32)
                for h in range(SSM_HG):
                    hh = g * SSM_HG + h
                    cols = slice(hh * SSM_P, (hh + 1) * SSM_P)
                    m = (cb * _ssd_decay(ai_ref, at_ref, hh, mask)).astype(BF16)
                    o_ref[:, cols] += jnp.dot(m, x_ref[:, cols].astype(BF16), preferred_element_type=F32)

    half = SSM_HG * SSM_P
    row_i = lambda off: pl.BlockSpec((blk, SSM_N), lambda i, j, off=off: (i, off))
    row_j = lambda off: pl.BlockSpec((blk, SSM_N), lambda i, j, off=off: (jnp.minimum(j, i), off))
    return _pair_call(
        body, (), nb, False,
        [row_i(_C_OFF), row_i(_C_OFF + 1), row_j(_B_OFF), row_j(_B_OFF + 1),
         pl.BlockSpec((blk, SSM_HEADS * SSM_P), lambda i, j: (j, 0)),
         pl.BlockSpec((blk, LANES), lambda i, j: (i, 0)),
         pl.BlockSpec((blk, LANES), lambda i, j: (j, 0)),
         pl.BlockSpec((SSM_HEADS, blk), lambda i, j: (0, j))],
        pl.BlockSpec((blk, SSM_HEADS * SSM_P), lambda i, j: (i, 0)),
        jax.ShapeDtypeStruct((t, SSM_HEADS * SSM_P), F32), [pltpu.VMEM((blk, SSM_HEADS * SSM_P), F32)],
        "ssd_fwd", (xbc_act, xbc_act, xbc_act, xbc_act, xdt, a_cum, a_cum, a_cum_t), gather)


def _ssd_bwd(xbc_act, xdt, a_cum, a_cum_t, dy, expand, t, exchange=()):
    blk = _seq_block(t)
    nb = t // blk
    width = SSM_HEADS * SSM_P

    half = SSM_HG * SSM_P

    def head_sums(prod, e_ref):
        return lax.dot_general(prod, e_ref[...], _DIMS["nt"], precision=HIGHEST, preferred_element_type=F32)

    def q_body(i, j, c0, c1, b0, b1, x_ref, ai_ref, aj_ref, at_ref, dy_ref, e_ref, dc_ref, da_ref, acc_c, acc_a, acc_p, dyu):
        @pl.when(j == 0)
        def _():
            acc_c[...] = jnp.zeros_like(acc_c)
            acc_a[...] = jnp.zeros_like(acc_a)
            acc_p[...] = jnp.zeros_like(acc_p)
            dyu[...] = (dy_ref[...].astype(F32) * _head_lanes(jnp.exp(ai_ref[...] - ai_ref[0:1, :]))).astype(BF16)

        @pl.when(j < i)
        def _():
            xv = (x_ref[...] * _head_lanes(jnp.exp(ai_ref[0:1, :] - aj_ref[...]))).astype(BF16)
            for g, (c_ref, b_ref) in enumerate(((c0, b0), (c1, b1))):
                cols = slice(g * half, (g + 1) * half)
                bj = b_ref[...].astype(BF16)
                cb = lax.dot_general(c_ref[...].astype(BF16), bj, _DIMS["nt"], preferred_element_type=F32).astype(BF16)
                dcb = lax.dot_general(dyu[:, cols], xv[:, cols], _DIMS["nt"], preferred_element_type=F32)
                acc_c[:, g * SSM_N:(g + 1) * SSM_N] += jnp.dot(dcb.astype(BF16), bj, preferred_element_type=F32)
                acc_p[:, cols] += dyu[:, cols].astype(F32) * jnp.dot(cb, xv[:, cols], preferred_element_type=F32)

        @pl.when(j == i)
        def _():
            mask = _causal_diff(i, j, blk) >= 0
            for g, (c_ref, b_ref) in enumerate(((c0, b0), (c1, b1))):
                bj = b_ref[...].astype(BF16)
                cb = lax.dot_general(c_ref[...].astype(BF16), bj, _DIMS["nt"], preferred_element_type=F32)
                dcb = jnp.zeros((blk, blk), F32)
                for h in range(SSM_HG):
                    hh = g * SSM_HG + h
                    cols = slice(hh * SSM_P, (hh + 1) * SSM_P)
                    dm = lax.dot_general(dy_ref[:, cols].astype(BF16), x_ref[:, cols].astype(BF16), _DIMS["nt"],
                                         preferred_element_type=F32) * _ssd_decay(ai_ref, at_ref, hh, mask)
                    dcb = dcb + dm
                    acc_a[:, hh:hh + 1] += jnp.sum(dm * cb, axis=1, keepdims=True)
                acc_c[:, g * SSM_N:(g + 1) * SSM_N] += jnp.dot(dcb.astype(BF16), bj, preferred_element_type=F32)
            dc_ref[...] = acc_c[...]
            da_ref[...] = acc_a[...] + head_sums(acc_p[...], e_ref)

    row_i = lambda off: pl.BlockSpec((blk, SSM_N), lambda i, j, off=off: (i, off))
    row_j = lambda off: pl.BlockSpec((blk, SSM_N), lambda i, j, off=off: (jnp.minimum(j, i), off))
    e_spec = pl.BlockSpec((LANES, width), lambda i, j: (0, 0))
    dc, da_q = _pair_call(
        q_body, (), nb, False,
        [row_i(_C_OFF), row_i(_C_OFF + 1), row_j(_B_OFF), row_j(_B_OFF + 1),
         pl.BlockSpec((blk, width), lambda i, j: (j, 0)),
         pl.BlockSpec((blk, LANES), lambda i, j: (i, 0)),
         pl.BlockSpec((blk, LANES), lambda i, j: (j, 0)),
         pl.BlockSpec((SSM_HEADS, blk), lambda i, j: (0, j)),
         pl.BlockSpec((blk, width), lambda i, j: (i, 0)), e_spec],
        [pl.BlockSpec((blk, SSM_GROUPS * SSM_N), lambda i, j: (i, 0)), pl.BlockSpec((blk, LANES), lambda i, j: (i, 0))],
        [jax.ShapeDtypeStruct((t, SSM_GROUPS * SSM_N), F32), jax.ShapeDtypeStruct((t, LANES), F32)],
        [pltpu.VMEM((blk, SSM_GROUPS * SSM_N), F32), pltpu.VMEM((blk, LANES), F32), pltpu.VMEM((blk, width), F32),
         pltpu.VMEM((blk, width), BF16)],
        "ssd_bwd_q", (xbc_act, xbc_act, xbc_act, xbc_act, xdt, a_cum, a_cum, a_cum_t, dy, expand))

    def k_body(i, j, c0, c1, b0, b1, x_ref, ai_ref, aj_ref, at_ref, dy_ref, e_ref, db_ref, dx_ref, da_ref, dat_ref, acc_b, acc_x, acc_a,
               acc_p):
        @pl.when(i == j)
        def _():
            acc_b[...] = jnp.zeros_like(acc_b)
            acc_x[...] = jnp.zeros_like(acc_x)
            acc_a[...] = jnp.zeros_like(acc_a)
            acc_p[...] = jnp.zeros_like(acc_p)

        @pl.when(i > j)
        def _():
            v = _head_lanes(jnp.exp(ai_ref[0:1, :] - aj_ref[...]))
            dyu_all = (dy_ref[...].astype(F32) * _head_lanes(jnp.exp(ai_ref[...] - ai_ref[0:1, :]))).astype(BF16)
            xv = (x_ref[...] * v).astype(BF16)
            for g, (c_ref, b_ref) in enumerate(((c0, b0), (c1, b1))):
                cols = slice(g * half, (g + 1) * half)
                ci = c_ref[...].astype(BF16)
                cb = lax.dot_general(ci, b_ref[...].astype(BF16), _DIMS["nt"], preferred_element_type=F32).astype(BF16)
                dcb = lax.dot_general(dyu_all[:, cols], xv[:, cols], _DIMS["nt"], preferred_element_type=F32)
                acc_b[:, g * SSM_N:(g + 1) * SSM_N] += lax.dot_general(dcb.astype(BF16), ci, _DIMS["tn"], preferred_element_type=F32)
                dxv = lax.dot_general(cb, dyu_all[:, cols], _DIMS["tn"], preferred_element_type=F32)
                acc_x[:, cols] += v[:, cols] * dxv
                acc_p[:, cols] += xv[:, cols].astype(F32) * dxv

        @pl.when(i == j)
        def _():
            mask = _causal_diff(i, j, blk) >= 0
            for g, (c_ref, b_ref) in enumerate(((c0, b0), (c1, b1))):
                ci = c_ref[...].astype(BF16)
                cb = lax.dot_general(ci, b_ref[...].astype(BF16), _DIMS["nt"], preferred_element_type=F32)
                dcb = jnp.zeros((blk, blk), F32)
                for h in range(SSM_HG):
                    hh = g * SSM_HG + h
                    cols = slice(hh * SSM_P, (hh + 1) * SSM_P)
                    decay = _ssd_decay(ai_ref, at_ref, hh, mask)
                    dyh = dy_ref[:, cols].astype(BF16)
                    acc_x[:, cols] += lax.dot_general((cb * decay).astype(BF16), dyh, _DIMS["tn"], preferred_element_type=F32)
                    dm = lax.dot_general(dyh, x_ref[:, cols].astype(BF16), _DIMS["nt"], preferred_element_type=F32) * decay
                    dcb = dcb + dm
                    acc_a[hh:hh + 1, :] += jnp.sum(dm * cb, axis=0, keepdims=True)
                acc_b[:, g * SSM_N:(g + 1) * SSM_N] += lax.dot_general(dcb.astype(BF16), ci, _DIMS["tn"], preferred_element_type=F32)

        @pl.when(i == nb - 1)
        def _():
            db_ref[...] = acc_b[...]
            dx_ref[...] = acc_x[...]
            dat_ref[...] = acc_a[...]
            da_ref[...] = head_sums(acc_p[...], e_ref)

    rowk_i = lambda off: pl.BlockSpec((blk, SSM_N), lambda j, i, off=off: (jnp.maximum(i, j), off))
    rowk_j = lambda off: pl.BlockSpec((blk, SSM_N), lambda j, i, off=off: (j, off))
    db, dx, da_k, da_k_t, *landed = _pair_call(
        k_body, (), nb, True,
        [rowk_i(_C_OFF), rowk_i(_C_OFF + 1), rowk_j(_B_OFF), rowk_j(_B_OFF + 1),
         pl.BlockSpec((blk, width), lambda j, i: (j, 0)),
         pl.BlockSpec((blk, LANES), lambda j, i: (i, 0)),
         pl.BlockSpec((blk, LANES), lambda j, i: (j, 0)),
         pl.BlockSpec((SSM_HEADS, blk), lambda j, i: (0, j)),
         pl.BlockSpec((blk, width), lambda j, i: (i, 0)), e_spec],
        [pl.BlockSpec((blk, SSM_GROUPS * SSM_N), lambda j, i: (j, 0)), pl.BlockSpec((blk, width), lambda j, i: (j, 0)),
         pl.BlockSpec((blk, LANES), lambda j, i: (j, 0)), pl.BlockSpec((SSM_HEADS, blk), lambda j, i: (0, j))],
        [jax.ShapeDtypeStruct((t, SSM_GROUPS * SSM_N), F32), jax.ShapeDtypeStruct((t, width), F32),
         jax.ShapeDtypeStruct((t, LANES), F32), jax.ShapeDtypeStruct((SSM_HEADS, t), F32)],
        [pltpu.VMEM((blk, SSM_GROUPS * SSM_N), F32), pltpu.VMEM((blk, width), F32), pltpu.VMEM((SSM_HEADS, blk), F32),
         pltpu.VMEM((blk, width), F32)],
        "ssd_bwd_k", (xbc_act, xbc_act, xbc_act, xbc_act, xdt, a_cum, a_cum, a_cum_t, dy, expand), exchange=exchange)
    return dc, da_q, db, dx, da_k, da_k_t, landed


S5_SUPER = 4
S5_NARROW, S5_WIDE = S5_GROUPS * S5_GROUP // S5_SUPER, S5_GROUPS * S5_STATE // S5_SUPER


def _s5_tiles(t):
    tm = min(1024, t)
    narrow = lambda off: pl.BlockSpec((tm, S5_NARROW), lambda s, i, off=off: (i, s + off))
    wide = pl.BlockSpec((tm, S5_WIDE), lambda s, i: (i, s))
    weight = lambda shape: pl.BlockSpec((1,) + shape, lambda s, i: (s, 0, 0))
    params = pltpu.CompilerParams(dimension_semantics=("arbitrary", "arbitrary"), vmem_limit_bytes=VMEM_LIMIT)
    return tm, narrow, wide, weight, params


def _s5_expand(a, a_off, w, w_is_wide_first, name):
    t = a.shape[0]
    tm, narrow, wide, weight, params = _s5_tiles(t)

    def body(a_ref, w_ref, re_ref, im_ref):
        dims = _DIMS["nt"] if w_is_wide_first else _DIMS["nn"]
        o = lax.dot_general(a_ref[...].astype(BF16), w_ref[0].astype(BF16), dims, preferred_element_type=F32)
        re_ref[...] = o[:, :S5_WIDE]
        im_ref[...] = o[:, S5_WIDE:]

    return pl.pallas_call(
        body, grid=(S5_SUPER, t // tm), in_specs=[narrow(a_off), weight(w.shape[1:])], out_specs=[wide, wide],
        out_shape=[jax.ShapeDtypeStruct((t, S5_COLS), F32)] * 2, compiler_params=params, name=name)(a, w)


def _s5_contract(x_re, x_im, w, w_is_wide_first, name, res=None, out_dtype=F32):
    t = x_re.shape[0]
    tm, narrow, wide, weight, params = _s5_tiles(t)
    has_res = res is not None

    def body(*refs):
        re_ref, im_ref, w_ref, o_ref = refs[0], refs[1], refs[2], refs[-1]
        x = jnp.concatenate([re_ref[...].astype(BF16), im_ref[...].astype(BF16)], axis=1)
        dims = _DIMS["nn"] if w_is_wide_first else _DIMS["nt"]
        o = lax.dot_general(x, w_ref[0].astype(BF16), dims, preferred_element_type=F32)
        o_ref[...] = (o + refs[3][...] if has_res else o).astype(out_dtype)

    return pl.pallas_call(
        body, grid=(S5_SUPER, t // tm), in_specs=[wide, wide, weight(w.shape[1:])] + ([narrow(0)] if has_res else []),
        out_specs=narrow(0), out_shape=jax.ShapeDtypeStruct((t, S5_SUPER * S5_NARROW), out_dtype), compiler_params=params,
        name=name)(*([x_re, x_im, w] + ([res] if has_res else [])))


def _s5_wgrad(a, a_off, x_re, x_im, wide_first, name):
    t = a.shape[0]
    tm, narrow, wide, weight, params = _s5_tiles(t)
    shape = (2 * S5_WIDE, S5_NARROW) if wide_first else (S5_NARROW, 2 * S5_WIDE)

    def body(a_ref, re_ref, im_ref, o_ref):
        @pl.when(pl.program_id(1) == 0)
        def _():
            o_ref[...] = jnp.zeros_like(o_ref)

        x = jnp.concatenate([re_ref[...].astype(BF16), im_ref[...].astype(BF16)], axis=1)
        av = a_ref[...].astype(BF16)
        o_ref[0] += lax.dot_general(x, av, _DIMS["tn"], preferred_element_type=F32) if wide_first else \
            lax.dot_general(av, x, _DIMS["tn"], preferred_element_type=F32)

    return pl.pallas_call(
        body, grid=(S5_SUPER, t // tm), in_specs=[narrow(a_off), wide, wide], out_specs=weight(shape),
        out_shape=jax.ShapeDtypeStruct((S5_SUPER,) + shape, F32), compiler_params=params, name=name)(a, x_re, x_im)


def _s5_scan(b_re, b_im, a_re, a_im, reverse, x_prev=None):
    t = b_re.shape[0]
    cb = 512
    ncb = S5_COLS // cb
    tb = min(1024, t)
    ntb = t // tb
    sub = 8
    shape = (sub, cb)

    def cmul(ar, ai, br, bi):
        return ar * br - ai * bi, ar * bi + ai * br

    def body(*refs):
        if reverse:
            br_ref, bi_ref, ar_ref, ai_ref, xr_ref, xi_ref, or_ref, oi_ref, dar_ref, dai_ref, carry, dacc, states = refs
        else:
            br_ref, bi_ref, ar_ref, ai_ref, or_ref, oi_ref, or_bf_ref, oi_bf_ref, carry = refs

        @pl.when(pl.program_id(1) == 0)
        def _():
            carry[...] = jnp.zeros_like(carry)
            if reverse:
                dacc[...] = jnp.zeros_like(dacc)

        row = lax.broadcasted_iota(jnp.int32, shape, 0)
        a1 = (jnp.broadcast_to(ar_ref[...], shape), jnp.broadcast_to(-ai_ref[...] if reverse else ai_ref[...], shape))
        a2 = cmul(*a1, *a1)
        a4 = cmul(*a2, *a2)
        a8 = cmul(*a4, *a4)
        steps = (sub - row) if reverse else (row + 1)
        pw = (jnp.ones(shape, F32), jnp.zeros(shape, F32))
        for bit, p in ((1, a1), (2, a2), (4, a4), (8, a8)):
            on = (steps & bit) != 0
            pw = cmul(*pw, jnp.where(on, p[0], 1.0), jnp.where(on, p[1], 0.0))
        edge = 0 if reverse else sub - 1

        def shift(v, s):
            if reverse:
                return jnp.where(row < sub - s, pltpu.roll(v, sub - s, 0), 0.0)
            return jnp.where(row >= s, pltpu.roll(v, s, 0), 0.0)

        def tile(n, state):
            r0 = pl.multiple_of((tb // sub - 1 - n if reverse else n) * sub, sub)
            rows = pl.ds(r0, sub)
            xr, xi = br_ref[rows, :], bi_ref[rows, :]
            for s, p in ((1, a1), (2, a2), (4, a4)):
                dr, di = cmul(*p, shift(xr, s), shift(xi, s))
                xr, xi = xr + dr, xi + di
            cr, ci = jnp.broadcast_to(state[0], shape), jnp.broadcast_to(state[1], shape)
            dr, di = cmul(*pw, cr, ci)
            xr, xi = xr + dr, xi + di
            if reverse:
                states[0, rows, :], states[1, rows, :] = xr, xi
                gr = jnp.where(row < sub - 1, pltpu.roll(xr, sub - 1, 0), cr)
                gi = jnp.where(row < sub - 1, pltpu.roll(xi, sub - 1, 0), ci)
                pr, pi = xr_ref[rows, :], xi_ref[rows, :]
                dacc[0] += gr * pr + gi * pi
                dacc[1] += gi * pr - gr * pi
            else:
                or_ref[rows, :], oi_ref[rows, :] = xr, xi
            return (jnp.sum(jnp.where(row == edge, xr, 0.0), axis=0, keepdims=True),
                    jnp.sum(jnp.where(row == edge, xi, 0.0), axis=0, keepdims=True))

        last = lax.fori_loop(0, tb // sub, tile, (carry[0], carry[1]))
        carry[0], carry[1] = last
        if reverse:
            or_ref[...], oi_ref[...] = states[0].astype(BF16), states[1].astype(BF16)
            dar_ref[...] = jnp.sum(dacc[0], axis=0, keepdims=True)
            dai_ref[...] = jnp.sum(dacc[1], axis=0, keepdims=True)
        else:
            or_bf_ref[...], oi_bf_ref[...] = or_ref[...].astype(BF16), oi_ref[...].astype(BF16)

    tblock = (lambda k: ntb - 1 - k) if reverse else (lambda k: k)
    re_spec = pl.BlockSpec((tb, cb), lambda j, k: (tblock(k), j))
    a_spec = pl.BlockSpec((1, cb), lambda j, k: (0, j))
    ins, in_specs = [b_re, b_im, a_re, a_im], [re_spec, re_spec, a_spec, a_spec]
    seq_bf = jax.ShapeDtypeStruct((t, S5_COLS), BF16)
    scratch = [pltpu.VMEM((2, 1, cb), F32)]
    if reverse:
        ins += list(x_prev)
        in_specs += [re_spec, re_spec]
        out_shape = [seq_bf, seq_bf] + [jax.ShapeDtypeStruct((1, S5_COLS), F32)] * 2
        out_specs = [re_spec, re_spec, a_spec, a_spec]
        scratch += [pltpu.VMEM((2, sub, cb), F32), pltpu.VMEM((2, tb, cb), F32)]
    else:
        out_shape = [jax.ShapeDtypeStruct((t, S5_COLS), F32)] * 2 + [seq_bf, seq_bf]
        out_specs = [re_spec] * 4
    return pl.pallas_call(
        body, grid=(ncb, ntb), in_specs=in_specs, out_specs=out_specs, out_shape=out_shape, scratch_shapes=scratch,
        compiler_params=pltpu.CompilerParams(dimension_semantics=("arbitrary", "arbitrary"), vmem_limit_bytes=VMEM_LIMIT),
        name="s5_scan_bwd" if reverse else "s5_scan_fwd")(*ins)


CONF_DIM = 512
CONF_K = 31
CONF_PAD = 32


def _conf_conv(proj_o, w, b, dc1=None):
    t = proj_o.shape[0]
    cb = LANES
    ncb = CONF_DIM // cb
    chunk = min(512, t)
    chunks = range(0, t, chunk)
    bwd = dc1 is not None

    def body(*refs):
        if bwd:
            ca_ref, cg_ref, w_ref, b_ref, dy_ref, dca_ref, dcg_ref, dw_ref, db_ref, xs, dys = refs
        else:
            ca_ref, cg_ref, w_ref, b_ref, o_ref, xs = refs
        xs[pl.ds(0, CONF_PAD), :] = jnp.zeros((CONF_PAD, cb), F32)
        for t0 in chunks:
            rows = pl.ds(t0, chunk)
            xs[pl.ds(CONF_PAD + t0, chunk), :] = ca_ref[rows, :] * jax.nn.sigmoid(cg_ref[rows, :])
        if not bwd:
            for t0 in chunks:
                acc = jnp.broadcast_to(b_ref[...], (chunk, cb))
                for k in range(CONF_K):
                    acc = acc + w_ref[k] * xs[pl.ds(CONF_PAD + t0 - (CONF_K - 1 - k), chunk), :]
                o_ref[pl.ds(t0, chunk), :] = acc
            return
        dys[pl.ds(t, CONF_PAD), :] = jnp.zeros((CONF_PAD, cb), F32)
        db = jnp.zeros((1, cb), F32)
        for t0 in chunks:
            dys[pl.ds(t0, chunk), :] = dy_ref[pl.ds(t0, chunk), :]
            db = db + jnp.sum(dy_ref[pl.ds(t0, chunk), :], axis=0, keepdims=True)
        db_ref[...] = db
        for t0 in chunks:
            rows = pl.ds(t0, chunk)
            acc = jnp.zeros((chunk, cb), F32)
            for k in range(CONF_K):
                acc = acc + w_ref[k] * dys[pl.ds(t0 + (CONF_K - 1 - k), chunk), :]
            sig = jax.nn.sigmoid(cg_ref[rows, :])
            dca_ref[rows, :] = (acc * sig).astype(BF16)
            dcg_ref[rows, :] = (acc * ca_ref[rows, :] * sig * (1.0 - sig)).astype(BF16)
        for k in range(CONF_K):
            dwk = jnp.zeros((1, cb), F32)
            for t0 in chunks:
                window = xs[pl.ds(CONF_PAD + t0 - (CONF_K - 1 - k), chunk), :]
                dwk = dwk + jnp.sum(dy_ref[pl.ds(t0, chunk), :] * window, axis=0, keepdims=True)
            dw_ref[k] = dwk

    col = lambda off: pl.BlockSpec((t, cb), lambda j, off=off: (0, j + off))
    w_spec = pl.BlockSpec((CONF_K, 1, cb), lambda j: (0, 0, j))
    b_spec = pl.BlockSpec((1, cb), lambda j: (0, j))
    seq = jax.ShapeDtypeStruct((t, CONF_DIM), F32)
    ins, in_specs = [proj_o, proj_o, w, b], [col(0), col(ncb), w_spec, b_spec]
    scratch = [pltpu.VMEM((CONF_PAD + t, cb), F32)]
    if bwd:
        ins, in_specs = ins + [dc1], in_specs + [col(0)]
        seq_bf = jax.ShapeDtypeStruct((t, CONF_DIM), BF16)
        out_shape, out_specs = [seq_bf, seq_bf, jax.ShapeDtypeStruct(w.shape, F32), jax.ShapeDtypeStruct(b.shape, F32)], [col(0), col(0), w_spec, b_spec]
        scratch = scratch + [pltpu.VMEM((t + CONF_PAD, cb), F32)]
    else:
        out_shape, out_specs = seq, col(0)
    return pl.pallas_call(
        body, grid=(ncb,), in_specs=in_specs, out_specs=out_specs, out_shape=out_shape, scratch_shapes=scratch,
        compiler_params=pltpu.CompilerParams(dimension_semantics=("arbitrary",), vmem_limit_bytes=VMEM_LIMIT),
        name="conf_conv_bwd" if bwd else "conf_conv")(*ins)


N_CHIPS = 4
_HBM = pl.BlockSpec(memory_space=pl.ANY)
_MESH_ID = pl.DeviceIdType.MESH


def _comm_call(body, srcs, out_shapes, n_sems, name):
    n = len(srcs)
    return pl.pallas_call(
        body, out_shape=out_shapes, in_specs=[_HBM] * n, out_specs=[_HBM] * n,
        scratch_shapes=[pltpu.SemaphoreType.DMA((n_sems,)), pltpu.SemaphoreType.DMA((n_sems,)), pltpu.SemaphoreType.DMA((n,))],
        compiler_params=pltpu.CompilerParams(has_side_effects=True), name=name)(*srcs)


GATHER_SEMS = N_DEV - 1


def _gather_phases(src_refs, out_refs, send_sems, recv_sems, local_sems):
    n = len(src_refs)
    x, y, c = lax.axis_index("x"), lax.axis_index("y"), lax.axis_index("c")
    me, sibling = (x, y, c), (x, y, 1 - c)
    chips = [(1 - x, y), (x, 1 - y), (1 - x, 1 - y)]

    def copy(i, k, block, to, from_src=False):
        rows = out_refs[i].at[4 * block[0] + 2 * block[1] + block[2]]
        return pltpu.make_async_remote_copy(
            src_ref=src_refs[i] if from_src else rows, dst_ref=rows, send_sem=send_sems.at[GATHER_SEMS * i + k],
            recv_sem=recv_sems.at[GATHER_SEMS * i + k], device_id=to, device_id_type=_MESH_ID)

    def local(i):
        return pltpu.make_async_copy(src_refs[i], out_refs[i].at[4 * x + 2 * y + c], local_sems.at[i])

    def first(i):
        return [copy(i, 0, me, sibling, True)] + [copy(i, 1 + j, me, (*chip, c), True) for j, chip in enumerate(chips)]

    def passed(i, j):
        return copy(i, 4 + j, (*chips[j], c), sibling)

    def start():
        for i in range(n):
            local(i).start()
            for cp in first(i):
                cp.start()

    def forward():
        for j, chip in enumerate(chips):
            for i in range(n):
                copy(i, 1 + j, (*chip, c), me).wait_recv()
                passed(i, j).start()

    def finish():
        for i in range(n):
            copy(i, 0, sibling, me).wait_recv()
            for j, chip in enumerate(chips):
                copy(i, 4 + j, (*chip, 1 - c), me).wait_recv()
        for i in range(n):
            for cp in first(i) + [passed(i, j) for j in range(len(chips))]:
                cp.wait_send()
            local(i).wait()

    return start, forward, finish


def _gather(srcs, name):
    n = len(srcs)

    def body(*refs):
        for phase in _gather_phases(refs[:n], refs[n:2 * n], *refs[2 * n:]):
            phase()

    return _comm_call(body, srcs, [jax.ShapeDtypeStruct((N_DEV,) + s.shape, s.dtype) for s in srcs], GATHER_SEMS * n, name)


def _swap_sibling(srcs, name):
    n = len(srcs)

    def body(*refs):
        src_refs, out_refs = refs[:n], refs[n:2 * n]
        send_sems, recv_sems, _ = refs[2 * n:]
        sibling = (lax.axis_index("x"), lax.axis_index("y"), 1 - lax.axis_index("c"))
        copies = [pltpu.make_async_remote_copy(src_ref=src_refs[i], dst_ref=out_refs[i], send_sem=send_sems.at[i],
                                               recv_sem=recv_sems.at[i], device_id=sibling, device_id_type=_MESH_ID) for i in range(n)]
        for cp in copies:
            cp.start()
        for cp in copies:
            cp.wait_recv()
        for cp in copies:
            cp.wait_send()

    return _comm_call(body, srcs, [jax.ShapeDtypeStruct(s.shape, s.dtype) for s in srcs], n, name)


EXCHANGE_SEMS = N_CHIPS - 1


def _exchange_phases(src_refs, out_refs, send_sems, recv_sems, local_sems):
    n = len(src_refs)
    x, y, c = lax.axis_index("x"), lax.axis_index("y"), lax.axis_index("c")
    mine = 2 * x + y

    def local(i):
        return pltpu.make_async_copy(src_refs[i].at[mine], out_refs[i].at[mine], local_sems.at[i])

    def copies(i, landing):
        out = []
        for k in range(1, N_CHIPS):
            px, py = x ^ (k >> 1), y ^ (k & 1)
            peer = 2 * px + py
            sem = EXCHANGE_SEMS * i + k - 1
            out.append(pltpu.make_async_remote_copy(
                src_ref=src_refs[i].at[peer], dst_ref=out_refs[i].at[peer if landing else mine], send_sem=send_sems.at[sem],
                recv_sem=recv_sems.at[sem], device_id=(px, py, c), device_id_type=_MESH_ID))
        return out

    def start():
        for i in range(n):
            local(i).start()
            for send in copies(i, False):
                send.start()

    def finish():
        for i in range(n):
            for recv in copies(i, True):
                recv.wait_recv()
        for i in range(n):
            for send in copies(i, False):
                send.wait_send()
            local(i).wait()

    return start, (lambda: None), finish


def _exchange_chips(srcs, name):
    n = len(srcs)

    def body(*refs):
        for phase in _exchange_phases(refs[:n], refs[n:2 * n], *refs[2 * n:]):
            phase()

    return _comm_call(body, srcs, [jax.ShapeDtypeStruct(s.shape, s.dtype) for s in srcs], EXCHANGE_SEMS * n, name)


def _add(a, b, name):
    k, rows, cols = a.shape
    tr = _row_tile(rows)

    def body(a_ref, b_ref, o_ref):
        o_ref[...] = (a_ref[...].astype(F32) + b_ref[...].astype(F32)).astype(o_ref.dtype)

    spec = pl.BlockSpec((k, tr, cols), lambda i: (0, i, 0))
    return pl.pallas_call(
        body, grid=(rows // tr,), in_specs=[spec, spec], out_specs=spec, out_shape=jax.ShapeDtypeStruct(a.shape, a.dtype),
        compiler_params=pltpu.CompilerParams(dimension_semantics=("parallel",), vmem_limit_bytes=VMEM_LIMIT), name=name)(a, b)


def _row_tile(r, pref=256):
    if r <= pref:
        return r
    t = pref // 16 * 16
    while r % t:
        t -= 16
    return t


def _join_cols(g, width, name):
    _, rows, ws = g.shape
    tr = _row_tile(rows)
    tail = width - N_DEV * ws

    def body(g_ref, o_ref):
        for d in range(N_DEV):
            o_ref[:, pl.ds(d * ws, ws)] = g_ref[d]
        if tail:
            o_ref[:, pl.ds(N_DEV * ws, tail)] = jnp.zeros((tr, tail), g.dtype)

    return pl.pallas_call(
        body, grid=(rows // tr,), in_specs=[pl.BlockSpec((N_DEV, tr, ws), lambda i: (0, i, 0))],
        out_specs=pl.BlockSpec((tr, width), lambda i: (i, 0)), out_shape=jax.ShapeDtypeStruct((rows, width), g.dtype),
        compiler_params=pltpu.CompilerParams(dimension_semantics=("parallel",), vmem_limit_bytes=VMEM_LIMIT), name=name)(g)


def _split_cols(full, ws, dtype, name):
    rows, width = full.shape
    tr = _row_tile(rows)

    def body(x_ref, o_ref):
        for d in range(N_DEV):
            o_ref[d] = x_ref[:, pl.ds(d * ws, ws)].astype(dtype)

    return pl.pallas_call(
        body, grid=(rows // tr,), in_specs=[pl.BlockSpec((tr, width), lambda i: (i, 0))],
        out_specs=pl.BlockSpec((N_DEV, tr, ws), lambda i: (0, i, 0)), out_shape=jax.ShapeDtypeStruct((N_DEV, rows, ws), dtype),
        compiler_params=pltpu.CompilerParams(dimension_semantics=("parallel",), vmem_limit_bytes=VMEM_LIMIT), name=name)(full)


def _adamw(parts, w, m, v, name):
    r, c = w.shape
    n_parts = parts.shape[0]
    tr = _row_tile(r)

    def body(p_ref, w_ref, m_ref, v_ref, g_ref, d_ref, nm_ref, nv_ref):
        g = p_ref[0].astype(F32)
        for s in range(1, n_parts):
            g = g + p_ref[s].astype(F32)
        nm = ADAM_B1 * m_ref[...] + (1.0 - ADAM_B1) * g
        nv = ADAM_B2 * v_ref[...] + (1.0 - ADAM_B2) * (g * g)
        m_hat = nm / (1.0 - ADAM_B1 ** ADAM_STEP)
        v_hat = nv / (1.0 - ADAM_B2 ** ADAM_STEP)
        g_ref[...] = g
        nm_ref[...] = nm
        nv_ref[...] = nv
        d_ref[...] = -ADAM_LR * (m_hat / (jnp.sqrt(v_hat) + ADAM_EPS) + ADAM_WD * w_ref[...])

    spec = pl.BlockSpec((tr, c), lambda i: (i, 0))
    return pl.pallas_call(
        body, grid=(r // tr,), in_specs=[pl.BlockSpec((n_parts, tr, c), lambda i: (0, i, 0)), spec, spec, spec],
        out_specs=[spec] * 4, out_shape=[jax.ShapeDtypeStruct((r, c), F32)] * 4,
        compiler_params=pltpu.CompilerParams(dimension_semantics=("parallel",), vmem_limit_bytes=VMEM_LIMIT),
        name=name)(parts, w, m, v)


def _pack_rows(n_elems, mult):
    rows = -(-n_elems // PACK_COLS)
    return -(-rows // mult) * mult


def _pack(arrays, dtype, mult, lead=()):
    flat = jnp.concatenate([a.astype(dtype).reshape(lead + (-1,)) for a in arrays], axis=-1)
    rows = _pack_rows(flat.shape[-1], mult)
    flat = jnp.pad(flat, [(0, 0)] * len(lead) + [(0, rows * PACK_COLS - flat.shape[-1])])
    return flat.reshape(lead + (rows, PACK_COLS))


def _unpack(buf, shapes, lead=()):
    flat = buf.reshape(lead + (-1,))
    out, off = [], 0
    for s in shapes:
        n = math.prod(s)
        out.append(flat[..., off:off + n].reshape(lead + tuple(s)))
        off += n
    return out


def _join_shards(piece, axis):
    moved = jnp.moveaxis(piece, 0, axis)
    shape = moved.shape
    return moved.reshape(shape[:axis] + (shape[axis] * shape[axis + 1],) + shape[axis + 2:])


def _split_shards(full, axis):
    shape = full.shape
    return jnp.moveaxis(full.reshape(shape[:axis] + (N_DEV, shape[axis] // N_DEV) + shape[axis + 1:]), axis, 0)


def _block_diag(blocks):
    g, r, c = blocks.shape
    k = g // S5_SUPER
    eye = jnp.eye(k, dtype=blocks.dtype)
    return (blocks.reshape(S5_SUPER, k, r, 1, c) * eye[None, :, None, :, None]).reshape(S5_SUPER, k * r, k * c)


def _diag_blocks(mat, r, c):
    k = mat.shape[1] // r
    eye = jnp.eye(k, dtype=mat.dtype)
    return jnp.sum(mat.reshape(S5_SUPER, k, r, k, c) * eye[None, :, None, :, None], axis=3).reshape(S5_SUPER * k, r, c)


def _pad_lanes(a):
    a = a.reshape(1, -1)
    return jnp.pad(a, ((0, 0), (0, LANES - a.shape[1])))


def _head_expand():
    h = jnp.arange(LANES)[:, None]
    ch = jnp.arange(SSM_HEADS * SSM_P)[None, :] // SSM_P
    return (h == ch).astype(F32)


def _rotary_tables(t):
    inv = 10000.0 ** (-jnp.arange(0, RET_DK, 2, dtype=F32) / RET_DK)
    ang = jnp.arange(t).astype(F32)[:, None] * inv[None, :]
    cos, sin = jnp.cos(ang), jnp.sin(ang)
    return jnp.concatenate([cos, cos], axis=1), jnp.concatenate([-sin, sin], axis=1)


def _rms_fwd(x, g, name):
    return _rows(_f_rms, x.shape[0], [(x, D_MODEL, 0)], [g], [D_MODEL], [], name, bf16_outs=(0,))[0]


def _rms_bwd(x, g, dh, dres, name):
    def fn(x_, dh_, dres_, g_):
        _, vjp = jax.vjp(lambda a, b: _rms(a, b), x_, g_)
        dx, dg = vjp(dh_)
        return dx + dres_, dx + dres_, dg
    return _rows(fn, x.shape[0], [(x, D_MODEL, 0), (dh, D_MODEL, 0), (dres, D_MODEL, 0)], [g], [D_MODEL, D_MODEL], [(1, D_MODEL)],
                 name, bf16_outs=(1,))


def _ffn_fwd(x, norm_g, w_up, dw_w, dw_b, w_down, tag):
    t = x.shape[0]
    nbk = D_FF // LANES
    h = _rms_fwd(x, norm_g, "ffn_norm_" + tag)
    up = _mm(h, w_up, "nn", "ffn_up_" + tag)
    mid = _cols(_f_ffnmid, nbk, [(up, 0), (up, nbk), (dw_w, 0), (dw_w, nbk), (dw_b, 0), (dw_b, nbk)], [(t,)], "ffn_mid_" + tag,
                bf16_outs=(0,))[0]
    out = _mm(mid, w_down, "nn", "ffn_down_" + tag, res=x)
    return out, (h, up, mid)


def _ffn_bwd(x, norm_g, w_up, dw_w, dw_b, w_down, saved, dout, dout_bf, tag):
    t = x.shape[0]
    nbk = D_FF // LANES
    h, up, mid = saved
    d_w_down = _mm(mid, dout_bf, "tn", "ffn_down_dw_" + tag, out_dtype=BF16)
    dmid = _mm(dout_bf, w_down, "nt", "ffn_down_dx_" + tag)
    dgin, duin, dwg, dwu, dbg, dbu = _cols(
        _grad_fn(_f_ffnmid, 6, 6), nbk,
        [(up, 0), (up, nbk), (dw_w, 0), (dw_w, nbk), (dw_b, 0), (dw_b, nbk), (dmid, 0)],
        [(t,), (t,), (3, 1), (3, 1), (1,), (1,)], "ffn_mid_bwd_" + tag, bf16_outs=(0, 1))
    dup = jnp.concatenate([dgin, duin], axis=1)
    d_w_up = _mm(h, dup, "tn", "ffn_up_dw_" + tag)
    dh = _mm(dup, w_up, "nt", "ffn_up_dx_" + tag)
    dx, dx_bf, dnorm = _rms_bwd(x, norm_g, dh, dout, "ffn_norm_bwd_" + tag)
    return dx, dx_bf, dict(norm=dnorm, w_up=d_w_up, dw_w=jnp.concatenate([dwg, dwu], axis=2)[:, 0],
                           dw_b=jnp.concatenate([dbg, dbu], axis=1), w_down=d_w_down)


def _local_step(x, tgt, w, late=None):
    t = x.shape[0]
    grads = {}
    expand = _head_expand()
    cosf, sins = _rotary_tables(t)
    mix_g = [w['mix_norm'][i:i + 1] for i in range(2)]
    ffn_g = [w['ffn_norm'][i:i + 1] for i in range(2)]
    ffn_dw_w = [w['ffn_dw_w'][i][:, None, :] for i in range(2)]
    ffn_dw_b = [w['ffn_dw_b'][i:i + 1] for i in range(2)]

    w_e = w['e_w_in']
    conv_w = w['e_conv_w'][0][:, None, :]
    conv_b = w['e_conv_b']
    dt_bias, a_log, d_skip = _pad_lanes(w['e_dt_bias']), _pad_lanes(w['e_a_log']), _pad_lanes(w['e_d'])
    h0 = _rms_fwd(x, mix_g[0], "mix_norm_0")
    proj = _mm(h0, w_e, "nn", "e_in")
    qr, kr = _rows(_f_retpre, t, [(proj, 512, 0), (proj, 512, 1), (cosf, LANES, 0), (sins, LANES, 0)], [], [512, 512], [], "ret_pre")
    if late:
        r, *landed = _ret_fwd(qr, kr, proj, t, late['ret_fwd'][0])
        late['ret_fwd'][1](w, landed)
    else:
        r = _ret_fwd(qr, kr, proj, t)
    y_ret = _rows(_f_retpost, t, [(r, 1024, 0), (proj, 1024, 2)], [], [1024], [], "ret_post", bf16_outs=(0,))[0]
    xbc_act = _cols(_f_ssdconv, 12, [(proj, 32), (conv_w, 0), (conv_b, 0)], [(t,)], "ssd_conv")[0]
    xdt, da = _rows(_f_ssdpre, t, [(xbc_act, 1024, 0), (proj, LANES, 44)], [dt_bias, a_log, expand], [1024, LANES], [], "ssd_pre")
    a_cum = _cumsum(da, False, "ssd_cumsum")
    a_cum_t = a_cum[:, :SSM_HEADS].T
    if late:
        yc, *landed = _ssd_fwd(xbc_act, xdt, a_cum, a_cum_t, t, late['ssd_fwd'][0])
        late['ssd_fwd'][1](w, landed)
    else:
        yc = _ssd_fwd(xbc_act, xdt, a_cum, a_cum_t, t)
    y_ssm = _rows(_f_ssdpost, t, [(yc, 1024, 0), (xbc_act, 1024, 0), (proj, 1024, 3)], [d_skip, w['e_ssm_norm'], expand],
                  [1024], [], "ssd_post", bf16_outs=(0,))[0]
    mix_e = jnp.concatenate([y_ret, y_ssm], axis=1)
    x1 = _mm(mix_e, w['e_w_out'], "nn", "e_out", res=x)
    x2, ffn0 = _ffn_fwd(x1, ffn_g[0], w['ffn_w_up'][0], ffn_dw_w[0], ffn_dw_b[0], w['ffn_w_down'][0], "0")

    lr, li = w['o_a_re'][0], w['o_a_im'][0]
    ls = w['o_log_step'].reshape(S5_GROUPS, 1)
    b_re3, b_im3 = jnp.transpose(w['o_b_re'][0], (2, 0, 1)), jnp.transpose(w['o_b_im'][0], (2, 0, 1))
    par_ins = [lr, li, ls, b_re3, b_im3]
    whole = lambda a: (a, a.shape, (lambda i, n=a.ndim: (0,) * n))
    par_shapes = [(S5_GROUPS, S5_STATE)] * 2 + [(S5_GROUP, S5_GROUPS, S5_STATE)] * 2
    ab_re, ab_im, bb_re, bb_im = _call(_f_s5par, (1,), [whole(a) for a in par_ins],
                                       [(s, F32, s, (lambda i, n=len(s): (0,) * n), False) for s in par_shapes], "s5_params")
    w_b = jnp.concatenate([_block_diag(jnp.transpose(bb_re, (1, 0, 2))), _block_diag(jnp.transpose(bb_im, (1, 0, 2)))], axis=2)
    w_c = jnp.concatenate([_block_diag(jnp.transpose(w['o_c_re'][0], (0, 2, 1))),
                           -_block_diag(jnp.transpose(w['o_c_im'][0], (0, 2, 1)))], axis=1)
    a_re, a_im = ab_re.reshape(1, S5_COLS), ab_im.reshape(1, S5_COLS)
    dw_w = w['o_dw_w'][0][:, None, :]
    glu_w = w['o_glu_w'].astype(F32)

    h1 = _rms_fwd(x2, mix_g[1], "mix_norm_1")
    proj_o = _mm(h1, w['o_w_in'], "nn", "o_in")
    c1 = _conf_conv(proj_o, dw_w, w['o_dw_b'])
    c2 = _rows(_f_confb, t, [(c1, 512, 0)], [w['o_ln_g'], w['o_ln_b']], [512], [], "conf_norm", bf16_outs=(0,))[0]
    u_off = 2 * CONF_DIM // S5_NARROW
    bu_re, bu_im = _s5_expand(proj_o, u_off, w_b, False, "s5_bu")
    xs_re, xs_im, xs_re_bf, xs_im_bf = _s5_scan(bu_re, bu_im, a_re, a_im, False)
    y_s5 = _s5_contract(xs_re_bf, xs_im_bf, w_c, True, "s5_cx")
    s_out = _rows(_f_s5post, t, [(y_s5, 512, 0), (proj_o, 512, 2)], [w['o_d'], glu_w], [512], [], "s5_post", bf16_outs=(0,))[0]
    mix_o = jnp.concatenate([c2, s_out], axis=1)
    x3 = _mm(mix_o, w['o_w_out'], "nn", "o_out", res=x2)
    x4, ffn1 = _ffn_fwd(x3, ffn_g[1], w['ffn_w_up'][1], ffn_dw_w[1], ffn_dw_b[1], w['ffn_w_down'][1], "1")

    dx4, dx4_bf, loss_blk, d_final = _rows(_loss_step, t, [(x4, D_MODEL, 0), (tgt, D_MODEL, 0)], [w['final_norm'].reshape(1, D_MODEL)],
                                           [D_MODEL, D_MODEL], [(8, LANES), (1, D_MODEL)], "loss_head", bf16_outs=(1,))
    loss = loss_blk[0, 0]
    grads['final_norm'] = d_final.reshape(D_MODEL)

    dx3, dx3_bf, g1 = _ffn_bwd(x3, ffn_g[1], w['ffn_w_up'][1], ffn_dw_w[1], ffn_dw_b[1], w['ffn_w_down'][1], ffn1, dx4, dx4_bf, "1")
    grads['o_w_out'] = _mm(mix_o, dx3_bf, "tn", "o_out_dw", out_dtype=BF16)
    dmix_o = _mm(dx3_bf, w['o_w_out'], "nt", "o_out_dx")
    dc1, d_ln_g, d_ln_b = _conf_norm_bwd(c1, w['o_ln_g'], w['o_ln_b'], dmix_o, t)
    dca, dcg, d_dw_w, d_dw_b = _conf_conv(proj_o, dw_w, w['o_dw_b'], dc1)
    dyc, du_skip, d_od, d_glu = _s5_post_bwd(y_s5, proj_o, w['o_d'], glu_w, dmix_o, t)
    grads['o_glu_w'] = d_glu
    d_w_c = _s5_wgrad(dyc, 0, xs_re_bf, xs_im_bf, True, "s5_cx_dw")
    dxs_re, dxs_im = _s5_expand(dyc, 0, w_c, True, "s5_cx_dx")
    g_re, g_im, d_are, d_aim = _s5_scan(dxs_re, dxs_im, a_re, a_im, True, (xs_re, xs_im))
    d_w_b = _s5_wgrad(proj_o, u_off, g_re, g_im, False, "s5_bu_dw")
    du = _s5_contract(g_re, g_im, w_b, False, "s5_bu_dx", res=du_skip, out_dtype=BF16)
    d_bb_re = jnp.transpose(_diag_blocks(d_w_b[:, :, :S5_WIDE], S5_GROUP, S5_STATE), (1, 0, 2))
    d_bb_im = jnp.transpose(_diag_blocks(d_w_b[:, :, S5_WIDE:], S5_GROUP, S5_STATE), (1, 0, 2))
    par_cts = [d_are.reshape(S5_GROUPS, S5_STATE), d_aim.reshape(S5_GROUPS, S5_STATE), d_bb_re, d_bb_im]
    in_shapes = [a.shape for a in par_ins]
    d_lr, d_li, d_ls, d_br3, d_bi3 = _call(_grad_fn(_f_s5par, 5, 5), (1,), [whole(a) for a in par_ins + par_cts],
                                           [(s, F32, s, (lambda i, n=len(s): (0,) * n), False) for s in in_shapes], "s5_params_bwd")
    grads['o_a_re'], grads['o_a_im'], grads['o_log_step'] = d_lr[None], d_li[None], d_ls.reshape(1, S5_GROUPS)
    grads['o_b_re'], grads['o_b_im'] = jnp.transpose(d_br3, (1, 2, 0))[None], jnp.transpose(d_bi3, (1, 2, 0))[None]
    grads['o_c_re'] = jnp.transpose(_diag_blocks(d_w_c[:, :S5_WIDE], S5_STATE, S5_GROUP), (0, 2, 1))[None]
    grads['o_c_im'] = -jnp.transpose(_diag_blocks(d_w_c[:, S5_WIDE:], S5_STATE, S5_GROUP), (0, 2, 1))[None]
    grads['o_d'], grads['o_ln_g'], grads['o_ln_b'] = d_od, d_ln_g, d_ln_b
    grads['o_dw_w'], grads['o_dw_b'] = d_dw_w[:, 0][None], d_dw_b
    dproj_o = jnp.concatenate([dca, dcg, du], axis=1)
    grads['o_w_in'] = _mm(h1, dproj_o, "tn", "o_in_dw")
    dh1 = _mm(dproj_o, w['o_w_in'], "nt", "o_in_dx")
    dx2, dx2_bf, d_mix1 = _rms_bwd(x2, mix_g[1], dh1, dx3, "mix_norm_bwd_1")
    in_flight = late['grads_ready']({**grads, 'ffn_w_up': [None, g1['w_up']], 'ffn_w_down': [None, g1['w_down']]}) if late else ()

    dx1, dx1_bf, g0 = _ffn_bwd(x1, ffn_g[0], w['ffn_w_up'][0], ffn_dw_w[0], ffn_dw_b[0], w['ffn_w_down'][0], ffn0, dx2, dx2_bf, "0")
    in_flight_ffn = late['ffn_grads_ready']({'ffn_w_up': [g0['w_up'], None], 'ffn_w_down': [g0['w_down'], None]}) if late else ()
    grads['e_w_out'] = _mm(mix_e, dx1_bf, "tn", "e_out_dw", out_dtype=BF16)
    dmix_e = _mm(dx1_bf, w['e_w_out'], "nt", "e_out_dx")
    dr, dg = _rows(_grad_fn(_f_retpost, 2, 2), t, [(r, 1024, 0), (proj, 1024, 2), (dmix_e, 1024, 0)], [], [1024, 1024], [], "ret_post_bwd",
                   bf16_outs=(0, 1))
    dqr, dkr, dv, landed = _ret_bwd(qr, kr, proj, dr, t, exchange=in_flight_ffn)
    if late:
        late['ffn_grads_landed'](landed)
    dq, dk = _ret_pre_bwd(proj, cosf, sins, dqr, dkr, t)
    dyc0, dxs1, dz, d_dskip, d_ssm_norm = _ssd_post_bwd(yc, xbc_act, proj, d_skip, w['e_ssm_norm'], expand, dmix_e, t)
    dcm, da_q, dbm, dxdt, da_k, da_k_t, landed = _ssd_bwd(xbc_act, xdt, a_cum, a_cum_t, dyc0, expand, t, exchange=in_flight)
    if late:
        late['grads_landed'](landed)
    d_a_cum = da_q - da_k - jnp.pad(da_k_t.T, ((0, 0), (0, LANES - SSM_HEADS)))
    dda = _cumsum(d_a_cum, True, "ssd_cumsum_bwd")
    dxs, ddtr, d_dt_bias, d_a_log = _ssd_pre_bwd(xbc_act, proj, dt_bias, a_log, expand, dxdt, dda, dxs1, t)
    dxbc_act = jnp.concatenate([dxs, dbm, dcm], axis=1)
    dxbc, d_conv_w, d_conv_b = _cols(_grad_fn(_f_ssdconv, 3, 3), 12, [(proj, 32), (conv_w, 0), (conv_b, 0), (dxbc_act, 0)],
                                     [(t,), (4, 1), (1,)], "ssd_conv_bwd", bf16_outs=(0,))
    dproj = jnp.concatenate([dq, dk, dv, dg, dz, dxbc, ddtr], axis=1)
    grads['e_w_in'] = _mm(h0, dproj, "tn", "e_in_dw")
    dh0 = _mm(dproj, w_e, "nt", "e_in_dx")
    dx0, _, d_mix0 = _rms_bwd(x, mix_g[0], dh0, dx1, "mix_norm_bwd_0")

    grads['mix_norm'] = jnp.concatenate([d_mix0, d_mix1], axis=0)
    grads['e_conv_w'], grads['e_conv_b'] = d_conv_w[:, 0][None], d_conv_b
    grads['e_dt_bias'], grads['e_a_log'], grads['e_d'] = d_dt_bias[:, :SSM_HEADS], d_a_log[:, :SSM_HEADS], d_dskip[:, :SSM_HEADS]
    grads['e_ssm_norm'] = d_ssm_norm
    grads['ffn_norm'] = jnp.concatenate([g0['norm'], g1['norm']], axis=0)
    grads['ffn_w_up'], grads['ffn_w_down'] = [g0['w_up'], g1['w_up']], [g0['w_down'], g1['w_down']]
    grads['ffn_dw_w'] = jnp.stack([g0['dw_w'], g1['dw_w']], axis=0)
    grads['ffn_dw_b'] = jnp.concatenate([g0['dw_b'], g1['dw_b']], axis=0)
    return loss, dx0, grads


def _conf_norm_bwd(c1, ln_g, ln_b, dmix_o, t):
    def fn(c1_, dy_, g_, b_):
        _, vjp = jax.vjp(lambda a, b, c: _f_confb(a, b, c)[0], c1_, g_, b_)
        return vjp(dy_)
    return _rows(fn, t, [(c1, 512, 0), (dmix_o, 512, 0)], [ln_g, ln_b], [512], [(1, 512), (1, 512)], "conf_norm_bwd")


def _s5_post_bwd(y_s5, proj_o, d_skip, glu_w, dmix_o, t):
    def fn(yc_, u_, dy_, d_, gw_):
        _, vjp = jax.vjp(lambda a, b, c, e: _f_s5post(a, b, c, e)[0], yc_, u_, d_, gw_)
        return vjp(dy_)
    return _rows(fn, t, [(y_s5, 512, 0), (proj_o, 512, 2), (dmix_o, 512, 1)], [d_skip, glu_w], [512, 512], [(1, 512), (512, 512)],
                 "s5_post_bwd", bf16_outs=(0,))


def _ret_pre_bwd(proj, cosf, sins, dqr, dkr, t):
    def fn(q_, k_, cos_, sin_, dq_, dk_):
        _, vjp = jax.vjp(lambda a, b: _f_retpre(a, b, cos_, sin_), q_, k_)
        return vjp((dq_, dk_))
    return _rows(fn, t, [(proj, 512, 0), (proj, 512, 1), (cosf, LANES, 0), (sins, LANES, 0), (dqr, 512, 0), (dkr, 512, 0)], [],
                 [512, 512], [], "ret_pre_bwd", bf16_outs=(0, 1))


def _ssd_post_bwd(yc, xbc_act, proj, d_skip, norm_w, expand, dmix_e, t):
    def fn(yc_, xs_, z_, dy_, d_, nw_, e_):
        _, vjp = jax.vjp(lambda a, b, c, dd, n: _f_ssdpost(a, b, c, dd, n, e_)[0], yc_, xs_, z_, d_, nw_)
        return vjp(dy_)
    return _rows(fn, t, [(yc, 1024, 0), (xbc_act, 1024, 0), (proj, 1024, 3), (dmix_e, 1024, 1)], [d_skip, norm_w, expand],
                 [1024, 1024, 1024], [(1, LANES), (1, 1024)], "ssd_post_bwd", bf16_outs=(0, 2))


def _ssd_pre_bwd(xbc_act, proj, dt_bias, a_log, expand, dxdt, dda, dxs1, t):
    def fn(xs_, dtr_, dx_, dda_, dxs1_, bias_, alog_, e_):
        _, vjp = jax.vjp(lambda a, b, c, dd: _f_ssdpre(a, b, c, dd, e_), xs_, dtr_, bias_, alog_)
        dxs, ddtr, dbias, dalog = vjp((dx_, dda_))
        return dxs + dxs1_, ddtr, dbias, dalog
    return _rows(fn, t, [(xbc_act, 1024, 0), (proj, LANES, 44), (dxdt, 1024, 0), (dda, LANES, 0), (dxs1, 1024, 0)],
                 [dt_bias, a_log, expand], [1024, LANES], [(1, LANES), (1, LANES)], "ssd_pre_bwd", bf16_outs=(1,))


def kernel(x, mix_norm, e_w_in, e_conv_w, e_conv_b, e_dt_bias, e_a_log, e_d, e_ssm_norm, e_w_out, o_w_in, o_dw_w, o_dw_b, o_ln_g, o_ln_b, o_a_re, o_a_im, o_b_re, o_b_im, o_c_re, o_c_im, o_d, o_log_step, o_glu_w, o_w_out, ffn_norm, ffn_w_up, ffn_dw_w, ffn_dw_b, ffn_w_down, final_norm, loss_target, m_mix_norm, m_e_w_in, m_e_conv_w, m_e_conv_b, m_e_dt_bias, m_e_a_log, m_e_d, m_e_ssm_norm, m_e_w_out, m_o_w_in, m_o_dw_w, m_o_dw_b, m_o_ln_g, m_o_ln_b, m_o_a_re, m_o_a_im, m_o_b_re, m_o_b_im, m_o_c_re, m_o_c_im, m_o_d, m_o_log_step, m_o_glu_w, m_o_w_out, m_ffn_norm, m_ffn_w_up, m_ffn_dw_w, m_ffn_dw_b, m_ffn_w_down, m_final_norm, v_mix_norm, v_e_w_in, v_e_conv_w, v_e_conv_b, v_e_dt_bias, v_e_a_log, v_e_d, v_e_ssm_norm, v_e_w_out, v_o_w_in, v_o_dw_w, v_o_dw_b, v_o_ln_g, v_o_ln_b, v_o_a_re, v_o_a_im, v_o_b_re, v_o_b_im, v_o_c_re, v_o_c_im, v_o_d, v_o_log_step, v_o_glu_w, v_o_w_out, v_ffn_norm, v_ffn_w_up, v_ffn_dw_w, v_ffn_dw_b, v_ffn_w_down, v_final_norm):
    p = dict(locals())

    kinds = ("grad_", "delta_", "new_m_", "new_v_")

    def block(name, layer):
        return p[name][0 if layer is None else layer]

    def shards(blocks):
        return [block(n, layer).astype(BF16) for n, layer, _ in blocks]

    def place(w, gathered, blocks):
        for (n, layer, by_cols), g in zip(blocks, gathered):
            if by_cols:
                full = _join_cols(g, E_IN_PAD if n == 'e_w_in' else N_DEV * g.shape[2], f"join_{n}_{layer}")
            else:
                full = g.reshape(N_DEV * g.shape[1], g.shape[2])
            if layer is None:
                w[n] = full
            else:
                w[n][layer] = full

    first = [b for b in MATMUL_BLOCKS if b[0].startswith('e_')]
    with_ret = [b for b in MATMUL_BLOCKS if b[0].startswith('ffn_') and b[1] == 0]
    with_ssd = [b for b in MATMUL_BLOCKS if b not in first and b not in with_ret]
    gathered = _gather(shards(first) + [_pack([p[n] for n in SMALL_SHARDED], F32, 16)], "gather_first_weights")
    w = {n: p[n] for n in REPLICATED}
    w['ffn_w_up'], w['ffn_w_down'] = [None, None], [None, None]
    place(w, gathered, first)
    for n, piece in zip(SMALL_SHARDED, _unpack(gathered[-1], [p[n].shape for n in SMALL_SHARDED], lead=(N_DEV,))):
        w[n] = _join_shards(piece, SHARDED[n])
    late = {'ret_fwd': (shards(with_ret), functools.partial(place, blocks=with_ret)),
            'ssd_fwd': (shards(with_ssd), functools.partial(place, blocks=with_ssd))}

    def chip_sums(grads, blocks, extra, tag):
        sends = []
        for n, layer, by_cols in blocks:
            g = grads[n] if layer is None else grads[n][layer]
            if by_cols:
                sends.append(_split_cols(g, block(n, layer).shape[1], BF16, f"split_{n}_{layer}"))
            else:
                sends.append(g.astype(BF16).reshape(N_DEV, -1, g.shape[1]))
        core = lax.axis_index("c")
        by_core = [s.reshape((N_CHIPS, 2) + s.shape[1:]) for s in sends + extra]
        keep = [lax.dynamic_index_in_dim(s, core, axis=1, keepdims=False) for s in by_core]
        give = [lax.dynamic_index_in_dim(s, 1 - core, axis=1, keepdims=False) for s in by_core]
        got = _swap_sibling(give, "swap_sibling_grads_" + tag)
        return [_add(a, b, f"chip_sum_{tag}_{i}") for i, (a, b) in enumerate(zip(keep, got))]

    landed_late = []
    late['grads_ready'] = lambda grads: chip_sums(grads, with_ssd, [], "late")
    late['grads_landed'] = landed_late.extend
    landed_ffn = []
    late['ffn_grads_ready'] = lambda grads: chip_sums(grads, with_ret, [], "ffn")
    late['ffn_grads_landed'] = landed_ffn.extend

    loss, dx, grads = _local_step(x[0], loss_target[0], w, late)
    loss = lax.psum(loss, MESH_AXES)

    early = first
    small_send = _pack([_split_shards(grads[n].reshape(p[n].shape[:SHARDED[n]] + (-1,) + p[n].shape[SHARDED[n] + 1:]), SHARDED[n])
                        for n in SMALL_SHARDED], F32, 128, lead=(N_DEV,))
    landed_early = list(_exchange_chips(chip_sums(grads, early, [small_send], "early"), "exchange_chip_grads"))
    part_of = {(n, layer): part for (n, layer, _), part in zip(early + with_ret + with_ssd, landed_early[:-1] + landed_ffn + landed_late)}
    parts = [landed_early[-1], _gather([_pack([grads[n].reshape(p[n].shape) for n in REPLICATED], F32, 128)], "gather_replicated_grads")[0]]

    out, by_layer = {}, {}
    for n, layer, _ in MATMUL_BLOCKS:
        by_layer.setdefault(n, {})[layer] = _adamw(part_of[(n, layer)], *[block(pre + n, layer) for pre in ("", "m_", "v_")],
                                                   f"adamw_{n}_{layer}")
    for n, res in by_layer.items():
        for i, kind in enumerate(kinds):
            out[kind + n] = res[None][i][None] if None in res else jnp.stack([res[0][i], res[1][i]], axis=0)
    for names, part, tag in ((SMALL_SHARDED, parts[-2], "small"), (REPLICATED, parts[-1], "replicated")):
        packed = [_pack([p[pre + n] for n in names], F32, 128) for pre in ("", "m_", "v_")]
        for kind, buf in zip(kinds, _adamw(part, *packed, "adamw_" + tag)):
            for n, a in zip(names, _unpack(buf, [p[n].shape for n in names])):
                out[kind + n] = a
    return (loss, dx[None], *[out[kind + n] for kind in kinds for n in WEIGHTS])
```
